```python
import math
import jax
import jax.numpy as jnp
from jax import lax
import numpy as np

D_MODEL = 1024
BATCH = 8
SEQ = 4096
DEPTH = 4

CHUNK = 64
EPS = 1e-6
A_HEADS = 8
A_HEAD_DIM = 64
A_WIDTH = A_HEADS * A_HEAD_DIM
A_PAST_CHUNKS = 8
A_BAND = A_PAST_CHUNKS + 1
A_MAX_REL = 128
B_HEADS = 4
B_HEAD_DIM = 128
B_WIDTH = B_HEADS * B_HEAD_DIM
CONV_K = 4
FFN_HIDDEN = ((8 * D_MODEL + 3 * 256 - 1) // (3 * 256)) * 256
N_MOD = 6
IN_SIZES = (A_WIDTH, A_WIDTH, A_WIDTH, 3 * B_WIDTH, B_WIDTH, B_HEADS, B_HEADS, D_MODEL, D_MODEL)
IN_DIM = sum(IN_SIZES)
IN_SPLITS = tuple(int(s) for s in np.cumsum(IN_SIZES)[:-1])

kernel_name = "hybrid_chunk_attn_gated_deltanet_adaln"


def rmsnorm(x, g):
    xf = x.astype(jnp.float32)
    y = xf * lax.rsqrt(jnp.mean(xf * xf, axis=-1, keepdims=True) + EPS)
    return (y * g.astype(jnp.float32)).astype(x.dtype)


def l2norm(t):
    return t * lax.rsqrt(jnp.sum(t * t, axis=-1, keepdims=True) + EPS)


def chunk_band_attention(q, k, v, rel_table):
    bsz, seq, h, dh = q.shape
    nc = seq // CHUNK
    band = A_BAND * CHUNK
    qc = q.reshape(bsz, nc, CHUNK, h, dh)
    pad = ((0, 0), (A_PAST_CHUNKS, 0), (0, 0), (0, 0), (0, 0))
    kp = jnp.pad(k.reshape(bsz, nc, CHUNK, h, dh), pad)
    vp = jnp.pad(v.reshape(bsz, nc, CHUNK, h, dh), pad)
    band_idx = jnp.arange(nc)[:, None] + jnp.arange(A_BAND)[None, :]
    kb = kp[:, band_idx].reshape(bsz, nc, band, h, dh)
    vb = vp[:, band_idx].reshape(bsz, nc, band, h, dh)
    s = jnp.einsum('bnqhd,bnkhd->bnhqk', qc, kb).astype(jnp.float32) * (dh ** -0.5)
    qpos = A_PAST_CHUNKS * CHUNK + jnp.arange(CHUNK)
    rel = jnp.clip(qpos[:, None] - jnp.arange(band)[None, :], -A_MAX_REL, A_MAX_REL) + A_MAX_REL
    bias = rel_table.astype(jnp.float32)[:, rel]
    key_chunk = jnp.arange(nc)[:, None] - A_PAST_CHUNKS + (jnp.arange(band) // CHUNK)[None, :]
    valid = (key_chunk >= 0)[None, :, None, None, :]
    s = jnp.where(valid, s + bias[None, None], -1e30)
    p = jax.nn.softmax(s, axis=-1).astype(v.dtype)
    o = jnp.einsum('bnhqk,bnkhd->bnqhd', p, vb)
    return o.reshape(bsz, seq, h * dh)


def causal_depthwise_conv(x, w):
    ch = x.shape[-1]
    return lax.conv_general_dilated(
        x, w[:, None, :], window_strides=(1,), padding=[(CONV_K - 1, 0)],
        dimension_numbers=('NWC', 'WIO', 'NWC'), feature_group_count=ch)


def gated_delta_rule_chunked(q, k, v, g, beta):
    bsz, seq, h, dk = q.shape
    dv = v.shape[-1]
    nc = seq // CHUNK
    to_c = lambda t: t.reshape(bsz, nc, CHUNK, h, t.shape[-1]).transpose(0, 1, 3, 2, 4)
    q, k, v = to_c(q), to_c(k), to_c(v)
    g = g.reshape(bsz, nc, CHUNK, h).transpose(0, 1, 3, 2)
    beta = beta.reshape(bsz, nc, CHUNK, h).transpose(0, 1, 3, 2)
    g_cum = jnp.cumsum(g, axis=-1)
    incl = jnp.tril(jnp.ones((CHUNK, CHUNK), dtype=bool))
    strict = jnp.tril(jnp.ones((CHUNK, CHUNK), dtype=bool), k=-1)
    diff = g_cum[..., :, None] - g_cum[..., None, :]
    decay = jnp.exp(jnp.where(incl, diff, -jnp.inf))
    kk = jnp.einsum('bnhid,bnhjd->bnhij', k, k)
    lower = jnp.where(strict, beta[..., :, None] * kk * decay, 0.0)
    rhs = jnp.concatenate([v * beta[..., None], k * (beta * jnp.exp(g_cum))[..., None]], axis=-1)
    sol = lax.linalg.triangular_solve(lower, rhs, left_side=True, lower=True, unit_diagonal=True)
    u_intra, k_cumdecay = sol[..., :dv], sol[..., dv:]
    qk = jnp.einsum('bnhid,bnhjd->bnhij', q, k) * decay
    q_dec = q * jnp.exp(g_cum)[..., None]
    k_dec = k * jnp.exp(g_cum[..., -1:] - g_cum)[..., None]
    g_last = jnp.exp(g_cum[..., -1])

    def step(state, xs):
        u_i, w_i, qk_i, qd_i, kd_i, gl_i = xs
        u = u_i - jnp.einsum('bhck,bhkv->bhcv', w_i, state)
        o = jnp.einsum('bhck,bhkv->bhcv', qd_i, state) + jnp.einsum('bhij,bhjv->bhiv', qk_i, u)
        state = state * gl_i[..., None, None] + jnp.einsum('bhck,bhcv->bhkv', kd_i, u)
        return state, o

    xs = tuple(jnp.moveaxis(t, 1, 0) for t in (u_intra, k_cumdecay, qk, q_dec, k_dec, g_last))
    s0 = jnp.zeros((bsz, h, dk, dv), jnp.float32)
    _, o = lax.scan(step, s0, xs)
    return o.transpose(1, 0, 3, 2, 4).reshape(bsz, seq, h, dv)


def gated_deltanet(qkv_raw, z, b_raw, a_raw, w_conv, a_log, dt_bias, norm_g):
    bsz, seq, _ = qkv_raw.shape
    qkv = jax.nn.silu(causal_depthwise_conv(qkv_raw, w_conv)).astype(jnp.float32)
    q, k, v = jnp.split(qkv, 3, axis=-1)
    q = l2norm(q.reshape(bsz, seq, B_HEADS, B_HEAD_DIM)) * (B_HEAD_DIM ** -0.5)
    k = l2norm(k.reshape(bsz, seq, B_HEADS, B_HEAD_DIM))
    v = v.reshape(bsz, seq, B_HEADS, B_HEAD_DIM)
    beta = jax.nn.sigmoid(b_raw.astype(jnp.float32))
    g = -jnp.exp(a_log.astype(jnp.float32)) * jax.nn.softplus(
        a_raw.astype(jnp.float32) + dt_bias.astype(jnp.float32))
    o = gated_delta_rule_chunked(q, k, v, g, beta)
    zf = z.astype(jnp.float32).reshape(bsz, seq, B_HEADS, B_HEAD_DIM)
    o = (o * lax.rsqrt(jnp.mean(o * o, axis=-1, keepdims=True) + EPS)
         * norm_g.astype(jnp.float32) * jax.nn.silu(zf))
    return o.reshape(bsz, seq, B_WIDTH).astype(qkv_raw.dtype)


def _fwd_setup_inputs(seed: int = 0) -> dict:
    key = jax.random.key(seed)
    ks = jax.random.split(key, 18)
    f32 = jnp.float32

    def nrm(k, shape, scale):
        return jax.random.normal(k, shape, f32) * scale

    x = nrm(ks[0], (BATCH, SEQ, D_MODEL), 1.0)
    c = nrm(ks[1], (BATCH, D_MODEL), 1.0)
    w_ada = nrm(ks[2], (DEPTH, D_MODEL, N_MOD * D_MODEL), 0.5 * D_MODEL ** -0.5)
    b_ada = nrm(ks[3], (DEPTH, N_MOD * D_MODEL), 0.02)
    norm1_g = 1.0 + nrm(ks[4], (DEPTH, D_MODEL), 0.02)
    norm2_g = 1.0 + nrm(ks[5], (DEPTH, D_MODEL), 0.02)
    w_in = nrm(ks[6], (DEPTH, D_MODEL, IN_DIM), D_MODEL ** -0.5)
    rel_table = nrm(ks[7], (DEPTH, A_HEADS, 2 * A_MAX_REL + 1), 0.2)
    w_conv = nrm(ks[8], (DEPTH, CONV_K, 3 * B_WIDTH), CONV_K ** -0.5)
    a_log = jnp.log(jax.random.uniform(ks[9], (DEPTH, B_HEADS), f32, 1.0, 16.0))
    dt = jnp.exp(jax.random.uniform(ks[10], (DEPTH, B_HEADS), f32, math.log(1e-3), math.log(1e-1)))
    dt_bias = dt + jnp.log(-jnp.expm1(-dt))
    gdn_norm_g = 1.0 + nrm(ks[11], (DEPTH, B_HEAD_DIM), 0.02)
    w_branch_a = nrm(ks[12], (DEPTH, A_WIDTH, D_MODEL), A_WIDTH ** -0.5)
    w_branch_b = nrm(ks[13], (DEPTH, B_WIDTH, D_MODEL), B_WIDTH ** -0.5)
    w_out = nrm(ks[14], (DEPTH, D_MODEL, D_MODEL), D_MODEL ** -0.5)
    w_ffn_in = nrm(ks[15], (DEPTH, D_MODEL, 2 * FFN_HIDDEN), D_MODEL ** -0.5)
    w_ffn_out = nrm(ks[16], (DEPTH, FFN_HIDDEN, D_MODEL), FFN_HIDDEN ** -0.5)
    final_g = 1.0 + nrm(ks[17], (D_MODEL,), 0.02)
    return {"x": x, "c": c, "w_ada": w_ada, "b_ada": b_ada, "norm1_g": norm1_g,
            "norm2_g": norm2_g, "w_in": w_in, "rel_table": rel_table, "w_conv": w_conv,
            "a_log": a_log, "dt_bias": dt_bias, "gdn_norm_g": gdn_norm_g,
            "w_branch_a": w_branch_a, "w_branch_b": w_branch_b, "w_out": w_out,
            "w_ffn_in": w_ffn_in, "w_ffn_out": w_ffn_out, "final_g": final_g}


def _fwd_reference(x, c, w_ada, b_ada, norm1_g, norm2_g, w_in, rel_table, w_conv, a_log, dt_bias,
              gdn_norm_g, w_branch_a, w_branch_b, w_out, w_ffn_in, w_ffn_out, final_g):
    bsz, seq, _ = x.shape
    cond = jax.nn.silu(c)
    for l in range(DEPTH):
        mod = (cond @ w_ada[l] + b_ada[l])[:, None, :]
        sh1, sc1, gt1, sh2, sc2, gt2 = jnp.split(mod, N_MOD, axis=-1)
        h = rmsnorm(x, norm1_g[l]) * (1.0 + sc1) + sh1
        proj = h @ w_in[l]
        qa, ka, va, qkvb, zb, bb, ab, ga, gb = jnp.split(proj, IN_SPLITS, axis=-1)
        ya = chunk_band_attention(
            qa.reshape(bsz, seq, A_HEADS, A_HEAD_DIM),
            ka.reshape(bsz, seq, A_HEADS, A_HEAD_DIM),
            va.reshape(bsz, seq, A_HEADS, A_HEAD_DIM), rel_table[l])
        yb = gated_deltanet(qkvb, zb, bb, ab, w_conv[l], a_log[l], dt_bias[l], gdn_norm_g[l])
        merged = (jax.nn.sigmoid(ga) * (ya @ w_branch_a[l])
                  + jax.nn.sigmoid(gb) * (yb @ w_branch_b[l]))
        x = x + gt1 * (merged @ w_out[l])
        h = rmsnorm(x, norm2_g[l]) * (1.0 + sc2) + sh2
        gate, up = jnp.split(h @ w_ffn_in[l], 2, axis=-1)
        x = x + gt2 * ((jax.nn.silu(gate) * up) @ w_ffn_out[l])
    return rmsnorm(x, final_g)


import jax as _jax
import jax.numpy as _jnp

TWIN_FORMAT = 'train_step'
FWD_PARAMS = ['x', 'c', 'w_ada', 'b_ada', 'norm1_g', 'norm2_g', 'w_in', 'rel_table', 'w_conv', 'a_log', 'dt_bias', 'gdn_norm_g', 'w_branch_a', 'w_branch_b', 'w_out', 'w_ffn_in', 'w_ffn_out', 'final_g']
TWIN_WEIGHTS = ['w_ada', 'b_ada', 'norm1_g', 'norm2_g', 'w_in', 'rel_table', 'w_conv', 'a_log', 'dt_bias', 'gdn_norm_g', 'w_branch_a', 'w_branch_b', 'w_out', 'w_ffn_in', 'w_ffn_out', 'final_g']
TWIN_DIFF_INPUT = 'x'
TWIN_INPUTS = ['x', 'c', 'w_ada', 'b_ada', 'norm1_g', 'norm2_g', 'w_in', 'rel_table', 'w_conv', 'a_log', 'dt_bias', 'gdn_norm_g', 'w_branch_a', 'w_branch_b', 'w_out', 'w_ffn_in', 'w_ffn_out', 'final_g', 'loss_target', 'm_w_ada', 'm_b_ada', 'm_norm1_g', 'm_norm2_g', 'm_w_in', 'm_rel_table', 'm_w_conv', 'm_a_log', 'm_dt_bias', 'm_gdn_norm_g', 'm_w_branch_a', 'm_w_branch_b', 'm_w_out', 'm_w_ffn_in', 'm_w_ffn_out', 'm_final_g', 'v_w_ada', 'v_b_ada', 'v_norm1_g', 'v_norm2_g', 'v_w_in', 'v_rel_table', 'v_w_conv', 'v_a_log', 'v_dt_bias', 'v_gdn_norm_g', 'v_w_branch_a', 'v_w_branch_b', 'v_w_out', 'v_w_ffn_in', 'v_w_ffn_out', 'v_final_g']
TWIN_OUTPUTS = ['loss', 'grad_x', 'grad_w_ada', 'grad_b_ada', 'grad_norm1_g', 'grad_norm2_g', 'grad_w_in', 'grad_rel_table', 'grad_w_conv', 'grad_a_log', 'grad_dt_bias', 'grad_gdn_norm_g', 'grad_w_branch_a', 'grad_w_branch_b', 'grad_w_out', 'grad_w_ffn_in', 'grad_w_ffn_out', 'grad_final_g', 'delta_w_ada', 'delta_b_ada', 'delta_norm1_g', 'delta_norm2_g', 'delta_w_in', 'delta_rel_table', 'delta_w_conv', 'delta_a_log', 'delta_dt_bias', 'delta_gdn_norm_g', 'delta_w_branch_a', 'delta_w_branch_b', 'delta_w_out', 'delta_w_ffn_in', 'delta_w_ffn_out', 'delta_final_g', 'new_m_w_ada', 'new_m_b_ada', 'new_m_norm1_g', 'new_m_norm2_g', 'new_m_w_in', 'new_m_rel_table', 'new_m_w_conv', 'new_m_a_log', 'new_m_dt_bias', 'new_m_gdn_norm_g', 'new_m_w_branch_a', 'new_m_w_branch_b', 'new_m_w_out', 'new_m_w_ffn_in', 'new_m_w_ffn_out', 'new_m_final_g', 'new_v_w_ada', 'new_v_b_ada', 'new_v_norm1_g', 'new_v_norm2_g', 'new_v_w_in', 'new_v_rel_table', 'new_v_w_conv', 'new_v_a_log', 'new_v_dt_bias', 'new_v_gdn_norm_g', 'new_v_w_branch_a', 'new_v_w_branch_b', 'new_v_w_out', 'new_v_w_ffn_in', 'new_v_w_ffn_out', 'new_v_final_g']
TWIN_LEAF_KINDS = {'loss': 'loss', 'grad_x': 'grad_x', 'grad_w_ada': 'grad_w', 'grad_b_ada': 'grad_w', 'grad_norm1_g': 'grad_w', 'grad_norm2_g': 'grad_w', 'grad_w_in': 'grad_w', 'grad_rel_table': 'grad_w', 'grad_w_conv': 'grad_w', 'grad_a_log': 'grad_w', 'grad_dt_bias': 'grad_w', 'grad_gdn_norm_g': 'grad_w', 'grad_w_branch_a': 'grad_w', 'grad_w_branch_b': 'grad_w', 'grad_w_out': 'grad_w', 'grad_w_ffn_in': 'grad_w', 'grad_w_ffn_out': 'grad_w', 'grad_final_g': 'grad_w', 'delta_w_ada': 'delta_w', 'delta_b_ada': 'delta_w', 'delta_norm1_g': 'delta_w', 'delta_norm2_g': 'delta_w', 'delta_w_in': 'delta_w', 'delta_rel_table': 'delta_w', 'delta_w_conv': 'delta_w', 'delta_a_log': 'delta_w', 'delta_dt_bias': 'delta_w', 'delta_gdn_norm_g': 'delta_w', 'delta_w_branch_a': 'delta_w', 'delta_w_branch_b': 'delta_w', 'delta_w_out': 'delta_w', 'delta_w_ffn_in': 'delta_w', 'delta_w_ffn_out': 'delta_w', 'delta_final_g': 'delta_w', 'new_m_w_ada': 'new_m', 'new_m_b_ada': 'new_m', 'new_m_norm1_g': 'new_m', 'new_m_norm2_g': 'new_m', 'new_m_w_in': 'new_m', 'new_m_rel_table': 'new_m', 'new_m_w_conv': 'new_m', 'new_m_a_log': 'new_m', 'new_m_dt_bias': 'new_m', 'new_m_gdn_norm_g': 'new_m', 'new_m_w_branch_a': 'new_m', 'new_m_w_branch_b': 'new_m', 'new_m_w_out': 'new_m', 'new_m_w_ffn_in': 'new_m', 'new_m_w_ffn_out': 'new_m', 'new_m_final_g': 'new_m', 'new_v_w_ada': 'new_v', 'new_v_b_ada': 'new_v', 'new_v_norm1_g': 'new_v', 'new_v_norm2_g': 'new_v', 'new_v_w_in': 'new_v', 'new_v_rel_table': 'new_v', 'new_v_w_conv': 'new_v', 'new_v_a_log': 'new_v', 'new_v_dt_bias': 'new_v', 'new_v_gdn_norm_g': 'new_v', 'new_v_w_branch_a': 'new_v', 'new_v_w_branch_b': 'new_v', 'new_v_w_out': 'new_v', 'new_v_w_ffn_in': 'new_v', 'new_v_w_ffn_out': 'new_v', 'new_v_final_g': 'new_v'}


def _forward(args):
    return _fwd_reference(*[args[k] for k in FWD_PARAMS])


def _output_shape():
    def fwd():
        inp = _fwd_setup_inputs(0)
        return _fwd_reference(*[inp[k] for k in FWD_PARAMS])
    out = _jax.eval_shape(fwd)
    return out.shape, out.dtype

N_MICROBATCH = 1
ADAM_LR = 0.001
ADAM_B1 = 0.9
ADAM_B2 = 0.999
ADAM_EPS = 1e-08
ADAM_WD = 0.01
ADAM_STEP = 10
PER_EXAMPLE_BATCH_AXIS = {'x': 0, 'c': 0, 'loss_target': 0}
SHARED_INPUTS = []
_WEIGHT_DTYPES = {'w_ada': _jnp.float32, 'b_ada': _jnp.float32, 'norm1_g': _jnp.float32, 'norm2_g': _jnp.float32, 'w_in': _jnp.float32, 'rel_table': _jnp.float32, 'w_conv': _jnp.float32, 'a_log': _jnp.float32, 'dt_bias': _jnp.float32, 'gdn_norm_g': _jnp.float32, 'w_branch_a': _jnp.float32, 'w_branch_b': _jnp.float32, 'w_out': _jnp.float32, 'w_ffn_in': _jnp.float32, 'w_ffn_out': _jnp.float32, 'final_g': _jnp.float32}
MOMENT_SCALE = {'w_ada': 4.277311e-02, 'b_ada': 7.325292e-02, 'norm1_g': 3.293557e-02, 'norm2_g': 5.223473e-02, 'w_in': 1.527389e-02, 'rel_table': 3.307977e-03, 'w_conv': 2.072166e-02, 'a_log': 1.234059e-01, 'dt_bias': 1.188288e-01, 'gdn_norm_g': 5.609870e-02, 'w_branch_a': 1.077821e-02, 'w_branch_b': 1.913607e-02, 'w_out': 2.180331e-02, 'w_ffn_in': 2.310355e-02, 'w_ffn_out': 3.767916e-02, 'final_g': 3.204077e+01}


def _to_microbatches(a, axis):
    t = _jnp.moveaxis(a, axis, 0)
    t = t.reshape((N_MICROBATCH, t.shape[0] // N_MICROBATCH) + t.shape[1:])
    return _jnp.moveaxis(t, 1, axis + 1)


def setup_inputs(seed: int = 0) -> dict:
    inp = _fwd_setup_inputs(seed)
    key = _jax.random.fold_in(_jax.random.key(seed), 7919)
    shape, _ = _output_shape()
    out = dict(inp)
    out["loss_target"] = _jax.random.normal(_jax.random.fold_in(key, 0), shape, _jnp.float32)
    for i, name in enumerate(TWIN_WEIGHTS):
        w = inp[name].astype(_jnp.float32)
        if MOMENT_SCALE is None:
            s = _jnp.sqrt(_jnp.mean(_jnp.square(w)) + 1e-30)
        else:
            s = MOMENT_SCALE[name]
        km, kv = _jax.random.split(_jax.random.fold_in(key, i + 1))
        out[name] = w
        out["m_" + name] = s * _jax.random.normal(km, w.shape, _jnp.float32)
        out["v_" + name] = (s * s) * _jax.random.uniform(kv, w.shape, _jnp.float32, 0.5, 1.5)
    if N_MICROBATCH > 1:
        for name, axis in PER_EXAMPLE_BATCH_AXIS.items():
            out[name] = _to_microbatches(out[name], axis)
    return {'x': out['x'], 'c': out['c'], 'w_ada': out['w_ada'], 'b_ada': out['b_ada'], 'norm1_g': out['norm1_g'], 'norm2_g': out['norm2_g'], 'w_in': out['w_in'], 'rel_table': out['rel_table'], 'w_conv': out['w_conv'], 'a_log': out['a_log'], 'dt_bias': out['dt_bias'], 'gdn_norm_g': out['gdn_norm_g'], 'w_branch_a': out['w_branch_a'], 'w_branch_b': out['w_branch_b'], 'w_out': out['w_out'], 'w_ffn_in': out['w_ffn_in'], 'w_ffn_out': out['w_ffn_out'], 'final_g': out['final_g'], 'loss_target': out['loss_target'], 'm_w_ada': out['m_w_ada'], 'm_b_ada': out['m_b_ada'], 'm_norm1_g': out['m_norm1_g'], 'm_norm2_g': out['m_norm2_g'], 'm_w_in': out['m_w_in'], 'm_rel_table': out['m_rel_table'], 'm_w_conv': out['m_w_conv'], 'm_a_log': out['m_a_log'], 'm_dt_bias': out['m_dt_bias'], 'm_gdn_norm_g': out['m_gdn_norm_g'], 'm_w_branch_a': out['m_w_branch_a'], 'm_w_branch_b': out['m_w_branch_b'], 'm_w_out': out['m_w_out'], 'm_w_ffn_in': out['m_w_ffn_in'], 'm_w_ffn_out': out['m_w_ffn_out'], 'm_final_g': out['m_final_g'], 'v_w_ada': out['v_w_ada'], 'v_b_ada': out['v_b_ada'], 'v_norm1_g': out['v_norm1_g'], 'v_norm2_g': out['v_norm2_g'], 'v_w_in': out['v_w_in'], 'v_rel_table': out['v_rel_table'], 'v_w_conv': out['v_w_conv'], 'v_a_log': out['v_a_log'], 'v_dt_bias': out['v_dt_bias'], 'v_gdn_norm_g': out['v_gdn_norm_g'], 'v_w_branch_a': out['v_w_branch_a'], 'v_w_branch_b': out['v_w_branch_b'], 'v_w_out': out['v_w_out'], 'v_w_ffn_in': out['v_w_ffn_in'], 'v_w_ffn_out': out['v_w_ffn_out'], 'v_final_g': out['v_final_g']}


def _loss(weights, diff, rest, loss_target):
    with _jax.named_scope("forward"):
        args = {**rest, TWIN_DIFF_INPUT: diff, **{k: w.astype(_WEIGHT_DTYPES[k]) for k, w in weights.items()}}
        y = _forward(args)
    with _jax.named_scope("loss_head"):
        err = _jnp.square(y.astype(_jnp.float32) - loss_target)
        return 0.5 * _jnp.sum(_jnp.mean(err, axis=-1)) if err.ndim else 0.5 * err


def _adamw(w, g, m, v):
    m = ADAM_B1 * m + (1.0 - ADAM_B1) * g
    v = ADAM_B2 * v + (1.0 - ADAM_B2) * _jnp.square(g)
    m_hat = m / (1.0 - ADAM_B1 ** ADAM_STEP)
    v_hat = v / (1.0 - ADAM_B2 ** ADAM_STEP)
    delta = -ADAM_LR * (m_hat / (_jnp.sqrt(v_hat) + ADAM_EPS) + ADAM_WD * w)
    return delta, m, v


def reference(x, c, w_ada, b_ada, norm1_g, norm2_g, w_in, rel_table, w_conv, a_log, dt_bias, gdn_norm_g, w_branch_a, w_branch_b, w_out, w_ffn_in, w_ffn_out, final_g, loss_target, m_w_ada, m_b_ada, m_norm1_g, m_norm2_g, m_w_in, m_rel_table, m_w_conv, m_a_log, m_dt_bias, m_gdn_norm_g, m_w_branch_a, m_w_branch_b, m_w_out, m_w_ffn_in, m_w_ffn_out, m_final_g, v_w_ada, v_b_ada, v_norm1_g, v_norm2_g, v_w_in, v_rel_table, v_w_conv, v_a_log, v_dt_bias, v_gdn_norm_g, v_w_branch_a, v_w_branch_b, v_w_out, v_w_ffn_in, v_w_ffn_out, v_final_g):
    given = dict(x=x, c=c, w_ada=w_ada, b_ada=b_ada, norm1_g=norm1_g, norm2_g=norm2_g, w_in=w_in, rel_table=rel_table, w_conv=w_conv, a_log=a_log, dt_bias=dt_bias, gdn_norm_g=gdn_norm_g, w_branch_a=w_branch_a, w_branch_b=w_branch_b, w_out=w_out, w_ffn_in=w_ffn_in, w_ffn_out=w_ffn_out, final_g=final_g, loss_target=loss_target, m_w_ada=m_w_ada, m_b_ada=m_b_ada, m_norm1_g=m_norm1_g, m_norm2_g=m_norm2_g, m_w_in=m_w_in, m_rel_table=m_rel_table, m_w_conv=m_w_conv, m_a_log=m_a_log, m_dt_bias=m_dt_bias, m_gdn_norm_g=m_gdn_norm_g, m_w_branch_a=m_w_branch_a, m_w_branch_b=m_w_branch_b, m_w_out=m_w_out, m_w_ffn_in=m_w_ffn_in, m_w_ffn_out=m_w_ffn_out, m_final_g=m_final_g, v_w_ada=v_w_ada, v_b_ada=v_b_ada, v_norm1_g=v_norm1_g, v_norm2_g=v_norm2_g, v_w_in=v_w_in, v_rel_table=v_rel_table, v_w_conv=v_w_conv, v_a_log=v_a_log, v_dt_bias=v_dt_bias, v_gdn_norm_g=v_gdn_norm_g, v_w_branch_a=v_w_branch_a, v_w_branch_b=v_w_branch_b, v_w_out=v_w_out, v_w_ffn_in=v_w_ffn_in, v_w_ffn_out=v_w_ffn_out, v_final_g=v_final_g)
    weights = {n: given[n] for n in TWIN_WEIGHTS}
    shared = {n: given[n] for n in SHARED_INPUTS}
    per_example = {n: given[n] for n in ['x', 'c']}
    grad_fn = _jax.value_and_grad(_loss, argnums=(0, 1))

    def one_microbatch(ex, loss_target):
        ex = dict(ex)
        diff = ex.pop(TWIN_DIFF_INPUT)
        return grad_fn(weights, diff, {**shared, **ex}, loss_target)

    if N_MICROBATCH == 1:
        loss, (grad_w, grad_x) = one_microbatch(per_example, given["loss_target"])
    else:
        def body(carry, xs):
            loss_sum, grad_sum = carry
            l_k, (gw_k, gx_k) = one_microbatch(xs[0], xs[1])
            with _jax.named_scope("update"):
                return (loss_sum + l_k, _jax.tree.map(_jnp.add, grad_sum, gw_k)), gx_k

        init = (_jnp.zeros((), _jnp.float32), _jax.tree.map(_jnp.zeros_like, weights))
        (loss, grad_w), grad_x = _jax.lax.scan(body, init, (per_example, given["loss_target"]))
    with _jax.named_scope("update"):
        delta_w, new_m, new_v = {}, {}, {}
        for n in TWIN_WEIGHTS:
            delta_w[n], new_m[n], new_v[n] = _adamw(weights[n], grad_w[n], given["m_" + n], given["v_" + n])
    return (loss, grad_x, *[grad_w[n] for n in TWIN_WEIGHTS], *[delta_w[n] for n in TWIN_WEIGHTS],
            *[new_m[n] for n in TWIN_WEIGHTS], *[new_v[n] for n in TWIN_WEIGHTS])
```

```python
import functools
import math

import jax
import jax.numpy as jnp
from jax import lax
from jax.experimental import pallas as pl
from jax.experimental.pallas import tpu as pltpu

F32 = jnp.float32
BF16 = jnp.bfloat16
HI = lax.Precision.HIGHEST

N_DEV = 8
D = 1024
DEPTH = 4
CH = 64
EPS = 1e-6
A_HEADS, A_DH = 8, 64
A_W = A_HEADS * A_DH
A_PAST = 8
A_MAX_REL = 128
QB = 256
KB = QB + A_PAST * CH
B_HEADS, B_DH = 4, 128
B_W = B_HEADS * B_DH
CONV_K = 4
FF = 2816
IN_DIM = 5640
IN_PAD = 5760
LANE = 128
NEG = -1e30
VMEM_LIMIT = 48 * 1024 * 1024

ADAM_LR, ADAM_B1, ADAM_B2, ADAM_EPS, ADAM_WD, ADAM_STEP = 0.001, 0.9, 0.999, 1e-08, 0.01, 10

OFF_GA, OFF_GB, OFF_QA, OFF_KA, OFF_VA, OFF_QB, OFF_KB, OFF_VB, OFF_ZB, OFF_BA = (
    0, 1024, 2048, 2560, 3072, 3584, 4096, 4608, 5120, 5632)


def _cp(sem=None):
    return pltpu.CompilerParams(dimension_semantics=sem, vmem_limit_bytes=VMEM_LIMIT)


def _tile(n, pref):
    if n <= pref:
        return n
    best = None
    for t in range(LANE, pref + 1, LANE):
        if n % t == 0:
            best = t
    assert best is not None, (n, pref)
    return best


def _sigmoid(x):
    return 1.0 / (1.0 + jnp.exp(-x))


def _silu(x):
    return x * _sigmoid(x)


def _dsilu(x):
    s = _sigmoid(x)
    return s * (1.0 + x * (1.0 - s))


def _dot(a, b, prec=None):
    return jnp.dot(a, b, preferred_element_type=F32, precision=prec)


def _dot_nt(a, b, prec=None):
    return lax.dot_general(a, b, (((1,), (1,)), ((), ())), preferred_element_type=F32, precision=prec)


def _dot_tn(a, b, prec=None):
    return lax.dot_general(a, b, (((0,), (0,)), ((), ())), preferred_element_type=F32, precision=prec)


def _mm(a, b, *, ta=False, tb=False, out_dtype=F32, name, tm=512, tn=512, tk=2048):
    m, k = (a.shape[1], a.shape[0]) if ta else a.shape
    n = b.shape[0] if tb else b.shape[1]
    assert k == (b.shape[1] if tb else b.shape[0]), (a.shape, b.shape, ta, tb)
    tm, tn, tk = _tile(m, tm), _tile(n, tn), _tile(k, tk)
    nk = k // tk
    dn = (((0 if ta else 1,), (1 if tb else 0,)), ((), ()))

    def body(a_ref, b_ref, o_ref, acc_ref):
        kk = pl.program_id(2)

        @pl.when(kk == 0)
        def _():
            acc_ref[...] = jnp.zeros_like(acc_ref)

        acc_ref[...] += lax.dot_general(a_ref[...].astype(BF16), b_ref[...].astype(BF16), dn,
                                        preferred_element_type=F32)

        @pl.when(kk == nk - 1)
        def _():
            o_ref[...] = acc_ref[...].astype(out_dtype)

    a_spec = pl.BlockSpec((tk, tm), lambda i, j, q: (q, i)) if ta else pl.BlockSpec((tm, tk), lambda i, j, q: (i, q))
    b_spec = pl.BlockSpec((tn, tk), lambda i, j, q: (j, q)) if tb else pl.BlockSpec((tk, tn), lambda i, j, q: (q, j))
    return pl.pallas_call(
        body, grid=(m // tm, n // tn, nk), in_specs=[a_spec, b_spec],
        out_specs=pl.BlockSpec((tm, tn), lambda i, j, q: (i, j)),
        out_shape=jax.ShapeDtypeStruct((m, n), out_dtype),
        scratch_shapes=[pltpu.VMEM((tm, tn), F32)],
        compiler_params=_cp(("parallel", "parallel", "arbitrary")), name=name)(a, b)


def _rb(tr, width, cb=0):
    return pl.BlockSpec((tr, width), lambda i: (i, cb))


def _whole(shape):
    nd = len(shape)
    return pl.BlockSpec(shape, lambda i: (0,) * nd)


def _colsum(v):
    return jnp.sum(v, axis=0, keepdims=True)


def _adaln_fwd(x, g, sc, sh, t=None, gt=None, *, name, tr=256):
    tt = x.shape[0]
    res = t is not None

    def body(*refs):
        if res:
            x_ref, t_ref, gt_ref, g_ref, sc_ref, sh_ref, xo_ref, h_ref = refs
            xv = x_ref[...] + gt_ref[...] * t_ref[...]
            xo_ref[...] = xv
        else:
            x_ref, g_ref, sc_ref, sh_ref, h_ref = refs
            xv = x_ref[...]
        r = lax.rsqrt(jnp.mean(xv * xv, axis=-1, keepdims=True) + EPS)
        h_ref[...] = ((xv * r * g_ref[...]) * (1.0 + sc_ref[...]) + sh_ref[...]).astype(BF16)

    row, vec = _rb(tr, D), _whole((1, D))
    if res:
        ins, in_specs = (x, t, gt, g, sc, sh), [row, row, vec, vec, vec, vec]
        out_shape = (jax.ShapeDtypeStruct((tt, D), F32), jax.ShapeDtypeStruct((tt, D), BF16))
        out_specs = (row, row)
    else:
        ins, in_specs = (x, g, sc, sh), [row, vec, vec, vec]
        out_shape, out_specs = jax.ShapeDtypeStruct((tt, D), BF16), row
    out = pl.pallas_call(body, grid=(tt // tr,), in_specs=in_specs, out_specs=out_specs, out_shape=out_shape,
                         compiler_params=_cp(("parallel",)), name=name)(*ins)
    return out if res else (x, out)


def _adaln_bwd(x, g, sc, sh, dh, dx_in, t=None, gt=None, *, name, tr=256):
    tt = x.shape[0]
    res = t is not None

    def body(*refs):
        if res:
            x_ref, g_ref, sc_ref, sh_ref, dh_ref, dxi_ref, t_ref, gt_ref, dx_ref, dt_ref, st_ref = refs
        else:
            x_ref, g_ref, sc_ref, sh_ref, dh_ref, dxi_ref, dx_ref, st_ref = refs

        @pl.when(pl.program_id(0) == 0)
        def _():
            st_ref[...] = jnp.zeros_like(st_ref)

        xv, dh = x_ref[...], dh_ref[...]
        r = lax.rsqrt(jnp.mean(xv * xv, axis=-1, keepdims=True) + EPS)
        nrm = xv * r
        y = nrm * g_ref[...]
        dy = dh * (1.0 + sc_ref[...])
        dn = dy * g_ref[...]
        dx = dxi_ref[...] + r * (dn - nrm * jnp.mean(dn * nrm, axis=-1, keepdims=True))
        dx_ref[...] = dx
        st_ref[0:1, :] += _colsum(dy * nrm)
        st_ref[1:2, :] += _colsum(dh * y)
        st_ref[2:3, :] += _colsum(dh)
        if res:
            dt_ref[...] = (gt_ref[...] * dx).astype(BF16)
            st_ref[3:4, :] += _colsum(dx * t_ref[...])

    row, vec, st = _rb(tr, D), _whole((1, D)), _whole((8, D))
    ins, in_specs = [x, g, sc, sh, dh, dx_in], [row, vec, vec, vec, row, row]
    out_shape, out_specs = [jax.ShapeDtypeStruct((tt, D), F32)], [row]
    if res:
        ins += [t, gt]
        in_specs += [row, vec]
        out_shape.append(jax.ShapeDtypeStruct((tt, D), BF16))
        out_specs.append(row)
    out_shape.append(jax.ShapeDtypeStruct((8, D), F32))
    out_specs.append(st)
    return pl.pallas_call(body, grid=(tt // tr,), in_specs=in_specs, out_specs=tuple(out_specs),
                          out_shape=tuple(out_shape), compiler_params=_cp(("arbitrary",)), name=name)(*ins)


def _loss_head(x, t, gt, fg, tgt, *, name, tr=256):
    tt = x.shape[0]

    def body(x_ref, t_ref, gt_ref, fg_ref, tgt_ref, dx_ref, dt_ref, st_ref):
        @pl.when(pl.program_id(0) == 0)
        def _():
            st_ref[...] = jnp.zeros_like(st_ref)

        tv = t_ref[...]
        xv = x_ref[...] + gt_ref[...] * tv
        r = lax.rsqrt(jnp.mean(xv * xv, axis=-1, keepdims=True) + EPS)
        nrm = xv * r
        err = nrm * fg_ref[...] - tgt_ref[...]
        st_ref[4:5, :] += 0.5 * jnp.sum(jnp.mean(err * err, axis=-1, keepdims=True), axis=0, keepdims=True)
        dy = err * (1.0 / D)
        dn = dy * fg_ref[...]
        dx = r * (dn - nrm * jnp.mean(dn * nrm, axis=-1, keepdims=True))
        dx_ref[...] = dx
        dt_ref[...] = (gt_ref[...] * dx).astype(BF16)
        st_ref[0:1, :] += _colsum(dy * nrm)
        st_ref[3:4, :] += _colsum(dx * tv)

    row, vec = _rb(tr, D), _whole((1, D))
    return pl.pallas_call(
        body, grid=(tt // tr,), in_specs=[row, row, vec, vec, row], out_specs=(row, row, _whole((8, D))),
        out_shape=(jax.ShapeDtypeStruct((tt, D), F32), jax.ShapeDtypeStruct((tt, D), BF16),
                   jax.ShapeDtypeStruct((8, D), F32)),
        compiler_params=_cp(("arbitrary",)), name=name)(x, t, gt, fg, tgt)


def _merge_fwd(proj, pa, pb, *, name, tr=256):
    tt = pa.shape[0]

    def body(ga_ref, gb_ref, pa_ref, pb_ref, o_ref):
        o_ref[...] = (_sigmoid(ga_ref[...]) * pa_ref[...] + _sigmoid(gb_ref[...]) * pb_ref[...]).astype(BF16)

    row = _rb(tr, D)
    return pl.pallas_call(body, grid=(tt // tr,), in_specs=[_rb(tr, D, 0), _rb(tr, D, 1), row, row], out_specs=row,
                          out_shape=jax.ShapeDtypeStruct((tt, D), BF16), compiler_params=_cp(("parallel",)),
                          name=name)(proj, proj, pa, pb)


def _merge_bwd(proj, pa, pb, dm, *, name, tr=256):
    tt = pa.shape[0]

    def body(ga_ref, gb_ref, pa_ref, pb_ref, dm_ref, dg_ref, dpa_ref, dpb_ref):
        dm_v = dm_ref[...]
        sa, sb = _sigmoid(ga_ref[...]), _sigmoid(gb_ref[...])
        dpa_ref[...] = (dm_v * sa).astype(BF16)
        dpb_ref[...] = (dm_v * sb).astype(BF16)
        dg_ref[:, 0:D] = (dm_v * pa_ref[...] * sa * (1.0 - sa)).astype(BF16)
        dg_ref[:, D:2 * D] = (dm_v * pb_ref[...] * sb * (1.0 - sb)).astype(BF16)

    row = _rb(tr, D)
    return pl.pallas_call(
        body, grid=(tt // tr,), in_specs=[_rb(tr, D, 0), _rb(tr, D, 1), row, row, row],
        out_specs=(_rb(tr, 2 * D), row, row),
        out_shape=(jax.ShapeDtypeStruct((tt, 2 * D), BF16), jax.ShapeDtypeStruct((tt, D), BF16),
                   jax.ShapeDtypeStruct((tt, D), BF16)),
        compiler_params=_cp(("parallel",)), name=name)(proj, proj, pa, pb, dm)


def _swiglu_fwd(gu, *, name, tr=256):
    tt = gu.shape[0]

    def body(g_ref, u_ref, o_ref):
        o_ref[...] = (_silu(g_ref[...]) * u_ref[...]).astype(BF16)

    return pl.pallas_call(body, grid=(tt // tr,), in_specs=[_rb(tr, FF, 0), _rb(tr, FF, 1)], out_specs=_rb(tr, FF),
                          out_shape=jax.ShapeDtypeStruct((tt, FF), BF16), compiler_params=_cp(("parallel",)),
                          name=name)(gu, gu)


def _swiglu_bwd(gu, dact, *, name, tr=256):
    tt = gu.shape[0]

    def body(g_ref, u_ref, da_ref, o_ref):
        gv, da = g_ref[...], da_ref[...]
        o_ref[:, 0:FF] = (da * u_ref[...] * _dsilu(gv)).astype(BF16)
        o_ref[:, FF:2 * FF] = (da * _silu(gv)).astype(BF16)

    return pl.pallas_call(body, grid=(tt // tr,), in_specs=[_rb(tr, FF, 0), _rb(tr, FF, 1), _rb(tr, FF)],
                          out_specs=_rb(tr, 2 * FF), out_shape=jax.ShapeDtypeStruct((tt, 2 * FF), BF16),
                          compiler_params=_cp(("parallel",)), name=name)(gu, gu, dact)


def _bias_from_table(table):
    lw = 1152
    n_hi = KB - A_MAX_REL
    w = jnp.concatenate([
        jnp.broadcast_to(table[:, 2 * A_MAX_REL:], (A_HEADS, n_hi)),
        jnp.flip(table[:, 1:2 * A_MAX_REL], axis=1),
        jnp.broadcast_to(table[:, 0:1], (A_HEADS, lw - n_hi - (2 * A_MAX_REL - 1)))], axis=1)
    flat = jnp.broadcast_to(w[:, None, :], (A_HEADS, QB, lw)).reshape(A_HEADS, QB * lw)
    skew = flat[:, :QB * (lw - 1)].reshape(A_HEADS, QB, lw - 1)
    bias = skew[:, :, QB - 1:QB - 1 + KB]
    qc = jnp.arange(QB)[:, None] // CH + A_PAST
    kc = jnp.arange(KB)[None, :] // CH
    inband = (kc <= qc) & (kc >= qc - A_PAST)
    return jnp.where(inband[None], bias, NEG)


def _attn_fwd(proj, kpad, vpad, bias, *, name):
    tt = proj.shape[0]

    def body(q_ref, k_ref, v_ref, b_ref, o_ref, l_ref):
        q0 = pl.multiple_of(pl.program_id(1) * QB, QB)
        q = q_ref[...]
        k = k_ref[pl.ds(q0, KB), :]
        v = v_ref[pl.ds(q0, KB), :]
        lane = lax.broadcasted_iota(jnp.int32, (QB, LANE), 1)
        valid = (lax.broadcasted_iota(jnp.int32, (QB, KB), 1) + q0) >= A_PAST * CH
        o = jnp.zeros((QB, LANE), F32)
        lse = jnp.zeros((QB, LANE), F32)
        for a in range(2):
            hm = (lane >= A_DH * a) & (lane < A_DH * (a + 1))
            s = _dot_nt(jnp.where(hm, q, 0.0).astype(BF16), k) * (A_DH ** -0.5) + b_ref[a]
            s = jnp.where(valid, s, NEG)
            m = jnp.max(s, axis=-1, keepdims=True)
            p = jnp.exp(s - m)
            l = jnp.sum(p, axis=-1, keepdims=True)
            o = jnp.where(hm, _dot((p / l).astype(BF16), v), o)
            lse = jnp.where(hm, m + jnp.log(l), lse)
        o_ref[...] = o.astype(BF16)
        l_ref[...] = lse

    kv = pl.BlockSpec((tt + A_PAST * CH, LANE), lambda h, i: (0, h))
    blk = pl.BlockSpec((QB, LANE), lambda h, i: (i, h))
    return pl.pallas_call(
        body, grid=(A_W // LANE, tt // QB),
        in_specs=[pl.BlockSpec((QB, LANE), lambda h, i: (i, OFF_QA // LANE + h)), kv, kv,
                  pl.BlockSpec((2, QB, KB), lambda h, i: (h, 0, 0))],
        out_specs=(blk, blk),
        out_shape=(jax.ShapeDtypeStruct((tt, A_W), BF16), jax.ShapeDtypeStruct((tt, A_W), F32)),
        compiler_params=_cp(("parallel", "parallel")), name=name)(proj, kpad, vpad, bias)


def _attn_bwd(proj, kpad, vpad, bias, o, lse, do, *, name):
    tt = proj.shape[0]

    def body(q_ref, k_ref, v_ref, b_ref, o_ref, l_ref, do_ref, dq_ref, dk_ref, dv_ref, db_ref):
        @pl.when(pl.program_id(1) == 0)
        def _():
            dk_ref[...] = jnp.zeros_like(dk_ref)
            dv_ref[...] = jnp.zeros_like(dv_ref)
            db_ref[...] = jnp.zeros_like(db_ref)

        q0 = pl.multiple_of(pl.program_id(1) * QB, QB)
        q, do_v, lse = q_ref[...], do_ref[...], l_ref[...]
        k = k_ref[pl.ds(q0, KB), :]
        v = v_ref[pl.ds(q0, KB), :]
        dsum = do_v * o_ref[...].astype(F32)
        lane = lax.broadcasted_iota(jnp.int32, (QB, LANE), 1)
        valid = (lax.broadcasted_iota(jnp.int32, (QB, KB), 1) + q0) >= A_PAST * CH
        dq = jnp.zeros((QB, LANE), F32)
        dk = jnp.zeros((KB, LANE), F32)
        dv = jnp.zeros((KB, LANE), F32)
        for a in range(2):
            hm = (lane >= A_DH * a) & (lane < A_DH * (a + 1))
            qa = jnp.where(hm, q, 0.0).astype(BF16)
            doa = jnp.where(hm, do_v, 0.0).astype(BF16)
            s = _dot_nt(qa, k) * (A_DH ** -0.5) + b_ref[a]
            s = jnp.where(valid, s, NEG)
            lse_a = jnp.max(jnp.where(hm, lse, NEG), axis=-1, keepdims=True)
            p = jnp.exp(s - lse_a)
            dp = _dot_nt(doa, v)
            dsum_a = jnp.sum(jnp.where(hm, dsum, 0.0), axis=-1, keepdims=True)
            ds = p * (dp - dsum_a)
            db_ref[a] += ds
            dsb = (ds * (A_DH ** -0.5)).astype(BF16)
            dq = jnp.where(hm, _dot(dsb, k), dq)
            dk += _dot_tn(dsb, qa)
            dv += _dot_tn(p.astype(BF16), doa)
        dq_ref[...] = dq
        dk_ref[pl.ds(q0, KB), :] += dk
        dv_ref[pl.ds(q0, KB), :] += dv

    kv = pl.BlockSpec((tt + A_PAST * CH, LANE), lambda h, i: (0, h))
    blk = pl.BlockSpec((QB, LANE), lambda h, i: (i, h))
    bsp = pl.BlockSpec((2, QB, KB), lambda h, i: (h, 0, 0))
    pad_shape = jax.ShapeDtypeStruct((tt + A_PAST * CH, A_W), F32)
    return pl.pallas_call(
        body, grid=(A_W // LANE, tt // QB),
        in_specs=[pl.BlockSpec((QB, LANE), lambda h, i: (i, OFF_QA // LANE + h)), kv, kv, bsp, blk, blk, blk],
        out_specs=(blk, kv, kv, bsp),
        out_shape=(jax.ShapeDtypeStruct((tt, A_W), F32), pad_shape, pad_shape,
                   jax.ShapeDtypeStruct((A_HEADS, QB, KB), F32)),
        compiler_params=_cp(("parallel", "arbitrary")), name=name)(proj, kpad, vpad, bias, o, lse, do)


GTR = 256


def _taps(w_ref, grp):
    return [w_ref[j:j + 1, grp * B_W:(grp + 1) * B_W] for j in range(CONV_K)]


def _shifts(xe, rows):
    return [xe[8:8 + rows]] + [pltpu.roll(xe, s, 0)[8:8 + rows] for s in range(1, CONV_K)]


def _conv(shifts, taps):
    acc = taps[CONV_K - 1] * shifts[0]
    for s in range(1, CONV_K):
        acc = acc + taps[CONV_K - 1 - s] * shifts[s]
    return acc


def _qk_scale(grp):
    return B_DH ** -0.5 if grp == 0 else 1.0


def _act_fwd(c, grp):
    y = _silu(c)
    if grp == 2:
        return y
    parts = []
    for hd in range(B_HEADS):
        yh = y[:, hd * B_DH:(hd + 1) * B_DH]
        parts.append(yh * (lax.rsqrt(jnp.sum(yh * yh, axis=-1, keepdims=True) + EPS) * _qk_scale(grp)))
    return jnp.concatenate(parts, axis=1)


def _act_bwd(c, dy, grp):
    if grp == 2:
        return dy * _dsilu(c)
    y = _silu(c)
    parts = []
    for hd in range(B_HEADS):
        yh = y[:, hd * B_DH:(hd + 1) * B_DH]
        r = lax.rsqrt(jnp.sum(yh * yh, axis=-1, keepdims=True) + EPS)
        dyh = dy[:, hd * B_DH:(hd + 1) * B_DH] * _qk_scale(grp)
        parts.append(r * dyh - yh * (r * r * r) * jnp.sum(dyh * yh, axis=-1, keepdims=True))
    return jnp.concatenate(parts, axis=1) * _dsilu(c)


def _chunk_tri(n, upper=False):
    r = lax.broadcasted_iota(jnp.int32, (n, n), 0)
    c = lax.broadcasted_iota(jnp.int32, (n, n), 1)
    same = (r // CH) == (c // CH)
    return jnp.where(same & ((r <= c) if upper else (r >= c)), 1.0, 0.0).astype(F32)


def _gate_rows(ba, par_ref):
    lane = lax.broadcasted_iota(jnp.int32, ba.shape, 1)
    z = ba + par_ref[1:2, :]
    sp = jnp.maximum(z, 0.0) + jnp.log(1.0 + jnp.exp(-jnp.abs(z)))
    g = -jnp.exp(par_ref[0:1, :]) * sp
    return jnp.where(lane < B_HEADS, _sigmoid(ba), jnp.where(lane < 2 * B_HEADS, g, 0.0)), z


def _prev8(cb):
    return pl.BlockSpec((8, B_W), lambda i: (jnp.maximum(i * (GTR // 8) - 1, 0), cb))


def _next8(cb, nb):
    return pl.BlockSpec((8, B_W), lambda i: (jnp.minimum((i + 1) * (GTR // 8), nb * (GTR // 8) - 1), cb))


def _gdn_pre_fwd(proj, wconv, par, *, name):
    tt = proj.shape[0]

    def body(q_ref, k_ref, v_ref, qh_ref, kh_ref, vh_ref, ba_ref, w_ref, par_ref, qo_ref, ko_ref, vo_ref, aux_ref):
        first = pl.program_id(0) == 0
        for grp, (x_ref, h_ref, o_ref) in enumerate(((q_ref, qh_ref, qo_ref), (k_ref, kh_ref, ko_ref),
                                                     (v_ref, vh_ref, vo_ref))):
            xe = jnp.concatenate([jnp.where(first, 0.0, h_ref[...]), x_ref[...]], axis=0)
            o_ref[...] = _act_fwd(_conv(_shifts(xe, GTR), _taps(w_ref, grp)), grp)
        bg, _ = _gate_rows(ba_ref[...], par_ref)
        lane = lax.broadcasted_iota(jnp.int32, bg.shape, 1)
        aux_ref[...] = jnp.where(lane < B_HEADS, bg, _dot(_chunk_tri(GTR), bg, HI))

    col = lambda off: _rb(GTR, B_W, off // B_W)
    outs = jax.ShapeDtypeStruct((tt, B_W), F32)
    return pl.pallas_call(
        body, grid=(tt // GTR,),
        in_specs=[col(OFF_QB), col(OFF_KB), col(OFF_VB), _prev8(OFF_QB // B_W), _prev8(OFF_KB // B_W),
                  _prev8(OFF_VB // B_W), _rb(GTR, LANE, OFF_BA // LANE), _whole((CONV_K, 3 * B_W)),
                  _whole((8, LANE))],
        out_specs=(_rb(GTR, B_W), _rb(GTR, B_W), _rb(GTR, B_W), _rb(GTR, LANE)),
        out_shape=(outs, outs, outs, jax.ShapeDtypeStruct((tt, LANE), F32)),
        compiler_params=_cp(("parallel",)), name=name)(proj, proj, proj, proj, proj, proj, proj, wconv, par)


def _gdn_pre_bwd(proj, wconv, par, dq, dk, dv, daux, *, name):
    tt = proj.shape[0]
    nb = tt // GTR

    def body(q_ref, k_ref, v_ref, qh_ref, kh_ref, vh_ref, qn_ref, kn_ref, vn_ref, ba_ref, w_ref, par_ref,
             dq_ref, dk_ref, dv_ref, dqn_ref, dkn_ref, dvn_ref, daux_ref, dx_ref, dba_ref, dw_ref, dpar_ref):
        i = pl.program_id(0)
        first, last = i == 0, i == nb - 1

        @pl.when(first)
        def _():
            dw_ref[...] = jnp.zeros_like(dw_ref)
            dpar_ref[...] = jnp.zeros_like(dpar_ref)

        groups = ((q_ref, qh_ref, qn_ref, dq_ref, dqn_ref), (k_ref, kh_ref, kn_ref, dk_ref, dkn_ref),
                  (v_ref, vh_ref, vn_ref, dv_ref, dvn_ref))
        for grp, (x_ref, h_ref, xn_ref, d_ref, dn_ref) in enumerate(groups):
            taps = _taps(w_ref, grp)
            xe = jnp.concatenate([jnp.where(first, 0.0, h_ref[...]), x_ref[...]], axis=0)
            sh = _shifts(xe, GTR)
            dc = _act_bwd(_conv(sh, taps), d_ref[...], grp)
            xe_n = jnp.concatenate([x_ref[GTR - 8:GTR, :], xn_ref[...]], axis=0)
            dcn = _act_bwd(_conv(_shifts(xe_n, 8), taps), dn_ref[...], grp)
            dce = jnp.concatenate([dc, jnp.where(last, 0.0, dcn)], axis=0)
            dx = taps[CONV_K - 1] * dc
            dw_ref[CONV_K - 1:CONV_K, grp * B_W:(grp + 1) * B_W] += _colsum(dc * sh[0])
            for s in range(1, CONV_K):
                dx = dx + taps[CONV_K - 1 - s] * pltpu.roll(dce, GTR + 8 - s, 0)[0:GTR]
                dw_ref[CONV_K - 1 - s:CONV_K - s, grp * B_W:(grp + 1) * B_W] += _colsum(dc * sh[s])
            dx_ref[:, grp * B_W:(grp + 1) * B_W] = dx.astype(BF16)
        ba = ba_ref[...]
        lane = lax.broadcasted_iota(jnp.int32, ba.shape, 1)
        bg, z = _gate_rows(ba, par_ref)
        daux_v = daux_ref[...]
        dg = _dot(_chunk_tri(GTR, upper=True), daux_v, HI)
        dgl = jnp.where((lane >= B_HEADS) & (lane < 2 * B_HEADS), dg, 0.0)
        da = dgl * (-jnp.exp(par_ref[0:1, :])) * _sigmoid(z)
        dbr = jnp.where(lane < B_HEADS, daux_v * bg * (1.0 - bg), 0.0)
        dba_ref[...] = (dbr + da).astype(BF16)
        dpar_ref[0:1, :] += _colsum(dgl * bg)
        dpar_ref[1:2, :] += _colsum(da)

    col = lambda off: _rb(GTR, B_W, off // B_W)
    row, rowl = _rb(GTR, B_W), _rb(GTR, LANE)
    return pl.pallas_call(
        body, grid=(nb,),
        in_specs=[col(OFF_QB), col(OFF_KB), col(OFF_VB),
                  _prev8(OFF_QB // B_W), _prev8(OFF_KB // B_W), _prev8(OFF_VB // B_W),
                  _next8(OFF_QB // B_W, nb), _next8(OFF_KB // B_W, nb), _next8(OFF_VB // B_W, nb),
                  _rb(GTR, LANE, OFF_BA // LANE), _whole((CONV_K, 3 * B_W)), _whole((8, LANE)),
                  row, row, row, _next8(0, nb), _next8(0, nb), _next8(0, nb), rowl],
        out_specs=(_rb(GTR, 3 * B_W), rowl, _whole((8, 3 * B_W)), _whole((8, LANE))),
        out_shape=(jax.ShapeDtypeStruct((tt, 3 * B_W), BF16), jax.ShapeDtypeStruct((tt, LANE), BF16),
                   jax.ShapeDtypeStruct((8, 3 * B_W), F32), jax.ShapeDtypeStruct((8, LANE), F32)),
        compiler_params=_cp(("arbitrary",)), name=name)(
            proj, proj, proj, proj, proj, proj, proj, proj, proj, proj, wconv, par, dq, dk, dv, dq, dk, dv, daux)


def _col(x, j):
    lane = lax.broadcasted_iota(jnp.int32, x.shape, 1)
    return jnp.sum(jnp.where(lane == j, x, 0.0), axis=-1, keepdims=True)


def _as_row(x, j):
    lane = lax.broadcasted_iota(jnp.int32, (CH, LANE), 1)
    return _dot_nt(jnp.where(lane == j, 1.0, 0.0).astype(F32), x, HI)


def _chunk_masks():
    r = lax.broadcasted_iota(jnp.int32, (CH, CH), 0)
    c = lax.broadcasted_iota(jnp.int32, (CH, CH), 1)
    return r > c, r >= c


def _gdn_lower(k, aux, *, name):
    tt = k.shape[0]

    def body(k_ref, aux_ref, l_ref):
        aux_v = aux_ref[...]
        strict, _ = _chunk_masks()
        for hd in range(B_HEADS):
            kh = k_ref[:, hd * B_DH:(hd + 1) * B_DH].astype(BF16)
            diff = _col(aux_v, B_HEADS + hd) - _as_row(aux_v, B_HEADS + hd)
            dec = jnp.exp(jnp.where(strict, diff, NEG))
            l_ref[hd] = _col(aux_v, hd) * _dot_nt(kh, kh) * dec

    return pl.pallas_call(
        body, grid=(tt // CH,), in_specs=[_rb(CH, B_W), _rb(CH, LANE)],
        out_specs=pl.BlockSpec((B_HEADS, CH, CH), lambda i: (i, 0, 0)),
        out_shape=jax.ShapeDtypeStruct((tt // CH * B_HEADS, CH, CH), F32),
        compiler_params=_cp(("parallel",)), name=name)(k, aux)


def _tri_inverse(lt, *, name):
    nb = lt.shape[2]

    def body(l_ref, t_ref):
        rowid = lax.broadcasted_iota(jnp.int32, (CH, nb), 0)

        def outer(i, carry):
            def inner(j, acc):
                return acc + l_ref[i, pl.ds(j, 1), :] * t_ref[j]

            acc = lax.fori_loop(0, i, inner, jnp.zeros((CH, nb), F32))
            t_ref[i] = jnp.where(rowid == i, 1.0, 0.0) - acc
            return carry

        lax.fori_loop(0, CH, outer, 0)

    return pl.pallas_call(body, out_shape=jax.ShapeDtypeStruct(lt.shape, F32),
                          in_specs=[pl.BlockSpec(memory_space=pltpu.VMEM)],
                          out_specs=pl.BlockSpec(memory_space=pltpu.VMEM),
                          compiler_params=_cp(), name=name)(lt)


def _gdn_chunk_terms(qh, kh, vh, aux_v, aux_last, tinv, hd):
    strict, incl = _chunk_masks()
    beta = _col(aux_v, hd)
    gc = _col(aux_v, B_HEADS + hd)
    gl = _col(aux_last, B_HEADS + hd)
    egc = jnp.exp(gc)
    diff = gc - _as_row(aux_v, B_HEADS + hd)
    dec = jnp.exp(jnp.where(incl, diff, NEG))
    kb = kh.astype(BF16)
    kk = _dot_nt(kb, kb)
    qk0 = _dot_nt(qh.astype(BF16), kb)
    rhs_v = vh * beta
    rhs_k = kh * (beta * egc)
    u0 = _dot(tinv, rhs_v, HI)
    w = _dot(tinv, rhs_k, HI)
    return dict(strict=strict, incl=incl, beta=beta, gc=gc, gl=gl, egc=egc, dec=dec, kk=kk, qk0=qk0,
                qk=qk0 * dec, u0=u0, w=w, q_dec=qh * egc, ekd=jnp.exp(gl - gc))


def _gdn_scan_fwd(q, k, v, aux, tinv, *, name):
    tt = q.shape[0]

    def body(q_ref, k_ref, v_ref, aux_ref, t_ref, o_ref, ss_ref, s_ref):
        @pl.when(pl.program_id(0) == 0)
        def _():
            s_ref[...] = jnp.zeros_like(s_ref)

        aux_v = aux_ref[...]
        aux_last = aux_ref[CH - 1:CH, :]
        for hd in range(B_HEADS):
            sl = slice(hd * B_DH, (hd + 1) * B_DH)
            qh, kh, vh = q_ref[:, sl], k_ref[:, sl], v_ref[:, sl]
            c = _gdn_chunk_terms(qh, kh, vh, aux_v, aux_last, t_ref[hd], hd)
            st = s_ref[hd]
            ss_ref[hd] = st
            sb = st.astype(BF16)
            u = c["u0"] - _dot(c["w"].astype(BF16), sb)
            ub = u.astype(BF16)
            o_ref[:, sl] = _dot(c["q_dec"].astype(BF16), sb) + _dot(c["qk"].astype(BF16), ub)
            s_ref[hd] = st * jnp.exp(c["gl"]) + _dot_tn((kh * c["ekd"]).astype(BF16), ub)

    row = _rb(CH, B_W)
    return pl.pallas_call(
        body, grid=(tt // CH,),
        in_specs=[row, row, row, _rb(CH, LANE), pl.BlockSpec((B_HEADS, CH, CH), lambda i: (i, 0, 0))],
        out_specs=(row, pl.BlockSpec((B_HEADS, B_DH, B_DH), lambda i: (i, 0, 0))),
        out_shape=(jax.ShapeDtypeStruct((tt, B_W), F32),
                   jax.ShapeDtypeStruct((tt // CH * B_HEADS, B_DH, B_DH), F32)),
        scratch_shapes=[pltpu.VMEM((B_HEADS, B_DH, B_DH), F32)],
        compiler_params=_cp(("arbitrary",)), name=name)(q, k, v, aux, tinv)


def _gdn_scan_bwd(q, k, v, aux, tinv, ss, do, *, name):
    tt = q.shape[0]
    nc = tt // CH

    def body(q_ref, k_ref, v_ref, aux_ref, t_ref, ss_ref, do_ref, dq_ref, dk_ref, dv_ref, daux_ref, ds_ref):
        @pl.when(pl.program_id(0) == 0)
        def _():
            ds_ref[...] = jnp.zeros_like(ds_ref)

        aux_v = aux_ref[...]
        aux_last = aux_ref[CH - 1:CH, :]
        lane = lax.broadcasted_iota(jnp.int32, (CH, LANE), 1)
        rowi = lax.broadcasted_iota(jnp.int32, (CH, 1), 0)
        daux = jnp.zeros((CH, LANE), F32)
        for hd in range(B_HEADS):
            sl = slice(hd * B_DH, (hd + 1) * B_DH)
            qh, kh, vh, do_h = q_ref[:, sl], k_ref[:, sl], v_ref[:, sl], do_ref[:, sl]
            tinv_h = t_ref[hd]
            c = _gdn_chunk_terms(qh, kh, vh, aux_v, aux_last, tinv_h, hd)
            beta, egc, ekd, dec, w = c["beta"], c["egc"], c["ekd"], c["dec"], c["w"]
            st, dst = ss_ref[hd], ds_ref[hd]
            sb, dsb = st.astype(BF16), dst.astype(BF16)
            eg_last = jnp.exp(c["gl"])
            wb = w.astype(BF16)
            u = c["u0"] - _dot(wb, sb)
            ub = u.astype(BF16)
            kdb = (kh * ekd).astype(BF16)
            qdb = c["q_dec"].astype(BF16)
            qkb = c["qk"].astype(BF16)
            dob = do_h.astype(BF16)
            du = _dot_tn(qkb, dob) + _dot(kdb, dsb)
            dub = du.astype(BF16)
            dq_dec = _dot_nt(dob, sb)
            dqk = jnp.where(c["incl"], _dot_nt(dob, ub), 0.0)
            dk_dec = _dot_nt(ub, dsb)
            dgl = jnp.sum(jnp.sum(st * dst, axis=-1, keepdims=True), axis=0, keepdims=True) * eg_last
            dw = -_dot_nt(dub, sb)
            ds_ref[hd] = _dot_tn(qdb, dob) + eg_last * dst - _dot_tn(wb, dub)
            drv = _dot_tn(tinv_h, du, HI)
            drk = _dot_tn(tinv_h, dw, HI)
            dl = -(_dot_nt(drv.astype(BF16), c["u0"].astype(BF16)) + _dot_nt(drk.astype(BF16), wb))
            dl = jnp.where(c["strict"], dl, 0.0)
            dv_ref[:, sl] = drv * beta
            rk = jnp.sum(drk * kh, axis=-1, keepdims=True)
            dbeta = jnp.sum(drv * vh, axis=-1, keepdims=True) + rk * egc
            dgc = rk * beta * egc
            dk = drk * (beta * egc)
            ldec = dl * dec
            dbeta = dbeta + jnp.sum(ldec * c["kk"], axis=-1, keepdims=True)
            dkk = (ldec * beta).astype(BF16)
            dqk0 = (dqk * dec).astype(BF16)
            ddec = ldec * beta * c["kk"] + dqk * c["qk"]
            kb, qb = kh.astype(BF16), qh.astype(BF16)
            dq = _dot(dqk0, kb) + dq_dec * egc
            dk = dk + _dot_tn(dqk0, qb) + _dot(dkk, kb) + _dot_tn(dkk, kb) + dk_dec * ekd
            dgc = dgc + jnp.sum(ddec, axis=-1, keepdims=True) - _col_from_rowsum(ddec)
            dgc = dgc + jnp.sum(dq_dec * qh, axis=-1, keepdims=True) * egc
            kd = jnp.sum(dk_dec * kh, axis=-1, keepdims=True) * ekd
            dgc = dgc - kd
            dgc = dgc + jnp.where(rowi == CH - 1, jnp.sum(kd, axis=0, keepdims=True) + dgl, 0.0)
            dq_ref[:, sl] = dq
            dk_ref[:, sl] = dk
            daux = daux + jnp.where(lane == hd, dbeta, 0.0) + jnp.where(lane == B_HEADS + hd, dgc, 0.0)
        daux_ref[...] = daux

    rev = lambda width: pl.BlockSpec((CH, width), lambda i: (nc - 1 - i, 0))
    rev3 = lambda a, b: pl.BlockSpec((B_HEADS, a, b), lambda i: (nc - 1 - i, 0, 0))
    outs = jax.ShapeDtypeStruct((tt, B_W), F32)
    return pl.pallas_call(
        body, grid=(nc,),
        in_specs=[rev(B_W), rev(B_W), rev(B_W), rev(LANE), rev3(CH, CH), rev3(B_DH, B_DH), rev(B_W)],
        out_specs=(rev(B_W), rev(B_W), rev(B_W), rev(LANE)),
        out_shape=(outs, outs, outs, jax.ShapeDtypeStruct((tt, LANE), F32)),
        scratch_shapes=[pltpu.VMEM((B_HEADS, B_DH, B_DH), F32)],
        compiler_params=_cp(("arbitrary",)), name=name)(q, k, v, aux, tinv, ss, do)


def _col_from_rowsum(m):
    return _dot_tn(m, jnp.ones((CH, LANE), F32), HI)[:, 0:1]


def _gdn_post_fwd(o, proj, gn, *, name, tr=256):
    tt = o.shape[0]

    def body(o_ref, z_ref, g_ref, y_ref):
        for hd in range(B_HEADS):
            sl = slice(hd * B_DH, (hd + 1) * B_DH)
            oh = o_ref[:, sl]
            r = lax.rsqrt(jnp.mean(oh * oh, axis=-1, keepdims=True) + EPS)
            y_ref[:, sl] = (oh * r * g_ref[...] * _silu(z_ref[:, sl])).astype(BF16)

    return pl.pallas_call(body, grid=(tt // tr,), in_specs=[_rb(tr, B_W), _rb(tr, B_W, OFF_ZB // B_W), _whole((1, B_DH))],
                          out_specs=_rb(tr, B_W), out_shape=jax.ShapeDtypeStruct((tt, B_W), BF16),
                          compiler_params=_cp(("parallel",)), name=name)(o, proj, gn)


def _gdn_post_bwd(o, proj, gn, dy, *, name, tr=256):
    tt = o.shape[0]

    def body(o_ref, z_ref, g_ref, dy_ref, do_ref, dz_ref, dg_ref):
        @pl.when(pl.program_id(0) == 0)
        def _():
            dg_ref[...] = jnp.zeros_like(dg_ref)

        g = g_ref[...]
        for hd in range(B_HEADS):
            sl = slice(hd * B_DH, (hd + 1) * B_DH)
            oh, zh, dyh = o_ref[:, sl], z_ref[:, sl], dy_ref[:, sl]
            r = lax.rsqrt(jnp.mean(oh * oh, axis=-1, keepdims=True) + EPS)
            a = oh * r
            s = _silu(zh)
            da = dyh * g * s
            dg_ref[0:1, :] += _colsum(dyh * a * s)
            dz_ref[:, sl] = (dyh * a * g * _dsilu(zh)).astype(BF16)
            do_ref[:, sl] = r * (da - a * jnp.mean(da * a, axis=-1, keepdims=True))

    return pl.pallas_call(
        body, grid=(tt // tr,), in_specs=[_rb(tr, B_W), _rb(tr, B_W, OFF_ZB // B_W), _whole((1, B_DH)), _rb(tr, B_W)],
        out_specs=(_rb(tr, B_W), _rb(tr, B_W), _whole((8, B_DH))),
        out_shape=(jax.ShapeDtypeStruct((tt, B_W), F32), jax.ShapeDtypeStruct((tt, B_W), BF16),
                   jax.ShapeDtypeStruct((8, B_DH), F32)),
        compiler_params=_cp(("arbitrary",)), name=name)(o, proj, gn, dy)


def _adamw(parts, w, m, v, *, name, tr=256):
    npart, r, c = parts.shape
    tr = max([t for t in range(8, min(r, tr) + 1, 8) if r % t == 0], default=r)
    c1, c2 = 1.0 - ADAM_B1 ** ADAM_STEP, 1.0 - ADAM_B2 ** ADAM_STEP

    def body(p_ref, w_ref, m_ref, v_ref, g_ref, d_ref, mo_ref, vo_ref):
        g = p_ref[0].astype(F32)
        for i in range(1, npart):
            g = g + p_ref[i].astype(F32)
        mn = ADAM_B1 * m_ref[...] + (1.0 - ADAM_B1) * g
        vn = ADAM_B2 * v_ref[...] + (1.0 - ADAM_B2) * (g * g)
        g_ref[...] = g
        mo_ref[...] = mn
        vo_ref[...] = vn
        d_ref[...] = -ADAM_LR * ((mn / c1) / (jnp.sqrt(vn / c2) + ADAM_EPS) + ADAM_WD * w_ref[...])

    row = pl.BlockSpec((tr, c), lambda i: (i, 0))
    out = jax.ShapeDtypeStruct((r, c), F32)
    return pl.pallas_call(body, grid=(r // tr,), in_specs=[pl.BlockSpec((npart, tr, c), lambda i: (0, i, 0)), row, row, row],
                          out_specs=(row, row, row, row), out_shape=(out, out, out, out),
                          compiler_params=_cp(("parallel",)), name=name)(parts, w, m, v)


def _peer(k):
    x, y, c = lax.axis_index("x"), lax.axis_index("y"), lax.axis_index("c")
    return ((1 - x) if k & 4 else x, (1 - y) if k & 2 else y, (1 - c) if k & 1 else c)


def _my_index():
    return 4 * lax.axis_index("x") + 2 * lax.axis_index("y") + lax.axis_index("c")


def _index_of(p):
    return 4 * p[0] + 2 * p[1] + p[2]


def _all_gather(xs, *, name):
    n = len(xs)

    def body(*refs):
        x_refs, o_refs = refs[:n], refs[n:2 * n]
        send, recv, loc = refs[2 * n:]
        me = _my_index()
        copies = []
        for a in range(n):
            cp = pltpu.make_async_copy(x_refs[a], o_refs[a].at[me], loc.at[a])
            cp.start()
            copies.append(cp)
        rdmas = []
        for a in range(n):
            for k in range(1, N_DEV):
                r = pltpu.make_async_remote_copy(
                    src_ref=x_refs[a], dst_ref=o_refs[a].at[me], send_sem=send.at[a, k - 1], recv_sem=recv.at[a, k - 1],
                    device_id=_peer(k), device_id_type=pl.DeviceIdType.MESH)
                r.start()
                rdmas.append(r)
        for a in range(n):
            for k in range(1, N_DEV):
                pltpu.make_async_remote_copy(
                    src_ref=x_refs[a], dst_ref=o_refs[a].at[_index_of(_peer(k))], send_sem=send.at[a, k - 1],
                    recv_sem=recv.at[a, k - 1], device_id=_peer(k), device_id_type=pl.DeviceIdType.MESH).wait_recv()
        for r in rdmas:
            r.wait_send()
        for cp in copies:
            cp.wait()

    any_spec = pl.BlockSpec(memory_space=pl.ANY)
    return pl.pallas_call(
        body, in_specs=[any_spec] * n, out_specs=tuple([any_spec] * n),
        out_shape=tuple(jax.ShapeDtypeStruct((N_DEV,) + x.shape, x.dtype) for x in xs),
        scratch_shapes=[pltpu.SemaphoreType.DMA((n, N_DEV - 1)), pltpu.SemaphoreType.DMA((n, N_DEV - 1)),
                        pltpu.SemaphoreType.DMA((n,))],
        name=name)(*xs)


def _all_to_all(xs, *, name):
    n = len(xs)

    def body(*refs):
        x_refs, o_refs = refs[:n], refs[n:2 * n]
        send, recv, loc = refs[2 * n:]
        me = _my_index()
        copies = []
        for a in range(n):
            cp = pltpu.make_async_copy(x_refs[a].at[me], o_refs[a].at[me], loc.at[a])
            cp.start()
            copies.append(cp)
        rdmas = []
        for a in range(n):
            for k in range(1, N_DEV):
                r = pltpu.make_async_remote_copy(
                    src_ref=x_refs[a].at[_index_of(_peer(k))], dst_ref=o_refs[a].at[me], send_sem=send.at[a, k - 1],
                    recv_sem=recv.at[a, k - 1], device_id=_peer(k), device_id_type=pl.DeviceIdType.MESH)
                r.start()
                rdmas.append(r)
        for a in range(n):
            for k in range(1, N_DEV):
                pltpu.make_async_remote_copy(
                    src_ref=x_refs[a].at[me], dst_ref=o_refs[a].at[_index_of(_peer(k))], send_sem=send.at[a, k - 1],
                    recv_sem=recv.at[a, k - 1], device_id=_peer(k), device_id_type=pl.DeviceIdType.MESH).wait_recv()
        for r in rdmas:
            r.wait_send()
        for cp in copies:
            cp.wait()

    any_spec = pl.BlockSpec(memory_space=pl.ANY)
    return pl.pallas_call(
        body, in_specs=[any_spec] * n, out_specs=tuple([any_spec] * n),
        out_shape=tuple(jax.ShapeDtypeStruct(x.shape, x.dtype) for x in xs),
        scratch_shapes=[pltpu.SemaphoreType.DMA((n, N_DEV - 1)), pltpu.SemaphoreType.DMA((n, N_DEV - 1)),
                        pltpu.SemaphoreType.DMA((n,))],
        name=name)(*xs)


def _win_to_mine(w):
    pad = jnp.zeros(w.shape[:-1] + (IN_PAD - IN_DIM,), w.dtype)
    return jnp.concatenate([w[..., 3592:5640], w[..., 0:3584], w[..., 3584:3592], pad], axis=-1)


def _win_from_mine(g):
    return jnp.concatenate([g[..., 2048:5632], g[..., 5632:5640], g[..., 0:2048]], axis=-1)


def _cols_gathered(g):
    return jnp.transpose(g, (1, 2, 0, 3)).reshape(g.shape[1], g.shape[2], N_DEV * g.shape[3])


def _rows_gathered(g):
    return jnp.transpose(g, (1, 0, 2, 3)).reshape(g.shape[1], N_DEV * g.shape[2], g.shape[3])


def _cols_to_slabs(g):
    l, k, n = g.shape
    return jnp.transpose(g.reshape(l, k, N_DEV, n // N_DEV), (2, 0, 1, 3))


def _rows_to_slabs(g):
    l, k, n = g.shape
    return jnp.transpose(g.reshape(l, N_DEV, k // N_DEV, n), (1, 0, 2, 3))


def _pad_rows(a, mult=8):
    r = (-a.shape[0]) % mult
    return a if r == 0 else jnp.concatenate([a, jnp.zeros((r,) + a.shape[1:], a.dtype)], axis=0)


def _lanes(vec, start):
    return jnp.zeros((1, LANE), F32).at[0, start:start + vec.shape[0]].set(vec)


def _small_spec(depth):
    return (("b_ada", (depth, 6 * D)), ("norm1_g", (depth, D)), ("norm2_g", (depth, D)),
            ("rel_table", (depth, A_HEADS, 2 * A_MAX_REL + 1)), ("a_log", (depth, B_HEADS)),
            ("dt_bias", (depth, B_HEADS)), ("gdn_norm_g", (depth, B_DH)), ("final_g", (D,)))


def _pack_small(d, extra, depth):
    spec = _small_spec(depth)
    rows = -(-(sum(math.prod(s) for _, s in spec) + 1) // (8 * LANE)) * 8
    flat = jnp.concatenate([d[n].reshape(-1).astype(F32) for n, _ in spec] + [extra.reshape(-1)])
    flat = jnp.concatenate([flat, jnp.zeros((rows * LANE - flat.shape[0],), F32)])
    return flat.reshape(rows, LANE)


def _unpack_small(p, depth):
    flat = p.reshape(-1)
    out, off = {}, 0
    for n, s in _small_spec(depth):
        sz = math.prod(s)
        out[n] = flat[off:off + sz].reshape(s)
        off += sz
    return out, flat[off]


def kernel(x, c, w_ada, b_ada, norm1_g, norm2_g, w_in, rel_table, w_conv, a_log, dt_bias, gdn_norm_g, w_branch_a, w_branch_b, w_out, w_ffn_in, w_ffn_out, final_g, loss_target, m_w_ada, m_b_ada, m_norm1_g, m_norm2_g, m_w_in, m_rel_table, m_w_conv, m_a_log, m_dt_bias, m_gdn_norm_g, m_w_branch_a, m_w_branch_b, m_w_out, m_w_ffn_in, m_w_ffn_out, m_final_g, v_w_ada, v_b_ada, v_norm1_g, v_norm2_g, v_w_in, v_rel_table, v_w_conv, v_a_log, v_dt_bias, v_gdn_norm_g, v_w_branch_a, v_w_branch_b, v_w_out, v_w_ffn_in, v_w_ffn_out, v_final_g):
    tt = x.shape[1]
    x0 = x[0]
    tgt = loss_target[0]
    me = _my_index()
    depth = w_in.shape[0]

    (g_in, g_a, g_b, g_out, g_fi, g_fo, g_conv, g_c) = _all_gather(
        [w_in.astype(BF16), w_branch_a.astype(BF16), w_branch_b.astype(BF16), w_out.astype(BF16),
         w_ffn_in.astype(BF16), w_ffn_out.astype(BF16), w_conv, _pad_rows(c)], name="gather_weights")
    win = _win_to_mine(_cols_gathered(g_in))
    wa, wb = _cols_gathered(g_a), _cols_gathered(g_b)
    wout, wfo = _rows_gathered(g_out), _rows_gathered(g_fo)
    wfi = _cols_gathered(g_fi)
    wconv = _cols_gathered(g_conv)
    c_all = g_c[:, 0, :]
    cond = c_all * (1.0 / (1.0 + jnp.exp(-c_all)))
    cond = _pad_rows(cond, 16)

    mod_cols = jnp.stack([_mm(cond, w_ada[l], name="mod_mm")[:N_DEV] for l in range(depth)])
    (g_mod,) = _all_gather([mod_cols], name="gather_mod")
    mod_all = jnp.transpose(g_mod, (1, 2, 0, 3)).reshape(depth, N_DEV, 6 * D)
    mod = lax.dynamic_index_in_dim(mod_all, me, axis=1, keepdims=False) + b_ada
    mods = mod.reshape(depth, 6, 1, D)

    n1g, n2g = norm1_g.reshape(depth, 1, D), norm2_g.reshape(depth, 1, D)
    gng = gdn_norm_g.reshape(depth, 1, B_DH)
    fg = final_g.reshape(1, D)

    saved = []
    xin, h1 = _adaln_fwd(x0, n1g[0], mods[0, 1], mods[0, 0], name="adaln1_first")
    for l in range(depth):
        sh1, sc1, gt1, sh2, sc2, gt2 = (mods[l, i] for i in range(6))
        proj = _mm(h1, win[l], name="proj_mm", tn=640)
        kpad = jnp.pad(proj[:, OFF_KA:OFF_KA + A_W].astype(BF16), ((A_PAST * CH, 0), (0, 0)))
        vpad = jnp.pad(proj[:, OFF_VA:OFF_VA + A_W].astype(BF16), ((A_PAST * CH, 0), (0, 0)))
        bias, bias_vjp = jax.vjp(_bias_from_table, rel_table[l])
        ya, lse = _attn_fwd(proj, kpad, vpad, bias, name="attn_fwd")
        par = jnp.concatenate([_lanes(a_log[l], B_HEADS), _lanes(dt_bias[l], B_HEADS), jnp.zeros((6, LANE), F32)], axis=0)
        qn, kn, vn, aux = _gdn_pre_fwd(proj, wconv[l], par, name="gdn_pre_fwd")
        lower = _gdn_lower(kn, aux, name="gdn_lower")
        tinv = jnp.transpose(_tri_inverse(jnp.transpose(lower, (1, 2, 0)), name="gdn_tri_inverse"), (2, 0, 1))
        og, ss = _gdn_scan_fwd(qn, kn, vn, aux, tinv, name="gdn_scan_fwd")
        yb = _gdn_post_fwd(og, proj, gng[l], name="gdn_post_fwd")
        pa = _mm(ya, wa[l], name="branch_a_mm")
        pb = _mm(yb, wb[l], name="branch_b_mm")
        merged = _merge_fwd(proj, pa, pb, name="merge_fwd")
        t1 = _mm(merged, wout[l], name="out_mm")
        x2, h2 = _adaln_fwd(xin, n2g[l], sc2, sh2, t1, gt1, name="adaln2_fwd")
        gu = _mm(h2, wfi[l], name="ffn_in_mm")
        act = _swiglu_fwd(gu, name="swiglu_fwd")
        t2 = _mm(act, wfo[l], name="ffn_out_mm", tk=1408)
        saved.append(dict(xin=xin, h1=h1, proj=proj, kpad=kpad, vpad=vpad, bias=bias, bias_vjp=bias_vjp, ya=ya, lse=lse,
                          par=par, qn=qn, kn=kn, vn=vn, aux=aux, tinv=tinv, ss=ss, og=og, yb=yb, pa=pa, pb=pb,
                          merged=merged, t1=t1, x2=x2, h2=h2, gu=gu, act=act, t2=t2))
        if l + 1 < depth:
            xin, h1 = _adaln_fwd(x2, n1g[l + 1], mods[l + 1, 1], mods[l + 1, 0], t2, gt2, name="adaln1_fwd")

    s = saved[-1]
    dx, dt2, st = _loss_head(s["x2"], s["t2"], mods[depth - 1, 5], fg, tgt, name="loss_head")
    loss_part = st[4, 0]
    small_g = {"final_g": st[0]}
    dmod_rows = [None] * depth
    g_full = {k: [None] * depth for k in ("w_in", "w_branch_a", "w_branch_b", "w_out", "w_ffn_in", "w_ffn_out", "w_conv")}
    for n in ("norm1_g", "norm2_g", "rel_table", "a_log", "dt_bias", "gdn_norm_g"):
        small_g[n] = [None] * depth
    dgt2 = st[3]
    for l in reversed(range(depth)):
        s = saved[l]
        sh1, sc1, gt1, sh2, sc2, gt2 = (mods[l, i] for i in range(6))
        g_full["w_ffn_out"][l] = _mm(s["act"], dt2, ta=True, out_dtype=BF16, name="ffn_out_dw", tm=1408)
        dact = _mm(dt2, wfo[l], tb=True, name="ffn_out_dx", tn=1408)
        dgu = _swiglu_bwd(s["gu"], dact, name="swiglu_bwd")
        g_full["w_ffn_in"][l] = _mm(s["h2"], dgu, ta=True, out_dtype=BF16, name="ffn_in_dw")
        dh2 = _mm(dgu, wfi[l], tb=True, name="ffn_in_dx")
        dx, dt1, st2 = _adaln_bwd(s["x2"], n2g[l], sc2, sh2, dh2, dx, s["t1"], gt1, name="adaln2_bwd")
        g_full["w_out"][l] = _mm(s["merged"], dt1, ta=True, out_dtype=BF16, name="out_dw")
        dmerged = _mm(dt1, wout[l], tb=True, name="out_dx")
        dgates, dpa, dpb = _merge_bwd(s["proj"], s["pa"], s["pb"], dmerged, name="merge_bwd")
        g_full["w_branch_a"][l] = _mm(s["ya"], dpa, ta=True, out_dtype=BF16, name="branch_a_dw")
        g_full["w_branch_b"][l] = _mm(s["yb"], dpb, ta=True, out_dtype=BF16, name="branch_b_dw")
        dya = _mm(dpa, wa[l], tb=True, name="branch_a_dx")
        dyb = _mm(dpb, wb[l], tb=True, name="branch_b_dx")
        dqa, dkpad, dvpad, dbias = _attn_bwd(s["proj"], s["kpad"], s["vpad"], s["bias"], s["ya"], s["lse"], dya,
                                             name="attn_bwd")
        small_g["rel_table"][l] = s["bias_vjp"](dbias)[0]
        dog, dz, dgn = _gdn_post_bwd(s["og"], s["proj"], gng[l], dyb, name="gdn_post_bwd")
        small_g["gdn_norm_g"][l] = dgn[0]
        dqn, dkn, dvn, daux = _gdn_scan_bwd(s["qn"], s["kn"], s["vn"], s["aux"], s["tinv"], s["ss"], dog,
                                            name="gdn_scan_bwd")
        dqkv, dba, dwc, dpar = _gdn_pre_bwd(s["proj"], wconv[l], s["par"], dqn, dkn, dvn, daux, name="gdn_pre_bwd")
        g_full["w_conv"][l] = dwc[0:CONV_K]
        small_g["a_log"][l] = dpar[0, B_HEADS:2 * B_HEADS]
        small_g["dt_bias"][l] = dpar[1, B_HEADS:2 * B_HEADS]
        dproj = jnp.concatenate([dgates, dqa.astype(BF16), dkpad[A_PAST * CH:].astype(BF16),
                                 dvpad[A_PAST * CH:].astype(BF16), dqkv, dz, dba], axis=1)
        g_full["w_in"][l] = _mm(s["h1"], dproj, ta=True, out_dtype=BF16, name="proj_dw", tn=640)
        dh1 = _mm(dproj, win[l], tb=True, name="proj_dx", tk=1152)
        if l > 0:
            p = saved[l - 1]
            dx, dt2, st1 = _adaln_bwd(s["xin"], n1g[l], sc1, sh1, dh1, dx, p["t2"], mods[l - 1, 5], name="adaln1_bwd")
        else:
            dx, st1 = _adaln_bwd(s["xin"], n1g[l], sc1, sh1, dh1, dx, name="adaln1_bwd_first")
        small_g["norm1_g"][l], small_g["norm2_g"][l] = st1[0], st2[0]
        dmod_rows[l] = jnp.concatenate([st1[2], st1[1], st2[3], st2[2], st2[1], dgt2])
        if l > 0:
            dgt2 = st1[3]
    grad_x = dx[None]

    small_local = {n: (jnp.stack(vs) if isinstance(vs, list) else vs) for n, vs in small_g.items()}
    small_local["b_ada"] = jnp.stack(dmod_rows)
    (g_small,) = _all_gather([_pack_small(small_local, loss_part, depth)], name="gather_small")
    wsm = _pack_small(dict(b_ada=b_ada, norm1_g=norm1_g, norm2_g=norm2_g, rel_table=rel_table, a_log=a_log,
                           dt_bias=dt_bias, gdn_norm_g=gdn_norm_g, final_g=final_g), jnp.zeros((1,), F32), depth)
    msm = _pack_small(dict(b_ada=m_b_ada, norm1_g=m_norm1_g, norm2_g=m_norm2_g, rel_table=m_rel_table, a_log=m_a_log,
                           dt_bias=m_dt_bias, gdn_norm_g=m_gdn_norm_g, final_g=m_final_g), jnp.zeros((1,), F32), depth)
    vsm = _pack_small(dict(b_ada=v_b_ada, norm1_g=v_norm1_g, norm2_g=v_norm2_g, rel_table=v_rel_table, a_log=v_a_log,
                           dt_bias=v_dt_bias, gdn_norm_g=v_gdn_norm_g, final_g=v_final_g), jnp.ones((1,), F32), depth)
    sm = [_unpack_small(t, depth) for t in _adamw(g_small, wsm, msm, vsm, name="adamw_small")]
    loss = sm[0][1]

    dmod_all = g_small.reshape(N_DEV, -1)[:, :depth * 6 * D].reshape(N_DEV, depth, 6 * D)
    dmod_mine = lax.dynamic_slice_in_dim(dmod_all, me * (6 * D // N_DEV), 6 * D // N_DEV, axis=2)
    g_ada = jnp.stack([_mm(cond, _pad_rows(dmod_mine[:, l], 16), ta=True, name="ada_dw") for l in range(depth)])

    st_in = _cols_to_slabs(_win_from_mine(jnp.stack(g_full["w_in"])))
    st_a, st_b = _cols_to_slabs(jnp.stack(g_full["w_branch_a"])), _cols_to_slabs(jnp.stack(g_full["w_branch_b"]))
    st_out, st_fo = _rows_to_slabs(jnp.stack(g_full["w_out"])), _rows_to_slabs(jnp.stack(g_full["w_ffn_out"]))
    st_fi = _cols_to_slabs(jnp.stack(g_full["w_ffn_in"]))
    st_conv = _cols_to_slabs(jnp.stack(g_full["w_conv"]))
    r_in, r_a, r_b, r_out, r_fi, r_fo, r_conv = _all_to_all([st_in, st_a, st_b, st_out, st_fi, st_fo, st_conv],
                                                            name="scatter_grads")

    def upd(parts, w, m, v, name):
        shp = w.shape
        two = lambda a: a.reshape(-1, shp[-1])
        return [o.reshape(shp) for o in _adamw(parts.reshape(parts.shape[0], -1, shp[-1]), two(w), two(m), two(v), name=name)]

    res = {
        "w_ada": upd(g_ada[None], w_ada, m_w_ada, v_w_ada, "adamw_w_ada"),
        "w_in": upd(r_in, w_in, m_w_in, v_w_in, "adamw_w_in"),
        "w_conv": upd(r_conv, w_conv, m_w_conv, v_w_conv, "adamw_w_conv"),
        "w_branch_a": upd(r_a, w_branch_a, m_w_branch_a, v_w_branch_a, "adamw_w_branch_a"),
        "w_branch_b": upd(r_b, w_branch_b, m_w_branch_b, v_w_branch_b, "adamw_w_branch_b"),
        "w_out": upd(r_out, w_out, m_w_out, v_w_out, "adamw_w_out"),
        "w_ffn_in": upd(r_fi, w_ffn_in, m_w_ffn_in, v_w_ffn_in, "adamw_w_ffn_in"),
        "w_ffn_out": upd(r_fo, w_ffn_out, m_w_ffn_out, v_w_ffn_out, "adamw_w_ffn_out"),
    }
    for n, _ in _small_spec(depth):
        res[n] = [sm[i][0][n] for i in range(4)]
    order = ("w_ada", "b_ada", "norm1_g", "norm2_g", "w_in", "rel_table", "w_conv", "a_log", "dt_bias", "gdn_norm_g",
             "w_branch_a", "w_branch_b", "w_out", "w_ffn_in", "w_ffn_out", "final_g")
    return (loss, grad_x, *[res[n][0] for n in order], *[res[n][1] for n in order],
            *[res[n][2] for n in order], *[res[n][3] for n in order])
```

```python
import functools
import math

import jax
import jax.numpy as jnp
from jax import lax
from jax.experimental import pallas as pl
from jax.experimental.pallas import tpu as pltpu

F32 = jnp.float32
BF16 = jnp.bfloat16
HI = lax.Precision.HIGHEST

N_DEV = 8
D = 1024
DEPTH = 4
CH = 64
EPS = 1e-6
A_HEADS, A_DH = 8, 64
A_W = A_HEADS * A_DH
A_PAST = 8
A_MAX_REL = 128
QB = 256
KB = QB + A_PAST * CH
B_HEADS, B_DH = 4, 128
B_W = B_HEADS * B_DH
CONV_K = 4
FF = 2816
IN_DIM = 5640
IN_PAD = 5760
LANE = 128
NEG = -1e30
VMEM_LIMIT = 48 * 1024 * 1024

ADAM_LR, ADAM_B1, ADAM_B2, ADAM_EPS, ADAM_WD, ADAM_STEP = 0.001, 0.9, 0.999, 1e-08, 0.01, 10

OFF_GA, OFF_GB, OFF_QA, OFF_KA, OFF_VA, OFF_QB, OFF_KB, OFF_VB, OFF_ZB, OFF_BA = (
    0, 1024, 2048, 2560, 3072, 3584, 4096, 4608, 5120, 5632)


def _cp(sem=None):
    return pltpu.CompilerParams(dimension_semantics=sem, vmem_limit_bytes=VMEM_LIMIT)


def _tile(n, pref):
    if n <= pref:
        return n
    best = None
    for t in range(LANE, pref + 1, LANE):
        if n % t == 0:
            best = t
    assert best is not None, (n, pref)
    return best


def _sigmoid(x):
    return 1.0 / (1.0 + jnp.exp(-x))


def _silu(x):
    return x * _sigmoid(x)


def _dsilu(x):
    s = _sigmoid(x)
    return s * (1.0 + x * (1.0 - s))


def _dot(a, b, prec=None):
    return jnp.dot(a, b, preferred_element_type=F32, precision=prec)


def _dot_nt(a, b, prec=None):
    return lax.dot_general(a, b, (((1,), (1,)), ((), ())), preferred_element_type=F32, precision=prec)


def _dot_tn(a, b, prec=None):
    return lax.dot_general(a, b, (((0,), (0,)), ((), ())), preferred_element_type=F32, precision=prec)


def _mm(a, b, *, ta=False, tb=False, out_dtype=F32, name, tm=1024, tn=1024, tk=1024):
    m, k = (a.shape[1], a.shape[0]) if ta else a.shape
    n = b.shape[0] if tb else b.shape[1]
    assert k == (b.shape[1] if tb else b.shape[0]), (a.shape, b.shape, ta, tb)
    tm, tn, tk = _tile(m, tm), _tile(n, tn), _tile(k, tk)
    nk = k // tk
    dn = (((0 if ta else 1,), (1 if tb else 0,)), ((), ()))

    def body(a_ref, b_ref, o_ref, acc_ref):
        kk = pl.program_id(2)

        @pl.when(kk == 0)
        def _():
            acc_ref[...] = jnp.zeros_like(acc_ref)

        acc_ref[...] += lax.dot_general(a_ref[...].astype(BF16), b_ref[...].astype(BF16), dn,
                                        preferred_element_type=F32)

        @pl.when(kk == nk - 1)
        def _():
            o_ref[...] = acc_ref[...].astype(out_dtype)

    a_spec = pl.BlockSpec((tk, tm), lambda i, j, q: (q, i)) if ta else pl.BlockSpec((tm, tk), lambda i, j, q: (i, q))
    b_spec = pl.BlockSpec((tn, tk), lambda i, j, q: (j, q)) if tb else pl.BlockSpec((tk, tn), lambda i, j, q: (q, j))
    return pl.pallas_call(
        body, grid=(m // tm, n // tn, nk), in_specs=[a_spec, b_spec],
        out_specs=pl.BlockSpec((tm, tn), lambda i, j, q: (i, j)),
        out_shape=jax.ShapeDtypeStruct((m, n), out_dtype),
        scratch_shapes=[pltpu.VMEM((tm, tn), F32)],
        compiler_params=_cp(("parallel", "parallel", "arbitrary")), name=name)(a, b)


def _rb(tr, width, cb=0):
    return pl.BlockSpec((tr, width), lambda i: (i, cb))


def _whole(shape):
    nd = len(shape)
    return pl.BlockSpec(shape, lambda i: (0,) * nd)


def _colsum(v):
    return jnp.sum(v, axis=0, keepdims=True)


def _adaln_fwd(x, g, sc, sh, t=None, gt=None, *, name, tr=256):
    tt = x.shape[0]
    res = t is not None

    def body(*refs):
        if res:
            x_ref, t_ref, gt_ref, g_ref, sc_ref, sh_ref, xo_ref, h_ref = refs
            xv = x_ref[...] + gt_ref[...] * t_ref[...]
            xo_ref[...] = xv
        else:
            x_ref, g_ref, sc_ref, sh_ref, h_ref = refs
            xv = x_ref[...]
        r = lax.rsqrt(jnp.mean(xv * xv, axis=-1, keepdims=True) + EPS)
        h_ref[...] = ((xv * r * g_ref[...]) * (1.0 + sc_ref[...]) + sh_ref[...]).astype(BF16)

    row, vec = _rb(tr, D), _whole((1, D))
    if res:
        ins, in_specs = (x, t, gt, g, sc, sh), [row, row, vec, vec, vec, vec]
        out_shape = (jax.ShapeDtypeStruct((tt, D), F32), jax.ShapeDtypeStruct((tt, D), BF16))
        out_specs = (row, row)
    else:
        ins, in_specs = (x, g, sc, sh), [row, vec, vec, vec]
        out_shape, out_specs = jax.ShapeDtypeStruct((tt, D), BF16), row
    out = pl.pallas_call(body, grid=(tt // tr,), in_specs=in_specs, out_specs=out_specs, out_shape=out_shape,
                         compiler_params=_cp(("parallel",)), name=name)(*ins)
    return out if res else (x, out)


def _adaln_bwd(x, g, sc, sh, dh, dx_in, t=None, gt=None, *, name, tr=256):
    tt = x.shape[0]
    res = t is not None

    def body(*refs):
        if res:
            x_ref, g_ref, sc_ref, sh_ref, dh_ref, dxi_ref, t_ref, gt_ref, dx_ref, dt_ref, st_ref = refs
        else:
            x_ref, g_ref, sc_ref, sh_ref, dh_ref, dxi_ref, dx_ref, st_ref = refs

        @pl.when(pl.program_id(0) == 0)
        def _():
            st_ref[...] = jnp.zeros_like(st_ref)

        xv, dh = x_ref[...], dh_ref[...]
        r = lax.rsqrt(jnp.mean(xv * xv, axis=-1, keepdims=True) + EPS)
        nrm = xv * r
        y = nrm * g_ref[...]
        dy = dh * (1.0 + sc_ref[...])
        dn = dy * g_ref[...]
        dx = dxi_ref[...] + r * (dn - nrm * jnp.mean(dn * nrm, axis=-1, keepdims=True))
        dx_ref[...] = dx
        st_ref[0:1, :] += _colsum(dy * nrm)
        st_ref[1:2, :] += _colsum(dh * y)
        st_ref[2:3, :] += _colsum(dh)
        if res:
            dt_ref[...] = (gt_ref[...] * dx).astype(BF16)
            st_ref[3:4, :] += _colsum(dx * t_ref[...])

    row, vec, st = _rb(tr, D), _whole((1, D)), _whole((8, D))
    ins, in_specs = [x, g, sc, sh, dh, dx_in], [row, vec, vec, vec, row, row]
    out_shape, out_specs = [jax.ShapeDtypeStruct((tt, D), F32)], [row]
    if res:
        ins += [t, gt]
        in_specs += [row, vec]
        out_shape.append(jax.ShapeDtypeStruct((tt, D), BF16))
        out_specs.append(row)
    out_shape.append(jax.ShapeDtypeStruct((8, D), F32))
    out_specs.append(st)
    return pl.pallas_call(body, grid=(tt // tr,), in_specs=in_specs, out_specs=tuple(out_specs),
                          out_shape=tuple(out_shape), compiler_params=_cp(("arbitrary",)), name=name)(*ins)


def _loss_head(x, t, gt, fg, tgt, *, name, tr=256):
    tt = x.shape[0]

    def body(x_ref, t_ref, gt_ref, fg_ref, tgt_ref, dx_ref, dt_ref, st_ref):
        @pl.when(pl.program_id(0) == 0)
        def _():
            st_ref[...] = jnp.zeros_like(st_ref)

        tv = t_ref[...]
        xv = x_ref[...] + gt_ref[...] * tv
        r = lax.rsqrt(jnp.mean(xv * xv, axis=-1, keepdims=True) + EPS)
        nrm = xv * r
        err = nrm * fg_ref[...] - tgt_ref[...]
        st_ref[4:5, :] += 0.5 * jnp.sum(jnp.mean(err * err, axis=-1, keepdims=True), axis=0, keepdims=True)
        dy = err * (1.0 / D)
        dn = dy * fg_ref[...]
        dx = r * (dn - nrm * jnp.mean(dn * nrm, axis=-1, keepdims=True))
        dx_ref[...] = dx
        dt_ref[...] = (gt_ref[...] * dx).astype(BF16)
        st_ref[0:1, :] += _colsum(dy * nrm)
        st_ref[3:4, :] += _colsum(dx * tv)

    row, vec = _rb(tr, D), _whole((1, D))
    return pl.pallas_call(
        body, grid=(tt // tr,), in_specs=[row, row, vec, vec, row], out_specs=(row, row, _whole((8, D))),
        out_shape=(jax.ShapeDtypeStruct((tt, D), F32), jax.ShapeDtypeStruct((tt, D), BF16),
                   jax.ShapeDtypeStruct((8, D), F32)),
        compiler_params=_cp(("arbitrary",)), name=name)(x, t, gt, fg, tgt)


def _merge_fwd(proj, pa, pb, *, name, tr=256):
    tt = pa.shape[0]

    def body(ga_ref, gb_ref, pa_ref, pb_ref, o_ref):
        o_ref[...] = (_sigmoid(ga_ref[...]) * pa_ref[...] + _sigmoid(gb_ref[...]) * pb_ref[...]).astype(BF16)

    row = _rb(tr, D)
    return pl.pallas_call(body, grid=(tt // tr,), in_specs=[_rb(tr, D, 0), _rb(tr, D, 1), row, row], out_specs=row,
                          out_shape=jax.ShapeDtypeStruct((tt, D), BF16), compiler_params=_cp(("parallel",)),
                          name=name)(proj, proj, pa, pb)


def _merge_bwd(proj, pa, pb, dm, *, name, tr=256):
    tt = pa.shape[0]

    def body(ga_ref, gb_ref, pa_ref, pb_ref, dm_ref, dg_ref, dpa_ref, dpb_ref):
        dm_v = dm_ref[...]
        sa, sb = _sigmoid(ga_ref[...]), _sigmoid(gb_ref[...])
        dpa_ref[...] = (dm_v * sa).astype(BF16)
        dpb_ref[...] = (dm_v * sb).astype(BF16)
        dg_ref[:, 0:D] = (dm_v * pa_ref[...] * sa * (1.0 - sa)).astype(BF16)
        dg_ref[:, D:2 * D] = (dm_v * pb_ref[...] * sb * (1.0 - sb)).astype(BF16)

    row = _rb(tr, D)
    return pl.pallas_call(
        body, grid=(tt // tr,), in_specs=[_rb(tr, D, 0), _rb(tr, D, 1), row, row, row],
        out_specs=(_rb(tr, 2 * D), row, row),
        out_shape=(jax.ShapeDtypeStruct((tt, 2 * D), BF16), jax.ShapeDtypeStruct((tt, D), BF16),
                   jax.ShapeDtypeStruct((tt, D), BF16)),
        compiler_params=_cp(("parallel",)), name=name)(proj, proj, pa, pb, dm)


def _swiglu_fwd(gu, *, name, tr=256):
    tt = gu.shape[0]

    def body(g_ref, u_ref, o_ref):
        o_ref[...] = (_silu(g_ref[...]) * u_ref[...]).astype(BF16)

    return pl.pallas_call(body, grid=(tt // tr,), in_specs=[_rb(tr, FF, 0), _rb(tr, FF, 1)], out_specs=_rb(tr, FF),
                          out_shape=jax.ShapeDtypeStruct((tt, FF), BF16), compiler_params=_cp(("parallel",)),
                          name=name)(gu, gu)


def _swiglu_bwd(gu, dact, *, name, tr=256):
    tt = gu.shape[0]

    def body(g_ref, u_ref, da_ref, o_ref):
        gv, da = g_ref[...], da_ref[...]
        o_ref[:, 0:FF] = (da * u_ref[...] * _dsilu(gv)).astype(BF16)
        o_ref[:, FF:2 * FF] = (da * _silu(gv)).astype(BF16)

    return pl.pallas_call(body, grid=(tt // tr,), in_specs=[_rb(tr, FF, 0), _rb(tr, FF, 1), _rb(tr, FF)],
                          out_specs=_rb(tr, 2 * FF), out_shape=jax.ShapeDtypeStruct((tt, 2 * FF), BF16),
                          compiler_params=_cp(("parallel",)), name=name)(gu, gu, dact)


def _bias_from_table(table):
    lw = 1152
    n_hi = KB - A_MAX_REL
    w = jnp.concatenate([
        jnp.broadcast_to(table[:, 2 * A_MAX_REL:], (A_HEADS, n_hi)),
        jnp.flip(table[:, 1:2 * A_MAX_REL], axis=1),
        jnp.broadcast_to(table[:, 0:1], (A_HEADS, lw - n_hi - (2 * A_MAX_REL - 1)))], axis=1)
    flat = jnp.broadcast_to(w[:, None, :], (A_HEADS, QB, lw)).reshape(A_HEADS, QB * lw)
    skew = flat[:, :QB * (lw - 1)].reshape(A_HEADS, QB, lw - 1)
    bias = skew[:, :, QB - 1:QB - 1 + KB]
    qc = jnp.arange(QB)[:, None] // CH + A_PAST
    kc = jnp.arange(KB)[None, :] // CH
    inband = (kc <= qc) & (kc >= qc - A_PAST)
    return jnp.where(inband[None], bias, NEG)


def _attn_fwd(proj, kpad, vpad, bias, *, name):
    tt = proj.shape[0]

    def body(q_ref, k_ref, v_ref, b_ref, o_ref, l_ref):
        q0 = pl.multiple_of(pl.program_id(1) * QB, QB)
        q = q_ref[...]
        k = k_ref[pl.ds(q0, KB), :]
        v = v_ref[pl.ds(q0, KB), :]
        lane = lax.broadcasted_iota(jnp.int32, (QB, LANE), 1)
        valid = (lax.broadcasted_iota(jnp.int32, (QB, KB), 1) + q0) >= A_PAST * CH
        o = jnp.zeros((QB, LANE), F32)
        lse = jnp.zeros((QB, LANE), F32)
        for a in range(2):
            hm = (lane >= A_DH * a) & (lane < A_DH * (a + 1))
            s = _dot_nt(jnp.where(hm, q, 0.0).astype(BF16), k) * (A_DH ** -0.5) + b_ref[a]
            s = jnp.where(valid, s, NEG)
            m = jnp.max(s, axis=-1, keepdims=True)
            p = jnp.exp(s - m)
            l = jnp.sum(p, axis=-1, keepdims=True)
            o = jnp.where(hm, _dot((p / l).astype(BF16), v), o)
            lse = jnp.where(hm, m + jnp.log(l), lse)
        o_ref[...] = o.astype(BF16)
        l_ref[...] = lse

    kv = pl.BlockSpec((tt + A_PAST * CH, LANE), lambda h, i: (0, h))
    blk = pl.BlockSpec((QB, LANE), lambda h, i: (i, h))
    return pl.pallas_call(
        body, grid=(A_W // LANE, tt // QB),
        in_specs=[pl.BlockSpec((QB, LANE), lambda h, i: (i, OFF_QA // LANE + h)), kv, kv,
                  pl.BlockSpec((2, QB, KB), lambda h, i: (h, 0, 0))],
        out_specs=(blk, blk),
        out_shape=(jax.ShapeDtypeStruct((tt, A_W), BF16), jax.ShapeDtypeStruct((tt, A_W), F32)),
        compiler_params=_cp(("parallel", "parallel")), name=name)(proj, kpad, vpad, bias)


def _attn_bwd(proj, kpad, vpad, bias, o, lse, do, *, name):
    tt = proj.shape[0]

    def body(q_ref, k_ref, v_ref, b_ref, o_ref, l_ref, do_ref, dq_ref, dk_ref, dv_ref, db_ref):
        @pl.when(pl.program_id(1) == 0)
        def _():
            dk_ref[...] = jnp.zeros_like(dk_ref)
            dv_ref[...] = jnp.zeros_like(dv_ref)
            db_ref[...] = jnp.zeros_like(db_ref)

        q0 = pl.multiple_of(pl.program_id(1) * QB, QB)
        q, do_v, lse = q_ref[...], do_ref[...], l_ref[...]
        k = k_ref[pl.ds(q0, KB), :]
        v = v_ref[pl.ds(q0, KB), :]
        dsum = do_v * o_ref[...].astype(F32)
        lane = lax.broadcasted_iota(jnp.int32, (QB, LANE), 1)
        valid = (lax.broadcasted_iota(jnp.int32, (QB, KB), 1) + q0) >= A_PAST * CH
        dq = jnp.zeros((QB, LANE), F32)
        dk = jnp.zeros((KB, LANE), F32)
        dv = jnp.zeros((KB, LANE), F32)
        for a in range(2):
            hm = (lane >= A_DH * a) & (lane < A_DH * (a + 1))
            qa = jnp.where(hm, q, 0.0).astype(BF16)
            doa = jnp.where(hm, do_v, 0.0).astype(BF16)
            s = _dot_nt(qa, k) * (A_DH ** -0.5) + b_ref[a]
            s = jnp.where(valid, s, NEG)
            lse_a = jnp.max(jnp.where(hm, lse, NEG), axis=-1, keepdims=True)
            p = jnp.exp(s - lse_a)
            dp = _dot_nt(doa, v)
            dsum_a = jnp.sum(jnp.where(hm, dsum, 0.0), axis=-1, keepdims=True)
            ds = p * (dp - dsum_a)
            db_ref[a] += ds
            dsb = (ds * (A_DH ** -0.5)).astype(BF16)
            dq = jnp.where(hm, _dot(dsb, k), dq)
            dk += _dot_tn(dsb, qa)
            dv += _dot_tn(p.astype(BF16), doa)
        dq_ref[...] = dq
        dk_ref[pl.ds(q0, KB), :] += dk
        dv_ref[pl.ds(q0, KB), :] += dv

    kv = pl.BlockSpec((tt + A_PAST * CH, LANE), lambda h, i: (0, h))
    blk = pl.BlockSpec((QB, LANE), lambda h, i: (i, h))
    bsp = pl.BlockSpec((2, QB, KB), lambda h, i: (h, 0, 0))
    pad_shape = jax.ShapeDtypeStruct((tt + A_PAST * CH, A_W), F32)
    return pl.pallas_call(
        body, grid=(A_W // LANE, tt // QB),
        in_specs=[pl.BlockSpec((QB, LANE), lambda h, i: (i, OFF_QA // LANE + h)), kv, kv, bsp, blk, blk, blk],
        out_specs=(blk, kv, kv, bsp),
        out_shape=(jax.ShapeDtypeStruct((tt, A_W), F32), pad_shape, pad_shape,
                   jax.ShapeDtypeStruct((A_HEADS, QB, KB), F32)),
        compiler_params=_cp(("parallel", "arbitrary")), name=name)(proj, kpad, vpad, bias, o, lse, do)


GTR = 256


def _taps(w_ref, grp):
    return [w_ref[j:j + 1, grp * B_W:(grp + 1) * B_W] for j in range(CONV_K)]


def _shifts(xe, rows):
    return [xe[8:8 + rows]] + [pltpu.roll(xe, s, 0)[8:8 + rows] for s in range(1, CONV_K)]


def _conv(shifts, taps):
    acc = taps[CONV_K - 1] * shifts[0]
    for s in range(1, CONV_K):
        acc = acc + taps[CONV_K - 1 - s] * shifts[s]
    return acc


def _qk_scale(grp):
    return B_DH ** -0.5 if grp == 0 else 1.0


def _act_fwd(c, grp):
    y = _silu(c)
    if grp == 2:
        return y
    parts = []
    for hd in range(B_HEADS):
        yh = y[:, hd * B_DH:(hd + 1) * B_DH]
        parts.append(yh * (lax.rsqrt(jnp.sum(yh * yh, axis=-1, keepdims=True) + EPS) * _qk_scale(grp)))
    return jnp.concatenate(parts, axis=1)


def _act_bwd(c, dy, grp):
    if grp == 2:
        return dy * _dsilu(c)
    y = _silu(c)
    parts = []
    for hd in range(B_HEADS):
        yh = y[:, hd * B_DH:(hd + 1) * B_DH]
        r = lax.rsqrt(jnp.sum(yh * yh, axis=-1, keepdims=True) + EPS)
        dyh = dy[:, hd * B_DH:(hd + 1) * B_DH] * _qk_scale(grp)
        parts.append(r * dyh - yh * (r * r * r) * jnp.sum(dyh * yh, axis=-1, keepdims=True))
    return jnp.concatenate(parts, axis=1) * _dsilu(c)


def _chunk_tri(n, upper=False):
    r = lax.broadcasted_iota(jnp.int32, (n, n), 0)
    c = lax.broadcasted_iota(jnp.int32, (n, n), 1)
    same = (r // CH) == (c // CH)
    return jnp.where(same & ((r <= c) if upper else (r >= c)), 1.0, 0.0).astype(F32)


def _gate_rows(ba, par_ref):
    lane = lax.broadcasted_iota(jnp.int32, ba.shape, 1)
    z = ba + par_ref[1:2, :]
    sp = jnp.maximum(z, 0.0) + jnp.log(1.0 + jnp.exp(-jnp.abs(z)))
    g = -jnp.exp(par_ref[0:1, :]) * sp
    return jnp.where(lane < B_HEADS, _sigmoid(ba), jnp.where(lane < 2 * B_HEADS, g, 0.0)), z


def _prev8(cb):
    return pl.BlockSpec((8, B_W), lambda i: (jnp.maximum(i * (GTR // 8) - 1, 0), cb))


def _next8(cb, nb):
    return pl.BlockSpec((8, B_W), lambda i: (jnp.minimum((i + 1) * (GTR // 8), nb * (GTR // 8) - 1), cb))


def _gdn_pre_fwd(proj, wconv, par, *, name):
    tt = proj.shape[0]

    def body(q_ref, k_ref, v_ref, qh_ref, kh_ref, vh_ref, ba_ref, w_ref, par_ref, qo_ref, ko_ref, vo_ref, aux_ref):
        first = pl.program_id(0) == 0
        for grp, (x_ref, h_ref, o_ref) in enumerate(((q_ref, qh_ref, qo_ref), (k_ref, kh_ref, ko_ref),
                                                     (v_ref, vh_ref, vo_ref))):
            xe = jnp.concatenate([jnp.where(first, 0.0, h_ref[...]), x_ref[...]], axis=0)
            o_ref[...] = _act_fwd(_conv(_shifts(xe, GTR), _taps(w_ref, grp)), grp)
        bg, _ = _gate_rows(ba_ref[...], par_ref)
        lane = lax.broadcasted_iota(jnp.int32, bg.shape, 1)
        aux_ref[...] = jnp.where(lane < B_HEADS, bg, _dot(_chunk_tri(GTR), bg, HI))

    col = lambda off: _rb(GTR, B_W, off // B_W)
    outs = jax.ShapeDtypeStruct((tt, B_W), F32)
    return pl.pallas_call(
        body, grid=(tt // GTR,),
        in_specs=[col(OFF_QB), col(OFF_KB), col(OFF_VB), _prev8(OFF_QB // B_W), _prev8(OFF_KB // B_W),
                  _prev8(OFF_VB // B_W), _rb(GTR, LANE, OFF_BA // LANE), _whole((CONV_K, 3 * B_W)),
                  _whole((8, LANE))],
        out_specs=(_rb(GTR, B_W), _rb(GTR, B_W), _rb(GTR, B_W), _rb(GTR, LANE)),
        out_shape=(outs, outs, outs, jax.ShapeDtypeStruct((tt, LANE), F32)),
        compiler_params=_cp(("parallel",)), name=name)(proj, proj, proj, proj, proj, proj, proj, wconv, par)


def _gdn_pre_bwd(proj, wconv, par, dq, dk, dv, daux, *, name):
    tt = proj.shape[0]
    nb = tt // GTR

    def body(q_ref, k_ref, v_ref, qh_ref, kh_ref, vh_ref, qn_ref, kn_ref, vn_ref, ba_ref, w_ref, par_ref,
             dq_ref, dk_ref, dv_ref, dqn_ref, dkn_ref, dvn_ref, daux_ref, dx_ref, dba_ref, dw_ref, dpar_ref):
        i = pl.program_id(0)
        first, last = i == 0, i == nb - 1

        @pl.when(first)
        def _():
            dw_ref[...] = jnp.zeros_like(dw_ref)
            dpar_ref[...] = jnp.zeros_like(dpar_ref)

        groups = ((q_ref, qh_ref, qn_ref, dq_ref, dqn_ref), (k_ref, kh_ref, kn_ref, dk_ref, dkn_ref),
                  (v_ref, vh_ref, vn_ref, dv_ref, dvn_ref))
        for grp, (x_ref, h_ref, xn_ref, d_ref, dn_ref) in enumerate(groups):
            taps = _taps(w_ref, grp)
            xe = jnp.concatenate([jnp.where(first, 0.0, h_ref[...]), x_ref[...]], axis=0)
            sh = _shifts(xe, GTR)
            dc = _act_bwd(_conv(sh, taps), d_ref[...], grp)
            xe_n = jnp.concatenate([x_ref[GTR - 8:GTR, :], xn_ref[...]], axis=0)
            dcn = _act_bwd(_conv(_shifts(xe_n, 8), taps), dn_ref[...], grp)
            dce = jnp.concatenate([dc, jnp.where(last, 0.0, dcn)], axis=0)
            dx = taps[CONV_K - 1] * dc
            dw_ref[CONV_K - 1:CONV_K, grp * B_W:(grp + 1) * B_W] += _colsum(dc * sh[0])
            for s in range(1, CONV_K):
                dx = dx + taps[CONV_K - 1 - s] * pltpu.roll(dce, GTR + 8 - s, 0)[0:GTR]
                dw_ref[CONV_K - 1 - s:CONV_K - s, grp * B_W:(grp + 1) * B_W] += _colsum(dc * sh[s])
            dx_ref[:, grp * B_W:(grp + 1) * B_W] = dx.astype(BF16)
        ba = ba_ref[...]
        lane = lax.broadcasted_iota(jnp.int32, ba.shape, 1)
        bg, z = _gate_rows(ba, par_ref)
        daux_v = daux_ref[...]
        dg = _dot(_chunk_tri(GTR, upper=True), daux_v, HI)
        dgl = jnp.where((lane >= B_HEADS) & (lane < 2 * B_HEADS), dg, 0.0)
        da = dgl * (-jnp.exp(par_ref[0:1, :])) * _sigmoid(z)
        dbr = jnp.where(lane < B_HEADS, daux_v * bg * (1.0 - bg), 0.0)
        dba_ref[...] = (dbr + da).astype(BF16)
        dpar_ref[0:1, :] += _colsum(dgl * bg)
        dpar_ref[1:2, :] += _colsum(da)

    col = lambda off: _rb(GTR, B_W, off // B_W)
    row, rowl = _rb(GTR, B_W), _rb(GTR, LANE)
    return pl.pallas_call(
        body, grid=(nb,),
        in_specs=[col(OFF_QB), col(OFF_KB), col(OFF_VB),
                  _prev8(OFF_QB // B_W), _prev8(OFF_KB // B_W), _prev8(OFF_VB // B_W),
                  _next8(OFF_QB // B_W, nb), _next8(OFF_KB // B_W, nb), _next8(OFF_VB // B_W, nb),
                  _rb(GTR, LANE, OFF_BA // LANE), _whole((CONV_K, 3 * B_W)), _whole((8, LANE)),
                  row, row, row, _next8(0, nb), _next8(0, nb), _next8(0, nb), rowl],
        out_specs=(_rb(GTR, 3 * B_W), rowl, _whole((8, 3 * B_W)), _whole((8, LANE))),
        out_shape=(jax.ShapeDtypeStruct((tt, 3 * B_W), BF16), jax.ShapeDtypeStruct((tt, LANE), BF16),
                   jax.ShapeDtypeStruct((8, 3 * B_W), F32), jax.ShapeDtypeStruct((8, LANE), F32)),
        compiler_params=_cp(("arbitrary",)), name=name)(
            proj, proj, proj, proj, proj, proj, proj, proj, proj, proj, wconv, par, dq, dk, dv, dq, dk, dv, daux)


def _col(x, j):
    lane = lax.broadcasted_iota(jnp.int32, x.shape, 1)
    return jnp.sum(jnp.where(lane == j, x, 0.0), axis=-1, keepdims=True)


def _as_row(x, j):
    lane = lax.broadcasted_iota(jnp.int32, (CH, LANE), 1)
    return _dot_nt(jnp.where(lane == j, 1.0, 0.0).astype(F32), x, HI)


def _chunk_masks():
    r = lax.broadcasted_iota(jnp.int32, (CH, CH), 0)
    c = lax.broadcasted_iota(jnp.int32, (CH, CH), 1)
    return r > c, r >= c


def _gdn_lower(k, aux, *, name):
    tt = k.shape[0]

    def body(k_ref, aux_ref, l_ref):
        aux_v = aux_ref[...]
        strict, _ = _chunk_masks()
        for hd in range(B_HEADS):
            kh = k_ref[:, hd * B_DH:(hd + 1) * B_DH].astype(BF16)
            diff = _col(aux_v, B_HEADS + hd) - _as_row(aux_v, B_HEADS + hd)
            dec = jnp.exp(jnp.where(strict, diff, NEG))
            l_ref[hd] = _col(aux_v, hd) * _dot_nt(kh, kh) * dec

    return pl.pallas_call(
        body, grid=(tt // CH,), in_specs=[_rb(CH, B_W), _rb(CH, LANE)],
        out_specs=pl.BlockSpec((B_HEADS, CH, CH), lambda i: (i, 0, 0)),
        out_shape=jax.ShapeDtypeStruct((tt // CH * B_HEADS, CH, CH), F32),
        compiler_params=_cp(("parallel",)), name=name)(k, aux)


def _tri_inverse(lt, *, name):
    nb = lt.shape[2]

    def body(l_ref, t_ref):
        rowid = lax.broadcasted_iota(jnp.int32, (CH, nb), 0)

        def outer(i, carry):
            def inner(j, acc):
                return acc + l_ref[i, pl.ds(j, 1), :] * t_ref[j]

            acc = lax.fori_loop(0, i, inner, jnp.zeros((CH, nb), F32))
            t_ref[i] = jnp.where(rowid == i, 1.0, 0.0) - acc
            return carry

        lax.fori_loop(0, CH, outer, 0)

    return pl.pallas_call(body, out_shape=jax.ShapeDtypeStruct(lt.shape, F32),
                          in_specs=[pl.BlockSpec(memory_space=pltpu.VMEM)],
                          out_specs=pl.BlockSpec(memory_space=pltpu.VMEM),
                          compiler_params=_cp(), name=name)(lt)


def _gdn_chunk_terms(qh, kh, vh, aux_v, aux_last, tinv, hd):
    strict, incl = _chunk_masks()
    beta = _col(aux_v, hd)
    gc = _col(aux_v, B_HEADS + hd)
    gl = _col(aux_last, B_HEADS + hd)
    egc = jnp.exp(gc)
    diff = gc - _as_row(aux_v, B_HEADS + hd)
    dec = jnp.exp(jnp.where(incl, diff, NEG))
    kb = kh.astype(BF16)
    kk = _dot_nt(kb, kb)
    qk0 = _dot_nt(qh.astype(BF16), kb)
    rhs_v = vh * beta
    rhs_k = kh * (beta * egc)
    u0 = _dot(tinv, rhs_v, HI)
    w = _dot(tinv, rhs_k, HI)
    return dict(strict=strict, incl=incl, beta=beta, gc=gc, gl=gl, egc=egc, dec=dec, kk=kk, qk0=qk0,
                qk=qk0 * dec, u0=u0, w=w, q_dec=qh * egc, ekd=jnp.exp(gl - gc))


def _gdn_scan_fwd(q, k, v, aux, tinv, *, name):
    tt = q.shape[0]

    def body(q_ref, k_ref, v_ref, aux_ref, t_ref, o_ref, ss_ref, s_ref):
        @pl.when(pl.program_id(0) == 0)
        def _():
            s_ref[...] = jnp.zeros_like(s_ref)

        aux_v = aux_ref[...]
        aux_last = aux_ref[CH - 1:CH, :]
        for hd in range(B_HEADS):
            sl = slice(hd * B_DH, (hd + 1) * B_DH)
            qh, kh, vh = q_ref[:, sl], k_ref[:, sl], v_ref[:, sl]
            c = _gdn_chunk_terms(qh, kh, vh, aux_v, aux_last, t_ref[hd], hd)
            st = s_ref[hd]
            ss_ref[hd] = st
            sb = st.astype(BF16)
            u = c["u0"] - _dot(c["w"].astype(BF16), sb)
            ub = u.astype(BF16)
            o_ref[:, sl] = _dot(c["q_dec"].astype(BF16), sb) + _dot(c["qk"].astype(BF16), ub)
            s_ref[hd] = st * jnp.exp(c["gl"]) + _dot_tn((kh * c["ekd"]).astype(BF16), ub)

    row = _rb(CH, B_W)
    return pl.pallas_call(
        body, grid=(tt // CH,),
        in_specs=[row, row, row, _rb(CH, LANE), pl.BlockSpec((B_HEADS, CH, CH), lambda i: (i, 0, 0))],
        out_specs=(row, pl.BlockSpec((B_HEADS, B_DH, B_DH), lambda i: (i, 0, 0))),
        out_shape=(jax.ShapeDtypeStruct((tt, B_W), F32),
                   jax.ShapeDtypeStruct((tt // CH * B_HEADS, B_DH, B_DH), F32)),
        scratch_shapes=[pltpu.VMEM((B_HEADS, B_DH, B_DH), F32)],
        compiler_params=_cp(("arbitrary",)), name=name)(q, k, v, aux, tinv)


def _gdn_scan_bwd(q, k, v, aux, tinv, ss, do, *, name):
    tt = q.shape[0]
    nc = tt // CH

    def body(q_ref, k_ref, v_ref, aux_ref, t_ref, ss_ref, do_ref, dq_ref, dk_ref, dv_ref, daux_ref, ds_ref):
        @pl.when(pl.program_id(0) == 0)
        def _():
            ds_ref[...] = jnp.zeros_like(ds_ref)

        aux_v = aux_ref[...]
        aux_last = aux_ref[CH - 1:CH, :]
        lane = lax.broadcasted_iota(jnp.int32, (CH, LANE), 1)
        rowi = lax.broadcasted_iota(jnp.int32, (CH, 1), 0)
        daux = jnp.zeros((CH, LANE), F32)
        for hd in range(B_HEADS):
            sl = slice(hd * B_DH, (hd + 1) * B_DH)
            qh, kh, vh, do_h = q_ref[:, sl], k_ref[:, sl], v_ref[:, sl], do_ref[:, sl]
            tinv_h = t_ref[hd]
            c = _gdn_chunk_terms(qh, kh, vh, aux_v, aux_last, tinv_h, hd)
            beta, egc, ekd, dec, w = c["beta"], c["egc"], c["ekd"], c["dec"], c["w"]
            st, dst = ss_ref[hd], ds_ref[hd]
            sb, dsb = st.astype(BF16), dst.astype(BF16)
            eg_last = jnp.exp(c["gl"])
            wb = w.astype(BF16)
            u = c["u0"] - _dot(wb, sb)
            ub = u.astype(BF16)
            kdb = (kh * ekd).astype(BF16)
            qdb = c["q_dec"].astype(BF16)
            qkb = c["qk"].astype(BF16)
            dob = do_h.astype(BF16)
            du = _dot_tn(qkb, dob) + _dot(kdb, dsb)
            dub = du.astype(BF16)
            dq_dec = _dot_nt(dob, sb)
            dqk = jnp.where(c["incl"], _dot_nt(dob, ub), 0.0)
            dk_dec = _dot_nt(ub, dsb)
            dgl = jnp.sum(jnp.sum(st * dst, axis=-1, keepdims=True), axis=0, keepdims=True) * eg_last
            dw = -_dot_nt(dub, sb)
            ds_ref[hd] = _dot_tn(qdb, dob) + eg_last * dst - _dot_tn(wb, dub)
            drv = _dot_tn(tinv_h, du, HI)
            drk = _dot_tn(tinv_h, dw, HI)
            dl = -(_dot_nt(drv.astype(BF16), c["u0"].astype(BF16)) + _dot_nt(drk.astype(BF16), wb))
            dl = jnp.where(c["strict"], dl, 0.0)
            dv_ref[:, sl] = drv * beta
            rk = jnp.sum(drk * kh, axis=-1, keepdims=True)
            dbeta = jnp.sum(drv * vh, axis=-1, keepdims=True) + rk * egc
            dgc = rk * beta * egc
            dk = drk * (beta * egc)
            ldec = dl * dec
            dbeta = dbeta + jnp.sum(ldec * c["kk"], axis=-1, keepdims=True)
            dkk = (ldec * beta).astype(BF16)
            dqk0 = (dqk * dec).astype(BF16)
            ddec = ldec * beta * c["kk"] + dqk * c["qk"]
            kb, qb = kh.astype(BF16), qh.astype(BF16)
            dq = _dot(dqk0, kb) + dq_dec * egc
            dk = dk + _dot_tn(dqk0, qb) + _dot(dkk, kb) + _dot_tn(dkk, kb) + dk_dec * ekd
            dgc = dgc + jnp.sum(ddec, axis=-1, keepdims=True) - _col_from_rowsum(ddec)
            dgc = dgc + jnp.sum(dq_dec * qh, axis=-1, keepdims=True) * egc
            kd = jnp.sum(dk_dec * kh, axis=-1, keepdims=True) * ekd
            dgc = dgc - kd
            dgc = dgc + jnp.where(rowi == CH - 1, jnp.sum(kd, axis=0, keepdims=True) + dgl, 0.0)
            dq_ref[:, sl] = dq
            dk_ref[:, sl] = dk
            daux = daux + jnp.where(lane == hd, dbeta, 0.0) + jnp.where(lane == B_HEADS + hd, dgc, 0.0)
        daux_ref[...] = daux

    rev = lambda width: pl.BlockSpec((CH, width), lambda i: (nc - 1 - i, 0))
    rev3 = lambda a, b: pl.BlockSpec((B_HEADS, a, b), lambda i: (nc - 1 - i, 0, 0))
    outs = jax.ShapeDtypeStruct((tt, B_W), F32)
    return pl.pallas_call(
        body, grid=(nc,),
        in_specs=[rev(B_W), rev(B_W), rev(B_W), rev(LANE), rev3(CH, CH), rev3(B_DH, B_DH), rev(B_W)],
        out_specs=(rev(B_W), rev(B_W), rev(B_W), rev(LANE)),
        out_shape=(outs, outs, outs, jax.ShapeDtypeStruct((tt, LANE), F32)),
        scratch_shapes=[pltpu.VMEM((B_HEADS, B_DH, B_DH), F32)],
        compiler_params=_cp(("arbitrary",)), name=name)(q, k, v, aux, tinv, ss, do)


def _col_from_rowsum(m):
    return _dot_tn(m, jnp.ones((CH, LANE), F32), HI)[:, 0:1]


def _gdn_post_fwd(o, proj, gn, *, name, tr=256):
    tt = o.shape[0]

    def body(o_ref, z_ref, g_ref, y_ref):
        for hd in range(B_HEADS):
            sl = slice(hd * B_DH, (hd + 1) * B_DH)
            oh = o_ref[:, sl]
            r = lax.rsqrt(jnp.mean(oh * oh, axis=-1, keepdims=True) + EPS)
            y_ref[:, sl] = (oh * r * g_ref[...] * _silu(z_ref[:, sl])).astype(BF16)

    return pl.pallas_call(body, grid=(tt // tr,), in_specs=[_rb(tr, B_W), _rb(tr, B_W, OFF_ZB // B_W), _whole((1, B_DH))],
                          out_specs=_rb(tr, B_W), out_shape=jax.ShapeDtypeStruct((tt, B_W), BF16),
                          compiler_params=_cp(("parallel",)), name=name)(o, proj, gn)


def _gdn_post_bwd(o, proj, gn, dy, *, name, tr=256):
    tt = o.shape[0]

    def body(o_ref, z_ref, g_ref, dy_ref, do_ref, dz_ref, dg_ref):
        @pl.when(pl.program_id(0) == 0)
        def _():
            dg_ref[...] = jnp.zeros_like(dg_ref)

        g = g_ref[...]
        for hd in range(B_HEADS):
            sl = slice(hd * B_DH, (hd + 1) * B_DH)
            oh, zh, dyh = o_ref[:, sl], z_ref[:, sl], dy_ref[:, sl]
            r = lax.rsqrt(jnp.mean(oh * oh, axis=-1, keepdims=True) + EPS)
            a = oh * r
            s = _silu(zh)
            da = dyh * g * s
            dg_ref[0:1, :] += _colsum(dyh * a * s)
            dz_ref[:, sl] = (dyh * a * g * _dsilu(zh)).astype(BF16)
            do_ref[:, sl] = r * (da - a * jnp.mean(da * a, axis=-1, keepdims=True))

    return pl.pallas_call(
        body, grid=(tt // tr,), in_specs=[_rb(tr, B_W), _rb(tr, B_W, OFF_ZB // B_W), _whole((1, B_DH)), _rb(tr, B_W)],
        out_specs=(_rb(tr, B_W), _rb(tr, B_W), _whole((8, B_DH))),
        out_shape=(jax.ShapeDtypeStruct((tt, B_W), F32), jax.ShapeDtypeStruct((tt, B_W), BF16),
                   jax.ShapeDtypeStruct((8, B_DH), F32)),
        compiler_params=_cp(("arbitrary",)), name=name)(o, proj, gn, dy)


def _adamw(parts, w, m, v, *, name, tr=256):
    npart, r, c = parts.shape
    tr = max([t for t in range(8, min(r, tr) + 1, 8) if r % t == 0], default=r)
    c1, c2 = 1.0 - ADAM_B1 ** ADAM_STEP, 1.0 - ADAM_B2 ** ADAM_STEP

    def body(p_ref, w_ref, m_ref, v_ref, g_ref, d_ref, mo_ref, vo_ref):
        g = p_ref[0].astype(F32)
        for i in range(1, npart):
            g = g + p_ref[i].astype(F32)
        mn = ADAM_B1 * m_ref[...] + (1.0 - ADAM_B1) * g
        vn = ADAM_B2 * v_ref[...] + (1.0 - ADAM_B2) * (g * g)
        g_ref[...] = g
        mo_ref[...] = mn
        vo_ref[...] = vn
        d_ref[...] = -ADAM_LR * ((mn / c1) / (jnp.sqrt(vn / c2) + ADAM_EPS) + ADAM_WD * w_ref[...])

    row = pl.BlockSpec((tr, c), lambda i: (i, 0))
    out = jax.ShapeDtypeStruct((r, c), F32)
    return pl.pallas_call(body, grid=(r // tr,), in_specs=[pl.BlockSpec((npart, tr, c), lambda i: (0, i, 0)), row, row, row],
                          out_specs=(row, row, row, row), out_shape=(out, out, out, out),
                          compiler_params=_cp(("parallel",)), name=name)(parts, w, m, v)


def _peer(k):
    x, y, c = lax.axis_index("x"), lax.axis_index("y"), lax.axis_index("c")
    return ((1 - x) if k & 4 else x, (1 - y) if k & 2 else y, (1 - c) if k & 1 else c)


def _my_index():
    return 4 * lax.axis_index("x") + 2 * lax.axis_index("y") + lax.axis_index("c")


def _index_of(p):
    return 4 * p[0] + 2 * p[1] + p[2]


def _all_gather(xs, *, name):
    n = len(xs)

    def body(*refs):
        x_refs, o_refs = refs[:n], refs[n:2 * n]
        send, recv, loc = refs[2 * n:]
        me = _my_index()
        copies = []
        for a in range(n):
            cp = pltpu.make_async_copy(x_refs[a], o_refs[a].at[me], loc.at[a])
            cp.start()
            copies.append(cp)
        rdmas = []
        for a in range(n):
            for k in range(1, N_DEV):
                r = pltpu.make_async_remote_copy(
                    src_ref=x_refs[a], dst_ref=o_refs[a].at[me], send_sem=send.at[a, k - 1], recv_sem=recv.at[a, k - 1],
                    device_id=_peer(k), device_id_type=pl.DeviceIdType.MESH)
                r.start()
                rdmas.append(r)
        for a in range(n):
            for k in range(1, N_DEV):
                pltpu.make_async_remote_copy(
                    src_ref=x_refs[a], dst_ref=o_refs[a].at[_index_of(_peer(k))], send_sem=send.at[a, k - 1],
                    recv_sem=recv.at[a, k - 1], device_id=_peer(k), device_id_type=pl.DeviceIdType.MESH).wait_recv()
        for r in rdmas:
            r.wait_send()
        for cp in copies:
            cp.wait()

    any_spec = pl.BlockSpec(memory_space=pl.ANY)
    return pl.pallas_call(
        body, in_specs=[any_spec] * n, out_specs=tuple([any_spec] * n),
        out_shape=tuple(jax.ShapeDtypeStruct((N_DEV,) + x.shape, x.dtype) for x in xs),
        scratch_shapes=[pltpu.SemaphoreType.DMA((n, N_DEV - 1)), pltpu.SemaphoreType.DMA((n, N_DEV - 1)),
                        pltpu.SemaphoreType.DMA((n,))],
        name=name)(*xs)


_HBM = pl.BlockSpec(memory_space=pltpu.HBM)
_SEM = pl.BlockSpec(memory_space=pltpu.SEMAPHORE)
_EFFECT = pltpu.SideEffectType.DATAFLOW_SIDE_EFFECTING


def _split_copy(src_ref, land_ref, send, recv, a, k, scatter, slot, sending):
    me, peer = _my_index(), _index_of(_peer(k))
    src = src_ref.at[peer if sending else me] if scatter else src_ref
    land = land_ref.at[me if sending else peer]
    if slot is not None:
        land = land.at[slot]
    sem = a * (N_DEV - 1) + k - 1
    return pltpu.make_async_remote_copy(src_ref=src, dst_ref=land, send_sem=send.at[sem], recv_sem=recv.at[sem],
                                        device_id=_peer(k), device_id_type=pl.DeviceIdType.MESH)


def _exchange_start(srcs, lands, after, *, scatter, slot=None, name):
    n = len(srcs)

    def body(*refs):
        src_refs, land_refs = refs[:n], refs[n:2 * n]
        send, recv, token = refs[2 * n + 1], refs[2 * n + 2], refs[-1]
        for a in range(n):
            for k in range(1, N_DEV):
                _split_copy(src_refs[a], land_refs[a], send, recv, a, k, scatter, slot, True).start()
        token[...] = jnp.zeros_like(token)

    hbm = lambda t: pltpu.HBM(t.shape, t.dtype)
    sems = pltpu.SemaphoreType.DMA((n * (N_DEV - 1),))
    out = pl.pallas_call(
        body, name=name,
        out_shape=(sems, sems, *[hbm(t) for t in srcs], *[hbm(t) for t in lands], jax.ShapeDtypeStruct((8, LANE), F32)),
        in_specs=[_HBM] * (2 * n) + [pl.BlockSpec(memory_space=pl.ANY)],
        out_specs=(_SEM, _SEM, *[_HBM] * (2 * n), pl.BlockSpec(memory_space=pltpu.VMEM)),
        input_output_aliases={i: 2 + i for i in range(2 * n)},
        compiler_params=pltpu.CompilerParams(has_side_effects=_EFFECT),
    )(*[pltpu.with_memory_space_constraint(t, pltpu.HBM) for t in (*srcs, *lands)], after)
    return out[0], out[1], out[2:2 + n], out[2 + n:2 + 2 * n], out[-1]


def _exchange_wait(send, recv, srcs, lands, after, *, scatter, slot=None, name):
    n = len(srcs)

    def body(*refs):
        src_refs, land_refs = refs[:n], refs[n:2 * n]
        send_ref, recv_ref = refs[2 * n], refs[2 * n + 1]
        for a in range(n):
            for k in range(1, N_DEV):
                _split_copy(src_refs[a], land_refs[a], send_ref, recv_ref, a, k, scatter, slot, True).wait_send()
                _split_copy(src_refs[a], land_refs[a], send_ref, recv_ref, a, k, scatter, slot, False).wait_recv()

    hbm = lambda t: pltpu.HBM(t.shape, t.dtype)
    out = pl.pallas_call(
        body, name=name, out_shape=(*[hbm(t) for t in srcs], *[hbm(t) for t in lands]),
        in_specs=[_HBM] * (2 * n) + [_SEM, _SEM, pl.BlockSpec(memory_space=pl.ANY)],
        out_specs=tuple([_HBM] * (2 * n)), input_output_aliases={i: i for i in range(2 * n)},
        compiler_params=pltpu.CompilerParams(has_side_effects=_EFFECT),
    )(*srcs, *lands, send, recv, after)
    return out[:n], out[n:]


def _win_to_mine(w):
    pad = jnp.zeros(w.shape[:-1] + (IN_PAD - IN_DIM,), w.dtype)
    return jnp.concatenate([w[..., 3592:5640], w[..., 0:3584], w[..., 3584:3592], pad], axis=-1)


def _win_from_mine(g):
    return jnp.concatenate([g[..., 2048:5632], g[..., 5632:5640], g[..., 0:2048]], axis=-1)


def _pad_rows(a, mult=8):
    r = (-a.shape[0]) % mult
    return a if r == 0 else jnp.concatenate([a, jnp.zeros((r,) + a.shape[1:], a.dtype)], axis=0)


def _lanes(vec, start):
    return jnp.zeros((1, LANE), F32).at[0, start:start + vec.shape[0]].set(vec)


def _small_spec(depth):
    return (("b_ada", (depth, 6 * D)), ("norm1_g", (depth, D)), ("norm2_g", (depth, D)),
            ("rel_table", (depth, A_HEADS, 2 * A_MAX_REL + 1)), ("a_log", (depth, B_HEADS)),
            ("dt_bias", (depth, B_HEADS)), ("gdn_norm_g", (depth, B_DH)), ("final_g", (D,)))


def _pack_small(d, extra, depth):
    spec = _small_spec(depth)
    rows = -(-(sum(math.prod(s) for _, s in spec) + 1) // (8 * LANE)) * 8
    flat = jnp.concatenate([d[n].reshape(-1).astype(F32) for n, _ in spec] + [extra.reshape(-1)])
    flat = jnp.concatenate([flat, jnp.zeros((rows * LANE - flat.shape[0],), F32)])
    return flat.reshape(rows, LANE)


def _unpack_small(p, depth):
    flat = p.reshape(-1)
    out, off = {}, 0
    for n, s in _small_spec(depth):
        sz = math.prod(s)
        out[n] = flat[off:off + sz].reshape(s)
        off += sz
    return out, flat[off]


def kernel(x, c, w_ada, b_ada, norm1_g, norm2_g, w_in, rel_table, w_conv, a_log, dt_bias, gdn_norm_g, w_branch_a, w_branch_b, w_out, w_ffn_in, w_ffn_out, final_g, loss_target, m_w_ada, m_b_ada, m_norm1_g, m_norm2_g, m_w_in, m_rel_table, m_w_conv, m_a_log, m_dt_bias, m_gdn_norm_g, m_w_branch_a, m_w_branch_b, m_w_out, m_w_ffn_in, m_w_ffn_out, m_final_g, v_w_ada, v_b_ada, v_norm1_g, v_norm2_g, v_w_in, v_rel_table, v_w_conv, v_a_log, v_dt_bias, v_gdn_norm_g, v_w_branch_a, v_w_branch_b, v_w_out, v_w_ffn_in, v_w_ffn_out, v_final_g):
    tt = x.shape[1]
    x0 = x[0]
    tgt = loss_target[0]
    me = _my_index()
    depth = w_in.shape[0]

    shards = [w_in.astype(BF16), w_branch_a.astype(BF16), w_branch_b.astype(BF16), w_out.astype(BF16),
              w_ffn_in.astype(BF16), w_ffn_out.astype(BF16), w_conv]
    first = _all_gather([t[0] for t in shards] + [_pad_rows(c)], name="gather_first")
    c_all = first[-1][:, 0, :]

    def unpack(g):
        cols = lambda t: jnp.transpose(t, (1, 0, 2)).reshape(t.shape[1], N_DEV * t.shape[2])
        rows = lambda t: t.reshape(N_DEV * t.shape[1], t.shape[2])
        return dict(win=_win_to_mine(cols(g[0])), wa=cols(g[1]), wb=cols(g[2]), wout=rows(g[3]), wfi=cols(g[4]),
                    wfo=rows(g[5]), wconv=cols(g[6]))

    def gather_start(l, after):
        srcs = [t[l] for t in shards]
        lands = [lax.empty((N_DEV,) + t.shape, t.dtype) for t in srcs]
        return _exchange_start(srcs, lands, after, scatter=False, name=f"gather_start_{l}")

    def gather_wait(l, pending, after):
        send, recv, srcs, lands, _ = pending
        srcs, lands = _exchange_wait(send, recv, srcs, lands, after, scatter=False, name=f"gather_wait_{l}")
        return unpack([lax.dynamic_update_index_in_dim(g, t, me, 0) for g, t in zip(lands, srcs)])

    weights = [unpack(first[:-1])] + [None] * (depth - 1)
    pending = gather_start(1, first[-1]) if depth > 1 else None
    cond = c_all * (1.0 / (1.0 + jnp.exp(-c_all)))
    cond = _pad_rows(cond, 16)

    mod_cols = jnp.stack([_mm(cond, w_ada[l], name="mod_mm")[:N_DEV] for l in range(depth)])
    (g_mod,) = _all_gather([mod_cols], name="gather_mod")
    mod_all = jnp.transpose(g_mod, (1, 2, 0, 3)).reshape(depth, N_DEV, 6 * D)
    mod = lax.dynamic_index_in_dim(mod_all, me, axis=1, keepdims=False) + b_ada
    mods = mod.reshape(depth, 6, 1, D)

    n1g, n2g = norm1_g.reshape(depth, 1, D), norm2_g.reshape(depth, 1, D)
    gng = gdn_norm_g.reshape(depth, 1, B_DH)
    fg = final_g.reshape(1, D)

    saved = []
    tok = pending[-1][0, 0] if pending is not None else 0.0
    xin, h1 = _adaln_fwd(x0, n1g[0], mods[0, 1] + tok, mods[0, 0], name="adaln1_first")
    for l in range(depth):
        sh1, sc1, gt1, sh2, sc2, gt2 = (mods[l, i] for i in range(6))
        wl = weights[l]
        proj = _mm(h1, wl["win"], name="proj_mm", tn=1152)
        kpad = jnp.pad(proj[:, OFF_KA:OFF_KA + A_W].astype(BF16), ((A_PAST * CH, 0), (0, 0)))
        vpad = jnp.pad(proj[:, OFF_VA:OFF_VA + A_W].astype(BF16), ((A_PAST * CH, 0), (0, 0)))
        bias, bias_vjp = jax.vjp(_bias_from_table, rel_table[l])
        ya, lse = _attn_fwd(proj, kpad, vpad, bias, name="attn_fwd")
        par = jnp.concatenate([_lanes(a_log[l], B_HEADS), _lanes(dt_bias[l], B_HEADS), jnp.zeros((6, LANE), F32)], axis=0)
        qn, kn, vn, aux = _gdn_pre_fwd(proj, wl["wconv"], par, name="gdn_pre_fwd")
        lower = _gdn_lower(kn, aux, name="gdn_lower")
        tinv = jnp.transpose(_tri_inverse(jnp.transpose(lower, (1, 2, 0)), name="gdn_tri_inverse"), (2, 0, 1))
        og, ss = _gdn_scan_fwd(qn, kn, vn, aux, tinv, name="gdn_scan_fwd")
        yb = _gdn_post_fwd(og, proj, gng[l], name="gdn_post_fwd")
        pa = _mm(ya, wl["wa"], name="branch_a_mm")
        pb = _mm(yb, wl["wb"], name="branch_b_mm")
        merged = _merge_fwd(proj, pa, pb, name="merge_fwd")
        t1 = _mm(merged, wl["wout"], name="out_mm")
        x2, h2 = _adaln_fwd(xin, n2g[l], sc2, sh2, t1, gt1, name="adaln2_fwd")
        gu = _mm(h2, wl["wfi"], name="ffn_in_mm", tn=1408)
        act = _swiglu_fwd(gu, name="swiglu_fwd")
        t2 = _mm(act, wl["wfo"], name="ffn_out_mm", tk=1408)
        saved.append(dict(xin=xin, h1=h1, proj=proj, kpad=kpad, vpad=vpad, bias=bias, bias_vjp=bias_vjp, ya=ya, lse=lse,
                          par=par, qn=qn, kn=kn, vn=vn, aux=aux, tinv=tinv, ss=ss, og=og, yb=yb, pa=pa, pb=pb,
                          merged=merged, t1=t1, x2=x2, h2=h2, gu=gu, act=act, t2=t2))
        if l + 1 < depth:
            weights[l + 1] = gather_wait(l + 1, pending, t2)
            pending = gather_start(l + 2, weights[l + 1]["wconv"]) if l + 2 < depth else None
            tok = pending[-1][0, 0] if pending is not None else 0.0
            xin, h1 = _adaln_fwd(x2, n1g[l + 1], mods[l + 1, 1] + tok, mods[l + 1, 0], t2, gt2, name="adaln1_fwd")

    s = saved[-1]
    dx, dt2, st = _loss_head(s["x2"], s["t2"], mods[depth - 1, 5], fg, tgt, name="loss_head")
    loss_part = st[4, 0]
    small_g = {"final_g": st[0]}
    dmod_rows = [None] * depth
    for n in ("norm1_g", "norm2_g", "rel_table", "a_log", "dt_bias", "gdn_norm_g"):
        small_g[n] = [None] * depth
    dgt2 = st[3]
    cols_slabs = lambda g: jnp.transpose(g.reshape(g.shape[0], N_DEV, g.shape[1] // N_DEV), (1, 0, 2))
    rows_slabs = lambda g: g.reshape(N_DEV, g.shape[0] // N_DEV, g.shape[1])
    lands = [lax.empty((N_DEV,) + t.shape, t.dtype) for t in shards]
    own, pending_s = [None] * depth, None
    for l in reversed(range(depth)):
        s, wl = saved[l], weights[l]
        sh1, sc1, gt1, sh2, sc2, gt2 = (mods[l, i] for i in range(6))
        gw_fo = _mm(s["act"], dt2, ta=True, out_dtype=BF16, name="ffn_out_dw", tm=1408)
        dact = _mm(dt2, wl["wfo"], tb=True, name="ffn_out_dx", tn=1408)
        dgu = _swiglu_bwd(s["gu"], dact, name="swiglu_bwd")
        gw_fi = _mm(s["h2"], dgu, ta=True, out_dtype=BF16, name="ffn_in_dw", tn=1408)
        dh2 = _mm(dgu, wl["wfi"], tb=True, name="ffn_in_dx", tk=1408)
        dx, dt1, st2 = _adaln_bwd(s["x2"], n2g[l], sc2, sh2, dh2, dx, s["t1"], gt1, name="adaln2_bwd")
        gw_out = _mm(s["merged"], dt1, ta=True, out_dtype=BF16, name="out_dw")
        dmerged = _mm(dt1, wl["wout"], tb=True, name="out_dx")
        dgates, dpa, dpb = _merge_bwd(s["proj"], s["pa"], s["pb"], dmerged, name="merge_bwd")
        gw_a = _mm(s["ya"], dpa, ta=True, out_dtype=BF16, name="branch_a_dw")
        gw_b = _mm(s["yb"], dpb, ta=True, out_dtype=BF16, name="branch_b_dw")
        dya = _mm(dpa, wl["wa"], tb=True, name="branch_a_dx")
        dyb = _mm(dpb, wl["wb"], tb=True, name="branch_b_dx")
        dqa, dkpad, dvpad, dbias = _attn_bwd(s["proj"], s["kpad"], s["vpad"], s["bias"], s["ya"], s["lse"], dya,
                                             name="attn_bwd")
        small_g["rel_table"][l] = s["bias_vjp"](dbias)[0]
        dog, dz, dgn = _gdn_post_bwd(s["og"], s["proj"], gng[l], dyb, name="gdn_post_bwd")
        small_g["gdn_norm_g"][l] = dgn[0]
        dqn, dkn, dvn, daux = _gdn_scan_bwd(s["qn"], s["kn"], s["vn"], s["aux"], s["tinv"], s["ss"], dog,
                                            name="gdn_scan_bwd")
        dqkv, dba, dwc, dpar = _gdn_pre_bwd(s["proj"], wl["wconv"], s["par"], dqn, dkn, dvn, daux, name="gdn_pre_bwd")
        small_g["a_log"][l] = dpar[0, B_HEADS:2 * B_HEADS]
        small_g["dt_bias"][l] = dpar[1, B_HEADS:2 * B_HEADS]
        dproj = jnp.concatenate([dgates, dqa.astype(BF16), dkpad[A_PAST * CH:].astype(BF16),
                                 dvpad[A_PAST * CH:].astype(BF16), dqkv, dz, dba], axis=1)
        gw_in = _mm(s["h1"], dproj, ta=True, out_dtype=BF16, name="proj_dw", tn=1152)
        dh1 = _mm(dproj, wl["win"], tb=True, name="proj_dx", tk=1152)
        srcs = [cols_slabs(_win_from_mine(gw_in)), cols_slabs(gw_a), cols_slabs(gw_b), rows_slabs(gw_out),
                cols_slabs(gw_fi), rows_slabs(gw_fo), cols_slabs(dwc[0:CONV_K])]
        if pending_s is not None:
            done, lands = _exchange_wait(*pending_s[:4], gw_in, scatter=True, slot=l + 1, name=f"scatter_wait_{l + 1}")
            own[l + 1] = [lax.dynamic_index_in_dim(t, me, 0, keepdims=False) for t in done]
        pending_s = _exchange_start(srcs, lands, dwc, scatter=True, slot=l, name=f"scatter_start_{l}")
        sc1 = sc1 + pending_s[-1][0, 0]
        if l > 0:
            p = saved[l - 1]
            dx, dt2, st1 = _adaln_bwd(s["xin"], n1g[l], sc1, sh1, dh1, dx, p["t2"], mods[l - 1, 5], name="adaln1_bwd")
        else:
            dx, st1 = _adaln_bwd(s["xin"], n1g[l], sc1, sh1, dh1, dx, name="adaln1_bwd_first")
        small_g["norm1_g"][l], small_g["norm2_g"][l] = st1[0], st2[0]
        dmod_rows[l] = jnp.concatenate([st1[2], st1[1], st2[3], st2[2], st2[1], dgt2])
        if l > 0:
            dgt2 = st1[3]
    grad_x = dx[None]

    small_local = {n: (jnp.stack(vs) if isinstance(vs, list) else vs) for n, vs in small_g.items()}
    small_local["b_ada"] = jnp.stack(dmod_rows)
    (g_small,) = _all_gather([_pack_small(small_local, loss_part, depth)], name="gather_small")
    wsm = _pack_small(dict(b_ada=b_ada, norm1_g=norm1_g, norm2_g=norm2_g, rel_table=rel_table, a_log=a_log,
                           dt_bias=dt_bias, gdn_norm_g=gdn_norm_g, final_g=final_g), jnp.zeros((1,), F32), depth)
    msm = _pack_small(dict(b_ada=m_b_ada, norm1_g=m_norm1_g, norm2_g=m_norm2_g, rel_table=m_rel_table, a_log=m_a_log,
                           dt_bias=m_dt_bias, gdn_norm_g=m_gdn_norm_g, final_g=m_final_g), jnp.zeros((1,), F32), depth)
    vsm = _pack_small(dict(b_ada=v_b_ada, norm1_g=v_norm1_g, norm2_g=v_norm2_g, rel_table=v_rel_table, a_log=v_a_log,
                           dt_bias=v_dt_bias, gdn_norm_g=v_gdn_norm_g, final_g=v_final_g), jnp.ones((1,), F32), depth)
    sm = [_unpack_small(t, depth) for t in _adamw(g_small, wsm, msm, vsm, name="adamw_small")]
    loss = sm[0][1]

    dmod_all = g_small.reshape(N_DEV, -1)[:, :depth * 6 * D].reshape(N_DEV, depth, 6 * D)
    dmod_mine = lax.dynamic_slice_in_dim(dmod_all, me * (6 * D // N_DEV), 6 * D // N_DEV, axis=2)
    g_ada = jnp.stack([_mm(cond, _pad_rows(dmod_mine[:, l], 16), ta=True, name="ada_dw") for l in range(depth)])

    done, lands = _exchange_wait(*pending_s[:4], g_ada, scatter=True, slot=0, name="scatter_wait_0")
    own[0] = [lax.dynamic_index_in_dim(t, me, 0, keepdims=False) for t in done]
    r_in, r_a, r_b, r_out, r_fi, r_fo, r_conv = [
        lax.dynamic_update_index_in_dim(land, jnp.stack([own[l][a] for l in range(depth)]), me, 0)
        for a, land in enumerate(lands)]

    def upd(parts, w, m, v, name):
        shp = w.shape
        two = lambda a: a.reshape(-1, shp[-1])
        return [o.reshape(shp) for o in _adamw(parts.reshape(parts.shape[0], -1, shp[-1]), two(w), two(m), two(v), name=name)]

    res = {
        "w_ada": upd(g_ada[None], w_ada, m_w_ada, v_w_ada, "adamw_w_ada"),
        "w_in": upd(r_in, w_in, m_w_in, v_w_in, "adamw_w_in"),
        "w_conv": upd(r_conv, w_conv, m_w_conv, v_w_conv, "adamw_w_conv"),
        "w_branch_a": upd(r_a, w_branch_a, m_w_branch_a, v_w_branch_a, "adamw_w_branch_a"),
        "w_branch_b": upd(r_b, w_branch_b, m_w_branch_b, v_w_branch_b, "adamw_w_branch_b"),
        "w_out": upd(r_out, w_out, m_w_out, v_w_out, "adamw_w_out"),
        "w_ffn_in": upd(r_fi, w_ffn_in, m_w_ffn_in, v_w_ffn_in, "adamw_w_ffn_in"),
        "w_ffn_out": upd(r_fo, w_ffn_out, m_w_ffn_out, v_w_ffn_out, "adamw_w_ffn_out"),
    }
    for n, _ in _small_spec(depth):
        res[n] = [sm[i][0][n] for i in range(4)]
    order = ("w_ada", "b_ada", "norm1_g", "norm2_g", "w_in", "rel_table", "w_conv", "a_log", "dt_bias", "gdn_norm_g",
             "w_branch_a", "w_branch_b", "w_out", "w_ffn_in", "w_ffn_out", "final_g")
    return (loss, grad_x, *[res[n][0] for n in order], *[res[n][1] for n in order],
            *[res[n][2] for n in order], *[res[n][3] for n in order])
```

```python
import functools
import math

import jax
import jax.numpy as jnp
from jax import lax
from jax.experimental import pallas as pl
from jax.experimental.pallas import tpu as pltpu

F32 = jnp.float32
BF16 = jnp.bfloat16
HI = lax.Precision.HIGHEST

N_DEV = 8
D = 1024
DEPTH = 4
CH = 64
EPS = 1e-6
A_HEADS, A_DH = 8, 64
A_W = A_HEADS * A_DH
A_PAST = 8
A_MAX_REL = 128
QB = 256
KB = QB + A_PAST * CH
B_HEADS, B_DH = 4, 128
B_W = B_HEADS * B_DH
CONV_K = 4
FF = 2816
IN_DIM = 5640
IN_PAD = 5760
LANE = 128
NEG = -1e30
VMEM_LIMIT = 48 * 1024 * 1024

ADAM_LR, ADAM_B1, ADAM_B2, ADAM_EPS, ADAM_WD, ADAM_STEP = 0.001, 0.9, 0.999, 1e-08, 0.01, 10

OFF_GA, OFF_GB, OFF_QA, OFF_KA, OFF_VA, OFF_QB, OFF_KB, OFF_VB, OFF_ZB, OFF_BA = (
    0, 1024, 2048, 2560, 3072, 3584, 4096, 4608, 5120, 5632)


def _cp(sem=None):
    return pltpu.CompilerParams(dimension_semantics=sem, vmem_limit_bytes=VMEM_LIMIT)


def _tile(n, pref):
    if n <= pref:
        return n
    best = None
    for t in range(LANE, pref + 1, LANE):
        if n % t == 0:
            best = t
    assert best is not None, (n, pref)
    return best


def _sigmoid(x):
    return 1.0 / (1.0 + jnp.exp(-x))


def _silu(x):
    return x * _sigmoid(x)


def _dsilu(x):
    s = _sigmoid(x)
    return s * (1.0 + x * (1.0 - s))


def _dot(a, b, prec=None):
    return jnp.dot(a, b, preferred_element_type=F32, precision=prec)


def _dot_nt(a, b, prec=None):
    return lax.dot_general(a, b, (((1,), (1,)), ((), ())), preferred_element_type=F32, precision=prec)


def _dot_tn(a, b, prec=None):
    return lax.dot_general(a, b, (((0,), (0,)), ((), ())), preferred_element_type=F32, precision=prec)


def _mm(a, b, *, ta=False, tb=False, out_dtype=F32, name, tm=1024, tn=1024, tk=1024):
    m, k = (a.shape[1], a.shape[0]) if ta else a.shape
    n = b.shape[0] if tb else b.shape[1]
    assert k == (b.shape[1] if tb else b.shape[0]), (a.shape, b.shape, ta, tb)
    tm, tn, tk = _tile(m, tm), _tile(n, tn), _tile(k, tk)
    nk = k // tk
    dn = (((0 if ta else 1,), (1 if tb else 0,)), ((), ()))

    def body(a_ref, b_ref, o_ref, *acc):
        part = lax.dot_general(a_ref[...].astype(BF16), b_ref[...].astype(BF16), dn, preferred_element_type=F32)
        if nk == 1:
            o_ref[...] = part.astype(out_dtype)
            return
        acc_ref, kk = acc[0], pl.program_id(2)

        @pl.when(kk == 0)
        def _():
            acc_ref[...] = part

        @pl.when(kk > 0)
        def _():
            acc_ref[...] += part

        @pl.when(kk == nk - 1)
        def _():
            o_ref[...] = acc_ref[...].astype(out_dtype)

    a_spec = pl.BlockSpec((tk, tm), lambda i, j, q: (q, i)) if ta else pl.BlockSpec((tm, tk), lambda i, j, q: (i, q))
    b_spec = pl.BlockSpec((tn, tk), lambda i, j, q: (j, q)) if tb else pl.BlockSpec((tk, tn), lambda i, j, q: (q, j))
    return pl.pallas_call(
        body, grid=(m // tm, n // tn, nk), in_specs=[a_spec, b_spec],
        out_specs=pl.BlockSpec((tm, tn), lambda i, j, q: (i, j)),
        out_shape=jax.ShapeDtypeStruct((m, n), out_dtype),
        scratch_shapes=[pltpu.VMEM((tm, tn), F32)] if nk > 1 else [],
        compiler_params=_cp(("parallel", "parallel", "arbitrary")), name=name)(a, b)


def _rb(tr, width, cb=0):
    return pl.BlockSpec((tr, width), lambda i: (i, cb))


def _whole(shape):
    nd = len(shape)
    return pl.BlockSpec(shape, lambda i: (0,) * nd)


def _colsum(v):
    return jnp.sum(v, axis=0, keepdims=True)


def _adaln_fwd(x, g, sc, sh, t=None, gt=None, *, name, tr=256):
    tt = x.shape[0]
    res = t is not None

    def body(*refs):
        if res:
            x_ref, t_ref, gt_ref, g_ref, sc_ref, sh_ref, xo_ref, h_ref = refs
            xv = x_ref[...] + gt_ref[...] * t_ref[...]
            xo_ref[...] = xv
        else:
            x_ref, g_ref, sc_ref, sh_ref, h_ref = refs
            xv = x_ref[...]
        r = lax.rsqrt(jnp.mean(xv * xv, axis=-1, keepdims=True) + EPS)
        h_ref[...] = ((xv * r * g_ref[...]) * (1.0 + sc_ref[...]) + sh_ref[...]).astype(BF16)

    row, vec = _rb(tr, D), _whole((1, D))
    if res:
        ins, in_specs = (x, t, gt, g, sc, sh), [row, row, vec, vec, vec, vec]
        out_shape = (jax.ShapeDtypeStruct((tt, D), F32), jax.ShapeDtypeStruct((tt, D), BF16))
        out_specs = (row, row)
    else:
        ins, in_specs = (x, g, sc, sh), [row, vec, vec, vec]
        out_shape, out_specs = jax.ShapeDtypeStruct((tt, D), BF16), row
    out = pl.pallas_call(body, grid=(tt // tr,), in_specs=in_specs, out_specs=out_specs, out_shape=out_shape,
                         compiler_params=_cp(("parallel",)), name=name)(*ins)
    return out if res else (x, out)


def _adaln_bwd(x, g, sc, sh, dh, dx_in, t=None, gt=None, *, name, tr=256):
    tt = x.shape[0]
    res = t is not None

    def body(*refs):
        if res:
            x_ref, g_ref, sc_ref, sh_ref, dh_ref, dxi_ref, t_ref, gt_ref, dx_ref, dt_ref, st_ref = refs
        else:
            x_ref, g_ref, sc_ref, sh_ref, dh_ref, dxi_ref, dx_ref, st_ref = refs

        @pl.when(pl.program_id(0) == 0)
        def _():
            st_ref[...] = jnp.zeros_like(st_ref)

        xv, dh = x_ref[...], dh_ref[...]
        r = lax.rsqrt(jnp.mean(xv * xv, axis=-1, keepdims=True) + EPS)
        nrm = xv * r
        y = nrm * g_ref[...]
        dy = dh * (1.0 + sc_ref[...])
        dn = dy * g_ref[...]
        dx = dxi_ref[...] + r * (dn - nrm * jnp.mean(dn * nrm, axis=-1, keepdims=True))
        dx_ref[...] = dx
        st_ref[0:1, :] += _colsum(dy * nrm)
        st_ref[1:2, :] += _colsum(dh * y)
        st_ref[2:3, :] += _colsum(dh)
        if res:
            dt_ref[...] = (gt_ref[...] * dx).astype(BF16)
            st_ref[3:4, :] += _colsum(dx * t_ref[...])

    row, vec, st = _rb(tr, D), _whole((1, D)), _whole((8, D))
    ins, in_specs = [x, g, sc, sh, dh, dx_in], [row, vec, vec, vec, row, row]
    out_shape, out_specs = [jax.ShapeDtypeStruct((tt, D), F32)], [row]
    if res:
        ins += [t, gt]
        in_specs += [row, vec]
        out_shape.append(jax.ShapeDtypeStruct((tt, D), BF16))
        out_specs.append(row)
    out_shape.append(jax.ShapeDtypeStruct((8, D), F32))
    out_specs.append(st)
    return pl.pallas_call(body, grid=(tt // tr,), in_specs=in_specs, out_specs=tuple(out_specs),
                          out_shape=tuple(out_shape), compiler_params=_cp(("arbitrary",)), name=name)(*ins)


def _loss_head(x, t, gt, fg, tgt, *, name, tr=256):
    tt = x.shape[0]

    def body(x_ref, t_ref, gt_ref, fg_ref, tgt_ref, dx_ref, dt_ref, st_ref):
        @pl.when(pl.program_id(0) == 0)
        def _():
            st_ref[...] = jnp.zeros_like(st_ref)

        tv = t_ref[...]
        xv = x_ref[...] + gt_ref[...] * tv
        r = lax.rsqrt(jnp.mean(xv * xv, axis=-1, keepdims=True) + EPS)
        nrm = xv * r
        err = nrm * fg_ref[...] - tgt_ref[...]
        st_ref[4:5, :] += 0.5 * jnp.sum(jnp.mean(err * err, axis=-1, keepdims=True), axis=0, keepdims=True)
        dy = err * (1.0 / D)
        dn = dy * fg_ref[...]
        dx = r * (dn - nrm * jnp.mean(dn * nrm, axis=-1, keepdims=True))
        dx_ref[...] = dx
        dt_ref[...] = (gt_ref[...] * dx).astype(BF16)
        st_ref[0:1, :] += _colsum(dy * nrm)
        st_ref[3:4, :] += _colsum(dx * tv)

    row, vec = _rb(tr, D), _whole((1, D))
    return pl.pallas_call(
        body, grid=(tt // tr,), in_specs=[row, row, vec, vec, row], out_specs=(row, row, _whole((8, D))),
        out_shape=(jax.ShapeDtypeStruct((tt, D), F32), jax.ShapeDtypeStruct((tt, D), BF16),
                   jax.ShapeDtypeStruct((8, D), F32)),
        compiler_params=_cp(("arbitrary",)), name=name)(x, t, gt, fg, tgt)


def _merge_fwd(proj, pa, pb, *, name, tr=256):
    tt = pa.shape[0]

    def body(ga_ref, gb_ref, pa_ref, pb_ref, o_ref):
        o_ref[...] = (_sigmoid(ga_ref[...]) * pa_ref[...].astype(F32)
                      + _sigmoid(gb_ref[...]) * pb_ref[...].astype(F32)).astype(BF16)

    row = _rb(tr, D)
    return pl.pallas_call(body, grid=(tt // tr,), in_specs=[_rb(tr, D, 0), _rb(tr, D, 1), row, row], out_specs=row,
                          out_shape=jax.ShapeDtypeStruct((tt, D), BF16), compiler_params=_cp(("parallel",)),
                          name=name)(proj, proj, pa, pb)


def _merge_bwd(proj, pa, pb, dm, *, name, tr=256):
    tt = pa.shape[0]

    def body(ga_ref, gb_ref, pa_ref, pb_ref, dm_ref, dg_ref, dpa_ref, dpb_ref):
        dm_v = dm_ref[...]
        sa, sb = _sigmoid(ga_ref[...]), _sigmoid(gb_ref[...])
        dpa_ref[...] = (dm_v * sa).astype(BF16)
        dpb_ref[...] = (dm_v * sb).astype(BF16)
        dg_ref[:, 0:D] = (dm_v * pa_ref[...].astype(F32) * sa * (1.0 - sa)).astype(BF16)
        dg_ref[:, D:2 * D] = (dm_v * pb_ref[...].astype(F32) * sb * (1.0 - sb)).astype(BF16)

    row = _rb(tr, D)
    return pl.pallas_call(
        body, grid=(tt // tr,), in_specs=[_rb(tr, D, 0), _rb(tr, D, 1), row, row, row],
        out_specs=(_rb(tr, 2 * D), row, row),
        out_shape=(jax.ShapeDtypeStruct((tt, 2 * D), BF16), jax.ShapeDtypeStruct((tt, D), BF16),
                   jax.ShapeDtypeStruct((tt, D), BF16)),
        compiler_params=_cp(("parallel",)), name=name)(proj, proj, pa, pb, dm)


def _swiglu_fwd(gu, *, name, tr=256):
    tt = gu.shape[0]

    def body(g_ref, u_ref, o_ref):
        o_ref[...] = (_silu(g_ref[...].astype(F32)) * u_ref[...].astype(F32)).astype(BF16)

    return pl.pallas_call(body, grid=(tt // tr,), in_specs=[_rb(tr, FF, 0), _rb(tr, FF, 1)], out_specs=_rb(tr, FF),
                          out_shape=jax.ShapeDtypeStruct((tt, FF), BF16), compiler_params=_cp(("parallel",)),
                          name=name)(gu, gu)


def _swiglu_bwd(gu, dact, *, name, tr=256):
    tt = gu.shape[0]

    def body(g_ref, u_ref, da_ref, o_ref):
        gv, da = g_ref[...].astype(F32), da_ref[...].astype(F32)
        o_ref[:, 0:FF] = (da * u_ref[...].astype(F32) * _dsilu(gv)).astype(BF16)
        o_ref[:, FF:2 * FF] = (da * _silu(gv)).astype(BF16)

    return pl.pallas_call(body, grid=(tt // tr,), in_specs=[_rb(tr, FF, 0), _rb(tr, FF, 1), _rb(tr, FF)],
                          out_specs=_rb(tr, 2 * FF), out_shape=jax.ShapeDtypeStruct((tt, 2 * FF), BF16),
                          compiler_params=_cp(("parallel",)), name=name)(gu, gu, dact)


def _bias_from_table(table):
    lw = 1152
    n_hi = KB - A_MAX_REL
    w = jnp.concatenate([
        jnp.broadcast_to(table[:, 2 * A_MAX_REL:], (A_HEADS, n_hi)),
        jnp.flip(table[:, 1:2 * A_MAX_REL], axis=1),
        jnp.broadcast_to(table[:, 0:1], (A_HEADS, lw - n_hi - (2 * A_MAX_REL - 1)))], axis=1)
    flat = jnp.broadcast_to(w[:, None, :], (A_HEADS, QB, lw)).reshape(A_HEADS, QB * lw)
    skew = flat[:, :QB * (lw - 1)].reshape(A_HEADS, QB, lw - 1)
    bias = skew[:, :, QB - 1:QB - 1 + KB]
    qc = jnp.arange(QB)[:, None] // CH + A_PAST
    kc = jnp.arange(KB)[None, :] // CH
    inband = (kc <= qc) & (kc >= qc - A_PAST)
    return jnp.where(inband[None], bias, NEG)


def _attn_fwd(proj, kpad, vpad, bias, *, name):
    tt = proj.shape[0]

    def body(q_ref, k_ref, v_ref, b_ref, o_ref, l_ref):
        q0 = pl.multiple_of(pl.program_id(1) * QB, QB)
        q = q_ref[...]
        k = k_ref[pl.ds(q0, KB), :]
        v = v_ref[pl.ds(q0, KB), :]
        lane = lax.broadcasted_iota(jnp.int32, (QB, LANE), 1)
        valid = (lax.broadcasted_iota(jnp.int32, (QB, KB), 1) + q0) >= A_PAST * CH
        o = jnp.zeros((QB, LANE), F32)
        lse = jnp.zeros((QB, LANE), F32)
        for a in range(2):
            hm = (lane >= A_DH * a) & (lane < A_DH * (a + 1))
            s = _dot_nt(jnp.where(hm, q, 0.0).astype(BF16), k) * (A_DH ** -0.5) + b_ref[a]
            s = jnp.where(valid, s, NEG)
            m = jnp.max(s, axis=-1, keepdims=True)
            p = jnp.exp(s - m)
            l = jnp.sum(p, axis=-1, keepdims=True)
            o = jnp.where(hm, _dot((p / l).astype(BF16), v), o)
            lse = jnp.where(hm, m + jnp.log(l), lse)
        o_ref[...] = o.astype(BF16)
        l_ref[...] = lse

    kv = pl.BlockSpec((tt + A_PAST * CH, LANE), lambda h, i: (0, h))
    blk = pl.BlockSpec((QB, LANE), lambda h, i: (i, h))
    return pl.pallas_call(
        body, grid=(A_W // LANE, tt // QB),
        in_specs=[pl.BlockSpec((QB, LANE), lambda h, i: (i, OFF_QA // LANE + h)), kv, kv,
                  pl.BlockSpec((2, QB, KB), lambda h, i: (h, 0, 0))],
        out_specs=(blk, blk),
        out_shape=(jax.ShapeDtypeStruct((tt, A_W), BF16), jax.ShapeDtypeStruct((tt, A_W), F32)),
        compiler_params=_cp(("parallel", "parallel")), name=name)(proj, kpad, vpad, bias)


def _attn_bwd(proj, kpad, vpad, bias, o, lse, do, *, name):
    tt = proj.shape[0]

    def body(q_ref, k_ref, v_ref, b_ref, o_ref, l_ref, do_ref, dq_ref, dk_ref, dv_ref, db_ref):
        @pl.when(pl.program_id(1) == 0)
        def _():
            dk_ref[...] = jnp.zeros_like(dk_ref)
            dv_ref[...] = jnp.zeros_like(dv_ref)
            db_ref[...] = jnp.zeros_like(db_ref)

        q0 = pl.multiple_of(pl.program_id(1) * QB, QB)
        q, do_v, lse = q_ref[...], do_ref[...], l_ref[...]
        k = k_ref[pl.ds(q0, KB), :]
        v = v_ref[pl.ds(q0, KB), :]
        dsum = do_v * o_ref[...].astype(F32)
        lane = lax.broadcasted_iota(jnp.int32, (QB, LANE), 1)
        valid = (lax.broadcasted_iota(jnp.int32, (QB, KB), 1) + q0) >= A_PAST * CH
        dq = jnp.zeros((QB, LANE), F32)
        dk = jnp.zeros((KB, LANE), F32)
        dv = jnp.zeros((KB, LANE), F32)
        for a in range(2):
            hm = (lane >= A_DH * a) & (lane < A_DH * (a + 1))
            qa = jnp.where(hm, q, 0.0).astype(BF16)
            doa = jnp.where(hm, do_v, 0.0).astype(BF16)
            s = _dot_nt(qa, k) * (A_DH ** -0.5) + b_ref[a]
            s = jnp.where(valid, s, NEG)
            lse_a = jnp.max(jnp.where(hm, lse, NEG), axis=-1, keepdims=True)
            p = jnp.exp(s - lse_a)
            dp = _dot_nt(doa, v)
            dsum_a = jnp.sum(jnp.where(hm, dsum, 0.0), axis=-1, keepdims=True)
            ds = p * (dp - dsum_a)
            db_ref[a] += ds
            dsb = (ds * (A_DH ** -0.5)).astype(BF16)
            dq = jnp.where(hm, _dot(dsb, k), dq)
            dk += _dot_tn(dsb, qa)
            dv += _dot_tn(p.astype(BF16), doa)
        dq_ref[...] = dq
        dk_ref[pl.ds(q0, KB), :] += dk
        dv_ref[pl.ds(q0, KB), :] += dv

    kv = pl.BlockSpec((tt + A_PAST * CH, LANE), lambda h, i: (0, h))
    blk = pl.BlockSpec((QB, LANE), lambda h, i: (i, h))
    bsp = pl.BlockSpec((2, QB, KB), lambda h, i: (h, 0, 0))
    pad_shape = jax.ShapeDtypeStruct((tt + A_PAST * CH, A_W), F32)
    return pl.pallas_call(
        body, grid=(A_W // LANE, tt // QB),
        in_specs=[pl.BlockSpec((QB, LANE), lambda h, i: (i, OFF_QA // LANE + h)), kv, kv, bsp, blk, blk, blk],
        out_specs=(blk, kv, kv, bsp),
        out_shape=(jax.ShapeDtypeStruct((tt, A_W), F32), pad_shape, pad_shape,
                   jax.ShapeDtypeStruct((A_HEADS, QB, KB), F32)),
        compiler_params=_cp(("parallel", "arbitrary")), name=name)(proj, kpad, vpad, bias, o, lse, do)


GTR = 256


def _taps(w_ref, grp):
    return [w_ref[j:j + 1, grp * B_W:(grp + 1) * B_W] for j in range(CONV_K)]


def _shifts(xe, rows):
    return [xe[8:8 + rows]] + [pltpu.roll(xe, s, 0)[8:8 + rows] for s in range(1, CONV_K)]


def _conv(shifts, taps):
    acc = taps[CONV_K - 1] * shifts[0]
    for s in range(1, CONV_K):
        acc = acc + taps[CONV_K - 1 - s] * shifts[s]
    return acc


def _qk_scale(grp):
    return B_DH ** -0.5 if grp == 0 else 1.0


def _act_fwd(c, grp):
    y = _silu(c)
    if grp == 2:
        return y
    parts = []
    for hd in range(B_HEADS):
        yh = y[:, hd * B_DH:(hd + 1) * B_DH]
        parts.append(yh * (lax.rsqrt(jnp.sum(yh * yh, axis=-1, keepdims=True) + EPS) * _qk_scale(grp)))
    return jnp.concatenate(parts, axis=1)


def _act_bwd(c, dy, grp):
    if grp == 2:
        return dy * _dsilu(c)
    y = _silu(c)
    parts = []
    for hd in range(B_HEADS):
        yh = y[:, hd * B_DH:(hd + 1) * B_DH]
        r = lax.rsqrt(jnp.sum(yh * yh, axis=-1, keepdims=True) + EPS)
        dyh = dy[:, hd * B_DH:(hd + 1) * B_DH] * _qk_scale(grp)
        parts.append(r * dyh - yh * (r * r * r) * jnp.sum(dyh * yh, axis=-1, keepdims=True))
    return jnp.concatenate(parts, axis=1) * _dsilu(c)


def _chunk_tri(n, upper=False):
    r = lax.broadcasted_iota(jnp.int32, (n, n), 0)
    c = lax.broadcasted_iota(jnp.int32, (n, n), 1)
    same = (r // CH) == (c // CH)
    return jnp.where(same & ((r <= c) if upper else (r >= c)), 1.0, 0.0).astype(F32)


def _gate_rows(ba, par_ref):
    lane = lax.broadcasted_iota(jnp.int32, ba.shape, 1)
    z = ba + par_ref[1:2, :]
    sp = jnp.maximum(z, 0.0) + jnp.log(1.0 + jnp.exp(-jnp.abs(z)))
    g = -jnp.exp(par_ref[0:1, :]) * sp
    return jnp.where(lane < B_HEADS, _sigmoid(ba), jnp.where(lane < 2 * B_HEADS, g, 0.0)), z


def _prev8(cb):
    return pl.BlockSpec((8, B_W), lambda i: (jnp.maximum(i * (GTR // 8) - 1, 0), cb))


def _next8(cb, nb):
    return pl.BlockSpec((8, B_W), lambda i: (jnp.minimum((i + 1) * (GTR // 8), nb * (GTR // 8) - 1), cb))


def _gdn_pre_fwd(proj, wconv, par, *, name):
    tt = proj.shape[0]

    def body(q_ref, k_ref, v_ref, qh_ref, kh_ref, vh_ref, ba_ref, w_ref, par_ref, qo_ref, ko_ref, vo_ref, aux_ref):
        first = pl.program_id(0) == 0
        for grp, (x_ref, h_ref, o_ref) in enumerate(((q_ref, qh_ref, qo_ref), (k_ref, kh_ref, ko_ref),
                                                     (v_ref, vh_ref, vo_ref))):
            xe = jnp.concatenate([jnp.where(first, 0.0, h_ref[...]), x_ref[...]], axis=0)
            o_ref[...] = _act_fwd(_conv(_shifts(xe, GTR), _taps(w_ref, grp)), grp)
        bg, _ = _gate_rows(ba_ref[...], par_ref)
        lane = lax.broadcasted_iota(jnp.int32, bg.shape, 1)
        aux_ref[...] = jnp.where(lane < B_HEADS, bg, _dot(_chunk_tri(GTR), bg, HI))

    col = lambda off: _rb(GTR, B_W, off // B_W)
    outs = jax.ShapeDtypeStruct((tt, B_W), F32)
    return pl.pallas_call(
        body, grid=(tt // GTR,),
        in_specs=[col(OFF_QB), col(OFF_KB), col(OFF_VB), _prev8(OFF_QB // B_W), _prev8(OFF_KB // B_W),
                  _prev8(OFF_VB // B_W), _rb(GTR, LANE, OFF_BA // LANE), _whole((CONV_K, 3 * B_W)),
                  _whole((8, LANE))],
        out_specs=(_rb(GTR, B_W), _rb(GTR, B_W), _rb(GTR, B_W), _rb(GTR, LANE)),
        out_shape=(outs, outs, outs, jax.ShapeDtypeStruct((tt, LANE), F32)),
        compiler_params=_cp(("parallel",)), name=name)(proj, proj, proj, proj, proj, proj, proj, wconv, par)


def _gdn_pre_bwd(proj, wconv, par, dq, dk, dv, daux, *, name):
    tt = proj.shape[0]
    nb = tt // GTR

    def body(q_ref, k_ref, v_ref, qh_ref, kh_ref, vh_ref, qn_ref, kn_ref, vn_ref, ba_ref, w_ref, par_ref,
             dq_ref, dk_ref, dv_ref, dqn_ref, dkn_ref, dvn_ref, daux_ref, dx_ref, dba_ref, dw_ref, dpar_ref):
        i = pl.program_id(0)
        first, last = i == 0, i == nb - 1

        @pl.when(first)
        def _():
            dw_ref[...] = jnp.zeros_like(dw_ref)
            dpar_ref[...] = jnp.zeros_like(dpar_ref)

        groups = ((q_ref, qh_ref, qn_ref, dq_ref, dqn_ref), (k_ref, kh_ref, kn_ref, dk_ref, dkn_ref),
                  (v_ref, vh_ref, vn_ref, dv_ref, dvn_ref))
        for grp, (x_ref, h_ref, xn_ref, d_ref, dn_ref) in enumerate(groups):
            taps = _taps(w_ref, grp)
            xe = jnp.concatenate([jnp.where(first, 0.0, h_ref[...]), x_ref[...]], axis=0)
            sh = _shifts(xe, GTR)
            dc = _act_bwd(_conv(sh, taps), d_ref[...], grp)
            xe_n = jnp.concatenate([x_ref[GTR - 8:GTR, :], xn_ref[...]], axis=0)
            dcn = _act_bwd(_conv(_shifts(xe_n, 8), taps), dn_ref[...], grp)
            dce = jnp.concatenate([dc, jnp.where(last, 0.0, dcn)], axis=0)
            dx = taps[CONV_K - 1] * dc
            dw_ref[CONV_K - 1:CONV_K, grp * B_W:(grp + 1) * B_W] += _colsum(dc * sh[0])
            for s in range(1, CONV_K):
                dx = dx + taps[CONV_K - 1 - s] * pltpu.roll(dce, GTR + 8 - s, 0)[0:GTR]
                dw_ref[CONV_K - 1 - s:CONV_K - s, grp * B_W:(grp + 1) * B_W] += _colsum(dc * sh[s])
            dx_ref[:, grp * B_W:(grp + 1) * B_W] = dx.astype(BF16)
        ba = ba_ref[...]
        lane = lax.broadcasted_iota(jnp.int32, ba.shape, 1)
        bg, z = _gate_rows(ba, par_ref)
        daux_v = daux_ref[...]
        dg = _dot(_chunk_tri(GTR, upper=True), daux_v, HI)
        dgl = jnp.where((lane >= B_HEADS) & (lane < 2 * B_HEADS), dg, 0.0)
        da = dgl * (-jnp.exp(par_ref[0:1, :])) * _sigmoid(z)
        dbr = jnp.where(lane < B_HEADS, daux_v * bg * (1.0 - bg), 0.0)
        dba_ref[...] = (dbr + da).astype(BF16)
        dpar_ref[0:1, :] += _colsum(dgl * bg)
        dpar_ref[1:2, :] += _colsum(da)

    col = lambda off: _rb(GTR, B_W, off // B_W)
    row, rowl = _rb(GTR, B_W), _rb(GTR, LANE)
    return pl.pallas_call(
        body, grid=(nb,),
        in_specs=[col(OFF_QB), col(OFF_KB), col(OFF_VB),
                  _prev8(OFF_QB // B_W), _prev8(OFF_KB // B_W), _prev8(OFF_VB // B_W),
                  _next8(OFF_QB // B_W, nb), _next8(OFF_KB // B_W, nb), _next8(OFF_VB // B_W, nb),
                  _rb(GTR, LANE, OFF_BA // LANE), _whole((CONV_K, 3 * B_W)), _whole((8, LANE)),
                  row, row, row, _next8(0, nb), _next8(0, nb), _next8(0, nb), rowl],
        out_specs=(_rb(GTR, 3 * B_W), rowl, _whole((8, 3 * B_W)), _whole((8, LANE))),
        out_shape=(jax.ShapeDtypeStruct((tt, 3 * B_W), BF16), jax.ShapeDtypeStruct((tt, LANE), BF16),
                   jax.ShapeDtypeStruct((8, 3 * B_W), F32), jax.ShapeDtypeStruct((8, LANE), F32)),
        compiler_params=_cp(("arbitrary",)), name=name)(
            proj, proj, proj, proj, proj, proj, proj, proj, proj, proj, wconv, par, dq, dk, dv, dq, dk, dv, daux)


def _col(x, j):
    lane = lax.broadcasted_iota(jnp.int32, x.shape, 1)
    return jnp.sum(jnp.where(lane == j, x, 0.0), axis=-1, keepdims=True)


def _split(x):
    hi = x.astype(BF16)
    return hi, (x - hi.astype(F32)).astype(BF16)


def _dot3(a, b, tn=False):
    dot = _dot_tn if tn else _dot
    (ah, al), (bh, bl) = _split(a), _split(b)
    return dot(ah, bh) + (dot(ah, bl) + dot(al, bh))


def _chunk_masks():
    r = lax.broadcasted_iota(jnp.int32, (CH, CH), 0)
    c = lax.broadcasted_iota(jnp.int32, (CH, CH), 1)
    return r > c, r >= c


def _gc_rows(aux, nc):
    t = jnp.transpose(aux[:, B_HEADS:2 * B_HEADS].reshape(nc, CH, B_HEADS), (0, 2, 1))
    return jnp.concatenate([t, jnp.zeros_like(t)], axis=1).reshape(nc * 8, CH)


_CHUNK8 = lambda width: pl.BlockSpec((8, width), lambda i: (i, 0))
_CHUNK4 = lambda a, b: pl.BlockSpec((B_HEADS, a, b), lambda i: (i, 0, 0))


def _gdn_lower(k, aux, auxt, *, name):
    tt = k.shape[0]

    def body(k_ref, aux_ref, auxt_ref, l_ref):
        aux_v = aux_ref[...]
        strict, _ = _chunk_masks()
        for hd in range(B_HEADS):
            kh = k_ref[:, hd * B_DH:(hd + 1) * B_DH].astype(BF16)
            diff = _col(aux_v, B_HEADS + hd) - auxt_ref[hd:hd + 1, :]
            dec = jnp.exp(jnp.where(strict, diff, NEG))
            l_ref[hd] = _col(aux_v, hd) * _dot_nt(kh, kh) * dec

    return pl.pallas_call(
        body, grid=(tt // CH,), in_specs=[_rb(CH, B_W), _rb(CH, LANE), _CHUNK8(CH)],
        out_specs=_CHUNK4(CH, CH),
        out_shape=jax.ShapeDtypeStruct((tt // CH * B_HEADS, CH, CH), F32),
        compiler_params=_cp(("parallel",)), name=name)(k, aux, auxt)


def _tri_inverse(lt, *, name):
    nb = lt.shape[2]

    def body(l_ref, t_ref):
        rowid = lax.broadcasted_iota(jnp.int32, (CH, nb), 0)

        def outer(i, carry):
            def inner(j, acc):
                return acc + l_ref[i, pl.ds(j, 1), :] * t_ref[j]

            acc = lax.fori_loop(0, i, inner, jnp.zeros((CH, nb), F32))
            t_ref[i] = jnp.where(rowid == i, 1.0, 0.0) - acc
            return carry

        lax.fori_loop(0, CH, outer, 0)

    return pl.pallas_call(body, out_shape=jax.ShapeDtypeStruct(lt.shape, F32),
                          in_specs=[pl.BlockSpec(memory_space=pltpu.VMEM)],
                          out_specs=pl.BlockSpec(memory_space=pltpu.VMEM),
                          compiler_params=_cp(), name=name)(lt)


def _gdn_gates(aux_v, aux_last, auxt_ref, hd):
    _, incl = _chunk_masks()
    beta = _col(aux_v, hd)
    gc = _col(aux_v, B_HEADS + hd)
    gl = _col(aux_last, B_HEADS + hd)
    dec = jnp.exp(jnp.where(incl, gc - auxt_ref[hd:hd + 1, :], NEG))
    return beta, gc, gl, jnp.exp(gc), dec


def _gdn_intra(q, k, v, aux, auxt, tinv, *, name):
    tt = q.shape[0]
    nc = tt // CH

    def body(q_ref, k_ref, v_ref, aux_ref, auxt_ref, t_ref, u0_ref, w_ref, qd_ref, kd_ref, qk_ref, gle_ref):
        aux_v = aux_ref[...]
        aux_last = aux_ref[CH - 1:CH, :]
        lane8 = lax.broadcasted_iota(jnp.int32, (8, LANE), 1)
        gle = jnp.zeros((8, LANE), F32)
        for hd in range(B_HEADS):
            sl = slice(hd * B_DH, (hd + 1) * B_DH)
            qh, kh, vh = q_ref[:, sl], k_ref[:, sl], v_ref[:, sl]
            beta, gc, gl, egc, dec = _gdn_gates(aux_v, aux_last, auxt_ref, hd)
            qk_ref[hd] = (_dot_nt(qh.astype(BF16), kh.astype(BF16)) * dec).astype(BF16)
            tinv = t_ref[hd]
            u0_ref[:, sl] = _dot3(tinv, vh * beta)
            w_ref[:, sl] = _dot3(tinv, kh * (beta * egc)).astype(BF16)
            qd_ref[:, sl] = (qh * egc).astype(BF16)
            kd_ref[:, sl] = (kh * jnp.exp(gl - gc)).astype(BF16)
            gle = gle + jnp.where(lane8 == hd, jnp.exp(gl), 0.0)
        gle_ref[...] = gle

    row = _rb(CH, B_W)
    half = jax.ShapeDtypeStruct((tt, B_W), BF16)
    return pl.pallas_call(
        body, grid=(nc,), in_specs=[row, row, row, _rb(CH, LANE), _CHUNK8(CH), _CHUNK4(CH, CH)],
        out_specs=(row, row, row, row, _CHUNK4(CH, CH), _CHUNK8(LANE)),
        out_shape=(jax.ShapeDtypeStruct((tt, B_W), F32), half, half, half,
                   jax.ShapeDtypeStruct((nc * B_HEADS, CH, CH), BF16), jax.ShapeDtypeStruct((nc * 8, LANE), F32)),
        compiler_params=_cp(("parallel",)), name=name)(q, k, v, aux, auxt, tinv)


def _gdn_scan_fwd(u0, w, qd, kd, qk, gle, *, name):
    tt = u0.shape[0]
    nc = tt // CH

    def body(u0_ref, w_ref, qd_ref, kd_ref, qk_ref, gle_ref, o_ref, ss_ref, u_ref, s_ref):
        @pl.when(pl.program_id(0) == 0)
        def _():
            s_ref[...] = jnp.zeros_like(s_ref)

        gle = gle_ref[0:1, :]
        for hd in range(B_HEADS):
            sl = slice(hd * B_DH, (hd + 1) * B_DH)
            st = s_ref[hd]
            ss_ref[hd] = st
            sb = st.astype(BF16)
            ub = (u0_ref[:, sl] - _dot(w_ref[:, sl], sb)).astype(BF16)
            u_ref[:, sl] = ub
            o_ref[:, sl] = _dot(qd_ref[:, sl], sb) + _dot(qk_ref[hd], ub)
            s_ref[hd] = st * _col(gle, hd) + _dot_tn(kd_ref[:, sl], ub)

    row = _rb(CH, B_W)
    return pl.pallas_call(
        body, grid=(nc,), in_specs=[row, row, row, row, _CHUNK4(CH, CH), _CHUNK8(LANE)],
        out_specs=(row, _CHUNK4(B_DH, B_DH), row),
        out_shape=(jax.ShapeDtypeStruct((tt, B_W), F32), jax.ShapeDtypeStruct((nc * B_HEADS, B_DH, B_DH), F32),
                   jax.ShapeDtypeStruct((tt, B_W), BF16)),
        scratch_shapes=[pltpu.VMEM((B_HEADS, B_DH, B_DH), F32)],
        compiler_params=_cp(("arbitrary",)), name=name)(u0, w, qd, kd, qk, gle)


def _gdn_scan_bwd(w, qd, kd, qk, gle, do, *, name):
    tt = w.shape[0]
    nc = tt // CH

    def body(w_ref, qd_ref, kd_ref, qk_ref, gle_ref, do_ref, du_ref, dss_ref, ds_ref):
        @pl.when(pl.program_id(0) == 0)
        def _():
            ds_ref[...] = jnp.zeros_like(ds_ref)

        gle = gle_ref[0:1, :]
        for hd in range(B_HEADS):
            sl = slice(hd * B_DH, (hd + 1) * B_DH)
            dst = ds_ref[hd]
            dss_ref[hd] = dst
            dob = do_ref[:, sl].astype(BF16)
            du = _dot_tn(qk_ref[hd], dob) + _dot(kd_ref[:, sl], dst.astype(BF16))
            du_ref[:, sl] = du
            ds_ref[hd] = _dot_tn(qd_ref[:, sl], dob) + _col(gle, hd) * dst - _dot_tn(w_ref[:, sl], du.astype(BF16))

    rev = lambda width: pl.BlockSpec((CH, width), lambda i: (nc - 1 - i, 0))
    rev4 = lambda a, b: pl.BlockSpec((B_HEADS, a, b), lambda i: (nc - 1 - i, 0, 0))
    return pl.pallas_call(
        body, grid=(nc,),
        in_specs=[rev(B_W), rev(B_W), rev(B_W), rev4(CH, CH), pl.BlockSpec((8, LANE), lambda i: (nc - 1 - i, 0)), rev(B_W)],
        out_specs=(rev(B_W), rev4(B_DH, B_DH)),
        out_shape=(jax.ShapeDtypeStruct((tt, B_W), F32), jax.ShapeDtypeStruct((nc * B_HEADS, B_DH, B_DH), F32)),
        scratch_shapes=[pltpu.VMEM((B_HEADS, B_DH, B_DH), F32)],
        compiler_params=_cp(("arbitrary",)), name=name)(w, qd, kd, qk, gle, do)


def _gdn_bwd(q, k, v, aux, auxt, tinv, u0, w, u, ss, dss, du, do, *, name):
    tt = q.shape[0]
    nc = tt // CH

    def body(q_ref, k_ref, v_ref, aux_ref, auxt_ref, t_ref, u0_ref, w_ref, u_ref, ss_ref, dss_ref, du_ref, do_ref,
             dq_ref, dk_ref, dv_ref, daux_ref):
        aux_v = aux_ref[...]
        aux_last = aux_ref[CH - 1:CH, :]
        lane = lax.broadcasted_iota(jnp.int32, (CH, LANE), 1)
        rowi = lax.broadcasted_iota(jnp.int32, (CH, 1), 0)
        strict, incl = _chunk_masks()
        daux = jnp.zeros((CH, LANE), F32)
        for hd in range(B_HEADS):
            sl = slice(hd * B_DH, (hd + 1) * B_DH)
            qh, kh, vh = q_ref[:, sl], k_ref[:, sl], v_ref[:, sl]
            tinv_h = t_ref[hd]
            beta, gc, gl, egc, dec = _gdn_gates(aux_v, aux_last, auxt_ref, hd)
            ekd, eg_last = jnp.exp(gl - gc), jnp.exp(gl)
            kb, qb = kh.astype(BF16), qh.astype(BF16)
            kk, qk0 = _dot_nt(kb, kb), _dot_nt(qb, kb)
            st, dst = ss_ref[hd], dss_ref[hd]
            sb, dsb = st.astype(BF16), dst.astype(BF16)
            wb, ub, du = w_ref[:, sl], u_ref[:, sl], du_ref[:, sl]
            dob = do_ref[:, sl].astype(BF16)
            dq_dec = _dot_nt(dob, sb)
            dqk = jnp.where(incl, _dot_nt(dob, ub), 0.0)
            dk_dec = _dot_nt(ub, dsb)
            dgl = jnp.sum(jnp.sum(st * dst, axis=-1, keepdims=True), axis=0, keepdims=True) * eg_last
            dw = -_dot_nt(du.astype(BF16), sb)
            drv = _dot3(tinv_h, du, tn=True)
            drk = _dot3(tinv_h, dw, tn=True)
            dl = -(_dot_nt(drv.astype(BF16), u0_ref[:, sl].astype(BF16)) + _dot_nt(drk.astype(BF16), wb))
            dl = jnp.where(strict, dl, 0.0)
            dv_ref[:, sl] = drv * beta
            rk = jnp.sum(drk * kh, axis=-1, keepdims=True)
            dbeta = jnp.sum(drv * vh, axis=-1, keepdims=True) + rk * egc
            dgc = rk * beta * egc
            dk = drk * (beta * egc)
            ldec = dl * dec
            dbeta = dbeta + jnp.sum(ldec * kk, axis=-1, keepdims=True)
            dkk = (ldec * beta).astype(BF16)
            dqk0 = (dqk * dec).astype(BF16)
            ddec = ldec * beta * kk + dqk * (qk0 * dec)
            dq = _dot(dqk0, kb) + dq_dec * egc
            dk = dk + _dot_tn(dqk0, qb) + _dot(dkk, kb) + _dot_tn(dkk, kb) + dk_dec * ekd
            dgc = dgc + jnp.sum(ddec, axis=-1, keepdims=True) - _col_from_rowsum(ddec)
            dgc = dgc + jnp.sum(dq_dec * qh, axis=-1, keepdims=True) * egc
            kd = jnp.sum(dk_dec * kh, axis=-1, keepdims=True) * ekd
            dgc = dgc - kd
            dgc = dgc + jnp.where(rowi == CH - 1, jnp.sum(kd, axis=0, keepdims=True) + dgl, 0.0)
            dq_ref[:, sl] = dq
            dk_ref[:, sl] = dk
            daux = daux + jnp.where(lane == hd, dbeta, 0.0) + jnp.where(lane == B_HEADS + hd, dgc, 0.0)
        daux_ref[...] = daux

    row = _rb(CH, B_W)
    outs = jax.ShapeDtypeStruct((tt, B_W), F32)
    return pl.pallas_call(
        body, grid=(nc,),
        in_specs=[row, row, row, _rb(CH, LANE), _CHUNK8(CH), _CHUNK4(CH, CH), row, row, row,
                  _CHUNK4(B_DH, B_DH), _CHUNK4(B_DH, B_DH), row, row],
        out_specs=(row, row, row, _rb(CH, LANE)),
        out_shape=(outs, outs, outs, jax.ShapeDtypeStruct((tt, LANE), F32)),
        compiler_params=_cp(("parallel",)), name=name)(q, k, v, aux, auxt, tinv, u0, w, u, ss, dss, du, do)


def _col_from_rowsum(m):
    hi, lo = _split(m)
    ones = jnp.ones((CH, LANE), BF16)
    return (_dot_tn(hi, ones) + _dot_tn(lo, ones))[:, 0:1]


def _gdn_post_fwd(o, proj, gn, *, name, tr=256):
    tt = o.shape[0]

    def body(o_ref, z_ref, g_ref, y_ref):
        for hd in range(B_HEADS):
            sl = slice(hd * B_DH, (hd + 1) * B_DH)
            oh = o_ref[:, sl]
            r = lax.rsqrt(jnp.mean(oh * oh, axis=-1, keepdims=True) + EPS)
            y_ref[:, sl] = (oh * r * g_ref[...] * _silu(z_ref[:, sl])).astype(BF16)

    return pl.pallas_call(body, grid=(tt // tr,), in_specs=[_rb(tr, B_W), _rb(tr, B_W, OFF_ZB // B_W), _whole((1, B_DH))],
                          out_specs=_rb(tr, B_W), out_shape=jax.ShapeDtypeStruct((tt, B_W), BF16),
                          compiler_params=_cp(("parallel",)), name=name)(o, proj, gn)


def _gdn_post_bwd(o, proj, gn, dy, *, name, tr=256):
    tt = o.shape[0]

    def body(o_ref, z_ref, g_ref, dy_ref, do_ref, dz_ref, dg_ref):
        @pl.when(pl.program_id(0) == 0)
        def _():
            dg_ref[...] = jnp.zeros_like(dg_ref)

        g = g_ref[...]
        for hd in range(B_HEADS):
            sl = slice(hd * B_DH, (hd + 1) * B_DH)
            oh, zh, dyh = o_ref[:, sl], z_ref[:, sl], dy_ref[:, sl]
            r = lax.rsqrt(jnp.mean(oh * oh, axis=-1, keepdims=True) + EPS)
            a = oh * r
            s = _silu(zh)
            da = dyh * g * s
            dg_ref[0:1, :] += _colsum(dyh * a * s)
            dz_ref[:, sl] = (dyh * a * g * _dsilu(zh)).astype(BF16)
            do_ref[:, sl] = r * (da - a * jnp.mean(da * a, axis=-1, keepdims=True))

    return pl.pallas_call(
        body, grid=(tt // tr,), in_specs=[_rb(tr, B_W), _rb(tr, B_W, OFF_ZB // B_W), _whole((1, B_DH)), _rb(tr, B_W)],
        out_specs=(_rb(tr, B_W), _rb(tr, B_W), _whole((8, B_DH))),
        out_shape=(jax.ShapeDtypeStruct((tt, B_W), F32), jax.ShapeDtypeStruct((tt, B_W), BF16),
                   jax.ShapeDtypeStruct((8, B_DH), F32)),
        compiler_params=_cp(("arbitrary",)), name=name)(o, proj, gn, dy)


def _adamw(parts, w, m, v, *, name, tr=256):
    npart, r, c = parts.shape
    tr = max([t for t in range(8, min(r, tr) + 1, 8) if r % t == 0], default=r)
    c1, c2 = 1.0 - ADAM_B1 ** ADAM_STEP, 1.0 - ADAM_B2 ** ADAM_STEP

    def body(p_ref, w_ref, m_ref, v_ref, g_ref, d_ref, mo_ref, vo_ref):
        g = p_ref[0].astype(F32)
        for i in range(1, npart):
            g = g + p_ref[i].astype(F32)
        mn = ADAM_B1 * m_ref[...] + (1.0 - ADAM_B1) * g
        vn = ADAM_B2 * v_ref[...] + (1.0 - ADAM_B2) * (g * g)
        g_ref[...] = g
        mo_ref[...] = mn
        vo_ref[...] = vn
        d_ref[...] = -ADAM_LR * ((mn / c1) / (jnp.sqrt(vn / c2) + ADAM_EPS) + ADAM_WD * w_ref[...])

    row = pl.BlockSpec((tr, c), lambda i: (i, 0))
    out = jax.ShapeDtypeStruct((r, c), F32)
    return pl.pallas_call(body, grid=(r // tr,), in_specs=[pl.BlockSpec((npart, tr, c), lambda i: (0, i, 0)), row, row, row],
                          out_specs=(row, row, row, row), out_shape=(out, out, out, out),
                          compiler_params=_cp(("parallel",)), name=name)(parts, w, m, v)


def _peer(k):
    x, y, c = lax.axis_index("x"), lax.axis_index("y"), lax.axis_index("c")
    return ((1 - x) if k & 4 else x, (1 - y) if k & 2 else y, (1 - c) if k & 1 else c)


def _my_index():
    return 4 * lax.axis_index("x") + 2 * lax.axis_index("y") + lax.axis_index("c")


def _index_of(p):
    return 4 * p[0] + 2 * p[1] + p[2]


def _all_gather(xs, *, name):
    n = len(xs)

    def body(*refs):
        x_refs, o_refs = refs[:n], refs[n:2 * n]
        send, recv, loc = refs[2 * n:]
        me = _my_index()
        copies = []
        for a in range(n):
            cp = pltpu.make_async_copy(x_refs[a], o_refs[a].at[me], loc.at[a])
            cp.start()
            copies.append(cp)
        rdmas = []
        for a in range(n):
            for k in range(1, N_DEV):
                r = pltpu.make_async_remote_copy(
                    src_ref=x_refs[a], dst_ref=o_refs[a].at[me], send_sem=send.at[a, k - 1], recv_sem=recv.at[a, k - 1],
                    device_id=_peer(k), device_id_type=pl.DeviceIdType.MESH)
                r.start()
                rdmas.append(r)
        for a in range(n):
            for k in range(1, N_DEV):
                pltpu.make_async_remote_copy(
                    src_ref=x_refs[a], dst_ref=o_refs[a].at[_index_of(_peer(k))], send_sem=send.at[a, k - 1],
                    recv_sem=recv.at[a, k - 1], device_id=_peer(k), device_id_type=pl.DeviceIdType.MESH).wait_recv()
        for r in rdmas:
            r.wait_send()
        for cp in copies:
            cp.wait()

    any_spec = pl.BlockSpec(memory_space=pl.ANY)
    return pl.pallas_call(
        body, in_specs=[any_spec] * n, out_specs=tuple([any_spec] * n),
        out_shape=tuple(jax.ShapeDtypeStruct((N_DEV,) + x.shape, x.dtype) for x in xs),
        scratch_shapes=[pltpu.SemaphoreType.DMA((n, N_DEV - 1)), pltpu.SemaphoreType.DMA((n, N_DEV - 1)),
                        pltpu.SemaphoreType.DMA((n,))],
        name=name)(*xs)


_HBM = pl.BlockSpec(memory_space=pltpu.HBM)
_SEM = pl.BlockSpec(memory_space=pltpu.SEMAPHORE)
_EFFECT = pltpu.SideEffectType.DATAFLOW_SIDE_EFFECTING


def _split_copy(src_ref, land_ref, send, recv, a, k, scatter, slot, sending):
    me, peer = _my_index(), _index_of(_peer(k))
    src = src_ref.at[peer if sending else me] if scatter else src_ref
    land = land_ref.at[me if sending else peer]
    if slot is not None:
        land = land.at[slot]
    sem = a * (N_DEV - 1) + k - 1
    return pltpu.make_async_remote_copy(src_ref=src, dst_ref=land, send_sem=send.at[sem], recv_sem=recv.at[sem],
                                        device_id=_peer(k), device_id_type=pl.DeviceIdType.MESH)


def _exchange_start(srcs, lands, after, *, scatter, slot=None, name):
    n = len(srcs)

    def body(*refs):
        src_refs, land_refs = refs[:n], refs[n:2 * n]
        send, recv, token = refs[2 * n + 1], refs[2 * n + 2], refs[-1]
        for a in range(n):
            for k in range(1, N_DEV):
                _split_copy(src_refs[a], land_refs[a], send, recv, a, k, scatter, slot, True).start()
        token[...] = jnp.zeros_like(token)

    hbm = lambda t: pltpu.HBM(t.shape, t.dtype)
    sems = pltpu.SemaphoreType.DMA((n * (N_DEV - 1),))
    out = pl.pallas_call(
        body, name=name,
        out_shape=(sems, sems, *[hbm(t) for t in srcs], *[hbm(t) for t in lands], jax.ShapeDtypeStruct((8, LANE), F32)),
        in_specs=[_HBM] * (2 * n) + [pl.BlockSpec(memory_space=pl.ANY)],
        out_specs=(_SEM, _SEM, *[_HBM] * (2 * n), pl.BlockSpec(memory_space=pltpu.VMEM)),
        input_output_aliases={i: 2 + i for i in range(2 * n)},
        compiler_params=pltpu.CompilerParams(has_side_effects=_EFFECT),
    )(*[pltpu.with_memory_space_constraint(t, pltpu.HBM) for t in (*srcs, *lands)], after)
    return out[0], out[1], out[2:2 + n], out[2 + n:2 + 2 * n], out[-1]


def _exchange_wait(send, recv, srcs, lands, after, *, scatter, slot=None, name):
    n = len(srcs)

    def body(*refs):
        src_refs, land_refs = refs[:n], refs[n:2 * n]
        send_ref, recv_ref = refs[2 * n], refs[2 * n + 1]
        for a in range(n):
            for k in range(1, N_DEV):
                _split_copy(src_refs[a], land_refs[a], send_ref, recv_ref, a, k, scatter, slot, True).wait_send()
                _split_copy(src_refs[a], land_refs[a], send_ref, recv_ref, a, k, scatter, slot, False).wait_recv()

    hbm = lambda t: pltpu.HBM(t.shape, t.dtype)
    out = pl.pallas_call(
        body, name=name, out_shape=(*[hbm(t) for t in srcs], *[hbm(t) for t in lands]),
        in_specs=[_HBM] * (2 * n) + [_SEM, _SEM, pl.BlockSpec(memory_space=pl.ANY)],
        out_specs=tuple([_HBM] * (2 * n)), input_output_aliases={i: i for i in range(2 * n)},
        compiler_params=pltpu.CompilerParams(has_side_effects=_EFFECT),
    )(*srcs, *lands, send, recv, after)
    return out[:n], out[n:]


def _win_to_mine(w):
    pad = jnp.zeros(w.shape[:-1] + (IN_PAD - IN_DIM,), w.dtype)
    return jnp.concatenate([w[..., 3592:5640], w[..., 0:3584], w[..., 3584:3592], pad], axis=-1)


def _win_from_mine(g):
    return jnp.concatenate([g[..., 2048:5632], g[..., 5632:5640], g[..., 0:2048]], axis=-1)


def _pad_rows(a, mult=8):
    r = (-a.shape[0]) % mult
    return a if r == 0 else jnp.concatenate([a, jnp.zeros((r,) + a.shape[1:], a.dtype)], axis=0)


def _lanes(vec, start):
    return jnp.zeros((1, LANE), F32).at[0, start:start + vec.shape[0]].set(vec)


def _small_spec(depth):
    return (("b_ada", (depth, 6 * D)), ("norm1_g", (depth, D)), ("norm2_g", (depth, D)),
            ("rel_table", (depth, A_HEADS, 2 * A_MAX_REL + 1)), ("a_log", (depth, B_HEADS)),
            ("dt_bias", (depth, B_HEADS)), ("gdn_norm_g", (depth, B_DH)), ("final_g", (D,)))


def _pack_small(d, extra, depth):
    spec = _small_spec(depth)
    rows = -(-(sum(math.prod(s) for _, s in spec) + 1) // (8 * LANE)) * 8
    flat = jnp.concatenate([d[n].reshape(-1).astype(F32) for n, _ in spec] + [extra.reshape(-1)])
    flat = jnp.concatenate([flat, jnp.zeros((rows * LANE - flat.shape[0],), F32)])
    return flat.reshape(rows, LANE)


def _unpack_small(p, depth):
    flat = p.reshape(-1)
    out, off = {}, 0
    for n, s in _small_spec(depth):
        sz = math.prod(s)
        out[n] = flat[off:off + sz].reshape(s)
        off += sz
    return out, flat[off]


def kernel(x, c, w_ada, b_ada, norm1_g, norm2_g, w_in, rel_table, w_conv, a_log, dt_bias, gdn_norm_g, w_branch_a, w_branch_b, w_out, w_ffn_in, w_ffn_out, final_g, loss_target, m_w_ada, m_b_ada, m_norm1_g, m_norm2_g, m_w_in, m_rel_table, m_w_conv, m_a_log, m_dt_bias, m_gdn_norm_g, m_w_branch_a, m_w_branch_b, m_w_out, m_w_ffn_in, m_w_ffn_out, m_final_g, v_w_ada, v_b_ada, v_norm1_g, v_norm2_g, v_w_in, v_rel_table, v_w_conv, v_a_log, v_dt_bias, v_gdn_norm_g, v_w_branch_a, v_w_branch_b, v_w_out, v_w_ffn_in, v_w_ffn_out, v_final_g):
    tt = x.shape[1]
    x0 = x[0]
    tgt = loss_target[0]
    me = _my_index()
    depth = w_in.shape[0]

    shards = [w_in.astype(BF16), w_branch_a.astype(BF16), w_branch_b.astype(BF16), w_out.astype(BF16),
              w_ffn_in.astype(BF16), w_ffn_out.astype(BF16), w_conv]
    first = _all_gather([t[0] for t in shards] + [_pad_rows(c)], name="gather_first")
    c_all = first[-1][:, 0, :]

    def unpack(g):
        cols = lambda t: jnp.transpose(t, (1, 0, 2)).reshape(t.shape[1], N_DEV * t.shape[2])
        rows = lambda t: t.reshape(N_DEV * t.shape[1], t.shape[2])
        return dict(win=_win_to_mine(cols(g[0])), wa=cols(g[1]), wb=cols(g[2]), wout=rows(g[3]), wfi=cols(g[4]),
                    wfo=rows(g[5]), wconv=cols(g[6]))

    def gather_start(l, after):
        srcs = [t[l] for t in shards]
        lands = [lax.empty((N_DEV,) + t.shape, t.dtype) for t in srcs]
        return _exchange_start(srcs, lands, after, scatter=False, name=f"gather_start_{l}")

    def gather_wait(l, pending, after):
        send, recv, srcs, lands, _ = pending
        srcs, lands = _exchange_wait(send, recv, srcs, lands, after, scatter=False, name=f"gather_wait_{l}")
        return unpack([lax.dynamic_update_index_in_dim(g, t, me, 0) for g, t in zip(lands, srcs)])

    weights = [unpack(first[:-1])] + [None] * (depth - 1)
    pending = gather_start(1, first[-1]) if depth > 1 else None
    cond = c_all * (1.0 / (1.0 + jnp.exp(-c_all)))
    cond = _pad_rows(cond, 16)

    mod_cols = jnp.stack([_mm(cond, w_ada[l], name="mod_mm")[:N_DEV] for l in range(depth)])
    (g_mod,) = _all_gather([mod_cols], name="gather_mod")
    mod_all = jnp.transpose(g_mod, (1, 2, 0, 3)).reshape(depth, N_DEV, 6 * D)
    mod = lax.dynamic_index_in_dim(mod_all, me, axis=1, keepdims=False) + b_ada
    mods = mod.reshape(depth, 6, 1, D)

    n1g, n2g = norm1_g.reshape(depth, 1, D), norm2_g.reshape(depth, 1, D)
    gng = gdn_norm_g.reshape(depth, 1, B_DH)
    fg = final_g.reshape(1, D)

    saved = []
    tok = pending[-1][0, 0] if pending is not None else 0.0
    xin, h1 = _adaln_fwd(x0, n1g[0], mods[0, 1] + tok, mods[0, 0], name="adaln1_first")
    for l in range(depth):
        sh1, sc1, gt1, sh2, sc2, gt2 = (mods[l, i] for i in range(6))
        wl = weights[l]
        proj = _mm(h1, wl["win"], name="proj_mm", tn=1152)
        kpad = jnp.pad(proj[:, OFF_KA:OFF_KA + A_W].astype(BF16), ((A_PAST * CH, 0), (0, 0)))
        vpad = jnp.pad(proj[:, OFF_VA:OFF_VA + A_W].astype(BF16), ((A_PAST * CH, 0), (0, 0)))
        bias, bias_vjp = jax.vjp(_bias_from_table, rel_table[l])
        ya, lse = _attn_fwd(proj, kpad, vpad, bias, name="attn_fwd")
        par = jnp.concatenate([_lanes(a_log[l], B_HEADS), _lanes(dt_bias[l], B_HEADS), jnp.zeros((6, LANE), F32)], axis=0)
        qn, kn, vn, aux = _gdn_pre_fwd(proj, wl["wconv"], par, name="gdn_pre_fwd")
        auxt = _gc_rows(aux, tt // CH)
        lower = _gdn_lower(kn, aux, auxt, name="gdn_lower")
        tinv = jnp.transpose(_tri_inverse(jnp.transpose(lower, (1, 2, 0)), name="gdn_tri_inverse"), (2, 0, 1))
        u0, wg, qd, kd, qk, gle = _gdn_intra(qn, kn, vn, aux, auxt, tinv, name="gdn_intra")
        og, ss, ug = _gdn_scan_fwd(u0, wg, qd, kd, qk, gle, name="gdn_scan_fwd")
        yb = _gdn_post_fwd(og, proj, gng[l], name="gdn_post_fwd")
        pa = _mm(ya, wl["wa"], out_dtype=BF16, name="branch_a_mm")
        pb = _mm(yb, wl["wb"], out_dtype=BF16, name="branch_b_mm")
        merged = _merge_fwd(proj, pa, pb, name="merge_fwd")
        t1 = _mm(merged, wl["wout"], name="out_mm")
        x2, h2 = _adaln_fwd(xin, n2g[l], sc2, sh2, t1, gt1, name="adaln2_fwd")
        gu = _mm(h2, wl["wfi"], out_dtype=BF16, name="ffn_in_mm", tn=1408)
        act = _swiglu_fwd(gu, name="swiglu_fwd")
        t2 = _mm(act, wl["wfo"], name="ffn_out_mm", tk=1408)
        saved.append(dict(xin=xin, h1=h1, proj=proj, kpad=kpad, vpad=vpad, bias=bias, bias_vjp=bias_vjp, ya=ya, lse=lse,
                          par=par, qn=qn, kn=kn, vn=vn, aux=aux, auxt=auxt, tinv=tinv, ss=ss, og=og, yb=yb, pa=pa, pb=pb,
                          u0=u0, wg=wg, qd=qd, kd=kd, qk=qk, gle=gle, ug=ug,
                          merged=merged, t1=t1, x2=x2, h2=h2, gu=gu, act=act, t2=t2))
        if l + 1 < depth:
            weights[l + 1] = gather_wait(l + 1, pending, t2)
            pending = gather_start(l + 2, weights[l + 1]["wconv"]) if l + 2 < depth else None
            tok = pending[-1][0, 0] if pending is not None else 0.0
            xin, h1 = _adaln_fwd(x2, n1g[l + 1], mods[l + 1, 1] + tok, mods[l + 1, 0], t2, gt2, name="adaln1_fwd")

    s = saved[-1]
    dx, dt2, st = _loss_head(s["x2"], s["t2"], mods[depth - 1, 5], fg, tgt, name="loss_head")
    loss_part = st[4, 0]
    small_g = {"final_g": st[0]}
    dmod_rows = [None] * depth
    for n in ("norm1_g", "norm2_g", "rel_table", "a_log", "dt_bias", "gdn_norm_g"):
        small_g[n] = [None] * depth
    dgt2 = st[3]
    cols_slabs = lambda g: jnp.transpose(g.reshape(g.shape[0], N_DEV, g.shape[1] // N_DEV), (1, 0, 2))
    rows_slabs = lambda g: g.reshape(N_DEV, g.shape[0] // N_DEV, g.shape[1])
    lands = [lax.empty((N_DEV,) + t.shape, t.dtype) for t in shards]
    own, pending_s = [None] * depth, None
    for l in reversed(range(depth)):
        s, wl = saved[l], weights[l]
        sh1, sc1, gt1, sh2, sc2, gt2 = (mods[l, i] for i in range(6))
        gw_fo = _mm(s["act"], dt2, ta=True, out_dtype=BF16, name="ffn_out_dw", tm=1408)
        dact = _mm(dt2, wl["wfo"], tb=True, out_dtype=BF16, name="ffn_out_dx", tn=1408)
        dgu = _swiglu_bwd(s["gu"], dact, name="swiglu_bwd")
        gw_fi = _mm(s["h2"], dgu, ta=True, out_dtype=BF16, name="ffn_in_dw", tn=1408)
        dh2 = _mm(dgu, wl["wfi"], tb=True, name="ffn_in_dx", tk=1408)
        dx, dt1, st2 = _adaln_bwd(s["x2"], n2g[l], sc2, sh2, dh2, dx, s["t1"], gt1, name="adaln2_bwd")
        gw_out = _mm(s["merged"], dt1, ta=True, out_dtype=BF16, name="out_dw")
        dmerged = _mm(dt1, wl["wout"], tb=True, name="out_dx")
        dgates, dpa, dpb = _merge_bwd(s["proj"], s["pa"], s["pb"], dmerged, name="merge_bwd")
        gw_a = _mm(s["ya"], dpa, ta=True, out_dtype=BF16, name="branch_a_dw")
        gw_b = _mm(s["yb"], dpb, ta=True, out_dtype=BF16, name="branch_b_dw")
        dya = _mm(dpa, wl["wa"], tb=True, name="branch_a_dx")
        dyb = _mm(dpb, wl["wb"], tb=True, name="branch_b_dx")
        dqa, dkpad, dvpad, dbias = _attn_bwd(s["proj"], s["kpad"], s["vpad"], s["bias"], s["ya"], s["lse"], dya,
                                             name="attn_bwd")
        small_g["rel_table"][l] = s["bias_vjp"](dbias)[0]
        dog, dz, dgn = _gdn_post_bwd(s["og"], s["proj"], gng[l], dyb, name="gdn_post_bwd")
        small_g["gdn_norm_g"][l] = dgn[0]
        dug, dss = _gdn_scan_bwd(s["wg"], s["qd"], s["kd"], s["qk"], s["gle"], dog, name="gdn_scan_bwd")
        dqn, dkn, dvn, daux = _gdn_bwd(s["qn"], s["kn"], s["vn"], s["aux"], s["auxt"], s["tinv"], s["u0"], s["wg"],
                                       s["ug"], s["ss"], dss, dug, dog, name="gdn_bwd")
        dqkv, dba, dwc, dpar = _gdn_pre_bwd(s["proj"], wl["wconv"], s["par"], dqn, dkn, dvn, daux, name="gdn_pre_bwd")
        small_g["a_log"][l] = dpar[0, B_HEADS:2 * B_HEADS]
        small_g["dt_bias"][l] = dpar[1, B_HEADS:2 * B_HEADS]
        dproj = jnp.concatenate([dgates, dqa.astype(BF16), dkpad[A_PAST * CH:].astype(BF16),
                                 dvpad[A_PAST * CH:].astype(BF16), dqkv, dz, dba], axis=1)
        gw_in = _mm(s["h1"], dproj, ta=True, out_dtype=BF16, name="proj_dw", tn=1152)
        dh1 = _mm(dproj, wl["win"], tb=True, name="proj_dx", tk=1152)
        srcs = [cols_slabs(_win_from_mine(gw_in)), cols_slabs(gw_a), cols_slabs(gw_b), rows_slabs(gw_out),
                cols_slabs(gw_fi), rows_slabs(gw_fo), cols_slabs(dwc[0:CONV_K])]
        if pending_s is not None:
            done, lands = _exchange_wait(*pending_s[:4], gw_in, scatter=True, slot=l + 1, name=f"scatter_wait_{l + 1}")
            own[l + 1] = [lax.dynamic_index_in_dim(t, me, 0, keepdims=False) for t in done]
        pending_s = _exchange_start(srcs, lands, dwc, scatter=True, slot=l, name=f"scatter_start_{l}")
        sc1 = sc1 + pending_s[-1][0, 0]
        if l > 0:
            p = saved[l - 1]
            dx, dt2, st1 = _adaln_bwd(s["xin"], n1g[l], sc1, sh1, dh1, dx, p["t2"], mods[l - 1, 5], name="adaln1_bwd")
        else:
            dx, st1 = _adaln_bwd(s["xin"], n1g[l], sc1, sh1, dh1, dx, name="adaln1_bwd_first")
        small_g["norm1_g"][l], small_g["norm2_g"][l] = st1[0], st2[0]
        dmod_rows[l] = jnp.concatenate([st1[2], st1[1], st2[3], st2[2], st2[1], dgt2])
        if l > 0:
            dgt2 = st1[3]
    grad_x = dx[None]

    small_local = {n: (jnp.stack(vs) if isinstance(vs, list) else vs) for n, vs in small_g.items()}
    small_local["b_ada"] = jnp.stack(dmod_rows)
    (g_small,) = _all_gather([_pack_small(small_local, loss_part, depth)], name="gather_small")
    wsm = _pack_small(dict(b_ada=b_ada, norm1_g=norm1_g, norm2_g=norm2_g, rel_table=rel_table, a_log=a_log,
                           dt_bias=dt_bias, gdn_norm_g=gdn_norm_g, final_g=final_g), jnp.zeros((1,), F32), depth)
    msm = _pack_small(dict(b_ada=m_b_ada, norm1_g=m_norm1_g, norm2_g=m_norm2_g, rel_table=m_rel_table, a_log=m_a_log,
                           dt_bias=m_dt_bias, gdn_norm_g=m_gdn_norm_g, final_g=m_final_g), jnp.zeros((1,), F32), depth)
    vsm = _pack_small(dict(b_ada=v_b_ada, norm1_g=v_norm1_g, norm2_g=v_norm2_g, rel_table=v_rel_table, a_log=v_a_log,
                           dt_bias=v_dt_bias, gdn_norm_g=v_gdn_norm_g, final_g=v_final_g), jnp.ones((1,), F32), depth)
    sm = [_unpack_small(t, depth) for t in _adamw(g_small, wsm, msm, vsm, name="adamw_small")]
    loss = sm[0][1]

    dmod_all = g_small.reshape(N_DEV, -1)[:, :depth * 6 * D].reshape(N_DEV, depth, 6 * D)
    dmod_mine = lax.dynamic_slice_in_dim(dmod_all, me * (6 * D // N_DEV), 6 * D // N_DEV, axis=2)
    g_ada = jnp.stack([_mm(cond, _pad_rows(dmod_mine[:, l], 16), ta=True, name="ada_dw") for l in range(depth)])

    done, lands = _exchange_wait(*pending_s[:4], g_ada, scatter=True, slot=0, name="scatter_wait_0")
    own[0] = [lax.dynamic_index_in_dim(t, me, 0, keepdims=False) for t in done]
    r_in, r_a, r_b, r_out, r_fi, r_fo, r_conv = [
        lax.dynamic_update_index_in_dim(land, jnp.stack([own[l][a] for l in range(depth)]), me, 0)
        for a, land in enumerate(lands)]

    def upd(parts, w, m, v, name):
        shp = w.shape
        two = lambda a: a.reshape(-1, shp[-1])
        return [o.reshape(shp) for o in _adamw(parts.reshape(parts.shape[0], -1, shp[-1]), two(w), two(m), two(v), name=name)]

    res = {
        "w_ada": upd(g_ada[None], w_ada, m_w_ada, v_w_ada, "adamw_w_ada"),
        "w_in": upd(r_in, w_in, m_w_in, v_w_in, "adamw_w_in"),
        "w_conv": upd(r_conv, w_conv, m_w_conv, v_w_conv, "adamw_w_conv"),
        "w_branch_a": upd(r_a, w_branch_a, m_w_branch_a, v_w_branch_a, "adamw_w_branch_a"),
        "w_branch_b": upd(r_b, w_branch_b, m_w_branch_b, v_w_branch_b, "adamw_w_branch_b"),
        "w_out": upd(r_out, w_out, m_w_out, v_w_out, "adamw_w_out"),
        "w_ffn_in": upd(r_fi, w_ffn_in, m_w_ffn_in, v_w_ffn_in, "adamw_w_ffn_in"),
        "w_ffn_out": upd(r_fo, w_ffn_out, m_w_ffn_out, v_w_ffn_out, "adamw_w_ffn_out"),
    }
    for n, _ in _small_spec(depth):
        res[n] = [sm[i][0][n] for i in range(4)]
    order = ("w_ada", "b_ada", "norm1_g", "norm2_g", "w_in", "rel_table", "w_conv", "a_log", "dt_bias", "gdn_norm_g",
             "w_branch_a", "w_branch_b", "w_out", "w_ffn_in", "w_ffn_out", "final_g")
    return (loss, grad_x, *[res[n][0] for n in order], *[res[n][1] for n in order],
            *[res[n][2] for n in order], *[res[n][3] for n in order])
```

```python
import functools
import math

import jax
import jax.numpy as jnp
from jax import lax
from jax.experimental import pallas as pl
from jax.experimental.pallas import tpu as pltpu

F32 = jnp.float32
BF16 = jnp.bfloat16
HI = lax.Precision.HIGHEST

N_DEV = 8
D = 1024
DEPTH = 4
CH = 64
EPS = 1e-6
A_HEADS, A_DH = 8, 64
A_W = A_HEADS * A_DH
A_PAST = 8
A_MAX_REL = 128
QB = 256
KB = QB + A_PAST * CH
B_HEADS, B_DH = 4, 128
B_W = B_HEADS * B_DH
CONV_K = 4
FF = 2816
IN_DIM = 5640
IN_PAD = 5760
LANE = 128
NEG = -1e30
VMEM_LIMIT = 48 * 1024 * 1024

ADAM_LR, ADAM_B1, ADAM_B2, ADAM_EPS, ADAM_WD, ADAM_STEP = 0.001, 0.9, 0.999, 1e-08, 0.01, 10

OFF_GA, OFF_GB, OFF_QA, OFF_KA, OFF_VA, OFF_QB, OFF_KB, OFF_VB, OFF_ZB, OFF_BA = (
    0, 1024, 2048, 2560, 3072, 3584, 4096, 4608, 5120, 5632)


def _cp(sem=None):
    return pltpu.CompilerParams(dimension_semantics=sem, vmem_limit_bytes=VMEM_LIMIT)


def _tile(n, pref):
    if n <= pref:
        return n
    best = None
    for t in range(LANE, pref + 1, LANE):
        if n % t == 0:
            best = t
    assert best is not None, (n, pref)
    return best


def _sigmoid(x):
    return 1.0 / (1.0 + jnp.exp(-x))


def _silu(x):
    return x * _sigmoid(x)


def _dsilu(x):
    s = _sigmoid(x)
    return s * (1.0 + x * (1.0 - s))


def _dot(a, b, prec=None):
    return jnp.dot(a, b, preferred_element_type=F32, precision=prec)


def _dot_nt(a, b, prec=None):
    return lax.dot_general(a, b, (((1,), (1,)), ((), ())), preferred_element_type=F32, precision=prec)


def _dot_tn(a, b, prec=None):
    return lax.dot_general(a, b, (((0,), (0,)), ((), ())), preferred_element_type=F32, precision=prec)


def _mm(a, b, *, ta=False, tb=False, out_dtype=F32, name, tm=1024, tn=1024, tk=1024):
    m, k = (a.shape[1], a.shape[0]) if ta else a.shape
    n = b.shape[0] if tb else b.shape[1]
    assert k == (b.shape[1] if tb else b.shape[0]), (a.shape, b.shape, ta, tb)
    tm, tn, tk = _tile(m, tm), _tile(n, tn), _tile(k, tk)
    nk = k // tk
    dn = (((0 if ta else 1,), (1 if tb else 0,)), ((), ()))

    def body(a_ref, b_ref, o_ref, *acc):
        part = lax.dot_general(a_ref[...].astype(BF16), b_ref[...].astype(BF16), dn, preferred_element_type=F32)
        if nk == 1:
            o_ref[...] = part.astype(out_dtype)
            return
        acc_ref, kk = acc[0], pl.program_id(2)

        @pl.when(kk == 0)
        def _():
            acc_ref[...] = part

        @pl.when(kk > 0)
        def _():
            acc_ref[...] += part

        @pl.when(kk == nk - 1)
        def _():
            o_ref[...] = acc_ref[...].astype(out_dtype)

    a_spec = pl.BlockSpec((tk, tm), lambda i, j, q: (q, i)) if ta else pl.BlockSpec((tm, tk), lambda i, j, q: (i, q))
    b_spec = pl.BlockSpec((tn, tk), lambda i, j, q: (j, q)) if tb else pl.BlockSpec((tk, tn), lambda i, j, q: (q, j))
    return pl.pallas_call(
        body, grid=(m // tm, n // tn, nk), in_specs=[a_spec, b_spec],
        out_specs=pl.BlockSpec((tm, tn), lambda i, j, q: (i, j)),
        out_shape=jax.ShapeDtypeStruct((m, n), out_dtype),
        scratch_shapes=[pltpu.VMEM((tm, tn), F32)] if nk > 1 else [],
        compiler_params=_cp(("parallel", "parallel", "arbitrary")), name=name)(a, b)


def _rb(tr, width, cb=0):
    return pl.BlockSpec((tr, width), lambda i: (i, cb))


def _whole(shape):
    nd = len(shape)
    return pl.BlockSpec(shape, lambda i: (0,) * nd)


def _colsum(v):
    return jnp.sum(v, axis=0, keepdims=True)


def _adaln_fwd(x, g, sc, sh, t=None, gt=None, *, name, tr=256):
    tt = x.shape[0]
    res = t is not None

    def body(*refs):
        if res:
            x_ref, t_ref, gt_ref, g_ref, sc_ref, sh_ref, xo_ref, h_ref = refs
            xv = x_ref[...] + gt_ref[...] * t_ref[...]
            xo_ref[...] = xv
        else:
            x_ref, g_ref, sc_ref, sh_ref, h_ref = refs
            xv = x_ref[...]
        r = lax.rsqrt(jnp.mean(xv * xv, axis=-1, keepdims=True) + EPS)
        h_ref[...] = ((xv * r * g_ref[...]) * (1.0 + sc_ref[...]) + sh_ref[...]).astype(BF16)

    row, vec = _rb(tr, D), _whole((1, D))
    if res:
        ins, in_specs = (x, t, gt, g, sc, sh), [row, row, vec, vec, vec, vec]
        out_shape = (jax.ShapeDtypeStruct((tt, D), F32), jax.ShapeDtypeStruct((tt, D), BF16))
        out_specs = (row, row)
    else:
        ins, in_specs = (x, g, sc, sh), [row, vec, vec, vec]
        out_shape, out_specs = jax.ShapeDtypeStruct((tt, D), BF16), row
    out = pl.pallas_call(body, grid=(tt // tr,), in_specs=in_specs, out_specs=out_specs, out_shape=out_shape,
                         compiler_params=_cp(("parallel",)), name=name)(*ins)
    return out if res else (x, out)


def _adaln_bwd(x, g, sc, sh, dh, dx_in, t=None, gt=None, *, name, tr=256):
    tt = x.shape[0]
    res = t is not None

    def body(*refs):
        if res:
            x_ref, g_ref, sc_ref, sh_ref, dh_ref, dxi_ref, t_ref, gt_ref, dx_ref, dt_ref, st_ref = refs
        else:
            x_ref, g_ref, sc_ref, sh_ref, dh_ref, dxi_ref, dx_ref, st_ref = refs

        @pl.when(pl.program_id(0) == 0)
        def _():
            st_ref[...] = jnp.zeros_like(st_ref)

        xv, dh = x_ref[...], dh_ref[...]
        r = lax.rsqrt(jnp.mean(xv * xv, axis=-1, keepdims=True) + EPS)
        nrm = xv * r
        y = nrm * g_ref[...]
        dy = dh * (1.0 + sc_ref[...])
        dn = dy * g_ref[...]
        dx = dxi_ref[...] + r * (dn - nrm * jnp.mean(dn * nrm, axis=-1, keepdims=True))
        dx_ref[...] = dx
        st_ref[0:1, :] += _colsum(dy * nrm)
        st_ref[1:2, :] += _colsum(dh * y)
        st_ref[2:3, :] += _colsum(dh)
        if res:
            dt_ref[...] = (gt_ref[...] * dx).astype(BF16)
            st_ref[3:4, :] += _colsum(dx * t_ref[...])

    row, vec, st = _rb(tr, D), _whole((1, D)), _whole((8, D))
    ins, in_specs = [x, g, sc, sh, dh, dx_in], [row, vec, vec, vec, row, row]
    out_shape, out_specs = [jax.ShapeDtypeStruct((tt, D), F32)], [row]
    if res:
        ins += [t, gt]
        in_specs += [row, vec]
        out_shape.append(jax.ShapeDtypeStruct((tt, D), BF16))
        out_specs.append(row)
    out_shape.append(jax.ShapeDtypeStruct((8, D), F32))
    out_specs.append(st)
    return pl.pallas_call(body, grid=(tt // tr,), in_specs=in_specs, out_specs=tuple(out_specs),
                          out_shape=tuple(out_shape), compiler_params=_cp(("arbitrary",)), name=name)(*ins)


def _loss_head(x, t, gt, fg, tgt, *, name, tr=256):
    tt = x.shape[0]

    def body(x_ref, t_ref, gt_ref, fg_ref, tgt_ref, dx_ref, dt_ref, st_ref):
        @pl.when(pl.program_id(0) == 0)
        def _():
            st_ref[...] = jnp.zeros_like(st_ref)

        tv = t_ref[...]
        xv = x_ref[...] + gt_ref[...] * tv
        r = lax.rsqrt(jnp.mean(xv * xv, axis=-1, keepdims=True) + EPS)
        nrm = xv * r
        err = nrm * fg_ref[...] - tgt_ref[...]
        st_ref[4:5, :] += 0.5 * jnp.sum(jnp.mean(err * err, axis=-1, keepdims=True), axis=0, keepdims=True)
        dy = err * (1.0 / D)
        dn = dy * fg_ref[...]
        dx = r * (dn - nrm * jnp.mean(dn * nrm, axis=-1, keepdims=True))
        dx_ref[...] = dx
        dt_ref[...] = (gt_ref[...] * dx).astype(BF16)
        st_ref[0:1, :] += _colsum(dy * nrm)
        st_ref[3:4, :] += _colsum(dx * tv)

    row, vec = _rb(tr, D), _whole((1, D))
    return pl.pallas_call(
        body, grid=(tt // tr,), in_specs=[row, row, vec, vec, row], out_specs=(row, row, _whole((8, D))),
        out_shape=(jax.ShapeDtypeStruct((tt, D), F32), jax.ShapeDtypeStruct((tt, D), BF16),
                   jax.ShapeDtypeStruct((8, D), F32)),
        compiler_params=_cp(("arbitrary",)), name=name)(x, t, gt, fg, tgt)


def _merge_fwd(proj, pa, pb, *, name, tr=256):
    tt = pa.shape[0]

    def body(ga_ref, gb_ref, pa_ref, pb_ref, o_ref):
        o_ref[...] = (_sigmoid(ga_ref[...]) * pa_ref[...].astype(F32)
                      + _sigmoid(gb_ref[...]) * pb_ref[...].astype(F32)).astype(BF16)

    row = _rb(tr, D)
    return pl.pallas_call(body, grid=(tt // tr,), in_specs=[_rb(tr, D, 0), _rb(tr, D, 1), row, row], out_specs=row,
                          out_shape=jax.ShapeDtypeStruct((tt, D), BF16), compiler_params=_cp(("parallel",)),
                          name=name)(proj, proj, pa, pb)


def _merge_bwd(proj, pa, pb, dm, *, name, tr=256):
    tt = pa.shape[0]

    def body(ga_ref, gb_ref, pa_ref, pb_ref, dm_ref, dg_ref, dpa_ref, dpb_ref):
        dm_v = dm_ref[...]
        sa, sb = _sigmoid(ga_ref[...]), _sigmoid(gb_ref[...])
        dpa_ref[...] = (dm_v * sa).astype(BF16)
        dpb_ref[...] = (dm_v * sb).astype(BF16)
        dg_ref[:, 0:D] = (dm_v * pa_ref[...].astype(F32) * sa * (1.0 - sa)).astype(BF16)
        dg_ref[:, D:2 * D] = (dm_v * pb_ref[...].astype(F32) * sb * (1.0 - sb)).astype(BF16)

    row = _rb(tr, D)
    return pl.pallas_call(
        body, grid=(tt // tr,), in_specs=[_rb(tr, D, 0), _rb(tr, D, 1), row, row, row],
        out_specs=(_rb(tr, 2 * D), row, row),
        out_shape=(jax.ShapeDtypeStruct((tt, 2 * D), BF16), jax.ShapeDtypeStruct((tt, D), BF16),
                   jax.ShapeDtypeStruct((tt, D), BF16)),
        compiler_params=_cp(("parallel",)), name=name)(proj, proj, pa, pb, dm)


def _swiglu_fwd(gu, *, name, tr=256):
    tt = gu.shape[0]

    def body(g_ref, u_ref, o_ref):
        o_ref[...] = (_silu(g_ref[...].astype(F32)) * u_ref[...].astype(F32)).astype(BF16)

    return pl.pallas_call(body, grid=(tt // tr,), in_specs=[_rb(tr, FF, 0), _rb(tr, FF, 1)], out_specs=_rb(tr, FF),
                          out_shape=jax.ShapeDtypeStruct((tt, FF), BF16), compiler_params=_cp(("parallel",)),
                          name=name)(gu, gu)


def _swiglu_bwd(gu, dact, *, name, tr=256):
    tt = gu.shape[0]

    def body(g_ref, u_ref, da_ref, o_ref):
        gv, da = g_ref[...].astype(F32), da_ref[...].astype(F32)
        o_ref[:, 0:FF] = (da * u_ref[...].astype(F32) * _dsilu(gv)).astype(BF16)
        o_ref[:, FF:2 * FF] = (da * _silu(gv)).astype(BF16)

    return pl.pallas_call(body, grid=(tt // tr,), in_specs=[_rb(tr, FF, 0), _rb(tr, FF, 1), _rb(tr, FF)],
                          out_specs=_rb(tr, 2 * FF), out_shape=jax.ShapeDtypeStruct((tt, 2 * FF), BF16),
                          compiler_params=_cp(("parallel",)), name=name)(gu, gu, dact)


def _bias_from_table(table):
    lw = 1152
    n_hi = KB - A_MAX_REL
    w = jnp.concatenate([
        jnp.broadcast_to(table[:, 2 * A_MAX_REL:], (A_HEADS, n_hi)),
        jnp.flip(table[:, 1:2 * A_MAX_REL], axis=1),
        jnp.broadcast_to(table[:, 0:1], (A_HEADS, lw - n_hi - (2 * A_MAX_REL - 1)))], axis=1)
    flat = jnp.broadcast_to(w[:, None, :], (A_HEADS, QB, lw)).reshape(A_HEADS, QB * lw)
    skew = flat[:, :QB * (lw - 1)].reshape(A_HEADS, QB, lw - 1)
    bias = skew[:, :, QB - 1:QB - 1 + KB]
    qc = jnp.arange(QB)[:, None] // CH + A_PAST
    kc = jnp.arange(KB)[None, :] // CH
    inband = (kc <= qc) & (kc >= qc - A_PAST)
    return jnp.where(inband[None], bias, NEG)


def _attn_fwd(proj, kpad, vpad, bias, *, name):
    tt = proj.shape[0]

    def body(q_ref, k_ref, v_ref, b_ref, o_ref, l_ref):
        q0 = pl.multiple_of(pl.program_id(1) * QB, QB)
        q = q_ref[...]
        k = k_ref[pl.ds(q0, KB), :]
        v = v_ref[pl.ds(q0, KB), :]
        lane = lax.broadcasted_iota(jnp.int32, (QB, LANE), 1)
        valid = (lax.broadcasted_iota(jnp.int32, (QB, KB), 1) + q0) >= A_PAST * CH
        o = jnp.zeros((QB, LANE), F32)
        lse = jnp.zeros((QB, LANE), F32)
        for a in range(2):
            hm = (lane >= A_DH * a) & (lane < A_DH * (a + 1))
            s = _dot_nt(jnp.where(hm, q, 0.0).astype(BF16), k) * (A_DH ** -0.5) + b_ref[a]
            s = jnp.where(valid, s, NEG)
            m = jnp.max(s, axis=-1, keepdims=True)
            p = jnp.exp(s - m)
            l = jnp.sum(p, axis=-1, keepdims=True)
            o = jnp.where(hm, _dot((p / l).astype(BF16), v), o)
            lse = jnp.where(hm, m + jnp.log(l), lse)
        o_ref[...] = o.astype(BF16)
        l_ref[...] = lse

    kv = pl.BlockSpec((tt + A_PAST * CH, LANE), lambda h, i: (0, h))
    blk = pl.BlockSpec((QB, LANE), lambda h, i: (i, h))
    return pl.pallas_call(
        body, grid=(A_W // LANE, tt // QB),
        in_specs=[pl.BlockSpec((QB, LANE), lambda h, i: (i, OFF_QA // LANE + h)), kv, kv,
                  pl.BlockSpec((2, QB, KB), lambda h, i: (h, 0, 0))],
        out_specs=(blk, blk),
        out_shape=(jax.ShapeDtypeStruct((tt, A_W), BF16), jax.ShapeDtypeStruct((tt, A_W), F32)),
        compiler_params=_cp(("parallel", "parallel")), name=name)(proj, kpad, vpad, bias)


def _attn_bwd(proj, kpad, vpad, bias, o, lse, do, *, name):
    tt = proj.shape[0]
    nq = tt // QB

    def body(q_ref, k_ref, v_ref, b_ref, o_ref, l_ref, do_ref, dq_ref, dko_ref, dvo_ref, db_ref, dk_ref, dv_ref):
        @pl.when(pl.program_id(1) == 0)
        def _():
            dk_ref[...] = jnp.zeros_like(dk_ref)
            dv_ref[...] = jnp.zeros_like(dv_ref)
            db_ref[...] = jnp.zeros_like(db_ref)

        q0 = pl.multiple_of(pl.program_id(1) * QB, QB)
        q, do_v, lse = q_ref[...], do_ref[...], l_ref[...]
        k = k_ref[pl.ds(q0, KB), :]
        v = v_ref[pl.ds(q0, KB), :]
        dsum = do_v * o_ref[...].astype(F32)
        lane = lax.broadcasted_iota(jnp.int32, (QB, LANE), 1)
        valid = (lax.broadcasted_iota(jnp.int32, (QB, KB), 1) + q0) >= A_PAST * CH
        dq = jnp.zeros((QB, LANE), F32)
        dk = jnp.zeros((KB, LANE), F32)
        dv = jnp.zeros((KB, LANE), F32)
        for a in range(2):
            hm = (lane >= A_DH * a) & (lane < A_DH * (a + 1))
            qa = jnp.where(hm, q, 0.0).astype(BF16)
            doa = jnp.where(hm, do_v, 0.0).astype(BF16)
            s = _dot_nt(qa, k) * (A_DH ** -0.5) + b_ref[a]
            s = jnp.where(valid, s, NEG)
            lse_a = jnp.max(jnp.where(hm, lse, NEG), axis=-1, keepdims=True)
            p = jnp.exp(s - lse_a)
            dp = _dot_nt(doa, v)
            dsum_a = jnp.sum(jnp.where(hm, dsum, 0.0), axis=-1, keepdims=True)
            ds = p * (dp - dsum_a)
            db_ref[a] += ds
            dsb = (ds * (A_DH ** -0.5)).astype(BF16)
            dq = jnp.where(hm, _dot(dsb, k), dq)
            dk += _dot_tn(dsb, qa)
            dv += _dot_tn(p.astype(BF16), doa)
        dq_ref[...] = dq.astype(BF16)
        dk_ref[pl.ds(q0, KB), :] += dk
        dv_ref[pl.ds(q0, KB), :] += dv

        @pl.when(pl.program_id(1) == nq - 1)
        def _():
            dko_ref[...] = dk_ref[A_PAST * CH:, :].astype(BF16)
            dvo_ref[...] = dv_ref[A_PAST * CH:, :].astype(BF16)

    kv = pl.BlockSpec((tt + A_PAST * CH, LANE), lambda h, i: (0, h))
    blk = pl.BlockSpec((QB, LANE), lambda h, i: (i, h))
    col = pl.BlockSpec((tt, LANE), lambda h, i: (0, h))
    bsp = pl.BlockSpec((2, QB, KB), lambda h, i: (h, 0, 0))
    out = jax.ShapeDtypeStruct((tt, A_W), BF16)
    return pl.pallas_call(
        body, grid=(A_W // LANE, nq),
        in_specs=[pl.BlockSpec((QB, LANE), lambda h, i: (i, OFF_QA // LANE + h)), kv, kv, bsp, blk, blk, blk],
        out_specs=(blk, col, col, bsp),
        out_shape=(out, out, out, jax.ShapeDtypeStruct((A_HEADS, QB, KB), F32)),
        scratch_shapes=[pltpu.VMEM((tt + A_PAST * CH, LANE), F32), pltpu.VMEM((tt + A_PAST * CH, LANE), F32)],
        compiler_params=_cp(("parallel", "arbitrary")), name=name)(proj, kpad, vpad, bias, o, lse, do)


GTR = 256


def _taps(w_ref, grp):
    return [w_ref[j:j + 1, grp * B_W:(grp + 1) * B_W] for j in range(CONV_K)]


def _shifts(xe, rows):
    return [xe[8:8 + rows]] + [pltpu.roll(xe, s, 0)[8:8 + rows] for s in range(1, CONV_K)]


def _conv(shifts, taps):
    acc = taps[CONV_K - 1] * shifts[0]
    for s in range(1, CONV_K):
        acc = acc + taps[CONV_K - 1 - s] * shifts[s]
    return acc


def _qk_scale(grp):
    return B_DH ** -0.5 if grp == 0 else 1.0


def _act_fwd(c, grp):
    y = _silu(c)
    if grp == 2:
        return y
    parts = []
    for hd in range(B_HEADS):
        yh = y[:, hd * B_DH:(hd + 1) * B_DH]
        parts.append(yh * (lax.rsqrt(jnp.sum(yh * yh, axis=-1, keepdims=True) + EPS) * _qk_scale(grp)))
    return jnp.concatenate(parts, axis=1)


def _act_bwd(c, dy, grp):
    if grp == 2:
        return dy * _dsilu(c)
    y = _silu(c)
    parts = []
    for hd in range(B_HEADS):
        yh = y[:, hd * B_DH:(hd + 1) * B_DH]
        r = lax.rsqrt(jnp.sum(yh * yh, axis=-1, keepdims=True) + EPS)
        dyh = dy[:, hd * B_DH:(hd + 1) * B_DH] * _qk_scale(grp)
        parts.append(r * dyh - yh * (r * r * r) * jnp.sum(dyh * yh, axis=-1, keepdims=True))
    return jnp.concatenate(parts, axis=1) * _dsilu(c)


def _chunk_tri(n, upper=False):
    r = lax.broadcasted_iota(jnp.int32, (n, n), 0)
    c = lax.broadcasted_iota(jnp.int32, (n, n), 1)
    same = (r // CH) == (c // CH)
    return jnp.where(same & ((r <= c) if upper else (r >= c)), 1.0, 0.0).astype(F32)


def _gate_rows(ba, par_ref):
    lane = lax.broadcasted_iota(jnp.int32, ba.shape, 1)
    z = ba + par_ref[1:2, :]
    sp = jnp.maximum(z, 0.0) + jnp.log(1.0 + jnp.exp(-jnp.abs(z)))
    g = -jnp.exp(par_ref[0:1, :]) * sp
    return jnp.where(lane < B_HEADS, _sigmoid(ba), jnp.where(lane < 2 * B_HEADS, g, 0.0)), z


def _prev8(cb):
    return pl.BlockSpec((8, B_W), lambda i: (jnp.maximum(i * (GTR // 8) - 1, 0), cb))


def _next8(cb, nb):
    return pl.BlockSpec((8, B_W), lambda i: (jnp.minimum((i + 1) * (GTR // 8), nb * (GTR // 8) - 1), cb))


def _gdn_pre_fwd(proj, wconv, par, *, name):
    tt = proj.shape[0]

    def body(q_ref, k_ref, v_ref, qh_ref, kh_ref, vh_ref, ba_ref, w_ref, par_ref, qo_ref, ko_ref, vo_ref, aux_ref):
        first = pl.program_id(0) == 0
        for grp, (x_ref, h_ref, o_ref) in enumerate(((q_ref, qh_ref, qo_ref), (k_ref, kh_ref, ko_ref),
                                                     (v_ref, vh_ref, vo_ref))):
            xe = jnp.concatenate([jnp.where(first, 0.0, h_ref[...]), x_ref[...]], axis=0)
            o_ref[...] = _act_fwd(_conv(_shifts(xe, GTR), _taps(w_ref, grp)), grp)
        bg, _ = _gate_rows(ba_ref[...], par_ref)
        lane = lax.broadcasted_iota(jnp.int32, bg.shape, 1)
        aux_ref[...] = jnp.where(lane < B_HEADS, bg, _dot(_chunk_tri(GTR), bg, HI))

    col = lambda off: _rb(GTR, B_W, off // B_W)
    outs = jax.ShapeDtypeStruct((tt, B_W), F32)
    return pl.pallas_call(
        body, grid=(tt // GTR,),
        in_specs=[col(OFF_QB), col(OFF_KB), col(OFF_VB), _prev8(OFF_QB // B_W), _prev8(OFF_KB // B_W),
                  _prev8(OFF_VB // B_W), _rb(GTR, LANE, OFF_BA // LANE), _whole((CONV_K, 3 * B_W)),
                  _whole((8, LANE))],
        out_specs=(_rb(GTR, B_W), _rb(GTR, B_W), _rb(GTR, B_W), _rb(GTR, LANE)),
        out_shape=(outs, outs, outs, jax.ShapeDtypeStruct((tt, LANE), F32)),
        compiler_params=_cp(("parallel",)), name=name)(proj, proj, proj, proj, proj, proj, proj, wconv, par)


def _gdn_pre_bwd(proj, wconv, par, dq, dk, dv, daux, *, name):
    tt = proj.shape[0]
    nb = tt // GTR

    def body(q_ref, k_ref, v_ref, qh_ref, kh_ref, vh_ref, qn_ref, kn_ref, vn_ref, ba_ref, w_ref, par_ref,
             dq_ref, dk_ref, dv_ref, dqn_ref, dkn_ref, dvn_ref, daux_ref, dx_ref, dba_ref, dw_ref, dpar_ref):
        i = pl.program_id(0)
        first, last = i == 0, i == nb - 1

        @pl.when(first)
        def _():
            dw_ref[...] = jnp.zeros_like(dw_ref)
            dpar_ref[...] = jnp.zeros_like(dpar_ref)

        groups = ((q_ref, qh_ref, qn_ref, dq_ref, dqn_ref), (k_ref, kh_ref, kn_ref, dk_ref, dkn_ref),
                  (v_ref, vh_ref, vn_ref, dv_ref, dvn_ref))
        for grp, (x_ref, h_ref, xn_ref, d_ref, dn_ref) in enumerate(groups):
            taps = _taps(w_ref, grp)
            xe = jnp.concatenate([jnp.where(first, 0.0, h_ref[...]), x_ref[...]], axis=0)
            sh = _shifts(xe, GTR)
            dc = _act_bwd(_conv(sh, taps), d_ref[...], grp)
            xe_n = jnp.concatenate([x_ref[GTR - 8:GTR, :], xn_ref[...]], axis=0)
            dcn = _act_bwd(_conv(_shifts(xe_n, 8), taps), dn_ref[...], grp)
            dce = jnp.concatenate([dc, jnp.where(last, 0.0, dcn)], axis=0)
            dx = taps[CONV_K - 1] * dc
            dw_ref[CONV_K - 1:CONV_K, grp * B_W:(grp + 1) * B_W] += _colsum(dc * sh[0])
            for s in range(1, CONV_K):
                dx = dx + taps[CONV_K - 1 - s] * pltpu.roll(dce, GTR + 8 - s, 0)[0:GTR]
                dw_ref[CONV_K - 1 - s:CONV_K - s, grp * B_W:(grp + 1) * B_W] += _colsum(dc * sh[s])
            dx_ref[:, grp * B_W:(grp + 1) * B_W] = dx.astype(BF16)
        ba = ba_ref[...]
        lane = lax.broadcasted_iota(jnp.int32, ba.shape, 1)
        bg, z = _gate_rows(ba, par_ref)
        daux_v = daux_ref[...]
        dg = _dot(_chunk_tri(GTR, upper=True), daux_v, HI)
        dgl = jnp.where((lane >= B_HEADS) & (lane < 2 * B_HEADS), dg, 0.0)
        da = dgl * (-jnp.exp(par_ref[0:1, :])) * _sigmoid(z)
        dbr = jnp.where(lane < B_HEADS, daux_v * bg * (1.0 - bg), 0.0)
        dba_ref[...] = (dbr + da).astype(BF16)
        dpar_ref[0:1, :] += _colsum(dgl * bg)
        dpar_ref[1:2, :] += _colsum(da)

    col = lambda off: _rb(GTR, B_W, off // B_W)
    row, rowl = _rb(GTR, B_W), _rb(GTR, LANE)
    return pl.pallas_call(
        body, grid=(nb,),
        in_specs=[col(OFF_QB), col(OFF_KB), col(OFF_VB),
                  _prev8(OFF_QB // B_W), _prev8(OFF_KB // B_W), _prev8(OFF_VB // B_W),
                  _next8(OFF_QB // B_W, nb), _next8(OFF_KB // B_W, nb), _next8(OFF_VB // B_W, nb),
                  _rb(GTR, LANE, OFF_BA // LANE), _whole((CONV_K, 3 * B_W)), _whole((8, LANE)),
                  row, row, row, _next8(0, nb), _next8(0, nb), _next8(0, nb), rowl],
        out_specs=(_rb(GTR, 3 * B_W), rowl, _whole((8, 3 * B_W)), _whole((8, LANE))),
        out_shape=(jax.ShapeDtypeStruct((tt, 3 * B_W), BF16), jax.ShapeDtypeStruct((tt, LANE), BF16),
                   jax.ShapeDtypeStruct((8, 3 * B_W), F32), jax.ShapeDtypeStruct((8, LANE), F32)),
        compiler_params=_cp(("arbitrary",)), name=name)(
            proj, proj, proj, proj, proj, proj, proj, proj, proj, proj, wconv, par, dq, dk, dv, dq, dk, dv, daux)


def _col(x, j):
    lane = lax.broadcasted_iota(jnp.int32, x.shape, 1)
    return jnp.sum(jnp.where(lane == j, x, 0.0), axis=-1, keepdims=True)


def _split(x):
    hi = x.astype(BF16)
    return hi, (x - hi.astype(F32)).astype(BF16)


def _dot3(a, b, tn=False):
    dot = _dot_tn if tn else _dot
    (ah, al), (bh, bl) = _split(a), _split(b)
    return dot(ah, bh) + (dot(ah, bl) + dot(al, bh))


def _chunk_masks():
    r = lax.broadcasted_iota(jnp.int32, (CH, CH), 0)
    c = lax.broadcasted_iota(jnp.int32, (CH, CH), 1)
    return r > c, r >= c


def _gc_rows(aux, nc):
    t = jnp.transpose(aux[:, B_HEADS:2 * B_HEADS].reshape(nc, CH, B_HEADS), (0, 2, 1))
    return jnp.concatenate([t, jnp.zeros_like(t)], axis=1).reshape(nc * 8, CH)


_CHUNK8 = lambda width, n=1: pl.BlockSpec((8 * n, width), lambda i: (i, 0))
_CHUNK4 = lambda a, b, n=1: pl.BlockSpec((B_HEADS * n, a, b), lambda i: (i, 0, 0))
NCH = 2


def _per_chunk(body, rows):
    def wrapped(*refs):
        for ci in range(NCH):
            body(*[r.at[pl.ds(ci * n, n)] for r, n in zip(refs, rows)])
    return wrapped


def _gdn_lower(k, aux, auxt, *, name):
    tt = k.shape[0]

    def body(k_ref, aux_ref, auxt_ref, l_ref):
        aux_v = aux_ref[...]
        strict, _ = _chunk_masks()
        for hd in range(B_HEADS):
            kh = k_ref[:, hd * B_DH:(hd + 1) * B_DH].astype(BF16)
            diff = _col(aux_v, B_HEADS + hd) - auxt_ref[hd:hd + 1, :]
            dec = jnp.exp(jnp.where(strict, diff, NEG))
            l_ref[hd] = _col(aux_v, hd) * _dot_nt(kh, kh) * dec

    return pl.pallas_call(
        _per_chunk(body, (CH, CH, 8, B_HEADS)), grid=(tt // CH // NCH,),
        in_specs=[_rb(NCH * CH, B_W), _rb(NCH * CH, LANE), _CHUNK8(CH, NCH)],
        out_specs=_CHUNK4(CH, CH, NCH),
        out_shape=jax.ShapeDtypeStruct((tt // CH * B_HEADS, CH, CH), F32),
        compiler_params=_cp(("parallel",)), name=name)(k, aux, auxt)


def _tri_inverse(lt, *, name):
    nb = lt.shape[2]

    def body(l_ref, t_ref):
        rowid = lax.broadcasted_iota(jnp.int32, (CH, nb), 0)

        def outer(i, carry):
            def inner(j, acc):
                return acc + l_ref[i, pl.ds(j, 1), :] * t_ref[j]

            acc = lax.fori_loop(0, i, inner, jnp.zeros((CH, nb), F32))
            t_ref[i] = jnp.where(rowid == i, 1.0, 0.0) - acc
            return carry

        lax.fori_loop(0, CH, outer, 0)

    return pl.pallas_call(body, out_shape=jax.ShapeDtypeStruct(lt.shape, F32),
                          in_specs=[pl.BlockSpec(memory_space=pltpu.VMEM)],
                          out_specs=pl.BlockSpec(memory_space=pltpu.VMEM),
                          compiler_params=_cp(), name=name)(lt)


def _gdn_gates(aux_v, aux_last, auxt_ref, hd):
    _, incl = _chunk_masks()
    beta = _col(aux_v, hd)
    gc = _col(aux_v, B_HEADS + hd)
    gl = _col(aux_last, B_HEADS + hd)
    dec = jnp.exp(jnp.where(incl, gc - auxt_ref[hd:hd + 1, :], NEG))
    return beta, gc, gl, jnp.exp(gc), dec


def _gdn_intra(q, k, v, aux, auxt, tinv, *, name):
    tt = q.shape[0]
    nc = tt // CH

    def body(q_ref, k_ref, v_ref, aux_ref, auxt_ref, t_ref, u0_ref, w_ref, qd_ref, kd_ref, qk_ref, gle_ref):
        aux_v = aux_ref[...]
        aux_last = aux_ref[CH - 1:CH, :]
        lane8 = lax.broadcasted_iota(jnp.int32, (8, LANE), 1)
        gle = jnp.zeros((8, LANE), F32)
        for hd in range(B_HEADS):
            sl = slice(hd * B_DH, (hd + 1) * B_DH)
            qh, kh, vh = q_ref[:, sl], k_ref[:, sl], v_ref[:, sl]
            beta, gc, gl, egc, dec = _gdn_gates(aux_v, aux_last, auxt_ref, hd)
            qk_ref[hd] = (_dot_nt(qh.astype(BF16), kh.astype(BF16)) * dec).astype(BF16)
            tinv = t_ref[hd]
            u0_ref[:, sl] = _dot3(tinv, vh * beta)
            w_ref[:, sl] = _dot3(tinv, kh * (beta * egc)).astype(BF16)
            qd_ref[:, sl] = (qh * egc).astype(BF16)
            kd_ref[:, sl] = (kh * jnp.exp(gl - gc)).astype(BF16)
            gle = gle + jnp.where(lane8 == hd, jnp.exp(gl), 0.0)
        gle_ref[...] = gle

    row = _rb(NCH * CH, B_W)
    half = jax.ShapeDtypeStruct((tt, B_W), BF16)
    return pl.pallas_call(
        _per_chunk(body, (CH, CH, CH, CH, 8, B_HEADS, CH, CH, CH, CH, B_HEADS, 8)), grid=(nc // NCH,),
        in_specs=[row, row, row, _rb(NCH * CH, LANE), _CHUNK8(CH, NCH), _CHUNK4(CH, CH, NCH)],
        out_specs=(row, row, row, row, _CHUNK4(CH, CH, NCH), _CHUNK8(LANE, NCH)),
        out_shape=(jax.ShapeDtypeStruct((tt, B_W), F32), half, half, half,
                   jax.ShapeDtypeStruct((nc * B_HEADS, CH, CH), BF16), jax.ShapeDtypeStruct((nc * 8, LANE), F32)),
        compiler_params=_cp(("parallel",)), name=name)(q, k, v, aux, auxt, tinv)


def _gdn_scan_fwd(u0, w, qd, kd, qk, gle, *, name):
    tt = u0.shape[0]
    nc = tt // CH

    def body(u0_ref, w_ref, qd_ref, kd_ref, qk_ref, gle_ref, o_ref, ss_ref, u_ref, s_ref):
        @pl.when(pl.program_id(0) == 0)
        def _():
            s_ref[...] = jnp.zeros_like(s_ref)

        gle = gle_ref[0:1, :]
        for hd in range(B_HEADS):
            sl = slice(hd * B_DH, (hd + 1) * B_DH)
            st = s_ref[hd]
            ss_ref[hd] = st
            sb = st.astype(BF16)
            ub = (u0_ref[:, sl] - _dot(w_ref[:, sl], sb)).astype(BF16)
            u_ref[:, sl] = ub
            o_ref[:, sl] = _dot(qd_ref[:, sl], sb) + _dot(qk_ref[hd], ub)
            s_ref[hd] = st * _col(gle, hd) + _dot_tn(kd_ref[:, sl], ub)

    row = _rb(CH, B_W)
    return pl.pallas_call(
        body, grid=(nc,), in_specs=[row, row, row, row, _CHUNK4(CH, CH), _CHUNK8(LANE)],
        out_specs=(row, _CHUNK4(B_DH, B_DH), row),
        out_shape=(jax.ShapeDtypeStruct((tt, B_W), F32), jax.ShapeDtypeStruct((nc * B_HEADS, B_DH, B_DH), F32),
                   jax.ShapeDtypeStruct((tt, B_W), BF16)),
        scratch_shapes=[pltpu.VMEM((B_HEADS, B_DH, B_DH), F32)],
        compiler_params=_cp(("arbitrary",)), name=name)(u0, w, qd, kd, qk, gle)


def _gdn_scan_bwd(w, qd, kd, qk, gle, do, *, name):
    tt = w.shape[0]
    nc = tt // CH

    def body(w_ref, qd_ref, kd_ref, qk_ref, gle_ref, do_ref, du_ref, dss_ref, ds_ref):
        @pl.when(pl.program_id(0) == 0)
        def _():
            ds_ref[...] = jnp.zeros_like(ds_ref)

        gle = gle_ref[0:1, :]
        for hd in range(B_HEADS):
            sl = slice(hd * B_DH, (hd + 1) * B_DH)
            dst = ds_ref[hd]
            dss_ref[hd] = dst
            dob = do_ref[:, sl].astype(BF16)
            du = _dot_tn(qk_ref[hd], dob) + _dot(kd_ref[:, sl], dst.astype(BF16))
            du_ref[:, sl] = du
            ds_ref[hd] = _dot_tn(qd_ref[:, sl], dob) + _col(gle, hd) * dst - _dot_tn(w_ref[:, sl], du.astype(BF16))

    rev = lambda width: pl.BlockSpec((CH, width), lambda i: (nc - 1 - i, 0))
    rev4 = lambda a, b: pl.BlockSpec((B_HEADS, a, b), lambda i: (nc - 1 - i, 0, 0))
    return pl.pallas_call(
        body, grid=(nc,),
        in_specs=[rev(B_W), rev(B_W), rev(B_W), rev4(CH, CH), pl.BlockSpec((8, LANE), lambda i: (nc - 1 - i, 0)), rev(B_W)],
        out_specs=(rev(B_W), rev4(B_DH, B_DH)),
        out_shape=(jax.ShapeDtypeStruct((tt, B_W), F32), jax.ShapeDtypeStruct((nc * B_HEADS, B_DH, B_DH), F32)),
        scratch_shapes=[pltpu.VMEM((B_HEADS, B_DH, B_DH), F32)],
        compiler_params=_cp(("arbitrary",)), name=name)(w, qd, kd, qk, gle, do)


def _gdn_bwd(q, k, v, aux, auxt, tinv, u0, w, u, ss, dss, du, do, *, name):
    tt = q.shape[0]
    nc = tt // CH

    def body(q_ref, k_ref, v_ref, aux_ref, auxt_ref, t_ref, u0_ref, w_ref, u_ref, ss_ref, dss_ref, du_ref, do_ref,
             dq_ref, dk_ref, dv_ref, daux_ref):
        aux_v = aux_ref[...]
        aux_last = aux_ref[CH - 1:CH, :]
        lane = lax.broadcasted_iota(jnp.int32, (CH, LANE), 1)
        rowi = lax.broadcasted_iota(jnp.int32, (CH, 1), 0)
        strict, incl = _chunk_masks()
        daux = jnp.zeros((CH, LANE), F32)
        for hd in range(B_HEADS):
            sl = slice(hd * B_DH, (hd + 1) * B_DH)
            qh, kh, vh = q_ref[:, sl], k_ref[:, sl], v_ref[:, sl]
            tinv_h = t_ref[hd]
            beta, gc, gl, egc, dec = _gdn_gates(aux_v, aux_last, auxt_ref, hd)
            ekd, eg_last = jnp.exp(gl - gc), jnp.exp(gl)
            kb, qb = kh.astype(BF16), qh.astype(BF16)
            kk, qk0 = _dot_nt(kb, kb), _dot_nt(qb, kb)
            st, dst = ss_ref[hd], dss_ref[hd]
            sb, dsb = st.astype(BF16), dst.astype(BF16)
            wb, ub, du = w_ref[:, sl], u_ref[:, sl], du_ref[:, sl]
            dob = do_ref[:, sl].astype(BF16)
            dq_dec = _dot_nt(dob, sb)
            dqk = jnp.where(incl, _dot_nt(dob, ub), 0.0)
            dk_dec = _dot_nt(ub, dsb)
            dgl = jnp.sum(jnp.sum(st * dst, axis=-1, keepdims=True), axis=0, keepdims=True) * eg_last
            dw = -_dot_nt(du.astype(BF16), sb)
            drv = _dot3(tinv_h, du, tn=True)
            drk = _dot3(tinv_h, dw, tn=True)
            dl = -(_dot_nt(drv.astype(BF16), u0_ref[:, sl].astype(BF16)) + _dot_nt(drk.astype(BF16), wb))
            dl = jnp.where(strict, dl, 0.0)
            dv_ref[:, sl] = drv * beta
            rk = jnp.sum(drk * kh, axis=-1, keepdims=True)
            dbeta = jnp.sum(drv * vh, axis=-1, keepdims=True) + rk * egc
            dgc = rk * beta * egc
            dk = drk * (beta * egc)
            ldec = dl * dec
            dbeta = dbeta + jnp.sum(ldec * kk, axis=-1, keepdims=True)
            dkk = (ldec * beta).astype(BF16)
            dqk0 = (dqk * dec).astype(BF16)
            ddec = ldec * beta * kk + dqk * (qk0 * dec)
            dq = _dot(dqk0, kb) + dq_dec * egc
            dk = dk + _dot_tn(dqk0, qb) + _dot(dkk, kb) + _dot_tn(dkk, kb) + dk_dec * ekd
            dgc = dgc + jnp.sum(ddec, axis=-1, keepdims=True) - _col_from_rowsum(ddec)
            dgc = dgc + jnp.sum(dq_dec * qh, axis=-1, keepdims=True) * egc
            kd = jnp.sum(dk_dec * kh, axis=-1, keepdims=True) * ekd
            dgc = dgc - kd
            dgc = dgc + jnp.where(rowi == CH - 1, jnp.sum(kd, axis=0, keepdims=True) + dgl, 0.0)
            dq_ref[:, sl] = dq
            dk_ref[:, sl] = dk
            daux = daux + jnp.where(lane == hd, dbeta, 0.0) + jnp.where(lane == B_HEADS + hd, dgc, 0.0)
        daux_ref[...] = daux

    row = _rb(NCH * CH, B_W)
    outs = jax.ShapeDtypeStruct((tt, B_W), F32)
    return pl.pallas_call(
        _per_chunk(body, (CH, CH, CH, CH, 8, B_HEADS, CH, CH, CH, B_HEADS, B_HEADS, CH, CH, CH, CH, CH, CH)),
        grid=(nc // NCH,),
        in_specs=[row, row, row, _rb(NCH * CH, LANE), _CHUNK8(CH, NCH), _CHUNK4(CH, CH, NCH), row, row, row,
                  _CHUNK4(B_DH, B_DH, NCH), _CHUNK4(B_DH, B_DH, NCH), row, row],
        out_specs=(row, row, row, _rb(NCH * CH, LANE)),
        out_shape=(outs, outs, outs, jax.ShapeDtypeStruct((tt, LANE), F32)),
        compiler_params=_cp(("parallel",)), name=name)(q, k, v, aux, auxt, tinv, u0, w, u, ss, dss, du, do)


def _col_from_rowsum(m):
    hi, lo = _split(m)
    ones = jnp.ones((CH, LANE), BF16)
    return (_dot_tn(hi, ones) + _dot_tn(lo, ones))[:, 0:1]


def _gdn_post_fwd(o, proj, gn, *, name, tr=256):
    tt = o.shape[0]

    def body(o_ref, z_ref, g_ref, y_ref):
        for hd in range(B_HEADS):
            sl = slice(hd * B_DH, (hd + 1) * B_DH)
            oh = o_ref[:, sl]
            r = lax.rsqrt(jnp.mean(oh * oh, axis=-1, keepdims=True) + EPS)
            y_ref[:, sl] = (oh * r * g_ref[...] * _silu(z_ref[:, sl])).astype(BF16)

    return pl.pallas_call(body, grid=(tt // tr,), in_specs=[_rb(tr, B_W), _rb(tr, B_W, OFF_ZB // B_W), _whole((1, B_DH))],
                          out_specs=_rb(tr, B_W), out_shape=jax.ShapeDtypeStruct((tt, B_W), BF16),
                          compiler_params=_cp(("parallel",)), name=name)(o, proj, gn)


def _gdn_post_bwd(o, proj, gn, dy, *, name, tr=256):
    tt = o.shape[0]

    def body(o_ref, z_ref, g_ref, dy_ref, do_ref, dz_ref, dg_ref):
        @pl.when(pl.program_id(0) == 0)
        def _():
            dg_ref[...] = jnp.zeros_like(dg_ref)

        g = g_ref[...]
        for hd in range(B_HEADS):
            sl = slice(hd * B_DH, (hd + 1) * B_DH)
            oh, zh, dyh = o_ref[:, sl], z_ref[:, sl], dy_ref[:, sl]
            r = lax.rsqrt(jnp.mean(oh * oh, axis=-1, keepdims=True) + EPS)
            a = oh * r
            s = _silu(zh)
            da = dyh * g * s
            dg_ref[0:1, :] += _colsum(dyh * a * s)
            dz_ref[:, sl] = (dyh * a * g * _dsilu(zh)).astype(BF16)
            do_ref[:, sl] = r * (da - a * jnp.mean(da * a, axis=-1, keepdims=True))

    return pl.pallas_call(
        body, grid=(tt // tr,), in_specs=[_rb(tr, B_W), _rb(tr, B_W, OFF_ZB // B_W), _whole((1, B_DH)), _rb(tr, B_W)],
        out_specs=(_rb(tr, B_W), _rb(tr, B_W), _whole((8, B_DH))),
        out_shape=(jax.ShapeDtypeStruct((tt, B_W), F32), jax.ShapeDtypeStruct((tt, B_W), BF16),
                   jax.ShapeDtypeStruct((8, B_DH), F32)),
        compiler_params=_cp(("arbitrary",)), name=name)(o, proj, gn, dy)


def _adamw(parts, w, m, v, own=None, sel=None, *, name, tr=256):
    npart, nl, r, c = parts.shape
    tr = max([t for t in range(8, min(r, tr) + 1, 8) if r % t == 0], default=r)
    c1, c2 = 1.0 - ADAM_B1 ** ADAM_STEP, 1.0 - ADAM_B2 ** ADAM_STEP

    def body(*refs):
        if own is None:
            p_ref, w_ref, m_ref, v_ref, g_ref, d_ref, mo_ref, vo_ref = refs
            part = lambda i: p_ref[i].astype(F32)
        else:
            p_ref, w_ref, m_ref, v_ref, own_ref, sel_ref, g_ref, d_ref, mo_ref, vo_ref = refs
            part = lambda i: jnp.where(sel_ref[i:i + 1, 0:1] > 0.5, own_ref[...].astype(F32), p_ref[i].astype(F32))
        g = part(0)
        for i in range(1, npart):
            g = g + part(i)
        mn = ADAM_B1 * m_ref[...] + (1.0 - ADAM_B1) * g
        vn = ADAM_B2 * v_ref[...] + (1.0 - ADAM_B2) * (g * g)
        g_ref[...] = g
        mo_ref[...] = mn
        vo_ref[...] = vn
        d_ref[...] = -ADAM_LR * ((mn / c1) / (jnp.sqrt(vn / c2) + ADAM_EPS) + ADAM_WD * w_ref[...])

    row = pl.BlockSpec((None, tr, c), lambda l, i: (l, i, 0))
    out = jax.ShapeDtypeStruct((nl, r, c), F32)
    ins, in_specs = [parts, w, m, v], [pl.BlockSpec((npart, None, tr, c), lambda l, i: (0, l, i, 0)), row, row, row]
    if own is not None:
        ins += [own, sel]
        in_specs += [row, pl.BlockSpec((N_DEV, LANE), lambda l, i: (0, 0))]
    return pl.pallas_call(body, grid=(nl, r // tr), in_specs=in_specs, out_specs=(row, row, row, row),
                          out_shape=(out, out, out, out), compiler_params=_cp(("parallel", "parallel")),
                          name=name)(*ins)


def _peer(k):
    x, y, c = lax.axis_index("x"), lax.axis_index("y"), lax.axis_index("c")
    return ((1 - x) if k & 4 else x, (1 - y) if k & 2 else y, (1 - c) if k & 1 else c)


def _my_index():
    return 4 * lax.axis_index("x") + 2 * lax.axis_index("y") + lax.axis_index("c")


def _index_of(p):
    return 4 * p[0] + 2 * p[1] + p[2]


def _all_gather(xs, *, name):
    n = len(xs)

    def body(*refs):
        x_refs, o_refs = refs[:n], refs[n:2 * n]
        send, recv, loc = refs[2 * n:]
        me = _my_index()
        copies = []
        for a in range(n):
            cp = pltpu.make_async_copy(x_refs[a], o_refs[a].at[me], loc.at[a])
            cp.start()
            copies.append(cp)
        rdmas = []
        for a in range(n):
            for k in range(1, N_DEV):
                r = pltpu.make_async_remote_copy(
                    src_ref=x_refs[a], dst_ref=o_refs[a].at[me], send_sem=send.at[a, k - 1], recv_sem=recv.at[a, k - 1],
                    device_id=_peer(k), device_id_type=pl.DeviceIdType.MESH)
                r.start()
                rdmas.append(r)
        for a in range(n):
            for k in range(1, N_DEV):
                pltpu.make_async_remote_copy(
                    src_ref=x_refs[a], dst_ref=o_refs[a].at[_index_of(_peer(k))], send_sem=send.at[a, k - 1],
                    recv_sem=recv.at[a, k - 1], device_id=_peer(k), device_id_type=pl.DeviceIdType.MESH).wait_recv()
        for r in rdmas:
            r.wait_send()
        for cp in copies:
            cp.wait()

    any_spec = pl.BlockSpec(memory_space=pl.ANY)
    return pl.pallas_call(
        body, in_specs=[any_spec] * n, out_specs=tuple([any_spec] * n),
        out_shape=tuple(jax.ShapeDtypeStruct((N_DEV,) + x.shape, x.dtype) for x in xs),
        scratch_shapes=[pltpu.SemaphoreType.DMA((n, N_DEV - 1)), pltpu.SemaphoreType.DMA((n, N_DEV - 1)),
                        pltpu.SemaphoreType.DMA((n,))],
        name=name)(*xs)


_HBM = pl.BlockSpec(memory_space=pltpu.HBM)
_SEM = pl.BlockSpec(memory_space=pltpu.SEMAPHORE)
_EFFECT = pltpu.SideEffectType.DATAFLOW_SIDE_EFFECTING


def _split_copy(src_ref, land_ref, send, recv, a, k, scatter, slot, sending):
    me, peer = _my_index(), _index_of(_peer(k))
    src = src_ref.at[peer if sending else me] if scatter else src_ref
    land = land_ref.at[me if sending else peer]
    if slot is not None:
        land = land.at[slot]
    sem = a * (N_DEV - 1) + k - 1
    return pltpu.make_async_remote_copy(src_ref=src, dst_ref=land, send_sem=send.at[sem], recv_sem=recv.at[sem],
                                        device_id=_peer(k), device_id_type=pl.DeviceIdType.MESH)


def _exchange_start(srcs, lands, after, *, scatter, slot=None, name):
    n = len(srcs)

    def body(*refs):
        src_refs, land_refs = refs[:n], refs[n:2 * n]
        send, recv, token = refs[2 * n + 1], refs[2 * n + 2], refs[-1]
        for a in range(n):
            for k in range(1, N_DEV):
                _split_copy(src_refs[a], land_refs[a], send, recv, a, k, scatter, slot, True).start()
        token[...] = jnp.zeros_like(token)

    hbm = lambda t: pltpu.HBM(t.shape, t.dtype)
    sems = pltpu.SemaphoreType.DMA((n * (N_DEV - 1),))
    out = pl.pallas_call(
        body, name=name,
        out_shape=(sems, sems, *[hbm(t) for t in srcs], *[hbm(t) for t in lands], jax.ShapeDtypeStruct((8, LANE), F32)),
        in_specs=[_HBM] * (2 * n) + [pl.BlockSpec(memory_space=pl.ANY)],
        out_specs=(_SEM, _SEM, *[_HBM] * (2 * n), pl.BlockSpec(memory_space=pltpu.VMEM)),
        input_output_aliases={i: 2 + i for i in range(2 * n)},
        compiler_params=pltpu.CompilerParams(has_side_effects=_EFFECT),
    )(*[pltpu.with_memory_space_constraint(t, pltpu.HBM) for t in (*srcs, *lands)], after)
    return out[0], out[1], out[2:2 + n], out[2 + n:2 + 2 * n], out[-1]


def _exchange_wait(send, recv, srcs, lands, after, *, scatter, slot=None, name):
    n = len(srcs)

    def body(*refs):
        src_refs, land_refs = refs[:n], refs[n:2 * n]
        send_ref, recv_ref = refs[2 * n], refs[2 * n + 1]
        for a in range(n):
            for k in range(1, N_DEV):
                _split_copy(src_refs[a], land_refs[a], send_ref, recv_ref, a, k, scatter, slot, True).wait_send()
                _split_copy(src_refs[a], land_refs[a], send_ref, recv_ref, a, k, scatter, slot, False).wait_recv()

    hbm = lambda t: pltpu.HBM(t.shape, t.dtype)
    out = pl.pallas_call(
        body, name=name, out_shape=(*[hbm(t) for t in srcs], *[hbm(t) for t in lands]),
        in_specs=[_HBM] * (2 * n) + [_SEM, _SEM, pl.BlockSpec(memory_space=pl.ANY)],
        out_specs=tuple([_HBM] * (2 * n)), input_output_aliases={i: i for i in range(2 * n)},
        compiler_params=pltpu.CompilerParams(has_side_effects=_EFFECT),
    )(*srcs, *lands, send, recv, after)
    return out[:n], out[n:]


def _win_to_mine(w):
    pad = jnp.zeros(w.shape[:-1] + (IN_PAD - IN_DIM,), w.dtype)
    return jnp.concatenate([w[..., 3592:5640], w[..., 0:3584], w[..., 3584:3592], pad], axis=-1)


def _win_from_mine(g):
    return jnp.concatenate([g[..., 2048:5632], g[..., 5632:5640], g[..., 0:2048]], axis=-1)


def _pad_rows(a, mult=8):
    r = (-a.shape[0]) % mult
    return a if r == 0 else jnp.concatenate([a, jnp.zeros((r,) + a.shape[1:], a.dtype)], axis=0)


def _lanes(vec, start):
    return jnp.zeros((1, LANE), F32).at[0, start:start + vec.shape[0]].set(vec)


def _small_spec(depth):
    return (("b_ada", (depth, 6 * D)), ("norm1_g", (depth, D)), ("norm2_g", (depth, D)),
            ("rel_table", (depth, A_HEADS, 2 * A_MAX_REL + 1)), ("a_log", (depth, B_HEADS)),
            ("dt_bias", (depth, B_HEADS)), ("gdn_norm_g", (depth, B_DH)), ("final_g", (D,)))


def _pack_small(d, extra, depth):
    spec = _small_spec(depth)
    rows = -(-(sum(math.prod(s) for _, s in spec) + 1) // (8 * LANE)) * 8
    flat = jnp.concatenate([d[n].reshape(-1).astype(F32) for n, _ in spec] + [extra.reshape(-1)])
    flat = jnp.concatenate([flat, jnp.zeros((rows * LANE - flat.shape[0],), F32)])
    return flat.reshape(rows, LANE)


def _unpack_small(p, depth):
    flat = p.reshape(-1)
    out, off = {}, 0
    for n, s in _small_spec(depth):
        sz = math.prod(s)
        out[n] = flat[off:off + sz].reshape(s)
        off += sz
    return out, flat[off]


def kernel(x, c, w_ada, b_ada, norm1_g, norm2_g, w_in, rel_table, w_conv, a_log, dt_bias, gdn_norm_g, w_branch_a, w_branch_b, w_out, w_ffn_in, w_ffn_out, final_g, loss_target, m_w_ada, m_b_ada, m_norm1_g, m_norm2_g, m_w_in, m_rel_table, m_w_conv, m_a_log, m_dt_bias, m_gdn_norm_g, m_w_branch_a, m_w_branch_b, m_w_out, m_w_ffn_in, m_w_ffn_out, m_final_g, v_w_ada, v_b_ada, v_norm1_g, v_norm2_g, v_w_in, v_rel_table, v_w_conv, v_a_log, v_dt_bias, v_gdn_norm_g, v_w_branch_a, v_w_branch_b, v_w_out, v_w_ffn_in, v_w_ffn_out, v_final_g):
    tt = x.shape[1]
    x0 = x[0]
    tgt = loss_target[0]
    me = _my_index()
    depth = w_in.shape[0]

    shards = [w_in.astype(BF16), w_branch_a.astype(BF16), w_branch_b.astype(BF16), w_out.astype(BF16),
              w_ffn_in.astype(BF16), w_ffn_out.astype(BF16), w_conv]
    names = ("win", "wa", "wb", "wout", "wfi", "wfo", "wconv")
    early, late, every = (0, 6), (1, 2, 3, 4, 5), tuple(range(7))
    first = _all_gather([shards[i][0] for i in early] + [_pad_rows(c)], name="gather_first")
    c_all = first[-1][:, 0, :]
    is_me = lax.broadcasted_iota(jnp.int32, (N_DEV, 1, 1), 0) == me

    def unpack(idx, g):
        cols = lambda t: jnp.transpose(t, (1, 0, 2)).reshape(t.shape[1], N_DEV * t.shape[2])
        rows = lambda t: t.reshape(N_DEV * t.shape[1], t.shape[2])
        how = (lambda t: _win_to_mine(cols(t)), cols, cols, rows, cols, rows, cols)
        return {names[i]: how[i](t) for i, t in zip(idx, g)}

    def gather_start(l, idx, after, tag=""):
        srcs = [shards[i][l] for i in idx]
        lands = [lax.empty((N_DEV,) + t.shape, t.dtype) for t in srcs]
        return _exchange_start(srcs, lands, after, scatter=False, name=f"gather_start_{l}{tag}")

    def gather_wait(l, idx, pending, after, tag=""):
        send, recv, srcs, lands, _ = pending
        srcs, lands = _exchange_wait(send, recv, srcs, lands, after, scatter=False, name=f"gather_wait_{l}{tag}")
        return unpack(idx, [jnp.where(is_me, t[None], g) for g, t in zip(lands, srcs)])

    weights = [unpack(early, first[:-1])] + [None] * (depth - 1)
    pending0 = gather_start(0, late, first[-1], "_rest")
    pending = gather_start(1, every, pending0[-1]) if depth > 1 else None
    cond = c_all * (1.0 / (1.0 + jnp.exp(-c_all)))
    cond = _pad_rows(cond, 16)

    mod_cols = jnp.stack([_mm(cond, w_ada[l], name="mod_mm")[:N_DEV] for l in range(depth)])
    (g_mod,) = _all_gather([mod_cols], name="gather_mod")
    mod_all = jnp.transpose(g_mod, (1, 2, 0, 3)).reshape(depth, N_DEV, 6 * D)
    mod = lax.dynamic_index_in_dim(mod_all, me, axis=1, keepdims=False) + b_ada
    mods = mod.reshape(depth, 6, 1, D)

    n1g, n2g = norm1_g.reshape(depth, 1, D), norm2_g.reshape(depth, 1, D)
    gng = gdn_norm_g.reshape(depth, 1, B_DH)
    fg = final_g.reshape(1, D)

    saved = []
    tok = (pending if pending is not None else pending0)[-1][0, 0]
    xin, h1 = _adaln_fwd(x0, n1g[0], mods[0, 1] + tok, mods[0, 0], name="adaln1_first")
    for l in range(depth):
        sh1, sc1, gt1, sh2, sc2, gt2 = (mods[l, i] for i in range(6))
        wl = weights[l]
        proj = _mm(h1, wl["win"], name="proj_mm", tn=1152)
        kpad = jnp.pad(proj[:, OFF_KA:OFF_KA + A_W].astype(BF16), ((A_PAST * CH, 0), (0, 0)))
        vpad = jnp.pad(proj[:, OFF_VA:OFF_VA + A_W].astype(BF16), ((A_PAST * CH, 0), (0, 0)))
        bias, bias_vjp = jax.vjp(_bias_from_table, rel_table[l])
        ya, lse = _attn_fwd(proj, kpad, vpad, bias, name="attn_fwd")
        par = jnp.concatenate([_lanes(a_log[l], B_HEADS), _lanes(dt_bias[l], B_HEADS), jnp.zeros((6, LANE), F32)], axis=0)
        qn, kn, vn, aux = _gdn_pre_fwd(proj, wl["wconv"], par, name="gdn_pre_fwd")
        auxt = _gc_rows(aux, tt // CH)
        lower = _gdn_lower(kn, aux, auxt, name="gdn_lower")
        tinv = jnp.transpose(_tri_inverse(jnp.transpose(lower, (1, 2, 0)), name="gdn_tri_inverse"), (2, 0, 1))
        u0, wg, qd, kd, qk, gle = _gdn_intra(qn, kn, vn, aux, auxt, tinv, name="gdn_intra")
        og, ss, ug = _gdn_scan_fwd(u0, wg, qd, kd, qk, gle, name="gdn_scan_fwd")
        yb = _gdn_post_fwd(og, proj, gng[l], name="gdn_post_fwd")
        if l == 0:
            wl.update(gather_wait(0, late, pending0, yb, "_rest"))
        pa = _mm(ya, wl["wa"], out_dtype=BF16, name="branch_a_mm")
        pb = _mm(yb, wl["wb"], out_dtype=BF16, name="branch_b_mm")
        merged = _merge_fwd(proj, pa, pb, name="merge_fwd")
        t1 = _mm(merged, wl["wout"], name="out_mm")
        x2, h2 = _adaln_fwd(xin, n2g[l], sc2, sh2, t1, gt1, name="adaln2_fwd")
        gu = _mm(h2, wl["wfi"], out_dtype=BF16, name="ffn_in_mm", tn=1408)
        act = _swiglu_fwd(gu, name="swiglu_fwd")
        t2 = _mm(act, wl["wfo"], name="ffn_out_mm", tk=1408)
        saved.append(dict(xin=xin, h1=h1, proj=proj, kpad=kpad, vpad=vpad, bias=bias, bias_vjp=bias_vjp, ya=ya, lse=lse,
                          par=par, qn=qn, kn=kn, vn=vn, aux=aux, auxt=auxt, tinv=tinv, ss=ss, og=og, yb=yb, pa=pa, pb=pb,
                          u0=u0, wg=wg, qd=qd, kd=kd, qk=qk, gle=gle, ug=ug,
                          merged=merged, t1=t1, x2=x2, h2=h2, gu=gu, act=act, t2=t2))
        if l + 1 < depth:
            weights[l + 1] = gather_wait(l + 1, every, pending, t2)
            pending = gather_start(l + 2, every, weights[l + 1]["wconv"]) if l + 2 < depth else None
            tok = pending[-1][0, 0] if pending is not None else 0.0
            xin, h1 = _adaln_fwd(x2, n1g[l + 1], mods[l + 1, 1] + tok, mods[l + 1, 0], t2, gt2, name="adaln1_fwd")

    s = saved[-1]
    dx, dt2, st = _loss_head(s["x2"], s["t2"], mods[depth - 1, 5], fg, tgt, name="loss_head")
    loss_part = st[4, 0]
    small_g = {"final_g": st[0]}
    dmod_rows = [None] * depth
    for n in ("norm1_g", "norm2_g", "rel_table", "a_log", "dt_bias", "gdn_norm_g"):
        small_g[n] = [None] * depth
    dgt2 = st[3]
    cols_slabs = lambda g: jnp.transpose(g.reshape(g.shape[0], N_DEV, g.shape[1] // N_DEV), (1, 0, 2))
    rows_slabs = lambda g: g.reshape(N_DEV, g.shape[0] // N_DEV, g.shape[1])
    mix, ffn = (0, 1, 2, 3, 6), (4, 5)
    lands = {kind: [lax.empty((N_DEV,) + shards[i].shape, shards[i].dtype) for i in idx]
             for kind, idx in (("mix", mix), ("ffn", ffn))}
    own = {kind: [None] * depth for kind in lands}
    pending_s = {kind: None for kind in lands}

    def scatter(kind, l, srcs, after):
        if pending_s[kind] is not None:
            done, lands[kind] = _exchange_wait(*pending_s[kind][:4], after, scatter=True, slot=l + 1,
                                               name=f"scatter_wait_{kind}_{l + 1}")
            own[kind][l + 1] = [lax.dynamic_index_in_dim(t, me, 0, keepdims=False) for t in done]
        pending_s[kind] = _exchange_start(srcs, lands[kind], after, scatter=True, slot=l, name=f"scatter_start_{kind}_{l}")
        return pending_s[kind][-1][0, 0]

    for l in reversed(range(depth)):
        s, wl = saved[l], weights[l]
        sh1, sc1, gt1, sh2, sc2, gt2 = (mods[l, i] for i in range(6))
        gw_fo = _mm(s["act"], dt2, ta=True, out_dtype=BF16, name="ffn_out_dw", tm=1408)
        dact = _mm(dt2, wl["wfo"], tb=True, out_dtype=BF16, name="ffn_out_dx", tn=1408)
        dgu = _swiglu_bwd(s["gu"], dact, name="swiglu_bwd")
        gw_fi = _mm(s["h2"], dgu, ta=True, out_dtype=BF16, name="ffn_in_dw", tn=1408)
        sc2 = sc2 + scatter("ffn", l, [cols_slabs(gw_fi), rows_slabs(gw_fo)], gw_fi)
        dh2 = _mm(dgu, wl["wfi"], tb=True, name="ffn_in_dx", tk=1408)
        dx, dt1, st2 = _adaln_bwd(s["x2"], n2g[l], sc2, sh2, dh2, dx, s["t1"], gt1, name="adaln2_bwd")
        gw_out = _mm(s["merged"], dt1, ta=True, out_dtype=BF16, name="out_dw")
        dmerged = _mm(dt1, wl["wout"], tb=True, name="out_dx")
        dgates, dpa, dpb = _merge_bwd(s["proj"], s["pa"], s["pb"], dmerged, name="merge_bwd")
        gw_a = _mm(s["ya"], dpa, ta=True, out_dtype=BF16, name="branch_a_dw")
        gw_b = _mm(s["yb"], dpb, ta=True, out_dtype=BF16, name="branch_b_dw")
        dya = _mm(dpa, wl["wa"], tb=True, name="branch_a_dx")
        dyb = _mm(dpb, wl["wb"], tb=True, name="branch_b_dx")
        dqa, dka, dva, dbias = _attn_bwd(s["proj"], s["kpad"], s["vpad"], s["bias"], s["ya"], s["lse"], dya,
                                             name="attn_bwd")
        small_g["rel_table"][l] = s["bias_vjp"](dbias)[0]
        dog, dz, dgn = _gdn_post_bwd(s["og"], s["proj"], gng[l], dyb, name="gdn_post_bwd")
        small_g["gdn_norm_g"][l] = dgn[0]
        dug, dss = _gdn_scan_bwd(s["wg"], s["qd"], s["kd"], s["qk"], s["gle"], dog, name="gdn_scan_bwd")
        dqn, dkn, dvn, daux = _gdn_bwd(s["qn"], s["kn"], s["vn"], s["aux"], s["auxt"], s["tinv"], s["u0"], s["wg"],
                                       s["ug"], s["ss"], dss, dug, dog, name="gdn_bwd")
        dqkv, dba, dwc, dpar = _gdn_pre_bwd(s["proj"], wl["wconv"], s["par"], dqn, dkn, dvn, daux, name="gdn_pre_bwd")
        small_g["a_log"][l] = dpar[0, B_HEADS:2 * B_HEADS]
        small_g["dt_bias"][l] = dpar[1, B_HEADS:2 * B_HEADS]
        dproj = jnp.concatenate([dgates, dqa, dka, dva, dqkv, dz, dba], axis=1)
        gw_in = _mm(s["h1"], dproj, ta=True, out_dtype=BF16, name="proj_dw", tn=1152)
        dh1 = _mm(dproj, wl["win"], tb=True, name="proj_dx", tk=1152)
        sc1 = sc1 + scatter("mix", l, [cols_slabs(_win_from_mine(gw_in)), cols_slabs(gw_a), cols_slabs(gw_b),
                                       rows_slabs(gw_out), cols_slabs(dwc[0:CONV_K])], gw_in)
        if l > 0:
            p = saved[l - 1]
            dx, dt2, st1 = _adaln_bwd(s["xin"], n1g[l], sc1, sh1, dh1, dx, p["t2"], mods[l - 1, 5], name="adaln1_bwd")
        else:
            dx, st1 = _adaln_bwd(s["xin"], n1g[l], sc1, sh1, dh1, dx, name="adaln1_bwd_first")
        small_g["norm1_g"][l], small_g["norm2_g"][l] = st1[0], st2[0]
        dmod_rows[l] = jnp.concatenate([st1[2], st1[1], st2[3], st2[2], st2[1], dgt2])
        if l > 0:
            dgt2 = st1[3]
    grad_x = dx[None]

    small_local = {n: (jnp.stack(vs) if isinstance(vs, list) else vs) for n, vs in small_g.items()}
    small_local["b_ada"] = jnp.stack(dmod_rows)
    (g_small,) = _all_gather([_pack_small(small_local, loss_part, depth)], name="gather_small")
    wsm = _pack_small(dict(b_ada=b_ada, norm1_g=norm1_g, norm2_g=norm2_g, rel_table=rel_table, a_log=a_log,
                           dt_bias=dt_bias, gdn_norm_g=gdn_norm_g, final_g=final_g), jnp.zeros((1,), F32), depth)
    msm = _pack_small(dict(b_ada=m_b_ada, norm1_g=m_norm1_g, norm2_g=m_norm2_g, rel_table=m_rel_table, a_log=m_a_log,
                           dt_bias=m_dt_bias, gdn_norm_g=m_gdn_norm_g, final_g=m_final_g), jnp.zeros((1,), F32), depth)
    vsm = _pack_small(dict(b_ada=v_b_ada, norm1_g=v_norm1_g, norm2_g=v_norm2_g, rel_table=v_rel_table, a_log=v_a_log,
                           dt_bias=v_dt_bias, gdn_norm_g=v_gdn_norm_g, final_g=v_final_g), jnp.ones((1,), F32), depth)
    sm = [_unpack_small(t, depth) for t in _adamw(g_small[:, None], wsm[None], msm[None], vsm[None], name="adamw_small")]
    loss = sm[0][1]

    dmod_all = g_small.reshape(N_DEV, -1)[:, :depth * 6 * D].reshape(N_DEV, depth, 6 * D)
    dmod_mine = lax.dynamic_slice_in_dim(dmod_all, me * (6 * D // N_DEV), 6 * D // N_DEV, axis=2)
    g_ada = jnp.stack([_mm(cond, _pad_rows(dmod_mine[:, l], 16), ta=True, name="ada_dw") for l in range(depth)])

    got, mine = {}, {}
    for kind, idx in (("ffn", ffn), ("mix", mix)):
        done, lands[kind] = _exchange_wait(*pending_s[kind][:4], g_ada, scatter=True, slot=0, name=f"scatter_wait_{kind}_0")
        own[kind][0] = [lax.dynamic_index_in_dim(t, me, 0, keepdims=False) for t in done]
        for a, i in enumerate(idx):
            got[i] = lands[kind][a]
            mine[i] = jnp.stack([own[kind][l][a] for l in range(depth)])
    sel = jnp.broadcast_to(jnp.where(is_me[:, :, 0], 1.0, 0.0), (N_DEV, LANE)).astype(F32)

    def upd(i, w, m, v, name):
        return _adamw(got[i], w, m, v, mine[i], sel, name=name)

    res = {
        "w_ada": _adamw(g_ada[None], w_ada, m_w_ada, v_w_ada, name="adamw_w_ada"),
        "w_in": upd(0, w_in, m_w_in, v_w_in, "adamw_w_in"),
        "w_conv": upd(6, w_conv, m_w_conv, v_w_conv, "adamw_w_conv"),
        "w_branch_a": upd(1, w_branch_a, m_w_branch_a, v_w_branch_a, "adamw_w_branch_a"),
        "w_branch_b": upd(2, w_branch_b, m_w_branch_b, v_w_branch_b, "adamw_w_branch_b"),
        "w_out": upd(3, w_out, m_w_out, v_w_out, "adamw_w_out"),
        "w_ffn_in": upd(4, w_ffn_in, m_w_ffn_in, v_w_ffn_in, "adamw_w_ffn_in"),
        "w_ffn_out": upd(5, w_ffn_out, m_w_ffn_out, v_w_ffn_out, "adamw_w_ffn_out"),
    }
    for n, _ in _small_spec(depth):
        res[n] = [sm[i][0][n] for i in range(4)]
    order = ("w_ada", "b_ada", "norm1_g", "norm2_g", "w_in", "rel_table", "w_conv", "a_log", "dt_bias", "gdn_norm_g",
             "w_branch_a", "w_branch_b", "w_out", "w_ffn_in", "w_ffn_out", "final_g")
    return (loss, grad_x, *[res[n][0] for n in order], *[res[n][1] for n in order],
            *[res[n][2] for n in order], *[res[n][3] for n in order])
```

```python
import functools
import math

import jax
import jax.numpy as jnp
from jax import lax
from jax.experimental import pallas as pl
from jax.experimental.pallas import tpu as pltpu

F32 = jnp.float32
BF16 = jnp.bfloat16
HI = lax.Precision.HIGHEST

N_DEV = 8
D = 1024
DEPTH = 4
CH = 64
EPS = 1e-6
A_HEADS, A_DH = 8, 64
A_W = A_HEADS * A_DH
A_PAST = 8
A_MAX_REL = 128
QB = 256
KB = QB + A_PAST * CH
B_HEADS, B_DH = 4, 128
B_W = B_HEADS * B_DH
CONV_K = 4
FF = 2816
IN_DIM = 5640
IN_PAD = 5760
LANE = 128
NEG = -1e30
VMEM_LIMIT = 48 * 1024 * 1024

ADAM_LR, ADAM_B1, ADAM_B2, ADAM_EPS, ADAM_WD, ADAM_STEP = 0.001, 0.9, 0.999, 1e-08, 0.01, 10

OFF_GA, OFF_GB, OFF_QA, OFF_KA, OFF_VA, OFF_QB, OFF_KB, OFF_VB, OFF_ZB, OFF_BA = (
    0, 1024, 2048, 2560, 3072, 3584, 4096, 4608, 5120, 5632)


def _cp(sem=None):
    return pltpu.CompilerParams(dimension_semantics=sem, vmem_limit_bytes=VMEM_LIMIT)


def _tile(n, pref):
    if n <= pref:
        return n
    best = None
    for t in range(LANE, pref + 1, LANE):
        if n % t == 0:
            best = t
    assert best is not None, (n, pref)
    return best


def _sigmoid(x):
    return 1.0 / (1.0 + jnp.exp(-x))


def _silu(x):
    return x * _sigmoid(x)


def _dsilu(x):
    s = _sigmoid(x)
    return s * (1.0 + x * (1.0 - s))


def _dot(a, b, prec=None):
    return jnp.dot(a, b, preferred_element_type=F32, precision=prec)


def _dot_nt(a, b, prec=None):
    return lax.dot_general(a, b, (((1,), (1,)), ((), ())), preferred_element_type=F32, precision=prec)


def _dot_tn(a, b, prec=None):
    return lax.dot_general(a, b, (((0,), (0,)), ((), ())), preferred_element_type=F32, precision=prec)


def _mm(a, b, *, ta=False, tb=False, out_dtype=F32, name, tm=1024, tn=1024, tk=1024):
    m, k = (a.shape[1], a.shape[0]) if ta else a.shape
    n = b.shape[0] if tb else b.shape[1]
    assert k == (b.shape[1] if tb else b.shape[0]), (a.shape, b.shape, ta, tb)
    tm, tn, tk = _tile(m, tm), _tile(n, tn), _tile(k, tk)
    nk = k // tk
    dn = (((0 if ta else 1,), (1 if tb else 0,)), ((), ()))

    def body(a_ref, b_ref, o_ref, *acc):
        part = lax.dot_general(a_ref[...].astype(BF16), b_ref[...].astype(BF16), dn, preferred_element_type=F32)
        if nk == 1:
            o_ref[...] = part.astype(out_dtype)
            return
        acc_ref, kk = acc[0], pl.program_id(2)

        @pl.when(kk == 0)
        def _():
            acc_ref[...] = part

        @pl.when(kk > 0)
        def _():
            acc_ref[...] += part

        @pl.when(kk == nk - 1)
        def _():
            o_ref[...] = acc_ref[...].astype(out_dtype)

    a_spec = pl.BlockSpec((tk, tm), lambda i, j, q: (q, i)) if ta else pl.BlockSpec((tm, tk), lambda i, j, q: (i, q))
    b_spec = pl.BlockSpec((tn, tk), lambda i, j, q: (j, q)) if tb else pl.BlockSpec((tk, tn), lambda i, j, q: (q, j))
    return pl.pallas_call(
        body, grid=(m // tm, n // tn, nk), in_specs=[a_spec, b_spec],
        out_specs=pl.BlockSpec((tm, tn), lambda i, j, q: (i, j)),
        out_shape=jax.ShapeDtypeStruct((m, n), out_dtype),
        scratch_shapes=[pltpu.VMEM((tm, tn), F32)] if nk > 1 else [],
        compiler_params=_cp(("parallel", "parallel", "arbitrary")), name=name)(a, b)


def _rb(tr, width, cb=0):
    return pl.BlockSpec((tr, width), lambda i: (i, cb))


def _whole(shape):
    nd = len(shape)
    return pl.BlockSpec(shape, lambda i: (0,) * nd)


def _colsum(v):
    return jnp.sum(v, axis=0, keepdims=True)


def _adaln_fwd(x, g, sc, sh, t=None, gt=None, *, name, tr=256):
    tt = x.shape[0]
    res = t is not None

    def body(*refs):
        if res:
            x_ref, t_ref, gt_ref, g_ref, sc_ref, sh_ref, xo_ref, h_ref = refs
            xv = x_ref[...] + gt_ref[...] * t_ref[...]
            xo_ref[...] = xv
        else:
            x_ref, g_ref, sc_ref, sh_ref, h_ref = refs
            xv = x_ref[...]
        r = lax.rsqrt(jnp.mean(xv * xv, axis=-1, keepdims=True) + EPS)
        h_ref[...] = ((xv * r * g_ref[...]) * (1.0 + sc_ref[...]) + sh_ref[...]).astype(BF16)

    row, vec = _rb(tr, D), _whole((1, D))
    if res:
        ins, in_specs = (x, t, gt, g, sc, sh), [row, row, vec, vec, vec, vec]
        out_shape = (jax.ShapeDtypeStruct((tt, D), F32), jax.ShapeDtypeStruct((tt, D), BF16))
        out_specs = (row, row)
    else:
        ins, in_specs = (x, g, sc, sh), [row, vec, vec, vec]
        out_shape, out_specs = jax.ShapeDtypeStruct((tt, D), BF16), row
    out = pl.pallas_call(body, grid=(tt // tr,), in_specs=in_specs, out_specs=out_specs, out_shape=out_shape,
                         compiler_params=_cp(("parallel",)), name=name)(*ins)
    return out if res else (x, out)


def _adaln_bwd(x, g, sc, sh, dh, dx_in, t=None, gt=None, *, name, tr=256):
    tt = x.shape[0]
    res = t is not None

    def body(*refs):
        if res:
            x_ref, g_ref, sc_ref, sh_ref, dh_ref, dxi_ref, t_ref, gt_ref, dx_ref, dt_ref, st_ref = refs
        else:
            x_ref, g_ref, sc_ref, sh_ref, dh_ref, dxi_ref, dx_ref, st_ref = refs

        @pl.when(pl.program_id(0) == 0)
        def _():
            st_ref[...] = jnp.zeros_like(st_ref)

        xv, dh = x_ref[...], dh_ref[...]
        r = lax.rsqrt(jnp.mean(xv * xv, axis=-1, keepdims=True) + EPS)
        nrm = xv * r
        y = nrm * g_ref[...]
        dy = dh * (1.0 + sc_ref[...])
        dn = dy * g_ref[...]
        dx = dxi_ref[...] + r * (dn - nrm * jnp.mean(dn * nrm, axis=-1, keepdims=True))
        dx_ref[...] = dx
        st_ref[0:1, :] += _colsum(dy * nrm)
        st_ref[1:2, :] += _colsum(dh * y)
        st_ref[2:3, :] += _colsum(dh)
        if res:
            dt_ref[...] = (gt_ref[...] * dx).astype(BF16)
            st_ref[3:4, :] += _colsum(dx * t_ref[...])

    row, vec, st = _rb(tr, D), _whole((1, D)), _whole((8, D))
    ins, in_specs = [x, g, sc, sh, dh, dx_in], [row, vec, vec, vec, row, row]
    out_shape, out_specs = [jax.ShapeDtypeStruct((tt, D), F32)], [row]
    if res:
        ins += [t, gt]
        in_specs += [row, vec]
        out_shape.append(jax.ShapeDtypeStruct((tt, D), BF16))
        out_specs.append(row)
    out_shape.append(jax.ShapeDtypeStruct((8, D), F32))
    out_specs.append(st)
    return pl.pallas_call(body, grid=(tt // tr,), in_specs=in_specs, out_specs=tuple(out_specs),
                          out_shape=tuple(out_shape), compiler_params=_cp(("arbitrary",)), name=name)(*ins)


def _loss_head(x, t, gt, fg, tgt, *, name, tr=256):
    tt = x.shape[0]

    def body(x_ref, t_ref, gt_ref, fg_ref, tgt_ref, dx_ref, dt_ref, st_ref):
        @pl.when(pl.program_id(0) == 0)
        def _():
            st_ref[...] = jnp.zeros_like(st_ref)

        tv = t_ref[...]
        xv = x_ref[...] + gt_ref[...] * tv
        r = lax.rsqrt(jnp.mean(xv * xv, axis=-1, keepdims=True) + EPS)
        nrm = xv * r
        err = nrm * fg_ref[...] - tgt_ref[...]
        st_ref[4:5, :] += 0.5 * jnp.sum(jnp.mean(err * err, axis=-1, keepdims=True), axis=0, keepdims=True)
        dy = err * (1.0 / D)
        dn = dy * fg_ref[...]
        dx = r * (dn - nrm * jnp.mean(dn * nrm, axis=-1, keepdims=True))
        dx_ref[...] = dx
        dt_ref[...] = (gt_ref[...] * dx).astype(BF16)
        st_ref[0:1, :] += _colsum(dy * nrm)
        st_ref[3:4, :] += _colsum(dx * tv)

    row, vec = _rb(tr, D), _whole((1, D))
    return pl.pallas_call(
        body, grid=(tt // tr,), in_specs=[row, row, vec, vec, row], out_specs=(row, row, _whole((8, D))),
        out_shape=(jax.ShapeDtypeStruct((tt, D), F32), jax.ShapeDtypeStruct((tt, D), BF16),
                   jax.ShapeDtypeStruct((8, D), F32)),
        compiler_params=_cp(("arbitrary",)), name=name)(x, t, gt, fg, tgt)


def _merge_fwd(proj, pa, pb, *, name, tr=256):
    tt = pa.shape[0]

    def body(ga_ref, gb_ref, pa_ref, pb_ref, o_ref):
        o_ref[...] = (_sigmoid(ga_ref[...]) * pa_ref[...].astype(F32)
                      + _sigmoid(gb_ref[...]) * pb_ref[...].astype(F32)).astype(BF16)

    row = _rb(tr, D)
    return pl.pallas_call(body, grid=(tt // tr,), in_specs=[_rb(tr, D, 0), _rb(tr, D, 1), row, row], out_specs=row,
                          out_shape=jax.ShapeDtypeStruct((tt, D), BF16), compiler_params=_cp(("parallel",)),
                          name=name)(proj, proj, pa, pb)


def _merge_bwd(proj, pa, pb, dm, *, name, tr=256):
    tt = pa.shape[0]

    def body(ga_ref, gb_ref, pa_ref, pb_ref, dm_ref, dg_ref, dpa_ref, dpb_ref):
        dm_v = dm_ref[...]
        sa, sb = _sigmoid(ga_ref[...]), _sigmoid(gb_ref[...])
        dpa_ref[...] = (dm_v * sa).astype(BF16)
        dpb_ref[...] = (dm_v * sb).astype(BF16)
        dg_ref[:, 0:D] = (dm_v * pa_ref[...].astype(F32) * sa * (1.0 - sa)).astype(BF16)
        dg_ref[:, D:2 * D] = (dm_v * pb_ref[...].astype(F32) * sb * (1.0 - sb)).astype(BF16)

    row = _rb(tr, D)
    return pl.pallas_call(
        body, grid=(tt // tr,), in_specs=[_rb(tr, D, 0), _rb(tr, D, 1), row, row, row],
        out_specs=(_rb(tr, 2 * D), row, row),
        out_shape=(jax.ShapeDtypeStruct((tt, 2 * D), BF16), jax.ShapeDtypeStruct((tt, D), BF16),
                   jax.ShapeDtypeStruct((tt, D), BF16)),
        compiler_params=_cp(("parallel",)), name=name)(proj, proj, pa, pb, dm)


def _swiglu_fwd(gu, *, name, tr=256):
    tt = gu.shape[0]

    def body(g_ref, u_ref, o_ref):
        o_ref[...] = (_silu(g_ref[...].astype(F32)) * u_ref[...].astype(F32)).astype(BF16)

    return pl.pallas_call(body, grid=(tt // tr,), in_specs=[_rb(tr, FF, 0), _rb(tr, FF, 1)], out_specs=_rb(tr, FF),
                          out_shape=jax.ShapeDtypeStruct((tt, FF), BF16), compiler_params=_cp(("parallel",)),
                          name=name)(gu, gu)


def _swiglu_bwd(gu, dact, *, name, tr=256):
    tt = gu.shape[0]

    def body(g_ref, u_ref, da_ref, o_ref):
        gv, da = g_ref[...].astype(F32), da_ref[...].astype(F32)
        o_ref[:, 0:FF] = (da * u_ref[...].astype(F32) * _dsilu(gv)).astype(BF16)
        o_ref[:, FF:2 * FF] = (da * _silu(gv)).astype(BF16)

    return pl.pallas_call(body, grid=(tt // tr,), in_specs=[_rb(tr, FF, 0), _rb(tr, FF, 1), _rb(tr, FF)],
                          out_specs=_rb(tr, 2 * FF), out_shape=jax.ShapeDtypeStruct((tt, 2 * FF), BF16),
                          compiler_params=_cp(("parallel",)), name=name)(gu, gu, dact)


BIAS_LW = 1152


def _bias_diagonals(table):
    n_far = A_PAST * CH - A_MAX_REL + 1
    far = jnp.broadcast_to(table[:, 2 * A_MAX_REL:], (A_HEADS, n_far))
    mid = jnp.flip(table[:, 1:2 * A_MAX_REL], axis=1)
    near = jnp.broadcast_to(table[:, 0:1], (A_HEADS, KB - n_far - (2 * A_MAX_REL - 1)))
    pos = jnp.concatenate([far, mid, near], axis=1)
    neg = jnp.broadcast_to(table[:, 2 * A_MAX_REL:], (A_HEADS, QB - 1))
    gap = jnp.zeros((A_HEADS, BIAS_LW - KB - (QB - 1)), F32)
    return jnp.concatenate([pos, gap, neg], axis=1)


def _bias_fwd(diag, *, name):
    def body(w_ref, o_ref):
        qc = lax.broadcasted_iota(jnp.int32, (QB, KB), 0) // CH + A_PAST
        kc = lax.broadcasted_iota(jnp.int32, (QB, KB), 1) // CH
        inband = (kc <= qc) & (kc >= qc - A_PAST)
        for h in range(A_HEADS):
            rows = pltpu.roll(jnp.broadcast_to(w_ref[h:h + 1, :], (QB, BIAS_LW)), 0, 1, stride=1, stride_axis=0)
            o_ref[h] = jnp.where(inband, rows[:, :KB], NEG)

    return pl.pallas_call(body, out_shape=jax.ShapeDtypeStruct((A_HEADS, QB, KB), F32), compiler_params=_cp(),
                          name=name)(diag)


def _bias_bwd(dbias_rev, *, name):
    def body(d_ref, o_ref):
        for h in range(A_HEADS):
            x = jnp.concatenate([d_ref[h], jnp.zeros((QB, BIAS_LW - KB), F32)], axis=1)
            o_ref[h:h + 1, :] = jnp.sum(pltpu.roll(x, 0, 1, stride=1, stride_axis=0), axis=0, keepdims=True)

    return pl.pallas_call(body, out_shape=jax.ShapeDtypeStruct((A_HEADS, BIAS_LW), F32), compiler_params=_cp(),
                          name=name)(dbias_rev)


def _kv_pad(proj, *, name, tr=256):
    tt = proj.shape[0]
    npad = A_PAST * CH // tr

    def body(k_ref, v_ref, ko_ref, vo_ref):
        i = pl.program_id(0)

        @pl.when(i < npad)
        def _():
            ko_ref[...] = jnp.zeros_like(ko_ref)
            vo_ref[...] = jnp.zeros_like(vo_ref)

        @pl.when(i >= npad)
        def _():
            ko_ref[...] = k_ref[...].astype(BF16)
            vo_ref[...] = v_ref[...].astype(BF16)

    src = lambda off: pl.BlockSpec((tr, A_W), lambda i: (jnp.maximum(i - npad, 0), off // A_W))
    out = jax.ShapeDtypeStruct((tt + A_PAST * CH, A_W), BF16)
    return pl.pallas_call(body, grid=(tt // tr + npad,), in_specs=[src(OFF_KA), src(OFF_VA)],
                          out_specs=(_rb(tr, A_W), _rb(tr, A_W)), out_shape=(out, out),
                          compiler_params=_cp(("parallel",)), name=name)(proj, proj)


def _attn_fwd(proj, kpad, vpad, bias, *, name):
    tt = proj.shape[0]

    def body(q_ref, k_ref, v_ref, b_ref, o_ref, l_ref):
        q0 = pl.multiple_of(pl.program_id(1) * QB, QB)
        q = q_ref[...]
        k = k_ref[pl.ds(q0, KB), :]
        v = v_ref[pl.ds(q0, KB), :]
        lane = lax.broadcasted_iota(jnp.int32, (QB, LANE), 1)
        valid = (lax.broadcasted_iota(jnp.int32, (QB, KB), 1) + q0) >= A_PAST * CH
        o = jnp.zeros((QB, LANE), F32)
        lse = jnp.zeros((QB, LANE), F32)
        for a in range(2):
            hm = (lane >= A_DH * a) & (lane < A_DH * (a + 1))
            s = _dot_nt(jnp.where(hm, q, 0.0).astype(BF16), k) * (A_DH ** -0.5) + b_ref[a]
            s = jnp.where(valid, s, NEG)
            m = jnp.max(s, axis=-1, keepdims=True)
            p = jnp.exp(s - m)
            l = jnp.sum(p, axis=-1, keepdims=True)
            o = jnp.where(hm, _dot((p / l).astype(BF16), v), o)
            lse = jnp.where(hm, m + jnp.log(l), lse)
        o_ref[...] = o.astype(BF16)
        l_ref[...] = lse

    kv = pl.BlockSpec((tt + A_PAST * CH, LANE), lambda h, i: (0, h))
    blk = pl.BlockSpec((QB, LANE), lambda h, i: (i, h))
    return pl.pallas_call(
        body, grid=(A_W // LANE, tt // QB),
        in_specs=[pl.BlockSpec((QB, LANE), lambda h, i: (i, OFF_QA // LANE + h)), kv, kv,
                  pl.BlockSpec((2, QB, KB), lambda h, i: (h, 0, 0))],
        out_specs=(blk, blk),
        out_shape=(jax.ShapeDtypeStruct((tt, A_W), BF16), jax.ShapeDtypeStruct((tt, A_W), F32)),
        compiler_params=_cp(("parallel", "parallel")), name=name)(proj, kpad, vpad, bias)


def _attn_bwd(proj, kpad, vpad, bias, o, lse, do, *, name):
    tt = proj.shape[0]
    nq = tt // QB

    def body(q_ref, k_ref, v_ref, b_ref, o_ref, l_ref, do_ref, dq_ref, dko_ref, dvo_ref, db_ref, dk_ref, dv_ref):
        @pl.when(pl.program_id(1) == 0)
        def _():
            dk_ref[...] = jnp.zeros_like(dk_ref)
            dv_ref[...] = jnp.zeros_like(dv_ref)
            db_ref[...] = jnp.zeros_like(db_ref)

        q0 = pl.multiple_of(pl.program_id(1) * QB, QB)
        q, do_v, lse = q_ref[...], do_ref[...], l_ref[...]
        k = k_ref[pl.ds(q0, KB), :]
        v = v_ref[pl.ds(q0, KB), :]
        dsum = do_v * o_ref[...].astype(F32)
        lane = lax.broadcasted_iota(jnp.int32, (QB, LANE), 1)
        valid = (lax.broadcasted_iota(jnp.int32, (QB, KB), 1) + q0) >= A_PAST * CH
        dq = jnp.zeros((QB, LANE), F32)
        dk = jnp.zeros((KB, LANE), F32)
        dv = jnp.zeros((KB, LANE), F32)
        for a in range(2):
            hm = (lane >= A_DH * a) & (lane < A_DH * (a + 1))
            qa = jnp.where(hm, q, 0.0).astype(BF16)
            doa = jnp.where(hm, do_v, 0.0).astype(BF16)
            s = _dot_nt(qa, k) * (A_DH ** -0.5) + b_ref[a]
            s = jnp.where(valid, s, NEG)
            lse_a = jnp.max(jnp.where(hm, lse, NEG), axis=-1, keepdims=True)
            p = jnp.exp(s - lse_a)
            dp = _dot_nt(doa, v)
            dsum_a = jnp.sum(jnp.where(hm, dsum, 0.0), axis=-1, keepdims=True)
            ds = p * (dp - dsum_a)
            db_ref[a] += ds
            dsb = (ds * (A_DH ** -0.5)).astype(BF16)
            dq = jnp.where(hm, _dot(dsb, k), dq)
            dk += _dot_tn(dsb, qa)
            dv += _dot_tn(p.astype(BF16), doa)
        dq_ref[...] = dq.astype(BF16)
        dk_ref[pl.ds(q0, KB), :] += dk
        dv_ref[pl.ds(q0, KB), :] += dv

        @pl.when(pl.program_id(1) == nq - 1)
        def _():
            dko_ref[...] = dk_ref[A_PAST * CH:, :].astype(BF16)
            dvo_ref[...] = dv_ref[A_PAST * CH:, :].astype(BF16)

    kv = pl.BlockSpec((tt + A_PAST * CH, LANE), lambda h, i: (0, h))
    blk = pl.BlockSpec((QB, LANE), lambda h, i: (i, h))
    col = pl.BlockSpec((tt, LANE), lambda h, i: (0, h))
    bsp = pl.BlockSpec((2, QB, KB), lambda h, i: (h, 0, 0))
    out = jax.ShapeDtypeStruct((tt, A_W), BF16)
    return pl.pallas_call(
        body, grid=(A_W // LANE, nq),
        in_specs=[pl.BlockSpec((QB, LANE), lambda h, i: (i, OFF_QA // LANE + h)), kv, kv, bsp, blk, blk, blk],
        out_specs=(blk, col, col, bsp),
        out_shape=(out, out, out, jax.ShapeDtypeStruct((A_HEADS, QB, KB), F32)),
        scratch_shapes=[pltpu.VMEM((tt + A_PAST * CH, LANE), F32), pltpu.VMEM((tt + A_PAST * CH, LANE), F32)],
        compiler_params=_cp(("parallel", "arbitrary")), name=name)(proj, kpad, vpad, bias, o, lse, do)


GTR = 256


def _taps(w_ref, grp):
    return [w_ref[j:j + 1, grp * B_W:(grp + 1) * B_W] for j in range(CONV_K)]


def _shifts(xe, rows):
    return [xe[8:8 + rows]] + [pltpu.roll(xe, s, 0)[8:8 + rows] for s in range(1, CONV_K)]


def _conv(shifts, taps):
    acc = taps[CONV_K - 1] * shifts[0]
    for s in range(1, CONV_K):
        acc = acc + taps[CONV_K - 1 - s] * shifts[s]
    return acc


def _qk_scale(grp):
    return B_DH ** -0.5 if grp == 0 else 1.0


def _act_fwd(c, grp):
    y = _silu(c)
    if grp == 2:
        return y
    parts = []
    for hd in range(B_HEADS):
        yh = y[:, hd * B_DH:(hd + 1) * B_DH]
        parts.append(yh * (lax.rsqrt(jnp.sum(yh * yh, axis=-1, keepdims=True) + EPS) * _qk_scale(grp)))
    return jnp.concatenate(parts, axis=1)


def _act_bwd(c, dy, grp):
    if grp == 2:
        return dy * _dsilu(c)
    y = _silu(c)
    parts = []
    for hd in range(B_HEADS):
        yh = y[:, hd * B_DH:(hd + 1) * B_DH]
        r = lax.rsqrt(jnp.sum(yh * yh, axis=-1, keepdims=True) + EPS)
        dyh = dy[:, hd * B_DH:(hd + 1) * B_DH] * _qk_scale(grp)
        parts.append(r * dyh - yh * (r * r * r) * jnp.sum(dyh * yh, axis=-1, keepdims=True))
    return jnp.concatenate(parts, axis=1) * _dsilu(c)


def _chunk_tri(n, upper=False):
    r = lax.broadcasted_iota(jnp.int32, (n, n), 0)
    c = lax.broadcasted_iota(jnp.int32, (n, n), 1)
    same = (r // CH) == (c // CH)
    return jnp.where(same & ((r <= c) if upper else (r >= c)), 1.0, 0.0).astype(F32)


def _gate_rows(ba, par_ref):
    lane = lax.broadcasted_iota(jnp.int32, ba.shape, 1)
    z = ba + par_ref[1:2, :]
    sp = jnp.maximum(z, 0.0) + jnp.log(1.0 + jnp.exp(-jnp.abs(z)))
    g = -jnp.exp(par_ref[0:1, :]) * sp
    return jnp.where(lane < B_HEADS, _sigmoid(ba), jnp.where(lane < 2 * B_HEADS, g, 0.0)), z


def _prev8(cb):
    return pl.BlockSpec((8, B_W), lambda i: (jnp.maximum(i * (GTR // 8) - 1, 0), cb))


def _next8(cb, nb):
    return pl.BlockSpec((8, B_W), lambda i: (jnp.minimum((i + 1) * (GTR // 8), nb * (GTR // 8) - 1), cb))


def _gdn_pre_fwd(proj, wconv, par, *, name):
    tt = proj.shape[0]

    def body(q_ref, k_ref, v_ref, qh_ref, kh_ref, vh_ref, ba_ref, w_ref, par_ref, qo_ref, ko_ref, vo_ref, aux_ref):
        first = pl.program_id(0) == 0
        for grp, (x_ref, h_ref, o_ref) in enumerate(((q_ref, qh_ref, qo_ref), (k_ref, kh_ref, ko_ref),
                                                     (v_ref, vh_ref, vo_ref))):
            xe = jnp.concatenate([jnp.where(first, 0.0, h_ref[...]), x_ref[...]], axis=0)
            o_ref[...] = _act_fwd(_conv(_shifts(xe, GTR), _taps(w_ref, grp)), grp)
        bg, _ = _gate_rows(ba_ref[...], par_ref)
        lane = lax.broadcasted_iota(jnp.int32, bg.shape, 1)
        aux_ref[...] = jnp.where(lane < B_HEADS, bg, _dot(_chunk_tri(GTR), bg, HI))

    col = lambda off: _rb(GTR, B_W, off // B_W)
    outs = jax.ShapeDtypeStruct((tt, B_W), F32)
    return pl.pallas_call(
        body, grid=(tt // GTR,),
        in_specs=[col(OFF_QB), col(OFF_KB), col(OFF_VB), _prev8(OFF_QB // B_W), _prev8(OFF_KB // B_W),
                  _prev8(OFF_VB // B_W), _rb(GTR, LANE, OFF_BA // LANE), _whole((CONV_K, 3 * B_W)),
                  _whole((8, LANE))],
        out_specs=(_rb(GTR, B_W), _rb(GTR, B_W), _rb(GTR, B_W), _rb(GTR, LANE)),
        out_shape=(outs, outs, outs, jax.ShapeDtypeStruct((tt, LANE), F32)),
        compiler_params=_cp(("parallel",)), name=name)(proj, proj, proj, proj, proj, proj, proj, wconv, par)


def _gdn_pre_bwd(proj, wconv, par, dq, dk, dv, daux, *, name):
    tt = proj.shape[0]
    nb = tt // GTR

    def body(q_ref, k_ref, v_ref, qh_ref, kh_ref, vh_ref, qn_ref, kn_ref, vn_ref, ba_ref, w_ref, par_ref,
             dq_ref, dk_ref, dv_ref, dqn_ref, dkn_ref, dvn_ref, daux_ref, dx_ref, dba_ref, dw_ref, dpar_ref):
        i = pl.program_id(0)
        first, last = i == 0, i == nb - 1

        @pl.when(first)
        def _():
            dw_ref[...] = jnp.zeros_like(dw_ref)
            dpar_ref[...] = jnp.zeros_like(dpar_ref)

        groups = ((q_ref, qh_ref, qn_ref, dq_ref, dqn_ref), (k_ref, kh_ref, kn_ref, dk_ref, dkn_ref),
                  (v_ref, vh_ref, vn_ref, dv_ref, dvn_ref))
        for grp, (x_ref, h_ref, xn_ref, d_ref, dn_ref) in enumerate(groups):
            taps = _taps(w_ref, grp)
            xe = jnp.concatenate([jnp.where(first, 0.0, h_ref[...]), x_ref[...]], axis=0)
            sh = _shifts(xe, GTR)
            dc = _act_bwd(_conv(sh, taps), d_ref[...], grp)
            xe_n = jnp.concatenate([x_ref[GTR - 8:GTR, :], xn_ref[...]], axis=0)
            dcn = _act_bwd(_conv(_shifts(xe_n, 8), taps), dn_ref[...], grp)
            dce = jnp.concatenate([dc, jnp.where(last, 0.0, dcn)], axis=0)
            dx = taps[CONV_K - 1] * dc
            dw_ref[CONV_K - 1:CONV_K, grp * B_W:(grp + 1) * B_W] += _colsum(dc * sh[0])
            for s in range(1, CONV_K):
                dx = dx + taps[CONV_K - 1 - s] * pltpu.roll(dce, GTR + 8 - s, 0)[0:GTR]
                dw_ref[CONV_K - 1 - s:CONV_K - s, grp * B_W:(grp + 1) * B_W] += _colsum(dc * sh[s])
            dx_ref[:, grp * B_W:(grp + 1) * B_W] = dx.astype(BF16)
        ba = ba_ref[...]
        lane = lax.broadcasted_iota(jnp.int32, ba.shape, 1)
        bg, z = _gate_rows(ba, par_ref)
        daux_v = daux_ref[...]
        dg = _dot(_chunk_tri(GTR, upper=True), daux_v, HI)
        dgl = jnp.where((lane >= B_HEADS) & (lane < 2 * B_HEADS), dg, 0.0)
        da = dgl * (-jnp.exp(par_ref[0:1, :])) * _sigmoid(z)
        dbr = jnp.where(lane < B_HEADS, daux_v * bg * (1.0 - bg), 0.0)
        dba_ref[...] = (dbr + da).astype(BF16)
        dpar_ref[0:1, :] += _colsum(dgl * bg)
        dpar_ref[1:2, :] += _colsum(da)

    col = lambda off: _rb(GTR, B_W, off // B_W)
    row, rowl = _rb(GTR, B_W), _rb(GTR, LANE)
    return pl.pallas_call(
        body, grid=(nb,),
        in_specs=[col(OFF_QB), col(OFF_KB), col(OFF_VB),
                  _prev8(OFF_QB // B_W), _prev8(OFF_KB // B_W), _prev8(OFF_VB // B_W),
                  _next8(OFF_QB // B_W, nb), _next8(OFF_KB // B_W, nb), _next8(OFF_VB // B_W, nb),
                  _rb(GTR, LANE, OFF_BA // LANE), _whole((CONV_K, 3 * B_W)), _whole((8, LANE)),
                  row, row, row, _next8(0, nb), _next8(0, nb), _next8(0, nb), rowl],
        out_specs=(_rb(GTR, 3 * B_W), rowl, _whole((8, 3 * B_W)), _whole((8, LANE))),
        out_shape=(jax.ShapeDtypeStruct((tt, 3 * B_W), BF16), jax.ShapeDtypeStruct((tt, LANE), BF16),
                   jax.ShapeDtypeStruct((8, 3 * B_W), F32), jax.ShapeDtypeStruct((8, LANE), F32)),
        compiler_params=_cp(("arbitrary",)), name=name)(
            proj, proj, proj, proj, proj, proj, proj, proj, proj, proj, wconv, par, dq, dk, dv, dq, dk, dv, daux)


def _col(x, j):
    lane = lax.broadcasted_iota(jnp.int32, x.shape, 1)
    return jnp.sum(jnp.where(lane == j, x, 0.0), axis=-1, keepdims=True)


def _split(x):
    hi = x.astype(BF16)
    return hi, (x - hi.astype(F32)).astype(BF16)


def _dot3(a, b, tn=False):
    dot = _dot_tn if tn else _dot
    (ah, al), (bh, bl) = _split(a), _split(b)
    return dot(ah, bh) + (dot(ah, bl) + dot(al, bh))


def _chunk_masks():
    r = lax.broadcasted_iota(jnp.int32, (CH, CH), 0)
    c = lax.broadcasted_iota(jnp.int32, (CH, CH), 1)
    return r > c, r >= c


def _gc_rows(aux, nc):
    t = jnp.transpose(aux[:, B_HEADS:2 * B_HEADS].reshape(nc, CH, B_HEADS), (0, 2, 1))
    return jnp.concatenate([t, jnp.zeros_like(t)], axis=1).reshape(nc * 8, CH)


_CHUNK8 = lambda width, n=1: pl.BlockSpec((8 * n, width), lambda i: (i, 0))
_CHUNK4 = lambda a, b, n=1: pl.BlockSpec((B_HEADS * n, a, b), lambda i: (i, 0, 0))
NCH = 2


def _per_chunk(body, rows):
    def wrapped(*refs):
        for ci in range(NCH):
            body(*[r.at[pl.ds(ci * n, n)] for r, n in zip(refs, rows)])
    return wrapped


def _gdn_lower(k, aux, auxt, *, name):
    tt = k.shape[0]

    def body(k_ref, aux_ref, auxt_ref, l_ref):
        aux_v = aux_ref[...]
        strict, _ = _chunk_masks()
        for hd in range(B_HEADS):
            kh = k_ref[:, hd * B_DH:(hd + 1) * B_DH].astype(BF16)
            diff = _col(aux_v, B_HEADS + hd) - auxt_ref[hd:hd + 1, :]
            dec = jnp.exp(jnp.where(strict, diff, NEG))
            l_ref[hd] = _col(aux_v, hd) * _dot_nt(kh, kh) * dec

    return pl.pallas_call(
        _per_chunk(body, (CH, CH, 8, B_HEADS)), grid=(tt // CH // NCH,),
        in_specs=[_rb(NCH * CH, B_W), _rb(NCH * CH, LANE), _CHUNK8(CH, NCH)],
        out_specs=_CHUNK4(CH, CH, NCH),
        out_shape=jax.ShapeDtypeStruct((tt // CH * B_HEADS, CH, CH), F32),
        compiler_params=_cp(("parallel",)), name=name)(k, aux, auxt)


def _tri_inverse(lt, *, name):
    nb = lt.shape[2]

    def body(l_ref, t_ref):
        rowid = lax.broadcasted_iota(jnp.int32, (CH, nb), 0)

        def outer(i, carry):
            def inner(j, acc):
                return acc + l_ref[i, pl.ds(j, 1), :] * t_ref[j]

            acc = lax.fori_loop(0, i, inner, jnp.zeros((CH, nb), F32))
            t_ref[i] = jnp.where(rowid == i, 1.0, 0.0) - acc
            return carry

        lax.fori_loop(0, CH, outer, 0)

    return pl.pallas_call(body, out_shape=jax.ShapeDtypeStruct(lt.shape, F32),
                          in_specs=[pl.BlockSpec(memory_space=pltpu.VMEM)],
                          out_specs=pl.BlockSpec(memory_space=pltpu.VMEM),
                          compiler_params=_cp(), name=name)(lt)


def _gdn_gates(aux_v, aux_last, auxt_ref, hd):
    _, incl = _chunk_masks()
    beta = _col(aux_v, hd)
    gc = _col(aux_v, B_HEADS + hd)
    gl = _col(aux_last, B_HEADS + hd)
    dec = jnp.exp(jnp.where(incl, gc - auxt_ref[hd:hd + 1, :], NEG))
    return beta, gc, gl, jnp.exp(gc), dec


def _gdn_intra(q, k, v, aux, auxt, tinv, *, name):
    tt = q.shape[0]
    nc = tt // CH

    def body(q_ref, k_ref, v_ref, aux_ref, auxt_ref, t_ref, u0_ref, w_ref, qd_ref, kd_ref, qk_ref, gle_ref):
        aux_v = aux_ref[...]
        aux_last = aux_ref[CH - 1:CH, :]
        lane8 = lax.broadcasted_iota(jnp.int32, (8, LANE), 1)
        gle = jnp.zeros((8, LANE), F32)
        for hd in range(B_HEADS):
            sl = slice(hd * B_DH, (hd + 1) * B_DH)
            qh, kh, vh = q_ref[:, sl], k_ref[:, sl], v_ref[:, sl]
            beta, gc, gl, egc, dec = _gdn_gates(aux_v, aux_last, auxt_ref, hd)
            qk_ref[hd] = (_dot_nt(qh.astype(BF16), kh.astype(BF16)) * dec).astype(BF16)
            tinv = t_ref[hd]
            u0_ref[:, sl] = _dot3(tinv, vh * beta)
            w_ref[:, sl] = _dot3(tinv, kh * (beta * egc)).astype(BF16)
            qd_ref[:, sl] = (qh * egc).astype(BF16)
            kd_ref[:, sl] = (kh * jnp.exp(gl - gc)).astype(BF16)
            gle = gle + jnp.where(lane8 == hd, jnp.exp(gl), 0.0)
        gle_ref[...] = gle

    row = _rb(NCH * CH, B_W)
    half = jax.ShapeDtypeStruct((tt, B_W), BF16)
    return pl.pallas_call(
        _per_chunk(body, (CH, CH, CH, CH, 8, B_HEADS, CH, CH, CH, CH, B_HEADS, 8)), grid=(nc // NCH,),
        in_specs=[row, row, row, _rb(NCH * CH, LANE), _CHUNK8(CH, NCH), _CHUNK4(CH, CH, NCH)],
        out_specs=(row, row, row, row, _CHUNK4(CH, CH, NCH), _CHUNK8(LANE, NCH)),
        out_shape=(jax.ShapeDtypeStruct((tt, B_W), F32), half, half, half,
                   jax.ShapeDtypeStruct((nc * B_HEADS, CH, CH), BF16), jax.ShapeDtypeStruct((nc * 8, LANE), F32)),
        compiler_params=_cp(("parallel",)), name=name)(q, k, v, aux, auxt, tinv)


def _gdn_scan_fwd(u0, w, qd, kd, qk, gle, *, name):
    tt = u0.shape[0]
    nc = tt // CH

    def body(u0_ref, w_ref, qd_ref, kd_ref, qk_ref, gle_ref, o_ref, ss_ref, u_ref, s_ref):
        @pl.when(pl.program_id(0) == 0)
        def _():
            s_ref[...] = jnp.zeros_like(s_ref)

        gle = gle_ref[0:1, :]
        heads = range(B_HEADS)
        sls = [slice(hd * B_DH, (hd + 1) * B_DH) for hd in heads]
        st = [s_ref[hd] for hd in heads]
        sb = [t.astype(BF16) for t in st]
        ws = [_dot(w_ref[:, sls[hd]], sb[hd]) for hd in heads]
        qs = [_dot(qd_ref[:, sls[hd]], sb[hd]) for hd in heads]
        ub = [(u0_ref[:, sls[hd]] - ws[hd]).astype(BF16) for hd in heads]
        ku = [_dot_tn(kd_ref[:, sls[hd]], ub[hd]) for hd in heads]
        qu = [_dot(qk_ref[hd], ub[hd]) for hd in heads]
        for hd in heads:
            ss_ref[hd] = st[hd]
            u_ref[:, sls[hd]] = ub[hd]
            o_ref[:, sls[hd]] = qs[hd] + qu[hd]
            s_ref[hd] = st[hd] * _col(gle, hd) + ku[hd]

    row = _rb(CH, B_W)
    return pl.pallas_call(
        body, grid=(nc,), in_specs=[row, row, row, row, _CHUNK4(CH, CH), _CHUNK8(LANE)],
        out_specs=(row, _CHUNK4(B_DH, B_DH), row),
        out_shape=(jax.ShapeDtypeStruct((tt, B_W), F32), jax.ShapeDtypeStruct((nc * B_HEADS, B_DH, B_DH), F32),
                   jax.ShapeDtypeStruct((tt, B_W), BF16)),
        scratch_shapes=[pltpu.VMEM((B_HEADS, B_DH, B_DH), F32)],
        compiler_params=_cp(("arbitrary",)), name=name)(u0, w, qd, kd, qk, gle)


def _gdn_scan_bwd(w, qd, kd, qk, gle, do, *, name):
    tt = w.shape[0]
    nc = tt // CH

    def body(w_ref, qd_ref, kd_ref, qk_ref, gle_ref, do_ref, du_ref, dss_ref, ds_ref):
        @pl.when(pl.program_id(0) == 0)
        def _():
            ds_ref[...] = jnp.zeros_like(ds_ref)

        gle = gle_ref[0:1, :]
        heads = range(B_HEADS)
        sls = [slice(hd * B_DH, (hd + 1) * B_DH) for hd in heads]
        dst = [ds_ref[hd] for hd in heads]
        dob = [do_ref[:, sls[hd]].astype(BF16) for hd in heads]
        kds = [_dot(kd_ref[:, sls[hd]], dst[hd].astype(BF16)) for hd in heads]
        qkd = [_dot_tn(qk_ref[hd], dob[hd]) for hd in heads]
        qdd = [_dot_tn(qd_ref[:, sls[hd]], dob[hd]) for hd in heads]
        du = [qkd[hd] + kds[hd] for hd in heads]
        wdu = [_dot_tn(w_ref[:, sls[hd]], du[hd].astype(BF16)) for hd in heads]
        for hd in heads:
            dss_ref[hd] = dst[hd]
            du_ref[:, sls[hd]] = du[hd]
            ds_ref[hd] = qdd[hd] + _col(gle, hd) * dst[hd] - wdu[hd]

    rev = lambda width: pl.BlockSpec((CH, width), lambda i: (nc - 1 - i, 0))
    rev4 = lambda a, b: pl.BlockSpec((B_HEADS, a, b), lambda i: (nc - 1 - i, 0, 0))
    return pl.pallas_call(
        body, grid=(nc,),
        in_specs=[rev(B_W), rev(B_W), rev(B_W), rev4(CH, CH), pl.BlockSpec((8, LANE), lambda i: (nc - 1 - i, 0)), rev(B_W)],
        out_specs=(rev(B_W), rev4(B_DH, B_DH)),
        out_shape=(jax.ShapeDtypeStruct((tt, B_W), F32), jax.ShapeDtypeStruct((nc * B_HEADS, B_DH, B_DH), F32)),
        scratch_shapes=[pltpu.VMEM((B_HEADS, B_DH, B_DH), F32)],
        compiler_params=_cp(("arbitrary",)), name=name)(w, qd, kd, qk, gle, do)


def _gdn_bwd(q, k, v, aux, auxt, tinv, u0, w, u, ss, dss, du, do, *, name):
    tt = q.shape[0]
    nc = tt // CH

    def body(q_ref, k_ref, v_ref, aux_ref, auxt_ref, t_ref, u0_ref, w_ref, u_ref, ss_ref, dss_ref, du_ref, do_ref,
             dq_ref, dk_ref, dv_ref, daux_ref):
        aux_v = aux_ref[...]
        aux_last = aux_ref[CH - 1:CH, :]
        lane = lax.broadcasted_iota(jnp.int32, (CH, LANE), 1)
        rowi = lax.broadcasted_iota(jnp.int32, (CH, 1), 0)
        strict, incl = _chunk_masks()
        daux = jnp.zeros((CH, LANE), F32)
        heads = range(B_HEADS)
        sls = [slice(hd * B_DH, (hd + 1) * B_DH) for hd in heads]
        gates = [_gdn_gates(aux_v, aux_last, auxt_ref, hd) for hd in heads]
        kbs = [k_ref[:, sl].astype(BF16) for sl in sls]
        qbs = [q_ref[:, sl].astype(BF16) for sl in sls]
        sbs = [ss_ref[hd].astype(BF16) for hd in heads]
        dsbs = [dss_ref[hd].astype(BF16) for hd in heads]
        dobs = [do_ref[:, sl].astype(BF16) for sl in sls]
        kks = [_dot_nt(kbs[hd], kbs[hd]) for hd in heads]
        qk0s = [_dot_nt(qbs[hd], kbs[hd]) for hd in heads]
        dq_decs = [_dot_nt(dobs[hd], sbs[hd]) for hd in heads]
        dqks = [_dot_nt(dobs[hd], u_ref[:, sls[hd]]) for hd in heads]
        dk_decs = [_dot_nt(u_ref[:, sls[hd]], dsbs[hd]) for hd in heads]
        dws = [-_dot_nt(du_ref[:, sls[hd]].astype(BF16), sbs[hd]) for hd in heads]
        drvs = [_dot3(t_ref[hd], du_ref[:, sls[hd]], tn=True) for hd in heads]
        drks = [_dot3(t_ref[hd], dws[hd], tn=True) for hd in heads]
        dls = [-(_dot_nt(drvs[hd].astype(BF16), u0_ref[:, sls[hd]].astype(BF16))
                 + _dot_nt(drks[hd].astype(BF16), w_ref[:, sls[hd]])) for hd in heads]
        for hd in heads:
            sl = sls[hd]
            qh, kh, vh = q_ref[:, sl], k_ref[:, sl], v_ref[:, sl]
            beta, gc, gl, egc, dec = gates[hd]
            ekd, eg_last = jnp.exp(gl - gc), jnp.exp(gl)
            kb, qb, kk, qk0 = kbs[hd], qbs[hd], kks[hd], qk0s[hd]
            st, dst = ss_ref[hd], dss_ref[hd]
            dq_dec, dk_dec = dq_decs[hd], dk_decs[hd]
            dqk = jnp.where(incl, dqks[hd], 0.0)
            dgl = jnp.sum(jnp.sum(st * dst, axis=-1, keepdims=True), axis=0, keepdims=True) * eg_last
            drv, drk = drvs[hd], drks[hd]
            dl = jnp.where(strict, dls[hd], 0.0)
            dv_ref[:, sl] = drv * beta
            rk = jnp.sum(drk * kh, axis=-1, keepdims=True)
            dbeta = jnp.sum(drv * vh, axis=-1, keepdims=True) + rk * egc
            dgc = rk * beta * egc
            dk = drk * (beta * egc)
            ldec = dl * dec
            dbeta = dbeta + jnp.sum(ldec * kk, axis=-1, keepdims=True)
            dkk = (ldec * beta).astype(BF16)
            dqk0 = (dqk * dec).astype(BF16)
            ddec = ldec * beta * kk + dqk * (qk0 * dec)
            dq = _dot(dqk0, kb) + dq_dec * egc
            dk = dk + _dot_tn(dqk0, qb) + _dot(dkk, kb) + _dot_tn(dkk, kb) + dk_dec * ekd
            dgc = dgc + jnp.sum(ddec, axis=-1, keepdims=True) - _col_from_rowsum(ddec)
            dgc = dgc + jnp.sum(dq_dec * qh, axis=-1, keepdims=True) * egc
            kd = jnp.sum(dk_dec * kh, axis=-1, keepdims=True) * ekd
            dgc = dgc - kd
            dgc = dgc + jnp.where(rowi == CH - 1, jnp.sum(kd, axis=0, keepdims=True) + dgl, 0.0)
            dq_ref[:, sl] = dq
            dk_ref[:, sl] = dk
            daux = daux + jnp.where(lane == hd, dbeta, 0.0) + jnp.where(lane == B_HEADS + hd, dgc, 0.0)
        daux_ref[...] = daux

    row = _rb(NCH * CH, B_W)
    outs = jax.ShapeDtypeStruct((tt, B_W), F32)
    return pl.pallas_call(
        _per_chunk(body, (CH, CH, CH, CH, 8, B_HEADS, CH, CH, CH, B_HEADS, B_HEADS, CH, CH, CH, CH, CH, CH)),
        grid=(nc // NCH,),
        in_specs=[row, row, row, _rb(NCH * CH, LANE), _CHUNK8(CH, NCH), _CHUNK4(CH, CH, NCH), row, row, row,
                  _CHUNK4(B_DH, B_DH, NCH), _CHUNK4(B_DH, B_DH, NCH), row, row],
        out_specs=(row, row, row, _rb(NCH * CH, LANE)),
        out_shape=(outs, outs, outs, jax.ShapeDtypeStruct((tt, LANE), F32)),
        compiler_params=_cp(("parallel",)), name=name)(q, k, v, aux, auxt, tinv, u0, w, u, ss, dss, du, do)


def _col_from_rowsum(m):
    hi, lo = _split(m)
    ones = jnp.ones((CH, LANE), BF16)
    return (_dot_tn(hi, ones) + _dot_tn(lo, ones))[:, 0:1]


def _gdn_post_fwd(o, proj, gn, *, name, tr=256):
    tt = o.shape[0]

    def body(o_ref, z_ref, g_ref, y_ref):
        for hd in range(B_HEADS):
            sl = slice(hd * B_DH, (hd + 1) * B_DH)
            oh = o_ref[:, sl]
            r = lax.rsqrt(jnp.mean(oh * oh, axis=-1, keepdims=True) + EPS)
            y_ref[:, sl] = (oh * r * g_ref[...] * _silu(z_ref[:, sl])).astype(BF16)

    return pl.pallas_call(body, grid=(tt // tr,), in_specs=[_rb(tr, B_W), _rb(tr, B_W, OFF_ZB // B_W), _whole((1, B_DH))],
                          out_specs=_rb(tr, B_W), out_shape=jax.ShapeDtypeStruct((tt, B_W), BF16),
                          compiler_params=_cp(("parallel",)), name=name)(o, proj, gn)


def _gdn_post_bwd(o, proj, gn, dy, *, name, tr=256):
    tt = o.shape[0]

    def body(o_ref, z_ref, g_ref, dy_ref, do_ref, dz_ref, dg_ref):
        @pl.when(pl.program_id(0) == 0)
        def _():
            dg_ref[...] = jnp.zeros_like(dg_ref)

        g = g_ref[...]
        for hd in range(B_HEADS):
            sl = slice(hd * B_DH, (hd + 1) * B_DH)
            oh, zh, dyh = o_ref[:, sl], z_ref[:, sl], dy_ref[:, sl]
            r = lax.rsqrt(jnp.mean(oh * oh, axis=-1, keepdims=True) + EPS)
            a = oh * r
            s = _silu(zh)
            da = dyh * g * s
            dg_ref[0:1, :] += _colsum(dyh * a * s)
            dz_ref[:, sl] = (dyh * a * g * _dsilu(zh)).astype(BF16)
            do_ref[:, sl] = r * (da - a * jnp.mean(da * a, axis=-1, keepdims=True))

    return pl.pallas_call(
        body, grid=(tt // tr,), in_specs=[_rb(tr, B_W), _rb(tr, B_W, OFF_ZB // B_W), _whole((1, B_DH)), _rb(tr, B_W)],
        out_specs=(_rb(tr, B_W), _rb(tr, B_W), _whole((8, B_DH))),
        out_shape=(jax.ShapeDtypeStruct((tt, B_W), F32), jax.ShapeDtypeStruct((tt, B_W), BF16),
                   jax.ShapeDtypeStruct((8, B_DH), F32)),
        compiler_params=_cp(("arbitrary",)), name=name)(o, proj, gn, dy)


def _adamw(parts, w, m, v, own=None, sel=None, *, name, tr=256):
    npart, nl, r, c = parts.shape
    tr = max([t for t in range(8, min(r, tr) + 1, 8) if r % t == 0], default=r)
    c1, c2 = 1.0 - ADAM_B1 ** ADAM_STEP, 1.0 - ADAM_B2 ** ADAM_STEP

    def body(*refs):
        if own is None:
            p_ref, w_ref, m_ref, v_ref, g_ref, d_ref, mo_ref, vo_ref = refs
            part = lambda i: p_ref[i].astype(F32)
        else:
            p_ref, w_ref, m_ref, v_ref, own_ref, sel_ref, g_ref, d_ref, mo_ref, vo_ref = refs
            part = lambda i: jnp.where(sel_ref[i:i + 1, 0:1] > 0.5, own_ref[...].astype(F32), p_ref[i].astype(F32))
        g = part(0)
        for i in range(1, npart):
            g = g + part(i)
        mn = ADAM_B1 * m_ref[...] + (1.0 - ADAM_B1) * g
        vn = ADAM_B2 * v_ref[...] + (1.0 - ADAM_B2) * (g * g)
        g_ref[...] = g
        mo_ref[...] = mn
        vo_ref[...] = vn
        d_ref[...] = -ADAM_LR * ((mn / c1) / (jnp.sqrt(vn / c2) + ADAM_EPS) + ADAM_WD * w_ref[...])

    row = pl.BlockSpec((None, tr, c), lambda l, i: (l, i, 0))
    out = jax.ShapeDtypeStruct((nl, r, c), F32)
    ins, in_specs = [parts, w, m, v], [pl.BlockSpec((npart, None, tr, c), lambda l, i: (0, l, i, 0)), row, row, row]
    if own is not None:
        ins += [own, sel]
        in_specs += [row, pl.BlockSpec((N_DEV, LANE), lambda l, i: (0, 0))]
    return pl.pallas_call(body, grid=(nl, r // tr), in_specs=in_specs, out_specs=(row, row, row, row),
                          out_shape=(out, out, out, out), compiler_params=_cp(("parallel", "parallel")),
                          name=name)(*ins)


def _peer(k):
    x, y, c = lax.axis_index("x"), lax.axis_index("y"), lax.axis_index("c")
    return ((1 - x) if k & 4 else x, (1 - y) if k & 2 else y, (1 - c) if k & 1 else c)


def _my_index():
    return 4 * lax.axis_index("x") + 2 * lax.axis_index("y") + lax.axis_index("c")


def _index_of(p):
    return 4 * p[0] + 2 * p[1] + p[2]


def _all_gather(xs, *, name):
    n = len(xs)

    def body(*refs):
        x_refs, o_refs = refs[:n], refs[n:2 * n]
        send, recv, loc = refs[2 * n:]
        me = _my_index()
        copies = []
        for a in range(n):
            cp = pltpu.make_async_copy(x_refs[a], o_refs[a].at[me], loc.at[a])
            cp.start()
            copies.append(cp)
        rdmas = []
        for a in range(n):
            for k in range(1, N_DEV):
                r = pltpu.make_async_remote_copy(
                    src_ref=x_refs[a], dst_ref=o_refs[a].at[me], send_sem=send.at[a, k - 1], recv_sem=recv.at[a, k - 1],
                    device_id=_peer(k), device_id_type=pl.DeviceIdType.MESH)
                r.start()
                rdmas.append(r)
        for a in range(n):
            for k in range(1, N_DEV):
                pltpu.make_async_remote_copy(
                    src_ref=x_refs[a], dst_ref=o_refs[a].at[_index_of(_peer(k))], send_sem=send.at[a, k - 1],
                    recv_sem=recv.at[a, k - 1], device_id=_peer(k), device_id_type=pl.DeviceIdType.MESH).wait_recv()
        for r in rdmas:
            r.wait_send()
        for cp in copies:
            cp.wait()

    any_spec = pl.BlockSpec(memory_space=pl.ANY)
    return pl.pallas_call(
        body, in_specs=[any_spec] * n, out_specs=tuple([any_spec] * n),
        out_shape=tuple(jax.ShapeDtypeStruct((N_DEV,) + x.shape, x.dtype) for x in xs),
        scratch_shapes=[pltpu.SemaphoreType.DMA((n, N_DEV - 1)), pltpu.SemaphoreType.DMA((n, N_DEV - 1)),
                        pltpu.SemaphoreType.DMA((n,))],
        name=name)(*xs)


_HBM = pl.BlockSpec(memory_space=pltpu.HBM)
_SEM = pl.BlockSpec(memory_space=pltpu.SEMAPHORE)
_EFFECT = pltpu.SideEffectType.DATAFLOW_SIDE_EFFECTING


def _split_copy(src_ref, land_ref, send, recv, a, k, scatter, slot, sending):
    me, peer = _my_index(), _index_of(_peer(k))
    src = src_ref.at[peer if sending else me] if scatter else src_ref
    land = land_ref.at[me if sending else peer]
    if slot is not None:
        land = land.at[slot]
    sem = a * (N_DEV - 1) + k - 1
    return pltpu.make_async_remote_copy(src_ref=src, dst_ref=land, send_sem=send.at[sem], recv_sem=recv.at[sem],
                                        device_id=_peer(k), device_id_type=pl.DeviceIdType.MESH)


def _exchange_start(srcs, lands, after, *, scatter, slot=None, name):
    n = len(srcs)

    def body(*refs):
        src_refs, land_refs = refs[:n], refs[n:2 * n]
        send, recv, token = refs[2 * n + 1], refs[2 * n + 2], refs[-1]
        for a in range(n):
            for k in range(1, N_DEV):
                _split_copy(src_refs[a], land_refs[a], send, recv, a, k, scatter, slot, True).start()
        token[...] = jnp.zeros_like(token)

    hbm = lambda t: pltpu.HBM(t.shape, t.dtype)
    sems = pltpu.SemaphoreType.DMA((n * (N_DEV - 1),))
    out = pl.pallas_call(
        body, name=name,
        out_shape=(sems, sems, *[hbm(t) for t in srcs], *[hbm(t) for t in lands], jax.ShapeDtypeStruct((8, LANE), F32)),
        in_specs=[_HBM] * (2 * n) + [pl.BlockSpec(memory_space=pl.ANY)],
        out_specs=(_SEM, _SEM, *[_HBM] * (2 * n), pl.BlockSpec(memory_space=pltpu.VMEM)),
        input_output_aliases={i: 2 + i for i in range(2 * n)},
        compiler_params=pltpu.CompilerParams(has_side_effects=_EFFECT),
    )(*[pltpu.with_memory_space_constraint(t, pltpu.HBM) for t in (*srcs, *lands)], after)
    return out[0], out[1], out[2:2 + n], out[2 + n:2 + 2 * n], out[-1]


def _exchange_wait(send, recv, srcs, lands, after, *, scatter, slot=None, name):
    n = len(srcs)

    def body(*refs):
        src_refs, land_refs = refs[:n], refs[n:2 * n]
        send_ref, recv_ref = refs[2 * n], refs[2 * n + 1]
        for a in range(n):
            for k in range(1, N_DEV):
                _split_copy(src_refs[a], land_refs[a], send_ref, recv_ref, a, k, scatter, slot, True).wait_send()
                _split_copy(src_refs[a], land_refs[a], send_ref, recv_ref, a, k, scatter, slot, False).wait_recv()

    hbm = lambda t: pltpu.HBM(t.shape, t.dtype)
    out = pl.pallas_call(
        body, name=name, out_shape=(*[hbm(t) for t in srcs], *[hbm(t) for t in lands]),
        in_specs=[_HBM] * (2 * n) + [_SEM, _SEM, pl.BlockSpec(memory_space=pl.ANY)],
        out_specs=tuple([_HBM] * (2 * n)), input_output_aliases={i: i for i in range(2 * n)},
        compiler_params=pltpu.CompilerParams(has_side_effects=_EFFECT),
    )(*srcs, *lands, send, recv, after)
    return out[:n], out[n:]


def _win_to_mine(w):
    pad = jnp.zeros(w.shape[:-1] + (IN_PAD - IN_DIM,), w.dtype)
    return jnp.concatenate([w[..., 3592:5640], w[..., 0:3584], w[..., 3584:3592], pad], axis=-1)


def _win_from_mine(g):
    return jnp.concatenate([g[..., 2048:5632], g[..., 5632:5640], g[..., 0:2048]], axis=-1)


def _pad_rows(a, mult=8):
    r = (-a.shape[0]) % mult
    return a if r == 0 else jnp.concatenate([a, jnp.zeros((r,) + a.shape[1:], a.dtype)], axis=0)


def _lanes(vec, start):
    return jnp.zeros((1, LANE), F32).at[0, start:start + vec.shape[0]].set(vec)


def _small_spec(depth):
    return (("b_ada", (depth, 6 * D)), ("norm1_g", (depth, D)), ("norm2_g", (depth, D)),
            ("rel_table", (depth, A_HEADS, 2 * A_MAX_REL + 1)), ("a_log", (depth, B_HEADS)),
            ("dt_bias", (depth, B_HEADS)), ("gdn_norm_g", (depth, B_DH)), ("final_g", (D,)))


def _pack_small(d, extra, depth):
    spec = _small_spec(depth)
    rows = -(-(sum(math.prod(s) for _, s in spec) + 1) // (8 * LANE)) * 8
    flat = jnp.concatenate([d[n].reshape(-1).astype(F32) for n, _ in spec] + [extra.reshape(-1)])
    flat = jnp.concatenate([flat, jnp.zeros((rows * LANE - flat.shape[0],), F32)])
    return flat.reshape(rows, LANE)


def _unpack_small(p, depth):
    flat = p.reshape(-1)
    out, off = {}, 0
    for n, s in _small_spec(depth):
        sz = math.prod(s)
        out[n] = flat[off:off + sz].reshape(s)
        off += sz
    return out, flat[off]


def kernel(x, c, w_ada, b_ada, norm1_g, norm2_g, w_in, rel_table, w_conv, a_log, dt_bias, gdn_norm_g, w_branch_a, w_branch_b, w_out, w_ffn_in, w_ffn_out, final_g, loss_target, m_w_ada, m_b_ada, m_norm1_g, m_norm2_g, m_w_in, m_rel_table, m_w_conv, m_a_log, m_dt_bias, m_gdn_norm_g, m_w_branch_a, m_w_branch_b, m_w_out, m_w_ffn_in, m_w_ffn_out, m_final_g, v_w_ada, v_b_ada, v_norm1_g, v_norm2_g, v_w_in, v_rel_table, v_w_conv, v_a_log, v_dt_bias, v_gdn_norm_g, v_w_branch_a, v_w_branch_b, v_w_out, v_w_ffn_in, v_w_ffn_out, v_final_g):
    tt = x.shape[1]
    x0 = x[0]
    tgt = loss_target[0]
    me = _my_index()
    depth = w_in.shape[0]

    shards = [w_in.astype(BF16), w_branch_a.astype(BF16), w_branch_b.astype(BF16), w_out.astype(BF16),
              w_ffn_in.astype(BF16), w_ffn_out.astype(BF16), w_conv]
    names = ("win", "wa", "wb", "wout", "wfi", "wfo", "wconv")
    early, late, every = (0, 6), (1, 2, 3, 4, 5), tuple(range(7))
    first = _all_gather([shards[i][0] for i in early] + [_pad_rows(c)], name="gather_first")
    c_all = first[-1][:, 0, :]
    is_me = lax.broadcasted_iota(jnp.int32, (N_DEV, 1, 1), 0) == me

    def unpack(idx, g):
        cols = lambda t: jnp.transpose(t, (1, 0, 2)).reshape(t.shape[1], N_DEV * t.shape[2])
        rows = lambda t: t.reshape(N_DEV * t.shape[1], t.shape[2])
        how = (lambda t: _win_to_mine(cols(t)), cols, cols, rows, cols, rows, cols)
        return {names[i]: how[i](t) for i, t in zip(idx, g)}

    def gather_start(l, idx, after, tag=""):
        srcs = [shards[i][l] for i in idx]
        lands = [lax.empty((N_DEV,) + t.shape, t.dtype) for t in srcs]
        return _exchange_start(srcs, lands, after, scatter=False, name=f"gather_start_{l}{tag}")

    def gather_wait(l, idx, pending, after, tag=""):
        send, recv, srcs, lands, _ = pending
        srcs, lands = _exchange_wait(send, recv, srcs, lands, after, scatter=False, name=f"gather_wait_{l}{tag}")
        return unpack(idx, [jnp.where(is_me, t[None], g) for g, t in zip(lands, srcs)])

    weights = [unpack(early, first[:-1])] + [None] * (depth - 1)
    pending0 = gather_start(0, late, first[-1], "_rest")
    pending = gather_start(1, every, pending0[-1]) if depth > 1 else None
    cond = c_all * (1.0 / (1.0 + jnp.exp(-c_all)))
    cond = _pad_rows(cond, 16)

    mod_cols = jnp.stack([_mm(cond, w_ada[l], name="mod_mm")[:N_DEV] for l in range(depth)])
    (g_mod,) = _all_gather([mod_cols], name="gather_mod")
    mod_all = jnp.transpose(g_mod, (1, 2, 0, 3)).reshape(depth, N_DEV, 6 * D)
    mod = lax.dynamic_index_in_dim(mod_all, me, axis=1, keepdims=False) + b_ada
    mods = mod.reshape(depth, 6, 1, D)

    n1g, n2g = norm1_g.reshape(depth, 1, D), norm2_g.reshape(depth, 1, D)
    gng = gdn_norm_g.reshape(depth, 1, B_DH)
    fg = final_g.reshape(1, D)

    saved = []
    tok = (pending if pending is not None else pending0)[-1][0, 0]
    xin, h1 = _adaln_fwd(x0, n1g[0], mods[0, 1] + tok, mods[0, 0], name="adaln1_first")
    for l in range(depth):
        sh1, sc1, gt1, sh2, sc2, gt2 = (mods[l, i] for i in range(6))
        wl = weights[l]
        proj = _mm(h1, wl["win"], name="proj_mm", tn=1152)
        kpad, vpad = _kv_pad(proj, name="kv_pad")
        diag, bias_vjp = jax.vjp(_bias_diagonals, rel_table[l])
        bias = _bias_fwd(diag, name="bias_fwd")
        ya, lse = _attn_fwd(proj, kpad, vpad, bias, name="attn_fwd")
        par = jnp.concatenate([_lanes(a_log[l], B_HEADS), _lanes(dt_bias[l], B_HEADS), jnp.zeros((6, LANE), F32)], axis=0)
        qn, kn, vn, aux = _gdn_pre_fwd(proj, wl["wconv"], par, name="gdn_pre_fwd")
        auxt = _gc_rows(aux, tt // CH)
        lower = _gdn_lower(kn, aux, auxt, name="gdn_lower")
        tinv = jnp.transpose(_tri_inverse(jnp.transpose(lower, (1, 2, 0)), name="gdn_tri_inverse"), (2, 0, 1))
        u0, wg, qd, kd, qk, gle = _gdn_intra(qn, kn, vn, aux, auxt, tinv, name="gdn_intra")
        og, ss, ug = _gdn_scan_fwd(u0, wg, qd, kd, qk, gle, name="gdn_scan_fwd")
        yb = _gdn_post_fwd(og, proj, gng[l], name="gdn_post_fwd")
        if l == 0:
            wl.update(gather_wait(0, late, pending0, yb, "_rest"))
        pa = _mm(ya, wl["wa"], out_dtype=BF16, name="branch_a_mm")
        pb = _mm(yb, wl["wb"], out_dtype=BF16, name="branch_b_mm")
        merged = _merge_fwd(proj, pa, pb, name="merge_fwd")
        t1 = _mm(merged, wl["wout"], name="out_mm")
        x2, h2 = _adaln_fwd(xin, n2g[l], sc2, sh2, t1, gt1, name="adaln2_fwd")
        gu = _mm(h2, wl["wfi"], out_dtype=BF16, name="ffn_in_mm", tn=1408)
        act = _swiglu_fwd(gu, name="swiglu_fwd")
        t2 = _mm(act, wl["wfo"], name="ffn_out_mm", tk=1408)
        saved.append(dict(xin=xin, h1=h1, proj=proj, kpad=kpad, vpad=vpad, bias=bias, bias_vjp=bias_vjp, ya=ya, lse=lse,
                          par=par, qn=qn, kn=kn, vn=vn, aux=aux, auxt=auxt, tinv=tinv, ss=ss, og=og, yb=yb, pa=pa, pb=pb,
                          u0=u0, wg=wg, qd=qd, kd=kd, qk=qk, gle=gle, ug=ug,
                          merged=merged, t1=t1, x2=x2, h2=h2, gu=gu, act=act, t2=t2))
        if l + 1 < depth:
            weights[l + 1] = gather_wait(l + 1, every, pending, t2)
            pending = gather_start(l + 2, every, weights[l + 1]["wconv"]) if l + 2 < depth else None
            tok = pending[-1][0, 0] if pending is not None else 0.0
            xin, h1 = _adaln_fwd(x2, n1g[l + 1], mods[l + 1, 1] + tok, mods[l + 1, 0], t2, gt2, name="adaln1_fwd")

    s = saved[-1]
    dx, dt2, st = _loss_head(s["x2"], s["t2"], mods[depth - 1, 5], fg, tgt, name="loss_head")
    loss_part = st[4, 0]
    small_g = {"final_g": st[0]}
    dmod_rows = [None] * depth
    for n in ("norm1_g", "norm2_g", "rel_table", "a_log", "dt_bias", "gdn_norm_g"):
        small_g[n] = [None] * depth
    dgt2 = st[3]
    cols_slabs = lambda g: jnp.transpose(g.reshape(g.shape[0], N_DEV, g.shape[1] // N_DEV), (1, 0, 2))
    rows_slabs = lambda g: g.reshape(N_DEV, g.shape[0] // N_DEV, g.shape[1])
    mix, ffn = (0, 1, 2, 3, 6), (4, 5)
    lands = {kind: [lax.empty((N_DEV,) + shards[i].shape, shards[i].dtype) for i in idx]
             for kind, idx in (("mix", mix), ("ffn", ffn))}
    own = {kind: [None] * depth for kind in lands}
    pending_s = {kind: None for kind in lands}

    def scatter(kind, l, srcs, after):
        if pending_s[kind] is not None:
            done, lands[kind] = _exchange_wait(*pending_s[kind][:4], after, scatter=True, slot=l + 1,
                                               name=f"scatter_wait_{kind}_{l + 1}")
            own[kind][l + 1] = [lax.dynamic_index_in_dim(t, me, 0, keepdims=False) for t in done]
        pending_s[kind] = _exchange_start(srcs, lands[kind], after, scatter=True, slot=l, name=f"scatter_start_{kind}_{l}")
        return pending_s[kind][-1][0, 0]

    for l in reversed(range(depth)):
        s, wl = saved[l], weights[l]
        sh1, sc1, gt1, sh2, sc2, gt2 = (mods[l, i] for i in range(6))
        gw_fo = _mm(s["act"], dt2, ta=True, out_dtype=BF16, name="ffn_out_dw", tm=1408)
        dact = _mm(dt2, wl["wfo"], tb=True, out_dtype=BF16, name="ffn_out_dx", tn=1408)
        dgu = _swiglu_bwd(s["gu"], dact, name="swiglu_bwd")
        gw_fi = _mm(s["h2"], dgu, ta=True, out_dtype=BF16, name="ffn_in_dw", tn=1408)
        sc2 = sc2 + scatter("ffn", l, [cols_slabs(gw_fi), rows_slabs(gw_fo)], gw_fi)
        dh2 = _mm(dgu, wl["wfi"], tb=True, name="ffn_in_dx", tk=1408)
        dx, dt1, st2 = _adaln_bwd(s["x2"], n2g[l], sc2, sh2, dh2, dx, s["t1"], gt1, name="adaln2_bwd")
        gw_out = _mm(s["merged"], dt1, ta=True, out_dtype=BF16, name="out_dw")
        dmerged = _mm(dt1, wl["wout"], tb=True, name="out_dx")
        dgates, dpa, dpb = _merge_bwd(s["proj"], s["pa"], s["pb"], dmerged, name="merge_bwd")
        gw_a = _mm(s["ya"], dpa, ta=True, out_dtype=BF16, name="branch_a_dw")
        gw_b = _mm(s["yb"], dpb, ta=True, out_dtype=BF16, name="branch_b_dw")
        dya = _mm(dpa, wl["wa"], tb=True, name="branch_a_dx")
        dyb = _mm(dpb, wl["wb"], tb=True, name="branch_b_dx")
        dqa, dka, dva, dbias = _attn_bwd(s["proj"], s["kpad"], s["vpad"], s["bias"], s["ya"], s["lse"], dya,
                                             name="attn_bwd")
        ddiag = jnp.roll(_bias_bwd(jnp.flip(dbias, axis=1), name="bias_bwd"), -(QB - 1), axis=1)
        small_g["rel_table"][l] = s["bias_vjp"](ddiag)[0]
        dog, dz, dgn = _gdn_post_bwd(s["og"], s["proj"], gng[l], dyb, name="gdn_post_bwd")
        small_g["gdn_norm_g"][l] = dgn[0]
        dug, dss = _gdn_scan_bwd(s["wg"], s["qd"], s["kd"], s["qk"], s["gle"], dog, name="gdn_scan_bwd")
        dqn, dkn, dvn, daux = _gdn_bwd(s["qn"], s["kn"], s["vn"], s["aux"], s["auxt"], s["tinv"], s["u0"], s["wg"],
                                       s["ug"], s["ss"], dss, dug, dog, name="gdn_bwd")
        dqkv, dba, dwc, dpar = _gdn_pre_bwd(s["proj"], wl["wconv"], s["par"], dqn, dkn, dvn, daux, name="gdn_pre_bwd")
        small_g["a_log"][l] = dpar[0, B_HEADS:2 * B_HEADS]
        small_g["dt_bias"][l] = dpar[1, B_HEADS:2 * B_HEADS]
        dproj = jnp.concatenate([dgates, dqa, dka, dva, dqkv, dz, dba], axis=1)
        gw_in = _mm(s["h1"], dproj, ta=True, out_dtype=BF16, name="proj_dw", tn=1152)
        dh1 = _mm(dproj, wl["win"], tb=True, name="proj_dx", tk=1152)
        sc1 = sc1 + scatter("mix", l, [cols_slabs(_win_from_mine(gw_in)), cols_slabs(gw_a), cols_slabs(gw_b),
                                       rows_slabs(gw_out), cols_slabs(dwc[0:CONV_K])], gw_in)
        if l > 0:
            p = saved[l - 1]
            dx, dt2, st1 = _adaln_bwd(s["xin"], n1g[l], sc1, sh1, dh1, dx, p["t2"], mods[l - 1, 5], name="adaln1_bwd")
        else:
            dx, st1 = _adaln_bwd(s["xin"], n1g[l], sc1, sh1, dh1, dx, name="adaln1_bwd_first")
        small_g["norm1_g"][l], small_g["norm2_g"][l] = st1[0], st2[0]
        dmod_rows[l] = jnp.concatenate([st1[2], st1[1], st2[3], st2[2], st2[1], dgt2])
        if l > 0:
            dgt2 = st1[3]
    grad_x = dx[None]

    small_local = {n: (jnp.stack(vs) if isinstance(vs, list) else vs) for n, vs in small_g.items()}
    small_local["b_ada"] = jnp.stack(dmod_rows)
    (g_small,) = _all_gather([_pack_small(small_local, loss_part, depth)], name="gather_small")
    wsm = _pack_small(dict(b_ada=b_ada, norm1_g=norm1_g, norm2_g=norm2_g, rel_table=rel_table, a_log=a_log,
                           dt_bias=dt_bias, gdn_norm_g=gdn_norm_g, final_g=final_g), jnp.zeros((1,), F32), depth)
    msm = _pack_small(dict(b_ada=m_b_ada, norm1_g=m_norm1_g, norm2_g=m_norm2_g, rel_table=m_rel_table, a_log=m_a_log,
                           dt_bias=m_dt_bias, gdn_norm_g=m_gdn_norm_g, final_g=m_final_g), jnp.zeros((1,), F32), depth)
    vsm = _pack_small(dict(b_ada=v_b_ada, norm1_g=v_norm1_g, norm2_g=v_norm2_g, rel_table=v_rel_table, a_log=v_a_log,
                           dt_bias=v_dt_bias, gdn_norm_g=v_gdn_norm_g, final_g=v_final_g), jnp.ones((1,), F32), depth)
    sm = [_unpack_small(t, depth) for t in _adamw(g_small[:, None], wsm[None], msm[None], vsm[None], name="adamw_small")]
    loss = sm[0][1]

    dmod_all = g_small.reshape(N_DEV, -1)[:, :depth * 6 * D].reshape(N_DEV, depth, 6 * D)
    dmod_mine = lax.dynamic_slice_in_dim(dmod_all, me * (6 * D // N_DEV), 6 * D // N_DEV, axis=2)
    g_ada = jnp.stack([_mm(cond, _pad_rows(dmod_mine[:, l], 16), ta=True, name="ada_dw") for l in range(depth)])

    got, mine = {}, {}
    for kind, idx in (("ffn", ffn), ("mix", mix)):
        done, lands[kind] = _exchange_wait(*pending_s[kind][:4], g_ada, scatter=True, slot=0, name=f"scatter_wait_{kind}_0")
        own[kind][0] = [lax.dynamic_index_in_dim(t, me, 0, keepdims=False) for t in done]
        for a, i in enumerate(idx):
            got[i] = lands[kind][a]
            mine[i] = jnp.stack([own[kind][l][a] for l in range(depth)])
    sel = jnp.broadcast_to(jnp.where(is_me[:, :, 0], 1.0, 0.0), (N_DEV, LANE)).astype(F32)

    def upd(i, w, m, v, name):
        return _adamw(got[i], w, m, v, mine[i], sel, name=name)

    res = {
        "w_ada": _adamw(g_ada[None], w_ada, m_w_ada, v_w_ada, name="adamw_w_ada"),
        "w_in": upd(0, w_in, m_w_in, v_w_in, "adamw_w_in"),
        "w_conv": upd(6, w_conv, m_w_conv, v_w_conv, "adamw_w_conv"),
        "w_branch_a": upd(1, w_branch_a, m_w_branch_a, v_w_branch_a, "adamw_w_branch_a"),
        "w_branch_b": upd(2, w_branch_b, m_w_branch_b, v_w_branch_b, "adamw_w_branch_b"),
        "w_out": upd(3, w_out, m_w_out, v_w_out, "adamw_w_out"),
        "w_ffn_in": upd(4, w_ffn_in, m_w_ffn_in, v_w_ffn_in, "adamw_w_ffn_in"),
        "w_ffn_out": upd(5, w_ffn_out, m_w_ffn_out, v_w_ffn_out, "adamw_w_ffn_out"),
    }
    for n, _ in _small_spec(depth):
        res[n] = [sm[i][0][n] for i in range(4)]
    order = ("w_ada", "b_ada", "norm1_g", "norm2_g", "w_in", "rel_table", "w_conv", "a_log", "dt_bias", "gdn_norm_g",
             "w_branch_a", "w_branch_b", "w_out", "w_ffn_in", "w_ffn_out", "final_g")
    return (loss, grad_x, *[res[n][0] for n in order], *[res[n][1] for n in order],
            *[res[n][2] for n in order], *[res[n][3] for n in order])
```

```python
import functools
import math

import jax
import jax.numpy as jnp
from jax import lax
from jax.experimental import pallas as pl
from jax.experimental.pallas import tpu as pltpu

F32 = jnp.float32
BF16 = jnp.bfloat16
HI = lax.Precision.HIGHEST

N_DEV = 8
D = 1024
DEPTH = 4
CH = 64
EPS = 1e-6
A_HEADS, A_DH = 8, 64
A_W = A_HEADS * A_DH
A_PAST = 8
A_MAX_REL = 128
QB = 256
KB = QB + A_PAST * CH
B_HEADS, B_DH = 4, 128
B_W = B_HEADS * B_DH
CONV_K = 4
FF = 2816
IN_DIM = 5640
IN_PAD = 5760
LANE = 128
NEG = -1e30
VMEM_LIMIT = 48 * 1024 * 1024

ADAM_LR, ADAM_B1, ADAM_B2, ADAM_EPS, ADAM_WD, ADAM_STEP = 0.001, 0.9, 0.999, 1e-08, 0.01, 10

OFF_GA, OFF_GB, OFF_QA, OFF_KA, OFF_VA, OFF_QB, OFF_KB, OFF_VB, OFF_ZB, OFF_BA = (
    0, 1024, 2048, 2560, 3072, 3584, 4096, 4608, 5120, 5632)


def _cp(sem=None):
    return pltpu.CompilerParams(dimension_semantics=sem, vmem_limit_bytes=VMEM_LIMIT)


def _tile(n, pref):
    if n <= pref:
        return n
    best = None
    for t in range(LANE, pref + 1, LANE):
        if n % t == 0:
            best = t
    assert best is not None, (n, pref)
    return best


def _sigmoid(x):
    return 1.0 / (1.0 + jnp.exp(-x))


def _silu(x):
    return x * _sigmoid(x)


def _dsilu(x):
    s = _sigmoid(x)
    return s * (1.0 + x * (1.0 - s))


def _dot(a, b, prec=None):
    return jnp.dot(a, b, preferred_element_type=F32, precision=prec)


def _dot_nt(a, b, prec=None):
    return lax.dot_general(a, b, (((1,), (1,)), ((), ())), preferred_element_type=F32, precision=prec)


def _dot_tn(a, b, prec=None):
    return lax.dot_general(a, b, (((0,), (0,)), ((), ())), preferred_element_type=F32, precision=prec)


def _mm(a, b, *, ta=False, tb=False, out_dtype=F32, name, tm=1024, tn=1024, tk=1024):
    m, k = (a.shape[1], a.shape[0]) if ta else a.shape
    n = b.shape[0] if tb else b.shape[1]
    assert k == (b.shape[1] if tb else b.shape[0]), (a.shape, b.shape, ta, tb)
    tm, tn, tk = _tile(m, tm), _tile(n, tn), _tile(k, tk)
    nk = k // tk
    dn = (((0 if ta else 1,), (1 if tb else 0,)), ((), ()))

    def body(a_ref, b_ref, o_ref, *acc):
        part = lax.dot_general(a_ref[...].astype(BF16), b_ref[...].astype(BF16), dn, preferred_element_type=F32)
        if nk == 1:
            o_ref[...] = part.astype(out_dtype)
            return
        acc_ref, kk = acc[0], pl.program_id(2)

        @pl.when(kk == 0)
        def _():
            acc_ref[...] = part

        @pl.when(kk > 0)
        def _():
            acc_ref[...] += part

        @pl.when(kk == nk - 1)
        def _():
            o_ref[...] = acc_ref[...].astype(out_dtype)

    a_spec = pl.BlockSpec((tk, tm), lambda i, j, q: (q, i)) if ta else pl.BlockSpec((tm, tk), lambda i, j, q: (i, q))
    b_spec = pl.BlockSpec((tn, tk), lambda i, j, q: (j, q)) if tb else pl.BlockSpec((tk, tn), lambda i, j, q: (q, j))
    return pl.pallas_call(
        body, grid=(m // tm, n // tn, nk), in_specs=[a_spec, b_spec],
        out_specs=pl.BlockSpec((tm, tn), lambda i, j, q: (i, j)),
        out_shape=jax.ShapeDtypeStruct((m, n), out_dtype),
        scratch_shapes=[pltpu.VMEM((tm, tn), F32)] if nk > 1 else [],
        compiler_params=_cp(("parallel", "parallel", "arbitrary")), name=name)(a, b)


def _rb(tr, width, cb=0):
    return pl.BlockSpec((tr, width), lambda i: (i, cb))


def _whole(shape):
    nd = len(shape)
    return pl.BlockSpec(shape, lambda i: (0,) * nd)


def _colsum(v):
    return jnp.sum(v, axis=0, keepdims=True)


def _adaln_fwd(x, g, sc, sh, t=None, gt=None, *, name, tr=256):
    tt = x.shape[0]
    res = t is not None

    def body(*refs):
        if res:
            x_ref, t_ref, gt_ref, g_ref, sc_ref, sh_ref, xo_ref, h_ref = refs
            xv = x_ref[...] + gt_ref[...] * t_ref[...]
            xo_ref[...] = xv
        else:
            x_ref, g_ref, sc_ref, sh_ref, h_ref = refs
            xv = x_ref[...]
        r = lax.rsqrt(jnp.mean(xv * xv, axis=-1, keepdims=True) + EPS)
        h_ref[...] = ((xv * r * g_ref[...]) * (1.0 + sc_ref[...]) + sh_ref[...]).astype(BF16)

    row, vec = _rb(tr, D), _whole((1, D))
    if res:
        ins, in_specs = (x, t, gt, g, sc, sh), [row, row, vec, vec, vec, vec]
        out_shape = (jax.ShapeDtypeStruct((tt, D), F32), jax.ShapeDtypeStruct((tt, D), BF16))
        out_specs = (row, row)
    else:
        ins, in_specs = (x, g, sc, sh), [row, vec, vec, vec]
        out_shape, out_specs = jax.ShapeDtypeStruct((tt, D), BF16), row
    out = pl.pallas_call(body, grid=(tt // tr,), in_specs=in_specs, out_specs=out_specs, out_shape=out_shape,
                         compiler_params=_cp(("parallel",)), name=name)(*ins)
    return out if res else (x, out)


def _adaln_bwd(x, g, sc, sh, dh, dx_in, t=None, gt=None, *, name, tr=256):
    tt = x.shape[0]
    res = t is not None

    def body(*refs):
        if res:
            x_ref, g_ref, sc_ref, sh_ref, dh_ref, dxi_ref, t_ref, gt_ref, dx_ref, dt_ref, st_ref = refs
        else:
            x_ref, g_ref, sc_ref, sh_ref, dh_ref, dxi_ref, dx_ref, st_ref = refs

        @pl.when(pl.program_id(0) == 0)
        def _():
            st_ref[...] = jnp.zeros_like(st_ref)

        xv, dh = x_ref[...], dh_ref[...]
        r = lax.rsqrt(jnp.mean(xv * xv, axis=-1, keepdims=True) + EPS)
        nrm = xv * r
        y = nrm * g_ref[...]
        dy = dh * (1.0 + sc_ref[...])
        dn = dy * g_ref[...]
        dx = dxi_ref[...] + r * (dn - nrm * jnp.mean(dn * nrm, axis=-1, keepdims=True))
        dx_ref[...] = dx
        st_ref[0:1, :] += _colsum(dy * nrm)
        st_ref[1:2, :] += _colsum(dh * y)
        st_ref[2:3, :] += _colsum(dh)
        if res:
            dt_ref[...] = (gt_ref[...] * dx).astype(BF16)
            st_ref[3:4, :] += _colsum(dx * t_ref[...])

    row, vec, st = _rb(tr, D), _whole((1, D)), _whole((8, D))
    ins, in_specs = [x, g, sc, sh, dh, dx_in], [row, vec, vec, vec, row, row]
    out_shape, out_specs = [jax.ShapeDtypeStruct((tt, D), F32)], [row]
    if res:
        ins += [t, gt]
        in_specs += [row, vec]
        out_shape.append(jax.ShapeDtypeStruct((tt, D), BF16))
        out_specs.append(row)
    out_shape.append(jax.ShapeDtypeStruct((8, D), F32))
    out_specs.append(st)
    return pl.pallas_call(body, grid=(tt // tr,), in_specs=in_specs, out_specs=tuple(out_specs),
                          out_shape=tuple(out_shape), compiler_params=_cp(("arbitrary",)), name=name)(*ins)


def _loss_head(x, t, gt, fg, tgt, *, name, tr=256):
    tt = x.shape[0]

    def body(x_ref, t_ref, gt_ref, fg_ref, tgt_ref, dx_ref, dt_ref, st_ref):
        @pl.when(pl.program_id(0) == 0)
        def _():
            st_ref[...] = jnp.zeros_like(st_ref)

        tv = t_ref[...]
        xv = x_ref[...] + gt_ref[...] * tv
        r = lax.rsqrt(jnp.mean(xv * xv, axis=-1, keepdims=True) + EPS)
        nrm = xv * r
        err = nrm * fg_ref[...] - tgt_ref[...]
        st_ref[4:5, :] += 0.5 * jnp.sum(jnp.mean(err * err, axis=-1, keepdims=True), axis=0, keepdims=True)
        dy = err * (1.0 / D)
        dn = dy * fg_ref[...]
        dx = r * (dn - nrm * jnp.mean(dn * nrm, axis=-1, keepdims=True))
        dx_ref[...] = dx
        dt_ref[...] = (gt_ref[...] * dx).astype(BF16)
        st_ref[0:1, :] += _colsum(dy * nrm)
        st_ref[3:4, :] += _colsum(dx * tv)

    row, vec = _rb(tr, D), _whole((1, D))
    return pl.pallas_call(
        body, grid=(tt // tr,), in_specs=[row, row, vec, vec, row], out_specs=(row, row, _whole((8, D))),
        out_shape=(jax.ShapeDtypeStruct((tt, D), F32), jax.ShapeDtypeStruct((tt, D), BF16),
                   jax.ShapeDtypeStruct((8, D), F32)),
        compiler_params=_cp(("arbitrary",)), name=name)(x, t, gt, fg, tgt)


def _merge_fwd(proj, pa, pb, *, name, tr=256):
    tt = pa.shape[0]

    def body(ga_ref, gb_ref, pa_ref, pb_ref, o_ref):
        o_ref[...] = (_sigmoid(ga_ref[...]) * pa_ref[...].astype(F32)
                      + _sigmoid(gb_ref[...]) * pb_ref[...].astype(F32)).astype(BF16)

    row = _rb(tr, D)
    return pl.pallas_call(body, grid=(tt // tr,), in_specs=[_rb(tr, D, 0), _rb(tr, D, 1), row, row], out_specs=row,
                          out_shape=jax.ShapeDtypeStruct((tt, D), BF16), compiler_params=_cp(("parallel",)),
                          name=name)(proj, proj, pa, pb)


def _merge_bwd(proj, pa, pb, dm, *, name, tr=256):
    tt = pa.shape[0]

    def body(ga_ref, gb_ref, pa_ref, pb_ref, dm_ref, dg_ref, dpa_ref, dpb_ref):
        dm_v = dm_ref[...]
        sa, sb = _sigmoid(ga_ref[...]), _sigmoid(gb_ref[...])
        dpa_ref[...] = (dm_v * sa).astype(BF16)
        dpb_ref[...] = (dm_v * sb).astype(BF16)
        dg_ref[:, 0:D] = (dm_v * pa_ref[...].astype(F32) * sa * (1.0 - sa)).astype(BF16)
        dg_ref[:, D:2 * D] = (dm_v * pb_ref[...].astype(F32) * sb * (1.0 - sb)).astype(BF16)

    row = _rb(tr, D)
    return pl.pallas_call(
        body, grid=(tt // tr,), in_specs=[_rb(tr, D, 0), _rb(tr, D, 1), row, row, row],
        out_specs=(_rb(tr, 2 * D), row, row),
        out_shape=(jax.ShapeDtypeStruct((tt, 2 * D), BF16), jax.ShapeDtypeStruct((tt, D), BF16),
                   jax.ShapeDtypeStruct((tt, D), BF16)),
        compiler_params=_cp(("parallel",)), name=name)(proj, proj, pa, pb, dm)


def _swiglu_fwd(gu, *, name, tr=256):
    tt = gu.shape[0]

    def body(g_ref, u_ref, o_ref):
        o_ref[...] = (_silu(g_ref[...].astype(F32)) * u_ref[...].astype(F32)).astype(BF16)

    return pl.pallas_call(body, grid=(tt // tr,), in_specs=[_rb(tr, FF, 0), _rb(tr, FF, 1)], out_specs=_rb(tr, FF),
                          out_shape=jax.ShapeDtypeStruct((tt, FF), BF16), compiler_params=_cp(("parallel",)),
                          name=name)(gu, gu)


def _swiglu_bwd(gu, dact, *, name, tr=256):
    tt = gu.shape[0]

    def body(g_ref, u_ref, da_ref, o_ref):
        gv, da = g_ref[...].astype(F32), da_ref[...].astype(F32)
        o_ref[:, 0:FF] = (da * u_ref[...].astype(F32) * _dsilu(gv)).astype(BF16)
        o_ref[:, FF:2 * FF] = (da * _silu(gv)).astype(BF16)

    return pl.pallas_call(body, grid=(tt // tr,), in_specs=[_rb(tr, FF, 0), _rb(tr, FF, 1), _rb(tr, FF)],
                          out_specs=_rb(tr, 2 * FF), out_shape=jax.ShapeDtypeStruct((tt, 2 * FF), BF16),
                          compiler_params=_cp(("parallel",)), name=name)(gu, gu, dact)


BIAS_LW = 1152


def _bias_diagonals(table):
    n_far = A_PAST * CH - A_MAX_REL + 1
    far = jnp.broadcast_to(table[:, 2 * A_MAX_REL:], (A_HEADS, n_far))
    mid = jnp.flip(table[:, 1:2 * A_MAX_REL], axis=1)
    near = jnp.broadcast_to(table[:, 0:1], (A_HEADS, KB - n_far - (2 * A_MAX_REL - 1)))
    pos = jnp.concatenate([far, mid, near], axis=1)
    neg = jnp.broadcast_to(table[:, 2 * A_MAX_REL:], (A_HEADS, QB - 1))
    gap = jnp.zeros((A_HEADS, BIAS_LW - KB - (QB - 1)), F32)
    return jnp.concatenate([pos, gap, neg], axis=1)


def _bias_fwd(diag, *, name):
    def body(w_ref, o_ref):
        qc = lax.broadcasted_iota(jnp.int32, (QB, KB), 0) // CH + A_PAST
        col = lax.broadcasted_iota(jnp.int32, (QB, KB), 1)
        inband = (col // CH <= qc) & (col // CH >= qc - A_PAST)
        for h in range(A_HEADS):
            rows = pltpu.roll(jnp.broadcast_to(w_ref[h:h + 1, :], (QB, BIAS_LW)), 0, 1, stride=1, stride_axis=0)
            for var in range(3):
                o_ref[var, h] = jnp.where(inband & (col >= A_PAST * CH - QB * var), rows[:, :KB], NEG)

    return pl.pallas_call(body, out_shape=jax.ShapeDtypeStruct((3, A_HEADS, QB, KB), F32), compiler_params=_cp(),
                          name=name)(diag)


def _bias_bwd(dbias, *, name):
    def body(d_ref, o_ref):
        r = lax.broadcasted_iota(jnp.int32, (QB, QB), 0)
        c = lax.broadcasted_iota(jnp.int32, (QB, QB), 1)
        flip = jnp.where(r + c == QB - 1, 1.0, 0.0).astype(F32)
        for h in range(A_HEADS):
            x = jnp.concatenate([_dot(flip, d_ref[h], HI), jnp.zeros((QB, BIAS_LW - KB), F32)], axis=1)
            o_ref[h:h + 1, :] = jnp.sum(pltpu.roll(x, 0, 1, stride=1, stride_axis=0), axis=0, keepdims=True)

    return pl.pallas_call(body, out_shape=jax.ShapeDtypeStruct((A_HEADS, BIAS_LW), F32), compiler_params=_cp(),
                          name=name)(dbias)


def _kv_pad(proj, *, name, tr=256):
    tt = proj.shape[0]
    npad = A_PAST * CH // tr

    def body(k_ref, v_ref, ko_ref, vo_ref):
        i = pl.program_id(0)

        @pl.when(i < npad)
        def _():
            ko_ref[...] = jnp.zeros_like(ko_ref)
            vo_ref[...] = jnp.zeros_like(vo_ref)

        @pl.when(i >= npad)
        def _():
            ko_ref[...] = k_ref[...].astype(BF16)
            vo_ref[...] = v_ref[...].astype(BF16)

    src = lambda off: pl.BlockSpec((tr, A_W), lambda i: (jnp.maximum(i - npad, 0), off // A_W))
    out = jax.ShapeDtypeStruct((tt + A_PAST * CH, A_W), BF16)
    return pl.pallas_call(body, grid=(tt // tr + npad,), in_specs=[src(OFF_KA), src(OFF_VA)],
                          out_specs=(_rb(tr, A_W), _rb(tr, A_W)), out_shape=(out, out),
                          compiler_params=_cp(("parallel",)), name=name)(proj, proj)


def _attn_fwd(proj, kpad, vpad, bias, *, name):
    tt = proj.shape[0]

    def body(q_ref, k_ref, v_ref, b_ref, o_ref, l_ref):
        q0 = pl.multiple_of(pl.program_id(1) * QB, QB)
        q = q_ref[...] * (A_DH ** -0.5)
        k = k_ref[pl.ds(q0, KB), :]
        v = v_ref[pl.ds(q0, KB), :]
        lane = lax.broadcasted_iota(jnp.int32, (QB, LANE), 1)
        o = jnp.zeros((QB, LANE), F32)
        lse = jnp.zeros((QB, LANE), F32)
        for a in range(2):
            hm = (lane >= A_DH * a) & (lane < A_DH * (a + 1))
            s = _dot_nt(jnp.where(hm, q, 0.0).astype(BF16), k) + b_ref[a]
            m = jnp.max(s, axis=-1, keepdims=True)
            p = jnp.exp(s - m)
            l = jnp.sum(p, axis=-1, keepdims=True)
            o = jnp.where(hm, _dot(p.astype(BF16), v) / l, o)
            lse = jnp.where(hm, m + jnp.log(l), lse)
        o_ref[...] = o.astype(BF16)
        l_ref[...] = lse

    kv = pl.BlockSpec((tt + A_PAST * CH, LANE), lambda h, i: (0, h))
    blk = pl.BlockSpec((QB, LANE), lambda h, i: (i, h))
    return pl.pallas_call(
        body, grid=(A_W // LANE, tt // QB),
        in_specs=[pl.BlockSpec((QB, LANE), lambda h, i: (i, OFF_QA // LANE + h)), kv, kv,
                  pl.BlockSpec((None, 2, QB, KB), lambda h, i: (jnp.minimum(i, 2), h, 0, 0))],
        out_specs=(blk, blk),
        out_shape=(jax.ShapeDtypeStruct((tt, A_W), BF16), jax.ShapeDtypeStruct((tt, A_W), F32)),
        compiler_params=_cp(("parallel", "parallel")), name=name)(proj, kpad, vpad, bias)


def _attn_bwd(proj, kpad, vpad, bias, o, lse, do, *, name):
    tt = proj.shape[0]
    nq = tt // QB

    def body(q_ref, k_ref, v_ref, b_ref, o_ref, l_ref, do_ref, dq_ref, dko_ref, dvo_ref, db_ref, dk_ref, dv_ref):
        @pl.when(pl.program_id(1) == 0)
        def _():
            dk_ref[...] = jnp.zeros_like(dk_ref)
            dv_ref[...] = jnp.zeros_like(dv_ref)
            db_ref[...] = jnp.zeros_like(db_ref)

        q0 = pl.multiple_of(pl.program_id(1) * QB, QB)
        q, do_v, lse = q_ref[...] * (A_DH ** -0.5), do_ref[...], l_ref[...]
        k = k_ref[pl.ds(q0, KB), :]
        v = v_ref[pl.ds(q0, KB), :]
        dsum = do_v * o_ref[...].astype(F32)
        lane = lax.broadcasted_iota(jnp.int32, (QB, LANE), 1)
        dq = jnp.zeros((QB, LANE), F32)
        dk = jnp.zeros((KB, LANE), F32)
        dv = jnp.zeros((KB, LANE), F32)
        for a in range(2):
            hm = (lane >= A_DH * a) & (lane < A_DH * (a + 1))
            qa = jnp.where(hm, q, 0.0).astype(BF16)
            doa = jnp.where(hm, do_v, 0.0).astype(BF16)
            s = _dot_nt(qa, k) + b_ref[a]
            lse_a = jnp.max(jnp.where(hm, lse, NEG), axis=-1, keepdims=True)
            p = jnp.exp(s - lse_a)
            dp = _dot_nt(doa, v)
            dsum_a = jnp.sum(jnp.where(hm, dsum, 0.0), axis=-1, keepdims=True)
            ds = p * (dp - dsum_a)
            db_ref[a] += ds
            dsb = ds.astype(BF16)
            dq = jnp.where(hm, _dot(dsb, k) * (A_DH ** -0.5), dq)
            dk += _dot_tn(dsb, qa)
            dv += _dot_tn(p.astype(BF16), doa)
        dq_ref[...] = dq.astype(BF16)
        dk_ref[pl.ds(q0, KB), :] += dk
        dv_ref[pl.ds(q0, KB), :] += dv

        @pl.when(pl.program_id(1) == nq - 1)
        def _():
            dko_ref[...] = dk_ref[A_PAST * CH:, :].astype(BF16)
            dvo_ref[...] = dv_ref[A_PAST * CH:, :].astype(BF16)

    kv = pl.BlockSpec((tt + A_PAST * CH, LANE), lambda h, i: (0, h))
    blk = pl.BlockSpec((QB, LANE), lambda h, i: (i, h))
    col = pl.BlockSpec((tt, LANE), lambda h, i: (0, h))
    bsp = pl.BlockSpec((2, QB, KB), lambda h, i: (h, 0, 0))
    bias_in = pl.BlockSpec((None, 2, QB, KB), lambda h, i: (jnp.minimum(i, 2), h, 0, 0))
    out = jax.ShapeDtypeStruct((tt, A_W), BF16)
    return pl.pallas_call(
        body, grid=(A_W // LANE, nq),
        in_specs=[pl.BlockSpec((QB, LANE), lambda h, i: (i, OFF_QA // LANE + h)), kv, kv, bias_in, blk, blk, blk],
        out_specs=(blk, col, col, bsp),
        out_shape=(out, out, out, jax.ShapeDtypeStruct((A_HEADS, QB, KB), F32)),
        scratch_shapes=[pltpu.VMEM((tt + A_PAST * CH, LANE), F32), pltpu.VMEM((tt + A_PAST * CH, LANE), F32)],
        compiler_params=_cp(("parallel", "arbitrary")), name=name)(proj, kpad, vpad, bias, o, lse, do)


GTR = 256


def _taps(w_ref, grp):
    return [w_ref[j:j + 1, grp * B_W:(grp + 1) * B_W] for j in range(CONV_K)]


def _shifts(xe, rows):
    return [xe[8:8 + rows]] + [pltpu.roll(xe, s, 0)[8:8 + rows] for s in range(1, CONV_K)]


def _conv(shifts, taps):
    acc = taps[CONV_K - 1] * shifts[0]
    for s in range(1, CONV_K):
        acc = acc + taps[CONV_K - 1 - s] * shifts[s]
    return acc


def _qk_scale(grp):
    return B_DH ** -0.5 if grp == 0 else 1.0


def _act_fwd(c, grp):
    y = _silu(c)
    if grp == 2:
        return y
    parts = []
    for hd in range(B_HEADS):
        yh = y[:, hd * B_DH:(hd + 1) * B_DH]
        parts.append(yh * (lax.rsqrt(jnp.sum(yh * yh, axis=-1, keepdims=True) + EPS) * _qk_scale(grp)))
    return jnp.concatenate(parts, axis=1)


def _act_bwd(c, dy, grp):
    if grp == 2:
        return dy * _dsilu(c)
    y = _silu(c)
    parts = []
    for hd in range(B_HEADS):
        yh = y[:, hd * B_DH:(hd + 1) * B_DH]
        r = lax.rsqrt(jnp.sum(yh * yh, axis=-1, keepdims=True) + EPS)
        dyh = dy[:, hd * B_DH:(hd + 1) * B_DH] * _qk_scale(grp)
        parts.append(r * dyh - yh * (r * r * r) * jnp.sum(dyh * yh, axis=-1, keepdims=True))
    return jnp.concatenate(parts, axis=1) * _dsilu(c)


def _chunk_tri(n, upper=False):
    r = lax.broadcasted_iota(jnp.int32, (n, n), 0)
    c = lax.broadcasted_iota(jnp.int32, (n, n), 1)
    same = (r // CH) == (c // CH)
    return jnp.where(same & ((r <= c) if upper else (r >= c)), 1.0, 0.0).astype(F32)


def _gate_rows(ba, par_ref):
    lane = lax.broadcasted_iota(jnp.int32, ba.shape, 1)
    z = ba + par_ref[1:2, :]
    sp = jnp.maximum(z, 0.0) + jnp.log(1.0 + jnp.exp(-jnp.abs(z)))
    g = -jnp.exp(par_ref[0:1, :]) * sp
    return jnp.where(lane < B_HEADS, _sigmoid(ba), jnp.where(lane < 2 * B_HEADS, g, 0.0)), z


def _prev8(cb):
    return pl.BlockSpec((8, B_W), lambda i: (jnp.maximum(i * (GTR // 8) - 1, 0), cb))


def _next8(cb, nb):
    return pl.BlockSpec((8, B_W), lambda i: (jnp.minimum((i + 1) * (GTR // 8), nb * (GTR // 8) - 1), cb))


def _gdn_pre_fwd(proj, wconv, par, *, name):
    tt = proj.shape[0]

    def body(q_ref, k_ref, v_ref, qh_ref, kh_ref, vh_ref, ba_ref, w_ref, par_ref, qo_ref, ko_ref, vo_ref, aux_ref):
        first = pl.program_id(0) == 0
        for grp, (x_ref, h_ref, o_ref) in enumerate(((q_ref, qh_ref, qo_ref), (k_ref, kh_ref, ko_ref),
                                                     (v_ref, vh_ref, vo_ref))):
            xe = jnp.concatenate([jnp.where(first, 0.0, h_ref[...]), x_ref[...]], axis=0)
            o_ref[...] = _act_fwd(_conv(_shifts(xe, GTR), _taps(w_ref, grp)), grp)
        bg, _ = _gate_rows(ba_ref[...], par_ref)
        lane = lax.broadcasted_iota(jnp.int32, bg.shape, 1)
        aux_ref[...] = jnp.where(lane < B_HEADS, bg, _dot(_chunk_tri(GTR), bg, HI))

    col = lambda off: _rb(GTR, B_W, off // B_W)
    outs = jax.ShapeDtypeStruct((tt, B_W), F32)
    return pl.pallas_call(
        body, grid=(tt // GTR,),
        in_specs=[col(OFF_QB), col(OFF_KB), col(OFF_VB), _prev8(OFF_QB // B_W), _prev8(OFF_KB // B_W),
                  _prev8(OFF_VB // B_W), _rb(GTR, LANE, OFF_BA // LANE), _whole((CONV_K, 3 * B_W)),
                  _whole((8, LANE))],
        out_specs=(_rb(GTR, B_W), _rb(GTR, B_W), _rb(GTR, B_W), _rb(GTR, LANE)),
        out_shape=(outs, outs, outs, jax.ShapeDtypeStruct((tt, LANE), F32)),
        compiler_params=_cp(("parallel",)), name=name)(proj, proj, proj, proj, proj, proj, proj, wconv, par)


def _gdn_pre_bwd(proj, wconv, par, dq, dk, dv, daux, *, name):
    tt = proj.shape[0]
    nb = tt // GTR

    def body(q_ref, k_ref, v_ref, qh_ref, kh_ref, vh_ref, qn_ref, kn_ref, vn_ref, ba_ref, w_ref, par_ref,
             dq_ref, dk_ref, dv_ref, dqn_ref, dkn_ref, dvn_ref, daux_ref, dx_ref, dba_ref, dw_ref, dpar_ref):
        i = pl.program_id(0)
        first, last = i == 0, i == nb - 1

        @pl.when(first)
        def _():
            dw_ref[...] = jnp.zeros_like(dw_ref)
            dpar_ref[...] = jnp.zeros_like(dpar_ref)

        groups = ((q_ref, qh_ref, qn_ref, dq_ref, dqn_ref), (k_ref, kh_ref, kn_ref, dk_ref, dkn_ref),
                  (v_ref, vh_ref, vn_ref, dv_ref, dvn_ref))
        for grp, (x_ref, h_ref, xn_ref, d_ref, dn_ref) in enumerate(groups):
            taps = _taps(w_ref, grp)
            xe = jnp.concatenate([jnp.where(first, 0.0, h_ref[...]), x_ref[...]], axis=0)
            sh = _shifts(xe, GTR)
            dc = _act_bwd(_conv(sh, taps), d_ref[...], grp)
            xe_n = jnp.concatenate([x_ref[GTR - 8:GTR, :], xn_ref[...]], axis=0)
            dcn = _act_bwd(_conv(_shifts(xe_n, 8), taps), dn_ref[...], grp)
            dce = jnp.concatenate([dc, jnp.where(last, 0.0, dcn)], axis=0)
            dx = taps[CONV_K - 1] * dc
            dw_ref[CONV_K - 1:CONV_K, grp * B_W:(grp + 1) * B_W] += _colsum(dc * sh[0])
            for s in range(1, CONV_K):
                dx = dx + taps[CONV_K - 1 - s] * pltpu.roll(dce, GTR + 8 - s, 0)[0:GTR]
                dw_ref[CONV_K - 1 - s:CONV_K - s, grp * B_W:(grp + 1) * B_W] += _colsum(dc * sh[s])
            dx_ref[:, grp * B_W:(grp + 1) * B_W] = dx.astype(BF16)
        ba = ba_ref[...]
        lane = lax.broadcasted_iota(jnp.int32, ba.shape, 1)
        bg, z = _gate_rows(ba, par_ref)
        daux_v = daux_ref[...]
        dg = _dot(_chunk_tri(GTR, upper=True), daux_v, HI)
        dgl = jnp.where((lane >= B_HEADS) & (lane < 2 * B_HEADS), dg, 0.0)
        da = dgl * (-jnp.exp(par_ref[0:1, :])) * _sigmoid(z)
        dbr = jnp.where(lane < B_HEADS, daux_v * bg * (1.0 - bg), 0.0)
        dba_ref[...] = (dbr + da).astype(BF16)
        dpar_ref[0:1, :] += _colsum(dgl * bg)
        dpar_ref[1:2, :] += _colsum(da)

    col = lambda off: _rb(GTR, B_W, off // B_W)
    row, rowl = _rb(GTR, B_W), _rb(GTR, LANE)
    return pl.pallas_call(
        body, grid=(nb,),
        in_specs=[col(OFF_QB), col(OFF_KB), col(OFF_VB),
                  _prev8(OFF_QB // B_W), _prev8(OFF_KB // B_W), _prev8(OFF_VB // B_W),
                  _next8(OFF_QB // B_W, nb), _next8(OFF_KB // B_W, nb), _next8(OFF_VB // B_W, nb),
                  _rb(GTR, LANE, OFF_BA // LANE), _whole((CONV_K, 3 * B_W)), _whole((8, LANE)),
                  row, row, row, _next8(0, nb), _next8(0, nb), _next8(0, nb), rowl],
        out_specs=(_rb(GTR, 3 * B_W), rowl, _whole((8, 3 * B_W)), _whole((8, LANE))),
        out_shape=(jax.ShapeDtypeStruct((tt, 3 * B_W), BF16), jax.ShapeDtypeStruct((tt, LANE), BF16),
                   jax.ShapeDtypeStruct((8, 3 * B_W), F32), jax.ShapeDtypeStruct((8, LANE), F32)),
        compiler_params=_cp(("arbitrary",)), name=name)(
            proj, proj, proj, proj, proj, proj, proj, proj, proj, proj, wconv, par, dq, dk, dv, dq, dk, dv, daux)


def _col(x, j):
    lane = lax.broadcasted_iota(jnp.int32, x.shape, 1)
    return jnp.sum(jnp.where(lane == j, x, 0.0), axis=-1, keepdims=True)


def _split(x):
    hi = x.astype(BF16)
    return hi, (x - hi.astype(F32)).astype(BF16)


def _dot3(a, b, tn=False):
    dot = _dot_tn if tn else _dot
    (ah, al), (bh, bl) = _split(a), _split(b)
    return dot(ah, bh) + (dot(ah, bl) + dot(al, bh))


def _chunk_masks():
    r = lax.broadcasted_iota(jnp.int32, (CH, CH), 0)
    c = lax.broadcasted_iota(jnp.int32, (CH, CH), 1)
    return r > c, r >= c


def _gc_rows(aux, nc):
    t = jnp.transpose(aux[:, B_HEADS:2 * B_HEADS].reshape(nc, CH, B_HEADS), (0, 2, 1))
    return jnp.concatenate([t, jnp.zeros_like(t)], axis=1).reshape(nc * 8, CH)


_CHUNK8 = lambda width, n=1: pl.BlockSpec((8 * n, width), lambda i: (i, 0))
_CHUNK4 = lambda a, b, n=1: pl.BlockSpec((B_HEADS * n, a, b), lambda i: (i, 0, 0))
NCH = 2


def _per_chunk(body, rows):
    def wrapped(*refs):
        for ci in range(NCH):
            body(*[r.at[pl.ds(ci * n, n)] for r, n in zip(refs, rows)])
    return wrapped


def _gdn_lower(k, aux, auxt, *, name):
    tt = k.shape[0]

    def body(k_ref, aux_ref, auxt_ref, l_ref):
        aux_v = aux_ref[...]
        strict, _ = _chunk_masks()
        for hd in range(B_HEADS):
            kh = k_ref[:, hd * B_DH:(hd + 1) * B_DH].astype(BF16)
            diff = _col(aux_v, B_HEADS + hd) - auxt_ref[hd:hd + 1, :]
            dec = jnp.exp(jnp.where(strict, diff, NEG))
            l_ref[hd] = _col(aux_v, hd) * _dot_nt(kh, kh) * dec

    return pl.pallas_call(
        _per_chunk(body, (CH, CH, 8, B_HEADS)), grid=(tt // CH // NCH,),
        in_specs=[_rb(NCH * CH, B_W), _rb(NCH * CH, LANE), _CHUNK8(CH, NCH)],
        out_specs=_CHUNK4(CH, CH, NCH),
        out_shape=jax.ShapeDtypeStruct((tt // CH * B_HEADS, CH, CH), F32),
        compiler_params=_cp(("parallel",)), name=name)(k, aux, auxt)


def _tri_inverse(lt, *, name):
    nb = lt.shape[2]

    def body(l_ref, t_ref):
        rowid = lax.broadcasted_iota(jnp.int32, (CH, nb), 0)

        def outer(i, carry):
            def inner(j, acc):
                return acc + l_ref[i, pl.ds(j, 1), :] * t_ref[j]

            acc = lax.fori_loop(0, i, inner, jnp.zeros((CH, nb), F32))
            t_ref[i] = jnp.where(rowid == i, 1.0, 0.0) - acc
            return carry

        lax.fori_loop(0, CH, outer, 0)

    return pl.pallas_call(body, out_shape=jax.ShapeDtypeStruct(lt.shape, F32),
                          in_specs=[pl.BlockSpec(memory_space=pltpu.VMEM)],
                          out_specs=pl.BlockSpec(memory_space=pltpu.VMEM),
                          compiler_params=_cp(), name=name)(lt)


def _gdn_gates(aux_v, aux_last, auxt_ref, hd):
    _, incl = _chunk_masks()
    beta = _col(aux_v, hd)
    gc = _col(aux_v, B_HEADS + hd)
    gl = _col(aux_last, B_HEADS + hd)
    dec = jnp.exp(jnp.where(incl, gc - auxt_ref[hd:hd + 1, :], NEG))
    return beta, gc, gl, jnp.exp(gc), dec


def _gdn_intra(q, k, v, aux, auxt, tinv, *, name):
    tt = q.shape[0]
    nc = tt // CH

    def body(q_ref, k_ref, v_ref, aux_ref, auxt_ref, t_ref, u0_ref, w_ref, qd_ref, kd_ref, qk_ref, gle_ref):
        aux_v = aux_ref[...]
        aux_last = aux_ref[CH - 1:CH, :]
        lane8 = lax.broadcasted_iota(jnp.int32, (8, LANE), 1)
        gle = jnp.zeros((8, LANE), F32)
        heads = range(B_HEADS)
        sls = [slice(hd * B_DH, (hd + 1) * B_DH) for hd in heads]
        gates = [_gdn_gates(aux_v, aux_last, auxt_ref, hd) for hd in heads]
        qk0 = [_dot_nt(q_ref[:, sls[hd]].astype(BF16), k_ref[:, sls[hd]].astype(BF16)) for hd in heads]
        u0 = [_dot3(t_ref[hd], v_ref[:, sls[hd]] * gates[hd][0]) for hd in heads]
        wk = [_dot3(t_ref[hd], k_ref[:, sls[hd]] * (gates[hd][0] * gates[hd][3])) for hd in heads]
        for hd in heads:
            sl = sls[hd]
            beta, gc, gl, egc, dec = gates[hd]
            qk_ref[hd] = (qk0[hd] * dec).astype(BF16)
            u0_ref[:, sl] = u0[hd]
            w_ref[:, sl] = wk[hd].astype(BF16)
            qd_ref[:, sl] = (q_ref[:, sl] * egc).astype(BF16)
            kd_ref[:, sl] = (k_ref[:, sl] * jnp.exp(gl - gc)).astype(BF16)
            gle = gle + jnp.where(lane8 == hd, jnp.exp(gl), 0.0)
        gle_ref[...] = gle

    row = _rb(NCH * CH, B_W)
    half = jax.ShapeDtypeStruct((tt, B_W), BF16)
    return pl.pallas_call(
        _per_chunk(body, (CH, CH, CH, CH, 8, B_HEADS, CH, CH, CH, CH, B_HEADS, 8)), grid=(nc // NCH,),
        in_specs=[row, row, row, _rb(NCH * CH, LANE), _CHUNK8(CH, NCH), _CHUNK4(CH, CH, NCH)],
        out_specs=(row, row, row, row, _CHUNK4(CH, CH, NCH), _CHUNK8(LANE, NCH)),
        out_shape=(jax.ShapeDtypeStruct((tt, B_W), F32), half, half, half,
                   jax.ShapeDtypeStruct((nc * B_HEADS, CH, CH), BF16), jax.ShapeDtypeStruct((nc * 8, LANE), F32)),
        compiler_params=_cp(("parallel",)), name=name)(q, k, v, aux, auxt, tinv)


def _gdn_scan_fwd(u0, w, qd, kd, qk, gle, *, name):
    tt = u0.shape[0]
    nc = tt // CH

    def body(u0_ref, w_ref, qd_ref, kd_ref, qk_ref, gle_ref, o_ref, ss_ref, u_ref, s_ref):
        @pl.when(pl.program_id(0) == 0)
        def _():
            s_ref[...] = jnp.zeros_like(s_ref)

        gle = gle_ref[0:1, :]
        heads = range(B_HEADS)
        sls = [slice(hd * B_DH, (hd + 1) * B_DH) for hd in heads]
        st = [s_ref[hd] for hd in heads]
        sb = [t.astype(BF16) for t in st]
        ws = [_dot(w_ref[:, sls[hd]], sb[hd]) for hd in heads]
        qs = [_dot(qd_ref[:, sls[hd]], sb[hd]) for hd in heads]
        ub = [(u0_ref[:, sls[hd]] - ws[hd]).astype(BF16) for hd in heads]
        ku = [_dot_tn(kd_ref[:, sls[hd]], ub[hd]) for hd in heads]
        qu = [_dot(qk_ref[hd], ub[hd]) for hd in heads]
        for hd in heads:
            ss_ref[hd] = st[hd]
            u_ref[:, sls[hd]] = ub[hd]
            o_ref[:, sls[hd]] = qs[hd] + qu[hd]
            s_ref[hd] = st[hd] * _col(gle, hd) + ku[hd]

    row = _rb(CH, B_W)
    return pl.pallas_call(
        body, grid=(nc,), in_specs=[row, row, row, row, _CHUNK4(CH, CH), _CHUNK8(LANE)],
        out_specs=(row, _CHUNK4(B_DH, B_DH), row),
        out_shape=(jax.ShapeDtypeStruct((tt, B_W), F32), jax.ShapeDtypeStruct((nc * B_HEADS, B_DH, B_DH), F32),
                   jax.ShapeDtypeStruct((tt, B_W), BF16)),
        scratch_shapes=[pltpu.VMEM((B_HEADS, B_DH, B_DH), F32)],
        compiler_params=_cp(("arbitrary",)), name=name)(u0, w, qd, kd, qk, gle)


def _gdn_scan_bwd(w, qd, kd, qk, gle, do, *, name):
    tt = w.shape[0]
    nc = tt // CH

    def body(w_ref, qd_ref, kd_ref, qk_ref, gle_ref, do_ref, du_ref, dss_ref, ds_ref):
        @pl.when(pl.program_id(0) == 0)
        def _():
            ds_ref[...] = jnp.zeros_like(ds_ref)

        gle = gle_ref[0:1, :]
        heads = range(B_HEADS)
        sls = [slice(hd * B_DH, (hd + 1) * B_DH) for hd in heads]
        dst = [ds_ref[hd] for hd in heads]
        dob = [do_ref[:, sls[hd]].astype(BF16) for hd in heads]
        kds = [_dot(kd_ref[:, sls[hd]], dst[hd].astype(BF16)) for hd in heads]
        qkd = [_dot_tn(qk_ref[hd], dob[hd]) for hd in heads]
        qdd = [_dot_tn(qd_ref[:, sls[hd]], dob[hd]) for hd in heads]
        du = [qkd[hd] + kds[hd] for hd in heads]
        wdu = [_dot_tn(w_ref[:, sls[hd]], du[hd].astype(BF16)) for hd in heads]
        for hd in heads:
            dss_ref[hd] = dst[hd]
            du_ref[:, sls[hd]] = du[hd]
            ds_ref[hd] = qdd[hd] + _col(gle, hd) * dst[hd] - wdu[hd]

    rev = lambda width: pl.BlockSpec((CH, width), lambda i: (nc - 1 - i, 0))
    rev4 = lambda a, b: pl.BlockSpec((B_HEADS, a, b), lambda i: (nc - 1 - i, 0, 0))
    return pl.pallas_call(
        body, grid=(nc,),
        in_specs=[rev(B_W), rev(B_W), rev(B_W), rev4(CH, CH), pl.BlockSpec((8, LANE), lambda i: (nc - 1 - i, 0)), rev(B_W)],
        out_specs=(rev(B_W), rev4(B_DH, B_DH)),
        out_shape=(jax.ShapeDtypeStruct((tt, B_W), F32), jax.ShapeDtypeStruct((nc * B_HEADS, B_DH, B_DH), F32)),
        scratch_shapes=[pltpu.VMEM((B_HEADS, B_DH, B_DH), F32)],
        compiler_params=_cp(("arbitrary",)), name=name)(w, qd, kd, qk, gle, do)


def _gdn_bwd(q, k, v, aux, auxt, tinv, u0, w, u, ss, dss, du, do, *, name):
    tt = q.shape[0]
    nc = tt // CH

    def body(q_ref, k_ref, v_ref, aux_ref, auxt_ref, t_ref, u0_ref, w_ref, u_ref, ss_ref, dss_ref, du_ref, do_ref,
             dq_ref, dk_ref, dv_ref, daux_ref):
        aux_v = aux_ref[...]
        aux_last = aux_ref[CH - 1:CH, :]
        lane = lax.broadcasted_iota(jnp.int32, (CH, LANE), 1)
        rowi = lax.broadcasted_iota(jnp.int32, (CH, 1), 0)
        strict, incl = _chunk_masks()
        daux = jnp.zeros((CH, LANE), F32)
        heads = range(B_HEADS)
        sls = [slice(hd * B_DH, (hd + 1) * B_DH) for hd in heads]
        gates = [_gdn_gates(aux_v, aux_last, auxt_ref, hd) for hd in heads]
        kbs = [k_ref[:, sl].astype(BF16) for sl in sls]
        qbs = [q_ref[:, sl].astype(BF16) for sl in sls]
        sbs = [ss_ref[hd].astype(BF16) for hd in heads]
        dsbs = [dss_ref[hd].astype(BF16) for hd in heads]
        dobs = [do_ref[:, sl].astype(BF16) for sl in sls]
        kks = [_dot_nt(kbs[hd], kbs[hd]) for hd in heads]
        qk0s = [_dot_nt(qbs[hd], kbs[hd]) for hd in heads]
        dq_decs = [_dot_nt(dobs[hd], sbs[hd]) for hd in heads]
        dqks = [_dot_nt(dobs[hd], u_ref[:, sls[hd]]) for hd in heads]
        dk_decs = [_dot_nt(u_ref[:, sls[hd]], dsbs[hd]) for hd in heads]
        dws = [-_dot_nt(du_ref[:, sls[hd]].astype(BF16), sbs[hd]) for hd in heads]
        drvs = [_dot3(t_ref[hd], du_ref[:, sls[hd]], tn=True) for hd in heads]
        drks = [_dot3(t_ref[hd], dws[hd], tn=True) for hd in heads]
        dls = [-(_dot_nt(drvs[hd].astype(BF16), u0_ref[:, sls[hd]].astype(BF16))
                 + _dot_nt(drks[hd].astype(BF16), w_ref[:, sls[hd]])) for hd in heads]
        for hd in heads:
            sl = sls[hd]
            qh, kh, vh = q_ref[:, sl], k_ref[:, sl], v_ref[:, sl]
            beta, gc, gl, egc, dec = gates[hd]
            ekd, eg_last = jnp.exp(gl - gc), jnp.exp(gl)
            kb, qb, kk, qk0 = kbs[hd], qbs[hd], kks[hd], qk0s[hd]
            st, dst = ss_ref[hd], dss_ref[hd]
            dq_dec, dk_dec = dq_decs[hd], dk_decs[hd]
            dqk = jnp.where(incl, dqks[hd], 0.0)
            dgl = jnp.sum(jnp.sum(st * dst, axis=-1, keepdims=True), axis=0, keepdims=True) * eg_last
            drv, drk = drvs[hd], drks[hd]
            dl = jnp.where(strict, dls[hd], 0.0)
            dv_ref[:, sl] = drv * beta
            rk = jnp.sum(drk * kh, axis=-1, keepdims=True)
            dbeta = jnp.sum(drv * vh, axis=-1, keepdims=True) + rk * egc
            dgc = rk * beta * egc
            dk = drk * (beta * egc)
            ldec = dl * dec
            dbeta = dbeta + jnp.sum(ldec * kk, axis=-1, keepdims=True)
            dkk = (ldec * beta).astype(BF16)
            dqk0 = (dqk * dec).astype(BF16)
            ddec = ldec * beta * kk + dqk * (qk0 * dec)
            dq = _dot(dqk0, kb) + dq_dec * egc
            dk = dk + _dot_tn(dqk0, qb) + _dot(dkk, kb) + _dot_tn(dkk, kb) + dk_dec * ekd
            dgc = dgc + jnp.sum(ddec, axis=-1, keepdims=True) - _col_from_rowsum(ddec)
            dgc = dgc + jnp.sum(dq_dec * qh, axis=-1, keepdims=True) * egc
            kd = jnp.sum(dk_dec * kh, axis=-1, keepdims=True) * ekd
            dgc = dgc - kd
            dgc = dgc + jnp.where(rowi == CH - 1, jnp.sum(kd, axis=0, keepdims=True) + dgl, 0.0)
            dq_ref[:, sl] = dq
            dk_ref[:, sl] = dk
            daux = daux + jnp.where(lane == hd, dbeta, 0.0) + jnp.where(lane == B_HEADS + hd, dgc, 0.0)
        daux_ref[...] = daux

    row = _rb(NCH * CH, B_W)
    outs = jax.ShapeDtypeStruct((tt, B_W), F32)
    return pl.pallas_call(
        _per_chunk(body, (CH, CH, CH, CH, 8, B_HEADS, CH, CH, CH, B_HEADS, B_HEADS, CH, CH, CH, CH, CH, CH)),
        grid=(nc // NCH,),
        in_specs=[row, row, row, _rb(NCH * CH, LANE), _CHUNK8(CH, NCH), _CHUNK4(CH, CH, NCH), row, row, row,
                  _CHUNK4(B_DH, B_DH, NCH), _CHUNK4(B_DH, B_DH, NCH), row, row],
        out_specs=(row, row, row, _rb(NCH * CH, LANE)),
        out_shape=(outs, outs, outs, jax.ShapeDtypeStruct((tt, LANE), F32)),
        compiler_params=_cp(("parallel",)), name=name)(q, k, v, aux, auxt, tinv, u0, w, u, ss, dss, du, do)


def _col_from_rowsum(m):
    hi, lo = _split(m)
    ones = jnp.ones((CH, LANE), BF16)
    return (_dot_tn(hi, ones) + _dot_tn(lo, ones))[:, 0:1]


def _gdn_post_fwd(o, proj, gn, *, name, tr=256):
    tt = o.shape[0]

    def body(o_ref, z_ref, g_ref, y_ref):
        for hd in range(B_HEADS):
            sl = slice(hd * B_DH, (hd + 1) * B_DH)
            oh = o_ref[:, sl]
            r = lax.rsqrt(jnp.mean(oh * oh, axis=-1, keepdims=True) + EPS)
            y_ref[:, sl] = (oh * r * g_ref[...] * _silu(z_ref[:, sl])).astype(BF16)

    return pl.pallas_call(body, grid=(tt // tr,), in_specs=[_rb(tr, B_W), _rb(tr, B_W, OFF_ZB // B_W), _whole((1, B_DH))],
                          out_specs=_rb(tr, B_W), out_shape=jax.ShapeDtypeStruct((tt, B_W), BF16),
                          compiler_params=_cp(("parallel",)), name=name)(o, proj, gn)


def _gdn_post_bwd(o, proj, gn, dy, *, name, tr=256):
    tt = o.shape[0]

    def body(o_ref, z_ref, g_ref, dy_ref, do_ref, dz_ref, dg_ref):
        @pl.when(pl.program_id(0) == 0)
        def _():
            dg_ref[...] = jnp.zeros_like(dg_ref)

        g = g_ref[...]
        for hd in range(B_HEADS):
            sl = slice(hd * B_DH, (hd + 1) * B_DH)
            oh, zh, dyh = o_ref[:, sl], z_ref[:, sl], dy_ref[:, sl]
            r = lax.rsqrt(jnp.mean(oh * oh, axis=-1, keepdims=True) + EPS)
            a = oh * r
            s = _silu(zh)
            da = dyh * g * s
            dg_ref[0:1, :] += _colsum(dyh * a * s)
            dz_ref[:, sl] = (dyh * a * g * _dsilu(zh)).astype(BF16)
            do_ref[:, sl] = r * (da - a * jnp.mean(da * a, axis=-1, keepdims=True))

    return pl.pallas_call(
        body, grid=(tt // tr,), in_specs=[_rb(tr, B_W), _rb(tr, B_W, OFF_ZB // B_W), _whole((1, B_DH)), _rb(tr, B_W)],
        out_specs=(_rb(tr, B_W), _rb(tr, B_W), _whole((8, B_DH))),
        out_shape=(jax.ShapeDtypeStruct((tt, B_W), F32), jax.ShapeDtypeStruct((tt, B_W), BF16),
                   jax.ShapeDtypeStruct((8, B_DH), F32)),
        compiler_params=_cp(("arbitrary",)), name=name)(o, proj, gn, dy)


def _adamw(parts, w, m, v, own=None, sel=None, *, name, tr=256):
    npart, nl, r, c = parts.shape
    tr = max([t for t in range(8, min(r, tr) + 1, 8) if r % t == 0], default=r)
    c1, c2 = 1.0 - ADAM_B1 ** ADAM_STEP, 1.0 - ADAM_B2 ** ADAM_STEP

    def body(*refs):
        if own is None:
            p_ref, w_ref, m_ref, v_ref, g_ref, d_ref, mo_ref, vo_ref = refs
            part = lambda i: p_ref[i].astype(F32)
        else:
            p_ref, w_ref, m_ref, v_ref, own_ref, sel_ref, g_ref, d_ref, mo_ref, vo_ref = refs
            part = lambda i: jnp.where(sel_ref[i:i + 1, 0:1] > 0.5, own_ref[...].astype(F32), p_ref[i].astype(F32))
        g = part(0)
        for i in range(1, npart):
            g = g + part(i)
        mn = ADAM_B1 * m_ref[...] + (1.0 - ADAM_B1) * g
        vn = ADAM_B2 * v_ref[...] + (1.0 - ADAM_B2) * (g * g)
        g_ref[...] = g
        mo_ref[...] = mn
        vo_ref[...] = vn
        d_ref[...] = -ADAM_LR * ((mn / c1) / (jnp.sqrt(vn / c2) + ADAM_EPS) + ADAM_WD * w_ref[...])

    row = pl.BlockSpec((None, tr, c), lambda l, i: (l, i, 0))
    out = jax.ShapeDtypeStruct((nl, r, c), F32)
    ins, in_specs = [parts, w, m, v], [pl.BlockSpec((npart, None, tr, c), lambda l, i: (0, l, i, 0)), row, row, row]
    if own is not None:
        ins += [own, sel]
        in_specs += [row, pl.BlockSpec((N_DEV, LANE), lambda l, i: (0, 0))]
    return pl.pallas_call(body, grid=(nl, r // tr), in_specs=in_specs, out_specs=(row, row, row, row),
                          out_shape=(out, out, out, out), compiler_params=_cp(("parallel", "parallel")),
                          name=name)(*ins)


def _peer(k):
    x, y, c = lax.axis_index("x"), lax.axis_index("y"), lax.axis_index("c")
    return ((1 - x) if k & 4 else x, (1 - y) if k & 2 else y, (1 - c) if k & 1 else c)


def _my_index():
    return 4 * lax.axis_index("x") + 2 * lax.axis_index("y") + lax.axis_index("c")


def _index_of(p):
    return 4 * p[0] + 2 * p[1] + p[2]


def _all_gather(xs, *, name):
    n = len(xs)

    def body(*refs):
        x_refs, o_refs = refs[:n], refs[n:2 * n]
        send, recv, loc = refs[2 * n:]
        me = _my_index()
        copies = []
        for a in range(n):
            cp = pltpu.make_async_copy(x_refs[a], o_refs[a].at[me], loc.at[a])
            cp.start()
            copies.append(cp)
        rdmas = []
        for a in range(n):
            for k in range(1, N_DEV):
                r = pltpu.make_async_remote_copy(
                    src_ref=x_refs[a], dst_ref=o_refs[a].at[me], send_sem=send.at[a, k - 1], recv_sem=recv.at[a, k - 1],
                    device_id=_peer(k), device_id_type=pl.DeviceIdType.MESH)
                r.start()
                rdmas.append(r)
        for a in range(n):
            for k in range(1, N_DEV):
                pltpu.make_async_remote_copy(
                    src_ref=x_refs[a], dst_ref=o_refs[a].at[_index_of(_peer(k))], send_sem=send.at[a, k - 1],
                    recv_sem=recv.at[a, k - 1], device_id=_peer(k), device_id_type=pl.DeviceIdType.MESH).wait_recv()
        for r in rdmas:
            r.wait_send()
        for cp in copies:
            cp.wait()

    any_spec = pl.BlockSpec(memory_space=pl.ANY)
    return pl.pallas_call(
        body, in_specs=[any_spec] * n, out_specs=tuple([any_spec] * n),
        out_shape=tuple(jax.ShapeDtypeStruct((N_DEV,) + x.shape, x.dtype) for x in xs),
        scratch_shapes=[pltpu.SemaphoreType.DMA((n, N_DEV - 1)), pltpu.SemaphoreType.DMA((n, N_DEV - 1)),
                        pltpu.SemaphoreType.DMA((n,))],
        name=name)(*xs)


_HBM = pl.BlockSpec(memory_space=pltpu.HBM)
_SEM = pl.BlockSpec(memory_space=pltpu.SEMAPHORE)
_EFFECT = pltpu.SideEffectType.DATAFLOW_SIDE_EFFECTING


def _split_copy(src_ref, land_ref, send, recv, a, k, scatter, slot, sending):
    me, peer = _my_index(), _index_of(_peer(k))
    src = src_ref.at[peer if sending else me] if scatter else src_ref
    land = land_ref.at[me if sending else peer]
    if slot is not None:
        land = land.at[slot]
    sem = a * (N_DEV - 1) + k - 1
    return pltpu.make_async_remote_copy(src_ref=src, dst_ref=land, send_sem=send.at[sem], recv_sem=recv.at[sem],
                                        device_id=_peer(k), device_id_type=pl.DeviceIdType.MESH)


def _exchange_start(srcs, lands, after, *, scatter, slot=None, name):
    n = len(srcs)

    def body(*refs):
        src_refs, land_refs = refs[:n], refs[n:2 * n]
        send, recv, token = refs[2 * n + 1], refs[2 * n + 2], refs[-1]
        for a in range(n):
            for k in range(1, N_DEV):
                _split_copy(src_refs[a], land_refs[a], send, recv, a, k, scatter, slot, True).start()
        token[...] = jnp.zeros_like(token)

    hbm = lambda t: pltpu.HBM(t.shape, t.dtype)
    sems = pltpu.SemaphoreType.DMA((n * (N_DEV - 1),))
    out = pl.pallas_call(
        body, name=name,
        out_shape=(sems, sems, *[hbm(t) for t in srcs], *[hbm(t) for t in lands], jax.ShapeDtypeStruct((8, LANE), F32)),
        in_specs=[_HBM] * (2 * n) + [pl.BlockSpec(memory_space=pl.ANY)],
        out_specs=(_SEM, _SEM, *[_HBM] * (2 * n), pl.BlockSpec(memory_space=pltpu.VMEM)),
        input_output_aliases={i: 2 + i for i in range(2 * n)},
        compiler_params=pltpu.CompilerParams(has_side_effects=_EFFECT),
    )(*[pltpu.with_memory_space_constraint(t, pltpu.HBM) for t in (*srcs, *lands)], after)
    return out[0], out[1], out[2:2 + n], out[2 + n:2 + 2 * n], out[-1]


def _exchange_wait(send, recv, srcs, lands, after, *, scatter, slot=None, name):
    n = len(srcs)

    def body(*refs):
        src_refs, land_refs = refs[:n], refs[n:2 * n]
        send_ref, recv_ref = refs[2 * n], refs[2 * n + 1]
        for a in range(n):
            for k in range(1, N_DEV):
                _split_copy(src_refs[a], land_refs[a], send_ref, recv_ref, a, k, scatter, slot, True).wait_send()
                _split_copy(src_refs[a], land_refs[a], send_ref, recv_ref, a, k, scatter, slot, False).wait_recv()

    hbm = lambda t: pltpu.HBM(t.shape, t.dtype)
    out = pl.pallas_call(
        body, name=name, out_shape=(*[hbm(t) for t in srcs], *[hbm(t) for t in lands]),
        in_specs=[_HBM] * (2 * n) + [_SEM, _SEM, pl.BlockSpec(memory_space=pl.ANY)],
        out_specs=tuple([_HBM] * (2 * n)), input_output_aliases={i: i for i in range(2 * n)},
        compiler_params=pltpu.CompilerParams(has_side_effects=_EFFECT),
    )(*srcs, *lands, send, recv, after)
    return out[:n], out[n:]


def _win_to_mine(w):
    pad = jnp.zeros(w.shape[:-1] + (IN_PAD - IN_DIM,), w.dtype)
    return jnp.concatenate([w[..., 3592:5640], w[..., 0:3584], w[..., 3584:3592], pad], axis=-1)


def _win_from_mine(g):
    return jnp.concatenate([g[..., 2048:5632], g[..., 5632:5640], g[..., 0:2048]], axis=-1)


def _pad_rows(a, mult=8):
    r = (-a.shape[0]) % mult
    return a if r == 0 else jnp.concatenate([a, jnp.zeros((r,) + a.shape[1:], a.dtype)], axis=0)


def _lanes(vec, start):
    return jnp.zeros((1, LANE), F32).at[0, start:start + vec.shape[0]].set(vec)


def _small_spec(depth):
    return (("b_ada", (depth, 6 * D)), ("norm1_g", (depth, D)), ("norm2_g", (depth, D)),
            ("rel_table", (depth, A_HEADS, 2 * A_MAX_REL + 1)), ("a_log", (depth, B_HEADS)),
            ("dt_bias", (depth, B_HEADS)), ("gdn_norm_g", (depth, B_DH)), ("final_g", (D,)))


def _pack_small(d, extra, depth):
    spec = _small_spec(depth)
    rows = -(-(sum(math.prod(s) for _, s in spec) + 1) // (8 * LANE)) * 8
    flat = jnp.concatenate([d[n].reshape(-1).astype(F32) for n, _ in spec] + [extra.reshape(-1)])
    flat = jnp.concatenate([flat, jnp.zeros((rows * LANE - flat.shape[0],), F32)])
    return flat.reshape(rows, LANE)


def _unpack_small(p, depth):
    flat = p.reshape(-1)
    out, off = {}, 0
    for n, s in _small_spec(depth):
        sz = math.prod(s)
        out[n] = flat[off:off + sz].reshape(s)
        off += sz
    return out, flat[off]


def kernel(x, c, w_ada, b_ada, norm1_g, norm2_g, w_in, rel_table, w_conv, a_log, dt_bias, gdn_norm_g, w_branch_a, w_branch_b, w_out, w_ffn_in, w_ffn_out, final_g, loss_target, m_w_ada, m_b_ada, m_norm1_g, m_norm2_g, m_w_in, m_rel_table, m_w_conv, m_a_log, m_dt_bias, m_gdn_norm_g, m_w_branch_a, m_w_branch_b, m_w_out, m_w_ffn_in, m_w_ffn_out, m_final_g, v_w_ada, v_b_ada, v_norm1_g, v_norm2_g, v_w_in, v_rel_table, v_w_conv, v_a_log, v_dt_bias, v_gdn_norm_g, v_w_branch_a, v_w_branch_b, v_w_out, v_w_ffn_in, v_w_ffn_out, v_final_g):
    tt = x.shape[1]
    x0 = x[0]
    tgt = loss_target[0]
    me = _my_index()
    depth = w_in.shape[0]

    shards = [w_in.astype(BF16), w_branch_a.astype(BF16), w_branch_b.astype(BF16), w_out.astype(BF16),
              w_ffn_in.astype(BF16), w_ffn_out.astype(BF16), w_conv]
    names = ("win", "wa", "wb", "wout", "wfi", "wfo", "wconv")
    early, late, every = (0, 6), (1, 2, 3, 4, 5), tuple(range(7))
    first = _all_gather([shards[i][0] for i in early] + [_pad_rows(c)], name="gather_first")
    c_all = first[-1][:, 0, :]
    is_me = lax.broadcasted_iota(jnp.int32, (N_DEV, 1, 1), 0) == me

    def unpack(idx, g):
        cols = lambda t: jnp.transpose(t, (1, 0, 2)).reshape(t.shape[1], N_DEV * t.shape[2])
        rows = lambda t: t.reshape(N_DEV * t.shape[1], t.shape[2])
        how = (lambda t: _win_to_mine(cols(t)), cols, cols, rows, cols, rows, cols)
        return {names[i]: how[i](t) for i, t in zip(idx, g)}

    def gather_start(l, idx, after, tag=""):
        srcs = [shards[i][l] for i in idx]
        lands = [lax.empty((N_DEV,) + t.shape, t.dtype) for t in srcs]
        return _exchange_start(srcs, lands, after, scatter=False, name=f"gather_start_{l}{tag}")

    def gather_wait(l, idx, pending, after, tag=""):
        send, recv, srcs, lands, _ = pending
        srcs, lands = _exchange_wait(send, recv, srcs, lands, after, scatter=False, name=f"gather_wait_{l}{tag}")
        return unpack(idx, [jnp.where(is_me, t[None], g) for g, t in zip(lands, srcs)])

    weights = [unpack(early, first[:-1])] + [None] * (depth - 1)
    pending0 = gather_start(0, late, first[-1], "_rest")
    pending = gather_start(1, every, pending0[-1]) if depth > 1 else None
    cond = c_all * (1.0 / (1.0 + jnp.exp(-c_all)))
    cond = _pad_rows(cond, 16)

    mod_cols = jnp.stack([_mm(cond, w_ada[l], name="mod_mm")[:N_DEV] for l in range(depth)])
    (g_mod,) = _all_gather([mod_cols], name="gather_mod")
    mod_all = jnp.transpose(g_mod, (1, 2, 0, 3)).reshape(depth, N_DEV, 6 * D)
    mod = lax.dynamic_index_in_dim(mod_all, me, axis=1, keepdims=False) + b_ada
    mods = mod.reshape(depth, 6, 1, D)

    n1g, n2g = norm1_g.reshape(depth, 1, D), norm2_g.reshape(depth, 1, D)
    gng = gdn_norm_g.reshape(depth, 1, B_DH)
    fg = final_g.reshape(1, D)

    saved = []
    tok = (pending if pending is not None else pending0)[-1][0, 0]
    xin, h1 = _adaln_fwd(x0, n1g[0], mods[0, 1] + tok, mods[0, 0], name="adaln1_first")
    for l in range(depth):
        sh1, sc1, gt1, sh2, sc2, gt2 = (mods[l, i] for i in range(6))
        wl = weights[l]
        proj = _mm(h1, wl["win"], name="proj_mm", tn=1152)
        kpad, vpad = _kv_pad(proj, name="kv_pad")
        diag, bias_vjp = jax.vjp(_bias_diagonals, rel_table[l])
        bias = _bias_fwd(diag, name="bias_fwd")
        ya, lse = _attn_fwd(proj, kpad, vpad, bias, name="attn_fwd")
        par = jnp.concatenate([_lanes(a_log[l], B_HEADS), _lanes(dt_bias[l], B_HEADS), jnp.zeros((6, LANE), F32)], axis=0)
        qn, kn, vn, aux = _gdn_pre_fwd(proj, wl["wconv"], par, name="gdn_pre_fwd")
        auxt = _gc_rows(aux, tt // CH)
        lower = _gdn_lower(kn, aux, auxt, name="gdn_lower")
        tinv = jnp.transpose(_tri_inverse(jnp.transpose(lower, (1, 2, 0)), name="gdn_tri_inverse"), (2, 0, 1))
        u0, wg, qd, kd, qk, gle = _gdn_intra(qn, kn, vn, aux, auxt, tinv, name="gdn_intra")
        og, ss, ug = _gdn_scan_fwd(u0, wg, qd, kd, qk, gle, name="gdn_scan_fwd")
        yb = _gdn_post_fwd(og, proj, gng[l], name="gdn_post_fwd")
        if l == 0:
            wl.update(gather_wait(0, late, pending0, yb, "_rest"))
        pa = _mm(ya, wl["wa"], out_dtype=BF16, name="branch_a_mm")
        pb = _mm(yb, wl["wb"], out_dtype=BF16, name="branch_b_mm")
        merged = _merge_fwd(proj, pa, pb, name="merge_fwd")
        t1 = _mm(merged, wl["wout"], name="out_mm")
        x2, h2 = _adaln_fwd(xin, n2g[l], sc2, sh2, t1, gt1, name="adaln2_fwd")
        gu = _mm(h2, wl["wfi"], out_dtype=BF16, name="ffn_in_mm", tn=1408)
        act = _swiglu_fwd(gu, name="swiglu_fwd")
        t2 = _mm(act, wl["wfo"], name="ffn_out_mm", tk=1408)
        saved.append(dict(xin=xin, h1=h1, proj=proj, kpad=kpad, vpad=vpad, bias=bias, bias_vjp=bias_vjp, ya=ya, lse=lse,
                          par=par, qn=qn, kn=kn, vn=vn, aux=aux, auxt=auxt, tinv=tinv, ss=ss, og=og, yb=yb, pa=pa, pb=pb,
                          u0=u0, wg=wg, qd=qd, kd=kd, qk=qk, gle=gle, ug=ug,
                          merged=merged, t1=t1, x2=x2, h2=h2, gu=gu, act=act, t2=t2))
        if l + 1 < depth:
            weights[l + 1] = gather_wait(l + 1, every, pending, t2)
            pending = gather_start(l + 2, every, weights[l + 1]["wconv"]) if l + 2 < depth else None
            tok = pending[-1][0, 0] if pending is not None else 0.0
            xin, h1 = _adaln_fwd(x2, n1g[l + 1], mods[l + 1, 1] + tok, mods[l + 1, 0], t2, gt2, name="adaln1_fwd")

    s = saved[-1]
    dx, dt2, st = _loss_head(s["x2"], s["t2"], mods[depth - 1, 5], fg, tgt, name="loss_head")
    loss_part = st[4, 0]
    small_g = {"final_g": st[0]}
    dmod_rows = [None] * depth
    for n in ("norm1_g", "norm2_g", "rel_table", "a_log", "dt_bias", "gdn_norm_g"):
        small_g[n] = [None] * depth
    dgt2 = st[3]
    cols_slabs = lambda g: jnp.transpose(g.reshape(g.shape[0], N_DEV, g.shape[1] // N_DEV), (1, 0, 2))
    rows_slabs = lambda g: g.reshape(N_DEV, g.shape[0] // N_DEV, g.shape[1])
    mix, ffn = (0, 1, 2, 3, 6), (4, 5)
    lands = {kind: [lax.empty((N_DEV,) + shards[i].shape, shards[i].dtype) for i in idx]
             for kind, idx in (("mix", mix), ("ffn", ffn))}
    own = {kind: [None] * depth for kind in lands}
    pending_s = {kind: None for kind in lands}

    def scatter(kind, l, srcs, after):
        if pending_s[kind] is not None:
            done, lands[kind] = _exchange_wait(*pending_s[kind][:4], after, scatter=True, slot=l + 1,
                                               name=f"scatter_wait_{kind}_{l + 1}")
            own[kind][l + 1] = [lax.dynamic_index_in_dim(t, me, 0, keepdims=False) for t in done]
        pending_s[kind] = _exchange_start(srcs, lands[kind], after, scatter=True, slot=l, name=f"scatter_start_{kind}_{l}")
        return pending_s[kind][-1][0, 0]

    for l in reversed(range(depth)):
        s, wl = saved[l], weights[l]
        sh1, sc1, gt1, sh2, sc2, gt2 = (mods[l, i] for i in range(6))
        gw_fo = _mm(s["act"], dt2, ta=True, out_dtype=BF16, name="ffn_out_dw", tm=1408)
        dact = _mm(dt2, wl["wfo"], tb=True, out_dtype=BF16, name="ffn_out_dx", tn=1408)
        dgu = _swiglu_bwd(s["gu"], dact, name="swiglu_bwd")
        gw_fi = _mm(s["h2"], dgu, ta=True, out_dtype=BF16, name="ffn_in_dw", tn=1408)
        sc2 = sc2 + scatter("ffn", l, [cols_slabs(gw_fi), rows_slabs(gw_fo)], gw_fi)
        dh2 = _mm(dgu, wl["wfi"], tb=True, name="ffn_in_dx", tk=1408)
        dx, dt1, st2 = _adaln_bwd(s["x2"], n2g[l], sc2, sh2, dh2, dx, s["t1"], gt1, name="adaln2_bwd")
        gw_out = _mm(s["merged"], dt1, ta=True, out_dtype=BF16, name="out_dw")
        dmerged = _mm(dt1, wl["wout"], tb=True, name="out_dx")
        dgates, dpa, dpb = _merge_bwd(s["proj"], s["pa"], s["pb"], dmerged, name="merge_bwd")
        gw_a = _mm(s["ya"], dpa, ta=True, out_dtype=BF16, name="branch_a_dw")
        gw_b = _mm(s["yb"], dpb, ta=True, out_dtype=BF16, name="branch_b_dw")
        dya = _mm(dpa, wl["wa"], tb=True, name="branch_a_dx")
        dyb = _mm(dpb, wl["wb"], tb=True, name="branch_b_dx")
        dqa, dka, dva, dbias = _attn_bwd(s["proj"], s["kpad"], s["vpad"], s["bias"], s["ya"], s["lse"], dya,
                                             name="attn_bwd")
        ddiag = jnp.roll(_bias_bwd(dbias, name="bias_bwd"), -(QB - 1), axis=1)
        small_g["rel_table"][l] = s["bias_vjp"](ddiag)[0]
        dog, dz, dgn = _gdn_post_bwd(s["og"], s["proj"], gng[l], dyb, name="gdn_post_bwd")
        small_g["gdn_norm_g"][l] = dgn[0]
        dug, dss = _gdn_scan_bwd(s["wg"], s["qd"], s["kd"], s["qk"], s["gle"], dog, name="gdn_scan_bwd")
        dqn, dkn, dvn, daux = _gdn_bwd(s["qn"], s["kn"], s["vn"], s["aux"], s["auxt"], s["tinv"], s["u0"], s["wg"],
                                       s["ug"], s["ss"], dss, dug, dog, name="gdn_bwd")
        dqkv, dba, dwc, dpar = _gdn_pre_bwd(s["proj"], wl["wconv"], s["par"], dqn, dkn, dvn, daux, name="gdn_pre_bwd")
        small_g["a_log"][l] = dpar[0, B_HEADS:2 * B_HEADS]
        small_g["dt_bias"][l] = dpar[1, B_HEADS:2 * B_HEADS]
        dproj = jnp.concatenate([dgates, dqa, dka, dva, dqkv, dz, dba], axis=1)
        gw_in = _mm(s["h1"], dproj, ta=True, out_dtype=BF16, name="proj_dw", tn=1152)
        dh1 = _mm(dproj, wl["win"], tb=True, name="proj_dx", tk=1152)
        mix_srcs = [cols_slabs(_win_from_mine(gw_in)), cols_slabs(gw_a), cols_slabs(gw_b), rows_slabs(gw_out),
                    cols_slabs(dwc[0:CONV_K])]
        if l > 0:
            sc1 = sc1 + scatter("mix", l, mix_srcs, gw_in)
        if l > 0:
            p = saved[l - 1]
            dx, dt2, st1 = _adaln_bwd(s["xin"], n1g[l], sc1, sh1, dh1, dx, p["t2"], mods[l - 1, 5], name="adaln1_bwd")
        else:
            dx, st1 = _adaln_bwd(s["xin"], n1g[l], sc1, sh1, dh1, dx, name="adaln1_bwd_first")
        small_g["norm1_g"][l], small_g["norm2_g"][l] = st1[0], st2[0]
        dmod_rows[l] = jnp.concatenate([st1[2], st1[1], st2[3], st2[2], st2[1], dgt2])
        if l > 0:
            dgt2 = st1[3]
    grad_x = dx[None]

    small_local = {n: (jnp.stack(vs) if isinstance(vs, list) else vs) for n, vs in small_g.items()}
    small_local["b_ada"] = jnp.stack(dmod_rows)
    (g_small,) = _all_gather([_pack_small(small_local, loss_part, depth)], name="gather_small")
    tok = scatter("mix", 0, mix_srcs, g_small)
    wsm = _pack_small(dict(b_ada=b_ada, norm1_g=norm1_g, norm2_g=norm2_g, rel_table=rel_table, a_log=a_log,
                           dt_bias=dt_bias, gdn_norm_g=gdn_norm_g, final_g=final_g), jnp.zeros((1,), F32) + tok, depth)
    msm = _pack_small(dict(b_ada=m_b_ada, norm1_g=m_norm1_g, norm2_g=m_norm2_g, rel_table=m_rel_table, a_log=m_a_log,
                           dt_bias=m_dt_bias, gdn_norm_g=m_gdn_norm_g, final_g=m_final_g), jnp.zeros((1,), F32), depth)
    vsm = _pack_small(dict(b_ada=v_b_ada, norm1_g=v_norm1_g, norm2_g=v_norm2_g, rel_table=v_rel_table, a_log=v_a_log,
                           dt_bias=v_dt_bias, gdn_norm_g=v_gdn_norm_g, final_g=v_final_g), jnp.ones((1,), F32), depth)
    sm = [_unpack_small(t, depth) for t in _adamw(g_small[:, None], wsm[None], msm[None], vsm[None], name="adamw_small")]
    loss = sm[0][1]

    dmod_all = g_small.reshape(N_DEV, -1)[:, :depth * 6 * D].reshape(N_DEV, depth, 6 * D)
    dmod_mine = lax.dynamic_slice_in_dim(dmod_all, me * (6 * D // N_DEV), 6 * D // N_DEV, axis=2)
    g_ada = jnp.stack([_mm(cond, _pad_rows(dmod_mine[:, l], 16), ta=True, name="ada_dw") for l in range(depth)])

    got, mine = {}, {}
    sel = jnp.broadcast_to(jnp.where(is_me[:, :, 0], 1.0, 0.0), (N_DEV, LANE)).astype(F32)

    def finish(kind, idx, after):
        done, lands[kind] = _exchange_wait(*pending_s[kind][:4], after, scatter=True, slot=0, name=f"scatter_wait_{kind}_0")
        own[kind][0] = [lax.dynamic_index_in_dim(t, me, 0, keepdims=False) for t in done]
        for a, i in enumerate(idx):
            got[i] = lands[kind][a]
            mine[i] = jnp.stack([own[kind][l][a] for l in range(depth)])

    def upd(i, w, m, v, name):
        return _adamw(got[i], w, m, v, mine[i], sel, name=name)

    finish("ffn", ffn, g_ada)
    res = {
        "w_ada": _adamw(g_ada[None], w_ada, m_w_ada, v_w_ada, name="adamw_w_ada"),
        "w_ffn_in": upd(4, w_ffn_in, m_w_ffn_in, v_w_ffn_in, "adamw_w_ffn_in"),
        "w_ffn_out": upd(5, w_ffn_out, m_w_ffn_out, v_w_ffn_out, "adamw_w_ffn_out"),
    }
    finish("mix", mix, res["w_ffn_out"][0])
    res.update({
        "w_in": upd(0, w_in, m_w_in, v_w_in, "adamw_w_in"),
        "w_conv": upd(6, w_conv, m_w_conv, v_w_conv, "adamw_w_conv"),
        "w_branch_a": upd(1, w_branch_a, m_w_branch_a, v_w_branch_a, "adamw_w_branch_a"),
        "w_branch_b": upd(2, w_branch_b, m_w_branch_b, v_w_branch_b, "adamw_w_branch_b"),
        "w_out": upd(3, w_out, m_w_out, v_w_out, "adamw_w_out"),
    })
    for n, _ in _small_spec(depth):
        res[n] = [sm[i][0][n] for i in range(4)]
    order = ("w_ada", "b_ada", "norm1_g", "norm2_g", "w_in", "rel_table", "w_conv", "a_log", "dt_bias", "gdn_norm_g",
             "w_branch_a", "w_branch_b", "w_out", "w_ffn_in", "w_ffn_out", "final_g")
    return (loss, grad_x, *[res[n][0] for n in order], *[res[n][1] for n in order],
            *[res[n][2] for n in order], *[res[n][3] for n in order])
```

```python
import functools
import math

import jax
import jax.numpy as jnp
from jax import lax
from jax.experimental import pallas as pl
from jax.experimental.pallas import tpu as pltpu

F32 = jnp.float32
BF16 = jnp.bfloat16
HI = lax.Precision.HIGHEST

N_DEV = 8
D = 1024
DEPTH = 4
CH = 64
EPS = 1e-6
A_HEADS, A_DH = 8, 64
A_W = A_HEADS * A_DH
A_PAST = 8
A_MAX_REL = 128
QB = 256
KB = QB + A_PAST * CH
B_HEADS, B_DH = 4, 128
B_W = B_HEADS * B_DH
CONV_K = 4
FF = 2816
IN_DIM = 5640
IN_PAD = 5760
LANE = 128
NEG = -1e30
VMEM_LIMIT = 48 * 1024 * 1024

ADAM_LR, ADAM_B1, ADAM_B2, ADAM_EPS, ADAM_WD, ADAM_STEP = 0.001, 0.9, 0.999, 1e-08, 0.01, 10

OFF_GA, OFF_GB, OFF_QA, OFF_KA, OFF_VA, OFF_QB, OFF_KB, OFF_VB, OFF_ZB, OFF_BA = (
    0, 1024, 2048, 2560, 3072, 3584, 4096, 4608, 5120, 5632)


def _cp(sem=None):
    return pltpu.CompilerParams(dimension_semantics=sem, vmem_limit_bytes=VMEM_LIMIT)


def _tile(n, pref):
    if n <= pref:
        return n
    best = None
    for t in range(LANE, pref + 1, LANE):
        if n % t == 0:
            best = t
    assert best is not None, (n, pref)
    return best


def _sigmoid(x):
    return 1.0 / (1.0 + jnp.exp(-x))


def _silu(x):
    return x * _sigmoid(x)


def _dsilu(x):
    s = _sigmoid(x)
    return s * (1.0 + x * (1.0 - s))


def _dot(a, b, prec=None):
    return jnp.dot(a, b, preferred_element_type=F32, precision=prec)


def _dot_nt(a, b, prec=None):
    return lax.dot_general(a, b, (((1,), (1,)), ((), ())), preferred_element_type=F32, precision=prec)


def _dot_tn(a, b, prec=None):
    return lax.dot_general(a, b, (((0,), (0,)), ((), ())), preferred_element_type=F32, precision=prec)


def _mm(a, b, *, ta=False, tb=False, out_dtype=F32, name, tm=1024, tn=1024, tk=1024):
    m, k = (a.shape[1], a.shape[0]) if ta else a.shape
    n = b.shape[0] if tb else b.shape[1]
    assert k == (b.shape[1] if tb else b.shape[0]), (a.shape, b.shape, ta, tb)
    tm, tn, tk = _tile(m, tm), _tile(n, tn), _tile(k, tk)
    nk = k // tk
    dn = (((0 if ta else 1,), (1 if tb else 0,)), ((), ()))

    def body(a_ref, b_ref, o_ref, *acc):
        part = lax.dot_general(a_ref[...].astype(BF16), b_ref[...].astype(BF16), dn, preferred_element_type=F32)
        if nk == 1:
            o_ref[...] = part.astype(out_dtype)
            return
        acc_ref, kk = acc[0], pl.program_id(2)

        @pl.when(kk == 0)
        def _():
            acc_ref[...] = part

        @pl.when(kk > 0)
        def _():
            acc_ref[...] += part

        @pl.when(kk == nk - 1)
        def _():
            o_ref[...] = acc_ref[...].astype(out_dtype)

    a_spec = pl.BlockSpec((tk, tm), lambda i, j, q: (q, i)) if ta else pl.BlockSpec((tm, tk), lambda i, j, q: (i, q))
    b_spec = pl.BlockSpec((tn, tk), lambda i, j, q: (j, q)) if tb else pl.BlockSpec((tk, tn), lambda i, j, q: (q, j))
    return pl.pallas_call(
        body, grid=(m // tm, n // tn, nk), in_specs=[a_spec, b_spec],
        out_specs=pl.BlockSpec((tm, tn), lambda i, j, q: (i, j)),
        out_shape=jax.ShapeDtypeStruct((m, n), out_dtype),
        scratch_shapes=[pltpu.VMEM((tm, tn), F32)] if nk > 1 else [],
        compiler_params=_cp(("parallel", "parallel", "arbitrary")), name=name)(a, b)


def _rb(tr, width, cb=0):
    return pl.BlockSpec((tr, width), lambda i: (i, cb))


def _whole(shape):
    nd = len(shape)
    return pl.BlockSpec(shape, lambda i: (0,) * nd)


def _colsum(v):
    return jnp.sum(v, axis=0, keepdims=True)


def _adaln_fwd(x, g, sc, sh, t=None, gt=None, *, name, tr=256):
    tt = x.shape[0]
    res = t is not None

    def body(*refs):
        if res:
            x_ref, t_ref, gt_ref, g_ref, sc_ref, sh_ref, xo_ref, h_ref = refs
            xv = x_ref[...] + gt_ref[...] * t_ref[...]
            xo_ref[...] = xv
        else:
            x_ref, g_ref, sc_ref, sh_ref, h_ref = refs
            xv = x_ref[...]
        r = lax.rsqrt(jnp.mean(xv * xv, axis=-1, keepdims=True) + EPS)
        h_ref[...] = ((xv * r * g_ref[...]) * (1.0 + sc_ref[...]) + sh_ref[...]).astype(BF16)

    row, vec = _rb(tr, D), _whole((1, D))
    if res:
        ins, in_specs = (x, t, gt, g, sc, sh), [row, row, vec, vec, vec, vec]
        out_shape = (jax.ShapeDtypeStruct((tt, D), F32), jax.ShapeDtypeStruct((tt, D), BF16))
        out_specs = (row, row)
    else:
        ins, in_specs = (x, g, sc, sh), [row, vec, vec, vec]
        out_shape, out_specs = jax.ShapeDtypeStruct((tt, D), BF16), row
    out = pl.pallas_call(body, grid=(tt // tr,), in_specs=in_specs, out_specs=out_specs, out_shape=out_shape,
                         compiler_params=_cp(("parallel",)), name=name)(*ins)
    return out if res else (x, out)


def _adaln_bwd(x, g, sc, sh, dh, dx_in, t=None, gt=None, *, name, tr=256):
    tt = x.shape[0]
    res = t is not None

    def body(*refs):
        if res:
            x_ref, g_ref, sc_ref, sh_ref, dh_ref, dxi_ref, t_ref, gt_ref, dx_ref, dt_ref, st_ref = refs
        else:
            x_ref, g_ref, sc_ref, sh_ref, dh_ref, dxi_ref, dx_ref, st_ref = refs

        @pl.when(pl.program_id(0) == 0)
        def _():
            st_ref[...] = jnp.zeros_like(st_ref)

        xv, dh = x_ref[...], dh_ref[...]
        r = lax.rsqrt(jnp.mean(xv * xv, axis=-1, keepdims=True) + EPS)
        nrm = xv * r
        y = nrm * g_ref[...]
        dy = dh * (1.0 + sc_ref[...])
        dn = dy * g_ref[...]
        dx = dxi_ref[...] + r * (dn - nrm * jnp.mean(dn * nrm, axis=-1, keepdims=True))
        dx_ref[...] = dx
        st_ref[0:1, :] += _colsum(dy * nrm)
        st_ref[1:2, :] += _colsum(dh * y)
        st_ref[2:3, :] += _colsum(dh)
        if res:
            dt_ref[...] = (gt_ref[...] * dx).astype(BF16)
            st_ref[3:4, :] += _colsum(dx * t_ref[...])

    row, vec, st = _rb(tr, D), _whole((1, D)), _whole((8, D))
    ins, in_specs = [x, g, sc, sh, dh, dx_in], [row, vec, vec, vec, row, row]
    out_shape, out_specs = [jax.ShapeDtypeStruct((tt, D), F32)], [row]
    if res:
        ins += [t, gt]
        in_specs += [row, vec]
        out_shape.append(jax.ShapeDtypeStruct((tt, D), BF16))
        out_specs.append(row)
    out_shape.append(jax.ShapeDtypeStruct((8, D), F32))
    out_specs.append(st)
    return pl.pallas_call(body, grid=(tt // tr,), in_specs=in_specs, out_specs=tuple(out_specs),
                          out_shape=tuple(out_shape), compiler_params=_cp(("arbitrary",)), name=name)(*ins)


def _loss_head(x, t, gt, fg, tgt, *, name, tr=256):
    tt = x.shape[0]

    def body(x_ref, t_ref, gt_ref, fg_ref, tgt_ref, dx_ref, dt_ref, st_ref):
        @pl.when(pl.program_id(0) == 0)
        def _():
            st_ref[...] = jnp.zeros_like(st_ref)

        tv = t_ref[...]
        xv = x_ref[...] + gt_ref[...] * tv
        r = lax.rsqrt(jnp.mean(xv * xv, axis=-1, keepdims=True) + EPS)
        nrm = xv * r
        err = nrm * fg_ref[...] - tgt_ref[...]
        st_ref[4:5, :] += 0.5 * jnp.sum(jnp.mean(err * err, axis=-1, keepdims=True), axis=0, keepdims=True)
        dy = err * (1.0 / D)
        dn = dy * fg_ref[...]
        dx = r * (dn - nrm * jnp.mean(dn * nrm, axis=-1, keepdims=True))
        dx_ref[...] = dx
        dt_ref[...] = (gt_ref[...] * dx).astype(BF16)
        st_ref[0:1, :] += _colsum(dy * nrm)
        st_ref[3:4, :] += _colsum(dx * tv)

    row, vec = _rb(tr, D), _whole((1, D))
    return pl.pallas_call(
        body, grid=(tt // tr,), in_specs=[row, row, vec, vec, row], out_specs=(row, row, _whole((8, D))),
        out_shape=(jax.ShapeDtypeStruct((tt, D), F32), jax.ShapeDtypeStruct((tt, D), BF16),
                   jax.ShapeDtypeStruct((8, D), F32)),
        compiler_params=_cp(("arbitrary",)), name=name)(x, t, gt, fg, tgt)


def _merge_fwd(proj, pa, pb, *, name, tr=256):
    tt = pa.shape[0]

    def body(ga_ref, gb_ref, pa_ref, pb_ref, o_ref):
        o_ref[...] = (_sigmoid(ga_ref[...]) * pa_ref[...].astype(F32)
                      + _sigmoid(gb_ref[...]) * pb_ref[...].astype(F32)).astype(BF16)

    row = _rb(tr, D)
    return pl.pallas_call(body, grid=(tt // tr,), in_specs=[_rb(tr, D, 0), _rb(tr, D, 1), row, row], out_specs=row,
                          out_shape=jax.ShapeDtypeStruct((tt, D), BF16), compiler_params=_cp(("parallel",)),
                          name=name)(proj, proj, pa, pb)


def _merge_bwd(proj, pa, pb, dm, *, name, tr=256):
    tt = pa.shape[0]

    def body(ga_ref, gb_ref, pa_ref, pb_ref, dm_ref, dg_ref, dpa_ref, dpb_ref):
        dm_v = dm_ref[...]
        sa, sb = _sigmoid(ga_ref[...]), _sigmoid(gb_ref[...])
        dpa_ref[...] = (dm_v * sa).astype(BF16)
        dpb_ref[...] = (dm_v * sb).astype(BF16)
        dg_ref[:, 0:D] = (dm_v * pa_ref[...].astype(F32) * sa * (1.0 - sa)).astype(BF16)
        dg_ref[:, D:2 * D] = (dm_v * pb_ref[...].astype(F32) * sb * (1.0 - sb)).astype(BF16)

    row = _rb(tr, D)
    return pl.pallas_call(
        body, grid=(tt // tr,), in_specs=[_rb(tr, D, 0), _rb(tr, D, 1), row, row, row],
        out_specs=(_rb(tr, 2 * D), row, row),
        out_shape=(jax.ShapeDtypeStruct((tt, 2 * D), BF16), jax.ShapeDtypeStruct((tt, D), BF16),
                   jax.ShapeDtypeStruct((tt, D), BF16)),
        compiler_params=_cp(("parallel",)), name=name)(proj, proj, pa, pb, dm)


def _swiglu_fwd(gu, *, name, tr=256):
    tt = gu.shape[0]

    def body(g_ref, u_ref, o_ref):
        o_ref[...] = (_silu(g_ref[...].astype(F32)) * u_ref[...].astype(F32)).astype(BF16)

    return pl.pallas_call(body, grid=(tt // tr,), in_specs=[_rb(tr, FF, 0), _rb(tr, FF, 1)], out_specs=_rb(tr, FF),
                          out_shape=jax.ShapeDtypeStruct((tt, FF), BF16), compiler_params=_cp(("parallel",)),
                          name=name)(gu, gu)


def _swiglu_bwd(gu, dact, *, name, tr=256):
    tt = gu.shape[0]

    def body(g_ref, u_ref, da_ref, o_ref):
        gv, da = g_ref[...].astype(F32), da_ref[...].astype(F32)
        o_ref[:, 0:FF] = (da * u_ref[...].astype(F32) * _dsilu(gv)).astype(BF16)
        o_ref[:, FF:2 * FF] = (da * _silu(gv)).astype(BF16)

    return pl.pallas_call(body, grid=(tt // tr,), in_specs=[_rb(tr, FF, 0), _rb(tr, FF, 1), _rb(tr, FF)],
                          out_specs=_rb(tr, 2 * FF), out_shape=jax.ShapeDtypeStruct((tt, 2 * FF), BF16),
                          compiler_params=_cp(("parallel",)), name=name)(gu, gu, dact)


BIAS_LW = 1152


def _bias_diagonals(table):
    n_far = A_PAST * CH - A_MAX_REL + 1
    far = jnp.broadcast_to(table[:, 2 * A_MAX_REL:], (A_HEADS, n_far))
    mid = jnp.flip(table[:, 1:2 * A_MAX_REL], axis=1)
    near = jnp.broadcast_to(table[:, 0:1], (A_HEADS, KB - n_far - (2 * A_MAX_REL - 1)))
    pos = jnp.concatenate([far, mid, near], axis=1)
    neg = jnp.broadcast_to(table[:, 2 * A_MAX_REL:], (A_HEADS, QB - 1))
    gap = jnp.zeros((A_HEADS, BIAS_LW - KB - (QB - 1)), F32)
    return jnp.concatenate([pos, gap, neg], axis=1)


def _bias_fwd(diag, *, name):
    def body(w_ref, o_ref):
        qc = lax.broadcasted_iota(jnp.int32, (QB, KB), 0) // CH + A_PAST
        col = lax.broadcasted_iota(jnp.int32, (QB, KB), 1)
        inband = (col // CH <= qc) & (col // CH >= qc - A_PAST)
        for h in range(A_HEADS):
            rows = pltpu.roll(jnp.broadcast_to(w_ref[h:h + 1, :], (QB, BIAS_LW)), 0, 1, stride=1, stride_axis=0)
            for var in range(3):
                o_ref[var, h] = jnp.where(inband & (col >= A_PAST * CH - QB * var), rows[:, :KB], NEG)

    return pl.pallas_call(body, out_shape=jax.ShapeDtypeStruct((3, A_HEADS, QB, KB), F32), compiler_params=_cp(),
                          name=name)(diag)


def _bias_bwd(dbias, *, name):
    def body(d_ref, o_ref):
        r = lax.broadcasted_iota(jnp.int32, (QB, QB), 0)
        c = lax.broadcasted_iota(jnp.int32, (QB, QB), 1)
        flip = jnp.where(r + c == QB - 1, 1.0, 0.0).astype(F32)
        for h in range(A_HEADS):
            x = jnp.concatenate([_dot(flip, d_ref[h], HI), jnp.zeros((QB, BIAS_LW - KB), F32)], axis=1)
            o_ref[h:h + 1, :] = jnp.sum(pltpu.roll(x, 0, 1, stride=1, stride_axis=0), axis=0, keepdims=True)

    return pl.pallas_call(body, out_shape=jax.ShapeDtypeStruct((A_HEADS, BIAS_LW), F32), compiler_params=_cp(),
                          name=name)(dbias)


def _kv_pad(proj, *, name, tr=256):
    tt = proj.shape[0]
    npad = A_PAST * CH // tr

    def body(k_ref, v_ref, ko_ref, vo_ref):
        i = pl.program_id(0)

        @pl.when(i < npad)
        def _():
            ko_ref[...] = jnp.zeros_like(ko_ref)
            vo_ref[...] = jnp.zeros_like(vo_ref)

        @pl.when(i >= npad)
        def _():
            ko_ref[...] = k_ref[...].astype(BF16)
            vo_ref[...] = v_ref[...].astype(BF16)

    src = lambda off: pl.BlockSpec((tr, A_W), lambda i: (jnp.maximum(i - npad, 0), off // A_W))
    out = jax.ShapeDtypeStruct((tt + A_PAST * CH, A_W), BF16)
    return pl.pallas_call(body, grid=(tt // tr + npad,), in_specs=[src(OFF_KA), src(OFF_VA)],
                          out_specs=(_rb(tr, A_W), _rb(tr, A_W)), out_shape=(out, out),
                          compiler_params=_cp(("parallel",)), name=name)(proj, proj)


def _attn_fwd(proj, kpad, vpad, bias, *, name):
    tt = proj.shape[0]

    def body(q_ref, k_ref, v_ref, b_ref, o_ref, l_ref):
        q0 = pl.multiple_of(pl.program_id(1) * QB, QB)
        q = q_ref[...] * (A_DH ** -0.5)
        k = k_ref[pl.ds(q0, KB), :]
        v = v_ref[pl.ds(q0, KB), :]
        lane = lax.broadcasted_iota(jnp.int32, (QB, LANE), 1)
        o = jnp.zeros((QB, LANE), F32)
        lse = jnp.zeros((QB, LANE), F32)
        for a in range(2):
            hm = (lane >= A_DH * a) & (lane < A_DH * (a + 1))
            s = _dot_nt(jnp.where(hm, q, 0.0).astype(BF16), k) + b_ref[a]
            m = jnp.max(s, axis=-1, keepdims=True)
            p = jnp.exp(s - m)
            l = jnp.sum(p, axis=-1, keepdims=True)
            o = jnp.where(hm, _dot(p.astype(BF16), v) / l, o)
            lse = jnp.where(hm, m + jnp.log(l), lse)
        o_ref[...] = o.astype(BF16)
        l_ref[...] = lse

    kv = pl.BlockSpec((tt + A_PAST * CH, LANE), lambda h, i: (0, h))
    blk = pl.BlockSpec((QB, LANE), lambda h, i: (i, h))
    return pl.pallas_call(
        body, grid=(A_W // LANE, tt // QB),
        in_specs=[pl.BlockSpec((QB, LANE), lambda h, i: (i, OFF_QA // LANE + h)), kv, kv,
                  pl.BlockSpec((None, 2, QB, KB), lambda h, i: (jnp.minimum(i, 2), h, 0, 0))],
        out_specs=(blk, blk),
        out_shape=(jax.ShapeDtypeStruct((tt, A_W), BF16), jax.ShapeDtypeStruct((tt, A_W), F32)),
        compiler_params=_cp(("parallel", "parallel")), name=name)(proj, kpad, vpad, bias)


def _attn_bwd(proj, kpad, vpad, bias, o, lse, do, *, name):
    tt = proj.shape[0]
    nq = tt // QB

    def body(q_ref, k_ref, v_ref, b_ref, o_ref, l_ref, do_ref, dq_ref, dko_ref, dvo_ref, db_ref, dk_ref, dv_ref):
        @pl.when(pl.program_id(1) == 0)
        def _():
            dk_ref[...] = jnp.zeros_like(dk_ref)
            dv_ref[...] = jnp.zeros_like(dv_ref)
            db_ref[...] = jnp.zeros_like(db_ref)

        q0 = pl.multiple_of(pl.program_id(1) * QB, QB)
        q, do_v, lse = q_ref[...] * (A_DH ** -0.5), do_ref[...], l_ref[...]
        k = k_ref[pl.ds(q0, KB), :]
        v = v_ref[pl.ds(q0, KB), :]
        dsum = do_v * o_ref[...].astype(F32)
        lane = lax.broadcasted_iota(jnp.int32, (QB, LANE), 1)
        dq = jnp.zeros((QB, LANE), F32)
        dk = jnp.zeros((KB, LANE), F32)
        dv = jnp.zeros((KB, LANE), F32)
        for a in range(2):
            hm = (lane >= A_DH * a) & (lane < A_DH * (a + 1))
            qa = jnp.where(hm, q, 0.0).astype(BF16)
            doa = jnp.where(hm, do_v, 0.0).astype(BF16)
            s = _dot_nt(qa, k) + b_ref[a]
            lse_a = jnp.max(jnp.where(hm, lse, NEG), axis=-1, keepdims=True)
            p = jnp.exp(s - lse_a)
            dp = _dot_nt(doa, v)
            dsum_a = jnp.sum(jnp.where(hm, dsum, 0.0), axis=-1, keepdims=True)
            ds = p * (dp - dsum_a)
            db_ref[a] += ds
            dsb = ds.astype(BF16)
            dq = jnp.where(hm, _dot(dsb, k) * (A_DH ** -0.5), dq)
            dk += _dot_tn(dsb, qa)
            dv += _dot_tn(p.astype(BF16), doa)
        dq_ref[...] = dq.astype(BF16)
        dk_ref[pl.ds(q0, KB), :] += dk
        dv_ref[pl.ds(q0, KB), :] += dv

        @pl.when(pl.program_id(1) == nq - 1)
        def _():
            dko_ref[...] = dk_ref[A_PAST * CH:, :].astype(BF16)
            dvo_ref[...] = dv_ref[A_PAST * CH:, :].astype(BF16)

    kv = pl.BlockSpec((tt + A_PAST * CH, LANE), lambda h, i: (0, h))
    blk = pl.BlockSpec((QB, LANE), lambda h, i: (i, h))
    col = pl.BlockSpec((tt, LANE), lambda h, i: (0, h))
    bsp = pl.BlockSpec((2, QB, KB), lambda h, i: (h, 0, 0))
    bias_in = pl.BlockSpec((None, 2, QB, KB), lambda h, i: (jnp.minimum(i, 2), h, 0, 0))
    out = jax.ShapeDtypeStruct((tt, A_W), BF16)
    return pl.pallas_call(
        body, grid=(A_W // LANE, nq),
        in_specs=[pl.BlockSpec((QB, LANE), lambda h, i: (i, OFF_QA // LANE + h)), kv, kv, bias_in, blk, blk, blk],
        out_specs=(blk, col, col, bsp),
        out_shape=(out, out, out, jax.ShapeDtypeStruct((A_HEADS, QB, KB), F32)),
        scratch_shapes=[pltpu.VMEM((tt + A_PAST * CH, LANE), F32), pltpu.VMEM((tt + A_PAST * CH, LANE), F32)],
        compiler_params=_cp(("parallel", "arbitrary")), name=name)(proj, kpad, vpad, bias, o, lse, do)


GTR = 256


def _taps(w_ref, grp):
    return [w_ref[j:j + 1, grp * B_W:(grp + 1) * B_W] for j in range(CONV_K)]


def _shifts(xe, rows):
    return [xe[8:8 + rows]] + [pltpu.roll(xe, s, 0)[8:8 + rows] for s in range(1, CONV_K)]


def _conv(shifts, taps):
    acc = taps[CONV_K - 1] * shifts[0]
    for s in range(1, CONV_K):
        acc = acc + taps[CONV_K - 1 - s] * shifts[s]
    return acc


def _qk_scale(grp):
    return B_DH ** -0.5 if grp == 0 else 1.0


def _act_fwd(c, grp):
    y = _silu(c)
    if grp == 2:
        return y
    parts = []
    for hd in range(B_HEADS):
        yh = y[:, hd * B_DH:(hd + 1) * B_DH]
        parts.append(yh * (lax.rsqrt(jnp.sum(yh * yh, axis=-1, keepdims=True) + EPS) * _qk_scale(grp)))
    return jnp.concatenate(parts, axis=1)


def _act_bwd(c, dy, grp):
    if grp == 2:
        return dy * _dsilu(c)
    y = _silu(c)
    parts = []
    for hd in range(B_HEADS):
        yh = y[:, hd * B_DH:(hd + 1) * B_DH]
        r = lax.rsqrt(jnp.sum(yh * yh, axis=-1, keepdims=True) + EPS)
        dyh = dy[:, hd * B_DH:(hd + 1) * B_DH] * _qk_scale(grp)
        parts.append(r * dyh - yh * (r * r * r) * jnp.sum(dyh * yh, axis=-1, keepdims=True))
    return jnp.concatenate(parts, axis=1) * _dsilu(c)


def _chunk_tri(n, upper=False):
    r = lax.broadcasted_iota(jnp.int32, (n, n), 0)
    c = lax.broadcasted_iota(jnp.int32, (n, n), 1)
    same = (r // CH) == (c // CH)
    return jnp.where(same & ((r <= c) if upper else (r >= c)), 1.0, 0.0).astype(F32)


def _gate_rows(ba, par_ref):
    lane = lax.broadcasted_iota(jnp.int32, ba.shape, 1)
    z = ba + par_ref[1:2, :]
    sp = jnp.maximum(z, 0.0) + jnp.log(1.0 + jnp.exp(-jnp.abs(z)))
    g = -jnp.exp(par_ref[0:1, :]) * sp
    return jnp.where(lane < B_HEADS, _sigmoid(ba), jnp.where(lane < 2 * B_HEADS, g, 0.0)), z


def _prev8(cb):
    return pl.BlockSpec((8, B_W), lambda i: (jnp.maximum(i * (GTR // 8) - 1, 0), cb))


def _next8(cb, nb):
    return pl.BlockSpec((8, B_W), lambda i: (jnp.minimum((i + 1) * (GTR // 8), nb * (GTR // 8) - 1), cb))


def _gdn_pre_fwd(proj, wconv, par, *, name):
    tt = proj.shape[0]

    def body(q_ref, k_ref, v_ref, qh_ref, kh_ref, vh_ref, ba_ref, w_ref, par_ref, qo_ref, ko_ref, vo_ref, aux_ref):
        first = pl.program_id(0) == 0
        for grp, (x_ref, h_ref, o_ref) in enumerate(((q_ref, qh_ref, qo_ref), (k_ref, kh_ref, ko_ref),
                                                     (v_ref, vh_ref, vo_ref))):
            xe = jnp.concatenate([jnp.where(first, 0.0, h_ref[...]), x_ref[...]], axis=0)
            o_ref[...] = _act_fwd(_conv(_shifts(xe, GTR), _taps(w_ref, grp)), grp)
        bg, _ = _gate_rows(ba_ref[...], par_ref)
        lane = lax.broadcasted_iota(jnp.int32, bg.shape, 1)
        aux_ref[...] = jnp.where(lane < B_HEADS, bg, _dot(_chunk_tri(GTR), bg, HI))

    col = lambda off: _rb(GTR, B_W, off // B_W)
    outs = jax.ShapeDtypeStruct((tt, B_W), F32)
    return pl.pallas_call(
        body, grid=(tt // GTR,),
        in_specs=[col(OFF_QB), col(OFF_KB), col(OFF_VB), _prev8(OFF_QB // B_W), _prev8(OFF_KB // B_W),
                  _prev8(OFF_VB // B_W), _rb(GTR, LANE, OFF_BA // LANE), _whole((CONV_K, 3 * B_W)),
                  _whole((8, LANE))],
        out_specs=(_rb(GTR, B_W), _rb(GTR, B_W), _rb(GTR, B_W), _rb(GTR, LANE)),
        out_shape=(outs, outs, outs, jax.ShapeDtypeStruct((tt, LANE), F32)),
        compiler_params=_cp(("parallel",)), name=name)(proj, proj, proj, proj, proj, proj, proj, wconv, par)


def _gdn_pre_bwd(proj, wconv, par, dq, dk, dv, daux, *, name):
    tt = proj.shape[0]
    nb = tt // GTR

    def body(q_ref, k_ref, v_ref, qh_ref, kh_ref, vh_ref, qn_ref, kn_ref, vn_ref, ba_ref, w_ref, par_ref,
             dq_ref, dk_ref, dv_ref, dqn_ref, dkn_ref, dvn_ref, daux_ref, dx_ref, dba_ref, dw_ref, dpar_ref):
        i = pl.program_id(0)
        first, last = i == 0, i == nb - 1

        @pl.when(first)
        def _():
            dw_ref[...] = jnp.zeros_like(dw_ref)
            dpar_ref[...] = jnp.zeros_like(dpar_ref)

        groups = ((q_ref, qh_ref, qn_ref, dq_ref, dqn_ref), (k_ref, kh_ref, kn_ref, dk_ref, dkn_ref),
                  (v_ref, vh_ref, vn_ref, dv_ref, dvn_ref))
        for grp, (x_ref, h_ref, xn_ref, d_ref, dn_ref) in enumerate(groups):
            taps = _taps(w_ref, grp)
            xe = jnp.concatenate([jnp.where(first, 0.0, h_ref[...]), x_ref[...]], axis=0)
            sh = _shifts(xe, GTR)
            dc = _act_bwd(_conv(sh, taps), d_ref[...], grp)
            xe_n = jnp.concatenate([x_ref[GTR - 8:GTR, :], xn_ref[...]], axis=0)
            dcn = _act_bwd(_conv(_shifts(xe_n, 8), taps), dn_ref[...], grp)
            dce = jnp.concatenate([dc, jnp.where(last, 0.0, dcn)], axis=0)
            dx = taps[CONV_K - 1] * dc
            dw_ref[CONV_K - 1:CONV_K, grp * B_W:(grp + 1) * B_W] += _colsum(dc * sh[0])
            for s in range(1, CONV_K):
                dx = dx + taps[CONV_K - 1 - s] * pltpu.roll(dce, GTR + 8 - s, 0)[0:GTR]
                dw_ref[CONV_K - 1 - s:CONV_K - s, grp * B_W:(grp + 1) * B_W] += _colsum(dc * sh[s])
            dx_ref[:, grp * B_W:(grp + 1) * B_W] = dx.astype(BF16)
        ba = ba_ref[...]
        lane = lax.broadcasted_iota(jnp.int32, ba.shape, 1)
        bg, z = _gate_rows(ba, par_ref)
        daux_v = daux_ref[...]
        dg = _dot(_chunk_tri(GTR, upper=True), daux_v, HI)
        dgl = jnp.where((lane >= B_HEADS) & (lane < 2 * B_HEADS), dg, 0.0)
        da = dgl * (-jnp.exp(par_ref[0:1, :])) * _sigmoid(z)
        dbr = jnp.where(lane < B_HEADS, daux_v * bg * (1.0 - bg), 0.0)
        dba_ref[...] = (dbr + da).astype(BF16)
        dpar_ref[0:1, :] += _colsum(dgl * bg)
        dpar_ref[1:2, :] += _colsum(da)

    col = lambda off: _rb(GTR, B_W, off // B_W)
    row, rowl = _rb(GTR, B_W), _rb(GTR, LANE)
    return pl.pallas_call(
        body, grid=(nb,),
        in_specs=[col(OFF_QB), col(OFF_KB), col(OFF_VB),
                  _prev8(OFF_QB // B_W), _prev8(OFF_KB // B_W), _prev8(OFF_VB // B_W),
                  _next8(OFF_QB // B_W, nb), _next8(OFF_KB // B_W, nb), _next8(OFF_VB // B_W, nb),
                  _rb(GTR, LANE, OFF_BA // LANE), _whole((CONV_K, 3 * B_W)), _whole((8, LANE)),
                  row, row, row, _next8(0, nb), _next8(0, nb), _next8(0, nb), rowl],
        out_specs=(_rb(GTR, 3 * B_W), rowl, _whole((8, 3 * B_W)), _whole((8, LANE))),
        out_shape=(jax.ShapeDtypeStruct((tt, 3 * B_W), BF16), jax.ShapeDtypeStruct((tt, LANE), BF16),
                   jax.ShapeDtypeStruct((8, 3 * B_W), F32), jax.ShapeDtypeStruct((8, LANE), F32)),
        compiler_params=_cp(("arbitrary",)), name=name)(
            proj, proj, proj, proj, proj, proj, proj, proj, proj, proj, wconv, par, dq, dk, dv, dq, dk, dv, daux)


def _col(x, j):
    lane = lax.broadcasted_iota(jnp.int32, x.shape, 1)
    return jnp.sum(jnp.where(lane == j, x, 0.0), axis=-1, keepdims=True)


def _split(x):
    hi = x.astype(BF16)
    return hi, (x - hi.astype(F32)).astype(BF16)


def _dot3(a, b, tn=False):
    dot = _dot_tn if tn else _dot
    (ah, al), (bh, bl) = _split(a), _split(b)
    return dot(ah, bh) + (dot(ah, bl) + dot(al, bh))


def _chunk_masks():
    r = lax.broadcasted_iota(jnp.int32, (CH, CH), 0)
    c = lax.broadcasted_iota(jnp.int32, (CH, CH), 1)
    return r > c, r >= c


def _gc_rows(aux, nc):
    t = jnp.transpose(aux[:, B_HEADS:2 * B_HEADS].reshape(nc, CH, B_HEADS), (0, 2, 1))
    return jnp.concatenate([t, jnp.zeros_like(t)], axis=1).reshape(nc * 8, CH)


_CHUNK8 = lambda width, n=1: pl.BlockSpec((8 * n, width), lambda i: (i, 0))
_CHUNK4 = lambda a, b, n=1: pl.BlockSpec((B_HEADS * n, a, b), lambda i: (i, 0, 0))
NCH = 2


def _per_chunk(body, rows):
    def wrapped(*refs):
        for ci in range(NCH):
            body(*[r.at[pl.ds(ci * n, n)] for r, n in zip(refs, rows)])
    return wrapped


def _gdn_lower(k, aux, auxt, *, name):
    tt = k.shape[0]

    def body(k_ref, aux_ref, auxt_ref, l_ref):
        aux_v = aux_ref[...]
        strict, _ = _chunk_masks()
        for hd in range(B_HEADS):
            kh = k_ref[:, hd * B_DH:(hd + 1) * B_DH].astype(BF16)
            diff = _col(aux_v, B_HEADS + hd) - auxt_ref[hd:hd + 1, :]
            dec = jnp.exp(jnp.where(strict, diff, NEG))
            l_ref[hd] = _col(aux_v, hd) * _dot_nt(kh, kh) * dec

    return pl.pallas_call(
        _per_chunk(body, (CH, CH, 8, B_HEADS)), grid=(tt // CH // NCH,),
        in_specs=[_rb(NCH * CH, B_W), _rb(NCH * CH, LANE), _CHUNK8(CH, NCH)],
        out_specs=_CHUNK4(CH, CH, NCH),
        out_shape=jax.ShapeDtypeStruct((tt // CH * B_HEADS, CH, CH), F32),
        compiler_params=_cp(("parallel",)), name=name)(k, aux, auxt)


def _tri_inverse(lt, *, name):
    nb = lt.shape[2]

    def body(l_ref, t_ref):
        rowid = lax.broadcasted_iota(jnp.int32, (CH, nb), 0)

        def outer(i, carry):
            def inner(j, acc):
                return acc + l_ref[i, pl.ds(j, 1), :] * t_ref[j]

            acc = lax.fori_loop(0, i, inner, jnp.zeros((CH, nb), F32))
            t_ref[i] = jnp.where(rowid == i, 1.0, 0.0) - acc
            return carry

        lax.fori_loop(0, CH, outer, 0)

    return pl.pallas_call(body, out_shape=jax.ShapeDtypeStruct(lt.shape, F32),
                          in_specs=[pl.BlockSpec(memory_space=pltpu.VMEM)],
                          out_specs=pl.BlockSpec(memory_space=pltpu.VMEM),
                          compiler_params=_cp(), name=name)(lt)


def _gdn_gates(aux_v, aux_last, auxt_ref, hd):
    _, incl = _chunk_masks()
    beta = _col(aux_v, hd)
    gc = _col(aux_v, B_HEADS + hd)
    gl = _col(aux_last, B_HEADS + hd)
    dec = jnp.exp(jnp.where(incl, gc - auxt_ref[hd:hd + 1, :], NEG))
    return beta, gc, gl, jnp.exp(gc), dec


def _gdn_intra(q, k, v, aux, auxt, tinv, *, name):
    tt = q.shape[0]
    nc = tt // CH

    def body(q_ref, k_ref, v_ref, aux_ref, auxt_ref, t_ref, u0_ref, w_ref, qd_ref, kd_ref, qk_ref, gle_ref):
        aux_v = aux_ref[...]
        aux_last = aux_ref[CH - 1:CH, :]
        lane8 = lax.broadcasted_iota(jnp.int32, (8, LANE), 1)
        gle = jnp.zeros((8, LANE), F32)
        heads = range(B_HEADS)
        sls = [slice(hd * B_DH, (hd + 1) * B_DH) for hd in heads]
        gates = [_gdn_gates(aux_v, aux_last, auxt_ref, hd) for hd in heads]
        qk0 = [_dot_nt(q_ref[:, sls[hd]].astype(BF16), k_ref[:, sls[hd]].astype(BF16)) for hd in heads]
        u0 = [_dot3(t_ref[hd], v_ref[:, sls[hd]] * gates[hd][0]) for hd in heads]
        wk = [_dot3(t_ref[hd], k_ref[:, sls[hd]] * (gates[hd][0] * gates[hd][3])) for hd in heads]
        for hd in heads:
            sl = sls[hd]
            beta, gc, gl, egc, dec = gates[hd]
            qk_ref[hd] = (qk0[hd] * dec).astype(BF16)
            u0_ref[:, sl] = u0[hd]
            w_ref[:, sl] = wk[hd].astype(BF16)
            qd_ref[:, sl] = (q_ref[:, sl] * egc).astype(BF16)
            kd_ref[:, sl] = (k_ref[:, sl] * jnp.exp(gl - gc)).astype(BF16)
            gle = gle + jnp.where(lane8 == hd, jnp.exp(gl), 0.0)
        gle_ref[...] = gle

    row = _rb(NCH * CH, B_W)
    half = jax.ShapeDtypeStruct((tt, B_W), BF16)
    return pl.pallas_call(
        _per_chunk(body, (CH, CH, CH, CH, 8, B_HEADS, CH, CH, CH, CH, B_HEADS, 8)), grid=(nc // NCH,),
        in_specs=[row, row, row, _rb(NCH * CH, LANE), _CHUNK8(CH, NCH), _CHUNK4(CH, CH, NCH)],
        out_specs=(row, row, row, row, _CHUNK4(CH, CH, NCH), _CHUNK8(LANE, NCH)),
        out_shape=(jax.ShapeDtypeStruct((tt, B_W), F32), half, half, half,
                   jax.ShapeDtypeStruct((nc * B_HEADS, CH, CH), BF16), jax.ShapeDtypeStruct((nc * 8, LANE), F32)),
        compiler_params=_cp(("parallel",)), name=name)(q, k, v, aux, auxt, tinv)


def _gdn_scan_fwd(u0, w, qd, kd, qk, gle, *, name):
    tt = u0.shape[0]
    nc = tt // CH

    def body(u0_ref, w_ref, qd_ref, kd_ref, qk_ref, gle_ref, o_ref, ss_ref, u_ref, s_ref):
        @pl.when(pl.program_id(0) == 0)
        def _():
            s_ref[...] = jnp.zeros_like(s_ref)

        gle = gle_ref[0:1, :]
        heads = range(B_HEADS)
        sls = [slice(hd * B_DH, (hd + 1) * B_DH) for hd in heads]
        st = [s_ref[hd] for hd in heads]
        sb = [t.astype(BF16) for t in st]
        ws = [_dot(w_ref[:, sls[hd]], sb[hd]) for hd in heads]
        qs = [_dot(qd_ref[:, sls[hd]], sb[hd]) for hd in heads]
        ub = [(u0_ref[:, sls[hd]] - ws[hd]).astype(BF16) for hd in heads]
        ku = [_dot_tn(kd_ref[:, sls[hd]], ub[hd]) for hd in heads]
        qu = [_dot(qk_ref[hd], ub[hd]) for hd in heads]
        for hd in heads:
            ss_ref[hd] = st[hd]
            u_ref[:, sls[hd]] = ub[hd]
            o_ref[:, sls[hd]] = qs[hd] + qu[hd]
            s_ref[hd] = st[hd] * _col(gle, hd) + ku[hd]

    row = _rb(CH, B_W)
    return pl.pallas_call(
        body, grid=(nc,), in_specs=[row, row, row, row, _CHUNK4(CH, CH), _CHUNK8(LANE)],
        out_specs=(row, _CHUNK4(B_DH, B_DH), row),
        out_shape=(jax.ShapeDtypeStruct((tt, B_W), F32), jax.ShapeDtypeStruct((nc * B_HEADS, B_DH, B_DH), F32),
                   jax.ShapeDtypeStruct((tt, B_W), BF16)),
        scratch_shapes=[pltpu.VMEM((B_HEADS, B_DH, B_DH), F32)],
        compiler_params=_cp(("arbitrary",)), name=name)(u0, w, qd, kd, qk, gle)


def _gdn_scan_bwd(w, qd, kd, qk, gle, do, *, name):
    tt = w.shape[0]
    nc = tt // CH

    def body(w_ref, qd_ref, kd_ref, qk_ref, gle_ref, do_ref, du_ref, dss_ref, ds_ref):
        @pl.when(pl.program_id(0) == 0)
        def _():
            ds_ref[...] = jnp.zeros_like(ds_ref)

        gle = gle_ref[0:1, :]
        heads = range(B_HEADS)
        sls = [slice(hd * B_DH, (hd + 1) * B_DH) for hd in heads]
        dst = [ds_ref[hd] for hd in heads]
        dob = [do_ref[:, sls[hd]].astype(BF16) for hd in heads]
        kds = [_dot(kd_ref[:, sls[hd]], dst[hd].astype(BF16)) for hd in heads]
        qkd = [_dot_tn(qk_ref[hd], dob[hd]) for hd in heads]
        qdd = [_dot_tn(qd_ref[:, sls[hd]], dob[hd]) for hd in heads]
        du = [qkd[hd] + kds[hd] for hd in heads]
        wdu = [_dot_tn(w_ref[:, sls[hd]], du[hd].astype(BF16)) for hd in heads]
        for hd in heads:
            dss_ref[hd] = dst[hd]
            du_ref[:, sls[hd]] = du[hd]
            ds_ref[hd] = qdd[hd] + _col(gle, hd) * dst[hd] - wdu[hd]

    rev = lambda width: pl.BlockSpec((CH, width), lambda i: (nc - 1 - i, 0))
    rev4 = lambda a, b: pl.BlockSpec((B_HEADS, a, b), lambda i: (nc - 1 - i, 0, 0))
    return pl.pallas_call(
        body, grid=(nc,),
        in_specs=[rev(B_W), rev(B_W), rev(B_W), rev4(CH, CH), pl.BlockSpec((8, LANE), lambda i: (nc - 1 - i, 0)), rev(B_W)],
        out_specs=(rev(B_W), rev4(B_DH, B_DH)),
        out_shape=(jax.ShapeDtypeStruct((tt, B_W), F32), jax.ShapeDtypeStruct((nc * B_HEADS, B_DH, B_DH), F32)),
        scratch_shapes=[pltpu.VMEM((B_HEADS, B_DH, B_DH), F32)],
        compiler_params=_cp(("arbitrary",)), name=name)(w, qd, kd, qk, gle, do)


def _gdn_bwd(q, k, v, aux, auxt, tinv, u0, w, u, ss, dss, du, do, *, name):
    tt = q.shape[0]
    nc = tt // CH

    def body(q_ref, k_ref, v_ref, aux_ref, auxt_ref, t_ref, u0_ref, w_ref, u_ref, ss_ref, dss_ref, du_ref, do_ref,
             dq_ref, dk_ref, dv_ref, daux_ref):
        aux_v = aux_ref[...]
        aux_last = aux_ref[CH - 1:CH, :]
        lane = lax.broadcasted_iota(jnp.int32, (CH, LANE), 1)
        rowi = lax.broadcasted_iota(jnp.int32, (CH, 1), 0)
        strict, incl = _chunk_masks()
        daux = jnp.zeros((CH, LANE), F32)
        heads = range(B_HEADS)
        sls = [slice(hd * B_DH, (hd + 1) * B_DH) for hd in heads]
        gates = [_gdn_gates(aux_v, aux_last, auxt_ref, hd) for hd in heads]
        kbs = [k_ref[:, sl].astype(BF16) for sl in sls]
        qbs = [q_ref[:, sl].astype(BF16) for sl in sls]
        sbs = [ss_ref[hd].astype(BF16) for hd in heads]
        dsbs = [dss_ref[hd].astype(BF16) for hd in heads]
        dobs = [do_ref[:, sl].astype(BF16) for sl in sls]
        kks = [_dot_nt(kbs[hd], kbs[hd]) for hd in heads]
        qk0s = [_dot_nt(qbs[hd], kbs[hd]) for hd in heads]
        dq_decs = [_dot_nt(dobs[hd], sbs[hd]) for hd in heads]
        dqks = [_dot_nt(dobs[hd], u_ref[:, sls[hd]]) for hd in heads]
        dk_decs = [_dot_nt(u_ref[:, sls[hd]], dsbs[hd]) for hd in heads]
        dws = [-_dot_nt(du_ref[:, sls[hd]].astype(BF16), sbs[hd]) for hd in heads]
        drvs = [_dot3(t_ref[hd], du_ref[:, sls[hd]], tn=True) for hd in heads]
        drks = [_dot3(t_ref[hd], dws[hd], tn=True) for hd in heads]
        dls = [-(_dot_nt(drvs[hd].astype(BF16), u0_ref[:, sls[hd]].astype(BF16))
                 + _dot_nt(drks[hd].astype(BF16), w_ref[:, sls[hd]])) for hd in heads]
        for hd in heads:
            sl = sls[hd]
            qh, kh, vh = q_ref[:, sl], k_ref[:, sl], v_ref[:, sl]
            beta, gc, gl, egc, dec = gates[hd]
            ekd, eg_last = jnp.exp(gl - gc), jnp.exp(gl)
            kb, qb, kk, qk0 = kbs[hd], qbs[hd], kks[hd], qk0s[hd]
            st, dst = ss_ref[hd], dss_ref[hd]
            dq_dec, dk_dec = dq_decs[hd], dk_decs[hd]
            dqk = jnp.where(incl, dqks[hd], 0.0)
            dgl = jnp.sum(jnp.sum(st * dst, axis=-1, keepdims=True), axis=0, keepdims=True) * eg_last
            drv, drk = drvs[hd], drks[hd]
            dl = jnp.where(strict, dls[hd], 0.0)
            dv_ref[:, sl] = drv * beta
            rk = jnp.sum(drk * kh, axis=-1, keepdims=True)
            dbeta = jnp.sum(drv * vh, axis=-1, keepdims=True) + rk * egc
            dgc = rk * beta * egc
            dk = drk * (beta * egc)
            ldec = dl * dec
            dbeta = dbeta + jnp.sum(ldec * kk, axis=-1, keepdims=True)
            dkk = (ldec * beta).astype(BF16)
            dqk0 = (dqk * dec).astype(BF16)
            ddec = ldec * beta * kk + dqk * (qk0 * dec)
            dq = _dot(dqk0, kb) + dq_dec * egc
            dk = dk + _dot_tn(dqk0, qb) + _dot(dkk, kb) + _dot_tn(dkk, kb) + dk_dec * ekd
            dgc = dgc + jnp.sum(ddec, axis=-1, keepdims=True) - _col_from_rowsum(ddec)
            dgc = dgc + jnp.sum(dq_dec * qh, axis=-1, keepdims=True) * egc
            kd = jnp.sum(dk_dec * kh, axis=-1, keepdims=True) * ekd
            dgc = dgc - kd
            dgc = dgc + jnp.where(rowi == CH - 1, jnp.sum(kd, axis=0, keepdims=True) + dgl, 0.0)
            dq_ref[:, sl] = dq
            dk_ref[:, sl] = dk
            daux = daux + jnp.where(lane == hd, dbeta, 0.0) + jnp.where(lane == B_HEADS + hd, dgc, 0.0)
        daux_ref[...] = daux

    row = _rb(NCH * CH, B_W)
    outs = jax.ShapeDtypeStruct((tt, B_W), F32)
    return pl.pallas_call(
        _per_chunk(body, (CH, CH, CH, CH, 8, B_HEADS, CH, CH, CH, B_HEADS, B_HEADS, CH, CH, CH, CH, CH, CH)),
        grid=(nc // NCH,),
        in_specs=[row, row, row, _rb(NCH * CH, LANE), _CHUNK8(CH, NCH), _CHUNK4(CH, CH, NCH), row, row, row,
                  _CHUNK4(B_DH, B_DH, NCH), _CHUNK4(B_DH, B_DH, NCH), row, row],
        out_specs=(row, row, row, _rb(NCH * CH, LANE)),
        out_shape=(outs, outs, outs, jax.ShapeDtypeStruct((tt, LANE), F32)),
        compiler_params=_cp(("parallel",)), name=name)(q, k, v, aux, auxt, tinv, u0, w, u, ss, dss, du, do)


def _col_from_rowsum(m):
    hi, lo = _split(m)
    ones = jnp.ones((CH, LANE), BF16)
    return (_dot_tn(hi, ones) + _dot_tn(lo, ones))[:, 0:1]


def _gdn_post_fwd(o, proj, gn, *, name, tr=256):
    tt = o.shape[0]

    def body(o_ref, z_ref, g_ref, y_ref):
        for hd in range(B_HEADS):
            sl = slice(hd * B_DH, (hd + 1) * B_DH)
            oh = o_ref[:, sl]
            r = lax.rsqrt(jnp.mean(oh * oh, axis=-1, keepdims=True) + EPS)
            y_ref[:, sl] = (oh * r * g_ref[...] * _silu(z_ref[:, sl])).astype(BF16)

    return pl.pallas_call(body, grid=(tt // tr,), in_specs=[_rb(tr, B_W), _rb(tr, B_W, OFF_ZB // B_W), _whole((1, B_DH))],
                          out_specs=_rb(tr, B_W), out_shape=jax.ShapeDtypeStruct((tt, B_W), BF16),
                          compiler_params=_cp(("parallel",)), name=name)(o, proj, gn)


def _gdn_post_bwd(o, proj, gn, dy, *, name, tr=256):
    tt = o.shape[0]

    def body(o_ref, z_ref, g_ref, dy_ref, do_ref, dz_ref, dg_ref):
        @pl.when(pl.program_id(0) == 0)
        def _():
            dg_ref[...] = jnp.zeros_like(dg_ref)

        g = g_ref[...]
        for hd in range(B_HEADS):
            sl = slice(hd * B_DH, (hd + 1) * B_DH)
            oh, zh, dyh = o_ref[:, sl], z_ref[:, sl], dy_ref[:, sl]
            r = lax.rsqrt(jnp.mean(oh * oh, axis=-1, keepdims=True) + EPS)
            a = oh * r
            s = _silu(zh)
            da = dyh * g * s
            dg_ref[0:1, :] += _colsum(dyh * a * s)
            dz_ref[:, sl] = (dyh * a * g * _dsilu(zh)).astype(BF16)
            do_ref[:, sl] = r * (da - a * jnp.mean(da * a, axis=-1, keepdims=True))

    return pl.pallas_call(
        body, grid=(tt // tr,), in_specs=[_rb(tr, B_W), _rb(tr, B_W, OFF_ZB // B_W), _whole((1, B_DH)), _rb(tr, B_W)],
        out_specs=(_rb(tr, B_W), _rb(tr, B_W), _whole((8, B_DH))),
        out_shape=(jax.ShapeDtypeStruct((tt, B_W), F32), jax.ShapeDtypeStruct((tt, B_W), BF16),
                   jax.ShapeDtypeStruct((8, B_DH), F32)),
        compiler_params=_cp(("arbitrary",)), name=name)(o, proj, gn, dy)


def _adamw(parts, w, m, v, own=None, sel=None, *, name, tr=256):
    npart, nl, r, c = parts.shape
    tr = max([t for t in range(8, min(r, tr) + 1, 8) if r % t == 0], default=r)
    tc = c if tr < r or r <= 256 or c % 256 else 256
    c1, c2 = 1.0 - ADAM_B1 ** ADAM_STEP, 1.0 - ADAM_B2 ** ADAM_STEP

    def body(*refs):
        if own is None:
            p_ref, w_ref, m_ref, v_ref, g_ref, d_ref, mo_ref, vo_ref = refs
            part = lambda i: p_ref[i].astype(F32)
        else:
            p_ref, w_ref, m_ref, v_ref, own_ref, sel_ref, g_ref, d_ref, mo_ref, vo_ref = refs
            part = lambda i: jnp.where(sel_ref[i:i + 1, 0:1] > 0.5, own_ref[...].astype(F32), p_ref[i].astype(F32))
        g = part(0)
        for i in range(1, npart):
            g = g + part(i)
        mn = ADAM_B1 * m_ref[...] + (1.0 - ADAM_B1) * g
        vn = ADAM_B2 * v_ref[...] + (1.0 - ADAM_B2) * (g * g)
        g_ref[...] = g
        mo_ref[...] = mn
        vo_ref[...] = vn
        d_ref[...] = -ADAM_LR * ((mn / c1) / (jnp.sqrt(vn / c2) + ADAM_EPS) + ADAM_WD * w_ref[...])

    row = pl.BlockSpec((None, tr, tc), lambda l, i, j: (l, i, j))
    out = jax.ShapeDtypeStruct((nl, r, c), F32)
    ins, in_specs = [parts, w, m, v], [pl.BlockSpec((npart, None, tr, tc), lambda l, i, j: (0, l, i, j)), row, row, row]
    if own is not None:
        ins += [own, sel]
        in_specs += [row, pl.BlockSpec((N_DEV, LANE), lambda l, i, j: (0, 0))]
    return pl.pallas_call(body, grid=(nl, r // tr, c // tc), in_specs=in_specs, out_specs=(row, row, row, row),
                          out_shape=(out, out, out, out), compiler_params=_cp(("parallel", "parallel", "parallel")),
                          name=name)(*ins)


def _peer(k):
    x, y, c = lax.axis_index("x"), lax.axis_index("y"), lax.axis_index("c")
    return ((1 - x) if k & 4 else x, (1 - y) if k & 2 else y, (1 - c) if k & 1 else c)


def _my_index():
    return 4 * lax.axis_index("x") + 2 * lax.axis_index("y") + lax.axis_index("c")


def _index_of(p):
    return 4 * p[0] + 2 * p[1] + p[2]


def _all_gather(xs, *, name):
    n = len(xs)

    def body(*refs):
        x_refs, o_refs = refs[:n], refs[n:2 * n]
        send, recv, loc = refs[2 * n:]
        me = _my_index()
        copies = []
        for a in range(n):
            cp = pltpu.make_async_copy(x_refs[a], o_refs[a].at[me], loc.at[a])
            cp.start()
            copies.append(cp)
        rdmas = []
        for a in range(n):
            for k in range(1, N_DEV):
                r = pltpu.make_async_remote_copy(
                    src_ref=x_refs[a], dst_ref=o_refs[a].at[me], send_sem=send.at[a, k - 1], recv_sem=recv.at[a, k - 1],
                    device_id=_peer(k), device_id_type=pl.DeviceIdType.MESH)
                r.start()
                rdmas.append(r)
        for a in range(n):
            for k in range(1, N_DEV):
                pltpu.make_async_remote_copy(
                    src_ref=x_refs[a], dst_ref=o_refs[a].at[_index_of(_peer(k))], send_sem=send.at[a, k - 1],
                    recv_sem=recv.at[a, k - 1], device_id=_peer(k), device_id_type=pl.DeviceIdType.MESH).wait_recv()
        for r in rdmas:
            r.wait_send()
        for cp in copies:
            cp.wait()

    any_spec = pl.BlockSpec(memory_space=pl.ANY)
    return pl.pallas_call(
        body, in_specs=[any_spec] * n, out_specs=tuple([any_spec] * n),
        out_shape=tuple(jax.ShapeDtypeStruct((N_DEV,) + x.shape, x.dtype) for x in xs),
        scratch_shapes=[pltpu.SemaphoreType.DMA((n, N_DEV - 1)), pltpu.SemaphoreType.DMA((n, N_DEV - 1)),
                        pltpu.SemaphoreType.DMA((n,))],
        name=name)(*xs)


_HBM = pl.BlockSpec(memory_space=pltpu.HBM)
_SEM = pl.BlockSpec(memory_space=pltpu.SEMAPHORE)
_EFFECT = pltpu.SideEffectType.DATAFLOW_SIDE_EFFECTING


def _split_copy(src_ref, land_ref, send, recv, a, k, scatter, slot, sending):
    me, peer = _my_index(), _index_of(_peer(k))
    src = src_ref.at[peer if sending else me] if scatter else src_ref
    land = land_ref.at[me if sending else peer]
    if slot is not None:
        land = land.at[slot]
    sem = a * (N_DEV - 1) + k - 1
    return pltpu.make_async_remote_copy(src_ref=src, dst_ref=land, send_sem=send.at[sem], recv_sem=recv.at[sem],
                                        device_id=_peer(k), device_id_type=pl.DeviceIdType.MESH)


def _exchange_start(srcs, lands, after, *, scatter, slot=None, name):
    n = len(srcs)

    def body(*refs):
        src_refs, land_refs = refs[:n], refs[n:2 * n]
        send, recv, token = refs[2 * n + 1], refs[2 * n + 2], refs[-1]
        for a in range(n):
            for k in range(1, N_DEV):
                _split_copy(src_refs[a], land_refs[a], send, recv, a, k, scatter, slot, True).start()
        token[...] = jnp.zeros_like(token)

    hbm = lambda t: pltpu.HBM(t.shape, t.dtype)
    sems = pltpu.SemaphoreType.DMA((n * (N_DEV - 1),))
    out = pl.pallas_call(
        body, name=name,
        out_shape=(sems, sems, *[hbm(t) for t in srcs], *[hbm(t) for t in lands], jax.ShapeDtypeStruct((8, LANE), F32)),
        in_specs=[_HBM] * (2 * n) + [pl.BlockSpec(memory_space=pl.ANY)],
        out_specs=(_SEM, _SEM, *[_HBM] * (2 * n), pl.BlockSpec(memory_space=pltpu.VMEM)),
        input_output_aliases={i: 2 + i for i in range(2 * n)},
        compiler_params=pltpu.CompilerParams(has_side_effects=_EFFECT),
    )(*[pltpu.with_memory_space_constraint(t, pltpu.HBM) for t in (*srcs, *lands)], after)
    return out[0], out[1], out[2:2 + n], out[2 + n:2 + 2 * n], out[-1]


def _exchange_wait(send, recv, srcs, lands, after, *, scatter, slot=None, name):
    n = len(srcs)

    def body(*refs):
        src_refs, land_refs = refs[:n], refs[n:2 * n]
        send_ref, recv_ref = refs[2 * n], refs[2 * n + 1]
        for a in range(n):
            for k in range(1, N_DEV):
                _split_copy(src_refs[a], land_refs[a], send_ref, recv_ref, a, k, scatter, slot, True).wait_send()
                _split_copy(src_refs[a], land_refs[a], send_ref, recv_ref, a, k, scatter, slot, False).wait_recv()

    hbm = lambda t: pltpu.HBM(t.shape, t.dtype)
    out = pl.pallas_call(
        body, name=name, out_shape=(*[hbm(t) for t in srcs], *[hbm(t) for t in lands]),
        in_specs=[_HBM] * (2 * n) + [_SEM, _SEM, pl.BlockSpec(memory_space=pl.ANY)],
        out_specs=tuple([_HBM] * (2 * n)), input_output_aliases={i: i for i in range(2 * n)},
        compiler_params=pltpu.CompilerParams(has_side_effects=_EFFECT),
    )(*srcs, *lands, send, recv, after)
    return out[:n], out[n:]


def _win_to_mine(wt):
    pad = jnp.zeros((IN_PAD - IN_DIM,) + wt.shape[1:], wt.dtype)
    return jnp.concatenate([wt[3592:5640], wt[0:3584], wt[3584:3592], pad], axis=0)


def _win_from_mine(gt):
    return jnp.concatenate([gt[2048:5632], gt[5632:5640], gt[0:2048]], axis=0)


def _pad_rows(a, mult=8):
    r = (-a.shape[0]) % mult
    return a if r == 0 else jnp.concatenate([a, jnp.zeros((r,) + a.shape[1:], a.dtype)], axis=0)


def _lanes(vec, start):
    return jnp.zeros((1, LANE), F32).at[0, start:start + vec.shape[0]].set(vec)


def _small_spec(depth):
    return (("b_ada", (depth, 6 * D)), ("norm1_g", (depth, D)), ("norm2_g", (depth, D)),
            ("rel_table", (depth, A_HEADS, 2 * A_MAX_REL + 1)), ("a_log", (depth, B_HEADS)),
            ("dt_bias", (depth, B_HEADS)), ("gdn_norm_g", (depth, B_DH)), ("final_g", (D,)))


def _pack_small(d, extra, depth):
    spec = _small_spec(depth)
    rows = -(-(sum(math.prod(s) for _, s in spec) + 1) // (8 * LANE)) * 8
    flat = jnp.concatenate([d[n].reshape(-1).astype(F32) for n, _ in spec] + [extra.reshape(-1)])
    flat = jnp.concatenate([flat, jnp.zeros((rows * LANE - flat.shape[0],), F32)])
    return flat.reshape(rows, LANE)


def _unpack_small(p, depth):
    flat = p.reshape(-1)
    out, off = {}, 0
    for n, s in _small_spec(depth):
        sz = math.prod(s)
        out[n] = flat[off:off + sz].reshape(s)
        off += sz
    return out, flat[off]


def kernel(x, c, w_ada, b_ada, norm1_g, norm2_g, w_in, rel_table, w_conv, a_log, dt_bias, gdn_norm_g, w_branch_a, w_branch_b, w_out, w_ffn_in, w_ffn_out, final_g, loss_target, m_w_ada, m_b_ada, m_norm1_g, m_norm2_g, m_w_in, m_rel_table, m_w_conv, m_a_log, m_dt_bias, m_gdn_norm_g, m_w_branch_a, m_w_branch_b, m_w_out, m_w_ffn_in, m_w_ffn_out, m_final_g, v_w_ada, v_b_ada, v_norm1_g, v_norm2_g, v_w_in, v_rel_table, v_w_conv, v_a_log, v_dt_bias, v_gdn_norm_g, v_w_branch_a, v_w_branch_b, v_w_out, v_w_ffn_in, v_w_ffn_out, v_final_g):
    tt = x.shape[1]
    x0 = x[0]
    tgt = loss_target[0]
    me = _my_index()
    depth = w_in.shape[0]

    tr_ = lambda t: jnp.transpose(t, (0, 2, 1))
    shards = [tr_(w_in).astype(BF16), w_branch_a.astype(BF16), w_branch_b.astype(BF16), w_out.astype(BF16),
              tr_(w_ffn_in).astype(BF16), w_ffn_out.astype(BF16), w_conv]
    names = ("win", "wa", "wb", "wout", "wfi", "wfo", "wconv")
    early, late, every = (0, 6), (1, 2, 3, 4, 5), tuple(range(7))
    first = _all_gather([shards[i][0] for i in early] + [_pad_rows(c)], name="gather_first")
    c_all = first[-1][:, 0, :]
    is_me = lax.broadcasted_iota(jnp.int32, (N_DEV, 1, 1), 0) == me

    def unpack(idx, g):
        cols = lambda t: jnp.transpose(t, (1, 0, 2)).reshape(t.shape[1], N_DEV * t.shape[2])
        rows = lambda t: t.reshape(N_DEV * t.shape[1], t.shape[2])
        how = (lambda t: _win_to_mine(rows(t)), cols, cols, rows, rows, rows, cols)
        return {names[i]: how[i](t) for i, t in zip(idx, g)}

    def gather_start(l, idx, after, tag=""):
        srcs = [shards[i][l] for i in idx]
        lands = [lax.empty((N_DEV,) + t.shape, t.dtype) for t in srcs]
        return _exchange_start(srcs, lands, after, scatter=False, name=f"gather_start_{l}{tag}")

    def gather_wait(l, idx, pending, after, tag=""):
        send, recv, srcs, lands, _ = pending
        srcs, lands = _exchange_wait(send, recv, srcs, lands, after, scatter=False, name=f"gather_wait_{l}{tag}")
        return unpack(idx, [jnp.where(is_me, t[None], g) for g, t in zip(lands, srcs)])

    weights = [unpack(early, first[:-1])] + [None] * (depth - 1)
    pending0 = gather_start(0, late, first[-1], "_rest")
    pending = gather_start(1, every, pending0[-1]) if depth > 1 else None
    cond = c_all * (1.0 / (1.0 + jnp.exp(-c_all)))
    cond = _pad_rows(cond, 16)

    mod_cols = jnp.stack([_mm(cond, w_ada[l], name="mod_mm")[:N_DEV] for l in range(depth)])
    (g_mod,) = _all_gather([mod_cols], name="gather_mod")
    mod_all = jnp.transpose(g_mod, (1, 2, 0, 3)).reshape(depth, N_DEV, 6 * D)
    mod = lax.dynamic_index_in_dim(mod_all, me, axis=1, keepdims=False) + b_ada
    mods = mod.reshape(depth, 6, 1, D)

    n1g, n2g = norm1_g.reshape(depth, 1, D), norm2_g.reshape(depth, 1, D)
    gng = gdn_norm_g.reshape(depth, 1, B_DH)
    fg = final_g.reshape(1, D)

    saved = []
    tok = (pending if pending is not None else pending0)[-1][0, 0]
    xin, h1 = _adaln_fwd(x0, n1g[0], mods[0, 1] + tok, mods[0, 0], name="adaln1_first")
    for l in range(depth):
        sh1, sc1, gt1, sh2, sc2, gt2 = (mods[l, i] for i in range(6))
        wl = weights[l]
        proj = _mm(h1, wl["win"], tb=True, name="proj_mm", tn=1152)
        kpad, vpad = _kv_pad(proj, name="kv_pad")
        diag, bias_vjp = jax.vjp(_bias_diagonals, rel_table[l])
        bias = _bias_fwd(diag, name="bias_fwd")
        ya, lse = _attn_fwd(proj, kpad, vpad, bias, name="attn_fwd")
        par = jnp.concatenate([_lanes(a_log[l], B_HEADS), _lanes(dt_bias[l], B_HEADS), jnp.zeros((6, LANE), F32)], axis=0)
        qn, kn, vn, aux = _gdn_pre_fwd(proj, wl["wconv"], par, name="gdn_pre_fwd")
        auxt = _gc_rows(aux, tt // CH)
        lower = _gdn_lower(kn, aux, auxt, name="gdn_lower")
        tinv = jnp.transpose(_tri_inverse(jnp.transpose(lower, (1, 2, 0)), name="gdn_tri_inverse"), (2, 0, 1))
        u0, wg, qd, kd, qk, gle = _gdn_intra(qn, kn, vn, aux, auxt, tinv, name="gdn_intra")
        og, ss, ug = _gdn_scan_fwd(u0, wg, qd, kd, qk, gle, name="gdn_scan_fwd")
        yb = _gdn_post_fwd(og, proj, gng[l], name="gdn_post_fwd")
        if l == 0:
            wl.update(gather_wait(0, late, pending0, yb, "_rest"))
        pa = _mm(ya, wl["wa"], out_dtype=BF16, name="branch_a_mm")
        pb = _mm(yb, wl["wb"], out_dtype=BF16, name="branch_b_mm")
        merged = _merge_fwd(proj, pa, pb, name="merge_fwd")
        t1 = _mm(merged, wl["wout"], name="out_mm")
        x2, h2 = _adaln_fwd(xin, n2g[l], sc2, sh2, t1, gt1, name="adaln2_fwd")
        gu = _mm(h2, wl["wfi"], tb=True, out_dtype=BF16, name="ffn_in_mm", tn=1408)
        act = _swiglu_fwd(gu, name="swiglu_fwd")
        t2 = _mm(act, wl["wfo"], name="ffn_out_mm", tk=1408)
        saved.append(dict(xin=xin, h1=h1, proj=proj, kpad=kpad, vpad=vpad, bias=bias, bias_vjp=bias_vjp, ya=ya, lse=lse,
                          par=par, qn=qn, kn=kn, vn=vn, aux=aux, auxt=auxt, tinv=tinv, ss=ss, og=og, yb=yb, pa=pa, pb=pb,
                          u0=u0, wg=wg, qd=qd, kd=kd, qk=qk, gle=gle, ug=ug,
                          merged=merged, t1=t1, x2=x2, h2=h2, gu=gu, act=act, t2=t2))
        if l + 1 < depth:
            weights[l + 1] = gather_wait(l + 1, every, pending, t2)
            pending = gather_start(l + 2, every, weights[l + 1]["wconv"]) if l + 2 < depth else None
            tok = pending[-1][0, 0] if pending is not None else 0.0
            xin, h1 = _adaln_fwd(x2, n1g[l + 1], mods[l + 1, 1] + tok, mods[l + 1, 0], t2, gt2, name="adaln1_fwd")

    s = saved[-1]
    dx, dt2, st = _loss_head(s["x2"], s["t2"], mods[depth - 1, 5], fg, tgt, name="loss_head")
    loss_part = st[4, 0]
    small_g = {"final_g": st[0]}
    dmod_rows = [None] * depth
    for n in ("norm1_g", "norm2_g", "rel_table", "a_log", "dt_bias", "gdn_norm_g"):
        small_g[n] = [None] * depth
    dgt2 = st[3]
    cols_slabs = lambda g: jnp.transpose(g.reshape(g.shape[0], N_DEV, g.shape[1] // N_DEV), (1, 0, 2))
    rows_slabs = lambda g: g.reshape(N_DEV, g.shape[0] // N_DEV, g.shape[1])
    mix, ffn = (0, 1, 2, 3, 6), (4, 5)
    lands = {kind: [lax.empty((N_DEV,) + shards[i].shape, shards[i].dtype) for i in idx]
             for kind, idx in (("mix", mix), ("ffn", ffn))}
    own = {kind: [None] * depth for kind in lands}
    pending_s = {kind: None for kind in lands}

    def scatter(kind, l, srcs, after):
        if pending_s[kind] is not None:
            done, lands[kind] = _exchange_wait(*pending_s[kind][:4], after, scatter=True, slot=l + 1,
                                               name=f"scatter_wait_{kind}_{l + 1}")
            own[kind][l + 1] = [lax.dynamic_index_in_dim(t, me, 0, keepdims=False) for t in done]
        pending_s[kind] = _exchange_start(srcs, lands[kind], after, scatter=True, slot=l, name=f"scatter_start_{kind}_{l}")
        return pending_s[kind][-1][0, 0]

    for l in reversed(range(depth)):
        s, wl = saved[l], weights[l]
        sh1, sc1, gt1, sh2, sc2, gt2 = (mods[l, i] for i in range(6))
        gw_fo = _mm(s["act"], dt2, ta=True, out_dtype=BF16, name="ffn_out_dw", tm=1408)
        dact = _mm(dt2, wl["wfo"], tb=True, out_dtype=BF16, name="ffn_out_dx", tn=1408)
        dgu = _swiglu_bwd(s["gu"], dact, name="swiglu_bwd")
        gw_fi = _mm(dgu, s["h2"], ta=True, out_dtype=BF16, name="ffn_in_dw", tm=1408)
        sc2 = sc2 + scatter("ffn", l, [rows_slabs(gw_fi), rows_slabs(gw_fo)], gw_fi)
        dh2 = _mm(dgu, wl["wfi"], name="ffn_in_dx", tk=1408)
        dx, dt1, st2 = _adaln_bwd(s["x2"], n2g[l], sc2, sh2, dh2, dx, s["t1"], gt1, name="adaln2_bwd")
        gw_out = _mm(s["merged"], dt1, ta=True, out_dtype=BF16, name="out_dw")
        dmerged = _mm(dt1, wl["wout"], tb=True, name="out_dx")
        dgates, dpa, dpb = _merge_bwd(s["proj"], s["pa"], s["pb"], dmerged, name="merge_bwd")
        gw_a = _mm(s["ya"], dpa, ta=True, out_dtype=BF16, name="branch_a_dw")
        gw_b = _mm(s["yb"], dpb, ta=True, out_dtype=BF16, name="branch_b_dw")
        dya = _mm(dpa, wl["wa"], tb=True, name="branch_a_dx")
        dyb = _mm(dpb, wl["wb"], tb=True, name="branch_b_dx")
        dqa, dka, dva, dbias = _attn_bwd(s["proj"], s["kpad"], s["vpad"], s["bias"], s["ya"], s["lse"], dya,
                                             name="attn_bwd")
        ddiag = jnp.roll(_bias_bwd(dbias, name="bias_bwd"), -(QB - 1), axis=1)
        small_g["rel_table"][l] = s["bias_vjp"](ddiag)[0]
        dog, dz, dgn = _gdn_post_bwd(s["og"], s["proj"], gng[l], dyb, name="gdn_post_bwd")
        small_g["gdn_norm_g"][l] = dgn[0]
        dug, dss = _gdn_scan_bwd(s["wg"], s["qd"], s["kd"], s["qk"], s["gle"], dog, name="gdn_scan_bwd")
        dqn, dkn, dvn, daux = _gdn_bwd(s["qn"], s["kn"], s["vn"], s["aux"], s["auxt"], s["tinv"], s["u0"], s["wg"],
                                       s["ug"], s["ss"], dss, dug, dog, name="gdn_bwd")
        dqkv, dba, dwc, dpar = _gdn_pre_bwd(s["proj"], wl["wconv"], s["par"], dqn, dkn, dvn, daux, name="gdn_pre_bwd")
        small_g["a_log"][l] = dpar[0, B_HEADS:2 * B_HEADS]
        small_g["dt_bias"][l] = dpar[1, B_HEADS:2 * B_HEADS]
        dproj = jnp.concatenate([dgates, dqa, dka, dva, dqkv, dz, dba], axis=1)
        gw_in = _mm(dproj, s["h1"], ta=True, out_dtype=BF16, name="proj_dw", tm=1152)
        dh1 = _mm(dproj, wl["win"], name="proj_dx", tk=1152)
        mix_srcs = [rows_slabs(_win_from_mine(gw_in)), cols_slabs(gw_a), cols_slabs(gw_b), rows_slabs(gw_out),
                    cols_slabs(dwc[0:CONV_K])]
        if l > 0:
            sc1 = sc1 + scatter("mix", l, mix_srcs, gw_in)
        if l > 0:
            p = saved[l - 1]
            dx, dt2, st1 = _adaln_bwd(s["xin"], n1g[l], sc1, sh1, dh1, dx, p["t2"], mods[l - 1, 5], name="adaln1_bwd")
        else:
            dx, st1 = _adaln_bwd(s["xin"], n1g[l], sc1, sh1, dh1, dx, name="adaln1_bwd_first")
        small_g["norm1_g"][l], small_g["norm2_g"][l] = st1[0], st2[0]
        dmod_rows[l] = jnp.concatenate([st1[2], st1[1], st2[3], st2[2], st2[1], dgt2])
        if l > 0:
            dgt2 = st1[3]
    grad_x = dx[None]

    small_local = {n: (jnp.stack(vs) if isinstance(vs, list) else vs) for n, vs in small_g.items()}
    small_local["b_ada"] = jnp.stack(dmod_rows)
    (g_small,) = _all_gather([_pack_small(small_local, loss_part, depth)], name="gather_small")
    tok = scatter("mix", 0, mix_srcs, g_small)
    wsm = _pack_small(dict(b_ada=b_ada, norm1_g=norm1_g, norm2_g=norm2_g, rel_table=rel_table, a_log=a_log,
                           dt_bias=dt_bias, gdn_norm_g=gdn_norm_g, final_g=final_g), jnp.zeros((1,), F32) + tok, depth)
    msm = _pack_small(dict(b_ada=m_b_ada, norm1_g=m_norm1_g, norm2_g=m_norm2_g, rel_table=m_rel_table, a_log=m_a_log,
                           dt_bias=m_dt_bias, gdn_norm_g=m_gdn_norm_g, final_g=m_final_g), jnp.zeros((1,), F32), depth)
    vsm = _pack_small(dict(b_ada=v_b_ada, norm1_g=v_norm1_g, norm2_g=v_norm2_g, rel_table=v_rel_table, a_log=v_a_log,
                           dt_bias=v_dt_bias, gdn_norm_g=v_gdn_norm_g, final_g=v_final_g), jnp.ones((1,), F32), depth)
    sm = [_unpack_small(t, depth) for t in _adamw(g_small[:, None], wsm[None], msm[None], vsm[None], name="adamw_small")]
    loss = sm[0][1]

    dmod_all = g_small.reshape(N_DEV, -1)[:, :depth * 6 * D].reshape(N_DEV, depth, 6 * D)
    dmod_mine = lax.dynamic_slice_in_dim(dmod_all, me * (6 * D // N_DEV), 6 * D // N_DEV, axis=2)
    g_ada = jnp.stack([_mm(cond, _pad_rows(dmod_mine[:, l], 16), ta=True, name="ada_dw") for l in range(depth)])

    got, mine = {}, {}
    sel = jnp.broadcast_to(jnp.where(is_me[:, :, 0], 1.0, 0.0), (N_DEV, LANE)).astype(F32)

    def finish(kind, idx, after):
        done, lands[kind] = _exchange_wait(*pending_s[kind][:4], after, scatter=True, slot=0, name=f"scatter_wait_{kind}_0")
        own[kind][0] = [lax.dynamic_index_in_dim(t, me, 0, keepdims=False) for t in done]
        for a, i in enumerate(idx):
            got[i] = lands[kind][a]
            mine[i] = jnp.stack([own[kind][l][a] for l in range(depth)])

    def upd(i, w, m, v, name):
        if i in (0, 4):
            return [tr_(t) for t in _adamw(got[i], tr_(w), tr_(m), tr_(v), mine[i], sel, name=name)]
        return _adamw(got[i], w, m, v, mine[i], sel, name=name)

    finish("ffn", ffn, g_ada)
    res = {
        "w_ada": _adamw(g_ada[None], w_ada, m_w_ada, v_w_ada, name="adamw_w_ada"),
        "w_ffn_in": upd(4, w_ffn_in, m_w_ffn_in, v_w_ffn_in, "adamw_w_ffn_in"),
        "w_ffn_out": upd(5, w_ffn_out, m_w_ffn_out, v_w_ffn_out, "adamw_w_ffn_out"),
    }
    finish("mix", mix, res["w_ffn_out"][0])
    res.update({
        "w_in": upd(0, w_in, m_w_in, v_w_in, "adamw_w_in"),
        "w_conv": upd(6, w_conv, m_w_conv, v_w_conv, "adamw_w_conv"),
        "w_branch_a": upd(1, w_branch_a, m_w_branch_a, v_w_branch_a, "adamw_w_branch_a"),
        "w_branch_b": upd(2, w_branch_b, m_w_branch_b, v_w_branch_b, "adamw_w_branch_b"),
        "w_out": upd(3, w_out, m_w_out, v_w_out, "adamw_w_out"),
    })
    for n, _ in _small_spec(depth):
        res[n] = [sm[i][0][n] for i in range(4)]
    order = ("w_ada", "b_ada", "norm1_g", "norm2_g", "w_in", "rel_table", "w_conv", "a_log", "dt_bias", "gdn_norm_g",
             "w_branch_a", "w_branch_b", "w_out", "w_ffn_in", "w_ffn_out", "final_g")
    return (loss, grad_x, *[res[n][0] for n in order], *[res[n][1] for n in order],
            *[res[n][2] for n in order], *[res[n][3] for n in order])
```

```python
import functools
import math

import jax
import jax.numpy as jnp
from jax import lax
from jax.experimental import pallas as pl
from jax.experimental.pallas import tpu as pltpu

F32 = jnp.float32
BF16 = jnp.bfloat16
HI = lax.Precision.HIGHEST

N_DEV = 8
D = 1024
DEPTH = 4
CH = 64
EPS = 1e-6
A_HEADS, A_DH = 8, 64
A_W = A_HEADS * A_DH
A_PAST = 8
A_MAX_REL = 128
QB = 256
KB = QB + A_PAST * CH
B_HEADS, B_DH = 4, 128
B_W = B_HEADS * B_DH
CONV_K = 4
FF = 2816
IN_DIM = 5640
IN_PAD = 5760
LANE = 128
NEG = -1e30
VMEM_LIMIT = 48 * 1024 * 1024

ADAM_LR, ADAM_B1, ADAM_B2, ADAM_EPS, ADAM_WD, ADAM_STEP = 0.001, 0.9, 0.999, 1e-08, 0.01, 10

OFF_GA, OFF_GB, OFF_QA, OFF_KA, OFF_VA, OFF_QB, OFF_KB, OFF_VB, OFF_ZB, OFF_BA = (
    0, 1024, 2048, 2560, 3072, 3584, 4096, 4608, 5120, 5632)


def _cp(sem=None):
    return pltpu.CompilerParams(dimension_semantics=sem, vmem_limit_bytes=VMEM_LIMIT)


def _tile(n, pref):
    if n <= pref:
        return n
    best = None
    for t in range(LANE, pref + 1, LANE):
        if n % t == 0:
            best = t
    assert best is not None, (n, pref)
    return best


def _sigmoid(x):
    return 1.0 / (1.0 + jnp.exp(-x))


def _silu(x):
    return x * _sigmoid(x)


def _dsilu(x):
    s = _sigmoid(x)
    return s * (1.0 + x * (1.0 - s))


def _dot(a, b, prec=None):
    return jnp.dot(a, b, preferred_element_type=F32, precision=prec)


def _dot_nt(a, b, prec=None):
    return lax.dot_general(a, b, (((1,), (1,)), ((), ())), preferred_element_type=F32, precision=prec)


def _dot_tn(a, b, prec=None):
    return lax.dot_general(a, b, (((0,), (0,)), ((), ())), preferred_element_type=F32, precision=prec)


def _mm(a, b, *, ta=False, tb=False, out_dtype=F32, name, tm=1024, tn=1024, tk=1024):
    halves = a.ndim == 3
    a_rows, a_cols = (a.shape[1], 2 * a.shape[2]) if halves else a.shape
    m, k = (a_cols, a_rows) if ta else (a_rows, a_cols)
    n = b.shape[0] if tb else b.shape[1]
    assert k == (b.shape[1] if tb else b.shape[0]), (a.shape, b.shape, ta, tb)
    tm, tn, tk = _tile(m, tm), _tile(n, tn), _tile(k, tk)
    nk = k // tk
    dn = (((0 if ta else 1,), (1 if tb else 0,)), ((), ()))

    def body(a_ref, b_ref, o_ref, *acc):
        part = lax.dot_general(a_ref[...].astype(BF16), b_ref[...].astype(BF16), dn, preferred_element_type=F32)
        if nk == 1:
            o_ref[...] = part.astype(out_dtype)
            return
        acc_ref, kk = acc[0], pl.program_id(2)

        @pl.when(kk == 0)
        def _():
            acc_ref[...] = part

        @pl.when(kk > 0)
        def _():
            acc_ref[...] += part

        @pl.when(kk == nk - 1)
        def _():
            o_ref[...] = acc_ref[...].astype(out_dtype)

    if halves:
        per = a.shape[2] // (tm if ta else tk)
        a_spec = (pl.BlockSpec((None, tk, tm), lambda i, j, q: (i // per, q, i % per)) if ta else
                  pl.BlockSpec((None, tm, tk), lambda i, j, q: (q // per, i, q % per)))
    else:
        a_spec = pl.BlockSpec((tk, tm), lambda i, j, q: (q, i)) if ta else pl.BlockSpec((tm, tk), lambda i, j, q: (i, q))
    b_spec = pl.BlockSpec((tn, tk), lambda i, j, q: (j, q)) if tb else pl.BlockSpec((tk, tn), lambda i, j, q: (q, j))
    return pl.pallas_call(
        body, grid=(m // tm, n // tn, nk), in_specs=[a_spec, b_spec],
        out_specs=pl.BlockSpec((tm, tn), lambda i, j, q: (i, j)),
        out_shape=jax.ShapeDtypeStruct((m, n), out_dtype),
        scratch_shapes=[pltpu.VMEM((tm, tn), F32)] if nk > 1 else [],
        compiler_params=_cp(("parallel", "parallel", "arbitrary")), name=name)(a, b)


def _rb(tr, width, cb=0):
    return pl.BlockSpec((tr, width), lambda i: (i, cb))


def _whole(shape):
    nd = len(shape)
    return pl.BlockSpec(shape, lambda i: (0,) * nd)


def _colsum(v):
    return jnp.sum(v, axis=0, keepdims=True)


def _adaln_fwd(x, g, sc, sh, t=None, gt=None, *, name, tr=256):
    tt = x.shape[0]
    res = t is not None

    def body(*refs):
        if res:
            x_ref, t_ref, gt_ref, g_ref, sc_ref, sh_ref, xo_ref, h_ref = refs
            xv = x_ref[...] + gt_ref[...] * t_ref[...]
            xo_ref[...] = xv
        else:
            x_ref, g_ref, sc_ref, sh_ref, h_ref = refs
            xv = x_ref[...]
        r = lax.rsqrt(jnp.mean(xv * xv, axis=-1, keepdims=True) + EPS)
        h_ref[...] = ((xv * r * g_ref[...]) * (1.0 + sc_ref[...]) + sh_ref[...]).astype(BF16)

    row, vec = _rb(tr, D), _whole((1, D))
    if res:
        ins, in_specs = (x, t, gt, g, sc, sh), [row, row, vec, vec, vec, vec]
        out_shape = (jax.ShapeDtypeStruct((tt, D), F32), jax.ShapeDtypeStruct((tt, D), BF16))
        out_specs = (row, row)
    else:
        ins, in_specs = (x, g, sc, sh), [row, vec, vec, vec]
        out_shape, out_specs = jax.ShapeDtypeStruct((tt, D), BF16), row
    out = pl.pallas_call(body, grid=(tt // tr,), in_specs=in_specs, out_specs=out_specs, out_shape=out_shape,
                         compiler_params=_cp(("parallel",)), name=name)(*ins)
    return out if res else (x, out)


def _adaln_bwd(x, g, sc, sh, dh, dx_in, t=None, gt=None, *, name, tr=256):
    tt = x.shape[0]
    res = t is not None

    def body(*refs):
        if res:
            x_ref, g_ref, sc_ref, sh_ref, dh_ref, dxi_ref, t_ref, gt_ref, dx_ref, dt_ref, st_ref = refs
        else:
            x_ref, g_ref, sc_ref, sh_ref, dh_ref, dxi_ref, dx_ref, st_ref = refs

        @pl.when(pl.program_id(0) == 0)
        def _():
            st_ref[...] = jnp.zeros_like(st_ref)

        xv, dh = x_ref[...], dh_ref[...]
        r = lax.rsqrt(jnp.mean(xv * xv, axis=-1, keepdims=True) + EPS)
        nrm = xv * r
        y = nrm * g_ref[...]
        dy = dh * (1.0 + sc_ref[...])
        dn = dy * g_ref[...]
        dx = dxi_ref[...] + r * (dn - nrm * jnp.mean(dn * nrm, axis=-1, keepdims=True))
        dx_ref[...] = dx
        st_ref[0:1, :] += _colsum(dy * nrm)
        st_ref[1:2, :] += _colsum(dh * y)
        st_ref[2:3, :] += _colsum(dh)
        if res:
            dt_ref[...] = (gt_ref[...] * dx).astype(BF16)
            st_ref[3:4, :] += _colsum(dx * t_ref[...])

    row, vec, st = _rb(tr, D), _whole((1, D)), _whole((8, D))
    ins, in_specs = [x, g, sc, sh, dh, dx_in], [row, vec, vec, vec, row, row]
    out_shape, out_specs = [jax.ShapeDtypeStruct((tt, D), F32)], [row]
    if res:
        ins += [t, gt]
        in_specs += [row, vec]
        out_shape.append(jax.ShapeDtypeStruct((tt, D), BF16))
        out_specs.append(row)
    out_shape.append(jax.ShapeDtypeStruct((8, D), F32))
    out_specs.append(st)
    return pl.pallas_call(body, grid=(tt // tr,), in_specs=in_specs, out_specs=tuple(out_specs),
                          out_shape=tuple(out_shape), compiler_params=_cp(("arbitrary",)), name=name)(*ins)


def _loss_head(x, t, gt, fg, tgt, *, name, tr=256):
    tt = x.shape[0]

    def body(x_ref, t_ref, gt_ref, fg_ref, tgt_ref, dx_ref, dt_ref, st_ref):
        @pl.when(pl.program_id(0) == 0)
        def _():
            st_ref[...] = jnp.zeros_like(st_ref)

        tv = t_ref[...]
        xv = x_ref[...] + gt_ref[...] * tv
        r = lax.rsqrt(jnp.mean(xv * xv, axis=-1, keepdims=True) + EPS)
        nrm = xv * r
        err = nrm * fg_ref[...] - tgt_ref[...]
        st_ref[4:5, :] += 0.5 * jnp.sum(jnp.mean(err * err, axis=-1, keepdims=True), axis=0, keepdims=True)
        dy = err * (1.0 / D)
        dn = dy * fg_ref[...]
        dx = r * (dn - nrm * jnp.mean(dn * nrm, axis=-1, keepdims=True))
        dx_ref[...] = dx
        dt_ref[...] = (gt_ref[...] * dx).astype(BF16)
        st_ref[0:1, :] += _colsum(dy * nrm)
        st_ref[3:4, :] += _colsum(dx * tv)

    row, vec = _rb(tr, D), _whole((1, D))
    return pl.pallas_call(
        body, grid=(tt // tr,), in_specs=[row, row, vec, vec, row], out_specs=(row, row, _whole((8, D))),
        out_shape=(jax.ShapeDtypeStruct((tt, D), F32), jax.ShapeDtypeStruct((tt, D), BF16),
                   jax.ShapeDtypeStruct((8, D), F32)),
        compiler_params=_cp(("arbitrary",)), name=name)(x, t, gt, fg, tgt)


def _branch_merge(ya, yb, wa, wb, proj, *, name, tm=512):
    tt = ya.shape[0]
    tm = _tile(tt, tm)

    def body(ya_ref, yb_ref, wa_ref, wb_ref, ga_ref, gb_ref, pa_ref, pb_ref, o_ref):
        pa = _dot(ya_ref[...], wa_ref[...])
        pb = _dot(yb_ref[...], wb_ref[...])
        pa_ref[...] = pa.astype(BF16)
        pb_ref[...] = pb.astype(BF16)
        o_ref[...] = (_sigmoid(ga_ref[...]) * pa + _sigmoid(gb_ref[...]) * pb).astype(BF16)

    row, half, wsp = _rb(tm, D), _rb(tm, A_W), _whole((A_W, D))
    out = jax.ShapeDtypeStruct((tt, D), BF16)
    return pl.pallas_call(body, grid=(tt // tm,), in_specs=[half, half, wsp, wsp, _rb(tm, D, 0), _rb(tm, D, 1)],
                          out_specs=(row, row, row), out_shape=(out, out, out), compiler_params=_cp(("parallel",)),
                          name=name)(ya, yb, wa, wb, proj, proj)


def _out_adaln(a, w, x, gt, g, sc, sh, *, name, tm=512):
    tt = a.shape[0]
    tm = _tile(tt, tm)

    def body(a_ref, w_ref, x_ref, gt_ref, g_ref, sc_ref, sh_ref, t_ref, xo_ref, h_ref):
        t = _dot(a_ref[...], w_ref[...])
        t_ref[...] = t
        xv = x_ref[...] + gt_ref[...] * t
        xo_ref[...] = xv
        r = lax.rsqrt(jnp.mean(xv * xv, axis=-1, keepdims=True) + EPS)
        h_ref[...] = ((xv * r * g_ref[...]) * (1.0 + sc_ref[...]) + sh_ref[...]).astype(BF16)

    row, vec = _rb(tm, D), _whole((1, D))
    f32 = jax.ShapeDtypeStruct((tt, D), F32)
    return pl.pallas_call(body, grid=(tt // tm,), in_specs=[row, _whole((D, D)), row, vec, vec, vec, vec],
                          out_specs=(row, row, row), out_shape=(f32, f32, jax.ShapeDtypeStruct((tt, D), BF16)),
                          compiler_params=_cp(("parallel",)), name=name)(a, w, x, gt, g, sc, sh)


def _out_bwd_merge(dt, wout, proj, pa, pb, *, name, tm=512):
    tt = dt.shape[0]
    tm = _tile(tt, tm)

    def body(dt_ref, w_ref, ga_ref, gb_ref, pa_ref, pb_ref, dg_ref, dpa_ref, dpb_ref):
        dm_v = _dot_nt(dt_ref[...], w_ref[...])
        sa, sb = _sigmoid(ga_ref[...]), _sigmoid(gb_ref[...])
        dpa_ref[...] = (dm_v * sa).astype(BF16)
        dpb_ref[...] = (dm_v * sb).astype(BF16)
        dg_ref[:, 0:D] = (dm_v * pa_ref[...].astype(F32) * sa * (1.0 - sa)).astype(BF16)
        dg_ref[:, D:2 * D] = (dm_v * pb_ref[...].astype(F32) * sb * (1.0 - sb)).astype(BF16)

    row = _rb(tm, D)
    return pl.pallas_call(
        body, grid=(tt // tm,), in_specs=[row, _whole((D, D)), _rb(tm, D, 0), _rb(tm, D, 1), row, row],
        out_specs=(_rb(tm, 2 * D), row, row),
        out_shape=(jax.ShapeDtypeStruct((tt, 2 * D), BF16), jax.ShapeDtypeStruct((tt, D), BF16),
                   jax.ShapeDtypeStruct((tt, D), BF16)),
        compiler_params=_cp(("parallel",)), name=name)(dt, wout, proj, proj, pa, pb)


FTN = FF // 2


def _ffn_in_swiglu(h, wt, *, name, tm=1024):
    tt = h.shape[0]
    tm = _tile(tt, tm)

    def body(h_ref, wg_ref, wu_ref, gu_ref, act_ref):
        hv = h_ref[...]
        g = _dot_nt(hv, wg_ref[...])
        u = _dot_nt(hv, wu_ref[...])
        gu_ref[0] = g.astype(BF16)
        gu_ref[1] = u.astype(BF16)
        act_ref[...] = (_silu(g) * u).astype(BF16)

    nj = FF // FTN
    return pl.pallas_call(
        body, grid=(tt // tm, nj),
        in_specs=[pl.BlockSpec((tm, D), lambda i, j: (i, 0)), pl.BlockSpec((FTN, D), lambda i, j: (j, 0)),
                  pl.BlockSpec((FTN, D), lambda i, j: (j + nj, 0))],
        out_specs=(pl.BlockSpec((2, tm, FTN), lambda i, j: (0, i, j)), pl.BlockSpec((tm, FTN), lambda i, j: (i, j))),
        out_shape=(jax.ShapeDtypeStruct((2, tt, FF), BF16), jax.ShapeDtypeStruct((tt, FF), BF16)),
        compiler_params=_cp(("parallel", "parallel")), name=name)(h, wt, wt)


def _ffn_out_bwd_swiglu(dt, wo, gu, *, name, tm=1024):
    tt = dt.shape[0]
    tm = _tile(tt, tm)

    def body(dt_ref, wo_ref, gu_ref, dgu_ref):
        da = _dot_nt(dt_ref[...], wo_ref[...])
        g, u = gu_ref[0].astype(F32), gu_ref[1].astype(F32)
        dgu_ref[0] = (da * u * _dsilu(g)).astype(BF16)
        dgu_ref[1] = (da * _silu(g)).astype(BF16)

    blk = pl.BlockSpec((2, tm, FTN), lambda i, j: (0, i, j))
    return pl.pallas_call(
        body, grid=(tt // tm, FF // FTN),
        in_specs=[pl.BlockSpec((tm, D), lambda i, j: (i, 0)), pl.BlockSpec((FTN, D), lambda i, j: (j, 0)), blk],
        out_specs=blk, out_shape=jax.ShapeDtypeStruct((2, tt, FF), BF16),
        compiler_params=_cp(("parallel", "parallel")), name=name)(dt, wo, gu)


BIAS_LW = 1152


def _bias_diagonals(table):
    n_far = A_PAST * CH - A_MAX_REL + 1
    far = jnp.broadcast_to(table[:, 2 * A_MAX_REL:], (A_HEADS, n_far))
    mid = jnp.flip(table[:, 1:2 * A_MAX_REL], axis=1)
    near = jnp.broadcast_to(table[:, 0:1], (A_HEADS, KB - n_far - (2 * A_MAX_REL - 1)))
    pos = jnp.concatenate([far, mid, near], axis=1)
    neg = jnp.broadcast_to(table[:, 2 * A_MAX_REL:], (A_HEADS, QB - 1))
    gap = jnp.zeros((A_HEADS, BIAS_LW - KB - (QB - 1)), F32)
    return jnp.concatenate([pos, gap, neg], axis=1)


def _bias_fwd(diag, *, name):
    def body(w_ref, o_ref):
        qc = lax.broadcasted_iota(jnp.int32, (QB, KB), 0) // CH + A_PAST
        col = lax.broadcasted_iota(jnp.int32, (QB, KB), 1)
        inband = (col // CH <= qc) & (col // CH >= qc - A_PAST)
        for h in range(A_HEADS):
            rows = pltpu.roll(jnp.broadcast_to(w_ref[h:h + 1, :], (QB, BIAS_LW)), 0, 1, stride=1, stride_axis=0)
            for var in range(3):
                o_ref[var, h] = jnp.where(inband & (col >= A_PAST * CH - QB * var), rows[:, :KB], NEG)

    return pl.pallas_call(body, out_shape=jax.ShapeDtypeStruct((3, A_HEADS, QB, KB), F32), compiler_params=_cp(),
                          name=name)(diag)


def _bias_bwd(dbias, *, name):
    def body(d_ref, o_ref):
        r = lax.broadcasted_iota(jnp.int32, (QB, QB), 0)
        c = lax.broadcasted_iota(jnp.int32, (QB, QB), 1)
        flip = jnp.where(r + c == QB - 1, 1.0, 0.0).astype(F32)
        for h in range(A_HEADS):
            x = jnp.concatenate([_dot(flip, d_ref[h], HI), jnp.zeros((QB, BIAS_LW - KB), F32)], axis=1)
            o_ref[h:h + 1, :] = jnp.sum(pltpu.roll(x, 0, 1, stride=1, stride_axis=0), axis=0, keepdims=True)

    return pl.pallas_call(body, out_shape=jax.ShapeDtypeStruct((A_HEADS, BIAS_LW), F32), compiler_params=_cp(),
                          name=name)(dbias)


def _kv_pad(proj, *, name, tr=256):
    tt = proj.shape[0]
    npad = A_PAST * CH // tr

    def body(k_ref, v_ref, ko_ref, vo_ref):
        i = pl.program_id(0)

        @pl.when(i < npad)
        def _():
            ko_ref[...] = jnp.zeros_like(ko_ref)
            vo_ref[...] = jnp.zeros_like(vo_ref)

        @pl.when(i >= npad)
        def _():
            ko_ref[...] = k_ref[...].astype(BF16)
            vo_ref[...] = v_ref[...].astype(BF16)

    src = lambda off: pl.BlockSpec((tr, A_W), lambda i: (jnp.maximum(i - npad, 0), off // A_W))
    out = jax.ShapeDtypeStruct((tt + A_PAST * CH, A_W), BF16)
    return pl.pallas_call(body, grid=(tt // tr + npad,), in_specs=[src(OFF_KA), src(OFF_VA)],
                          out_specs=(_rb(tr, A_W), _rb(tr, A_W)), out_shape=(out, out),
                          compiler_params=_cp(("parallel",)), name=name)(proj, proj)


def _attn_fwd(proj, kpad, vpad, bias, *, name):
    tt = proj.shape[0]

    def body(q_ref, k_ref, v_ref, b_ref, o_ref, l_ref):
        q0 = pl.multiple_of(pl.program_id(1) * QB, QB)
        q = q_ref[...] * (A_DH ** -0.5)
        k = k_ref[pl.ds(q0, KB), :]
        v = v_ref[pl.ds(q0, KB), :]
        lane = lax.broadcasted_iota(jnp.int32, (QB, LANE), 1)
        o = jnp.zeros((QB, LANE), F32)
        lse = jnp.zeros((QB, LANE), F32)
        for a in range(2):
            hm = (lane >= A_DH * a) & (lane < A_DH * (a + 1))
            s = _dot_nt(jnp.where(hm, q, 0.0).astype(BF16), k) + b_ref[a]
            m = jnp.max(s, axis=-1, keepdims=True)
            p = jnp.exp(s - m)
            l = jnp.sum(p, axis=-1, keepdims=True)
            o = jnp.where(hm, _dot(p.astype(BF16), v) / l, o)
            lse = jnp.where(hm, m + jnp.log(l), lse)
        o_ref[...] = o.astype(BF16)
        l_ref[...] = lse

    kv = pl.BlockSpec((tt + A_PAST * CH, LANE), lambda h, i: (0, h))
    blk = pl.BlockSpec((QB, LANE), lambda h, i: (i, h))
    return pl.pallas_call(
        body, grid=(A_W // LANE, tt // QB),
        in_specs=[pl.BlockSpec((QB, LANE), lambda h, i: (i, OFF_QA // LANE + h)), kv, kv,
                  pl.BlockSpec((None, 2, QB, KB), lambda h, i: (jnp.minimum(i, 2), h, 0, 0))],
        out_specs=(blk, blk),
        out_shape=(jax.ShapeDtypeStruct((tt, A_W), BF16), jax.ShapeDtypeStruct((tt, A_W), F32)),
        compiler_params=_cp(("parallel", "parallel")), name=name)(proj, kpad, vpad, bias)


def _attn_bwd(proj, kpad, vpad, bias, o, lse, do, *, name):
    tt = proj.shape[0]
    nq = tt // QB

    def body(q_ref, k_ref, v_ref, b_ref, o_ref, l_ref, do_ref, dq_ref, dko_ref, dvo_ref, db_ref, dk_ref, dv_ref):
        @pl.when(pl.program_id(1) == 0)
        def _():
            dk_ref[...] = jnp.zeros_like(dk_ref)
            dv_ref[...] = jnp.zeros_like(dv_ref)
            db_ref[...] = jnp.zeros_like(db_ref)

        q0 = pl.multiple_of(pl.program_id(1) * QB, QB)
        q, do_v, lse = q_ref[...] * (A_DH ** -0.5), do_ref[...], l_ref[...]
        k = k_ref[pl.ds(q0, KB), :]
        v = v_ref[pl.ds(q0, KB), :]
        dsum = do_v * o_ref[...].astype(F32)
        lane = lax.broadcasted_iota(jnp.int32, (QB, LANE), 1)
        dq = jnp.zeros((QB, LANE), F32)
        dk = jnp.zeros((KB, LANE), F32)
        dv = jnp.zeros((KB, LANE), F32)
        for a in range(2):
            hm = (lane >= A_DH * a) & (lane < A_DH * (a + 1))
            qa = jnp.where(hm, q, 0.0).astype(BF16)
            doa = jnp.where(hm, do_v, 0.0).astype(BF16)
            s = _dot_nt(qa, k) + b_ref[a]
            lse_a = jnp.max(jnp.where(hm, lse, NEG), axis=-1, keepdims=True)
            p = jnp.exp(s - lse_a)
            dp = _dot_nt(doa, v)
            dsum_a = jnp.sum(jnp.where(hm, dsum, 0.0), axis=-1, keepdims=True)
            ds = p * (dp - dsum_a)
            db_ref[a] += ds
            dsb = ds.astype(BF16)
            dq = jnp.where(hm, _dot(dsb, k) * (A_DH ** -0.5), dq)
            dk += _dot_tn(dsb, qa)
            dv += _dot_tn(p.astype(BF16), doa)
        dq_ref[...] = dq.astype(BF16)
        dk_ref[pl.ds(q0, KB), :] += dk
        dv_ref[pl.ds(q0, KB), :] += dv

        @pl.when(pl.program_id(1) == nq - 1)
        def _():
            dko_ref[...] = dk_ref[A_PAST * CH:, :].astype(BF16)
            dvo_ref[...] = dv_ref[A_PAST * CH:, :].astype(BF16)

    kv = pl.BlockSpec((tt + A_PAST * CH, LANE), lambda h, i: (0, h))
    blk = pl.BlockSpec((QB, LANE), lambda h, i: (i, h))
    col = pl.BlockSpec((tt, LANE), lambda h, i: (0, h))
    bsp = pl.BlockSpec((2, QB, KB), lambda h, i: (h, 0, 0))
    bias_in = pl.BlockSpec((None, 2, QB, KB), lambda h, i: (jnp.minimum(i, 2), h, 0, 0))
    out = jax.ShapeDtypeStruct((tt, A_W), BF16)
    return pl.pallas_call(
        body, grid=(A_W // LANE, nq),
        in_specs=[pl.BlockSpec((QB, LANE), lambda h, i: (i, OFF_QA // LANE + h)), kv, kv, bias_in, blk, blk, blk],
        out_specs=(blk, col, col, bsp),
        out_shape=(out, out, out, jax.ShapeDtypeStruct((A_HEADS, QB, KB), F32)),
        scratch_shapes=[pltpu.VMEM((tt + A_PAST * CH, LANE), F32), pltpu.VMEM((tt + A_PAST * CH, LANE), F32)],
        compiler_params=_cp(("parallel", "arbitrary")), name=name)(proj, kpad, vpad, bias, o, lse, do)


GTR = 256


def _taps(w_ref, grp):
    return [w_ref[j:j + 1, grp * B_W:(grp + 1) * B_W] for j in range(CONV_K)]


def _shifts(xe, rows):
    return [xe[8:8 + rows]] + [pltpu.roll(xe, s, 0)[8:8 + rows] for s in range(1, CONV_K)]


def _conv(shifts, taps):
    acc = taps[CONV_K - 1] * shifts[0]
    for s in range(1, CONV_K):
        acc = acc + taps[CONV_K - 1 - s] * shifts[s]
    return acc


def _qk_scale(grp):
    return B_DH ** -0.5 if grp == 0 else 1.0


def _act_fwd(c, grp):
    y = _silu(c)
    if grp == 2:
        return y
    parts = []
    for hd in range(B_HEADS):
        yh = y[:, hd * B_DH:(hd + 1) * B_DH]
        parts.append(yh * (lax.rsqrt(jnp.sum(yh * yh, axis=-1, keepdims=True) + EPS) * _qk_scale(grp)))
    return jnp.concatenate(parts, axis=1)


def _act_bwd(c, dy, grp):
    if grp == 2:
        return dy * _dsilu(c)
    y = _silu(c)
    parts = []
    for hd in range(B_HEADS):
        yh = y[:, hd * B_DH:(hd + 1) * B_DH]
        r = lax.rsqrt(jnp.sum(yh * yh, axis=-1, keepdims=True) + EPS)
        dyh = dy[:, hd * B_DH:(hd + 1) * B_DH] * _qk_scale(grp)
        parts.append(r * dyh - yh * (r * r * r) * jnp.sum(dyh * yh, axis=-1, keepdims=True))
    return jnp.concatenate(parts, axis=1) * _dsilu(c)


def _chunk_tri(n, upper=False):
    r = lax.broadcasted_iota(jnp.int32, (n, n), 0)
    c = lax.broadcasted_iota(jnp.int32, (n, n), 1)
    same = (r // CH) == (c // CH)
    return jnp.where(same & ((r <= c) if upper else (r >= c)), 1.0, 0.0).astype(F32)


def _gate_rows(ba, par_ref):
    lane = lax.broadcasted_iota(jnp.int32, ba.shape, 1)
    z = ba + par_ref[1:2, :]
    sp = jnp.maximum(z, 0.0) + jnp.log(1.0 + jnp.exp(-jnp.abs(z)))
    g = -jnp.exp(par_ref[0:1, :]) * sp
    return jnp.where(lane < B_HEADS, _sigmoid(ba), jnp.where(lane < 2 * B_HEADS, g, 0.0)), z


def _prev8(cb):
    return pl.BlockSpec((8, B_W), lambda i: (jnp.maximum(i * (GTR // 8) - 1, 0), cb))


def _next8(cb, nb):
    return pl.BlockSpec((8, B_W), lambda i: (jnp.minimum((i + 1) * (GTR // 8), nb * (GTR // 8) - 1), cb))


def _gdn_pre_fwd(proj, wconv, par, *, name):
    tt = proj.shape[0]

    def body(q_ref, k_ref, v_ref, qh_ref, kh_ref, vh_ref, ba_ref, w_ref, par_ref, qo_ref, ko_ref, vo_ref, aux_ref):
        first = pl.program_id(0) == 0
        for grp, (x_ref, h_ref, o_ref) in enumerate(((q_ref, qh_ref, qo_ref), (k_ref, kh_ref, ko_ref),
                                                     (v_ref, vh_ref, vo_ref))):
            xe = jnp.concatenate([jnp.where(first, 0.0, h_ref[...]), x_ref[...]], axis=0)
            o_ref[...] = _act_fwd(_conv(_shifts(xe, GTR), _taps(w_ref, grp)), grp)
        bg, _ = _gate_rows(ba_ref[...], par_ref)
        lane = lax.broadcasted_iota(jnp.int32, bg.shape, 1)
        aux_ref[...] = jnp.where(lane < B_HEADS, bg, _dot(_chunk_tri(GTR), bg, HI))

    col = lambda off: _rb(GTR, B_W, off // B_W)
    outs = jax.ShapeDtypeStruct((tt, B_W), F32)
    return pl.pallas_call(
        body, grid=(tt // GTR,),
        in_specs=[col(OFF_QB), col(OFF_KB), col(OFF_VB), _prev8(OFF_QB // B_W), _prev8(OFF_KB // B_W),
                  _prev8(OFF_VB // B_W), _rb(GTR, LANE, OFF_BA // LANE), _whole((CONV_K, 3 * B_W)),
                  _whole((8, LANE))],
        out_specs=(_rb(GTR, B_W), _rb(GTR, B_W), _rb(GTR, B_W), _rb(GTR, LANE)),
        out_shape=(outs, outs, outs, jax.ShapeDtypeStruct((tt, LANE), F32)),
        compiler_params=_cp(("parallel",)), name=name)(proj, proj, proj, proj, proj, proj, proj, wconv, par)


def _gdn_pre_bwd(proj, wconv, par, dq, dk, dv, daux, *, name):
    tt = proj.shape[0]
    nb = tt // GTR

    def body(q_ref, k_ref, v_ref, qh_ref, kh_ref, vh_ref, qn_ref, kn_ref, vn_ref, ba_ref, w_ref, par_ref,
             dq_ref, dk_ref, dv_ref, dqn_ref, dkn_ref, dvn_ref, daux_ref, dx_ref, dba_ref, dw_ref, dpar_ref):
        i = pl.program_id(0)
        first, last = i == 0, i == nb - 1

        @pl.when(first)
        def _():
            dw_ref[...] = jnp.zeros_like(dw_ref)
            dpar_ref[...] = jnp.zeros_like(dpar_ref)

        groups = ((q_ref, qh_ref, qn_ref, dq_ref, dqn_ref), (k_ref, kh_ref, kn_ref, dk_ref, dkn_ref),
                  (v_ref, vh_ref, vn_ref, dv_ref, dvn_ref))
        for grp, (x_ref, h_ref, xn_ref, d_ref, dn_ref) in enumerate(groups):
            taps = _taps(w_ref, grp)
            xe = jnp.concatenate([jnp.where(first, 0.0, h_ref[...]), x_ref[...]], axis=0)
            sh = _shifts(xe, GTR)
            dc = _act_bwd(_conv(sh, taps), d_ref[...], grp)
            xe_n = jnp.concatenate([x_ref[GTR - 8:GTR, :], xn_ref[...]], axis=0)
            dcn = _act_bwd(_conv(_shifts(xe_n, 8), taps), dn_ref[...], grp)
            dce = jnp.concatenate([dc, jnp.where(last, 0.0, dcn)], axis=0)
            dx = taps[CONV_K - 1] * dc
            dw_ref[CONV_K - 1:CONV_K, grp * B_W:(grp + 1) * B_W] += _colsum(dc * sh[0])
            for s in range(1, CONV_K):
                dx = dx + taps[CONV_K - 1 - s] * pltpu.roll(dce, GTR + 8 - s, 0)[0:GTR]
                dw_ref[CONV_K - 1 - s:CONV_K - s, grp * B_W:(grp + 1) * B_W] += _colsum(dc * sh[s])
            dx_ref[:, grp * B_W:(grp + 1) * B_W] = dx.astype(BF16)
        ba = ba_ref[...]
        lane = lax.broadcasted_iota(jnp.int32, ba.shape, 1)
        bg, z = _gate_rows(ba, par_ref)
        daux_v = daux_ref[...]
        dg = _dot(_chunk_tri(GTR, upper=True), daux_v, HI)
        dgl = jnp.where((lane >= B_HEADS) & (lane < 2 * B_HEADS), dg, 0.0)
        da = dgl * (-jnp.exp(par_ref[0:1, :])) * _sigmoid(z)
        dbr = jnp.where(lane < B_HEADS, daux_v * bg * (1.0 - bg), 0.0)
        dba_ref[...] = (dbr + da).astype(BF16)
        dpar_ref[0:1, :] += _colsum(dgl * bg)
        dpar_ref[1:2, :] += _colsum(da)

    col = lambda off: _rb(GTR, B_W, off // B_W)
    row, rowl = _rb(GTR, B_W), _rb(GTR, LANE)
    return pl.pallas_call(
        body, grid=(nb,),
        in_specs=[col(OFF_QB), col(OFF_KB), col(OFF_VB),
                  _prev8(OFF_QB // B_W), _prev8(OFF_KB // B_W), _prev8(OFF_VB // B_W),
                  _next8(OFF_QB // B_W, nb), _next8(OFF_KB // B_W, nb), _next8(OFF_VB // B_W, nb),
                  _rb(GTR, LANE, OFF_BA // LANE), _whole((CONV_K, 3 * B_W)), _whole((8, LANE)),
                  row, row, row, _next8(0, nb), _next8(0, nb), _next8(0, nb), rowl],
        out_specs=(_rb(GTR, 3 * B_W), rowl, _whole((8, 3 * B_W)), _whole((8, LANE))),
        out_shape=(jax.ShapeDtypeStruct((tt, 3 * B_W), BF16), jax.ShapeDtypeStruct((tt, LANE), BF16),
                   jax.ShapeDtypeStruct((8, 3 * B_W), F32), jax.ShapeDtypeStruct((8, LANE), F32)),
        compiler_params=_cp(("arbitrary",)), name=name)(
            proj, proj, proj, proj, proj, proj, proj, proj, proj, proj, wconv, par, dq, dk, dv, dq, dk, dv, daux)


def _col(x, j):
    lane = lax.broadcasted_iota(jnp.int32, x.shape, 1)
    return jnp.sum(jnp.where(lane == j, x, 0.0), axis=-1, keepdims=True)


def _split(x):
    hi = x.astype(BF16)
    return hi, (x - hi.astype(F32)).astype(BF16)


def _dot3(a, b, tn=False):
    dot = _dot_tn if tn else _dot
    (ah, al), (bh, bl) = _split(a), _split(b)
    return dot(ah, bh) + (dot(ah, bl) + dot(al, bh))


def _chunk_masks():
    r = lax.broadcasted_iota(jnp.int32, (CH, CH), 0)
    c = lax.broadcasted_iota(jnp.int32, (CH, CH), 1)
    return r > c, r >= c


def _gc_rows(aux, nc):
    t = jnp.transpose(aux[:, B_HEADS:2 * B_HEADS].reshape(nc, CH, B_HEADS), (0, 2, 1))
    return jnp.concatenate([t, jnp.zeros_like(t)], axis=1).reshape(nc * 8, CH)


_CHUNK8 = lambda width, n=1: pl.BlockSpec((8 * n, width), lambda i: (i, 0))
_CHUNK4 = lambda a, b, n=1: pl.BlockSpec((B_HEADS * n, a, b), lambda i: (i, 0, 0))
NCH = 2


def _per_chunk(body, rows):
    def wrapped(*refs):
        for ci in range(NCH):
            body(*[r.at[pl.ds(ci * n, n)] for r, n in zip(refs, rows)])
    return wrapped


def _gdn_lower(k, aux, auxt, *, name):
    tt = k.shape[0]

    def body(k_ref, aux_ref, auxt_ref, l_ref):
        aux_v = aux_ref[...]
        strict, _ = _chunk_masks()
        for hd in range(B_HEADS):
            kh = k_ref[:, hd * B_DH:(hd + 1) * B_DH].astype(BF16)
            diff = _col(aux_v, B_HEADS + hd) - auxt_ref[hd:hd + 1, :]
            dec = jnp.exp(jnp.where(strict, diff, NEG))
            l_ref[hd] = _col(aux_v, hd) * _dot_nt(kh, kh) * dec

    return pl.pallas_call(
        _per_chunk(body, (CH, CH, 8, B_HEADS)), grid=(tt // CH // NCH,),
        in_specs=[_rb(NCH * CH, B_W), _rb(NCH * CH, LANE), _CHUNK8(CH, NCH)],
        out_specs=_CHUNK4(CH, CH, NCH),
        out_shape=jax.ShapeDtypeStruct((tt // CH * B_HEADS, CH, CH), F32),
        compiler_params=_cp(("parallel",)), name=name)(k, aux, auxt)


def _tri_inverse(lt, *, name):
    nb = lt.shape[2]

    def body(l_ref, t_ref):
        rowid = lax.broadcasted_iota(jnp.int32, (CH, nb), 0)

        def outer(i, carry):
            def inner(j, acc):
                return acc + l_ref[i, pl.ds(j, 1), :] * t_ref[j]

            acc = lax.fori_loop(0, i, inner, jnp.zeros((CH, nb), F32))
            t_ref[i] = jnp.where(rowid == i, 1.0, 0.0) - acc
            return carry

        lax.fori_loop(0, CH, outer, 0)

    return pl.pallas_call(body, out_shape=jax.ShapeDtypeStruct(lt.shape, F32),
                          in_specs=[pl.BlockSpec(memory_space=pltpu.VMEM)],
                          out_specs=pl.BlockSpec(memory_space=pltpu.VMEM),
                          compiler_params=_cp(), name=name)(lt)


def _gdn_gates(aux_v, aux_last, auxt_ref, hd):
    _, incl = _chunk_masks()
    beta = _col(aux_v, hd)
    gc = _col(aux_v, B_HEADS + hd)
    gl = _col(aux_last, B_HEADS + hd)
    dec = jnp.exp(jnp.where(incl, gc - auxt_ref[hd:hd + 1, :], NEG))
    return beta, gc, gl, jnp.exp(gc), dec


def _gdn_intra(q, k, v, aux, auxt, tinv, *, name):
    tt = q.shape[0]
    nc = tt // CH

    def body(q_ref, k_ref, v_ref, aux_ref, auxt_ref, t_ref, u0_ref, w_ref, qd_ref, kd_ref, qk_ref, gle_ref):
        aux_v = aux_ref[...]
        aux_last = aux_ref[CH - 1:CH, :]
        lane8 = lax.broadcasted_iota(jnp.int32, (8, LANE), 1)
        gle = jnp.zeros((8, LANE), F32)
        heads = range(B_HEADS)
        sls = [slice(hd * B_DH, (hd + 1) * B_DH) for hd in heads]
        gates = [_gdn_gates(aux_v, aux_last, auxt_ref, hd) for hd in heads]
        qk0 = [_dot_nt(q_ref[:, sls[hd]].astype(BF16), k_ref[:, sls[hd]].astype(BF16)) for hd in heads]
        u0 = [_dot3(t_ref[hd], v_ref[:, sls[hd]] * gates[hd][0]) for hd in heads]
        wk = [_dot3(t_ref[hd], k_ref[:, sls[hd]] * (gates[hd][0] * gates[hd][3])) for hd in heads]
        for hd in heads:
            sl = sls[hd]
            beta, gc, gl, egc, dec = gates[hd]
            qk_ref[hd] = (qk0[hd] * dec).astype(BF16)
            u0_ref[:, sl] = u0[hd]
            w_ref[:, sl] = wk[hd].astype(BF16)
            qd_ref[:, sl] = (q_ref[:, sl] * egc).astype(BF16)
            kd_ref[:, sl] = (k_ref[:, sl] * jnp.exp(gl - gc)).astype(BF16)
            gle = gle + jnp.where(lane8 == hd, jnp.exp(gl), 0.0)
        gle_ref[...] = gle

    row = _rb(NCH * CH, B_W)
    half = jax.ShapeDtypeStruct((tt, B_W), BF16)
    return pl.pallas_call(
        _per_chunk(body, (CH, CH, CH, CH, 8, B_HEADS, CH, CH, CH, CH, B_HEADS, 8)), grid=(nc // NCH,),
        in_specs=[row, row, row, _rb(NCH * CH, LANE), _CHUNK8(CH, NCH), _CHUNK4(CH, CH, NCH)],
        out_specs=(row, row, row, row, _CHUNK4(CH, CH, NCH), _CHUNK8(LANE, NCH)),
        out_shape=(jax.ShapeDtypeStruct((tt, B_W), F32), half, half, half,
                   jax.ShapeDtypeStruct((nc * B_HEADS, CH, CH), BF16), jax.ShapeDtypeStruct((nc * 8, LANE), F32)),
        compiler_params=_cp(("parallel",)), name=name)(q, k, v, aux, auxt, tinv)


def _gdn_scan_fwd(u0, w, qd, kd, qk, gle, *, name):
    tt = u0.shape[0]
    nc = tt // CH

    def body(u0_ref, w_ref, qd_ref, kd_ref, qk_ref, gle_ref, o_ref, ss_ref, u_ref, s_ref):
        @pl.when(pl.program_id(0) == 0)
        def _():
            s_ref[...] = jnp.zeros_like(s_ref)

        gle = gle_ref[0:1, :]
        heads = range(B_HEADS)
        sls = [slice(hd * B_DH, (hd + 1) * B_DH) for hd in heads]
        st = [s_ref[hd] for hd in heads]
        sb = [t.astype(BF16) for t in st]
        ws = [_dot(w_ref[:, sls[hd]], sb[hd]) for hd in heads]
        qs = [_dot(qd_ref[:, sls[hd]], sb[hd]) for hd in heads]
        ub = [(u0_ref[:, sls[hd]] - ws[hd]).astype(BF16) for hd in heads]
        ku = [_dot_tn(kd_ref[:, sls[hd]], ub[hd]) for hd in heads]
        qu = [_dot(qk_ref[hd], ub[hd]) for hd in heads]
        for hd in heads:
            ss_ref[hd] = st[hd]
            u_ref[:, sls[hd]] = ub[hd]
            o_ref[:, sls[hd]] = qs[hd] + qu[hd]
            s_ref[hd] = st[hd] * _col(gle, hd) + ku[hd]

    row = _rb(CH, B_W)
    return pl.pallas_call(
        body, grid=(nc,), in_specs=[row, row, row, row, _CHUNK4(CH, CH), _CHUNK8(LANE)],
        out_specs=(row, _CHUNK4(B_DH, B_DH), row),
        out_shape=(jax.ShapeDtypeStruct((tt, B_W), F32), jax.ShapeDtypeStruct((nc * B_HEADS, B_DH, B_DH), F32),
                   jax.ShapeDtypeStruct((tt, B_W), BF16)),
        scratch_shapes=[pltpu.VMEM((B_HEADS, B_DH, B_DH), F32)],
        compiler_params=_cp(("arbitrary",)), name=name)(u0, w, qd, kd, qk, gle)


def _gdn_scan_bwd(w, qd, kd, qk, gle, do, *, name):
    tt = w.shape[0]
    nc = tt // CH

    def body(w_ref, qd_ref, kd_ref, qk_ref, gle_ref, do_ref, du_ref, dss_ref, ds_ref):
        @pl.when(pl.program_id(0) == 0)
        def _():
            ds_ref[...] = jnp.zeros_like(ds_ref)

        gle = gle_ref[0:1, :]
        heads = range(B_HEADS)
        sls = [slice(hd * B_DH, (hd + 1) * B_DH) for hd in heads]
        dst = [ds_ref[hd] for hd in heads]
        dob = [do_ref[:, sls[hd]].astype(BF16) for hd in heads]
        kds = [_dot(kd_ref[:, sls[hd]], dst[hd].astype(BF16)) for hd in heads]
        qkd = [_dot_tn(qk_ref[hd], dob[hd]) for hd in heads]
        qdd = [_dot_tn(qd_ref[:, sls[hd]], dob[hd]) for hd in heads]
        du = [qkd[hd] + kds[hd] for hd in heads]
        wdu = [_dot_tn(w_ref[:, sls[hd]], du[hd].astype(BF16)) for hd in heads]
        for hd in heads:
            dss_ref[hd] = dst[hd]
            du_ref[:, sls[hd]] = du[hd]
            ds_ref[hd] = qdd[hd] + _col(gle, hd) * dst[hd] - wdu[hd]

    rev = lambda width: pl.BlockSpec((CH, width), lambda i: (nc - 1 - i, 0))
    rev4 = lambda a, b: pl.BlockSpec((B_HEADS, a, b), lambda i: (nc - 1 - i, 0, 0))
    return pl.pallas_call(
        body, grid=(nc,),
        in_specs=[rev(B_W), rev(B_W), rev(B_W), rev4(CH, CH), pl.BlockSpec((8, LANE), lambda i: (nc - 1 - i, 0)), rev(B_W)],
        out_specs=(rev(B_W), rev4(B_DH, B_DH)),
        out_shape=(jax.ShapeDtypeStruct((tt, B_W), F32), jax.ShapeDtypeStruct((nc * B_HEADS, B_DH, B_DH), F32)),
        scratch_shapes=[pltpu.VMEM((B_HEADS, B_DH, B_DH), F32)],
        compiler_params=_cp(("arbitrary",)), name=name)(w, qd, kd, qk, gle, do)


def _gdn_bwd(q, k, v, aux, auxt, tinv, u0, w, u, ss, dss, du, do, *, name):
    tt = q.shape[0]
    nc = tt // CH

    def body(q_ref, k_ref, v_ref, aux_ref, auxt_ref, t_ref, u0_ref, w_ref, u_ref, ss_ref, dss_ref, du_ref, do_ref,
             dq_ref, dk_ref, dv_ref, daux_ref):
        aux_v = aux_ref[...]
        aux_last = aux_ref[CH - 1:CH, :]
        lane = lax.broadcasted_iota(jnp.int32, (CH, LANE), 1)
        rowi = lax.broadcasted_iota(jnp.int32, (CH, 1), 0)
        strict, incl = _chunk_masks()
        daux = jnp.zeros((CH, LANE), F32)
        heads = range(B_HEADS)
        sls = [slice(hd * B_DH, (hd + 1) * B_DH) for hd in heads]
        gates = [_gdn_gates(aux_v, aux_last, auxt_ref, hd) for hd in heads]
        kbs = [k_ref[:, sl].astype(BF16) for sl in sls]
        qbs = [q_ref[:, sl].astype(BF16) for sl in sls]
        sbs = [ss_ref[hd].astype(BF16) for hd in heads]
        dsbs = [dss_ref[hd].astype(BF16) for hd in heads]
        dobs = [do_ref[:, sl].astype(BF16) for sl in sls]
        kks = [_dot_nt(kbs[hd], kbs[hd]) for hd in heads]
        qk0s = [_dot_nt(qbs[hd], kbs[hd]) for hd in heads]
        dq_decs = [_dot_nt(dobs[hd], sbs[hd]) for hd in heads]
        dqks = [_dot_nt(dobs[hd], u_ref[:, sls[hd]]) for hd in heads]
        dk_decs = [_dot_nt(u_ref[:, sls[hd]], dsbs[hd]) for hd in heads]
        dws = [-_dot_nt(du_ref[:, sls[hd]].astype(BF16), sbs[hd]) for hd in heads]
        drvs = [_dot3(t_ref[hd], du_ref[:, sls[hd]], tn=True) for hd in heads]
        drks = [_dot3(t_ref[hd], dws[hd], tn=True) for hd in heads]
        dls = [-(_dot_nt(drvs[hd].astype(BF16), u0_ref[:, sls[hd]].astype(BF16))
                 + _dot_nt(drks[hd].astype(BF16), w_ref[:, sls[hd]])) for hd in heads]
        for hd in heads:
            sl = sls[hd]
            qh, kh, vh = q_ref[:, sl], k_ref[:, sl], v_ref[:, sl]
            beta, gc, gl, egc, dec = gates[hd]
            ekd, eg_last = jnp.exp(gl - gc), jnp.exp(gl)
            kb, qb, kk, qk0 = kbs[hd], qbs[hd], kks[hd], qk0s[hd]
            st, dst = ss_ref[hd], dss_ref[hd]
            dq_dec, dk_dec = dq_decs[hd], dk_decs[hd]
            dqk = jnp.where(incl, dqks[hd], 0.0)
            dgl = jnp.sum(jnp.sum(st * dst, axis=-1, keepdims=True), axis=0, keepdims=True) * eg_last
            drv, drk = drvs[hd], drks[hd]
            dl = jnp.where(strict, dls[hd], 0.0)
            dv_ref[:, sl] = drv * beta
            rk = jnp.sum(drk * kh, axis=-1, keepdims=True)
            dbeta = jnp.sum(drv * vh, axis=-1, keepdims=True) + rk * egc
            dgc = rk * beta * egc
            dk = drk * (beta * egc)
            ldec = dl * dec
            dbeta = dbeta + jnp.sum(ldec * kk, axis=-1, keepdims=True)
            dkk = (ldec * beta).astype(BF16)
            dqk0 = (dqk * dec).astype(BF16)
            ddec = ldec * beta * kk + dqk * (qk0 * dec)
            dq = _dot(dqk0, kb) + dq_dec * egc
            dk = dk + _dot_tn(dqk0, qb) + _dot(dkk, kb) + _dot_tn(dkk, kb) + dk_dec * ekd
            dgc = dgc + jnp.sum(ddec, axis=-1, keepdims=True) - _col_from_rowsum(ddec)
            dgc = dgc + jnp.sum(dq_dec * qh, axis=-1, keepdims=True) * egc
            kd = jnp.sum(dk_dec * kh, axis=-1, keepdims=True) * ekd
            dgc = dgc - kd
            dgc = dgc + jnp.where(rowi == CH - 1, jnp.sum(kd, axis=0, keepdims=True) + dgl, 0.0)
            dq_ref[:, sl] = dq
            dk_ref[:, sl] = dk
            daux = daux + jnp.where(lane == hd, dbeta, 0.0) + jnp.where(lane == B_HEADS + hd, dgc, 0.0)
        daux_ref[...] = daux

    row = _rb(NCH * CH, B_W)
    outs = jax.ShapeDtypeStruct((tt, B_W), F32)
    return pl.pallas_call(
        _per_chunk(body, (CH, CH, CH, CH, 8, B_HEADS, CH, CH, CH, B_HEADS, B_HEADS, CH, CH, CH, CH, CH, CH)),
        grid=(nc // NCH,),
        in_specs=[row, row, row, _rb(NCH * CH, LANE), _CHUNK8(CH, NCH), _CHUNK4(CH, CH, NCH), row, row, row,
                  _CHUNK4(B_DH, B_DH, NCH), _CHUNK4(B_DH, B_DH, NCH), row, row],
        out_specs=(row, row, row, _rb(NCH * CH, LANE)),
        out_shape=(outs, outs, outs, jax.ShapeDtypeStruct((tt, LANE), F32)),
        compiler_params=_cp(("parallel",)), name=name)(q, k, v, aux, auxt, tinv, u0, w, u, ss, dss, du, do)


def _col_from_rowsum(m):
    hi, lo = _split(m)
    ones = jnp.ones((CH, LANE), BF16)
    return (_dot_tn(hi, ones) + _dot_tn(lo, ones))[:, 0:1]


def _gdn_post_fwd(o, proj, gn, *, name, tr=256):
    tt = o.shape[0]

    def body(o_ref, z_ref, g_ref, y_ref):
        for hd in range(B_HEADS):
            sl = slice(hd * B_DH, (hd + 1) * B_DH)
            oh = o_ref[:, sl]
            r = lax.rsqrt(jnp.mean(oh * oh, axis=-1, keepdims=True) + EPS)
            y_ref[:, sl] = (oh * r * g_ref[...] * _silu(z_ref[:, sl])).astype(BF16)

    return pl.pallas_call(body, grid=(tt // tr,), in_specs=[_rb(tr, B_W), _rb(tr, B_W, OFF_ZB // B_W), _whole((1, B_DH))],
                          out_specs=_rb(tr, B_W), out_shape=jax.ShapeDtypeStruct((tt, B_W), BF16),
                          compiler_params=_cp(("parallel",)), name=name)(o, proj, gn)


def _gdn_post_bwd(o, proj, gn, dy, *, name, tr=256):
    tt = o.shape[0]

    def body(o_ref, z_ref, g_ref, dy_ref, do_ref, dz_ref, dg_ref):
        @pl.when(pl.program_id(0) == 0)
        def _():
            dg_ref[...] = jnp.zeros_like(dg_ref)

        g = g_ref[...]
        for hd in range(B_HEADS):
            sl = slice(hd * B_DH, (hd + 1) * B_DH)
            oh, zh, dyh = o_ref[:, sl], z_ref[:, sl], dy_ref[:, sl]
            r = lax.rsqrt(jnp.mean(oh * oh, axis=-1, keepdims=True) + EPS)
            a = oh * r
            s = _silu(zh)
            da = dyh * g * s
            dg_ref[0:1, :] += _colsum(dyh * a * s)
            dz_ref[:, sl] = (dyh * a * g * _dsilu(zh)).astype(BF16)
            do_ref[:, sl] = r * (da - a * jnp.mean(da * a, axis=-1, keepdims=True))

    return pl.pallas_call(
        body, grid=(tt // tr,), in_specs=[_rb(tr, B_W), _rb(tr, B_W, OFF_ZB // B_W), _whole((1, B_DH)), _rb(tr, B_W)],
        out_specs=(_rb(tr, B_W), _rb(tr, B_W), _whole((8, B_DH))),
        out_shape=(jax.ShapeDtypeStruct((tt, B_W), F32), jax.ShapeDtypeStruct((tt, B_W), BF16),
                   jax.ShapeDtypeStruct((8, B_DH), F32)),
        compiler_params=_cp(("arbitrary",)), name=name)(o, proj, gn, dy)


def _adamw(parts, w, m, v, own=None, sel=None, *, name, tr=256):
    npart, nl, r, c = parts.shape
    tr = max([t for t in range(8, min(r, tr) + 1, 8) if r % t == 0], default=r)
    tc = c if tr < r or r <= 256 or c % 256 else 256
    c1, c2 = 1.0 - ADAM_B1 ** ADAM_STEP, 1.0 - ADAM_B2 ** ADAM_STEP

    def body(*refs):
        if own is None:
            p_ref, w_ref, m_ref, v_ref, g_ref, d_ref, mo_ref, vo_ref = refs
            part = lambda i: p_ref[i].astype(F32)
        else:
            p_ref, w_ref, m_ref, v_ref, own_ref, sel_ref, g_ref, d_ref, mo_ref, vo_ref = refs
            part = lambda i: jnp.where(sel_ref[i:i + 1, 0:1] > 0.5, own_ref[...].astype(F32), p_ref[i].astype(F32))
        g = part(0)
        for i in range(1, npart):
            g = g + part(i)
        mn = ADAM_B1 * m_ref[...] + (1.0 - ADAM_B1) * g
        vn = ADAM_B2 * v_ref[...] + (1.0 - ADAM_B2) * (g * g)
        g_ref[...] = g
        mo_ref[...] = mn
        vo_ref[...] = vn
        d_ref[...] = -ADAM_LR * ((mn / c1) / (jnp.sqrt(vn / c2) + ADAM_EPS) + ADAM_WD * w_ref[...])

    row = pl.BlockSpec((None, tr, tc), lambda l, i, j: (l, i, j))
    out = jax.ShapeDtypeStruct((nl, r, c), F32)
    ins, in_specs = [parts, w, m, v], [pl.BlockSpec((npart, None, tr, tc), lambda l, i, j: (0, l, i, j)), row, row, row]
    if own is not None:
        ins += [own, sel]
        in_specs += [row, pl.BlockSpec((N_DEV, LANE), lambda l, i, j: (0, 0))]
    return pl.pallas_call(body, grid=(nl, r // tr, c // tc), in_specs=in_specs, out_specs=(row, row, row, row),
                          out_shape=(out, out, out, out), compiler_params=_cp(("parallel", "parallel", "parallel")),
                          name=name)(*ins)


def _peer(k):
    x, y, c = lax.axis_index("x"), lax.axis_index("y"), lax.axis_index("c")
    return ((1 - x) if k & 4 else x, (1 - y) if k & 2 else y, (1 - c) if k & 1 else c)


def _my_index():
    return 4 * lax.axis_index("x") + 2 * lax.axis_index("y") + lax.axis_index("c")


def _index_of(p):
    return 4 * p[0] + 2 * p[1] + p[2]


def _all_gather(xs, *, name):
    n = len(xs)

    def body(*refs):
        x_refs, o_refs = refs[:n], refs[n:2 * n]
        send, recv, loc = refs[2 * n:]
        me = _my_index()
        copies = []
        for a in range(n):
            cp = pltpu.make_async_copy(x_refs[a], o_refs[a].at[me], loc.at[a])
            cp.start()
            copies.append(cp)
        rdmas = []
        for a in range(n):
            for k in range(1, N_DEV):
                r = pltpu.make_async_remote_copy(
                    src_ref=x_refs[a], dst_ref=o_refs[a].at[me], send_sem=send.at[a, k - 1], recv_sem=recv.at[a, k - 1],
                    device_id=_peer(k), device_id_type=pl.DeviceIdType.MESH)
                r.start()
                rdmas.append(r)
        for a in range(n):
            for k in range(1, N_DEV):
                pltpu.make_async_remote_copy(
                    src_ref=x_refs[a], dst_ref=o_refs[a].at[_index_of(_peer(k))], send_sem=send.at[a, k - 1],
                    recv_sem=recv.at[a, k - 1], device_id=_peer(k), device_id_type=pl.DeviceIdType.MESH).wait_recv()
        for r in rdmas:
            r.wait_send()
        for cp in copies:
            cp.wait()

    any_spec = pl.BlockSpec(memory_space=pl.ANY)
    return pl.pallas_call(
        body, in_specs=[any_spec] * n, out_specs=tuple([any_spec] * n),
        out_shape=tuple(jax.ShapeDtypeStruct((N_DEV,) + x.shape, x.dtype) for x in xs),
        scratch_shapes=[pltpu.SemaphoreType.DMA((n, N_DEV - 1)), pltpu.SemaphoreType.DMA((n, N_DEV - 1)),
                        pltpu.SemaphoreType.DMA((n,))],
        name=name)(*xs)


_HBM = pl.BlockSpec(memory_space=pltpu.HBM)
_SEM = pl.BlockSpec(memory_space=pltpu.SEMAPHORE)
_EFFECT = pltpu.SideEffectType.DATAFLOW_SIDE_EFFECTING


def _split_copy(src_ref, land_ref, send, recv, a, k, scatter, slot, sending):
    me, peer = _my_index(), _index_of(_peer(k))
    src = src_ref.at[peer if sending else me] if scatter else src_ref
    land = land_ref.at[me if sending else peer]
    if slot is not None:
        land = land.at[slot]
    sem = a * (N_DEV - 1) + k - 1
    return pltpu.make_async_remote_copy(src_ref=src, dst_ref=land, send_sem=send.at[sem], recv_sem=recv.at[sem],
                                        device_id=_peer(k), device_id_type=pl.DeviceIdType.MESH)


def _exchange_start(srcs, lands, after, *, scatter, slot=None, name):
    n = len(srcs)

    def body(*refs):
        src_refs, land_refs = refs[:n], refs[n:2 * n]
        send, recv, token = refs[2 * n + 1], refs[2 * n + 2], refs[-1]
        for a in range(n):
            for k in range(1, N_DEV):
                _split_copy(src_refs[a], land_refs[a], send, recv, a, k, scatter, slot, True).start()
        token[...] = jnp.zeros_like(token)

    hbm = lambda t: pltpu.HBM(t.shape, t.dtype)
    sems = pltpu.SemaphoreType.DMA((n * (N_DEV - 1),))
    out = pl.pallas_call(
        body, name=name,
        out_shape=(sems, sems, *[hbm(t) for t in srcs], *[hbm(t) for t in lands], jax.ShapeDtypeStruct((8, LANE), F32)),
        in_specs=[_HBM] * (2 * n) + [pl.BlockSpec(memory_space=pl.ANY)],
        out_specs=(_SEM, _SEM, *[_HBM] * (2 * n), pl.BlockSpec(memory_space=pltpu.VMEM)),
        input_output_aliases={i: 2 + i for i in range(2 * n)},
        compiler_params=pltpu.CompilerParams(has_side_effects=_EFFECT),
    )(*[pltpu.with_memory_space_constraint(t, pltpu.HBM) for t in (*srcs, *lands)], after)
    return out[0], out[1], out[2:2 + n], out[2 + n:2 + 2 * n], out[-1]


def _exchange_wait(send, recv, srcs, lands, after, *, scatter, slot=None, name):
    n = len(srcs)

    def body(*refs):
        src_refs, land_refs = refs[:n], refs[n:2 * n]
        send_ref, recv_ref = refs[2 * n], refs[2 * n + 1]
        for a in range(n):
            for k in range(1, N_DEV):
                _split_copy(src_refs[a], land_refs[a], send_ref, recv_ref, a, k, scatter, slot, True).wait_send()
                _split_copy(src_refs[a], land_refs[a], send_ref, recv_ref, a, k, scatter, slot, False).wait_recv()

    hbm = lambda t: pltpu.HBM(t.shape, t.dtype)
    out = pl.pallas_call(
        body, name=name, out_shape=(*[hbm(t) for t in srcs], *[hbm(t) for t in lands]),
        in_specs=[_HBM] * (2 * n) + [_SEM, _SEM, pl.BlockSpec(memory_space=pl.ANY)],
        out_specs=tuple([_HBM] * (2 * n)), input_output_aliases={i: i for i in range(2 * n)},
        compiler_params=pltpu.CompilerParams(has_side_effects=_EFFECT),
    )(*srcs, *lands, send, recv, after)
    return out[:n], out[n:]


def _win_to_mine(wt):
    pad = jnp.zeros((IN_PAD - IN_DIM,) + wt.shape[1:], wt.dtype)
    return jnp.concatenate([wt[3592:5640], wt[0:3584], wt[3584:3592], pad], axis=0)


def _win_from_mine(gt):
    return jnp.concatenate([gt[2048:5632], gt[5632:5640], gt[0:2048]], axis=0)


def _pad_rows(a, mult=8):
    r = (-a.shape[0]) % mult
    return a if r == 0 else jnp.concatenate([a, jnp.zeros((r,) + a.shape[1:], a.dtype)], axis=0)


def _lanes(vec, start):
    return jnp.zeros((1, LANE), F32).at[0, start:start + vec.shape[0]].set(vec)


def _small_spec(depth):
    return (("b_ada", (depth, 6 * D)), ("norm1_g", (depth, D)), ("norm2_g", (depth, D)),
            ("rel_table", (depth, A_HEADS, 2 * A_MAX_REL + 1)), ("a_log", (depth, B_HEADS)),
            ("dt_bias", (depth, B_HEADS)), ("gdn_norm_g", (depth, B_DH)), ("final_g", (D,)))


def _pack_small(d, extra, depth):
    spec = _small_spec(depth)
    rows = -(-(sum(math.prod(s) for _, s in spec) + 1) // (8 * LANE)) * 8
    flat = jnp.concatenate([d[n].reshape(-1).astype(F32) for n, _ in spec] + [extra.reshape(-1)])
    flat = jnp.concatenate([flat, jnp.zeros((rows * LANE - flat.shape[0],), F32)])
    return flat.reshape(rows, LANE)


def _unpack_small(p, depth):
    flat = p.reshape(-1)
    out, off = {}, 0
    for n, s in _small_spec(depth):
        sz = math.prod(s)
        out[n] = flat[off:off + sz].reshape(s)
        off += sz
    return out, flat[off]


def kernel(x, c, w_ada, b_ada, norm1_g, norm2_g, w_in, rel_table, w_conv, a_log, dt_bias, gdn_norm_g, w_branch_a, w_branch_b, w_out, w_ffn_in, w_ffn_out, final_g, loss_target, m_w_ada, m_b_ada, m_norm1_g, m_norm2_g, m_w_in, m_rel_table, m_w_conv, m_a_log, m_dt_bias, m_gdn_norm_g, m_w_branch_a, m_w_branch_b, m_w_out, m_w_ffn_in, m_w_ffn_out, m_final_g, v_w_ada, v_b_ada, v_norm1_g, v_norm2_g, v_w_in, v_rel_table, v_w_conv, v_a_log, v_dt_bias, v_gdn_norm_g, v_w_branch_a, v_w_branch_b, v_w_out, v_w_ffn_in, v_w_ffn_out, v_final_g):
    tt = x.shape[1]
    x0 = x[0]
    tgt = loss_target[0]
    me = _my_index()
    depth = w_in.shape[0]

    tr_ = lambda t: jnp.transpose(t, (0, 2, 1))
    shards = [tr_(w_in).astype(BF16), w_branch_a.astype(BF16), w_branch_b.astype(BF16), w_out.astype(BF16),
              tr_(w_ffn_in).astype(BF16), w_ffn_out.astype(BF16), w_conv]
    names = ("win", "wa", "wb", "wout", "wfi", "wfo", "wconv")
    early, late, every = (0, 6), (1, 2, 3, 4, 5), tuple(range(7))
    first = _all_gather([shards[i][0] for i in early] + [_pad_rows(c)], name="gather_first")
    c_all = first[-1][:, 0, :]
    is_me = lax.broadcasted_iota(jnp.int32, (N_DEV, 1, 1), 0) == me

    def unpack(idx, g):
        cols = lambda t: jnp.transpose(t, (1, 0, 2)).reshape(t.shape[1], N_DEV * t.shape[2])
        rows = lambda t: t.reshape(N_DEV * t.shape[1], t.shape[2])
        how = (lambda t: _win_to_mine(rows(t)), cols, cols, rows, rows, rows, cols)
        return {names[i]: how[i](t) for i, t in zip(idx, g)}

    def gather_start(l, idx, after, tag=""):
        srcs = [shards[i][l] for i in idx]
        lands = [lax.empty((N_DEV,) + t.shape, t.dtype) for t in srcs]
        return _exchange_start(srcs, lands, after, scatter=False, name=f"gather_start_{l}{tag}")

    def gather_wait(l, idx, pending, after, tag=""):
        send, recv, srcs, lands, _ = pending
        srcs, lands = _exchange_wait(send, recv, srcs, lands, after, scatter=False, name=f"gather_wait_{l}{tag}")
        return unpack(idx, [jnp.where(is_me, t[None], g) for g, t in zip(lands, srcs)])

    weights = [unpack(early, first[:-1])] + [None] * (depth - 1)
    pending0 = gather_start(0, late, first[-1], "_rest")
    pending = gather_start(1, every, pending0[-1]) if depth > 1 else None
    cond = c_all * (1.0 / (1.0 + jnp.exp(-c_all)))
    cond = _pad_rows(cond, 16)

    mod_cols = jnp.stack([_mm(cond, w_ada[l], name="mod_mm")[:N_DEV] for l in range(depth)])
    (g_mod,) = _all_gather([mod_cols], name="gather_mod")
    mod_all = jnp.transpose(g_mod, (1, 2, 0, 3)).reshape(depth, N_DEV, 6 * D)
    mod = lax.dynamic_index_in_dim(mod_all, me, axis=1, keepdims=False) + b_ada
    mods = mod.reshape(depth, 6, 1, D)

    n1g, n2g = norm1_g.reshape(depth, 1, D), norm2_g.reshape(depth, 1, D)
    gng = gdn_norm_g.reshape(depth, 1, B_DH)
    fg = final_g.reshape(1, D)

    saved = []
    tok = (pending if pending is not None else pending0)[-1][0, 0]
    xin, h1 = _adaln_fwd(x0, n1g[0], mods[0, 1] + tok, mods[0, 0], name="adaln1_first")
    for l in range(depth):
        sh1, sc1, gt1, sh2, sc2, gt2 = (mods[l, i] for i in range(6))
        wl = weights[l]
        proj = _mm(h1, wl["win"], tb=True, name="proj_mm", tn=1152)
        kpad, vpad = _kv_pad(proj, name="kv_pad")
        diag, bias_vjp = jax.vjp(_bias_diagonals, rel_table[l])
        bias = _bias_fwd(diag, name="bias_fwd")
        ya, lse = _attn_fwd(proj, kpad, vpad, bias, name="attn_fwd")
        par = jnp.concatenate([_lanes(a_log[l], B_HEADS), _lanes(dt_bias[l], B_HEADS), jnp.zeros((6, LANE), F32)], axis=0)
        qn, kn, vn, aux = _gdn_pre_fwd(proj, wl["wconv"], par, name="gdn_pre_fwd")
        auxt = _gc_rows(aux, tt // CH)
        lower = _gdn_lower(kn, aux, auxt, name="gdn_lower")
        tinv = jnp.transpose(_tri_inverse(jnp.transpose(lower, (1, 2, 0)), name="gdn_tri_inverse"), (2, 0, 1))
        u0, wg, qd, kd, qk, gle = _gdn_intra(qn, kn, vn, aux, auxt, tinv, name="gdn_intra")
        og, ss, ug = _gdn_scan_fwd(u0, wg, qd, kd, qk, gle, name="gdn_scan_fwd")
        yb = _gdn_post_fwd(og, proj, gng[l], name="gdn_post_fwd")
        if l == 0:
            wl.update(gather_wait(0, late, pending0, yb, "_rest"))
        pa, pb, merged = _branch_merge(ya, yb, wl["wa"], wl["wb"], proj, name="branch_merge")
        t1, x2, h2 = _out_adaln(merged, wl["wout"], xin, gt1, n2g[l], sc2, sh2, name="out_adaln2")
        gu, act = _ffn_in_swiglu(h2, wl["wfi"], name="ffn_in_swiglu")
        t2 = _mm(act, wl["wfo"], name="ffn_out_mm", tk=1408)
        saved.append(dict(xin=xin, h1=h1, proj=proj, kpad=kpad, vpad=vpad, bias=bias, bias_vjp=bias_vjp, ya=ya, lse=lse,
                          par=par, qn=qn, kn=kn, vn=vn, aux=aux, auxt=auxt, tinv=tinv, ss=ss, og=og, yb=yb, pa=pa, pb=pb,
                          u0=u0, wg=wg, qd=qd, kd=kd, qk=qk, gle=gle, ug=ug,
                          merged=merged, t1=t1, x2=x2, h2=h2, gu=gu, act=act, t2=t2))
        if l + 1 < depth:
            weights[l + 1] = gather_wait(l + 1, every, pending, t2)
            pending = gather_start(l + 2, every, weights[l + 1]["wconv"]) if l + 2 < depth else None
            tok = pending[-1][0, 0] if pending is not None else 0.0
            xin, h1 = _adaln_fwd(x2, n1g[l + 1], mods[l + 1, 1] + tok, mods[l + 1, 0], t2, gt2, name="adaln1_fwd")

    s = saved[-1]
    dx, dt2, st = _loss_head(s["x2"], s["t2"], mods[depth - 1, 5], fg, tgt, name="loss_head")
    loss_part = st[4, 0]
    small_g = {"final_g": st[0]}
    dmod_rows = [None] * depth
    for n in ("norm1_g", "norm2_g", "rel_table", "a_log", "dt_bias", "gdn_norm_g"):
        small_g[n] = [None] * depth
    dgt2 = st[3]
    cols_slabs = lambda g: jnp.transpose(g.reshape(g.shape[0], N_DEV, g.shape[1] // N_DEV), (1, 0, 2))
    rows_slabs = lambda g: g.reshape(N_DEV, g.shape[0] // N_DEV, g.shape[1])
    mix, ffn = (0, 1, 2, 3, 6), (4, 5)
    lands = {kind: [lax.empty((N_DEV,) + shards[i].shape, shards[i].dtype) for i in idx]
             for kind, idx in (("mix", mix), ("ffn", ffn))}
    own = {kind: [None] * depth for kind in lands}
    pending_s = {kind: None for kind in lands}

    def scatter(kind, l, srcs, after):
        if pending_s[kind] is not None:
            done, lands[kind] = _exchange_wait(*pending_s[kind][:4], after, scatter=True, slot=l + 1,
                                               name=f"scatter_wait_{kind}_{l + 1}")
            own[kind][l + 1] = [lax.dynamic_index_in_dim(t, me, 0, keepdims=False) for t in done]
        pending_s[kind] = _exchange_start(srcs, lands[kind], after, scatter=True, slot=l, name=f"scatter_start_{kind}_{l}")
        return pending_s[kind][-1][0, 0]

    for l in reversed(range(depth)):
        s, wl = saved[l], weights[l]
        sh1, sc1, gt1, sh2, sc2, gt2 = (mods[l, i] for i in range(6))
        gw_fo = _mm(s["act"], dt2, ta=True, out_dtype=BF16, name="ffn_out_dw", tm=1408)
        dgu = _ffn_out_bwd_swiglu(dt2, wl["wfo"], s["gu"], name="ffn_out_bwd_swiglu")
        gw_fi = _mm(dgu, s["h2"], ta=True, out_dtype=BF16, name="ffn_in_dw", tm=1408)
        sc2 = sc2 + scatter("ffn", l, [rows_slabs(gw_fi), rows_slabs(gw_fo)], gw_fi)
        dh2 = _mm(dgu, wl["wfi"], name="ffn_in_dx", tk=1408)
        dx, dt1, st2 = _adaln_bwd(s["x2"], n2g[l], sc2, sh2, dh2, dx, s["t1"], gt1, name="adaln2_bwd")
        gw_out = _mm(s["merged"], dt1, ta=True, out_dtype=BF16, name="out_dw")
        dgates, dpa, dpb = _out_bwd_merge(dt1, wl["wout"], s["proj"], s["pa"], s["pb"], name="out_bwd_merge")
        gw_a = _mm(s["ya"], dpa, ta=True, out_dtype=BF16, name="branch_a_dw")
        gw_b = _mm(s["yb"], dpb, ta=True, out_dtype=BF16, name="branch_b_dw")
        dya = _mm(dpa, wl["wa"], tb=True, name="branch_a_dx")
        dyb = _mm(dpb, wl["wb"], tb=True, name="branch_b_dx")
        dqa, dka, dva, dbias = _attn_bwd(s["proj"], s["kpad"], s["vpad"], s["bias"], s["ya"], s["lse"], dya,
                                             name="attn_bwd")
        ddiag = jnp.roll(_bias_bwd(dbias, name="bias_bwd"), -(QB - 1), axis=1)
        small_g["rel_table"][l] = s["bias_vjp"](ddiag)[0]
        dog, dz, dgn = _gdn_post_bwd(s["og"], s["proj"], gng[l], dyb, name="gdn_post_bwd")
        small_g["gdn_norm_g"][l] = dgn[0]
        dug, dss = _gdn_scan_bwd(s["wg"], s["qd"], s["kd"], s["qk"], s["gle"], dog, name="gdn_scan_bwd")
        dqn, dkn, dvn, daux = _gdn_bwd(s["qn"], s["kn"], s["vn"], s["aux"], s["auxt"], s["tinv"], s["u0"], s["wg"],
                                       s["ug"], s["ss"], dss, dug, dog, name="gdn_bwd")
        dqkv, dba, dwc, dpar = _gdn_pre_bwd(s["proj"], wl["wconv"], s["par"], dqn, dkn, dvn, daux, name="gdn_pre_bwd")
        small_g["a_log"][l] = dpar[0, B_HEADS:2 * B_HEADS]
        small_g["dt_bias"][l] = dpar[1, B_HEADS:2 * B_HEADS]
        dproj = jnp.concatenate([dgates, dqa, dka, dva, dqkv, dz, dba], axis=1)
        gw_in = _mm(dproj, s["h1"], ta=True, out_dtype=BF16, name="proj_dw", tm=1152)
        dh1 = _mm(dproj, wl["win"], name="proj_dx", tk=1152)
        mix_srcs = [rows_slabs(_win_from_mine(gw_in)), cols_slabs(gw_a), cols_slabs(gw_b), rows_slabs(gw_out),
                    cols_slabs(dwc[0:CONV_K])]
        if l > 0:
            sc1 = sc1 + scatter("mix", l, mix_srcs, gw_in)
        if l > 0:
            p = saved[l - 1]
            dx, dt2, st1 = _adaln_bwd(s["xin"], n1g[l], sc1, sh1, dh1, dx, p["t2"], mods[l - 1, 5], name="adaln1_bwd")
        else:
            dx, st1 = _adaln_bwd(s["xin"], n1g[l], sc1, sh1, dh1, dx, name="adaln1_bwd_first")
        small_g["norm1_g"][l], small_g["norm2_g"][l] = st1[0], st2[0]
        dmod_rows[l] = jnp.concatenate([st1[2], st1[1], st2[3], st2[2], st2[1], dgt2])
        if l > 0:
            dgt2 = st1[3]
    grad_x = dx[None]

    small_local = {n: (jnp.stack(vs) if isinstance(vs, list) else vs) for n, vs in small_g.items()}
    small_local["b_ada"] = jnp.stack(dmod_rows)
    (g_small,) = _all_gather([_pack_small(small_local, loss_part, depth)], name="gather_small")
    tok = scatter("mix", 0, mix_srcs, g_small)
    wsm = _pack_small(dict(b_ada=b_ada, norm1_g=norm1_g, norm2_g=norm2_g, rel_table=rel_table, a_log=a_log,
                           dt_bias=dt_bias, gdn_norm_g=gdn_norm_g, final_g=final_g), jnp.zeros((1,), F32) + tok, depth)
    msm = _pack_small(dict(b_ada=m_b_ada, norm1_g=m_norm1_g, norm2_g=m_norm2_g, rel_table=m_rel_table, a_log=m_a_log,
                           dt_bias=m_dt_bias, gdn_norm_g=m_gdn_norm_g, final_g=m_final_g), jnp.zeros((1,), F32), depth)
    vsm = _pack_small(dict(b_ada=v_b_ada, norm1_g=v_norm1_g, norm2_g=v_norm2_g, rel_table=v_rel_table, a_log=v_a_log,
                           dt_bias=v_dt_bias, gdn_norm_g=v_gdn_norm_g, final_g=v_final_g), jnp.ones((1,), F32), depth)
    sm = [_unpack_small(t, depth) for t in _adamw(g_small[:, None], wsm[None], msm[None], vsm[None], name="adamw_small")]
    loss = sm[0][1]

    dmod_all = g_small.reshape(N_DEV, -1)[:, :depth * 6 * D].reshape(N_DEV, depth, 6 * D)
    dmod_mine = lax.dynamic_slice_in_dim(dmod_all, me * (6 * D // N_DEV), 6 * D // N_DEV, axis=2)
    g_ada = jnp.stack([_mm(cond, _pad_rows(dmod_mine[:, l], 16), ta=True, name="ada_dw") for l in range(depth)])

    got, mine = {}, {}
    sel = jnp.broadcast_to(jnp.where(is_me[:, :, 0], 1.0, 0.0), (N_DEV, LANE)).astype(F32)

    def finish(kind, idx, after):
        done, lands[kind] = _exchange_wait(*pending_s[kind][:4], after, scatter=True, slot=0, name=f"scatter_wait_{kind}_0")
        own[kind][0] = [lax.dynamic_index_in_dim(t, me, 0, keepdims=False) for t in done]
        for a, i in enumerate(idx):
            got[i] = lands[kind][a]
            mine[i] = jnp.stack([own[kind][l][a] for l in range(depth)])

    def upd(i, w, m, v, name):
        if i in (0, 4):
            return [tr_(t) for t in _adamw(got[i], tr_(w), tr_(m), tr_(v), mine[i], sel, name=name)]
        return _adamw(got[i], w, m, v, mine[i], sel, name=name)

    finish("ffn", ffn, g_ada)
    res = {
        "w_ada": _adamw(g_ada[None], w_ada, m_w_ada, v_w_ada, name="adamw_w_ada"),
        "w_ffn_in": upd(4, w_ffn_in, m_w_ffn_in, v_w_ffn_in, "adamw_w_ffn_in"),
        "w_ffn_out": upd(5, w_ffn_out, m_w_ffn_out, v_w_ffn_out, "adamw_w_ffn_out"),
    }
    finish("mix", mix, res["w_ffn_out"][0])
    res.update({
        "w_in": upd(0, w_in, m_w_in, v_w_in, "adamw_w_in"),
        "w_conv": upd(6, w_conv, m_w_conv, v_w_conv, "adamw_w_conv"),
        "w_branch_a": upd(1, w_branch_a, m_w_branch_a, v_w_branch_a, "adamw_w_branch_a"),
        "w_branch_b": upd(2, w_branch_b, m_w_branch_b, v_w_branch_b, "adamw_w_branch_b"),
        "w_out": upd(3, w_out, m_w_out, v_w_out, "adamw_w_out"),
    })
    for n, _ in _small_spec(depth):
        res[n] = [sm[i][0][n] for i in range(4)]
    order = ("w_ada", "b_ada", "norm1_g", "norm2_g", "w_in", "rel_table", "w_conv", "a_log", "dt_bias", "gdn_norm_g",
             "w_branch_a", "w_branch_b", "w_out", "w_ffn_in", "w_ffn_out", "final_g")
    return (loss, grad_x, *[res[n][0] for n in order], *[res[n][1] for n in order],
            *[res[n][2] for n in order], *[res[n][3] for n in order])
```

```python
import functools
import math

import jax
import jax.numpy as jnp
from jax import lax
from jax.experimental import pallas as pl
from jax.experimental.pallas import tpu as pltpu

F32 = jnp.float32
BF16 = jnp.bfloat16
HI = lax.Precision.HIGHEST

N_DEV = 8
D = 1024
DEPTH = 4
CH = 64
EPS = 1e-6
A_HEADS, A_DH = 8, 64
A_W = A_HEADS * A_DH
A_PAST = 8
A_MAX_REL = 128
QB = 256
KB = QB + A_PAST * CH
B_HEADS, B_DH = 4, 128
B_W = B_HEADS * B_DH
CONV_K = 4
FF = 2816
IN_DIM = 5640
IN_PAD = 5760
LANE = 128
NEG = -1e30
VMEM_LIMIT = 48 * 1024 * 1024

ADAM_LR, ADAM_B1, ADAM_B2, ADAM_EPS, ADAM_WD, ADAM_STEP = 0.001, 0.9, 0.999, 1e-08, 0.01, 10

OFF_GA, OFF_GB, OFF_QA, OFF_KA, OFF_VA, OFF_QB, OFF_KB, OFF_VB, OFF_ZB, OFF_BA = (
    0, 1024, 2048, 2560, 3072, 3584, 4096, 4608, 5120, 5632)


def _cp(sem=None):
    return pltpu.CompilerParams(dimension_semantics=sem, vmem_limit_bytes=VMEM_LIMIT)


def _tile(n, pref):
    if n <= pref:
        return n
    best = None
    for t in range(LANE, pref + 1, LANE):
        if n % t == 0:
            best = t
    assert best is not None, (n, pref)
    return best


def _sigmoid(x):
    return 1.0 / (1.0 + jnp.exp(-x))


def _silu(x):
    return x * _sigmoid(x)


def _dsilu(x):
    s = _sigmoid(x)
    return s * (1.0 + x * (1.0 - s))


def _dot(a, b, prec=None):
    return jnp.dot(a, b, preferred_element_type=F32, precision=prec)


def _dot_nt(a, b, prec=None):
    return lax.dot_general(a, b, (((1,), (1,)), ((), ())), preferred_element_type=F32, precision=prec)


def _dot_tn(a, b, prec=None):
    return lax.dot_general(a, b, (((0,), (0,)), ((), ())), preferred_element_type=F32, precision=prec)


def _mm(a, b, *, ta=False, tb=False, out_dtype=F32, name, tm=1024, tn=1024, tk=1024):
    halves = a.ndim == 3
    a_rows, a_cols = (a.shape[1], 2 * a.shape[2]) if halves else a.shape
    m, k = (a_cols, a_rows) if ta else (a_rows, a_cols)
    n = b.shape[0] if tb else b.shape[1]
    assert k == (b.shape[1] if tb else b.shape[0]), (a.shape, b.shape, ta, tb)
    tm, tn, tk = _tile(m, tm), _tile(n, tn), _tile(k, tk)
    nk = k // tk
    dn = (((0 if ta else 1,), (1 if tb else 0,)), ((), ()))

    def body(a_ref, b_ref, o_ref, *acc):
        part = lax.dot_general(a_ref[...].astype(BF16), b_ref[...].astype(BF16), dn, preferred_element_type=F32)
        if nk == 1:
            o_ref[...] = part.astype(out_dtype)
            return
        acc_ref, kk = acc[0], pl.program_id(2)

        @pl.when(kk == 0)
        def _():
            acc_ref[...] = part

        @pl.when(kk > 0)
        def _():
            acc_ref[...] += part

        @pl.when(kk == nk - 1)
        def _():
            o_ref[...] = acc_ref[...].astype(out_dtype)

    if halves:
        per = a.shape[2] // (tm if ta else tk)
        a_spec = (pl.BlockSpec((None, tk, tm), lambda i, j, q: (i // per, q, i % per)) if ta else
                  pl.BlockSpec((None, tm, tk), lambda i, j, q: (q // per, i, q % per)))
    else:
        a_spec = pl.BlockSpec((tk, tm), lambda i, j, q: (q, i)) if ta else pl.BlockSpec((tm, tk), lambda i, j, q: (i, q))
    b_spec = pl.BlockSpec((tn, tk), lambda i, j, q: (j, q)) if tb else pl.BlockSpec((tk, tn), lambda i, j, q: (q, j))
    return pl.pallas_call(
        body, grid=(m // tm, n // tn, nk), in_specs=[a_spec, b_spec],
        out_specs=pl.BlockSpec((tm, tn), lambda i, j, q: (i, j)),
        out_shape=jax.ShapeDtypeStruct((m, n), out_dtype),
        scratch_shapes=[pltpu.VMEM((tm, tn), F32)] if nk > 1 else [],
        compiler_params=_cp(("parallel", "parallel", "arbitrary")), name=name)(a, b)


def _rb(tr, width, cb=0):
    return pl.BlockSpec((tr, width), lambda i: (i, cb))


def _whole(shape):
    nd = len(shape)
    return pl.BlockSpec(shape, lambda i: (0,) * nd)


def _colsum(v):
    return jnp.sum(v, axis=0, keepdims=True)


def _adaln_fwd(x, g, sc, sh, t=None, gt=None, *, name, tr=256):
    tt = x.shape[0]
    res = t is not None

    def body(*refs):
        if res:
            x_ref, t_ref, gt_ref, g_ref, sc_ref, sh_ref, xo_ref, h_ref = refs
            xv = x_ref[...] + gt_ref[...] * t_ref[...]
            xo_ref[...] = xv
        else:
            x_ref, g_ref, sc_ref, sh_ref, h_ref = refs
            xv = x_ref[...]
        r = lax.rsqrt(jnp.mean(xv * xv, axis=-1, keepdims=True) + EPS)
        h_ref[...] = ((xv * r * g_ref[...]) * (1.0 + sc_ref[...]) + sh_ref[...]).astype(BF16)

    row, vec = _rb(tr, D), _whole((1, D))
    if res:
        ins, in_specs = (x, t, gt, g, sc, sh), [row, row, vec, vec, vec, vec]
        out_shape = (jax.ShapeDtypeStruct((tt, D), F32), jax.ShapeDtypeStruct((tt, D), BF16))
        out_specs = (row, row)
    else:
        ins, in_specs = (x, g, sc, sh), [row, vec, vec, vec]
        out_shape, out_specs = jax.ShapeDtypeStruct((tt, D), BF16), row
    out = pl.pallas_call(body, grid=(tt // tr,), in_specs=in_specs, out_specs=out_specs, out_shape=out_shape,
                         compiler_params=_cp(("parallel",)), name=name)(*ins)
    return out if res else (x, out)


def _adaln_bwd(x, g, sc, sh, dh, dx_in, t=None, gt=None, *, name, tr=256):
    tt = x.shape[0]
    res = t is not None

    def body(*refs):
        if res:
            x_ref, g_ref, sc_ref, sh_ref, dh_ref, dxi_ref, t_ref, gt_ref, dx_ref, dt_ref, st_ref = refs
        else:
            x_ref, g_ref, sc_ref, sh_ref, dh_ref, dxi_ref, dx_ref, st_ref = refs

        @pl.when(pl.program_id(0) == 0)
        def _():
            st_ref[...] = jnp.zeros_like(st_ref)

        xv, dh = x_ref[...], dh_ref[...]
        r = lax.rsqrt(jnp.mean(xv * xv, axis=-1, keepdims=True) + EPS)
        nrm = xv * r
        y = nrm * g_ref[...]
        dy = dh * (1.0 + sc_ref[...])
        dn = dy * g_ref[...]
        dx = dxi_ref[...] + r * (dn - nrm * jnp.mean(dn * nrm, axis=-1, keepdims=True))
        dx_ref[...] = dx
        st_ref[0:1, :] += _colsum(dy * nrm)
        st_ref[1:2, :] += _colsum(dh * y)
        st_ref[2:3, :] += _colsum(dh)
        if res:
            dt_ref[...] = (gt_ref[...] * dx).astype(BF16)
            st_ref[3:4, :] += _colsum(dx * t_ref[...])

    row, vec, st = _rb(tr, D), _whole((1, D)), _whole((8, D))
    ins, in_specs = [x, g, sc, sh, dh, dx_in], [row, vec, vec, vec, row, row]
    out_shape, out_specs = [jax.ShapeDtypeStruct((tt, D), F32)], [row]
    if res:
        ins += [t, gt]
        in_specs += [row, vec]
        out_shape.append(jax.ShapeDtypeStruct((tt, D), BF16))
        out_specs.append(row)
    out_shape.append(jax.ShapeDtypeStruct((8, D), F32))
    out_specs.append(st)
    return pl.pallas_call(body, grid=(tt // tr,), in_specs=in_specs, out_specs=tuple(out_specs),
                          out_shape=tuple(out_shape), compiler_params=_cp(("arbitrary",)), name=name)(*ins)


def _loss_head(x, t, gt, fg, tgt, *, name, tr=256):
    tt = x.shape[0]

    def body(x_ref, t_ref, gt_ref, fg_ref, tgt_ref, dx_ref, dt_ref, st_ref):
        @pl.when(pl.program_id(0) == 0)
        def _():
            st_ref[...] = jnp.zeros_like(st_ref)

        tv = t_ref[...]
        xv = x_ref[...] + gt_ref[...] * tv
        r = lax.rsqrt(jnp.mean(xv * xv, axis=-1, keepdims=True) + EPS)
        nrm = xv * r
        err = nrm * fg_ref[...] - tgt_ref[...]
        st_ref[4:5, :] += 0.5 * jnp.sum(jnp.mean(err * err, axis=-1, keepdims=True), axis=0, keepdims=True)
        dy = err * (1.0 / D)
        dn = dy * fg_ref[...]
        dx = r * (dn - nrm * jnp.mean(dn * nrm, axis=-1, keepdims=True))
        dx_ref[...] = dx
        dt_ref[...] = (gt_ref[...] * dx).astype(BF16)
        st_ref[0:1, :] += _colsum(dy * nrm)
        st_ref[3:4, :] += _colsum(dx * tv)

    row, vec = _rb(tr, D), _whole((1, D))
    return pl.pallas_call(
        body, grid=(tt // tr,), in_specs=[row, row, vec, vec, row], out_specs=(row, row, _whole((8, D))),
        out_shape=(jax.ShapeDtypeStruct((tt, D), F32), jax.ShapeDtypeStruct((tt, D), BF16),
                   jax.ShapeDtypeStruct((8, D), F32)),
        compiler_params=_cp(("arbitrary",)), name=name)(x, t, gt, fg, tgt)


def _branch_merge(ya, yb, wa, wb, proj, *, name, tm=512):
    tt = ya.shape[0]
    tm = _tile(tt, tm)

    def body(ya_ref, yb_ref, wa_ref, wb_ref, ga_ref, gb_ref, pa_ref, pb_ref, o_ref):
        pa = _dot(ya_ref[...], wa_ref[...])
        pb = _dot(yb_ref[...], wb_ref[...])
        pa_ref[...] = pa.astype(BF16)
        pb_ref[...] = pb.astype(BF16)
        o_ref[...] = (_sigmoid(ga_ref[...]) * pa + _sigmoid(gb_ref[...]) * pb).astype(BF16)

    row, half, wsp = _rb(tm, D), _rb(tm, A_W), _whole((A_W, D))
    out = jax.ShapeDtypeStruct((tt, D), BF16)
    return pl.pallas_call(body, grid=(tt // tm,), in_specs=[half, half, wsp, wsp, _rb(tm, D, 0), _rb(tm, D, 1)],
                          out_specs=(row, row, row), out_shape=(out, out, out), compiler_params=_cp(("parallel",)),
                          name=name)(ya, yb, wa, wb, proj, proj)


def _out_adaln(a, w, x, gt, g, sc, sh, *, name, tm=512):
    tt = a.shape[0]
    tm = _tile(tt, tm)

    def body(a_ref, w_ref, x_ref, gt_ref, g_ref, sc_ref, sh_ref, t_ref, xo_ref, h_ref):
        t = _dot(a_ref[...], w_ref[...])
        t_ref[...] = t
        xv = x_ref[...] + gt_ref[...] * t
        xo_ref[...] = xv
        r = lax.rsqrt(jnp.mean(xv * xv, axis=-1, keepdims=True) + EPS)
        h_ref[...] = ((xv * r * g_ref[...]) * (1.0 + sc_ref[...]) + sh_ref[...]).astype(BF16)

    row, vec = _rb(tm, D), _whole((1, D))
    f32 = jax.ShapeDtypeStruct((tt, D), F32)
    return pl.pallas_call(body, grid=(tt // tm,), in_specs=[row, _whole((D, D)), row, vec, vec, vec, vec],
                          out_specs=(row, row, row), out_shape=(f32, f32, jax.ShapeDtypeStruct((tt, D), BF16)),
                          compiler_params=_cp(("parallel",)), name=name)(a, w, x, gt, g, sc, sh)


def _out_bwd_merge(dt, wout, proj, pa, pb, *, name, tm=512):
    tt = dt.shape[0]
    tm = _tile(tt, tm)

    def body(dt_ref, w_ref, ga_ref, gb_ref, pa_ref, pb_ref, dg_ref, dpa_ref, dpb_ref):
        dm_v = _dot_nt(dt_ref[...], w_ref[...])
        sa, sb = _sigmoid(ga_ref[...]), _sigmoid(gb_ref[...])
        dpa_ref[...] = (dm_v * sa).astype(BF16)
        dpb_ref[...] = (dm_v * sb).astype(BF16)
        dg_ref[:, 0:D] = (dm_v * pa_ref[...].astype(F32) * sa * (1.0 - sa)).astype(BF16)
        dg_ref[:, D:2 * D] = (dm_v * pb_ref[...].astype(F32) * sb * (1.0 - sb)).astype(BF16)

    row = _rb(tm, D)
    return pl.pallas_call(
        body, grid=(tt // tm,), in_specs=[row, _whole((D, D)), _rb(tm, D, 0), _rb(tm, D, 1), row, row],
        out_specs=(_rb(tm, 2 * D), row, row),
        out_shape=(jax.ShapeDtypeStruct((tt, 2 * D), BF16), jax.ShapeDtypeStruct((tt, D), BF16),
                   jax.ShapeDtypeStruct((tt, D), BF16)),
        compiler_params=_cp(("parallel",)), name=name)(dt, wout, proj, proj, pa, pb)


FTN = FF // 2


def _ffn_in_swiglu(h, wt, *, name, tm=1024):
    tt = h.shape[0]
    tm = _tile(tt, tm)

    def body(h_ref, wg_ref, wu_ref, gu_ref, act_ref):
        hv = h_ref[...]
        g = _dot_nt(hv, wg_ref[...])
        u = _dot_nt(hv, wu_ref[...])
        gu_ref[0] = g.astype(BF16)
        gu_ref[1] = u.astype(BF16)
        act_ref[...] = (_silu(g) * u).astype(BF16)

    nj = FF // FTN
    return pl.pallas_call(
        body, grid=(tt // tm, nj),
        in_specs=[pl.BlockSpec((tm, D), lambda i, j: (i, 0)), pl.BlockSpec((FTN, D), lambda i, j: (j, 0)),
                  pl.BlockSpec((FTN, D), lambda i, j: (j + nj, 0))],
        out_specs=(pl.BlockSpec((2, tm, FTN), lambda i, j: (0, i, j)), pl.BlockSpec((tm, FTN), lambda i, j: (i, j))),
        out_shape=(jax.ShapeDtypeStruct((2, tt, FF), BF16), jax.ShapeDtypeStruct((tt, FF), BF16)),
        compiler_params=_cp(("parallel", "parallel")), name=name)(h, wt, wt)


def _ffn_out_bwd_swiglu(dt, wo, gu, *, name, tm=1024):
    tt = dt.shape[0]
    tm = _tile(tt, tm)

    def body(dt_ref, wo_ref, gu_ref, dgu_ref):
        da = _dot_nt(dt_ref[...], wo_ref[...])
        g, u = gu_ref[0].astype(F32), gu_ref[1].astype(F32)
        dgu_ref[0] = (da * u * _dsilu(g)).astype(BF16)
        dgu_ref[1] = (da * _silu(g)).astype(BF16)

    blk = pl.BlockSpec((2, tm, FTN), lambda i, j: (0, i, j))
    return pl.pallas_call(
        body, grid=(tt // tm, FF // FTN),
        in_specs=[pl.BlockSpec((tm, D), lambda i, j: (i, 0)), pl.BlockSpec((FTN, D), lambda i, j: (j, 0)), blk],
        out_specs=blk, out_shape=jax.ShapeDtypeStruct((2, tt, FF), BF16),
        compiler_params=_cp(("parallel", "parallel")), name=name)(dt, wo, gu)


BIAS_LW = 1152


def _bias_diagonals(table):
    n_far = A_PAST * CH - A_MAX_REL + 1
    far = jnp.broadcast_to(table[:, 2 * A_MAX_REL:], (A_HEADS, n_far))
    mid = jnp.flip(table[:, 1:2 * A_MAX_REL], axis=1)
    near = jnp.broadcast_to(table[:, 0:1], (A_HEADS, KB - n_far - (2 * A_MAX_REL - 1)))
    pos = jnp.concatenate([far, mid, near], axis=1)
    neg = jnp.broadcast_to(table[:, 2 * A_MAX_REL:], (A_HEADS, QB - 1))
    gap = jnp.zeros((A_HEADS, BIAS_LW - KB - (QB - 1)), F32)
    return jnp.concatenate([pos, gap, neg], axis=1)


def _bias_fwd(diag, *, name):
    def body(w_ref, o_ref):
        qc = lax.broadcasted_iota(jnp.int32, (QB, KB), 0) // CH + A_PAST
        col = lax.broadcasted_iota(jnp.int32, (QB, KB), 1)
        inband = (col // CH <= qc) & (col // CH >= qc - A_PAST)
        for h in range(A_HEADS):
            rows = pltpu.roll(jnp.broadcast_to(w_ref[h:h + 1, :], (QB, BIAS_LW)), 0, 1, stride=1, stride_axis=0)
            for var in range(3):
                o_ref[var, h] = jnp.where(inband & (col >= A_PAST * CH - QB * var), rows[:, :KB], NEG)

    return pl.pallas_call(body, out_shape=jax.ShapeDtypeStruct((3, A_HEADS, QB, KB), F32), compiler_params=_cp(),
                          name=name)(diag)


def _bias_bwd(dbias, *, name):
    def body(d_ref, o_ref):
        r = lax.broadcasted_iota(jnp.int32, (QB, QB), 0)
        c = lax.broadcasted_iota(jnp.int32, (QB, QB), 1)
        flip = jnp.where(r + c == QB - 1, 1.0, 0.0).astype(F32)
        for h in range(A_HEADS):
            x = jnp.concatenate([_dot(flip, d_ref[h], HI), jnp.zeros((QB, BIAS_LW - KB), F32)], axis=1)
            o_ref[h:h + 1, :] = jnp.sum(pltpu.roll(x, 0, 1, stride=1, stride_axis=0), axis=0, keepdims=True)

    return pl.pallas_call(body, out_shape=jax.ShapeDtypeStruct((A_HEADS, BIAS_LW), F32), compiler_params=_cp(),
                          name=name)(dbias)


def _kv_pad(proj, *, name, tr=256):
    tt = proj.shape[0]
    npad = A_PAST * CH // tr

    def body(k_ref, v_ref, ko_ref, vo_ref):
        i = pl.program_id(0)

        @pl.when(i < npad)
        def _():
            ko_ref[...] = jnp.zeros_like(ko_ref)
            vo_ref[...] = jnp.zeros_like(vo_ref)

        @pl.when(i >= npad)
        def _():
            ko_ref[...] = k_ref[...].astype(BF16)
            vo_ref[...] = v_ref[...].astype(BF16)

    src = lambda off: pl.BlockSpec((tr, A_W), lambda i: (jnp.maximum(i - npad, 0), off // A_W))
    out = jax.ShapeDtypeStruct((tt + A_PAST * CH, A_W), BF16)
    return pl.pallas_call(body, grid=(tt // tr + npad,), in_specs=[src(OFF_KA), src(OFF_VA)],
                          out_specs=(_rb(tr, A_W), _rb(tr, A_W)), out_shape=(out, out),
                          compiler_params=_cp(("parallel",)), name=name)(proj, proj)


ROWG = 16


def _attn_fwd(proj, kpad, vpad, bias, *, name):
    tt = proj.shape[0]

    def body(q_ref, k_ref, v_ref, b_ref, o_ref, l_ref, s_scr, p_scr, col_scr):
        q0 = pl.multiple_of(pl.program_id(1) * QB, QB)
        q = q_ref[...] * (A_DH ** -0.5)
        k = k_ref[pl.ds(q0, KB), :]
        v = v_ref[pl.ds(q0, KB), :]
        lane = lax.broadcasted_iota(jnp.int32, (QB, LANE), 1)
        o = jnp.zeros((QB, LANE), F32)
        lse = jnp.zeros((QB, LANE), F32)
        for a in range(2):
            hm = (lane >= A_DH * a) & (lane < A_DH * (a + 1))
            s_scr[...] = _dot_nt(jnp.where(hm, q, 0.0).astype(BF16), k)

            def rows(g, carry):
                r = pl.ds(pl.multiple_of(g * ROWG, ROWG), ROWG)
                s = s_scr[r, :] + b_ref[a, r, :]
                m = jnp.max(s, axis=-1, keepdims=True)
                p = jnp.exp(s - m)
                l = jnp.sum(p, axis=-1, keepdims=True)
                p_scr[r, :] = p.astype(BF16)
                col_scr[0, r, :] = jnp.broadcast_to(1.0 / l, (ROWG, LANE))
                col_scr[1, r, :] = jnp.broadcast_to(m + jnp.log(l), (ROWG, LANE))
                return carry

            lax.fori_loop(0, QB // ROWG, rows, 0)
            o = jnp.where(hm, _dot(p_scr[...], v) * col_scr[0], o)
            lse = jnp.where(hm, col_scr[1], lse)
        o_ref[...] = o.astype(BF16)
        l_ref[...] = lse

    kv = pl.BlockSpec((tt + A_PAST * CH, LANE), lambda h, i: (0, h))
    blk = pl.BlockSpec((QB, LANE), lambda h, i: (i, h))
    return pl.pallas_call(
        body, grid=(A_W // LANE, tt // QB),
        in_specs=[pl.BlockSpec((QB, LANE), lambda h, i: (i, OFF_QA // LANE + h)), kv, kv,
                  pl.BlockSpec((None, 2, QB, KB), lambda h, i: (jnp.minimum(i, 2), h, 0, 0))],
        out_specs=(blk, blk),
        out_shape=(jax.ShapeDtypeStruct((tt, A_W), BF16), jax.ShapeDtypeStruct((tt, A_W), F32)),
        scratch_shapes=[pltpu.VMEM((QB, KB), F32), pltpu.VMEM((QB, KB), BF16), pltpu.VMEM((2, QB, LANE), F32)],
        compiler_params=_cp(("parallel", "parallel")), name=name)(proj, kpad, vpad, bias)


def _attn_bwd(proj, kpad, vpad, bias, o, lse, do, *, name):
    tt = proj.shape[0]
    nq = tt // QB

    def body(q_ref, k_ref, v_ref, b_ref, o_ref, l_ref, do_ref, dq_ref, dko_ref, dvo_ref, db_ref, dk_ref, dv_ref,
             s_scr, dp_scr, p_scr, ds_scr, col_scr):
        @pl.when(pl.program_id(1) == 0)
        def _():
            dk_ref[...] = jnp.zeros_like(dk_ref)
            dv_ref[...] = jnp.zeros_like(dv_ref)
            db_ref[...] = jnp.zeros_like(db_ref)

        q0 = pl.multiple_of(pl.program_id(1) * QB, QB)
        q, do_v, lse = q_ref[...] * (A_DH ** -0.5), do_ref[...], l_ref[...]
        k = k_ref[pl.ds(q0, KB), :]
        v = v_ref[pl.ds(q0, KB), :]
        dsum = do_v * o_ref[...].astype(F32)
        lane = lax.broadcasted_iota(jnp.int32, (QB, LANE), 1)
        dq = jnp.zeros((QB, LANE), F32)
        dk = jnp.zeros((KB, LANE), F32)
        dv = jnp.zeros((KB, LANE), F32)
        for a in range(2):
            hm = (lane >= A_DH * a) & (lane < A_DH * (a + 1))
            qa = jnp.where(hm, q, 0.0).astype(BF16)
            doa = jnp.where(hm, do_v, 0.0).astype(BF16)
            s_scr[...] = _dot_nt(qa, k)
            dp_scr[...] = _dot_nt(doa, v)
            col_scr[0] = jnp.broadcast_to(jnp.max(jnp.where(hm, lse, NEG), axis=-1, keepdims=True), (QB, LANE))
            col_scr[1] = jnp.broadcast_to(jnp.sum(jnp.where(hm, dsum, 0.0), axis=-1, keepdims=True), (QB, LANE))

            def rows(g, carry):
                r = pl.ds(pl.multiple_of(g * ROWG, ROWG), ROWG)
                p = jnp.exp(s_scr[r, :] + b_ref[a, r, :] - col_scr[0, r, 0:1])
                ds = p * (dp_scr[r, :] - col_scr[1, r, 0:1])
                db_ref[a, r, :] += ds
                p_scr[r, :] = p.astype(BF16)
                ds_scr[r, :] = ds.astype(BF16)
                return carry

            lax.fori_loop(0, QB // ROWG, rows, 0)
            dsb = ds_scr[...]
            dq = jnp.where(hm, _dot(dsb, k) * (A_DH ** -0.5), dq)
            dk += _dot_tn(dsb, qa)
            dv += _dot_tn(p_scr[...], doa)
        dq_ref[...] = dq.astype(BF16)
        dk_ref[pl.ds(q0, KB), :] += dk
        dv_ref[pl.ds(q0, KB), :] += dv

        @pl.when(pl.program_id(1) == nq - 1)
        def _():
            dko_ref[...] = dk_ref[A_PAST * CH:, :].astype(BF16)
            dvo_ref[...] = dv_ref[A_PAST * CH:, :].astype(BF16)

    kv = pl.BlockSpec((tt + A_PAST * CH, LANE), lambda h, i: (0, h))
    blk = pl.BlockSpec((QB, LANE), lambda h, i: (i, h))
    col = pl.BlockSpec((tt, LANE), lambda h, i: (0, h))
    bsp = pl.BlockSpec((2, QB, KB), lambda h, i: (h, 0, 0))
    bias_in = pl.BlockSpec((None, 2, QB, KB), lambda h, i: (jnp.minimum(i, 2), h, 0, 0))
    out = jax.ShapeDtypeStruct((tt, A_W), BF16)
    return pl.pallas_call(
        body, grid=(A_W // LANE, nq),
        in_specs=[pl.BlockSpec((QB, LANE), lambda h, i: (i, OFF_QA // LANE + h)), kv, kv, bias_in, blk, blk, blk],
        out_specs=(blk, col, col, bsp),
        out_shape=(out, out, out, jax.ShapeDtypeStruct((A_HEADS, QB, KB), F32)),
        scratch_shapes=[pltpu.VMEM((tt + A_PAST * CH, LANE), F32), pltpu.VMEM((tt + A_PAST * CH, LANE), F32),
                        pltpu.VMEM((QB, KB), F32), pltpu.VMEM((QB, KB), F32), pltpu.VMEM((QB, KB), BF16),
                        pltpu.VMEM((QB, KB), BF16), pltpu.VMEM((2, QB, LANE), F32)],
        compiler_params=_cp(("parallel", "arbitrary")), name=name)(proj, kpad, vpad, bias, o, lse, do)


GTR = 256


def _taps(w_ref, grp):
    return [w_ref[j:j + 1, grp * B_W:(grp + 1) * B_W] for j in range(CONV_K)]


def _shifts(xe, rows):
    return [xe[8:8 + rows]] + [pltpu.roll(xe, s, 0)[8:8 + rows] for s in range(1, CONV_K)]


def _conv(shifts, taps):
    acc = taps[CONV_K - 1] * shifts[0]
    for s in range(1, CONV_K):
        acc = acc + taps[CONV_K - 1 - s] * shifts[s]
    return acc


def _qk_scale(grp):
    return B_DH ** -0.5 if grp == 0 else 1.0


def _act_fwd(c, grp):
    y = _silu(c)
    if grp == 2:
        return y
    parts = []
    for hd in range(B_HEADS):
        yh = y[:, hd * B_DH:(hd + 1) * B_DH]
        parts.append(yh * (lax.rsqrt(jnp.sum(yh * yh, axis=-1, keepdims=True) + EPS) * _qk_scale(grp)))
    return jnp.concatenate(parts, axis=1)


def _act_bwd(c, dy, grp):
    if grp == 2:
        return dy * _dsilu(c)
    y = _silu(c)
    parts = []
    for hd in range(B_HEADS):
        yh = y[:, hd * B_DH:(hd + 1) * B_DH]
        r = lax.rsqrt(jnp.sum(yh * yh, axis=-1, keepdims=True) + EPS)
        dyh = dy[:, hd * B_DH:(hd + 1) * B_DH] * _qk_scale(grp)
        parts.append(r * dyh - yh * (r * r * r) * jnp.sum(dyh * yh, axis=-1, keepdims=True))
    return jnp.concatenate(parts, axis=1) * _dsilu(c)


def _chunk_tri(n, upper=False):
    r = lax.broadcasted_iota(jnp.int32, (n, n), 0)
    c = lax.broadcasted_iota(jnp.int32, (n, n), 1)
    same = (r // CH) == (c // CH)
    return jnp.where(same & ((r <= c) if upper else (r >= c)), 1.0, 0.0).astype(F32)


def _gate_rows(ba, par_ref):
    lane = lax.broadcasted_iota(jnp.int32, ba.shape, 1)
    z = ba + par_ref[1:2, :]
    sp = jnp.maximum(z, 0.0) + jnp.log(1.0 + jnp.exp(-jnp.abs(z)))
    g = -jnp.exp(par_ref[0:1, :]) * sp
    return jnp.where(lane < B_HEADS, _sigmoid(ba), jnp.where(lane < 2 * B_HEADS, g, 0.0)), z


def _prev8(cb):
    return pl.BlockSpec((8, B_W), lambda i: (jnp.maximum(i * (GTR // 8) - 1, 0), cb))


def _next8(cb, nb):
    return pl.BlockSpec((8, B_W), lambda i: (jnp.minimum((i + 1) * (GTR // 8), nb * (GTR // 8) - 1), cb))


def _gdn_pre_fwd(proj, wconv, par, *, name):
    tt = proj.shape[0]

    def body(q_ref, k_ref, v_ref, qh_ref, kh_ref, vh_ref, ba_ref, w_ref, par_ref, qo_ref, ko_ref, vo_ref, aux_ref):
        first = pl.program_id(0) == 0
        for grp, (x_ref, h_ref, o_ref) in enumerate(((q_ref, qh_ref, qo_ref), (k_ref, kh_ref, ko_ref),
                                                     (v_ref, vh_ref, vo_ref))):
            xe = jnp.concatenate([jnp.where(first, 0.0, h_ref[...]), x_ref[...]], axis=0)
            o_ref[...] = _act_fwd(_conv(_shifts(xe, GTR), _taps(w_ref, grp)), grp)
        bg, _ = _gate_rows(ba_ref[...], par_ref)
        lane = lax.broadcasted_iota(jnp.int32, bg.shape, 1)
        aux_ref[...] = jnp.where(lane < B_HEADS, bg, _dot(_chunk_tri(GTR), bg, HI))

    col = lambda off: _rb(GTR, B_W, off // B_W)
    outs = jax.ShapeDtypeStruct((tt, B_W), F32)
    return pl.pallas_call(
        body, grid=(tt // GTR,),
        in_specs=[col(OFF_QB), col(OFF_KB), col(OFF_VB), _prev8(OFF_QB // B_W), _prev8(OFF_KB // B_W),
                  _prev8(OFF_VB // B_W), _rb(GTR, LANE, OFF_BA // LANE), _whole((CONV_K, 3 * B_W)),
                  _whole((8, LANE))],
        out_specs=(_rb(GTR, B_W), _rb(GTR, B_W), _rb(GTR, B_W), _rb(GTR, LANE)),
        out_shape=(outs, outs, outs, jax.ShapeDtypeStruct((tt, LANE), F32)),
        compiler_params=_cp(("parallel",)), name=name)(proj, proj, proj, proj, proj, proj, proj, wconv, par)


def _gdn_pre_bwd(proj, wconv, par, dq, dk, dv, daux, *, name):
    tt = proj.shape[0]
    nb = tt // GTR

    def body(q_ref, k_ref, v_ref, qh_ref, kh_ref, vh_ref, qn_ref, kn_ref, vn_ref, ba_ref, w_ref, par_ref,
             dq_ref, dk_ref, dv_ref, dqn_ref, dkn_ref, dvn_ref, daux_ref, dx_ref, dba_ref, dw_ref, dpar_ref):
        i = pl.program_id(0)
        first, last = i == 0, i == nb - 1

        @pl.when(first)
        def _():
            dw_ref[...] = jnp.zeros_like(dw_ref)
            dpar_ref[...] = jnp.zeros_like(dpar_ref)

        groups = ((q_ref, qh_ref, qn_ref, dq_ref, dqn_ref), (k_ref, kh_ref, kn_ref, dk_ref, dkn_ref),
                  (v_ref, vh_ref, vn_ref, dv_ref, dvn_ref))
        for grp, (x_ref, h_ref, xn_ref, d_ref, dn_ref) in enumerate(groups):
            taps = _taps(w_ref, grp)
            xe = jnp.concatenate([jnp.where(first, 0.0, h_ref[...]), x_ref[...]], axis=0)
            sh = _shifts(xe, GTR)
            dc = _act_bwd(_conv(sh, taps), d_ref[...], grp)
            xe_n = jnp.concatenate([x_ref[GTR - 8:GTR, :], xn_ref[...]], axis=0)
            dcn = _act_bwd(_conv(_shifts(xe_n, 8), taps), dn_ref[...], grp)
            dce = jnp.concatenate([dc, jnp.where(last, 0.0, dcn)], axis=0)
            dx = taps[CONV_K - 1] * dc
            dw_ref[CONV_K - 1:CONV_K, grp * B_W:(grp + 1) * B_W] += _colsum(dc * sh[0])
            for s in range(1, CONV_K):
                dx = dx + taps[CONV_K - 1 - s] * pltpu.roll(dce, GTR + 8 - s, 0)[0:GTR]
                dw_ref[CONV_K - 1 - s:CONV_K - s, grp * B_W:(grp + 1) * B_W] += _colsum(dc * sh[s])
            dx_ref[:, grp * B_W:(grp + 1) * B_W] = dx.astype(BF16)
        ba = ba_ref[...]
        lane = lax.broadcasted_iota(jnp.int32, ba.shape, 1)
        bg, z = _gate_rows(ba, par_ref)
        daux_v = daux_ref[...]
        dg = _dot(_chunk_tri(GTR, upper=True), daux_v, HI)
        dgl = jnp.where((lane >= B_HEADS) & (lane < 2 * B_HEADS), dg, 0.0)
        da = dgl * (-jnp.exp(par_ref[0:1, :])) * _sigmoid(z)
        dbr = jnp.where(lane < B_HEADS, daux_v * bg * (1.0 - bg), 0.0)
        dba_ref[...] = (dbr + da).astype(BF16)
        dpar_ref[0:1, :] += _colsum(dgl * bg)
        dpar_ref[1:2, :] += _colsum(da)

    col = lambda off: _rb(GTR, B_W, off // B_W)
    row, rowl = _rb(GTR, B_W), _rb(GTR, LANE)
    return pl.pallas_call(
        body, grid=(nb,),
        in_specs=[col(OFF_QB), col(OFF_KB), col(OFF_VB),
                  _prev8(OFF_QB // B_W), _prev8(OFF_KB // B_W), _prev8(OFF_VB // B_W),
                  _next8(OFF_QB // B_W, nb), _next8(OFF_KB // B_W, nb), _next8(OFF_VB // B_W, nb),
                  _rb(GTR, LANE, OFF_BA // LANE), _whole((CONV_K, 3 * B_W)), _whole((8, LANE)),
                  row, row, row, _next8(0, nb), _next8(0, nb), _next8(0, nb), rowl],
        out_specs=(_rb(GTR, 3 * B_W), rowl, _whole((8, 3 * B_W)), _whole((8, LANE))),
        out_shape=(jax.ShapeDtypeStruct((tt, 3 * B_W), BF16), jax.ShapeDtypeStruct((tt, LANE), BF16),
                   jax.ShapeDtypeStruct((8, 3 * B_W), F32), jax.ShapeDtypeStruct((8, LANE), F32)),
        compiler_params=_cp(("arbitrary",)), name=name)(
            proj, proj, proj, proj, proj, proj, proj, proj, proj, proj, wconv, par, dq, dk, dv, dq, dk, dv, daux)


def _col(x, j):
    lane = lax.broadcasted_iota(jnp.int32, x.shape, 1)
    return jnp.sum(jnp.where(lane == j, x, 0.0), axis=-1, keepdims=True)


def _split(x):
    hi = x.astype(BF16)
    return hi, (x - hi.astype(F32)).astype(BF16)


def _dot3(a, b, tn=False):
    dot = _dot_tn if tn else _dot
    (ah, al), (bh, bl) = _split(a), _split(b)
    return dot(ah, bh) + (dot(ah, bl) + dot(al, bh))


def _chunk_masks():
    r = lax.broadcasted_iota(jnp.int32, (CH, CH), 0)
    c = lax.broadcasted_iota(jnp.int32, (CH, CH), 1)
    return r > c, r >= c


def _gc_rows(aux, nc):
    t = jnp.transpose(aux[:, B_HEADS:2 * B_HEADS].reshape(nc, CH, B_HEADS), (0, 2, 1))
    return jnp.concatenate([t, jnp.zeros_like(t)], axis=1).reshape(nc * 8, CH)


_CHUNK8 = lambda width, n=1: pl.BlockSpec((8 * n, width), lambda i: (i, 0))
_CHUNK4 = lambda a, b, n=1: pl.BlockSpec((B_HEADS * n, a, b), lambda i: (i, 0, 0))
NCH = 2


def _per_chunk(body, rows):
    def wrapped(*refs):
        for ci in range(NCH):
            body(*[r.at[pl.ds(ci * n, n)] for r, n in zip(refs, rows)])
    return wrapped


def _gdn_lower(k, aux, auxt, *, name):
    tt = k.shape[0]

    def body(k_ref, aux_ref, auxt_ref, l_ref):
        aux_v = aux_ref[...]
        strict, _ = _chunk_masks()
        for hd in range(B_HEADS):
            kh = k_ref[:, hd * B_DH:(hd + 1) * B_DH].astype(BF16)
            diff = _col(aux_v, B_HEADS + hd) - auxt_ref[hd:hd + 1, :]
            dec = jnp.exp(jnp.where(strict, diff, NEG))
            l_ref[hd] = _col(aux_v, hd) * _dot_nt(kh, kh) * dec

    return pl.pallas_call(
        _per_chunk(body, (CH, CH, 8, B_HEADS)), grid=(tt // CH // NCH,),
        in_specs=[_rb(NCH * CH, B_W), _rb(NCH * CH, LANE), _CHUNK8(CH, NCH)],
        out_specs=_CHUNK4(CH, CH, NCH),
        out_shape=jax.ShapeDtypeStruct((tt // CH * B_HEADS, CH, CH), F32),
        compiler_params=_cp(("parallel",)), name=name)(k, aux, auxt)


def _tri_inverse(lt, *, name):
    nb = lt.shape[2]

    def body(l_ref, t_ref):
        rowid = lax.broadcasted_iota(jnp.int32, (CH, nb), 0)

        def outer(i, carry):
            def inner(j, acc):
                return acc + l_ref[i, pl.ds(j, 1), :] * t_ref[j]

            acc = lax.fori_loop(0, i, inner, jnp.zeros((CH, nb), F32))
            t_ref[i] = jnp.where(rowid == i, 1.0, 0.0) - acc
            return carry

        lax.fori_loop(0, CH, outer, 0)

    return pl.pallas_call(body, out_shape=jax.ShapeDtypeStruct(lt.shape, F32),
                          in_specs=[pl.BlockSpec(memory_space=pltpu.VMEM)],
                          out_specs=pl.BlockSpec(memory_space=pltpu.VMEM),
                          compiler_params=_cp(), name=name)(lt)


def _gdn_gates(aux_v, aux_last, auxt_ref, hd):
    _, incl = _chunk_masks()
    beta = _col(aux_v, hd)
    gc = _col(aux_v, B_HEADS + hd)
    gl = _col(aux_last, B_HEADS + hd)
    dec = jnp.exp(jnp.where(incl, gc - auxt_ref[hd:hd + 1, :], NEG))
    return beta, gc, gl, jnp.exp(gc), dec


def _gdn_intra(q, k, v, aux, auxt, tinv, *, name):
    tt = q.shape[0]
    nc = tt // CH

    def body(q_ref, k_ref, v_ref, aux_ref, auxt_ref, t_ref, u0_ref, w_ref, qd_ref, kd_ref, qk_ref, gle_ref):
        aux_v = aux_ref[...]
        aux_last = aux_ref[CH - 1:CH, :]
        lane8 = lax.broadcasted_iota(jnp.int32, (8, LANE), 1)
        gle = jnp.zeros((8, LANE), F32)
        heads = range(B_HEADS)
        sls = [slice(hd * B_DH, (hd + 1) * B_DH) for hd in heads]
        gates = [_gdn_gates(aux_v, aux_last, auxt_ref, hd) for hd in heads]
        qk0 = [_dot_nt(q_ref[:, sls[hd]].astype(BF16), k_ref[:, sls[hd]].astype(BF16)) for hd in heads]
        u0 = [_dot3(t_ref[hd], v_ref[:, sls[hd]] * gates[hd][0]) for hd in heads]
        wk = [_dot3(t_ref[hd], k_ref[:, sls[hd]] * (gates[hd][0] * gates[hd][3])) for hd in heads]
        for hd in heads:
            sl = sls[hd]
            beta, gc, gl, egc, dec = gates[hd]
            qk_ref[hd] = (qk0[hd] * dec).astype(BF16)
            u0_ref[:, sl] = u0[hd]
            w_ref[:, sl] = wk[hd].astype(BF16)
            qd_ref[:, sl] = (q_ref[:, sl] * egc).astype(BF16)
            kd_ref[:, sl] = (k_ref[:, sl] * jnp.exp(gl - gc)).astype(BF16)
            gle = gle + jnp.where(lane8 == hd, jnp.exp(gl), 0.0)
        gle_ref[...] = gle

    row = _rb(NCH * CH, B_W)
    half = jax.ShapeDtypeStruct((tt, B_W), BF16)
    return pl.pallas_call(
        _per_chunk(body, (CH, CH, CH, CH, 8, B_HEADS, CH, CH, CH, CH, B_HEADS, 8)), grid=(nc // NCH,),
        in_specs=[row, row, row, _rb(NCH * CH, LANE), _CHUNK8(CH, NCH), _CHUNK4(CH, CH, NCH)],
        out_specs=(row, row, row, row, _CHUNK4(CH, CH, NCH), _CHUNK8(LANE, NCH)),
        out_shape=(jax.ShapeDtypeStruct((tt, B_W), F32), half, half, half,
                   jax.ShapeDtypeStruct((nc * B_HEADS, CH, CH), BF16), jax.ShapeDtypeStruct((nc * 8, LANE), F32)),
        compiler_params=_cp(("parallel",)), name=name)(q, k, v, aux, auxt, tinv)


def _gdn_scan_fwd(u0, w, qd, kd, qk, gle, *, name):
    tt = u0.shape[0]
    nc = tt // CH

    def body(u0_ref, w_ref, qd_ref, kd_ref, qk_ref, gle_ref, o_ref, ss_ref, u_ref, s_ref):
        @pl.when(pl.program_id(0) == 0)
        def _():
            s_ref[...] = jnp.zeros_like(s_ref)

        gle = gle_ref[0:1, :]
        heads = range(B_HEADS)
        sls = [slice(hd * B_DH, (hd + 1) * B_DH) for hd in heads]
        st = [s_ref[hd] for hd in heads]
        sb = [t.astype(BF16) for t in st]
        ws = [_dot(w_ref[:, sls[hd]], sb[hd]) for hd in heads]
        qs = [_dot(qd_ref[:, sls[hd]], sb[hd]) for hd in heads]
        ub = [(u0_ref[:, sls[hd]] - ws[hd]).astype(BF16) for hd in heads]
        ku = [_dot_tn(kd_ref[:, sls[hd]], ub[hd]) for hd in heads]
        qu = [_dot(qk_ref[hd], ub[hd]) for hd in heads]
        for hd in heads:
            ss_ref[hd] = st[hd]
            u_ref[:, sls[hd]] = ub[hd]
            o_ref[:, sls[hd]] = qs[hd] + qu[hd]
            s_ref[hd] = st[hd] * _col(gle, hd) + ku[hd]

    row = _rb(CH, B_W)
    return pl.pallas_call(
        body, grid=(nc,), in_specs=[row, row, row, row, _CHUNK4(CH, CH), _CHUNK8(LANE)],
        out_specs=(row, _CHUNK4(B_DH, B_DH), row),
        out_shape=(jax.ShapeDtypeStruct((tt, B_W), F32), jax.ShapeDtypeStruct((nc * B_HEADS, B_DH, B_DH), F32),
                   jax.ShapeDtypeStruct((tt, B_W), BF16)),
        scratch_shapes=[pltpu.VMEM((B_HEADS, B_DH, B_DH), F32)],
        compiler_params=_cp(("arbitrary",)), name=name)(u0, w, qd, kd, qk, gle)


def _gdn_scan_bwd(w, qd, kd, qk, gle, do, *, name):
    tt = w.shape[0]
    nc = tt // CH

    def body(w_ref, qd_ref, kd_ref, qk_ref, gle_ref, do_ref, du_ref, dss_ref, ds_ref):
        @pl.when(pl.program_id(0) == 0)
        def _():
            ds_ref[...] = jnp.zeros_like(ds_ref)

        gle = gle_ref[0:1, :]
        heads = range(B_HEADS)
        sls = [slice(hd * B_DH, (hd + 1) * B_DH) for hd in heads]
        dst = [ds_ref[hd] for hd in heads]
        dob = [do_ref[:, sls[hd]].astype(BF16) for hd in heads]
        kds = [_dot(kd_ref[:, sls[hd]], dst[hd].astype(BF16)) for hd in heads]
        qkd = [_dot_tn(qk_ref[hd], dob[hd]) for hd in heads]
        qdd = [_dot_tn(qd_ref[:, sls[hd]], dob[hd]) for hd in heads]
        du = [qkd[hd] + kds[hd] for hd in heads]
        wdu = [_dot_tn(w_ref[:, sls[hd]], du[hd].astype(BF16)) for hd in heads]
        for hd in heads:
            dss_ref[hd] = dst[hd]
            du_ref[:, sls[hd]] = du[hd]
            ds_ref[hd] = qdd[hd] + _col(gle, hd) * dst[hd] - wdu[hd]

    rev = lambda width: pl.BlockSpec((CH, width), lambda i: (nc - 1 - i, 0))
    rev4 = lambda a, b: pl.BlockSpec((B_HEADS, a, b), lambda i: (nc - 1 - i, 0, 0))
    return pl.pallas_call(
        body, grid=(nc,),
        in_specs=[rev(B_W), rev(B_W), rev(B_W), rev4(CH, CH), pl.BlockSpec((8, LANE), lambda i: (nc - 1 - i, 0)), rev(B_W)],
        out_specs=(rev(B_W), rev4(B_DH, B_DH)),
        out_shape=(jax.ShapeDtypeStruct((tt, B_W), F32), jax.ShapeDtypeStruct((nc * B_HEADS, B_DH, B_DH), F32)),
        scratch_shapes=[pltpu.VMEM((B_HEADS, B_DH, B_DH), F32)],
        compiler_params=_cp(("arbitrary",)), name=name)(w, qd, kd, qk, gle, do)


def _gdn_bwd(q, k, v, aux, auxt, tinv, u0, w, u, ss, dss, du, do, *, name):
    tt = q.shape[0]
    nc = tt // CH

    def body(q_ref, k_ref, v_ref, aux_ref, auxt_ref, t_ref, u0_ref, w_ref, u_ref, ss_ref, dss_ref, du_ref, do_ref,
             dq_ref, dk_ref, dv_ref, daux_ref):
        aux_v = aux_ref[...]
        aux_last = aux_ref[CH - 1:CH, :]
        lane = lax.broadcasted_iota(jnp.int32, (CH, LANE), 1)
        rowi = lax.broadcasted_iota(jnp.int32, (CH, 1), 0)
        strict, incl = _chunk_masks()
        daux = jnp.zeros((CH, LANE), F32)
        heads = range(B_HEADS)
        sls = [slice(hd * B_DH, (hd + 1) * B_DH) for hd in heads]
        gates = [_gdn_gates(aux_v, aux_last, auxt_ref, hd) for hd in heads]
        kbs = [k_ref[:, sl].astype(BF16) for sl in sls]
        qbs = [q_ref[:, sl].astype(BF16) for sl in sls]
        sbs = [ss_ref[hd].astype(BF16) for hd in heads]
        dsbs = [dss_ref[hd].astype(BF16) for hd in heads]
        dobs = [do_ref[:, sl].astype(BF16) for sl in sls]
        kks = [_dot_nt(kbs[hd], kbs[hd]) for hd in heads]
        qk0s = [_dot_nt(qbs[hd], kbs[hd]) for hd in heads]
        dq_decs = [_dot_nt(dobs[hd], sbs[hd]) for hd in heads]
        dqks = [_dot_nt(dobs[hd], u_ref[:, sls[hd]]) for hd in heads]
        dk_decs = [_dot_nt(u_ref[:, sls[hd]], dsbs[hd]) for hd in heads]
        dws = [-_dot_nt(du_ref[:, sls[hd]].astype(BF16), sbs[hd]) for hd in heads]
        drvs = [_dot3(t_ref[hd], du_ref[:, sls[hd]], tn=True) for hd in heads]
        drks = [_dot3(t_ref[hd], dws[hd], tn=True) for hd in heads]
        dls = [-(_dot_nt(drvs[hd].astype(BF16), u0_ref[:, sls[hd]].astype(BF16))
                 + _dot_nt(drks[hd].astype(BF16), w_ref[:, sls[hd]])) for hd in heads]
        for hd in heads:
            sl = sls[hd]
            qh, kh, vh = q_ref[:, sl], k_ref[:, sl], v_ref[:, sl]
            beta, gc, gl, egc, dec = gates[hd]
            ekd, eg_last = jnp.exp(gl - gc), jnp.exp(gl)
            kb, qb, kk, qk0 = kbs[hd], qbs[hd], kks[hd], qk0s[hd]
            st, dst = ss_ref[hd], dss_ref[hd]
            dq_dec, dk_dec = dq_decs[hd], dk_decs[hd]
            dqk = jnp.where(incl, dqks[hd], 0.0)
            dgl = jnp.sum(jnp.sum(st * dst, axis=-1, keepdims=True), axis=0, keepdims=True) * eg_last
            drv, drk = drvs[hd], drks[hd]
            dl = jnp.where(strict, dls[hd], 0.0)
            dv_ref[:, sl] = drv * beta
            rk = jnp.sum(drk * kh, axis=-1, keepdims=True)
            dbeta = jnp.sum(drv * vh, axis=-1, keepdims=True) + rk * egc
            dgc = rk * beta * egc
            dk = drk * (beta * egc)
            ldec = dl * dec
            dbeta = dbeta + jnp.sum(ldec * kk, axis=-1, keepdims=True)
            dkk = (ldec * beta).astype(BF16)
            dqk0 = (dqk * dec).astype(BF16)
            ddec = ldec * beta * kk + dqk * (qk0 * dec)
            dq = _dot(dqk0, kb) + dq_dec * egc
            dk = dk + _dot_tn(dqk0, qb) + _dot(dkk, kb) + _dot_tn(dkk, kb) + dk_dec * ekd
            dgc = dgc + jnp.sum(ddec, axis=-1, keepdims=True) - _col_from_rowsum(ddec)
            dgc = dgc + jnp.sum(dq_dec * qh, axis=-1, keepdims=True) * egc
            kd = jnp.sum(dk_dec * kh, axis=-1, keepdims=True) * ekd
            dgc = dgc - kd
            dgc = dgc + jnp.where(rowi == CH - 1, jnp.sum(kd, axis=0, keepdims=True) + dgl, 0.0)
            dq_ref[:, sl] = dq
            dk_ref[:, sl] = dk
            daux = daux + jnp.where(lane == hd, dbeta, 0.0) + jnp.where(lane == B_HEADS + hd, dgc, 0.0)
        daux_ref[...] = daux

    row = _rb(NCH * CH, B_W)
    outs = jax.ShapeDtypeStruct((tt, B_W), F32)
    return pl.pallas_call(
        _per_chunk(body, (CH, CH, CH, CH, 8, B_HEADS, CH, CH, CH, B_HEADS, B_HEADS, CH, CH, CH, CH, CH, CH)),
        grid=(nc // NCH,),
        in_specs=[row, row, row, _rb(NCH * CH, LANE), _CHUNK8(CH, NCH), _CHUNK4(CH, CH, NCH), row, row, row,
                  _CHUNK4(B_DH, B_DH, NCH), _CHUNK4(B_DH, B_DH, NCH), row, row],
        out_specs=(row, row, row, _rb(NCH * CH, LANE)),
        out_shape=(outs, outs, outs, jax.ShapeDtypeStruct((tt, LANE), F32)),
        compiler_params=_cp(("parallel",)), name=name)(q, k, v, aux, auxt, tinv, u0, w, u, ss, dss, du, do)


def _col_from_rowsum(m):
    hi, lo = _split(m)
    ones = jnp.ones((CH, LANE), BF16)
    return (_dot_tn(hi, ones) + _dot_tn(lo, ones))[:, 0:1]


def _gdn_post_fwd(o, proj, gn, *, name, tr=256):
    tt = o.shape[0]

    def body(o_ref, z_ref, g_ref, y_ref):
        for hd in range(B_HEADS):
            sl = slice(hd * B_DH, (hd + 1) * B_DH)
            oh = o_ref[:, sl]
            r = lax.rsqrt(jnp.mean(oh * oh, axis=-1, keepdims=True) + EPS)
            y_ref[:, sl] = (oh * r * g_ref[...] * _silu(z_ref[:, sl])).astype(BF16)

    return pl.pallas_call(body, grid=(tt // tr,), in_specs=[_rb(tr, B_W), _rb(tr, B_W, OFF_ZB // B_W), _whole((1, B_DH))],
                          out_specs=_rb(tr, B_W), out_shape=jax.ShapeDtypeStruct((tt, B_W), BF16),
                          compiler_params=_cp(("parallel",)), name=name)(o, proj, gn)


def _gdn_post_bwd(o, proj, gn, dy, *, name, tr=256):
    tt = o.shape[0]

    def body(o_ref, z_ref, g_ref, dy_ref, do_ref, dz_ref, dg_ref):
        @pl.when(pl.program_id(0) == 0)
        def _():
            dg_ref[...] = jnp.zeros_like(dg_ref)

        g = g_ref[...]
        for hd in range(B_HEADS):
            sl = slice(hd * B_DH, (hd + 1) * B_DH)
            oh, zh, dyh = o_ref[:, sl], z_ref[:, sl], dy_ref[:, sl]
            r = lax.rsqrt(jnp.mean(oh * oh, axis=-1, keepdims=True) + EPS)
            a = oh * r
            s = _silu(zh)
            da = dyh * g * s
            dg_ref[0:1, :] += _colsum(dyh * a * s)
            dz_ref[:, sl] = (dyh * a * g * _dsilu(zh)).astype(BF16)
            do_ref[:, sl] = r * (da - a * jnp.mean(da * a, axis=-1, keepdims=True))

    return pl.pallas_call(
        body, grid=(tt // tr,), in_specs=[_rb(tr, B_W), _rb(tr, B_W, OFF_ZB // B_W), _whole((1, B_DH)), _rb(tr, B_W)],
        out_specs=(_rb(tr, B_W), _rb(tr, B_W), _whole((8, B_DH))),
        out_shape=(jax.ShapeDtypeStruct((tt, B_W), F32), jax.ShapeDtypeStruct((tt, B_W), BF16),
                   jax.ShapeDtypeStruct((8, B_DH), F32)),
        compiler_params=_cp(("arbitrary",)), name=name)(o, proj, gn, dy)


def _adamw(parts, w, m, v, own=None, sel=None, *, name, tr=256):
    npart, nl, r, c = parts.shape
    tr = max([t for t in range(8, min(r, tr) + 1, 8) if r % t == 0], default=r)
    tc = c if tr < r or r <= 256 or c % 256 else 256
    c1, c2 = 1.0 - ADAM_B1 ** ADAM_STEP, 1.0 - ADAM_B2 ** ADAM_STEP

    def body(*refs):
        if own is None:
            p_ref, w_ref, m_ref, v_ref, g_ref, d_ref, mo_ref, vo_ref = refs
            part = lambda i: p_ref[i].astype(F32)
        else:
            p_ref, w_ref, m_ref, v_ref, own_ref, sel_ref, g_ref, d_ref, mo_ref, vo_ref = refs
            part = lambda i: jnp.where(sel_ref[i:i + 1, 0:1] > 0.5, own_ref[...].astype(F32), p_ref[i].astype(F32))
        g = part(0)
        for i in range(1, npart):
            g = g + part(i)
        mn = ADAM_B1 * m_ref[...] + (1.0 - ADAM_B1) * g
        vn = ADAM_B2 * v_ref[...] + (1.0 - ADAM_B2) * (g * g)
        g_ref[...] = g
        mo_ref[...] = mn
        vo_ref[...] = vn
        d_ref[...] = -ADAM_LR * ((mn / c1) / (jnp.sqrt(vn / c2) + ADAM_EPS) + ADAM_WD * w_ref[...])

    row = pl.BlockSpec((None, tr, tc), lambda l, i, j: (l, i, j))
    out = jax.ShapeDtypeStruct((nl, r, c), F32)
    ins, in_specs = [parts, w, m, v], [pl.BlockSpec((npart, None, tr, tc), lambda l, i, j: (0, l, i, j)), row, row, row]
    if own is not None:
        ins += [own, sel]
        in_specs += [row, pl.BlockSpec((N_DEV, LANE), lambda l, i, j: (0, 0))]
    return pl.pallas_call(body, grid=(nl, r // tr, c // tc), in_specs=in_specs, out_specs=(row, row, row, row),
                          out_shape=(out, out, out, out), compiler_params=_cp(("parallel", "parallel", "parallel")),
                          name=name)(*ins)


def _peer(k):
    x, y, c = lax.axis_index("x"), lax.axis_index("y"), lax.axis_index("c")
    return ((1 - x) if k & 4 else x, (1 - y) if k & 2 else y, (1 - c) if k & 1 else c)


def _my_index():
    return 4 * lax.axis_index("x") + 2 * lax.axis_index("y") + lax.axis_index("c")


def _index_of(p):
    return 4 * p[0] + 2 * p[1] + p[2]


def _all_gather(xs, *, name):
    n = len(xs)

    def body(*refs):
        x_refs, o_refs = refs[:n], refs[n:2 * n]
        send, recv, loc = refs[2 * n:]
        me = _my_index()
        copies = []
        for a in range(n):
            cp = pltpu.make_async_copy(x_refs[a], o_refs[a].at[me], loc.at[a])
            cp.start()
            copies.append(cp)
        rdmas = []
        for a in range(n):
            for k in range(1, N_DEV):
                r = pltpu.make_async_remote_copy(
                    src_ref=x_refs[a], dst_ref=o_refs[a].at[me], send_sem=send.at[a, k - 1], recv_sem=recv.at[a, k - 1],
                    device_id=_peer(k), device_id_type=pl.DeviceIdType.MESH)
                r.start()
                rdmas.append(r)
        for a in range(n):
            for k in range(1, N_DEV):
                pltpu.make_async_remote_copy(
                    src_ref=x_refs[a], dst_ref=o_refs[a].at[_index_of(_peer(k))], send_sem=send.at[a, k - 1],
                    recv_sem=recv.at[a, k - 1], device_id=_peer(k), device_id_type=pl.DeviceIdType.MESH).wait_recv()
        for r in rdmas:
            r.wait_send()
        for cp in copies:
            cp.wait()

    any_spec = pl.BlockSpec(memory_space=pl.ANY)
    return pl.pallas_call(
        body, in_specs=[any_spec] * n, out_specs=tuple([any_spec] * n),
        out_shape=tuple(jax.ShapeDtypeStruct((N_DEV,) + x.shape, x.dtype) for x in xs),
        scratch_shapes=[pltpu.SemaphoreType.DMA((n, N_DEV - 1)), pltpu.SemaphoreType.DMA((n, N_DEV - 1)),
                        pltpu.SemaphoreType.DMA((n,))],
        name=name)(*xs)


_HBM = pl.BlockSpec(memory_space=pltpu.HBM)
_SEM = pl.BlockSpec(memory_space=pltpu.SEMAPHORE)
_EFFECT = pltpu.SideEffectType.DATAFLOW_SIDE_EFFECTING


def _split_copy(src_ref, land_ref, send, recv, a, k, scatter, slot, sending):
    me, peer = _my_index(), _index_of(_peer(k))
    src = src_ref.at[peer if sending else me] if scatter else src_ref
    land = land_ref.at[me if sending else peer]
    if slot is not None:
        land = land.at[slot]
    sem = a * (N_DEV - 1) + k - 1
    return pltpu.make_async_remote_copy(src_ref=src, dst_ref=land, send_sem=send.at[sem], recv_sem=recv.at[sem],
                                        device_id=_peer(k), device_id_type=pl.DeviceIdType.MESH)


def _exchange_start(srcs, lands, after, *, scatter, slot=None, name):
    n = len(srcs)

    def body(*refs):
        src_refs, land_refs = refs[:n], refs[n:2 * n]
        send, recv, token = refs[2 * n + 1], refs[2 * n + 2], refs[-1]
        for a in range(n):
            for k in range(1, N_DEV):
                _split_copy(src_refs[a], land_refs[a], send, recv, a, k, scatter, slot, True).start()
        token[...] = jnp.zeros_like(token)

    hbm = lambda t: pltpu.HBM(t.shape, t.dtype)
    sems = pltpu.SemaphoreType.DMA((n * (N_DEV - 1),))
    out = pl.pallas_call(
        body, name=name,
        out_shape=(sems, sems, *[hbm(t) for t in srcs], *[hbm(t) for t in lands], jax.ShapeDtypeStruct((8, LANE), F32)),
        in_specs=[_HBM] * (2 * n) + [pl.BlockSpec(memory_space=pl.ANY)],
        out_specs=(_SEM, _SEM, *[_HBM] * (2 * n), pl.BlockSpec(memory_space=pltpu.VMEM)),
        input_output_aliases={i: 2 + i for i in range(2 * n)},
        compiler_params=pltpu.CompilerParams(has_side_effects=_EFFECT),
    )(*[pltpu.with_memory_space_constraint(t, pltpu.HBM) for t in (*srcs, *lands)], after)
    return out[0], out[1], out[2:2 + n], out[2 + n:2 + 2 * n], out[-1]


def _exchange_wait(send, recv, srcs, lands, after, *, scatter, slot=None, name):
    n = len(srcs)

    def body(*refs):
        src_refs, land_refs = refs[:n], refs[n:2 * n]
        send_ref, recv_ref = refs[2 * n], refs[2 * n + 1]
        for a in range(n):
            for k in range(1, N_DEV):
                _split_copy(src_refs[a], land_refs[a], send_ref, recv_ref, a, k, scatter, slot, True).wait_send()
                _split_copy(src_refs[a], land_refs[a], send_ref, recv_ref, a, k, scatter, slot, False).wait_recv()

    hbm = lambda t: pltpu.HBM(t.shape, t.dtype)
    out = pl.pallas_call(
        body, name=name, out_shape=(*[hbm(t) for t in srcs], *[hbm(t) for t in lands]),
        in_specs=[_HBM] * (2 * n) + [_SEM, _SEM, pl.BlockSpec(memory_space=pl.ANY)],
        out_specs=tuple([_HBM] * (2 * n)), input_output_aliases={i: i for i in range(2 * n)},
        compiler_params=pltpu.CompilerParams(has_side_effects=_EFFECT),
    )(*srcs, *lands, send, recv, after)
    return out[:n], out[n:]


def _win_to_mine(wt):
    pad = jnp.zeros((IN_PAD - IN_DIM,) + wt.shape[1:], wt.dtype)
    return jnp.concatenate([wt[3592:5640], wt[0:3584], wt[3584:3592], pad], axis=0)


def _win_from_mine(gt):
    return jnp.concatenate([gt[2048:5632], gt[5632:5640], gt[0:2048]], axis=0)


def _pad_rows(a, mult=8):
    r = (-a.shape[0]) % mult
    return a if r == 0 else jnp.concatenate([a, jnp.zeros((r,) + a.shape[1:], a.dtype)], axis=0)


def _lanes(vec, start):
    return jnp.zeros((1, LANE), F32).at[0, start:start + vec.shape[0]].set(vec)


def _small_spec(depth):
    return (("b_ada", (depth, 6 * D)), ("norm1_g", (depth, D)), ("norm2_g", (depth, D)),
            ("rel_table", (depth, A_HEADS, 2 * A_MAX_REL + 1)), ("a_log", (depth, B_HEADS)),
            ("dt_bias", (depth, B_HEADS)), ("gdn_norm_g", (depth, B_DH)), ("final_g", (D,)))


def _pack_small(d, extra, depth):
    spec = _small_spec(depth)
    rows = -(-(sum(math.prod(s) for _, s in spec) + 1) // (8 * LANE)) * 8
    flat = jnp.concatenate([d[n].reshape(-1).astype(F32) for n, _ in spec] + [extra.reshape(-1)])
    flat = jnp.concatenate([flat, jnp.zeros((rows * LANE - flat.shape[0],), F32)])
    return flat.reshape(rows, LANE)


def _unpack_small(p, depth):
    flat = p.reshape(-1)
    out, off = {}, 0
    for n, s in _small_spec(depth):
        sz = math.prod(s)
        out[n] = flat[off:off + sz].reshape(s)
        off += sz
    return out, flat[off]


def kernel(x, c, w_ada, b_ada, norm1_g, norm2_g, w_in, rel_table, w_conv, a_log, dt_bias, gdn_norm_g, w_branch_a, w_branch_b, w_out, w_ffn_in, w_ffn_out, final_g, loss_target, m_w_ada, m_b_ada, m_norm1_g, m_norm2_g, m_w_in, m_rel_table, m_w_conv, m_a_log, m_dt_bias, m_gdn_norm_g, m_w_branch_a, m_w_branch_b, m_w_out, m_w_ffn_in, m_w_ffn_out, m_final_g, v_w_ada, v_b_ada, v_norm1_g, v_norm2_g, v_w_in, v_rel_table, v_w_conv, v_a_log, v_dt_bias, v_gdn_norm_g, v_w_branch_a, v_w_branch_b, v_w_out, v_w_ffn_in, v_w_ffn_out, v_final_g):
    tt = x.shape[1]
    x0 = x[0]
    tgt = loss_target[0]
    me = _my_index()
    depth = w_in.shape[0]

    tr_ = lambda t: jnp.transpose(t, (0, 2, 1))
    shards = [tr_(w_in).astype(BF16), w_branch_a.astype(BF16), w_branch_b.astype(BF16), w_out.astype(BF16),
              tr_(w_ffn_in).astype(BF16), w_ffn_out.astype(BF16), w_conv]
    names = ("win", "wa", "wb", "wout", "wfi", "wfo", "wconv")
    early, late, every = (0, 6), (1, 2, 3, 4, 5), tuple(range(7))
    first = _all_gather([shards[i][0] for i in early] + [_pad_rows(c)], name="gather_first")
    c_all = first[-1][:, 0, :]
    is_me = lax.broadcasted_iota(jnp.int32, (N_DEV, 1, 1), 0) == me

    def unpack(idx, g):
        cols = lambda t: jnp.transpose(t, (1, 0, 2)).reshape(t.shape[1], N_DEV * t.shape[2])
        rows = lambda t: t.reshape(N_DEV * t.shape[1], t.shape[2])
        how = (lambda t: _win_to_mine(rows(t)), cols, cols, rows, rows, rows, cols)
        return {names[i]: how[i](t) for i, t in zip(idx, g)}

    def gather_start(l, idx, after, tag=""):
        srcs = [shards[i][l] for i in idx]
        lands = [lax.empty((N_DEV,) + t.shape, t.dtype) for t in srcs]
        return _exchange_start(srcs, lands, after, scatter=False, name=f"gather_start_{l}{tag}")

    def gather_wait(l, idx, pending, after, tag=""):
        send, recv, srcs, lands, _ = pending
        srcs, lands = _exchange_wait(send, recv, srcs, lands, after, scatter=False, name=f"gather_wait_{l}{tag}")
        return unpack(idx, [jnp.where(is_me, t[None], g) for g, t in zip(lands, srcs)])

    weights = [unpack(early, first[:-1])] + [None] * (depth - 1)
    pending0 = gather_start(0, late, first[-1], "_rest")
    pending = gather_start(1, every, pending0[-1]) if depth > 1 else None
    cond = c_all * (1.0 / (1.0 + jnp.exp(-c_all)))
    cond = _pad_rows(cond, 16)

    mod_cols = jnp.stack([_mm(cond, w_ada[l], name="mod_mm")[:N_DEV] for l in range(depth)])
    (g_mod,) = _all_gather([mod_cols], name="gather_mod")
    mod_all = jnp.transpose(g_mod, (1, 2, 0, 3)).reshape(depth, N_DEV, 6 * D)
    mod = lax.dynamic_index_in_dim(mod_all, me, axis=1, keepdims=False) + b_ada
    mods = mod.reshape(depth, 6, 1, D)

    n1g, n2g = norm1_g.reshape(depth, 1, D), norm2_g.reshape(depth, 1, D)
    gng = gdn_norm_g.reshape(depth, 1, B_DH)
    fg = final_g.reshape(1, D)

    saved = []
    tok = (pending if pending is not None else pending0)[-1][0, 0]
    xin, h1 = _adaln_fwd(x0, n1g[0], mods[0, 1] + tok, mods[0, 0], name="adaln1_first")
    for l in range(depth):
        sh1, sc1, gt1, sh2, sc2, gt2 = (mods[l, i] for i in range(6))
        wl = weights[l]
        proj = _mm(h1, wl["win"], tb=True, name="proj_mm", tn=1152)
        kpad, vpad = _kv_pad(proj, name="kv_pad")
        diag, bias_vjp = jax.vjp(_bias_diagonals, rel_table[l])
        bias = _bias_fwd(diag, name="bias_fwd")
        ya, lse = _attn_fwd(proj, kpad, vpad, bias, name="attn_fwd")
        par = jnp.concatenate([_lanes(a_log[l], B_HEADS), _lanes(dt_bias[l], B_HEADS), jnp.zeros((6, LANE), F32)], axis=0)
        qn, kn, vn, aux = _gdn_pre_fwd(proj, wl["wconv"], par, name="gdn_pre_fwd")
        auxt = _gc_rows(aux, tt // CH)
        lower = _gdn_lower(kn, aux, auxt, name="gdn_lower")
        tinv = jnp.transpose(_tri_inverse(jnp.transpose(lower, (1, 2, 0)), name="gdn_tri_inverse"), (2, 0, 1))
        u0, wg, qd, kd, qk, gle = _gdn_intra(qn, kn, vn, aux, auxt, tinv, name="gdn_intra")
        og, ss, ug = _gdn_scan_fwd(u0, wg, qd, kd, qk, gle, name="gdn_scan_fwd")
        yb = _gdn_post_fwd(og, proj, gng[l], name="gdn_post_fwd")
        if l == 0:
            wl.update(gather_wait(0, late, pending0, yb, "_rest"))
        pa, pb, merged = _branch_merge(ya, yb, wl["wa"], wl["wb"], proj, name="branch_merge")
        t1, x2, h2 = _out_adaln(merged, wl["wout"], xin, gt1, n2g[l], sc2, sh2, name="out_adaln2")
        gu, act = _ffn_in_swiglu(h2, wl["wfi"], name="ffn_in_swiglu")
        t2 = _mm(act, wl["wfo"], name="ffn_out_mm", tk=1408)
        saved.append(dict(xin=xin, h1=h1, proj=proj, kpad=kpad, vpad=vpad, bias=bias, bias_vjp=bias_vjp, ya=ya, lse=lse,
                          par=par, qn=qn, kn=kn, vn=vn, aux=aux, auxt=auxt, tinv=tinv, ss=ss, og=og, yb=yb, pa=pa, pb=pb,
                          u0=u0, wg=wg, qd=qd, kd=kd, qk=qk, gle=gle, ug=ug,
                          merged=merged, t1=t1, x2=x2, h2=h2, gu=gu, act=act, t2=t2))
        if l + 1 < depth:
            weights[l + 1] = gather_wait(l + 1, every, pending, t2)
            pending = gather_start(l + 2, every, weights[l + 1]["wconv"]) if l + 2 < depth else None
            tok = pending[-1][0, 0] if pending is not None else 0.0
            xin, h1 = _adaln_fwd(x2, n1g[l + 1], mods[l + 1, 1] + tok, mods[l + 1, 0], t2, gt2, name="adaln1_fwd")

    s = saved[-1]
    dx, dt2, st = _loss_head(s["x2"], s["t2"], mods[depth - 1, 5], fg, tgt, name="loss_head")
    loss_part = st[4, 0]
    small_g = {"final_g": st[0]}
    dmod_rows = [None] * depth
    for n in ("norm1_g", "norm2_g", "rel_table", "a_log", "dt_bias", "gdn_norm_g"):
        small_g[n] = [None] * depth
    dgt2 = st[3]
    cols_slabs = lambda g: jnp.transpose(g.reshape(g.shape[0], N_DEV, g.shape[1] // N_DEV), (1, 0, 2))
    rows_slabs = lambda g: g.reshape(N_DEV, g.shape[0] // N_DEV, g.shape[1])
    mix, ffn = (0, 1, 2, 3, 6), (4, 5)
    lands = {kind: [lax.empty((N_DEV,) + shards[i].shape, shards[i].dtype) for i in idx]
             for kind, idx in (("mix", mix), ("ffn", ffn))}
    own = {kind: [None] * depth for kind in lands}
    pending_s = {kind: None for kind in lands}

    def scatter(kind, l, srcs, after):
        if pending_s[kind] is not None:
            done, lands[kind] = _exchange_wait(*pending_s[kind][:4], after, scatter=True, slot=l + 1,
                                               name=f"scatter_wait_{kind}_{l + 1}")
            own[kind][l + 1] = [lax.dynamic_index_in_dim(t, me, 0, keepdims=False) for t in done]
        pending_s[kind] = _exchange_start(srcs, lands[kind], after, scatter=True, slot=l, name=f"scatter_start_{kind}_{l}")
        return pending_s[kind][-1][0, 0]

    for l in reversed(range(depth)):
        s, wl = saved[l], weights[l]
        sh1, sc1, gt1, sh2, sc2, gt2 = (mods[l, i] for i in range(6))
        gw_fo = _mm(s["act"], dt2, ta=True, out_dtype=BF16, name="ffn_out_dw", tm=1408)
        dgu = _ffn_out_bwd_swiglu(dt2, wl["wfo"], s["gu"], name="ffn_out_bwd_swiglu")
        gw_fi = _mm(dgu, s["h2"], ta=True, out_dtype=BF16, name="ffn_in_dw", tm=1408)
        sc2 = sc2 + scatter("ffn", l, [rows_slabs(gw_fi), rows_slabs(gw_fo)], gw_fi)
        dh2 = _mm(dgu, wl["wfi"], name="ffn_in_dx", tk=1408)
        dx, dt1, st2 = _adaln_bwd(s["x2"], n2g[l], sc2, sh2, dh2, dx, s["t1"], gt1, name="adaln2_bwd")
        gw_out = _mm(s["merged"], dt1, ta=True, out_dtype=BF16, name="out_dw")
        dgates, dpa, dpb = _out_bwd_merge(dt1, wl["wout"], s["proj"], s["pa"], s["pb"], name="out_bwd_merge")
        gw_a = _mm(s["ya"], dpa, ta=True, out_dtype=BF16, name="branch_a_dw")
        gw_b = _mm(s["yb"], dpb, ta=True, out_dtype=BF16, name="branch_b_dw")
        dya = _mm(dpa, wl["wa"], tb=True, name="branch_a_dx")
        dyb = _mm(dpb, wl["wb"], tb=True, name="branch_b_dx")
        dqa, dka, dva, dbias = _attn_bwd(s["proj"], s["kpad"], s["vpad"], s["bias"], s["ya"], s["lse"], dya,
                                             name="attn_bwd")
        ddiag = jnp.roll(_bias_bwd(dbias, name="bias_bwd"), -(QB - 1), axis=1)
        small_g["rel_table"][l] = s["bias_vjp"](ddiag)[0]
        dog, dz, dgn = _gdn_post_bwd(s["og"], s["proj"], gng[l], dyb, name="gdn_post_bwd")
        small_g["gdn_norm_g"][l] = dgn[0]
        dug, dss = _gdn_scan_bwd(s["wg"], s["qd"], s["kd"], s["qk"], s["gle"], dog, name="gdn_scan_bwd")
        dqn, dkn, dvn, daux = _gdn_bwd(s["qn"], s["kn"], s["vn"], s["aux"], s["auxt"], s["tinv"], s["u0"], s["wg"],
                                       s["ug"], s["ss"], dss, dug, dog, name="gdn_bwd")
        dqkv, dba, dwc, dpar = _gdn_pre_bwd(s["proj"], wl["wconv"], s["par"], dqn, dkn, dvn, daux, name="gdn_pre_bwd")
        small_g["a_log"][l] = dpar[0, B_HEADS:2 * B_HEADS]
        small_g["dt_bias"][l] = dpar[1, B_HEADS:2 * B_HEADS]
        dproj = jnp.concatenate([dgates, dqa, dka, dva, dqkv, dz, dba], axis=1)
        gw_in = _mm(dproj, s["h1"], ta=True, out_dtype=BF16, name="proj_dw", tm=1152)
        dh1 = _mm(dproj, wl["win"], name="proj_dx", tk=1152)
        mix_srcs = [rows_slabs(_win_from_mine(gw_in)), cols_slabs(gw_a), cols_slabs(gw_b), rows_slabs(gw_out),
                    cols_slabs(dwc[0:CONV_K])]
        if l > 0:
            sc1 = sc1 + scatter("mix", l, mix_srcs, gw_in)
        if l > 0:
            p = saved[l - 1]
            dx, dt2, st1 = _adaln_bwd(s["xin"], n1g[l], sc1, sh1, dh1, dx, p["t2"], mods[l - 1, 5], name="adaln1_bwd")
        else:
            dx, st1 = _adaln_bwd(s["xin"], n1g[l], sc1, sh1, dh1, dx, name="adaln1_bwd_first")
        small_g["norm1_g"][l], small_g["norm2_g"][l] = st1[0], st2[0]
        dmod_rows[l] = jnp.concatenate([st1[2], st1[1], st2[3], st2[2], st2[1], dgt2])
        if l > 0:
            dgt2 = st1[3]
    grad_x = dx[None]

    small_local = {n: (jnp.stack(vs) if isinstance(vs, list) else vs) for n, vs in small_g.items()}
    small_local["b_ada"] = jnp.stack(dmod_rows)
    (g_small,) = _all_gather([_pack_small(small_local, loss_part, depth)], name="gather_small")
    tok = scatter("mix", 0, mix_srcs, g_small)
    wsm = _pack_small(dict(b_ada=b_ada, norm1_g=norm1_g, norm2_g=norm2_g, rel_table=rel_table, a_log=a_log,
                           dt_bias=dt_bias, gdn_norm_g=gdn_norm_g, final_g=final_g), jnp.zeros((1,), F32) + tok, depth)
    msm = _pack_small(dict(b_ada=m_b_ada, norm1_g=m_norm1_g, norm2_g=m_norm2_g, rel_table=m_rel_table, a_log=m_a_log,
                           dt_bias=m_dt_bias, gdn_norm_g=m_gdn_norm_g, final_g=m_final_g), jnp.zeros((1,), F32), depth)
    vsm = _pack_small(dict(b_ada=v_b_ada, norm1_g=v_norm1_g, norm2_g=v_norm2_g, rel_table=v_rel_table, a_log=v_a_log,
                           dt_bias=v_dt_bias, gdn_norm_g=v_gdn_norm_g, final_g=v_final_g), jnp.ones((1,), F32), depth)
    sm = [_unpack_small(t, depth) for t in _adamw(g_small[:, None], wsm[None], msm[None], vsm[None], name="adamw_small")]
    loss = sm[0][1]

    dmod_all = g_small.reshape(N_DEV, -1)[:, :depth * 6 * D].reshape(N_DEV, depth, 6 * D)
    dmod_mine = lax.dynamic_slice_in_dim(dmod_all, me * (6 * D // N_DEV), 6 * D // N_DEV, axis=2)
    g_ada = jnp.stack([_mm(cond, _pad_rows(dmod_mine[:, l], 16), ta=True, name="ada_dw") for l in range(depth)])

    got, mine = {}, {}
    sel = jnp.broadcast_to(jnp.where(is_me[:, :, 0], 1.0, 0.0), (N_DEV, LANE)).astype(F32)

    def finish(kind, idx, after):
        done, lands[kind] = _exchange_wait(*pending_s[kind][:4], after, scatter=True, slot=0, name=f"scatter_wait_{kind}_0")
        own[kind][0] = [lax.dynamic_index_in_dim(t, me, 0, keepdims=False) for t in done]
        for a, i in enumerate(idx):
            got[i] = lands[kind][a]
            mine[i] = jnp.stack([own[kind][l][a] for l in range(depth)])

    def upd(i, w, m, v, name):
        if i in (0, 4):
            return [tr_(t) for t in _adamw(got[i], tr_(w), tr_(m), tr_(v), mine[i], sel, name=name)]
        return _adamw(got[i], w, m, v, mine[i], sel, name=name)

    finish("ffn", ffn, g_ada)
    res = {
        "w_ada": _adamw(g_ada[None], w_ada, m_w_ada, v_w_ada, name="adamw_w_ada"),
        "w_ffn_in": upd(4, w_ffn_in, m_w_ffn_in, v_w_ffn_in, "adamw_w_ffn_in"),
        "w_ffn_out": upd(5, w_ffn_out, m_w_ffn_out, v_w_ffn_out, "adamw_w_ffn_out"),
    }
    finish("mix", mix, res["w_ffn_out"][0])
    res.update({
        "w_in": upd(0, w_in, m_w_in, v_w_in, "adamw_w_in"),
        "w_conv": upd(6, w_conv, m_w_conv, v_w_conv, "adamw_w_conv"),
        "w_branch_a": upd(1, w_branch_a, m_w_branch_a, v_w_branch_a, "adamw_w_branch_a"),
        "w_branch_b": upd(2, w_branch_b, m_w_branch_b, v_w_branch_b, "adamw_w_branch_b"),
        "w_out": upd(3, w_out, m_w_out, v_w_out, "adamw_w_out"),
    })
    for n, _ in _small_spec(depth):
        res[n] = [sm[i][0][n] for i in range(4)]
    order = ("w_ada", "b_ada", "norm1_g", "norm2_g", "w_in", "rel_table", "w_conv", "a_log", "dt_bias", "gdn_norm_g",
             "w_branch_a", "w_branch_b", "w_out", "w_ffn_in", "w_ffn_out", "final_g")
    return (loss, grad_x, *[res[n][0] for n in order], *[res[n][1] for n in order],
            *[res[n][2] for n in order], *[res[n][3] for n in order])
```

```python
import functools
import math

import jax
import jax.numpy as jnp
from jax import lax
from jax.experimental import pallas as pl
from jax.experimental.pallas import tpu as pltpu

F32 = jnp.float32
BF16 = jnp.bfloat16
HI = lax.Precision.HIGHEST

N_DEV = 8
D = 1024
DEPTH = 4
CH = 64
EPS = 1e-6
A_HEADS, A_DH = 8, 64
A_W = A_HEADS * A_DH
A_PAST = 8
A_MAX_REL = 128
QB = 256
KB = QB + A_PAST * CH
B_HEADS, B_DH = 4, 128
B_W = B_HEADS * B_DH
CONV_K = 4
FF = 2816
IN_DIM = 5640
IN_PAD = 5760
LANE = 128
NEG = -1e30
VMEM_LIMIT = 48 * 1024 * 1024

ADAM_LR, ADAM_B1, ADAM_B2, ADAM_EPS, ADAM_WD, ADAM_STEP = 0.001, 0.9, 0.999, 1e-08, 0.01, 10

OFF_GA, OFF_GB, OFF_QA, OFF_KA, OFF_VA, OFF_QB, OFF_KB, OFF_VB, OFF_ZB, OFF_BA = (
    0, 1024, 2048, 2560, 3072, 3584, 4096, 4608, 5120, 5632)


def _cp(sem=None):
    return pltpu.CompilerParams(dimension_semantics=sem, vmem_limit_bytes=VMEM_LIMIT)


def _tile(n, pref):
    if n <= pref:
        return n
    best = None
    for t in range(LANE, pref + 1, LANE):
        if n % t == 0:
            best = t
    assert best is not None, (n, pref)
    return best


def _sigmoid(x):
    return 1.0 / (1.0 + jnp.exp(-x))


def _silu(x):
    return x * _sigmoid(x)


def _dsilu(x):
    s = _sigmoid(x)
    return s * (1.0 + x * (1.0 - s))


def _dot(a, b, prec=None):
    return jnp.dot(a, b, preferred_element_type=F32, precision=prec)


def _dot_nt(a, b, prec=None):
    return lax.dot_general(a, b, (((1,), (1,)), ((), ())), preferred_element_type=F32, precision=prec)


def _dot_tn(a, b, prec=None):
    return lax.dot_general(a, b, (((0,), (0,)), ((), ())), preferred_element_type=F32, precision=prec)


def _mm(a, b, *, ta=False, tb=False, out_dtype=F32, name, tm=1024, tn=1024, tk=1024):
    halves = a.ndim == 3
    a_rows, a_cols = (a.shape[1], 2 * a.shape[2]) if halves else a.shape
    m, k = (a_cols, a_rows) if ta else (a_rows, a_cols)
    n = b.shape[0] if tb else b.shape[1]
    assert k == (b.shape[1] if tb else b.shape[0]), (a.shape, b.shape, ta, tb)
    tm, tn, tk = _tile(m, tm), _tile(n, tn), _tile(k, tk)
    nk = k // tk
    dn = (((0 if ta else 1,), (1 if tb else 0,)), ((), ()))

    def body(a_ref, b_ref, o_ref, *acc):
        part = lax.dot_general(a_ref[...].astype(BF16), b_ref[...].astype(BF16), dn, preferred_element_type=F32)
        if nk == 1:
            o_ref[...] = part.astype(out_dtype)
            return
        acc_ref, kk = acc[0], pl.program_id(2)

        @pl.when(kk == 0)
        def _():
            acc_ref[...] = part

        @pl.when(kk > 0)
        def _():
            acc_ref[...] += part

        @pl.when(kk == nk - 1)
        def _():
            o_ref[...] = acc_ref[...].astype(out_dtype)

    if halves:
        per = a.shape[2] // (tm if ta else tk)
        a_spec = (pl.BlockSpec((None, tk, tm), lambda i, j, q: (i // per, q, i % per)) if ta else
                  pl.BlockSpec((None, tm, tk), lambda i, j, q: (q // per, i, q % per)))
    else:
        a_spec = pl.BlockSpec((tk, tm), lambda i, j, q: (q, i)) if ta else pl.BlockSpec((tm, tk), lambda i, j, q: (i, q))
    b_spec = pl.BlockSpec((tn, tk), lambda i, j, q: (j, q)) if tb else pl.BlockSpec((tk, tn), lambda i, j, q: (q, j))
    return pl.pallas_call(
        body, grid=(m // tm, n // tn, nk), in_specs=[a_spec, b_spec],
        out_specs=pl.BlockSpec((tm, tn), lambda i, j, q: (i, j)),
        out_shape=jax.ShapeDtypeStruct((m, n), out_dtype),
        scratch_shapes=[pltpu.VMEM((tm, tn), F32)] if nk > 1 else [],
        compiler_params=_cp(("parallel", "parallel", "arbitrary")), name=name)(a, b)


def _rb(tr, width, cb=0):
    return pl.BlockSpec((tr, width), lambda i: (i, cb))


def _whole(shape):
    nd = len(shape)
    return pl.BlockSpec(shape, lambda i: (0,) * nd)


def _colsum(v):
    return jnp.sum(v, axis=0, keepdims=True)


def _adaln_fwd(x, g, sc, sh, t=None, gt=None, *, name, tr=256):
    tt = x.shape[0]
    res = t is not None

    def body(*refs):
        if res:
            x_ref, t_ref, gt_ref, g_ref, sc_ref, sh_ref, xo_ref, h_ref = refs
            xv = x_ref[...] + gt_ref[...] * t_ref[...]
            xo_ref[...] = xv
        else:
            x_ref, g_ref, sc_ref, sh_ref, h_ref = refs
            xv = x_ref[...]
        r = lax.rsqrt(jnp.mean(xv * xv, axis=-1, keepdims=True) + EPS)
        h_ref[...] = ((xv * r * g_ref[...]) * (1.0 + sc_ref[...]) + sh_ref[...]).astype(BF16)

    row, vec = _rb(tr, D), _whole((1, D))
    if res:
        ins, in_specs = (x, t, gt, g, sc, sh), [row, row, vec, vec, vec, vec]
        out_shape = (jax.ShapeDtypeStruct((tt, D), F32), jax.ShapeDtypeStruct((tt, D), BF16))
        out_specs = (row, row)
    else:
        ins, in_specs = (x, g, sc, sh), [row, vec, vec, vec]
        out_shape, out_specs = jax.ShapeDtypeStruct((tt, D), BF16), row
    out = pl.pallas_call(body, grid=(tt // tr,), in_specs=in_specs, out_specs=out_specs, out_shape=out_shape,
                         compiler_params=_cp(("parallel",)), name=name)(*ins)
    return out if res else (x, out)


def _mm_adaln_bwd(a, b, x, g, sc, sh, dx_in, t=None, gt=None, *, name, tk, tm=512):
    tt = x.shape[0]
    res = t is not None
    halves = a.ndim == 3
    k = 2 * a.shape[2] if halves else a.shape[1]
    tm, nk = _tile(tt, tm), k // tk

    def body(*refs):
        if res:
            a_ref, b_ref, x_ref, g_ref, sc_ref, sh_ref, dxi_ref, t_ref, gt_ref, dx_ref, dt_ref, st_ref, acc_ref = refs
        else:
            a_ref, b_ref, x_ref, g_ref, sc_ref, sh_ref, dxi_ref, dx_ref, st_ref, acc_ref = refs
        i, q = pl.program_id(0), pl.program_id(1)
        part = _dot(a_ref[...], b_ref[...])

        @pl.when((i == 0) & (q == 0))
        def _():
            st_ref[...] = jnp.zeros_like(st_ref)

        @pl.when(q == 0)
        def _():
            acc_ref[...] = part

        @pl.when(q > 0)
        def _():
            acc_ref[...] += part

        @pl.when(q == nk - 1)
        def _():
            xv, dh = x_ref[...], acc_ref[...]
            r = lax.rsqrt(jnp.mean(xv * xv, axis=-1, keepdims=True) + EPS)
            nrm = xv * r
            y = nrm * g_ref[...]
            dy = dh * (1.0 + sc_ref[...])
            dn = dy * g_ref[...]
            dx = dxi_ref[...] + r * (dn - nrm * jnp.mean(dn * nrm, axis=-1, keepdims=True))
            dx_ref[...] = dx
            st_ref[0:1, :] += _colsum(dy * nrm)
            st_ref[1:2, :] += _colsum(dh * y)
            st_ref[2:3, :] += _colsum(dh)
            if res:
                dt_ref[...] = (gt_ref[...] * dx).astype(BF16)
                st_ref[3:4, :] += _colsum(dx * t_ref[...])

    if halves:
        per = a.shape[2] // tk
        a_spec = pl.BlockSpec((None, tm, tk), lambda i, q: (q // per, i, q % per))
    else:
        a_spec = pl.BlockSpec((tm, tk), lambda i, q: (i, q))
    row = pl.BlockSpec((tm, D), lambda i, q: (i, 0))
    vec = pl.BlockSpec((1, D), lambda i, q: (0, 0))
    ins = [a, b, x, g, sc, sh, dx_in]
    in_specs = [a_spec, pl.BlockSpec((tk, D), lambda i, q: (q, 0)), row, vec, vec, vec, row]
    out_shape, out_specs = [jax.ShapeDtypeStruct((tt, D), F32)], [row]
    if res:
        ins += [t, gt]
        in_specs += [row, vec]
        out_shape.append(jax.ShapeDtypeStruct((tt, D), BF16))
        out_specs.append(row)
    out_shape.append(jax.ShapeDtypeStruct((8, D), F32))
    out_specs.append(pl.BlockSpec((8, D), lambda i, q: (0, 0)))
    return pl.pallas_call(body, grid=(tt // tm, nk), in_specs=in_specs, out_specs=tuple(out_specs),
                          out_shape=tuple(out_shape), scratch_shapes=[pltpu.VMEM((tm, D), F32)],
                          compiler_params=_cp(("arbitrary", "arbitrary")), name=name)(*ins)


def _loss_head(x, t, gt, fg, tgt, *, name, tr=256):
    tt = x.shape[0]

    def body(x_ref, t_ref, gt_ref, fg_ref, tgt_ref, dx_ref, dt_ref, st_ref):
        @pl.when(pl.program_id(0) == 0)
        def _():
            st_ref[...] = jnp.zeros_like(st_ref)

        tv = t_ref[...]
        xv = x_ref[...] + gt_ref[...] * tv
        r = lax.rsqrt(jnp.mean(xv * xv, axis=-1, keepdims=True) + EPS)
        nrm = xv * r
        err = nrm * fg_ref[...] - tgt_ref[...]
        st_ref[4:5, :] += 0.5 * jnp.sum(jnp.mean(err * err, axis=-1, keepdims=True), axis=0, keepdims=True)
        dy = err * (1.0 / D)
        dn = dy * fg_ref[...]
        dx = r * (dn - nrm * jnp.mean(dn * nrm, axis=-1, keepdims=True))
        dx_ref[...] = dx
        dt_ref[...] = (gt_ref[...] * dx).astype(BF16)
        st_ref[0:1, :] += _colsum(dy * nrm)
        st_ref[3:4, :] += _colsum(dx * tv)

    row, vec = _rb(tr, D), _whole((1, D))
    return pl.pallas_call(
        body, grid=(tt // tr,), in_specs=[row, row, vec, vec, row], out_specs=(row, row, _whole((8, D))),
        out_shape=(jax.ShapeDtypeStruct((tt, D), F32), jax.ShapeDtypeStruct((tt, D), BF16),
                   jax.ShapeDtypeStruct((8, D), F32)),
        compiler_params=_cp(("arbitrary",)), name=name)(x, t, gt, fg, tgt)


def _branch_merge(ya, yb, wa, wb, proj, *, name, tm=512):
    tt = ya.shape[0]
    tm = _tile(tt, tm)

    def body(ya_ref, yb_ref, wa_ref, wb_ref, ga_ref, gb_ref, pa_ref, pb_ref, o_ref):
        pa = _dot(ya_ref[...], wa_ref[...])
        pb = _dot(yb_ref[...], wb_ref[...])
        pa_ref[...] = pa.astype(BF16)
        pb_ref[...] = pb.astype(BF16)
        o_ref[...] = (_sigmoid(ga_ref[...]) * pa + _sigmoid(gb_ref[...]) * pb).astype(BF16)

    row, half, wsp = _rb(tm, D), _rb(tm, A_W), _whole((A_W, D))
    out = jax.ShapeDtypeStruct((tt, D), BF16)
    return pl.pallas_call(body, grid=(tt // tm,), in_specs=[half, half, wsp, wsp, _rb(tm, D, 0), _rb(tm, D, 1)],
                          out_specs=(row, row, row), out_shape=(out, out, out), compiler_params=_cp(("parallel",)),
                          name=name)(ya, yb, wa, wb, proj, proj)


def _out_adaln(a, w, x, gt, g, sc, sh, *, name, tk=None, tm=512):
    tt, k = a.shape
    tm, tk = _tile(tt, tm), tk or k
    nk = k // tk

    def body(a_ref, w_ref, x_ref, gt_ref, g_ref, sc_ref, sh_ref, t_ref, xo_ref, h_ref):
        q = pl.program_id(1)
        part = _dot(a_ref[...], w_ref[...])

        @pl.when(q == 0)
        def _():
            t_ref[...] = part

        @pl.when(q > 0)
        def _():
            t_ref[...] += part

        @pl.when(q == nk - 1)
        def _():
            xv = x_ref[...] + gt_ref[...] * t_ref[...]
            xo_ref[...] = xv
            r = lax.rsqrt(jnp.mean(xv * xv, axis=-1, keepdims=True) + EPS)
            h_ref[...] = ((xv * r * g_ref[...]) * (1.0 + sc_ref[...]) + sh_ref[...]).astype(BF16)

    row = pl.BlockSpec((tm, D), lambda i, q: (i, 0))
    vec = pl.BlockSpec((1, D), lambda i, q: (0, 0))
    f32 = jax.ShapeDtypeStruct((tt, D), F32)
    return pl.pallas_call(
        body, grid=(tt // tm, nk),
        in_specs=[pl.BlockSpec((tm, tk), lambda i, q: (i, q)), pl.BlockSpec((tk, D), lambda i, q: (q, 0)),
                  row, vec, vec, vec, vec],
        out_specs=(row, row, row), out_shape=(f32, f32, jax.ShapeDtypeStruct((tt, D), BF16)),
        compiler_params=_cp(("parallel", "arbitrary")), name=name)(a, w, x, gt, g, sc, sh)


def _out_bwd_merge(dt, wout, proj, pa, pb, *, name, tm=512):
    tt = dt.shape[0]
    tm = _tile(tt, tm)

    def body(dt_ref, w_ref, ga_ref, gb_ref, pa_ref, pb_ref, dg_ref, dpa_ref, dpb_ref):
        dm_v = _dot_nt(dt_ref[...], w_ref[...])
        sa, sb = _sigmoid(ga_ref[...]), _sigmoid(gb_ref[...])
        dpa_ref[...] = (dm_v * sa).astype(BF16)
        dpb_ref[...] = (dm_v * sb).astype(BF16)
        dg_ref[:, 0:D] = (dm_v * pa_ref[...].astype(F32) * sa * (1.0 - sa)).astype(BF16)
        dg_ref[:, D:2 * D] = (dm_v * pb_ref[...].astype(F32) * sb * (1.0 - sb)).astype(BF16)

    row = _rb(tm, D)
    return pl.pallas_call(
        body, grid=(tt // tm,), in_specs=[row, _whole((D, D)), _rb(tm, D, 0), _rb(tm, D, 1), row, row],
        out_specs=(_rb(tm, 2 * D), row, row),
        out_shape=(jax.ShapeDtypeStruct((tt, 2 * D), BF16), jax.ShapeDtypeStruct((tt, D), BF16),
                   jax.ShapeDtypeStruct((tt, D), BF16)),
        compiler_params=_cp(("parallel",)), name=name)(dt, wout, proj, proj, pa, pb)


FTN = FF // 2


def _ffn_in_swiglu(h, wt, *, name, tm=1024):
    tt = h.shape[0]
    tm = _tile(tt, tm)

    def body(h_ref, wg_ref, wu_ref, gu_ref, act_ref):
        hv = h_ref[...]
        g = _dot_nt(hv, wg_ref[...])
        u = _dot_nt(hv, wu_ref[...])
        gu_ref[0] = g.astype(BF16)
        gu_ref[1] = u.astype(BF16)
        act_ref[...] = (_silu(g) * u).astype(BF16)

    nj = FF // FTN
    return pl.pallas_call(
        body, grid=(tt // tm, nj),
        in_specs=[pl.BlockSpec((tm, D), lambda i, j: (i, 0)), pl.BlockSpec((FTN, D), lambda i, j: (j, 0)),
                  pl.BlockSpec((FTN, D), lambda i, j: (j + nj, 0))],
        out_specs=(pl.BlockSpec((2, tm, FTN), lambda i, j: (0, i, j)), pl.BlockSpec((tm, FTN), lambda i, j: (i, j))),
        out_shape=(jax.ShapeDtypeStruct((2, tt, FF), BF16), jax.ShapeDtypeStruct((tt, FF), BF16)),
        compiler_params=_cp(("parallel", "parallel")), name=name)(h, wt, wt)


def _ffn_out_bwd_swiglu(dt, wo, gu, *, name, tm=1024):
    tt = dt.shape[0]
    tm = _tile(tt, tm)

    def body(dt_ref, wo_ref, gu_ref, dgu_ref):
        da = _dot_nt(dt_ref[...], wo_ref[...])
        g, u = gu_ref[0].astype(F32), gu_ref[1].astype(F32)
        dgu_ref[0] = (da * u * _dsilu(g)).astype(BF16)
        dgu_ref[1] = (da * _silu(g)).astype(BF16)

    blk = pl.BlockSpec((2, tm, FTN), lambda i, j: (0, i, j))
    return pl.pallas_call(
        body, grid=(tt // tm, FF // FTN),
        in_specs=[pl.BlockSpec((tm, D), lambda i, j: (i, 0)), pl.BlockSpec((FTN, D), lambda i, j: (j, 0)), blk],
        out_specs=blk, out_shape=jax.ShapeDtypeStruct((2, tt, FF), BF16),
        compiler_params=_cp(("parallel", "parallel")), name=name)(dt, wo, gu)


BIAS_LW = 1152


def _bias_diagonals(table):
    n_far = A_PAST * CH - A_MAX_REL + 1
    far = jnp.broadcast_to(table[:, 2 * A_MAX_REL:], (A_HEADS, n_far))
    mid = jnp.flip(table[:, 1:2 * A_MAX_REL], axis=1)
    near = jnp.broadcast_to(table[:, 0:1], (A_HEADS, KB - n_far - (2 * A_MAX_REL - 1)))
    pos = jnp.concatenate([far, mid, near], axis=1)
    neg = jnp.broadcast_to(table[:, 2 * A_MAX_REL:], (A_HEADS, QB - 1))
    gap = jnp.zeros((A_HEADS, BIAS_LW - KB - (QB - 1)), F32)
    return jnp.concatenate([pos, gap, neg], axis=1)


def _bias_fwd(diag, *, name):
    def body(w_ref, o_ref):
        qc = lax.broadcasted_iota(jnp.int32, (QB, KB), 0) // CH + A_PAST
        col = lax.broadcasted_iota(jnp.int32, (QB, KB), 1)
        inband = (col // CH <= qc) & (col // CH >= qc - A_PAST)
        for h in range(A_HEADS):
            rows = pltpu.roll(jnp.broadcast_to(w_ref[h:h + 1, :], (QB, BIAS_LW)), 0, 1, stride=1, stride_axis=0)
            for var in range(3):
                o_ref[var, h] = jnp.where(inband & (col >= A_PAST * CH - QB * var), rows[:, :KB], NEG)

    return pl.pallas_call(body, out_shape=jax.ShapeDtypeStruct((3, A_HEADS, QB, KB), F32), compiler_params=_cp(),
                          name=name)(diag)


def _bias_bwd(dbias, *, name):
    def body(d_ref, o_ref):
        r = lax.broadcasted_iota(jnp.int32, (QB, QB), 0)
        c = lax.broadcasted_iota(jnp.int32, (QB, QB), 1)
        flip = jnp.where(r + c == QB - 1, 1.0, 0.0).astype(F32)
        for h in range(A_HEADS):
            x = jnp.concatenate([_dot(flip, d_ref[h], HI), jnp.zeros((QB, BIAS_LW - KB), F32)], axis=1)
            o_ref[h:h + 1, :] = jnp.sum(pltpu.roll(x, 0, 1, stride=1, stride_axis=0), axis=0, keepdims=True)

    return pl.pallas_call(body, out_shape=jax.ShapeDtypeStruct((A_HEADS, BIAS_LW), F32), compiler_params=_cp(),
                          name=name)(dbias)


def _kv_pad(proj, *, name, tr=256):
    tt = proj.shape[0]
    npad = A_PAST * CH // tr

    def body(k_ref, v_ref, ko_ref, vo_ref):
        i = pl.program_id(0)

        @pl.when(i < npad)
        def _():
            ko_ref[...] = jnp.zeros_like(ko_ref)
            vo_ref[...] = jnp.zeros_like(vo_ref)

        @pl.when(i >= npad)
        def _():
            ko_ref[...] = k_ref[...].astype(BF16)
            vo_ref[...] = v_ref[...].astype(BF16)

    src = lambda off: pl.BlockSpec((tr, A_W), lambda i: (jnp.maximum(i - npad, 0), off // A_W))
    out = jax.ShapeDtypeStruct((tt + A_PAST * CH, A_W), BF16)
    return pl.pallas_call(body, grid=(tt // tr + npad,), in_specs=[src(OFF_KA), src(OFF_VA)],
                          out_specs=(_rb(tr, A_W), _rb(tr, A_W)), out_shape=(out, out),
                          compiler_params=_cp(("parallel",)), name=name)(proj, proj)


def _attn_fwd(proj, kpad, vpad, bias, *, name):
    tt = proj.shape[0]

    def body(q_ref, k_ref, v_ref, b_ref, o_ref, l_ref):
        q0 = pl.multiple_of(pl.program_id(1) * QB, QB)
        q = q_ref[...] * (A_DH ** -0.5)
        k = k_ref[pl.ds(q0, KB), :]
        v = v_ref[pl.ds(q0, KB), :]
        lane = lax.broadcasted_iota(jnp.int32, (QB, LANE), 1)
        o = jnp.zeros((QB, LANE), F32)
        lse = jnp.zeros((QB, LANE), F32)
        for a in range(2):
            hm = (lane >= A_DH * a) & (lane < A_DH * (a + 1))
            s = _dot_nt(jnp.where(hm, q, 0.0).astype(BF16), k) + b_ref[a]
            m = jnp.max(s, axis=-1, keepdims=True)
            p = jnp.exp(s - m)
            l = jnp.sum(p, axis=-1, keepdims=True)
            o = jnp.where(hm, _dot(p.astype(BF16), v) / l, o)
            lse = jnp.where(hm, m + jnp.log(l), lse)
        o_ref[...] = o.astype(BF16)
        l_ref[...] = lse

    kv = pl.BlockSpec((tt + A_PAST * CH, LANE), lambda h, i: (0, h))
    blk = pl.BlockSpec((QB, LANE), lambda h, i: (i, h))
    return pl.pallas_call(
        body, grid=(A_W // LANE, tt // QB),
        in_specs=[pl.BlockSpec((QB, LANE), lambda h, i: (i, OFF_QA // LANE + h)), kv, kv,
                  pl.BlockSpec((None, 2, QB, KB), lambda h, i: (jnp.minimum(i, 2), h, 0, 0))],
        out_specs=(blk, blk),
        out_shape=(jax.ShapeDtypeStruct((tt, A_W), BF16), jax.ShapeDtypeStruct((tt, A_W), F32)),
        compiler_params=_cp(("parallel", "parallel")), name=name)(proj, kpad, vpad, bias)


def _attn_bwd(proj, kpad, vpad, bias, o, lse, do, *, name):
    tt = proj.shape[0]
    nq = tt // QB

    def body(q_ref, k_ref, v_ref, b_ref, o_ref, l_ref, do_ref, dq_ref, dko_ref, dvo_ref, db_ref, dk_ref, dv_ref):
        @pl.when(pl.program_id(1) == 0)
        def _():
            dk_ref[...] = jnp.zeros_like(dk_ref)
            dv_ref[...] = jnp.zeros_like(dv_ref)
            db_ref[...] = jnp.zeros_like(db_ref)

        q0 = pl.multiple_of(pl.program_id(1) * QB, QB)
        q, do_v, lse = q_ref[...] * (A_DH ** -0.5), do_ref[...], l_ref[...]
        k = k_ref[pl.ds(q0, KB), :]
        v = v_ref[pl.ds(q0, KB), :]
        dsum = do_v * o_ref[...].astype(F32)
        lane = lax.broadcasted_iota(jnp.int32, (QB, LANE), 1)
        dq = jnp.zeros((QB, LANE), F32)
        dk = jnp.zeros((KB, LANE), F32)
        dv = jnp.zeros((KB, LANE), F32)
        for a in range(2):
            hm = (lane >= A_DH * a) & (lane < A_DH * (a + 1))
            qa = jnp.where(hm, q, 0.0).astype(BF16)
            doa = jnp.where(hm, do_v, 0.0).astype(BF16)
            s = _dot_nt(qa, k) + b_ref[a]
            lse_a = jnp.max(jnp.where(hm, lse, NEG), axis=-1, keepdims=True)
            p = jnp.exp(s - lse_a)
            dp = _dot_nt(doa, v)
            dsum_a = jnp.sum(jnp.where(hm, dsum, 0.0), axis=-1, keepdims=True)
            ds = p * (dp - dsum_a)
            db_ref[a] += ds
            dsb = ds.astype(BF16)
            dq = jnp.where(hm, _dot(dsb, k) * (A_DH ** -0.5), dq)
            dk += _dot_tn(dsb, qa)
            dv += _dot_tn(p.astype(BF16), doa)
        dq_ref[...] = dq.astype(BF16)
        dk_ref[pl.ds(q0, KB), :] += dk
        dv_ref[pl.ds(q0, KB), :] += dv

        @pl.when(pl.program_id(1) == nq - 1)
        def _():
            dko_ref[...] = dk_ref[A_PAST * CH:, :].astype(BF16)
            dvo_ref[...] = dv_ref[A_PAST * CH:, :].astype(BF16)

    kv = pl.BlockSpec((tt + A_PAST * CH, LANE), lambda h, i: (0, h))
    blk = pl.BlockSpec((QB, LANE), lambda h, i: (i, h))
    col = pl.BlockSpec((tt, LANE), lambda h, i: (0, h))
    bsp = pl.BlockSpec((2, QB, KB), lambda h, i: (h, 0, 0))
    bias_in = pl.BlockSpec((None, 2, QB, KB), lambda h, i: (jnp.minimum(i, 2), h, 0, 0))
    out = jax.ShapeDtypeStruct((tt, A_W), BF16)
    return pl.pallas_call(
        body, grid=(A_W // LANE, nq),
        in_specs=[pl.BlockSpec((QB, LANE), lambda h, i: (i, OFF_QA // LANE + h)), kv, kv, bias_in, blk, blk, blk],
        out_specs=(blk, col, col, bsp),
        out_shape=(out, out, out, jax.ShapeDtypeStruct((A_HEADS, QB, KB), F32)),
        scratch_shapes=[pltpu.VMEM((tt + A_PAST * CH, LANE), F32), pltpu.VMEM((tt + A_PAST * CH, LANE), F32)],
        compiler_params=_cp(("parallel", "arbitrary")), name=name)(proj, kpad, vpad, bias, o, lse, do)


GTR = 256


def _taps(w_ref, grp):
    return [w_ref[j:j + 1, grp * B_W:(grp + 1) * B_W] for j in range(CONV_K)]


def _shifts(xe, rows):
    return [xe[8:8 + rows]] + [pltpu.roll(xe, s, 0)[8:8 + rows] for s in range(1, CONV_K)]


def _conv(shifts, taps):
    acc = taps[CONV_K - 1] * shifts[0]
    for s in range(1, CONV_K):
        acc = acc + taps[CONV_K - 1 - s] * shifts[s]
    return acc


def _qk_scale(grp):
    return B_DH ** -0.5 if grp == 0 else 1.0


def _act_fwd(c, grp):
    y = _silu(c)
    if grp == 2:
        return y
    parts = []
    for hd in range(B_HEADS):
        yh = y[:, hd * B_DH:(hd + 1) * B_DH]
        parts.append(yh * (lax.rsqrt(jnp.sum(yh * yh, axis=-1, keepdims=True) + EPS) * _qk_scale(grp)))
    return jnp.concatenate(parts, axis=1)


def _act_bwd(c, dy, grp):
    if grp == 2:
        return dy * _dsilu(c)
    y = _silu(c)
    parts = []
    for hd in range(B_HEADS):
        yh = y[:, hd * B_DH:(hd + 1) * B_DH]
        r = lax.rsqrt(jnp.sum(yh * yh, axis=-1, keepdims=True) + EPS)
        dyh = dy[:, hd * B_DH:(hd + 1) * B_DH] * _qk_scale(grp)
        parts.append(r * dyh - yh * (r * r * r) * jnp.sum(dyh * yh, axis=-1, keepdims=True))
    return jnp.concatenate(parts, axis=1) * _dsilu(c)


def _chunk_tri(n, upper=False):
    r = lax.broadcasted_iota(jnp.int32, (n, n), 0)
    c = lax.broadcasted_iota(jnp.int32, (n, n), 1)
    same = (r // CH) == (c // CH)
    return jnp.where(same & ((r <= c) if upper else (r >= c)), 1.0, 0.0).astype(F32)


def _gate_rows(ba, par_ref):
    lane = lax.broadcasted_iota(jnp.int32, ba.shape, 1)
    z = ba + par_ref[1:2, :]
    sp = jnp.maximum(z, 0.0) + jnp.log(1.0 + jnp.exp(-jnp.abs(z)))
    g = -jnp.exp(par_ref[0:1, :]) * sp
    return jnp.where(lane < B_HEADS, _sigmoid(ba), jnp.where(lane < 2 * B_HEADS, g, 0.0)), z


def _prev8(cb):
    return pl.BlockSpec((8, B_W), lambda i: (jnp.maximum(i * (GTR // 8) - 1, 0), cb))


def _next8(cb, nb):
    return pl.BlockSpec((8, B_W), lambda i: (jnp.minimum((i + 1) * (GTR // 8), nb * (GTR // 8) - 1), cb))


def _gdn_pre_fwd(proj, wconv, par, *, name):
    tt = proj.shape[0]

    def body(q_ref, k_ref, v_ref, qh_ref, kh_ref, vh_ref, ba_ref, w_ref, par_ref, qo_ref, ko_ref, vo_ref, aux_ref):
        first = pl.program_id(0) == 0
        for grp, (x_ref, h_ref, o_ref) in enumerate(((q_ref, qh_ref, qo_ref), (k_ref, kh_ref, ko_ref),
                                                     (v_ref, vh_ref, vo_ref))):
            xe = jnp.concatenate([jnp.where(first, 0.0, h_ref[...]), x_ref[...]], axis=0)
            o_ref[...] = _act_fwd(_conv(_shifts(xe, GTR), _taps(w_ref, grp)), grp)
        bg, _ = _gate_rows(ba_ref[...], par_ref)
        lane = lax.broadcasted_iota(jnp.int32, bg.shape, 1)
        aux_ref[...] = jnp.where(lane < B_HEADS, bg, _dot(_chunk_tri(GTR), bg, HI))

    col = lambda off: _rb(GTR, B_W, off // B_W)
    outs = jax.ShapeDtypeStruct((tt, B_W), F32)
    return pl.pallas_call(
        body, grid=(tt // GTR,),
        in_specs=[col(OFF_QB), col(OFF_KB), col(OFF_VB), _prev8(OFF_QB // B_W), _prev8(OFF_KB // B_W),
                  _prev8(OFF_VB // B_W), _rb(GTR, LANE, OFF_BA // LANE), _whole((CONV_K, 3 * B_W)),
                  _whole((8, LANE))],
        out_specs=(_rb(GTR, B_W), _rb(GTR, B_W), _rb(GTR, B_W), _rb(GTR, LANE)),
        out_shape=(outs, outs, outs, jax.ShapeDtypeStruct((tt, LANE), F32)),
        compiler_params=_cp(("parallel",)), name=name)(proj, proj, proj, proj, proj, proj, proj, wconv, par)


def _gdn_pre_bwd(proj, wconv, par, dq, dk, dv, daux, *, name):
    tt = proj.shape[0]
    nb = tt // GTR

    def body(q_ref, k_ref, v_ref, qh_ref, kh_ref, vh_ref, qn_ref, kn_ref, vn_ref, ba_ref, w_ref, par_ref,
             dq_ref, dk_ref, dv_ref, dqn_ref, dkn_ref, dvn_ref, daux_ref, dx_ref, dba_ref, dw_ref, dpar_ref):
        i = pl.program_id(0)
        first, last = i == 0, i == nb - 1

        @pl.when(first)
        def _():
            dw_ref[...] = jnp.zeros_like(dw_ref)
            dpar_ref[...] = jnp.zeros_like(dpar_ref)

        groups = ((q_ref, qh_ref, qn_ref, dq_ref, dqn_ref), (k_ref, kh_ref, kn_ref, dk_ref, dkn_ref),
                  (v_ref, vh_ref, vn_ref, dv_ref, dvn_ref))
        for grp, (x_ref, h_ref, xn_ref, d_ref, dn_ref) in enumerate(groups):
            taps = _taps(w_ref, grp)
            xe = jnp.concatenate([jnp.where(first, 0.0, h_ref[...]), x_ref[...]], axis=0)
            sh = _shifts(xe, GTR)
            dc = _act_bwd(_conv(sh, taps), d_ref[...], grp)
            xe_n = jnp.concatenate([x_ref[GTR - 8:GTR, :], xn_ref[...]], axis=0)
            dcn = _act_bwd(_conv(_shifts(xe_n, 8), taps), dn_ref[...], grp)
            dce = jnp.concatenate([dc, jnp.where(last, 0.0, dcn)], axis=0)
            dx = taps[CONV_K - 1] * dc
            dw_ref[CONV_K - 1:CONV_K, grp * B_W:(grp + 1) * B_W] += _colsum(dc * sh[0])
            for s in range(1, CONV_K):
                dx = dx + taps[CONV_K - 1 - s] * pltpu.roll(dce, GTR + 8 - s, 0)[0:GTR]
                dw_ref[CONV_K - 1 - s:CONV_K - s, grp * B_W:(grp + 1) * B_W] += _colsum(dc * sh[s])
            dx_ref[:, grp * B_W:(grp + 1) * B_W] = dx.astype(BF16)
        ba = ba_ref[...]
        lane = lax.broadcasted_iota(jnp.int32, ba.shape, 1)
        bg, z = _gate_rows(ba, par_ref)
        daux_v = daux_ref[...]
        dg = _dot(_chunk_tri(GTR, upper=True), daux_v, HI)
        dgl = jnp.where((lane >= B_HEADS) & (lane < 2 * B_HEADS), dg, 0.0)
        da = dgl * (-jnp.exp(par_ref[0:1, :])) * _sigmoid(z)
        dbr = jnp.where(lane < B_HEADS, daux_v * bg * (1.0 - bg), 0.0)
        dba_ref[...] = (dbr + da).astype(BF16)
        dpar_ref[0:1, :] += _colsum(dgl * bg)
        dpar_ref[1:2, :] += _colsum(da)

    col = lambda off: _rb(GTR, B_W, off // B_W)
    row, rowl = _rb(GTR, B_W), _rb(GTR, LANE)
    return pl.pallas_call(
        body, grid=(nb,),
        in_specs=[col(OFF_QB), col(OFF_KB), col(OFF_VB),
                  _prev8(OFF_QB // B_W), _prev8(OFF_KB // B_W), _prev8(OFF_VB // B_W),
                  _next8(OFF_QB // B_W, nb), _next8(OFF_KB // B_W, nb), _next8(OFF_VB // B_W, nb),
                  _rb(GTR, LANE, OFF_BA // LANE), _whole((CONV_K, 3 * B_W)), _whole((8, LANE)),
                  row, row, row, _next8(0, nb), _next8(0, nb), _next8(0, nb), rowl],
        out_specs=(_rb(GTR, 3 * B_W), rowl, _whole((8, 3 * B_W)), _whole((8, LANE))),
        out_shape=(jax.ShapeDtypeStruct((tt, 3 * B_W), BF16), jax.ShapeDtypeStruct((tt, LANE), BF16),
                   jax.ShapeDtypeStruct((8, 3 * B_W), F32), jax.ShapeDtypeStruct((8, LANE), F32)),
        compiler_params=_cp(("arbitrary",)), name=name)(
            proj, proj, proj, proj, proj, proj, proj, proj, proj, proj, wconv, par, dq, dk, dv, dq, dk, dv, daux)


def _col(x, j):
    lane = lax.broadcasted_iota(jnp.int32, x.shape, 1)
    return jnp.sum(jnp.where(lane == j, x, 0.0), axis=-1, keepdims=True)


def _split(x):
    hi = x.astype(BF16)
    return hi, (x - hi.astype(F32)).astype(BF16)


def _dot3(a, b, tn=False):
    dot = _dot_tn if tn else _dot
    (ah, al), (bh, bl) = _split(a), _split(b)
    return dot(ah, bh) + (dot(ah, bl) + dot(al, bh))


def _chunk_masks():
    r = lax.broadcasted_iota(jnp.int32, (CH, CH), 0)
    c = lax.broadcasted_iota(jnp.int32, (CH, CH), 1)
    return r > c, r >= c


def _gc_rows(aux, nc):
    t = jnp.transpose(aux[:, B_HEADS:2 * B_HEADS].reshape(nc, CH, B_HEADS), (0, 2, 1))
    return jnp.concatenate([t, jnp.zeros_like(t)], axis=1).reshape(nc * 8, CH)


_CHUNK8 = lambda width, n=1: pl.BlockSpec((8 * n, width), lambda i: (i, 0))
_CHUNK4 = lambda a, b, n=1: pl.BlockSpec((B_HEADS * n, a, b), lambda i: (i, 0, 0))
NCH = 2


def _per_chunk(body, rows):
    def wrapped(*refs):
        for ci in range(NCH):
            body(*[r.at[pl.ds(ci * n, n)] for r, n in zip(refs, rows)])
    return wrapped


def _gdn_lower(k, aux, auxt, *, name):
    tt = k.shape[0]

    def body(k_ref, aux_ref, auxt_ref, l_ref):
        aux_v = aux_ref[...]
        strict, _ = _chunk_masks()
        for hd in range(B_HEADS):
            kh = k_ref[:, hd * B_DH:(hd + 1) * B_DH].astype(BF16)
            diff = _col(aux_v, B_HEADS + hd) - auxt_ref[hd:hd + 1, :]
            dec = jnp.exp(jnp.where(strict, diff, NEG))
            l_ref[hd] = _col(aux_v, hd) * _dot_nt(kh, kh) * dec

    return pl.pallas_call(
        _per_chunk(body, (CH, CH, 8, B_HEADS)), grid=(tt // CH // NCH,),
        in_specs=[_rb(NCH * CH, B_W), _rb(NCH * CH, LANE), _CHUNK8(CH, NCH)],
        out_specs=_CHUNK4(CH, CH, NCH),
        out_shape=jax.ShapeDtypeStruct((tt // CH * B_HEADS, CH, CH), F32),
        compiler_params=_cp(("parallel",)), name=name)(k, aux, auxt)


def _tri_inverse(lt, *, name):
    nb = lt.shape[2]

    def body(l_ref, t_ref):
        rowid = lax.broadcasted_iota(jnp.int32, (CH, nb), 0)

        def outer(i, carry):
            def inner(j, acc):
                return acc + l_ref[i, pl.ds(j, 1), :] * t_ref[j]

            acc = lax.fori_loop(0, i, inner, jnp.zeros((CH, nb), F32))
            t_ref[i] = jnp.where(rowid == i, 1.0, 0.0) - acc
            return carry

        lax.fori_loop(0, CH, outer, 0)

    return pl.pallas_call(body, out_shape=jax.ShapeDtypeStruct(lt.shape, F32),
                          in_specs=[pl.BlockSpec(memory_space=pltpu.VMEM)],
                          out_specs=pl.BlockSpec(memory_space=pltpu.VMEM),
                          compiler_params=_cp(), name=name)(lt)


def _gdn_gates(aux_v, aux_last, auxt_ref, hd):
    _, incl = _chunk_masks()
    beta = _col(aux_v, hd)
    gc = _col(aux_v, B_HEADS + hd)
    gl = _col(aux_last, B_HEADS + hd)
    dec = jnp.exp(jnp.where(incl, gc - auxt_ref[hd:hd + 1, :], NEG))
    return beta, gc, gl, jnp.exp(gc), dec


def _gdn_intra(q, k, v, aux, auxt, tinv, *, name):
    tt = q.shape[0]
    nc = tt // CH

    def body(q_ref, k_ref, v_ref, aux_ref, auxt_ref, t_ref, u0_ref, w_ref, qd_ref, kd_ref, qk_ref, gle_ref):
        aux_v = aux_ref[...]
        aux_last = aux_ref[CH - 1:CH, :]
        lane8 = lax.broadcasted_iota(jnp.int32, (8, LANE), 1)
        gle = jnp.zeros((8, LANE), F32)
        heads = range(B_HEADS)
        sls = [slice(hd * B_DH, (hd + 1) * B_DH) for hd in heads]
        gates = [_gdn_gates(aux_v, aux_last, auxt_ref, hd) for hd in heads]
        qk0 = [_dot_nt(q_ref[:, sls[hd]].astype(BF16), k_ref[:, sls[hd]].astype(BF16)) for hd in heads]
        u0 = [_dot3(t_ref[hd], v_ref[:, sls[hd]] * gates[hd][0]) for hd in heads]
        wk = [_dot3(t_ref[hd], k_ref[:, sls[hd]] * (gates[hd][0] * gates[hd][3])) for hd in heads]
        for hd in heads:
            sl = sls[hd]
            beta, gc, gl, egc, dec = gates[hd]
            qk_ref[hd] = (qk0[hd] * dec).astype(BF16)
            u0_ref[:, sl] = u0[hd]
            w_ref[:, sl] = wk[hd].astype(BF16)
            qd_ref[:, sl] = (q_ref[:, sl] * egc).astype(BF16)
            kd_ref[:, sl] = (k_ref[:, sl] * jnp.exp(gl - gc)).astype(BF16)
            gle = gle + jnp.where(lane8 == hd, jnp.exp(gl), 0.0)
        gle_ref[...] = gle

    row = _rb(NCH * CH, B_W)
    half = jax.ShapeDtypeStruct((tt, B_W), BF16)
    return pl.pallas_call(
        _per_chunk(body, (CH, CH, CH, CH, 8, B_HEADS, CH, CH, CH, CH, B_HEADS, 8)), grid=(nc // NCH,),
        in_specs=[row, row, row, _rb(NCH * CH, LANE), _CHUNK8(CH, NCH), _CHUNK4(CH, CH, NCH)],
        out_specs=(row, row, row, row, _CHUNK4(CH, CH, NCH), _CHUNK8(LANE, NCH)),
        out_shape=(jax.ShapeDtypeStruct((tt, B_W), F32), half, half, half,
                   jax.ShapeDtypeStruct((nc * B_HEADS, CH, CH), BF16), jax.ShapeDtypeStruct((nc * 8, LANE), F32)),
        compiler_params=_cp(("parallel",)), name=name)(q, k, v, aux, auxt, tinv)


def _gdn_scan_fwd(u0, w, qd, kd, qk, gle, *, name):
    tt = u0.shape[0]
    nc = tt // CH

    def body(u0_ref, w_ref, qd_ref, kd_ref, qk_ref, gle_ref, o_ref, ss_ref, u_ref, s_ref):
        @pl.when(pl.program_id(0) == 0)
        def _():
            s_ref[...] = jnp.zeros_like(s_ref)

        gle = gle_ref[0:1, :]
        heads = range(B_HEADS)
        sls = [slice(hd * B_DH, (hd + 1) * B_DH) for hd in heads]
        st = [s_ref[hd] for hd in heads]
        sb = [t.astype(BF16) for t in st]
        ws = [_dot(w_ref[:, sls[hd]], sb[hd]) for hd in heads]
        qs = [_dot(qd_ref[:, sls[hd]], sb[hd]) for hd in heads]
        ub = [(u0_ref[:, sls[hd]] - ws[hd]).astype(BF16) for hd in heads]
        ku = [_dot_tn(kd_ref[:, sls[hd]], ub[hd]) for hd in heads]
        qu = [_dot(qk_ref[hd], ub[hd]) for hd in heads]
        for hd in heads:
            ss_ref[hd] = st[hd]
            u_ref[:, sls[hd]] = ub[hd]
            o_ref[:, sls[hd]] = qs[hd] + qu[hd]
            s_ref[hd] = st[hd] * _col(gle, hd) + ku[hd]

    row = _rb(CH, B_W)
    return pl.pallas_call(
        body, grid=(nc,), in_specs=[row, row, row, row, _CHUNK4(CH, CH), _CHUNK8(LANE)],
        out_specs=(row, _CHUNK4(B_DH, B_DH), row),
        out_shape=(jax.ShapeDtypeStruct((tt, B_W), F32), jax.ShapeDtypeStruct((nc * B_HEADS, B_DH, B_DH), F32),
                   jax.ShapeDtypeStruct((tt, B_W), BF16)),
        scratch_shapes=[pltpu.VMEM((B_HEADS, B_DH, B_DH), F32)],
        compiler_params=_cp(("arbitrary",)), name=name)(u0, w, qd, kd, qk, gle)


def _gdn_scan_bwd(w, qd, kd, qk, gle, do, *, name):
    tt = w.shape[0]
    nc = tt // CH

    def body(w_ref, qd_ref, kd_ref, qk_ref, gle_ref, do_ref, du_ref, dss_ref, ds_ref):
        @pl.when(pl.program_id(0) == 0)
        def _():
            ds_ref[...] = jnp.zeros_like(ds_ref)

        gle = gle_ref[0:1, :]
        heads = range(B_HEADS)
        sls = [slice(hd * B_DH, (hd + 1) * B_DH) for hd in heads]
        dst = [ds_ref[hd] for hd in heads]
        dob = [do_ref[:, sls[hd]].astype(BF16) for hd in heads]
        kds = [_dot(kd_ref[:, sls[hd]], dst[hd].astype(BF16)) for hd in heads]
        qkd = [_dot_tn(qk_ref[hd], dob[hd]) for hd in heads]
        qdd = [_dot_tn(qd_ref[:, sls[hd]], dob[hd]) for hd in heads]
        du = [qkd[hd] + kds[hd] for hd in heads]
        wdu = [_dot_tn(w_ref[:, sls[hd]], du[hd].astype(BF16)) for hd in heads]
        for hd in heads:
            dss_ref[hd] = dst[hd]
            du_ref[:, sls[hd]] = du[hd]
            ds_ref[hd] = qdd[hd] + _col(gle, hd) * dst[hd] - wdu[hd]

    rev = lambda width: pl.BlockSpec((CH, width), lambda i: (nc - 1 - i, 0))
    rev4 = lambda a, b: pl.BlockSpec((B_HEADS, a, b), lambda i: (nc - 1 - i, 0, 0))
    return pl.pallas_call(
        body, grid=(nc,),
        in_specs=[rev(B_W), rev(B_W), rev(B_W), rev4(CH, CH), pl.BlockSpec((8, LANE), lambda i: (nc - 1 - i, 0)), rev(B_W)],
        out_specs=(rev(B_W), rev4(B_DH, B_DH)),
        out_shape=(jax.ShapeDtypeStruct((tt, B_W), F32), jax.ShapeDtypeStruct((nc * B_HEADS, B_DH, B_DH), F32)),
        scratch_shapes=[pltpu.VMEM((B_HEADS, B_DH, B_DH), F32)],
        compiler_params=_cp(("arbitrary",)), name=name)(w, qd, kd, qk, gle, do)


def _gdn_bwd(q, k, v, aux, auxt, tinv, u0, w, u, ss, dss, du, do, *, name):
    tt = q.shape[0]
    nc = tt // CH

    def body(q_ref, k_ref, v_ref, aux_ref, auxt_ref, t_ref, u0_ref, w_ref, u_ref, ss_ref, dss_ref, du_ref, do_ref,
             dq_ref, dk_ref, dv_ref, daux_ref):
        aux_v = aux_ref[...]
        aux_last = aux_ref[CH - 1:CH, :]
        lane = lax.broadcasted_iota(jnp.int32, (CH, LANE), 1)
        rowi = lax.broadcasted_iota(jnp.int32, (CH, 1), 0)
        strict, incl = _chunk_masks()
        daux = jnp.zeros((CH, LANE), F32)
        heads = range(B_HEADS)
        sls = [slice(hd * B_DH, (hd + 1) * B_DH) for hd in heads]
        gates = [_gdn_gates(aux_v, aux_last, auxt_ref, hd) for hd in heads]
        kbs = [k_ref[:, sl].astype(BF16) for sl in sls]
        qbs = [q_ref[:, sl].astype(BF16) for sl in sls]
        sbs = [ss_ref[hd].astype(BF16) for hd in heads]
        dsbs = [dss_ref[hd].astype(BF16) for hd in heads]
        dobs = [do_ref[:, sl].astype(BF16) for sl in sls]
        kks = [_dot_nt(kbs[hd], kbs[hd]) for hd in heads]
        qk0s = [_dot_nt(qbs[hd], kbs[hd]) for hd in heads]
        dq_decs = [_dot_nt(dobs[hd], sbs[hd]) for hd in heads]
        dqks = [_dot_nt(dobs[hd], u_ref[:, sls[hd]]) for hd in heads]
        dk_decs = [_dot_nt(u_ref[:, sls[hd]], dsbs[hd]) for hd in heads]
        dws = [-_dot_nt(du_ref[:, sls[hd]].astype(BF16), sbs[hd]) for hd in heads]
        drvs = [_dot3(t_ref[hd], du_ref[:, sls[hd]], tn=True) for hd in heads]
        drks = [_dot3(t_ref[hd], dws[hd], tn=True) for hd in heads]
        dls = [-(_dot_nt(drvs[hd].astype(BF16), u0_ref[:, sls[hd]].astype(BF16))
                 + _dot_nt(drks[hd].astype(BF16), w_ref[:, sls[hd]])) for hd in heads]
        for hd in heads:
            sl = sls[hd]
            qh, kh, vh = q_ref[:, sl], k_ref[:, sl], v_ref[:, sl]
            beta, gc, gl, egc, dec = gates[hd]
            ekd, eg_last = jnp.exp(gl - gc), jnp.exp(gl)
            kb, qb, kk, qk0 = kbs[hd], qbs[hd], kks[hd], qk0s[hd]
            st, dst = ss_ref[hd], dss_ref[hd]
            dq_dec, dk_dec = dq_decs[hd], dk_decs[hd]
            dqk = jnp.where(incl, dqks[hd], 0.0)
            dgl = jnp.sum(jnp.sum(st * dst, axis=-1, keepdims=True), axis=0, keepdims=True) * eg_last
            drv, drk = drvs[hd], drks[hd]
            dl = jnp.where(strict, dls[hd], 0.0)
            dv_ref[:, sl] = drv * beta
            rk = jnp.sum(drk * kh, axis=-1, keepdims=True)
            dbeta = jnp.sum(drv * vh, axis=-1, keepdims=True) + rk * egc
            dgc = rk * beta * egc
            dk = drk * (beta * egc)
            ldec = dl * dec
            dbeta = dbeta + jnp.sum(ldec * kk, axis=-1, keepdims=True)
            dkk = (ldec * beta).astype(BF16)
            dqk0 = (dqk * dec).astype(BF16)
            ddec = ldec * beta * kk + dqk * (qk0 * dec)
            dq = _dot(dqk0, kb) + dq_dec * egc
            dk = dk + _dot_tn(dqk0, qb) + _dot(dkk, kb) + _dot_tn(dkk, kb) + dk_dec * ekd
            dgc = dgc + jnp.sum(ddec, axis=-1, keepdims=True) - _col_from_rowsum(ddec)
            dgc = dgc + jnp.sum(dq_dec * qh, axis=-1, keepdims=True) * egc
            kd = jnp.sum(dk_dec * kh, axis=-1, keepdims=True) * ekd
            dgc = dgc - kd
            dgc = dgc + jnp.where(rowi == CH - 1, jnp.sum(kd, axis=0, keepdims=True) + dgl, 0.0)
            dq_ref[:, sl] = dq
            dk_ref[:, sl] = dk
            daux = daux + jnp.where(lane == hd, dbeta, 0.0) + jnp.where(lane == B_HEADS + hd, dgc, 0.0)
        daux_ref[...] = daux

    row = _rb(NCH * CH, B_W)
    outs = jax.ShapeDtypeStruct((tt, B_W), F32)
    return pl.pallas_call(
        _per_chunk(body, (CH, CH, CH, CH, 8, B_HEADS, CH, CH, CH, B_HEADS, B_HEADS, CH, CH, CH, CH, CH, CH)),
        grid=(nc // NCH,),
        in_specs=[row, row, row, _rb(NCH * CH, LANE), _CHUNK8(CH, NCH), _CHUNK4(CH, CH, NCH), row, row, row,
                  _CHUNK4(B_DH, B_DH, NCH), _CHUNK4(B_DH, B_DH, NCH), row, row],
        out_specs=(row, row, row, _rb(NCH * CH, LANE)),
        out_shape=(outs, outs, outs, jax.ShapeDtypeStruct((tt, LANE), F32)),
        compiler_params=_cp(("parallel",)), name=name)(q, k, v, aux, auxt, tinv, u0, w, u, ss, dss, du, do)


def _col_from_rowsum(m):
    hi, lo = _split(m)
    ones = jnp.ones((CH, LANE), BF16)
    return (_dot_tn(hi, ones) + _dot_tn(lo, ones))[:, 0:1]


def _gdn_post_fwd(o, proj, gn, *, name, tr=256):
    tt = o.shape[0]

    def body(o_ref, z_ref, g_ref, y_ref):
        for hd in range(B_HEADS):
            sl = slice(hd * B_DH, (hd + 1) * B_DH)
            oh = o_ref[:, sl]
            r = lax.rsqrt(jnp.mean(oh * oh, axis=-1, keepdims=True) + EPS)
            y_ref[:, sl] = (oh * r * g_ref[...] * _silu(z_ref[:, sl])).astype(BF16)

    return pl.pallas_call(body, grid=(tt // tr,), in_specs=[_rb(tr, B_W), _rb(tr, B_W, OFF_ZB // B_W), _whole((1, B_DH))],
                          out_specs=_rb(tr, B_W), out_shape=jax.ShapeDtypeStruct((tt, B_W), BF16),
                          compiler_params=_cp(("parallel",)), name=name)(o, proj, gn)


def _gdn_post_bwd(o, proj, gn, dy, *, name, tr=256):
    tt = o.shape[0]

    def body(o_ref, z_ref, g_ref, dy_ref, do_ref, dz_ref, dg_ref):
        @pl.when(pl.program_id(0) == 0)
        def _():
            dg_ref[...] = jnp.zeros_like(dg_ref)

        g = g_ref[...]
        for hd in range(B_HEADS):
            sl = slice(hd * B_DH, (hd + 1) * B_DH)
            oh, zh, dyh = o_ref[:, sl], z_ref[:, sl], dy_ref[:, sl]
            r = lax.rsqrt(jnp.mean(oh * oh, axis=-1, keepdims=True) + EPS)
            a = oh * r
            s = _silu(zh)
            da = dyh * g * s
            dg_ref[0:1, :] += _colsum(dyh * a * s)
            dz_ref[:, sl] = (dyh * a * g * _dsilu(zh)).astype(BF16)
            do_ref[:, sl] = r * (da - a * jnp.mean(da * a, axis=-1, keepdims=True))

    return pl.pallas_call(
        body, grid=(tt // tr,), in_specs=[_rb(tr, B_W), _rb(tr, B_W, OFF_ZB // B_W), _whole((1, B_DH)), _rb(tr, B_W)],
        out_specs=(_rb(tr, B_W), _rb(tr, B_W), _whole((8, B_DH))),
        out_shape=(jax.ShapeDtypeStruct((tt, B_W), F32), jax.ShapeDtypeStruct((tt, B_W), BF16),
                   jax.ShapeDtypeStruct((8, B_DH), F32)),
        compiler_params=_cp(("arbitrary",)), name=name)(o, proj, gn, dy)


def _adamw(parts, w, m, v, own=None, sel=None, *, name, tr=256):
    npart, nl, r, c = parts.shape
    tr = max([t for t in range(8, min(r, tr) + 1, 8) if r % t == 0], default=r)
    tc = c if tr < r or r <= 256 or c % 256 else 256
    c1, c2 = 1.0 - ADAM_B1 ** ADAM_STEP, 1.0 - ADAM_B2 ** ADAM_STEP

    def body(*refs):
        if own is None:
            p_ref, w_ref, m_ref, v_ref, g_ref, d_ref, mo_ref, vo_ref = refs
            part = lambda i: p_ref[i].astype(F32)
        else:
            p_ref, w_ref, m_ref, v_ref, own_ref, sel_ref, g_ref, d_ref, mo_ref, vo_ref = refs
            part = lambda i: jnp.where(sel_ref[i:i + 1, 0:1] > 0.5, own_ref[...].astype(F32), p_ref[i].astype(F32))
        g = part(0)
        for i in range(1, npart):
            g = g + part(i)
        mn = ADAM_B1 * m_ref[...] + (1.0 - ADAM_B1) * g
        vn = ADAM_B2 * v_ref[...] + (1.0 - ADAM_B2) * (g * g)
        g_ref[...] = g
        mo_ref[...] = mn
        vo_ref[...] = vn
        d_ref[...] = -ADAM_LR * ((mn / c1) / (jnp.sqrt(vn / c2) + ADAM_EPS) + ADAM_WD * w_ref[...])

    row = pl.BlockSpec((None, tr, tc), lambda l, i, j: (l, i, j))
    out = jax.ShapeDtypeStruct((nl, r, c), F32)
    ins, in_specs = [parts, w, m, v], [pl.BlockSpec((npart, None, tr, tc), lambda l, i, j: (0, l, i, j)), row, row, row]
    if own is not None:
        ins += [own, sel]
        in_specs += [row, pl.BlockSpec((N_DEV, LANE), lambda l, i, j: (0, 0))]
    return pl.pallas_call(body, grid=(nl, r // tr, c // tc), in_specs=in_specs, out_specs=(row, row, row, row),
                          out_shape=(out, out, out, out), compiler_params=_cp(("parallel", "parallel", "parallel")),
                          name=name)(*ins)


def _peer(k):
    x, y, c = lax.axis_index("x"), lax.axis_index("y"), lax.axis_index("c")
    return ((1 - x) if k & 4 else x, (1 - y) if k & 2 else y, (1 - c) if k & 1 else c)


def _my_index():
    return 4 * lax.axis_index("x") + 2 * lax.axis_index("y") + lax.axis_index("c")


def _index_of(p):
    return 4 * p[0] + 2 * p[1] + p[2]


def _all_gather(xs, *, name):
    n = len(xs)

    def body(*refs):
        x_refs, o_refs = refs[:n], refs[n:2 * n]
        send, recv, loc = refs[2 * n:]
        me = _my_index()
        copies = []
        for a in range(n):
            cp = pltpu.make_async_copy(x_refs[a], o_refs[a].at[me], loc.at[a])
            cp.start()
            copies.append(cp)
        rdmas = []
        for a in range(n):
            for k in range(1, N_DEV):
                r = pltpu.make_async_remote_copy(
                    src_ref=x_refs[a], dst_ref=o_refs[a].at[me], send_sem=send.at[a, k - 1], recv_sem=recv.at[a, k - 1],
                    device_id=_peer(k), device_id_type=pl.DeviceIdType.MESH)
                r.start()
                rdmas.append(r)
        for a in range(n):
            for k in range(1, N_DEV):
                pltpu.make_async_remote_copy(
                    src_ref=x_refs[a], dst_ref=o_refs[a].at[_index_of(_peer(k))], send_sem=send.at[a, k - 1],
                    recv_sem=recv.at[a, k - 1], device_id=_peer(k), device_id_type=pl.DeviceIdType.MESH).wait_recv()
        for r in rdmas:
            r.wait_send()
        for cp in copies:
            cp.wait()

    any_spec = pl.BlockSpec(memory_space=pl.ANY)
    return pl.pallas_call(
        body, in_specs=[any_spec] * n, out_specs=tuple([any_spec] * n),
        out_shape=tuple(jax.ShapeDtypeStruct((N_DEV,) + x.shape, x.dtype) for x in xs),
        scratch_shapes=[pltpu.SemaphoreType.DMA((n, N_DEV - 1)), pltpu.SemaphoreType.DMA((n, N_DEV - 1)),
                        pltpu.SemaphoreType.DMA((n,))],
        name=name)(*xs)


_HBM = pl.BlockSpec(memory_space=pltpu.HBM)
_SEM = pl.BlockSpec(memory_space=pltpu.SEMAPHORE)
_EFFECT = pltpu.SideEffectType.DATAFLOW_SIDE_EFFECTING


def _split_copy(src_ref, land_ref, send, recv, a, k, scatter, slot, sending):
    me, peer = _my_index(), _index_of(_peer(k))
    src = src_ref.at[peer if sending else me] if scatter else src_ref
    land = land_ref.at[me if sending else peer]
    if slot is not None:
        land = land.at[slot]
    sem = a * (N_DEV - 1) + k - 1
    return pltpu.make_async_remote_copy(src_ref=src, dst_ref=land, send_sem=send.at[sem], recv_sem=recv.at[sem],
                                        device_id=_peer(k), device_id_type=pl.DeviceIdType.MESH)


def _exchange_start(srcs, lands, after, *, scatter, slot=None, name):
    n = len(srcs)

    def body(*refs):
        src_refs, land_refs = refs[:n], refs[n:2 * n]
        send, recv, token = refs[2 * n + 1], refs[2 * n + 2], refs[-1]
        for a in range(n):
            for k in range(1, N_DEV):
                _split_copy(src_refs[a], land_refs[a], send, recv, a, k, scatter, slot, True).start()
        token[...] = jnp.zeros_like(token)

    hbm = lambda t: pltpu.HBM(t.shape, t.dtype)
    sems = pltpu.SemaphoreType.DMA((n * (N_DEV - 1),))
    out = pl.pallas_call(
        body, name=name,
        out_shape=(sems, sems, *[hbm(t) for t in srcs], *[hbm(t) for t in lands], jax.ShapeDtypeStruct((8, LANE), F32)),
        in_specs=[_HBM] * (2 * n) + [pl.BlockSpec(memory_space=pl.ANY)],
        out_specs=(_SEM, _SEM, *[_HBM] * (2 * n), pl.BlockSpec(memory_space=pltpu.VMEM)),
        input_output_aliases={i: 2 + i for i in range(2 * n)},
        compiler_params=pltpu.CompilerParams(has_side_effects=_EFFECT),
    )(*[pltpu.with_memory_space_constraint(t, pltpu.HBM) for t in (*srcs, *lands)], after)
    return out[0], out[1], out[2:2 + n], out[2 + n:2 + 2 * n], out[-1]


def _exchange_wait(send, recv, srcs, lands, after, *, scatter, slot=None, name):
    n = len(srcs)

    def body(*refs):
        src_refs, land_refs = refs[:n], refs[n:2 * n]
        send_ref, recv_ref = refs[2 * n], refs[2 * n + 1]
        for a in range(n):
            for k in range(1, N_DEV):
                _split_copy(src_refs[a], land_refs[a], send_ref, recv_ref, a, k, scatter, slot, True).wait_send()
                _split_copy(src_refs[a], land_refs[a], send_ref, recv_ref, a, k, scatter, slot, False).wait_recv()

    hbm = lambda t: pltpu.HBM(t.shape, t.dtype)
    out = pl.pallas_call(
        body, name=name, out_shape=(*[hbm(t) for t in srcs], *[hbm(t) for t in lands]),
        in_specs=[_HBM] * (2 * n) + [_SEM, _SEM, pl.BlockSpec(memory_space=pl.ANY)],
        out_specs=tuple([_HBM] * (2 * n)), input_output_aliases={i: i for i in range(2 * n)},
        compiler_params=pltpu.CompilerParams(has_side_effects=_EFFECT),
    )(*srcs, *lands, send, recv, after)
    return out[:n], out[n:]


def _win_to_mine(wt):
    pad = jnp.zeros((IN_PAD - IN_DIM,) + wt.shape[1:], wt.dtype)
    return jnp.concatenate([wt[3592:5640], wt[0:3584], wt[3584:3592], pad], axis=0)


def _win_from_mine(gt):
    return jnp.concatenate([gt[2048:5632], gt[5632:5640], gt[0:2048]], axis=0)


def _pad_rows(a, mult=8):
    r = (-a.shape[0]) % mult
    return a if r == 0 else jnp.concatenate([a, jnp.zeros((r,) + a.shape[1:], a.dtype)], axis=0)


def _lanes(vec, start):
    return jnp.zeros((1, LANE), F32).at[0, start:start + vec.shape[0]].set(vec)


def _small_spec(depth):
    return (("b_ada", (depth, 6 * D)), ("norm1_g", (depth, D)), ("norm2_g", (depth, D)),
            ("rel_table", (depth, A_HEADS, 2 * A_MAX_REL + 1)), ("a_log", (depth, B_HEADS)),
            ("dt_bias", (depth, B_HEADS)), ("gdn_norm_g", (depth, B_DH)), ("final_g", (D,)))


def _pack_small(d, extra, depth):
    spec = _small_spec(depth)
    rows = -(-(sum(math.prod(s) for _, s in spec) + 1) // (8 * LANE)) * 8
    flat = jnp.concatenate([d[n].reshape(-1).astype(F32) for n, _ in spec] + [extra.reshape(-1)])
    flat = jnp.concatenate([flat, jnp.zeros((rows * LANE - flat.shape[0],), F32)])
    return flat.reshape(rows, LANE)


def _unpack_small(p, depth):
    flat = p.reshape(-1)
    out, off = {}, 0
    for n, s in _small_spec(depth):
        sz = math.prod(s)
        out[n] = flat[off:off + sz].reshape(s)
        off += sz
    return out, flat[off]


def kernel(x, c, w_ada, b_ada, norm1_g, norm2_g, w_in, rel_table, w_conv, a_log, dt_bias, gdn_norm_g, w_branch_a, w_branch_b, w_out, w_ffn_in, w_ffn_out, final_g, loss_target, m_w_ada, m_b_ada, m_norm1_g, m_norm2_g, m_w_in, m_rel_table, m_w_conv, m_a_log, m_dt_bias, m_gdn_norm_g, m_w_branch_a, m_w_branch_b, m_w_out, m_w_ffn_in, m_w_ffn_out, m_final_g, v_w_ada, v_b_ada, v_norm1_g, v_norm2_g, v_w_in, v_rel_table, v_w_conv, v_a_log, v_dt_bias, v_gdn_norm_g, v_w_branch_a, v_w_branch_b, v_w_out, v_w_ffn_in, v_w_ffn_out, v_final_g):
    tt = x.shape[1]
    x0 = x[0]
    tgt = loss_target[0]
    me = _my_index()
    depth = w_in.shape[0]

    tr_ = lambda t: jnp.transpose(t, (0, 2, 1))
    shards = [tr_(w_in).astype(BF16), w_branch_a.astype(BF16), w_branch_b.astype(BF16), w_out.astype(BF16),
              tr_(w_ffn_in).astype(BF16), w_ffn_out.astype(BF16), w_conv]
    names = ("win", "wa", "wb", "wout", "wfi", "wfo", "wconv")
    early, late, every = (0, 6), (1, 2, 3, 4, 5), tuple(range(7))
    first = _all_gather([shards[i][0] for i in early] + [_pad_rows(c)], name="gather_first")
    c_all = first[-1][:, 0, :]
    is_me = lax.broadcasted_iota(jnp.int32, (N_DEV, 1, 1), 0) == me

    def unpack(idx, g):
        cols = lambda t: jnp.transpose(t, (1, 0, 2)).reshape(t.shape[1], N_DEV * t.shape[2])
        rows = lambda t: t.reshape(N_DEV * t.shape[1], t.shape[2])
        how = (lambda t: _win_to_mine(rows(t)), cols, cols, rows, rows, rows, cols)
        return {names[i]: how[i](t) for i, t in zip(idx, g)}

    def gather_start(l, idx, after, tag=""):
        srcs = [shards[i][l] for i in idx]
        lands = [lax.empty((N_DEV,) + t.shape, t.dtype) for t in srcs]
        return _exchange_start(srcs, lands, after, scatter=False, name=f"gather_start_{l}{tag}")

    def gather_wait(l, idx, pending, after, tag=""):
        send, recv, srcs, lands, _ = pending
        srcs, lands = _exchange_wait(send, recv, srcs, lands, after, scatter=False, name=f"gather_wait_{l}{tag}")
        return unpack(idx, [jnp.where(is_me, t[None], g) for g, t in zip(lands, srcs)])

    weights = [unpack(early, first[:-1])] + [None] * (depth - 1)
    pending0 = gather_start(0, late, first[-1], "_rest")
    pending = gather_start(1, every, pending0[-1]) if depth > 1 else None
    cond = c_all * (1.0 / (1.0 + jnp.exp(-c_all)))
    cond = _pad_rows(cond, 16)

    mod_cols = jnp.stack([_mm(cond, w_ada[l], name="mod_mm")[:N_DEV] for l in range(depth)])
    (g_mod,) = _all_gather([mod_cols], name="gather_mod")
    mod_all = jnp.transpose(g_mod, (1, 2, 0, 3)).reshape(depth, N_DEV, 6 * D)
    mod = lax.dynamic_index_in_dim(mod_all, me, axis=1, keepdims=False) + b_ada
    mods = mod.reshape(depth, 6, 1, D)

    n1g, n2g = norm1_g.reshape(depth, 1, D), norm2_g.reshape(depth, 1, D)
    gng = gdn_norm_g.reshape(depth, 1, B_DH)
    fg = final_g.reshape(1, D)

    saved = []
    tok = (pending if pending is not None else pending0)[-1][0, 0]
    xin, h1 = _adaln_fwd(x0, n1g[0], mods[0, 1] + tok, mods[0, 0], name="adaln1_first")
    for l in range(depth):
        sh1, sc1, gt1, sh2, sc2, gt2 = (mods[l, i] for i in range(6))
        wl = weights[l]
        proj = _mm(h1, wl["win"], tb=True, name="proj_mm", tn=1152)
        kpad, vpad = _kv_pad(proj, name="kv_pad")
        diag, bias_vjp = jax.vjp(_bias_diagonals, rel_table[l])
        bias = _bias_fwd(diag, name="bias_fwd")
        ya, lse = _attn_fwd(proj, kpad, vpad, bias, name="attn_fwd")
        par = jnp.concatenate([_lanes(a_log[l], B_HEADS), _lanes(dt_bias[l], B_HEADS), jnp.zeros((6, LANE), F32)], axis=0)
        qn, kn, vn, aux = _gdn_pre_fwd(proj, wl["wconv"], par, name="gdn_pre_fwd")
        auxt = _gc_rows(aux, tt // CH)
        lower = _gdn_lower(kn, aux, auxt, name="gdn_lower")
        tinv = jnp.transpose(_tri_inverse(jnp.transpose(lower, (1, 2, 0)), name="gdn_tri_inverse"), (2, 0, 1))
        u0, wg, qd, kd, qk, gle = _gdn_intra(qn, kn, vn, aux, auxt, tinv, name="gdn_intra")
        og, ss, ug = _gdn_scan_fwd(u0, wg, qd, kd, qk, gle, name="gdn_scan_fwd")
        yb = _gdn_post_fwd(og, proj, gng[l], name="gdn_post_fwd")
        if l == 0:
            wl.update(gather_wait(0, late, pending0, yb, "_rest"))
        pa, pb, merged = _branch_merge(ya, yb, wl["wa"], wl["wb"], proj, name="branch_merge")
        t1, x2, h2 = _out_adaln(merged, wl["wout"], xin, gt1, n2g[l], sc2, sh2, name="out_adaln2")
        gu, act = _ffn_in_swiglu(h2, wl["wfi"], name="ffn_in_swiglu")
        saved.append(dict(xin=xin, h1=h1, proj=proj, kpad=kpad, vpad=vpad, bias=bias, bias_vjp=bias_vjp, ya=ya, lse=lse,
                          par=par, qn=qn, kn=kn, vn=vn, aux=aux, auxt=auxt, tinv=tinv, ss=ss, og=og, yb=yb, pa=pa, pb=pb,
                          u0=u0, wg=wg, qd=qd, kd=kd, qk=qk, gle=gle, ug=ug,
                          merged=merged, t1=t1, x2=x2, h2=h2, gu=gu, act=act))
        if l + 1 < depth:
            weights[l + 1] = gather_wait(l + 1, every, pending, act)
            pending = gather_start(l + 2, every, weights[l + 1]["wconv"]) if l + 2 < depth else None
            tok = pending[-1][0, 0] if pending is not None else 0.0
            t2, xin, h1 = _out_adaln(act, wl["wfo"], x2, gt2, n1g[l + 1], mods[l + 1, 1] + tok, mods[l + 1, 0],
                                     tk=FTN, name="ffn_out_adaln1")
        else:
            t2 = _mm(act, wl["wfo"], name="ffn_out_mm", tk=FTN)
        saved[-1]["t2"] = t2

    s = saved[-1]
    dx, dt2, st = _loss_head(s["x2"], s["t2"], mods[depth - 1, 5], fg, tgt, name="loss_head")
    loss_part = st[4, 0]
    small_g = {"final_g": st[0]}
    dmod_rows = [None] * depth
    for n in ("norm1_g", "norm2_g", "rel_table", "a_log", "dt_bias", "gdn_norm_g"):
        small_g[n] = [None] * depth
    dgt2 = st[3]
    cols_slabs = lambda g: jnp.transpose(g.reshape(g.shape[0], N_DEV, g.shape[1] // N_DEV), (1, 0, 2))
    rows_slabs = lambda g: g.reshape(N_DEV, g.shape[0] // N_DEV, g.shape[1])
    mix, ffn = (0, 1, 2, 3, 6), (4, 5)
    lands = {kind: [lax.empty((N_DEV,) + shards[i].shape, shards[i].dtype) for i in idx]
             for kind, idx in (("mix", mix), ("ffn", ffn))}
    own = {kind: [None] * depth for kind in lands}
    pending_s = {kind: None for kind in lands}

    def scatter(kind, l, srcs, after):
        if pending_s[kind] is not None:
            done, lands[kind] = _exchange_wait(*pending_s[kind][:4], after, scatter=True, slot=l + 1,
                                               name=f"scatter_wait_{kind}_{l + 1}")
            own[kind][l + 1] = [lax.dynamic_index_in_dim(t, me, 0, keepdims=False) for t in done]
        pending_s[kind] = _exchange_start(srcs, lands[kind], after, scatter=True, slot=l, name=f"scatter_start_{kind}_{l}")
        return pending_s[kind][-1][0, 0]

    for l in reversed(range(depth)):
        s, wl = saved[l], weights[l]
        sh1, sc1, gt1, sh2, sc2, gt2 = (mods[l, i] for i in range(6))
        gw_fo = _mm(s["act"], dt2, ta=True, out_dtype=BF16, name="ffn_out_dw", tm=1408)
        dgu = _ffn_out_bwd_swiglu(dt2, wl["wfo"], s["gu"], name="ffn_out_bwd_swiglu")
        gw_fi = _mm(dgu, s["h2"], ta=True, out_dtype=BF16, name="ffn_in_dw", tm=1408)
        sc2 = sc2 + scatter("ffn", l, [rows_slabs(gw_fi), rows_slabs(gw_fo)], gw_fi)
        dx, dt1, st2 = _mm_adaln_bwd(dgu, wl["wfi"], s["x2"], n2g[l], sc2, sh2, dx, s["t1"], gt1, tk=FTN,
                                     name="ffn_in_dx_adaln2")
        gw_out = _mm(s["merged"], dt1, ta=True, out_dtype=BF16, name="out_dw")
        dgates, dpa, dpb = _out_bwd_merge(dt1, wl["wout"], s["proj"], s["pa"], s["pb"], name="out_bwd_merge")
        gw_a = _mm(s["ya"], dpa, ta=True, out_dtype=BF16, name="branch_a_dw")
        gw_b = _mm(s["yb"], dpb, ta=True, out_dtype=BF16, name="branch_b_dw")
        dya = _mm(dpa, wl["wa"], tb=True, name="branch_a_dx")
        dyb = _mm(dpb, wl["wb"], tb=True, name="branch_b_dx")
        dqa, dka, dva, dbias = _attn_bwd(s["proj"], s["kpad"], s["vpad"], s["bias"], s["ya"], s["lse"], dya,
                                             name="attn_bwd")
        ddiag = jnp.roll(_bias_bwd(dbias, name="bias_bwd"), -(QB - 1), axis=1)
        small_g["rel_table"][l] = s["bias_vjp"](ddiag)[0]
        dog, dz, dgn = _gdn_post_bwd(s["og"], s["proj"], gng[l], dyb, name="gdn_post_bwd")
        small_g["gdn_norm_g"][l] = dgn[0]
        dug, dss = _gdn_scan_bwd(s["wg"], s["qd"], s["kd"], s["qk"], s["gle"], dog, name="gdn_scan_bwd")
        dqn, dkn, dvn, daux = _gdn_bwd(s["qn"], s["kn"], s["vn"], s["aux"], s["auxt"], s["tinv"], s["u0"], s["wg"],
                                       s["ug"], s["ss"], dss, dug, dog, name="gdn_bwd")
        dqkv, dba, dwc, dpar = _gdn_pre_bwd(s["proj"], wl["wconv"], s["par"], dqn, dkn, dvn, daux, name="gdn_pre_bwd")
        small_g["a_log"][l] = dpar[0, B_HEADS:2 * B_HEADS]
        small_g["dt_bias"][l] = dpar[1, B_HEADS:2 * B_HEADS]
        dproj = jnp.concatenate([dgates, dqa, dka, dva, dqkv, dz, dba], axis=1)
        gw_in = _mm(dproj, s["h1"], ta=True, out_dtype=BF16, name="proj_dw", tm=1152)
        mix_srcs = [rows_slabs(_win_from_mine(gw_in)), cols_slabs(gw_a), cols_slabs(gw_b), rows_slabs(gw_out),
                    cols_slabs(dwc[0:CONV_K])]
        if l > 0:
            sc1 = sc1 + scatter("mix", l, mix_srcs, gw_in)
        if l > 0:
            p = saved[l - 1]
            dx, dt2, st1 = _mm_adaln_bwd(dproj, wl["win"], s["xin"], n1g[l], sc1, sh1, dx, p["t2"], mods[l - 1, 5],
                                         tk=1152, name="proj_dx_adaln1")
        else:
            dx, st1 = _mm_adaln_bwd(dproj, wl["win"], s["xin"], n1g[l], sc1, sh1, dx, tk=1152,
                                    name="proj_dx_adaln1_first")
        small_g["norm1_g"][l], small_g["norm2_g"][l] = st1[0], st2[0]
        dmod_rows[l] = jnp.concatenate([st1[2], st1[1], st2[3], st2[2], st2[1], dgt2])
        if l > 0:
            dgt2 = st1[3]
    grad_x = dx[None]

    small_local = {n: (jnp.stack(vs) if isinstance(vs, list) else vs) for n, vs in small_g.items()}
    small_local["b_ada"] = jnp.stack(dmod_rows)
    (g_small,) = _all_gather([_pack_small(small_local, loss_part, depth)], name="gather_small")
    tok = scatter("mix", 0, mix_srcs, g_small)
    wsm = _pack_small(dict(b_ada=b_ada, norm1_g=norm1_g, norm2_g=norm2_g, rel_table=rel_table, a_log=a_log,
                           dt_bias=dt_bias, gdn_norm_g=gdn_norm_g, final_g=final_g), jnp.zeros((1,), F32) + tok, depth)
    msm = _pack_small(dict(b_ada=m_b_ada, norm1_g=m_norm1_g, norm2_g=m_norm2_g, rel_table=m_rel_table, a_log=m_a_log,
                           dt_bias=m_dt_bias, gdn_norm_g=m_gdn_norm_g, final_g=m_final_g), jnp.zeros((1,), F32), depth)
    vsm = _pack_small(dict(b_ada=v_b_ada, norm1_g=v_norm1_g, norm2_g=v_norm2_g, rel_table=v_rel_table, a_log=v_a_log,
                           dt_bias=v_dt_bias, gdn_norm_g=v_gdn_norm_g, final_g=v_final_g), jnp.ones((1,), F32), depth)
    sm = [_unpack_small(t, depth) for t in _adamw(g_small[:, None], wsm[None], msm[None], vsm[None], name="adamw_small")]
    loss = sm[0][1]

    dmod_all = g_small.reshape(N_DEV, -1)[:, :depth * 6 * D].reshape(N_DEV, depth, 6 * D)
    dmod_mine = lax.dynamic_slice_in_dim(dmod_all, me * (6 * D // N_DEV), 6 * D // N_DEV, axis=2)
    g_ada = jnp.stack([_mm(cond, _pad_rows(dmod_mine[:, l], 16), ta=True, name="ada_dw") for l in range(depth)])

    got, mine = {}, {}
    sel = jnp.broadcast_to(jnp.where(is_me[:, :, 0], 1.0, 0.0), (N_DEV, LANE)).astype(F32)

    def finish(kind, idx, after):
        done, lands[kind] = _exchange_wait(*pending_s[kind][:4], after, scatter=True, slot=0, name=f"scatter_wait_{kind}_0")
        own[kind][0] = [lax.dynamic_index_in_dim(t, me, 0, keepdims=False) for t in done]
        for a, i in enumerate(idx):
            got[i] = lands[kind][a]
            mine[i] = jnp.stack([own[kind][l][a] for l in range(depth)])

    def upd(i, w, m, v, name):
        if i in (0, 4):
            return [tr_(t) for t in _adamw(got[i], tr_(w), tr_(m), tr_(v), mine[i], sel, name=name)]
        return _adamw(got[i], w, m, v, mine[i], sel, name=name)

    finish("ffn", ffn, g_ada)
    res = {
        "w_ada": _adamw(g_ada[None], w_ada, m_w_ada, v_w_ada, name="adamw_w_ada"),
        "w_ffn_in": upd(4, w_ffn_in, m_w_ffn_in, v_w_ffn_in, "adamw_w_ffn_in"),
        "w_ffn_out": upd(5, w_ffn_out, m_w_ffn_out, v_w_ffn_out, "adamw_w_ffn_out"),
    }
    finish("mix", mix, res["w_ffn_out"][0])
    res.update({
        "w_in": upd(0, w_in, m_w_in, v_w_in, "adamw_w_in"),
        "w_conv": upd(6, w_conv, m_w_conv, v_w_conv, "adamw_w_conv"),
        "w_branch_a": upd(1, w_branch_a, m_w_branch_a, v_w_branch_a, "adamw_w_branch_a"),
        "w_branch_b": upd(2, w_branch_b, m_w_branch_b, v_w_branch_b, "adamw_w_branch_b"),
        "w_out": upd(3, w_out, m_w_out, v_w_out, "adamw_w_out"),
    })
    for n, _ in _small_spec(depth):
        res[n] = [sm[i][0][n] for i in range(4)]
    order = ("w_ada", "b_ada", "norm1_g", "norm2_g", "w_in", "rel_table", "w_conv", "a_log", "dt_bias", "gdn_norm_g",
             "w_branch_a", "w_branch_b", "w_out", "w_ffn_in", "w_ffn_out", "final_g")
    return (loss, grad_x, *[res[n][0] for n in order], *[res[n][1] for n in order],
            *[res[n][2] for n in order], *[res[n][3] for n in order])
```

```python
import functools
import math

import jax
import jax.numpy as jnp
from jax import lax
from jax.experimental import pallas as pl
from jax.experimental.pallas import tpu as pltpu

F32 = jnp.float32
BF16 = jnp.bfloat16
HI = lax.Precision.HIGHEST

N_DEV = 8
D = 1024
DEPTH = 4
CH = 64
EPS = 1e-6
A_HEADS, A_DH = 8, 64
A_W = A_HEADS * A_DH
A_PAST = 8
A_MAX_REL = 128
QB = 256
KB = QB + A_PAST * CH
B_HEADS, B_DH = 4, 128
B_W = B_HEADS * B_DH
CONV_K = 4
FF = 2816
IN_DIM = 5640
IN_PAD = 5760
LANE = 128
NEG = -1e30
VMEM_LIMIT = 48 * 1024 * 1024

ADAM_LR, ADAM_B1, ADAM_B2, ADAM_EPS, ADAM_WD, ADAM_STEP = 0.001, 0.9, 0.999, 1e-08, 0.01, 10

OFF_GA, OFF_GB, OFF_QA, OFF_KA, OFF_VA, OFF_QB, OFF_KB, OFF_VB, OFF_ZB, OFF_BA = (
    0, 1024, 2048, 2560, 3072, 3584, 4096, 4608, 5120, 5632)


def _cp(sem=None):
    return pltpu.CompilerParams(dimension_semantics=sem, vmem_limit_bytes=VMEM_LIMIT)


def _tile(n, pref):
    if n <= pref:
        return n
    best = None
    for t in range(LANE, pref + 1, LANE):
        if n % t == 0:
            best = t
    assert best is not None, (n, pref)
    return best


def _sigmoid(x):
    return 1.0 / (1.0 + jnp.exp(-x))


def _silu(x):
    return x * _sigmoid(x)


def _dsilu(x):
    s = _sigmoid(x)
    return s * (1.0 + x * (1.0 - s))


def _dot(a, b, prec=None):
    return jnp.dot(a, b, preferred_element_type=F32, precision=prec)


def _dot_nt(a, b, prec=None):
    return lax.dot_general(a, b, (((1,), (1,)), ((), ())), preferred_element_type=F32, precision=prec)


def _dot_tn(a, b, prec=None):
    return lax.dot_general(a, b, (((0,), (0,)), ((), ())), preferred_element_type=F32, precision=prec)


def _mm(a, b, *, ta=False, tb=False, out_dtype=F32, name, tm=1024, tn=1024, tk=1024):
    halves = a.ndim == 3
    a_rows, a_cols = (a.shape[1], 2 * a.shape[2]) if halves else a.shape
    m, k = (a_cols, a_rows) if ta else (a_rows, a_cols)
    n = b.shape[0] if tb else b.shape[1]
    assert k == (b.shape[1] if tb else b.shape[0]), (a.shape, b.shape, ta, tb)
    tm, tn, tk = _tile(m, tm), _tile(n, tn), _tile(k, tk)
    nk = k // tk
    dn = (((0 if ta else 1,), (1 if tb else 0,)), ((), ()))

    def body(a_ref, b_ref, o_ref, *acc):
        part = lax.dot_general(a_ref[...].astype(BF16), b_ref[...].astype(BF16), dn, preferred_element_type=F32)
        if nk == 1:
            o_ref[...] = part.astype(out_dtype)
            return
        acc_ref, kk = acc[0], pl.program_id(2)

        @pl.when(kk == 0)
        def _():
            acc_ref[...] = part

        @pl.when(kk > 0)
        def _():
            acc_ref[...] += part

        @pl.when(kk == nk - 1)
        def _():
            o_ref[...] = acc_ref[...].astype(out_dtype)

    if halves:
        per = a.shape[2] // (tm if ta else tk)
        a_spec = (pl.BlockSpec((None, tk, tm), lambda i, j, q: (i // per, q, i % per)) if ta else
                  pl.BlockSpec((None, tm, tk), lambda i, j, q: (q // per, i, q % per)))
    else:
        a_spec = pl.BlockSpec((tk, tm), lambda i, j, q: (q, i)) if ta else pl.BlockSpec((tm, tk), lambda i, j, q: (i, q))
    b_spec = pl.BlockSpec((tn, tk), lambda i, j, q: (j, q)) if tb else pl.BlockSpec((tk, tn), lambda i, j, q: (q, j))
    return pl.pallas_call(
        body, grid=(m // tm, n // tn, nk), in_specs=[a_spec, b_spec],
        out_specs=pl.BlockSpec((tm, tn), lambda i, j, q: (i, j)),
        out_shape=jax.ShapeDtypeStruct((m, n), out_dtype),
        scratch_shapes=[pltpu.VMEM((tm, tn), F32)] if nk > 1 else [],
        compiler_params=_cp(("parallel", "parallel", "arbitrary")), name=name)(a, b)


def _rb(tr, width, cb=0):
    return pl.BlockSpec((tr, width), lambda i: (i, cb))


def _whole(shape):
    nd = len(shape)
    return pl.BlockSpec(shape, lambda i: (0,) * nd)


def _colsum(v):
    return jnp.sum(v, axis=0, keepdims=True)


def _adaln_fwd(x, g, sc, sh, t=None, gt=None, *, name, tr=256):
    tt = x.shape[0]
    res = t is not None

    def body(*refs):
        if res:
            x_ref, t_ref, gt_ref, g_ref, sc_ref, sh_ref, xo_ref, h_ref = refs
            xv = x_ref[...] + gt_ref[...] * t_ref[...]
            xo_ref[...] = xv
        else:
            x_ref, g_ref, sc_ref, sh_ref, h_ref = refs
            xv = x_ref[...]
        r = lax.rsqrt(jnp.mean(xv * xv, axis=-1, keepdims=True) + EPS)
        h_ref[...] = ((xv * r * g_ref[...]) * (1.0 + sc_ref[...]) + sh_ref[...]).astype(BF16)

    row, vec = _rb(tr, D), _whole((1, D))
    if res:
        ins, in_specs = (x, t, gt, g, sc, sh), [row, row, vec, vec, vec, vec]
        out_shape = (jax.ShapeDtypeStruct((tt, D), F32), jax.ShapeDtypeStruct((tt, D), BF16))
        out_specs = (row, row)
    else:
        ins, in_specs = (x, g, sc, sh), [row, vec, vec, vec]
        out_shape, out_specs = jax.ShapeDtypeStruct((tt, D), BF16), row
    out = pl.pallas_call(body, grid=(tt // tr,), in_specs=in_specs, out_specs=out_specs, out_shape=out_shape,
                         compiler_params=_cp(("parallel",)), name=name)(*ins)
    return out if res else (x, out)


def _mm_adaln_bwd(a, b, x, g, sc, sh, dx_in, t=None, gt=None, *, name, tk, tm=512):
    tt = x.shape[0]
    res = t is not None
    halves = a.ndim == 3
    k = 2 * a.shape[2] if halves else a.shape[1]
    tm, nk = _tile(tt, tm), k // tk

    def body(*refs):
        if res:
            a_ref, b_ref, x_ref, g_ref, sc_ref, sh_ref, dxi_ref, t_ref, gt_ref, dx_ref, dt_ref, st_ref, acc_ref = refs
        else:
            a_ref, b_ref, x_ref, g_ref, sc_ref, sh_ref, dxi_ref, dx_ref, st_ref, acc_ref = refs
        i, q = pl.program_id(0), pl.program_id(1)
        part = _dot(a_ref[...], b_ref[...])

        @pl.when((i == 0) & (q == 0))
        def _():
            st_ref[...] = jnp.zeros_like(st_ref)

        @pl.when(q == 0)
        def _():
            acc_ref[...] = part

        @pl.when(q > 0)
        def _():
            acc_ref[...] += part

        @pl.when(q == nk - 1)
        def _():
            xv, dh = x_ref[...], acc_ref[...]
            r = lax.rsqrt(jnp.mean(xv * xv, axis=-1, keepdims=True) + EPS)
            nrm = xv * r
            y = nrm * g_ref[...]
            dy = dh * (1.0 + sc_ref[...])
            dn = dy * g_ref[...]
            dx = dxi_ref[...] + r * (dn - nrm * jnp.mean(dn * nrm, axis=-1, keepdims=True))
            dx_ref[...] = dx
            st_ref[0:1, :] += _colsum(dy * nrm)
            st_ref[1:2, :] += _colsum(dh * y)
            st_ref[2:3, :] += _colsum(dh)
            if res:
                dt_ref[...] = (gt_ref[...] * dx).astype(BF16)
                st_ref[3:4, :] += _colsum(dx * t_ref[...])

    if halves:
        per = a.shape[2] // tk
        a_spec = pl.BlockSpec((None, tm, tk), lambda i, q: (q // per, i, q % per))
    else:
        a_spec = pl.BlockSpec((tm, tk), lambda i, q: (i, q))
    row = pl.BlockSpec((tm, D), lambda i, q: (i, 0))
    vec = pl.BlockSpec((1, D), lambda i, q: (0, 0))
    ins = [a, b, x, g, sc, sh, dx_in]
    in_specs = [a_spec, pl.BlockSpec((tk, D), lambda i, q: (q, 0)), row, vec, vec, vec, row]
    out_shape, out_specs = [jax.ShapeDtypeStruct((tt, D), F32)], [row]
    if res:
        ins += [t, gt]
        in_specs += [row, vec]
        out_shape.append(jax.ShapeDtypeStruct((tt, D), BF16))
        out_specs.append(row)
    out_shape.append(jax.ShapeDtypeStruct((8, D), F32))
    out_specs.append(pl.BlockSpec((8, D), lambda i, q: (0, 0)))
    return pl.pallas_call(body, grid=(tt // tm, nk), in_specs=in_specs, out_specs=tuple(out_specs),
                          out_shape=tuple(out_shape), scratch_shapes=[pltpu.VMEM((tm, D), F32)],
                          compiler_params=_cp(("arbitrary", "arbitrary")), name=name)(*ins)


def _loss_head(x, t, gt, fg, tgt, *, name, tr=256):
    tt = x.shape[0]

    def body(x_ref, t_ref, gt_ref, fg_ref, tgt_ref, dx_ref, dt_ref, st_ref):
        @pl.when(pl.program_id(0) == 0)
        def _():
            st_ref[...] = jnp.zeros_like(st_ref)

        tv = t_ref[...]
        xv = x_ref[...] + gt_ref[...] * tv
        r = lax.rsqrt(jnp.mean(xv * xv, axis=-1, keepdims=True) + EPS)
        nrm = xv * r
        err = nrm * fg_ref[...] - tgt_ref[...]
        st_ref[4:5, :] += 0.5 * jnp.sum(jnp.mean(err * err, axis=-1, keepdims=True), axis=0, keepdims=True)
        dy = err * (1.0 / D)
        dn = dy * fg_ref[...]
        dx = r * (dn - nrm * jnp.mean(dn * nrm, axis=-1, keepdims=True))
        dx_ref[...] = dx
        dt_ref[...] = (gt_ref[...] * dx).astype(BF16)
        st_ref[0:1, :] += _colsum(dy * nrm)
        st_ref[3:4, :] += _colsum(dx * tv)

    row, vec = _rb(tr, D), _whole((1, D))
    return pl.pallas_call(
        body, grid=(tt // tr,), in_specs=[row, row, vec, vec, row], out_specs=(row, row, _whole((8, D))),
        out_shape=(jax.ShapeDtypeStruct((tt, D), F32), jax.ShapeDtypeStruct((tt, D), BF16),
                   jax.ShapeDtypeStruct((8, D), F32)),
        compiler_params=_cp(("arbitrary",)), name=name)(x, t, gt, fg, tgt)


def _branch_merge(ya, yb, wa, wb, proj, *, name, tm=512):
    tt = ya.shape[0]
    tm = _tile(tt, tm)

    def body(ya_ref, yb_ref, wa_ref, wb_ref, ga_ref, gb_ref, pa_ref, pb_ref, o_ref):
        pa = _dot(ya_ref[...], wa_ref[...])
        pb = _dot(yb_ref[...], wb_ref[...])
        pa_ref[...] = pa.astype(BF16)
        pb_ref[...] = pb.astype(BF16)
        o_ref[...] = (_sigmoid(ga_ref[...]) * pa + _sigmoid(gb_ref[...]) * pb).astype(BF16)

    row, half, wsp = _rb(tm, D), _rb(tm, A_W), _whole((A_W, D))
    out = jax.ShapeDtypeStruct((tt, D), BF16)
    return pl.pallas_call(body, grid=(tt // tm,), in_specs=[half, half, wsp, wsp, _rb(tm, D, 0), _rb(tm, D, 1)],
                          out_specs=(row, row, row), out_shape=(out, out, out), compiler_params=_cp(("parallel",)),
                          name=name)(ya, yb, wa, wb, proj, proj)


def _out_adaln(a, w, x, gt, g, sc, sh, *, name, tk=None, tm=512):
    tt, k = a.shape
    tm, tk = _tile(tt, tm), tk or k
    nk = k // tk

    def body(a_ref, w_ref, x_ref, gt_ref, g_ref, sc_ref, sh_ref, t_ref, xo_ref, h_ref):
        q = pl.program_id(1)
        part = _dot(a_ref[...], w_ref[...])

        @pl.when(q == 0)
        def _():
            t_ref[...] = part

        @pl.when(q > 0)
        def _():
            t_ref[...] += part

        @pl.when(q == nk - 1)
        def _():
            xv = x_ref[...] + gt_ref[...] * t_ref[...]
            xo_ref[...] = xv
            r = lax.rsqrt(jnp.mean(xv * xv, axis=-1, keepdims=True) + EPS)
            h_ref[...] = ((xv * r * g_ref[...]) * (1.0 + sc_ref[...]) + sh_ref[...]).astype(BF16)

    row = pl.BlockSpec((tm, D), lambda i, q: (i, 0))
    vec = pl.BlockSpec((1, D), lambda i, q: (0, 0))
    f32 = jax.ShapeDtypeStruct((tt, D), F32)
    return pl.pallas_call(
        body, grid=(tt // tm, nk),
        in_specs=[pl.BlockSpec((tm, tk), lambda i, q: (i, q)), pl.BlockSpec((tk, D), lambda i, q: (q, 0)),
                  row, vec, vec, vec, vec],
        out_specs=(row, row, row), out_shape=(f32, f32, jax.ShapeDtypeStruct((tt, D), BF16)),
        compiler_params=_cp(("parallel", "arbitrary")), name=name)(a, w, x, gt, g, sc, sh)


def _out_bwd_merge(dt, wout, proj, pa, pb, *, name, tm=512):
    tt = dt.shape[0]
    tm = _tile(tt, tm)

    def body(dt_ref, w_ref, ga_ref, gb_ref, pa_ref, pb_ref, dg_ref, dpa_ref, dpb_ref):
        dm_v = _dot_nt(dt_ref[...], w_ref[...])
        sa, sb = _sigmoid(ga_ref[...]), _sigmoid(gb_ref[...])
        dpa_ref[...] = (dm_v * sa).astype(BF16)
        dpb_ref[...] = (dm_v * sb).astype(BF16)
        dg_ref[:, 0:D] = (dm_v * pa_ref[...].astype(F32) * sa * (1.0 - sa)).astype(BF16)
        dg_ref[:, D:2 * D] = (dm_v * pb_ref[...].astype(F32) * sb * (1.0 - sb)).astype(BF16)

    row = _rb(tm, D)
    return pl.pallas_call(
        body, grid=(tt // tm,), in_specs=[row, _whole((D, D)), _rb(tm, D, 0), _rb(tm, D, 1), row, row],
        out_specs=(_rb(tm, 2 * D), row, row),
        out_shape=(jax.ShapeDtypeStruct((tt, 2 * D), BF16), jax.ShapeDtypeStruct((tt, D), BF16),
                   jax.ShapeDtypeStruct((tt, D), BF16)),
        compiler_params=_cp(("parallel",)), name=name)(dt, wout, proj, proj, pa, pb)


FTN = FF // 2


def _ffn_in_swiglu(h, wt, *, name, tm=1024):
    tt = h.shape[0]
    tm = _tile(tt, tm)

    def body(h_ref, wg_ref, wu_ref, gu_ref, act_ref):
        hv = h_ref[...]
        g = _dot_nt(hv, wg_ref[...])
        u = _dot_nt(hv, wu_ref[...])
        gu_ref[0] = g.astype(BF16)
        gu_ref[1] = u.astype(BF16)
        act_ref[...] = (_silu(g) * u).astype(BF16)

    nj = FF // FTN
    return pl.pallas_call(
        body, grid=(tt // tm, nj),
        in_specs=[pl.BlockSpec((tm, D), lambda i, j: (i, 0)), pl.BlockSpec((FTN, D), lambda i, j: (j, 0)),
                  pl.BlockSpec((FTN, D), lambda i, j: (j + nj, 0))],
        out_specs=(pl.BlockSpec((2, tm, FTN), lambda i, j: (0, i, j)), pl.BlockSpec((tm, FTN), lambda i, j: (i, j))),
        out_shape=(jax.ShapeDtypeStruct((2, tt, FF), BF16), jax.ShapeDtypeStruct((tt, FF), BF16)),
        compiler_params=_cp(("parallel", "parallel")), name=name)(h, wt, wt)


def _ffn_out_bwd_swiglu(dt, wo, gu, *, name, tm=1024):
    tt = dt.shape[0]
    tm = _tile(tt, tm)

    def body(dt_ref, wo_ref, gu_ref, dgu_ref):
        da = _dot_nt(dt_ref[...], wo_ref[...])
        g, u = gu_ref[0].astype(F32), gu_ref[1].astype(F32)
        dgu_ref[0] = (da * u * _dsilu(g)).astype(BF16)
        dgu_ref[1] = (da * _silu(g)).astype(BF16)

    blk = pl.BlockSpec((2, tm, FTN), lambda i, j: (0, i, j))
    return pl.pallas_call(
        body, grid=(tt // tm, FF // FTN),
        in_specs=[pl.BlockSpec((tm, D), lambda i, j: (i, 0)), pl.BlockSpec((FTN, D), lambda i, j: (j, 0)), blk],
        out_specs=blk, out_shape=jax.ShapeDtypeStruct((2, tt, FF), BF16),
        compiler_params=_cp(("parallel", "parallel")), name=name)(dt, wo, gu)


BIAS_LW = 1152


def _bias_diagonals(table):
    n_far = A_PAST * CH - A_MAX_REL + 1
    far = jnp.broadcast_to(table[:, 2 * A_MAX_REL:], (A_HEADS, n_far))
    mid = jnp.flip(table[:, 1:2 * A_MAX_REL], axis=1)
    near = jnp.broadcast_to(table[:, 0:1], (A_HEADS, KB - n_far - (2 * A_MAX_REL - 1)))
    pos = jnp.concatenate([far, mid, near], axis=1)
    neg = jnp.broadcast_to(table[:, 2 * A_MAX_REL:], (A_HEADS, QB - 1))
    gap = jnp.zeros((A_HEADS, BIAS_LW - KB - (QB - 1)), F32)
    return jnp.concatenate([pos, gap, neg], axis=1)


def _bias_fwd(diag, *, name):
    def body(w_ref, o_ref):
        qc = lax.broadcasted_iota(jnp.int32, (QB, KB), 0) // CH + A_PAST
        col = lax.broadcasted_iota(jnp.int32, (QB, KB), 1)
        inband = (col // CH <= qc) & (col // CH >= qc - A_PAST)
        for h in range(A_HEADS):
            rows = pltpu.roll(jnp.broadcast_to(w_ref[h:h + 1, :], (QB, BIAS_LW)), 0, 1, stride=1, stride_axis=0)
            for var in range(3):
                o_ref[var, h] = jnp.where(inband & (col >= A_PAST * CH - QB * var), rows[:, :KB], NEG)

    return pl.pallas_call(body, out_shape=jax.ShapeDtypeStruct((3, A_HEADS, QB, KB), F32), compiler_params=_cp(),
                          name=name)(diag)


def _bias_bwd(dbias, *, name):
    def body(d_ref, o_ref):
        r = lax.broadcasted_iota(jnp.int32, (QB, QB), 0)
        c = lax.broadcasted_iota(jnp.int32, (QB, QB), 1)
        flip = jnp.where(r + c == QB - 1, 1.0, 0.0).astype(F32)
        for h in range(A_HEADS):
            x = jnp.concatenate([_dot(flip, d_ref[h], HI), jnp.zeros((QB, BIAS_LW - KB), F32)], axis=1)
            o_ref[h:h + 1, :] = jnp.sum(pltpu.roll(x, 0, 1, stride=1, stride_axis=0), axis=0, keepdims=True)

    return pl.pallas_call(body, out_shape=jax.ShapeDtypeStruct((A_HEADS, BIAS_LW), F32), compiler_params=_cp(),
                          name=name)(dbias)


def _kv_pad(proj, *, name, tr=256):
    tt = proj.shape[0]
    npad = A_PAST * CH // tr

    def body(k_ref, v_ref, ko_ref, vo_ref):
        i = pl.program_id(0)

        @pl.when(i < npad)
        def _():
            ko_ref[...] = jnp.zeros_like(ko_ref)
            vo_ref[...] = jnp.zeros_like(vo_ref)

        @pl.when(i >= npad)
        def _():
            ko_ref[...] = k_ref[...].astype(BF16)
            vo_ref[...] = v_ref[...].astype(BF16)

    src = lambda off: pl.BlockSpec((tr, A_W), lambda i: (jnp.maximum(i - npad, 0), off // A_W))
    out = jax.ShapeDtypeStruct((tt + A_PAST * CH, A_W), BF16)
    return pl.pallas_call(body, grid=(tt // tr + npad,), in_specs=[src(OFF_KA), src(OFF_VA)],
                          out_specs=(_rb(tr, A_W), _rb(tr, A_W)), out_shape=(out, out),
                          compiler_params=_cp(("parallel",)), name=name)(proj, proj)


def _attn_fwd(proj, kpad, vpad, bias, *, name):
    tt = proj.shape[0]

    def body(q_ref, k_ref, v_ref, b_ref, o_ref, l_ref):
        q0 = pl.multiple_of(pl.program_id(1) * QB, QB)
        q = q_ref[...] * (A_DH ** -0.5)
        k = k_ref[pl.ds(q0, KB), :]
        v = v_ref[pl.ds(q0, KB), :]
        lane = lax.broadcasted_iota(jnp.int32, (QB, LANE), 1)
        o = jnp.zeros((QB, LANE), F32)
        lse = jnp.zeros((QB, LANE), F32)
        for a in range(2):
            hm = (lane >= A_DH * a) & (lane < A_DH * (a + 1))
            s = _dot_nt(jnp.where(hm, q, 0.0).astype(BF16), k) + b_ref[a]
            m = jnp.max(s, axis=-1, keepdims=True)
            p = jnp.exp(s - m)
            l = jnp.sum(p, axis=-1, keepdims=True)
            o = jnp.where(hm, _dot(p.astype(BF16), v) / l, o)
            lse = jnp.where(hm, m + jnp.log(l), lse)
        o_ref[...] = o.astype(BF16)
        l_ref[...] = lse

    kv = pl.BlockSpec((tt + A_PAST * CH, LANE), lambda h, i: (0, h))
    blk = pl.BlockSpec((QB, LANE), lambda h, i: (i, h))
    return pl.pallas_call(
        body, grid=(A_W // LANE, tt // QB),
        in_specs=[pl.BlockSpec((QB, LANE), lambda h, i: (i, OFF_QA // LANE + h)), kv, kv,
                  pl.BlockSpec((None, 2, QB, KB), lambda h, i: (jnp.minimum(i, 2), h, 0, 0))],
        out_specs=(blk, blk),
        out_shape=(jax.ShapeDtypeStruct((tt, A_W), BF16), jax.ShapeDtypeStruct((tt, A_W), F32)),
        compiler_params=_cp(("parallel", "parallel")), name=name)(proj, kpad, vpad, bias)


def _attn_bwd(proj, kpad, vpad, bias, o, lse, do, *, name):
    tt = proj.shape[0]
    nq = tt // QB

    def body(q_ref, k_ref, v_ref, b_ref, o_ref, l_ref, do_ref, dq_ref, dko_ref, dvo_ref, db_ref, dk_ref, dv_ref):
        @pl.when(pl.program_id(1) == 0)
        def _():
            dk_ref[...] = jnp.zeros_like(dk_ref)
            dv_ref[...] = jnp.zeros_like(dv_ref)
            db_ref[...] = jnp.zeros_like(db_ref)

        q0 = pl.multiple_of(pl.program_id(1) * QB, QB)
        q, do_v, lse = q_ref[...] * (A_DH ** -0.5), do_ref[...], l_ref[...]
        k = k_ref[pl.ds(q0, KB), :]
        v = v_ref[pl.ds(q0, KB), :]
        dsum = do_v * o_ref[...].astype(F32)
        lane = lax.broadcasted_iota(jnp.int32, (QB, LANE), 1)
        dq = jnp.zeros((QB, LANE), F32)
        dk = jnp.zeros((KB, LANE), F32)
        dv = jnp.zeros((KB, LANE), F32)
        for a in range(2):
            hm = (lane >= A_DH * a) & (lane < A_DH * (a + 1))
            qa = jnp.where(hm, q, 0.0).astype(BF16)
            doa = jnp.where(hm, do_v, 0.0).astype(BF16)
            s = _dot_nt(qa, k) + b_ref[a]
            lse_a = jnp.max(jnp.where(hm, lse, NEG), axis=-1, keepdims=True)
            p = jnp.exp(s - lse_a)
            dp = _dot_nt(doa, v)
            dsum_a = jnp.sum(jnp.where(hm, dsum, 0.0), axis=-1, keepdims=True)
            ds = p * (dp - dsum_a)
            db_ref[a] += ds
            dsb = ds.astype(BF16)
            dq = jnp.where(hm, _dot(dsb, k) * (A_DH ** -0.5), dq)
            dk += _dot_tn(dsb, qa)
            dv += _dot_tn(p.astype(BF16), doa)
        dq_ref[...] = dq.astype(BF16)
        dk_ref[pl.ds(q0, KB), :] += dk
        dv_ref[pl.ds(q0, KB), :] += dv

        @pl.when(pl.program_id(1) == nq - 1)
        def _():
            dko_ref[...] = dk_ref[A_PAST * CH:, :].astype(BF16)
            dvo_ref[...] = dv_ref[A_PAST * CH:, :].astype(BF16)

    kv = pl.BlockSpec((tt + A_PAST * CH, LANE), lambda h, i: (0, h))
    blk = pl.BlockSpec((QB, LANE), lambda h, i: (i, h))
    col = pl.BlockSpec((tt, LANE), lambda h, i: (0, h))
    bsp = pl.BlockSpec((2, QB, KB), lambda h, i: (h, 0, 0))
    bias_in = pl.BlockSpec((None, 2, QB, KB), lambda h, i: (jnp.minimum(i, 2), h, 0, 0))
    out = jax.ShapeDtypeStruct((tt, A_W), BF16)
    return pl.pallas_call(
        body, grid=(A_W // LANE, nq),
        in_specs=[pl.BlockSpec((QB, LANE), lambda h, i: (i, OFF_QA // LANE + h)), kv, kv, bias_in, blk, blk, blk],
        out_specs=(blk, col, col, bsp),
        out_shape=(out, out, out, jax.ShapeDtypeStruct((A_HEADS, QB, KB), F32)),
        scratch_shapes=[pltpu.VMEM((tt + A_PAST * CH, LANE), F32), pltpu.VMEM((tt + A_PAST * CH, LANE), F32)],
        compiler_params=_cp(("parallel", "arbitrary")), name=name)(proj, kpad, vpad, bias, o, lse, do)


GTR = 256


def _taps(w_ref, grp):
    return [w_ref[j:j + 1, grp * B_W:(grp + 1) * B_W] for j in range(CONV_K)]


def _shifts(xe, rows):
    return [xe[8:8 + rows]] + [pltpu.roll(xe, s, 0)[8:8 + rows] for s in range(1, CONV_K)]


def _conv(shifts, taps):
    acc = taps[CONV_K - 1] * shifts[0]
    for s in range(1, CONV_K):
        acc = acc + taps[CONV_K - 1 - s] * shifts[s]
    return acc


def _qk_scale(grp):
    return B_DH ** -0.5 if grp == 0 else 1.0


def _act_fwd(c, grp):
    y = _silu(c)
    if grp == 2:
        return y
    parts = []
    for hd in range(B_HEADS):
        yh = y[:, hd * B_DH:(hd + 1) * B_DH]
        parts.append(yh * (lax.rsqrt(jnp.sum(yh * yh, axis=-1, keepdims=True) + EPS) * _qk_scale(grp)))
    return jnp.concatenate(parts, axis=1)


def _act_bwd(c, dy, grp):
    if grp == 2:
        return dy * _dsilu(c)
    y = _silu(c)
    parts = []
    for hd in range(B_HEADS):
        yh = y[:, hd * B_DH:(hd + 1) * B_DH]
        r = lax.rsqrt(jnp.sum(yh * yh, axis=-1, keepdims=True) + EPS)
        dyh = dy[:, hd * B_DH:(hd + 1) * B_DH] * _qk_scale(grp)
        parts.append(r * dyh - yh * (r * r * r) * jnp.sum(dyh * yh, axis=-1, keepdims=True))
    return jnp.concatenate(parts, axis=1) * _dsilu(c)


def _chunk_tri(n, upper=False):
    r = lax.broadcasted_iota(jnp.int32, (n, n), 0)
    c = lax.broadcasted_iota(jnp.int32, (n, n), 1)
    same = (r // CH) == (c // CH)
    return jnp.where(same & ((r <= c) if upper else (r >= c)), 1.0, 0.0).astype(F32)


def _gate_rows(ba, par_ref):
    lane = lax.broadcasted_iota(jnp.int32, ba.shape, 1)
    z = ba + par_ref[1:2, :]
    sp = jnp.maximum(z, 0.0) + jnp.log(1.0 + jnp.exp(-jnp.abs(z)))
    g = -jnp.exp(par_ref[0:1, :]) * sp
    return jnp.where(lane < B_HEADS, _sigmoid(ba), jnp.where(lane < 2 * B_HEADS, g, 0.0)), z


def _prev8(cb):
    return pl.BlockSpec((8, B_W), lambda i: (jnp.maximum(i * (GTR // 8) - 1, 0), cb))


def _next8(cb, nb):
    return pl.BlockSpec((8, B_W), lambda i: (jnp.minimum((i + 1) * (GTR // 8), nb * (GTR // 8) - 1), cb))


def _gdn_pre_fwd(proj, wconv, par, *, name):
    tt = proj.shape[0]

    def body(q_ref, k_ref, v_ref, qh_ref, kh_ref, vh_ref, ba_ref, w_ref, par_ref, qo_ref, ko_ref, vo_ref, aux_ref):
        first = pl.program_id(0) == 0
        for grp, (x_ref, h_ref, o_ref) in enumerate(((q_ref, qh_ref, qo_ref), (k_ref, kh_ref, ko_ref),
                                                     (v_ref, vh_ref, vo_ref))):
            xe = jnp.concatenate([jnp.where(first, 0.0, h_ref[...]), x_ref[...]], axis=0)
            o_ref[...] = _act_fwd(_conv(_shifts(xe, GTR), _taps(w_ref, grp)), grp)
        bg, _ = _gate_rows(ba_ref[...], par_ref)
        lane = lax.broadcasted_iota(jnp.int32, bg.shape, 1)
        aux_ref[...] = jnp.where(lane < B_HEADS, bg, _dot(_chunk_tri(GTR), bg, HI))

    col = lambda off: _rb(GTR, B_W, off // B_W)
    outs = jax.ShapeDtypeStruct((tt, B_W), F32)
    return pl.pallas_call(
        body, grid=(tt // GTR,),
        in_specs=[col(OFF_QB), col(OFF_KB), col(OFF_VB), _prev8(OFF_QB // B_W), _prev8(OFF_KB // B_W),
                  _prev8(OFF_VB // B_W), _rb(GTR, LANE, OFF_BA // LANE), _whole((CONV_K, 3 * B_W)),
                  _whole((8, LANE))],
        out_specs=(_rb(GTR, B_W), _rb(GTR, B_W), _rb(GTR, B_W), _rb(GTR, LANE)),
        out_shape=(outs, outs, outs, jax.ShapeDtypeStruct((tt, LANE), F32)),
        compiler_params=_cp(("parallel",)), name=name)(proj, proj, proj, proj, proj, proj, proj, wconv, par)


def _gdn_pre_bwd(proj, wconv, par, dq, dk, dv, daux, *, name):
    tt = proj.shape[0]
    nb = tt // GTR

    def body(q_ref, k_ref, v_ref, qh_ref, kh_ref, vh_ref, qn_ref, kn_ref, vn_ref, ba_ref, w_ref, par_ref,
             dq_ref, dk_ref, dv_ref, dqn_ref, dkn_ref, dvn_ref, daux_ref, dx_ref, dba_ref, dw_ref, dpar_ref):
        i = pl.program_id(0)
        first, last = i == 0, i == nb - 1

        @pl.when(first)
        def _():
            dw_ref[...] = jnp.zeros_like(dw_ref)
            dpar_ref[...] = jnp.zeros_like(dpar_ref)

        groups = ((q_ref, qh_ref, qn_ref, dq_ref, dqn_ref), (k_ref, kh_ref, kn_ref, dk_ref, dkn_ref),
                  (v_ref, vh_ref, vn_ref, dv_ref, dvn_ref))
        for grp, (x_ref, h_ref, xn_ref, d_ref, dn_ref) in enumerate(groups):
            taps = _taps(w_ref, grp)
            xe = jnp.concatenate([jnp.where(first, 0.0, h_ref[...]), x_ref[...]], axis=0)
            sh = _shifts(xe, GTR)
            dc = _act_bwd(_conv(sh, taps), d_ref[...], grp)
            xe_n = jnp.concatenate([x_ref[GTR - 8:GTR, :], xn_ref[...]], axis=0)
            dcn = _act_bwd(_conv(_shifts(xe_n, 8), taps), dn_ref[...], grp)
            dce = jnp.concatenate([dc, jnp.where(last, 0.0, dcn)], axis=0)
            dx = taps[CONV_K - 1] * dc
            dw_ref[CONV_K - 1:CONV_K, grp * B_W:(grp + 1) * B_W] += _colsum(dc * sh[0])
            for s in range(1, CONV_K):
                dx = dx + taps[CONV_K - 1 - s] * pltpu.roll(dce, GTR + 8 - s, 0)[0:GTR]
                dw_ref[CONV_K - 1 - s:CONV_K - s, grp * B_W:(grp + 1) * B_W] += _colsum(dc * sh[s])
            dx_ref[:, grp * B_W:(grp + 1) * B_W] = dx.astype(BF16)
        ba = ba_ref[...]
        lane = lax.broadcasted_iota(jnp.int32, ba.shape, 1)
        bg, z = _gate_rows(ba, par_ref)
        daux_v = daux_ref[...]
        dg = _dot(_chunk_tri(GTR, upper=True), daux_v, HI)
        dgl = jnp.where((lane >= B_HEADS) & (lane < 2 * B_HEADS), dg, 0.0)
        da = dgl * (-jnp.exp(par_ref[0:1, :])) * _sigmoid(z)
        dbr = jnp.where(lane < B_HEADS, daux_v * bg * (1.0 - bg), 0.0)
        dba_ref[...] = (dbr + da).astype(BF16)
        dpar_ref[0:1, :] += _colsum(dgl * bg)
        dpar_ref[1:2, :] += _colsum(da)

    col = lambda off: _rb(GTR, B_W, off // B_W)
    row, rowl = _rb(GTR, B_W), _rb(GTR, LANE)
    return pl.pallas_call(
        body, grid=(nb,),
        in_specs=[col(OFF_QB), col(OFF_KB), col(OFF_VB),
                  _prev8(OFF_QB // B_W), _prev8(OFF_KB // B_W), _prev8(OFF_VB // B_W),
                  _next8(OFF_QB // B_W, nb), _next8(OFF_KB // B_W, nb), _next8(OFF_VB // B_W, nb),
                  _rb(GTR, LANE, OFF_BA // LANE), _whole((CONV_K, 3 * B_W)), _whole((8, LANE)),
                  row, row, row, _next8(0, nb), _next8(0, nb), _next8(0, nb), rowl],
        out_specs=(_rb(GTR, 3 * B_W), rowl, _whole((8, 3 * B_W)), _whole((8, LANE))),
        out_shape=(jax.ShapeDtypeStruct((tt, 3 * B_W), BF16), jax.ShapeDtypeStruct((tt, LANE), BF16),
                   jax.ShapeDtypeStruct((8, 3 * B_W), F32), jax.ShapeDtypeStruct((8, LANE), F32)),
        compiler_params=_cp(("arbitrary",)), name=name)(
            proj, proj, proj, proj, proj, proj, proj, proj, proj, proj, wconv, par, dq, dk, dv, dq, dk, dv, daux)


def _col(x, j):
    lane = lax.broadcasted_iota(jnp.int32, x.shape, 1)
    return jnp.sum(jnp.where(lane == j, x, 0.0), axis=-1, keepdims=True)


def _split(x):
    hi = x.astype(BF16)
    return hi, (x - hi.astype(F32)).astype(BF16)


def _dot3(a, b, tn=False):
    dot = _dot_tn if tn else _dot
    (ah, al), (bh, bl) = _split(a), _split(b)
    return dot(ah, bh) + (dot(ah, bl) + dot(al, bh))


def _chunk_masks():
    r = lax.broadcasted_iota(jnp.int32, (CH, CH), 0)
    c = lax.broadcasted_iota(jnp.int32, (CH, CH), 1)
    return r > c, r >= c


def _gc_rows(aux, nc):
    t = jnp.transpose(aux[:, B_HEADS:2 * B_HEADS].reshape(nc, CH, B_HEADS), (0, 2, 1))
    return jnp.concatenate([t, jnp.zeros_like(t)], axis=1).reshape(nc * 8, CH)


_CHUNK8 = lambda width, n=1: pl.BlockSpec((8 * n, width), lambda i: (i, 0))
_CHUNK4 = lambda a, b, n=1: pl.BlockSpec((B_HEADS * n, a, b), lambda i: (i, 0, 0))
NCH = 2


def _per_chunk(body, rows):
    def wrapped(*refs):
        for ci in range(NCH):
            body(*[r.at[pl.ds(ci * n, n)] for r, n in zip(refs, rows)])
    return wrapped


def _gdn_lower(k, aux, auxt, *, name):
    tt = k.shape[0]

    def body(k_ref, aux_ref, auxt_ref, l_ref):
        aux_v = aux_ref[...]
        strict, _ = _chunk_masks()
        khs = [k_ref[:, hd * B_DH:(hd + 1) * B_DH].astype(BF16) for hd in range(B_HEADS)]
        kks = [_dot_nt(kh, kh) for kh in khs]
        for hd in range(B_HEADS):
            diff = _col(aux_v, B_HEADS + hd) - auxt_ref[hd:hd + 1, :]
            dec = jnp.exp(jnp.where(strict, diff, NEG))
            l_ref[hd] = _col(aux_v, hd) * kks[hd] * dec

    return pl.pallas_call(
        _per_chunk(body, (CH, CH, 8, B_HEADS)), grid=(tt // CH // NCH,),
        in_specs=[_rb(NCH * CH, B_W), _rb(NCH * CH, LANE), _CHUNK8(CH, NCH)],
        out_specs=_CHUNK4(CH, CH, NCH),
        out_shape=jax.ShapeDtypeStruct((tt // CH * B_HEADS, CH, CH), F32),
        compiler_params=_cp(("parallel",)), name=name)(k, aux, auxt)


def _tri_inverse(lt, *, name):
    nb = lt.shape[2]

    def body(l_ref, t_ref):
        rowid = lax.broadcasted_iota(jnp.int32, (CH, nb), 0)

        def outer(i, carry):
            def inner(j, acc):
                return acc + l_ref[i, pl.ds(j, 1), :] * t_ref[j]

            acc = lax.fori_loop(0, i, inner, jnp.zeros((CH, nb), F32))
            t_ref[i] = jnp.where(rowid == i, 1.0, 0.0) - acc
            return carry

        lax.fori_loop(0, CH, outer, 0)

    return pl.pallas_call(body, out_shape=jax.ShapeDtypeStruct(lt.shape, F32),
                          in_specs=[pl.BlockSpec(memory_space=pltpu.VMEM)],
                          out_specs=pl.BlockSpec(memory_space=pltpu.VMEM),
                          compiler_params=_cp(), name=name)(lt)


def _gdn_gates(aux_v, aux_last, auxt_ref, hd):
    _, incl = _chunk_masks()
    beta = _col(aux_v, hd)
    gc = _col(aux_v, B_HEADS + hd)
    gl = _col(aux_last, B_HEADS + hd)
    dec = jnp.exp(jnp.where(incl, gc - auxt_ref[hd:hd + 1, :], NEG))
    return beta, gc, gl, jnp.exp(gc), dec


def _gdn_intra(q, k, v, aux, auxt, tinv, *, name):
    tt = q.shape[0]
    nc = tt // CH

    def body(q_ref, k_ref, v_ref, aux_ref, auxt_ref, t_ref, u0_ref, w_ref, qd_ref, kd_ref, qk_ref, gle_ref):
        aux_v = aux_ref[...]
        aux_last = aux_ref[CH - 1:CH, :]
        lane8 = lax.broadcasted_iota(jnp.int32, (8, LANE), 1)
        gle = jnp.zeros((8, LANE), F32)
        heads = range(B_HEADS)
        sls = [slice(hd * B_DH, (hd + 1) * B_DH) for hd in heads]
        gates = [_gdn_gates(aux_v, aux_last, auxt_ref, hd) for hd in heads]
        qk0 = [_dot_nt(q_ref[:, sls[hd]].astype(BF16), k_ref[:, sls[hd]].astype(BF16)) for hd in heads]
        u0 = [_dot3(t_ref[hd], v_ref[:, sls[hd]] * gates[hd][0]) for hd in heads]
        wk = [_dot3(t_ref[hd], k_ref[:, sls[hd]] * (gates[hd][0] * gates[hd][3])) for hd in heads]
        for hd in heads:
            sl = sls[hd]
            beta, gc, gl, egc, dec = gates[hd]
            qk_ref[hd] = (qk0[hd] * dec).astype(BF16)
            u0_ref[:, sl] = u0[hd]
            w_ref[:, sl] = wk[hd].astype(BF16)
            qd_ref[:, sl] = (q_ref[:, sl] * egc).astype(BF16)
            kd_ref[:, sl] = (k_ref[:, sl] * jnp.exp(gl - gc)).astype(BF16)
            gle = gle + jnp.where(lane8 == hd, jnp.exp(gl), 0.0)
        gle_ref[...] = gle

    row = _rb(NCH * CH, B_W)
    half = jax.ShapeDtypeStruct((tt, B_W), BF16)
    return pl.pallas_call(
        _per_chunk(body, (CH, CH, CH, CH, 8, B_HEADS, CH, CH, CH, CH, B_HEADS, 8)), grid=(nc // NCH,),
        in_specs=[row, row, row, _rb(NCH * CH, LANE), _CHUNK8(CH, NCH), _CHUNK4(CH, CH, NCH)],
        out_specs=(row, row, row, row, _CHUNK4(CH, CH, NCH), _CHUNK8(LANE, NCH)),
        out_shape=(jax.ShapeDtypeStruct((tt, B_W), F32), half, half, half,
                   jax.ShapeDtypeStruct((nc * B_HEADS, CH, CH), BF16), jax.ShapeDtypeStruct((nc * 8, LANE), F32)),
        compiler_params=_cp(("parallel",)), name=name)(q, k, v, aux, auxt, tinv)


def _gdn_scan_fwd(u0, w, qd, kd, qk, gle, *, name):
    tt = u0.shape[0]
    nc = tt // CH

    def body(u0_ref, w_ref, qd_ref, kd_ref, qk_ref, gle_ref, o_ref, ss_ref, u_ref, s_ref):
        @pl.when(pl.program_id(0) == 0)
        def _():
            s_ref[...] = jnp.zeros_like(s_ref)

        gle = gle_ref[0:1, :]
        heads = range(B_HEADS)
        sls = [slice(hd * B_DH, (hd + 1) * B_DH) for hd in heads]
        st = [s_ref[hd] for hd in heads]
        sb = [t.astype(BF16) for t in st]
        ws = [_dot(w_ref[:, sls[hd]], sb[hd]) for hd in heads]
        qs = [_dot(qd_ref[:, sls[hd]], sb[hd]) for hd in heads]
        ub = [(u0_ref[:, sls[hd]] - ws[hd]).astype(BF16) for hd in heads]
        ku = [_dot_tn(kd_ref[:, sls[hd]], ub[hd]) for hd in heads]
        qu = [_dot(qk_ref[hd], ub[hd]) for hd in heads]
        for hd in heads:
            ss_ref[hd] = st[hd]
            u_ref[:, sls[hd]] = ub[hd]
            o_ref[:, sls[hd]] = qs[hd] + qu[hd]
            s_ref[hd] = st[hd] * _col(gle, hd) + ku[hd]

    row = _rb(CH, B_W)
    return pl.pallas_call(
        body, grid=(nc,), in_specs=[row, row, row, row, _CHUNK4(CH, CH), _CHUNK8(LANE)],
        out_specs=(row, _CHUNK4(B_DH, B_DH), row),
        out_shape=(jax.ShapeDtypeStruct((tt, B_W), F32), jax.ShapeDtypeStruct((nc * B_HEADS, B_DH, B_DH), F32),
                   jax.ShapeDtypeStruct((tt, B_W), BF16)),
        scratch_shapes=[pltpu.VMEM((B_HEADS, B_DH, B_DH), F32)],
        compiler_params=_cp(("arbitrary",)), name=name)(u0, w, qd, kd, qk, gle)


def _gdn_scan_bwd(w, qd, kd, qk, gle, do, *, name):
    tt = w.shape[0]
    nc = tt // CH

    def body(w_ref, qd_ref, kd_ref, qk_ref, gle_ref, do_ref, du_ref, dss_ref, ds_ref):
        @pl.when(pl.program_id(0) == 0)
        def _():
            ds_ref[...] = jnp.zeros_like(ds_ref)

        gle = gle_ref[0:1, :]
        heads = range(B_HEADS)
        sls = [slice(hd * B_DH, (hd + 1) * B_DH) for hd in heads]
        dst = [ds_ref[hd] for hd in heads]
        dob = [do_ref[:, sls[hd]].astype(BF16) for hd in heads]
        kds = [_dot(kd_ref[:, sls[hd]], dst[hd].astype(BF16)) for hd in heads]
        qkd = [_dot_tn(qk_ref[hd], dob[hd]) for hd in heads]
        qdd = [_dot_tn(qd_ref[:, sls[hd]], dob[hd]) for hd in heads]
        du = [qkd[hd] + kds[hd] for hd in heads]
        wdu = [_dot_tn(w_ref[:, sls[hd]], du[hd].astype(BF16)) for hd in heads]
        for hd in heads:
            dss_ref[hd] = dst[hd]
            du_ref[:, sls[hd]] = du[hd]
            ds_ref[hd] = qdd[hd] + _col(gle, hd) * dst[hd] - wdu[hd]

    rev = lambda width: pl.BlockSpec((CH, width), lambda i: (nc - 1 - i, 0))
    rev4 = lambda a, b: pl.BlockSpec((B_HEADS, a, b), lambda i: (nc - 1 - i, 0, 0))
    return pl.pallas_call(
        body, grid=(nc,),
        in_specs=[rev(B_W), rev(B_W), rev(B_W), rev4(CH, CH), pl.BlockSpec((8, LANE), lambda i: (nc - 1 - i, 0)), rev(B_W)],
        out_specs=(rev(B_W), rev4(B_DH, B_DH)),
        out_shape=(jax.ShapeDtypeStruct((tt, B_W), F32), jax.ShapeDtypeStruct((nc * B_HEADS, B_DH, B_DH), F32)),
        scratch_shapes=[pltpu.VMEM((B_HEADS, B_DH, B_DH), F32)],
        compiler_params=_cp(("arbitrary",)), name=name)(w, qd, kd, qk, gle, do)


def _gdn_bwd(q, k, v, aux, auxt, tinv, u0, w, u, ss, dss, du, do, *, name):
    tt = q.shape[0]
    nc = tt // CH

    def body(q_ref, k_ref, v_ref, aux_ref, auxt_ref, t_ref, u0_ref, w_ref, u_ref, ss_ref, dss_ref, du_ref, do_ref,
             dq_ref, dk_ref, dv_ref, daux_ref):
        aux_v = aux_ref[...]
        aux_last = aux_ref[CH - 1:CH, :]
        lane = lax.broadcasted_iota(jnp.int32, (CH, LANE), 1)
        rowi = lax.broadcasted_iota(jnp.int32, (CH, 1), 0)
        strict, incl = _chunk_masks()
        daux = jnp.zeros((CH, LANE), F32)
        heads = range(B_HEADS)
        sls = [slice(hd * B_DH, (hd + 1) * B_DH) for hd in heads]
        gates = [_gdn_gates(aux_v, aux_last, auxt_ref, hd) for hd in heads]
        kbs = [k_ref[:, sl].astype(BF16) for sl in sls]
        qbs = [q_ref[:, sl].astype(BF16) for sl in sls]
        sbs = [ss_ref[hd].astype(BF16) for hd in heads]
        dsbs = [dss_ref[hd].astype(BF16) for hd in heads]
        dobs = [do_ref[:, sl].astype(BF16) for sl in sls]
        kks = [_dot_nt(kbs[hd], kbs[hd]) for hd in heads]
        qk0s = [_dot_nt(qbs[hd], kbs[hd]) for hd in heads]
        dq_decs = [_dot_nt(dobs[hd], sbs[hd]) for hd in heads]
        dqks = [_dot_nt(dobs[hd], u_ref[:, sls[hd]]) for hd in heads]
        dk_decs = [_dot_nt(u_ref[:, sls[hd]], dsbs[hd]) for hd in heads]
        dws = [-_dot_nt(du_ref[:, sls[hd]].astype(BF16), sbs[hd]) for hd in heads]
        drvs = [_dot3(t_ref[hd], du_ref[:, sls[hd]], tn=True) for hd in heads]
        drks = [_dot3(t_ref[hd], dws[hd], tn=True) for hd in heads]
        dls = [-(_dot_nt(drvs[hd].astype(BF16), u0_ref[:, sls[hd]].astype(BF16))
                 + _dot_nt(drks[hd].astype(BF16), w_ref[:, sls[hd]])) for hd in heads]
        ldecs = [jnp.where(strict, dls[hd], 0.0) * gates[hd][4] for hd in heads]
        dqkm = [jnp.where(incl, dqks[hd], 0.0) for hd in heads]
        dkks = [(ldecs[hd] * gates[hd][0]).astype(BF16) for hd in heads]
        dqk0s = [(dqkm[hd] * gates[hd][4]).astype(BF16) for hd in heads]
        ddecs = [ldecs[hd] * gates[hd][0] * kks[hd] + dqkm[hd] * (qk0s[hd] * gates[hd][4]) for hd in heads]
        dq_mm = [_dot(dqk0s[hd], kbs[hd]) for hd in heads]
        dk_mm = [_dot_tn(dqk0s[hd], qbs[hd]) + _dot(dkks[hd], kbs[hd]) + _dot_tn(dkks[hd], kbs[hd]) for hd in heads]
        dcols = [_col_from_rowsum(ddecs[hd]) for hd in heads]
        for hd in heads:
            sl = sls[hd]
            qh, kh, vh = q_ref[:, sl], k_ref[:, sl], v_ref[:, sl]
            beta, gc, gl, egc, dec = gates[hd]
            ekd, eg_last = jnp.exp(gl - gc), jnp.exp(gl)
            kk = kks[hd]
            st, dst = ss_ref[hd], dss_ref[hd]
            dq_dec, dk_dec = dq_decs[hd], dk_decs[hd]
            dgl = jnp.sum(jnp.sum(st * dst, axis=-1, keepdims=True), axis=0, keepdims=True) * eg_last
            drv, drk = drvs[hd], drks[hd]
            dv_ref[:, sl] = drv * beta
            rk = jnp.sum(drk * kh, axis=-1, keepdims=True)
            dbeta = jnp.sum(drv * vh, axis=-1, keepdims=True) + rk * egc
            dgc = rk * beta * egc
            dk = drk * (beta * egc)
            ldec, ddec = ldecs[hd], ddecs[hd]
            dbeta = dbeta + jnp.sum(ldec * kk, axis=-1, keepdims=True)
            dq = dq_mm[hd] + dq_dec * egc
            dk = dk + dk_mm[hd] + dk_dec * ekd
            dgc = dgc + jnp.sum(ddec, axis=-1, keepdims=True) - dcols[hd]
            dgc = dgc + jnp.sum(dq_dec * qh, axis=-1, keepdims=True) * egc
            kd = jnp.sum(dk_dec * kh, axis=-1, keepdims=True) * ekd
            dgc = dgc - kd
            dgc = dgc + jnp.where(rowi == CH - 1, jnp.sum(kd, axis=0, keepdims=True) + dgl, 0.0)
            dq_ref[:, sl] = dq
            dk_ref[:, sl] = dk
            daux = daux + jnp.where(lane == hd, dbeta, 0.0) + jnp.where(lane == B_HEADS + hd, dgc, 0.0)
        daux_ref[...] = daux

    row = _rb(NCH * CH, B_W)
    outs = jax.ShapeDtypeStruct((tt, B_W), F32)
    return pl.pallas_call(
        _per_chunk(body, (CH, CH, CH, CH, 8, B_HEADS, CH, CH, CH, B_HEADS, B_HEADS, CH, CH, CH, CH, CH, CH)),
        grid=(nc // NCH,),
        in_specs=[row, row, row, _rb(NCH * CH, LANE), _CHUNK8(CH, NCH), _CHUNK4(CH, CH, NCH), row, row, row,
                  _CHUNK4(B_DH, B_DH, NCH), _CHUNK4(B_DH, B_DH, NCH), row, row],
        out_specs=(row, row, row, _rb(NCH * CH, LANE)),
        out_shape=(outs, outs, outs, jax.ShapeDtypeStruct((tt, LANE), F32)),
        compiler_params=_cp(("parallel",)), name=name)(q, k, v, aux, auxt, tinv, u0, w, u, ss, dss, du, do)


def _col_from_rowsum(m):
    hi, lo = _split(m)
    ones = jnp.ones((CH, LANE), BF16)
    return (_dot_tn(hi, ones) + _dot_tn(lo, ones))[:, 0:1]


def _gdn_post_fwd(o, proj, gn, *, name, tr=256):
    tt = o.shape[0]

    def body(o_ref, z_ref, g_ref, y_ref):
        for hd in range(B_HEADS):
            sl = slice(hd * B_DH, (hd + 1) * B_DH)
            oh = o_ref[:, sl]
            r = lax.rsqrt(jnp.mean(oh * oh, axis=-1, keepdims=True) + EPS)
            y_ref[:, sl] = (oh * r * g_ref[...] * _silu(z_ref[:, sl])).astype(BF16)

    return pl.pallas_call(body, grid=(tt // tr,), in_specs=[_rb(tr, B_W), _rb(tr, B_W, OFF_ZB // B_W), _whole((1, B_DH))],
                          out_specs=_rb(tr, B_W), out_shape=jax.ShapeDtypeStruct((tt, B_W), BF16),
                          compiler_params=_cp(("parallel",)), name=name)(o, proj, gn)


def _gdn_post_bwd(o, proj, gn, dy, *, name, tr=256):
    tt = o.shape[0]

    def body(o_ref, z_ref, g_ref, dy_ref, do_ref, dz_ref, dg_ref):
        @pl.when(pl.program_id(0) == 0)
        def _():
            dg_ref[...] = jnp.zeros_like(dg_ref)

        g = g_ref[...]
        for hd in range(B_HEADS):
            sl = slice(hd * B_DH, (hd + 1) * B_DH)
            oh, zh, dyh = o_ref[:, sl], z_ref[:, sl], dy_ref[:, sl]
            r = lax.rsqrt(jnp.mean(oh * oh, axis=-1, keepdims=True) + EPS)
            a = oh * r
            s = _silu(zh)
            da = dyh * g * s
            dg_ref[0:1, :] += _colsum(dyh * a * s)
            dz_ref[:, sl] = (dyh * a * g * _dsilu(zh)).astype(BF16)
            do_ref[:, sl] = r * (da - a * jnp.mean(da * a, axis=-1, keepdims=True))

    return pl.pallas_call(
        body, grid=(tt // tr,), in_specs=[_rb(tr, B_W), _rb(tr, B_W, OFF_ZB // B_W), _whole((1, B_DH)), _rb(tr, B_W)],
        out_specs=(_rb(tr, B_W), _rb(tr, B_W), _whole((8, B_DH))),
        out_shape=(jax.ShapeDtypeStruct((tt, B_W), F32), jax.ShapeDtypeStruct((tt, B_W), BF16),
                   jax.ShapeDtypeStruct((8, B_DH), F32)),
        compiler_params=_cp(("arbitrary",)), name=name)(o, proj, gn, dy)


def _adamw(parts, w, m, v, own=None, sel=None, *, name, tr=256):
    npart, nl, r, c = parts.shape
    tr = max([t for t in range(8, min(r, tr) + 1, 8) if r % t == 0], default=r)
    tc = c if tr < r or r <= 256 or c % 256 else 256
    c1, c2 = 1.0 - ADAM_B1 ** ADAM_STEP, 1.0 - ADAM_B2 ** ADAM_STEP

    def body(*refs):
        if own is None:
            p_ref, w_ref, m_ref, v_ref, g_ref, d_ref, mo_ref, vo_ref = refs
            part = lambda i: p_ref[i].astype(F32)
        else:
            p_ref, w_ref, m_ref, v_ref, own_ref, sel_ref, g_ref, d_ref, mo_ref, vo_ref = refs
            part = lambda i: jnp.where(sel_ref[i:i + 1, 0:1] > 0.5, own_ref[...].astype(F32), p_ref[i].astype(F32))
        g = part(0)
        for i in range(1, npart):
            g = g + part(i)
        mn = ADAM_B1 * m_ref[...] + (1.0 - ADAM_B1) * g
        vn = ADAM_B2 * v_ref[...] + (1.0 - ADAM_B2) * (g * g)
        g_ref[...] = g
        mo_ref[...] = mn
        vo_ref[...] = vn
        d_ref[...] = -ADAM_LR * ((mn / c1) / (jnp.sqrt(vn / c2) + ADAM_EPS) + ADAM_WD * w_ref[...])

    row = pl.BlockSpec((None, tr, tc), lambda l, i, j: (l, i, j))
    out = jax.ShapeDtypeStruct((nl, r, c), F32)
    ins, in_specs = [parts, w, m, v], [pl.BlockSpec((npart, None, tr, tc), lambda l, i, j: (0, l, i, j)), row, row, row]
    if own is not None:
        ins += [own, sel]
        in_specs += [row, pl.BlockSpec((N_DEV, LANE), lambda l, i, j: (0, 0))]
    return pl.pallas_call(body, grid=(nl, r // tr, c // tc), in_specs=in_specs, out_specs=(row, row, row, row),
                          out_shape=(out, out, out, out), compiler_params=_cp(("parallel", "parallel", "parallel")),
                          name=name)(*ins)


def _peer(k):
    x, y, c = lax.axis_index("x"), lax.axis_index("y"), lax.axis_index("c")
    return ((1 - x) if k & 4 else x, (1 - y) if k & 2 else y, (1 - c) if k & 1 else c)


def _my_index():
    return 4 * lax.axis_index("x") + 2 * lax.axis_index("y") + lax.axis_index("c")


def _index_of(p):
    return 4 * p[0] + 2 * p[1] + p[2]


def _all_gather(xs, *, name):
    n = len(xs)

    def body(*refs):
        x_refs, o_refs = refs[:n], refs[n:2 * n]
        send, recv, loc = refs[2 * n:]
        me = _my_index()
        copies = []
        for a in range(n):
            cp = pltpu.make_async_copy(x_refs[a], o_refs[a].at[me], loc.at[a])
            cp.start()
            copies.append(cp)
        rdmas = []
        for a in range(n):
            for k in range(1, N_DEV):
                r = pltpu.make_async_remote_copy(
                    src_ref=x_refs[a], dst_ref=o_refs[a].at[me], send_sem=send.at[a, k - 1], recv_sem=recv.at[a, k - 1],
                    device_id=_peer(k), device_id_type=pl.DeviceIdType.MESH)
                r.start()
                rdmas.append(r)
        for a in range(n):
            for k in range(1, N_DEV):
                pltpu.make_async_remote_copy(
                    src_ref=x_refs[a], dst_ref=o_refs[a].at[_index_of(_peer(k))], send_sem=send.at[a, k - 1],
                    recv_sem=recv.at[a, k - 1], device_id=_peer(k), device_id_type=pl.DeviceIdType.MESH).wait_recv()
        for r in rdmas:
            r.wait_send()
        for cp in copies:
            cp.wait()

    any_spec = pl.BlockSpec(memory_space=pl.ANY)
    return pl.pallas_call(
        body, in_specs=[any_spec] * n, out_specs=tuple([any_spec] * n),
        out_shape=tuple(jax.ShapeDtypeStruct((N_DEV,) + x.shape, x.dtype) for x in xs),
        scratch_shapes=[pltpu.SemaphoreType.DMA((n, N_DEV - 1)), pltpu.SemaphoreType.DMA((n, N_DEV - 1)),
                        pltpu.SemaphoreType.DMA((n,))],
        name=name)(*xs)


def _all_gather_two_level(xs, *, name):
    n = len(xs)

    def body(*refs):
        x_refs, o_refs = refs[:n], refs[n:2 * n]
        send, recv, loc = refs[2 * n:]
        x, y, c = lax.axis_index("x"), lax.axis_index("y"), lax.axis_index("c")
        me, sibling = (x, y, c), (x, y, 1 - c)
        chips = [(1 - x, y), (x, 1 - y), (1 - x, 1 - y)]

        def copy(a, k, block, to, src=None):
            dst = o_refs[a].at[_index_of(block)]
            return pltpu.make_async_remote_copy(src_ref=dst if src is None else src, dst_ref=dst, send_sem=send.at[a, k],
                                                recv_sem=recv.at[a, k], device_id=to, device_id_type=pl.DeviceIdType.MESH)

        mine = [pltpu.make_async_copy(x_refs[a], o_refs[a].at[_index_of(me)], loc.at[a]) for a in range(n)]
        first = [copy(a, 0, me, sibling, src=x_refs[a]) for a in range(n)]
        first += [copy(a, 1 + j, me, (*chip, c), src=x_refs[a]) for a in range(n) for j, chip in enumerate(chips)]
        for cp in mine + first:
            cp.start()
        passed = []
        for a in range(n):
            for j, chip in enumerate(chips):
                copy(a, 1 + j, (*chip, c), me).wait_recv()
                passed.append(copy(a, 4 + j, (*chip, c), sibling))
                passed[-1].start()
        for a in range(n):
            copy(a, 0, sibling, me).wait_recv()
            for j, chip in enumerate(chips):
                copy(a, 4 + j, (*chip, 1 - c), me).wait_recv()
        for cp in first + passed:
            cp.wait_send()
        for cp in mine:
            cp.wait()

    any_spec = pl.BlockSpec(memory_space=pl.ANY)
    return pl.pallas_call(
        body, in_specs=[any_spec] * n, out_specs=tuple([any_spec] * n),
        out_shape=tuple(jax.ShapeDtypeStruct((N_DEV,) + t.shape, t.dtype) for t in xs),
        scratch_shapes=[pltpu.SemaphoreType.DMA((n, N_DEV - 1)), pltpu.SemaphoreType.DMA((n, N_DEV - 1)),
                        pltpu.SemaphoreType.DMA((n,))],
        name=name)(*xs)


_HBM = pl.BlockSpec(memory_space=pltpu.HBM)
_SEM = pl.BlockSpec(memory_space=pltpu.SEMAPHORE)
_EFFECT = pltpu.SideEffectType.DATAFLOW_SIDE_EFFECTING


def _split_copy(src_ref, land_ref, send, recv, a, k, scatter, slot, sending):
    me, peer = _my_index(), _index_of(_peer(k))
    src = src_ref.at[peer if sending else me] if scatter else src_ref
    land = land_ref.at[me if sending else peer]
    if slot is not None:
        land = land.at[slot]
    sem = a * (N_DEV - 1) + k - 1
    return pltpu.make_async_remote_copy(src_ref=src, dst_ref=land, send_sem=send.at[sem], recv_sem=recv.at[sem],
                                        device_id=_peer(k), device_id_type=pl.DeviceIdType.MESH)


def _exchange_start(srcs, lands, after, *, scatter, slot=None, name):
    n = len(srcs)

    def body(*refs):
        src_refs, land_refs = refs[:n], refs[n:2 * n]
        send, recv, token = refs[2 * n + 1], refs[2 * n + 2], refs[-1]
        for a in range(n):
            for k in range(1, N_DEV):
                _split_copy(src_refs[a], land_refs[a], send, recv, a, k, scatter, slot, True).start()
        token[...] = jnp.zeros_like(token)

    hbm = lambda t: pltpu.HBM(t.shape, t.dtype)
    sems = pltpu.SemaphoreType.DMA((n * (N_DEV - 1),))
    out = pl.pallas_call(
        body, name=name,
        out_shape=(sems, sems, *[hbm(t) for t in srcs], *[hbm(t) for t in lands], jax.ShapeDtypeStruct((8, LANE), F32)),
        in_specs=[_HBM] * (2 * n) + [pl.BlockSpec(memory_space=pl.ANY)],
        out_specs=(_SEM, _SEM, *[_HBM] * (2 * n), pl.BlockSpec(memory_space=pltpu.VMEM)),
        input_output_aliases={i: 2 + i for i in range(2 * n)},
        compiler_params=pltpu.CompilerParams(has_side_effects=_EFFECT),
    )(*[pltpu.with_memory_space_constraint(t, pltpu.HBM) for t in (*srcs, *lands)], after)
    return out[0], out[1], out[2:2 + n], out[2 + n:2 + 2 * n], out[-1]


def _exchange_wait(send, recv, srcs, lands, after, *, scatter, slot=None, name):
    n = len(srcs)

    def body(*refs):
        src_refs, land_refs = refs[:n], refs[n:2 * n]
        send_ref, recv_ref = refs[2 * n], refs[2 * n + 1]
        for a in range(n):
            for k in range(1, N_DEV):
                _split_copy(src_refs[a], land_refs[a], send_ref, recv_ref, a, k, scatter, slot, True).wait_send()
                _split_copy(src_refs[a], land_refs[a], send_ref, recv_ref, a, k, scatter, slot, False).wait_recv()

    hbm = lambda t: pltpu.HBM(t.shape, t.dtype)
    out = pl.pallas_call(
        body, name=name, out_shape=(*[hbm(t) for t in srcs], *[hbm(t) for t in lands]),
        in_specs=[_HBM] * (2 * n) + [_SEM, _SEM, pl.BlockSpec(memory_space=pl.ANY)],
        out_specs=tuple([_HBM] * (2 * n)), input_output_aliases={i: i for i in range(2 * n)},
        compiler_params=pltpu.CompilerParams(has_side_effects=_EFFECT),
    )(*srcs, *lands, send, recv, after)
    return out[:n], out[n:]


def _win_to_mine(wt):
    pad = jnp.zeros((IN_PAD - IN_DIM,) + wt.shape[1:], wt.dtype)
    return jnp.concatenate([wt[3592:5640], wt[0:3584], wt[3584:3592], pad], axis=0)


def _win_from_mine(gt):
    return jnp.concatenate([gt[2048:5632], gt[5632:5640], gt[0:2048]], axis=0)


def _pad_rows(a, mult=8):
    r = (-a.shape[0]) % mult
    return a if r == 0 else jnp.concatenate([a, jnp.zeros((r,) + a.shape[1:], a.dtype)], axis=0)


def _lanes(vec, start):
    return jnp.zeros((1, LANE), F32).at[0, start:start + vec.shape[0]].set(vec)


def _small_spec(depth):
    return (("b_ada", (depth, 6 * D)), ("norm1_g", (depth, D)), ("norm2_g", (depth, D)),
            ("rel_table", (depth, A_HEADS, 2 * A_MAX_REL + 1)), ("a_log", (depth, B_HEADS)),
            ("dt_bias", (depth, B_HEADS)), ("gdn_norm_g", (depth, B_DH)), ("final_g", (D,)))


def _pack_small(d, extra, depth):
    spec = _small_spec(depth)
    rows = -(-(sum(math.prod(s) for _, s in spec) + 1) // (8 * LANE)) * 8
    flat = jnp.concatenate([d[n].reshape(-1).astype(F32) for n, _ in spec] + [extra.reshape(-1)])
    flat = jnp.concatenate([flat, jnp.zeros((rows * LANE - flat.shape[0],), F32)])
    return flat.reshape(rows, LANE)


def _unpack_small(p, depth):
    flat = p.reshape(-1)
    out, off = {}, 0
    for n, s in _small_spec(depth):
        sz = math.prod(s)
        out[n] = flat[off:off + sz].reshape(s)
        off += sz
    return out, flat[off]


def kernel(x, c, w_ada, b_ada, norm1_g, norm2_g, w_in, rel_table, w_conv, a_log, dt_bias, gdn_norm_g, w_branch_a, w_branch_b, w_out, w_ffn_in, w_ffn_out, final_g, loss_target, m_w_ada, m_b_ada, m_norm1_g, m_norm2_g, m_w_in, m_rel_table, m_w_conv, m_a_log, m_dt_bias, m_gdn_norm_g, m_w_branch_a, m_w_branch_b, m_w_out, m_w_ffn_in, m_w_ffn_out, m_final_g, v_w_ada, v_b_ada, v_norm1_g, v_norm2_g, v_w_in, v_rel_table, v_w_conv, v_a_log, v_dt_bias, v_gdn_norm_g, v_w_branch_a, v_w_branch_b, v_w_out, v_w_ffn_in, v_w_ffn_out, v_final_g):
    tt = x.shape[1]
    x0 = x[0]
    tgt = loss_target[0]
    me = _my_index()
    depth = w_in.shape[0]

    tr_ = lambda t: jnp.transpose(t, (0, 2, 1))
    shards = [tr_(w_in).astype(BF16), w_branch_a.astype(BF16), w_branch_b.astype(BF16), w_out.astype(BF16),
              tr_(w_ffn_in).astype(BF16), w_ffn_out.astype(BF16), w_conv]
    names = ("win", "wa", "wb", "wout", "wfi", "wfo", "wconv")
    early, late, every = (0, 6), (1, 2, 3, 4, 5), tuple(range(7))
    first = _all_gather_two_level([shards[i][0] for i in early] + [_pad_rows(c)], name="gather_first")
    c_all = first[-1][:, 0, :]
    is_me = lax.broadcasted_iota(jnp.int32, (N_DEV, 1, 1), 0) == me

    def unpack(idx, g):
        cols = lambda t: jnp.transpose(t, (1, 0, 2)).reshape(t.shape[1], N_DEV * t.shape[2])
        rows = lambda t: t.reshape(N_DEV * t.shape[1], t.shape[2])
        how = (lambda t: _win_to_mine(rows(t)), cols, cols, rows, rows, rows, cols)
        return {names[i]: how[i](t) for i, t in zip(idx, g)}

    def gather_start(l, idx, after, tag=""):
        srcs = [shards[i][l] for i in idx]
        lands = [lax.empty((N_DEV,) + t.shape, t.dtype) for t in srcs]
        return _exchange_start(srcs, lands, after, scatter=False, name=f"gather_start_{l}{tag}")

    def gather_wait(l, idx, pending, after, tag=""):
        send, recv, srcs, lands, _ = pending
        srcs, lands = _exchange_wait(send, recv, srcs, lands, after, scatter=False, name=f"gather_wait_{l}{tag}")
        return unpack(idx, [jnp.where(is_me, t[None], g) for g, t in zip(lands, srcs)])

    weights = [unpack(early, first[:-1])] + [None] * (depth - 1)
    pending0 = gather_start(0, late, first[-1], "_rest")
    pending = gather_start(1, every, pending0[-1]) if depth > 1 else None
    cond = c_all * (1.0 / (1.0 + jnp.exp(-c_all)))
    cond = _pad_rows(cond, 16)

    mod_cols = jnp.stack([_mm(cond, w_ada[l], name="mod_mm")[:N_DEV] for l in range(depth)])
    (g_mod,) = _all_gather([mod_cols], name="gather_mod")
    mod_all = jnp.transpose(g_mod, (1, 2, 0, 3)).reshape(depth, N_DEV, 6 * D)
    mod = lax.dynamic_index_in_dim(mod_all, me, axis=1, keepdims=False) + b_ada
    mods = mod.reshape(depth, 6, 1, D)

    n1g, n2g = norm1_g.reshape(depth, 1, D), norm2_g.reshape(depth, 1, D)
    gng = gdn_norm_g.reshape(depth, 1, B_DH)
    fg = final_g.reshape(1, D)

    saved = []
    tok = (pending if pending is not None else pending0)[-1][0, 0]
    xin, h1 = _adaln_fwd(x0, n1g[0], mods[0, 1] + tok, mods[0, 0], name="adaln1_first")
    for l in range(depth):
        sh1, sc1, gt1, sh2, sc2, gt2 = (mods[l, i] for i in range(6))
        wl = weights[l]
        proj = _mm(h1, wl["win"], tb=True, name="proj_mm", tn=1152)
        kpad, vpad = _kv_pad(proj, name="kv_pad")
        diag, bias_vjp = jax.vjp(_bias_diagonals, rel_table[l])
        bias = _bias_fwd(diag, name="bias_fwd")
        ya, lse = _attn_fwd(proj, kpad, vpad, bias, name="attn_fwd")
        par = jnp.concatenate([_lanes(a_log[l], B_HEADS), _lanes(dt_bias[l], B_HEADS), jnp.zeros((6, LANE), F32)], axis=0)
        qn, kn, vn, aux = _gdn_pre_fwd(proj, wl["wconv"], par, name="gdn_pre_fwd")
        auxt = _gc_rows(aux, tt // CH)
        lower = _gdn_lower(kn, aux, auxt, name="gdn_lower")
        tinv = jnp.transpose(_tri_inverse(jnp.transpose(lower, (1, 2, 0)), name="gdn_tri_inverse"), (2, 0, 1))
        u0, wg, qd, kd, qk, gle = _gdn_intra(qn, kn, vn, aux, auxt, tinv, name="gdn_intra")
        og, ss, ug = _gdn_scan_fwd(u0, wg, qd, kd, qk, gle, name="gdn_scan_fwd")
        yb = _gdn_post_fwd(og, proj, gng[l], name="gdn_post_fwd")
        if l == 0:
            wl.update(gather_wait(0, late, pending0, yb, "_rest"))
        pa, pb, merged = _branch_merge(ya, yb, wl["wa"], wl["wb"], proj, name="branch_merge")
        t1, x2, h2 = _out_adaln(merged, wl["wout"], xin, gt1, n2g[l], sc2, sh2, name="out_adaln2")
        gu, act = _ffn_in_swiglu(h2, wl["wfi"], name="ffn_in_swiglu")
        saved.append(dict(xin=xin, h1=h1, proj=proj, kpad=kpad, vpad=vpad, bias=bias, bias_vjp=bias_vjp, ya=ya, lse=lse,
                          par=par, qn=qn, kn=kn, vn=vn, aux=aux, auxt=auxt, tinv=tinv, ss=ss, og=og, yb=yb, pa=pa, pb=pb,
                          u0=u0, wg=wg, qd=qd, kd=kd, qk=qk, gle=gle, ug=ug,
                          merged=merged, t1=t1, x2=x2, h2=h2, gu=gu, act=act))
        if l + 1 < depth:
            weights[l + 1] = gather_wait(l + 1, every, pending, act)
            pending = gather_start(l + 2, every, weights[l + 1]["wconv"]) if l + 2 < depth else None
            tok = pending[-1][0, 0] if pending is not None else 0.0
            t2, xin, h1 = _out_adaln(act, wl["wfo"], x2, gt2, n1g[l + 1], mods[l + 1, 1] + tok, mods[l + 1, 0],
                                     tk=FTN, name="ffn_out_adaln1")
        else:
            t2 = _mm(act, wl["wfo"], name="ffn_out_mm", tk=FTN)
        saved[-1]["t2"] = t2

    s = saved[-1]
    dx, dt2, st = _loss_head(s["x2"], s["t2"], mods[depth - 1, 5], fg, tgt, name="loss_head")
    loss_part = st[4, 0]
    small_g = {"final_g": st[0]}
    dmod_rows = [None] * depth
    for n in ("norm1_g", "norm2_g", "rel_table", "a_log", "dt_bias", "gdn_norm_g"):
        small_g[n] = [None] * depth
    dgt2 = st[3]
    cols_slabs = lambda g: jnp.transpose(g.reshape(g.shape[0], N_DEV, g.shape[1] // N_DEV), (1, 0, 2))
    rows_slabs = lambda g: g.reshape(N_DEV, g.shape[0] // N_DEV, g.shape[1])
    mix, ffn = (0, 1, 2, 3, 6), (4, 5)
    lands = {kind: [lax.empty((N_DEV,) + shards[i].shape, shards[i].dtype) for i in idx]
             for kind, idx in (("mix", mix), ("ffn", ffn))}
    own = {kind: [None] * depth for kind in lands}
    pending_s = {kind: None for kind in lands}

    def scatter(kind, l, srcs, after):
        if pending_s[kind] is not None:
            done, lands[kind] = _exchange_wait(*pending_s[kind][:4], after, scatter=True, slot=l + 1,
                                               name=f"scatter_wait_{kind}_{l + 1}")
            own[kind][l + 1] = [lax.dynamic_index_in_dim(t, me, 0, keepdims=False) for t in done]
        pending_s[kind] = _exchange_start(srcs, lands[kind], after, scatter=True, slot=l, name=f"scatter_start_{kind}_{l}")
        return pending_s[kind][-1][0, 0]

    for l in reversed(range(depth)):
        s, wl = saved[l], weights[l]
        sh1, sc1, gt1, sh2, sc2, gt2 = (mods[l, i] for i in range(6))
        gw_fo = _mm(s["act"], dt2, ta=True, out_dtype=BF16, name="ffn_out_dw", tm=1408)
        dgu = _ffn_out_bwd_swiglu(dt2, wl["wfo"], s["gu"], name="ffn_out_bwd_swiglu")
        gw_fi = _mm(dgu, s["h2"], ta=True, out_dtype=BF16, name="ffn_in_dw", tm=1408)
        sc2 = sc2 + scatter("ffn", l, [rows_slabs(gw_fi), rows_slabs(gw_fo)], gw_fi)
        dx, dt1, st2 = _mm_adaln_bwd(dgu, wl["wfi"], s["x2"], n2g[l], sc2, sh2, dx, s["t1"], gt1, tk=FTN,
                                     name="ffn_in_dx_adaln2")
        gw_out = _mm(s["merged"], dt1, ta=True, out_dtype=BF16, name="out_dw")
        dgates, dpa, dpb = _out_bwd_merge(dt1, wl["wout"], s["proj"], s["pa"], s["pb"], name="out_bwd_merge")
        gw_a = _mm(s["ya"], dpa, ta=True, out_dtype=BF16, name="branch_a_dw")
        gw_b = _mm(s["yb"], dpb, ta=True, out_dtype=BF16, name="branch_b_dw")
        dya = _mm(dpa, wl["wa"], tb=True, name="branch_a_dx")
        dyb = _mm(dpb, wl["wb"], tb=True, name="branch_b_dx")
        dqa, dka, dva, dbias = _attn_bwd(s["proj"], s["kpad"], s["vpad"], s["bias"], s["ya"], s["lse"], dya,
                                             name="attn_bwd")
        ddiag = jnp.roll(_bias_bwd(dbias, name="bias_bwd"), -(QB - 1), axis=1)
        small_g["rel_table"][l] = s["bias_vjp"](ddiag)[0]
        dog, dz, dgn = _gdn_post_bwd(s["og"], s["proj"], gng[l], dyb, name="gdn_post_bwd")
        small_g["gdn_norm_g"][l] = dgn[0]
        dug, dss = _gdn_scan_bwd(s["wg"], s["qd"], s["kd"], s["qk"], s["gle"], dog, name="gdn_scan_bwd")
        dqn, dkn, dvn, daux = _gdn_bwd(s["qn"], s["kn"], s["vn"], s["aux"], s["auxt"], s["tinv"], s["u0"], s["wg"],
                                       s["ug"], s["ss"], dss, dug, dog, name="gdn_bwd")
        dqkv, dba, dwc, dpar = _gdn_pre_bwd(s["proj"], wl["wconv"], s["par"], dqn, dkn, dvn, daux, name="gdn_pre_bwd")
        small_g["a_log"][l] = dpar[0, B_HEADS:2 * B_HEADS]
        small_g["dt_bias"][l] = dpar[1, B_HEADS:2 * B_HEADS]
        dproj = jnp.concatenate([dgates, dqa, dka, dva, dqkv, dz, dba], axis=1)
        gw_in = _mm(dproj, s["h1"], ta=True, out_dtype=BF16, name="proj_dw", tm=1152)
        mix_srcs = [rows_slabs(_win_from_mine(gw_in)), cols_slabs(gw_a), cols_slabs(gw_b), rows_slabs(gw_out),
                    cols_slabs(dwc[0:CONV_K])]
        if l > 0:
            sc1 = sc1 + scatter("mix", l, mix_srcs, gw_in)
        if l > 0:
            p = saved[l - 1]
            dx, dt2, st1 = _mm_adaln_bwd(dproj, wl["win"], s["xin"], n1g[l], sc1, sh1, dx, p["t2"], mods[l - 1, 5],
                                         tk=1152, name="proj_dx_adaln1")
        else:
            dx, st1 = _mm_adaln_bwd(dproj, wl["win"], s["xin"], n1g[l], sc1, sh1, dx, tk=1152,
                                    name="proj_dx_adaln1_first")
        small_g["norm1_g"][l], small_g["norm2_g"][l] = st1[0], st2[0]
        dmod_rows[l] = jnp.concatenate([st1[2], st1[1], st2[3], st2[2], st2[1], dgt2])
        if l > 0:
            dgt2 = st1[3]
    grad_x = dx[None]

    small_local = {n: (jnp.stack(vs) if isinstance(vs, list) else vs) for n, vs in small_g.items()}
    small_local["b_ada"] = jnp.stack(dmod_rows)
    (g_small,) = _all_gather([_pack_small(small_local, loss_part, depth)], name="gather_small")
    tok = scatter("mix", 0, mix_srcs, g_small)
    wsm = _pack_small(dict(b_ada=b_ada, norm1_g=norm1_g, norm2_g=norm2_g, rel_table=rel_table, a_log=a_log,
                           dt_bias=dt_bias, gdn_norm_g=gdn_norm_g, final_g=final_g), jnp.zeros((1,), F32) + tok, depth)
    msm = _pack_small(dict(b_ada=m_b_ada, norm1_g=m_norm1_g, norm2_g=m_norm2_g, rel_table=m_rel_table, a_log=m_a_log,
                           dt_bias=m_dt_bias, gdn_norm_g=m_gdn_norm_g, final_g=m_final_g), jnp.zeros((1,), F32), depth)
    vsm = _pack_small(dict(b_ada=v_b_ada, norm1_g=v_norm1_g, norm2_g=v_norm2_g, rel_table=v_rel_table, a_log=v_a_log,
                           dt_bias=v_dt_bias, gdn_norm_g=v_gdn_norm_g, final_g=v_final_g), jnp.ones((1,), F32), depth)
    sm = [_unpack_small(t, depth) for t in _adamw(g_small[:, None], wsm[None], msm[None], vsm[None], name="adamw_small")]
    loss = sm[0][1]

    dmod_all = g_small.reshape(N_DEV, -1)[:, :depth * 6 * D].reshape(N_DEV, depth, 6 * D)
    dmod_mine = lax.dynamic_slice_in_dim(dmod_all, me * (6 * D // N_DEV), 6 * D // N_DEV, axis=2)
    g_ada = jnp.stack([_mm(cond, _pad_rows(dmod_mine[:, l], 16), ta=True, name="ada_dw") for l in range(depth)])

    got, mine = {}, {}
    sel = jnp.broadcast_to(jnp.where(is_me[:, :, 0], 1.0, 0.0), (N_DEV, LANE)).astype(F32)

    def finish(kind, idx, after):
        done, lands[kind] = _exchange_wait(*pending_s[kind][:4], after, scatter=True, slot=0, name=f"scatter_wait_{kind}_0")
        own[kind][0] = [lax.dynamic_index_in_dim(t, me, 0, keepdims=False) for t in done]
        for a, i in enumerate(idx):
            got[i] = lands[kind][a]
            mine[i] = jnp.stack([own[kind][l][a] for l in range(depth)])

    def upd(i, w, m, v, name):
        if i in (0, 4):
            return [tr_(t) for t in _adamw(got[i], tr_(w), tr_(m), tr_(v), mine[i], sel, name=name)]
        return _adamw(got[i], w, m, v, mine[i], sel, name=name)

    finish("ffn", ffn, g_ada)
    res = {
        "w_ada": _adamw(g_ada[None], w_ada, m_w_ada, v_w_ada, name="adamw_w_ada"),
        "w_ffn_in": upd(4, w_ffn_in, m_w_ffn_in, v_w_ffn_in, "adamw_w_ffn_in"),
        "w_ffn_out": upd(5, w_ffn_out, m_w_ffn_out, v_w_ffn_out, "adamw_w_ffn_out"),
    }
    finish("mix", mix, res["w_ffn_out"][0])
    res.update({
        "w_in": upd(0, w_in, m_w_in, v_w_in, "adamw_w_in"),
        "w_conv": upd(6, w_conv, m_w_conv, v_w_conv, "adamw_w_conv"),
        "w_branch_a": upd(1, w_branch_a, m_w_branch_a, v_w_branch_a, "adamw_w_branch_a"),
        "w_branch_b": upd(2, w_branch_b, m_w_branch_b, v_w_branch_b, "adamw_w_branch_b"),
        "w_out": upd(3, w_out, m_w_out, v_w_out, "adamw_w_out"),
    })
    for n, _ in _small_spec(depth):
        res[n] = [sm[i][0][n] for i in range(4)]
    order = ("w_ada", "b_ada", "norm1_g", "norm2_g", "w_in", "rel_table", "w_conv", "a_log", "dt_bias", "gdn_norm_g",
             "w_branch_a", "w_branch_b", "w_out", "w_ffn_in", "w_ffn_out", "final_g")
    return (loss, grad_x, *[res[n][0] for n in order], *[res[n][1] for n in order],
            *[res[n][2] for n in order], *[res[n][3] for n in order])
```

```python
import functools
import math

import jax
import jax.numpy as jnp
from jax import lax
from jax.experimental import pallas as pl
from jax.experimental.pallas import tpu as pltpu

F32 = jnp.float32
BF16 = jnp.bfloat16
HI = lax.Precision.HIGHEST

N_DEV = 8
D = 1024
DEPTH = 4
CH = 64
EPS = 1e-6
A_HEADS, A_DH = 8, 64
A_W = A_HEADS * A_DH
A_PAST = 8
A_MAX_REL = 128
QB = 256
KB = QB + A_PAST * CH
B_HEADS, B_DH = 4, 128
B_W = B_HEADS * B_DH
CONV_K = 4
FF = 2816
IN_DIM = 5640
IN_PAD = 5760
LANE = 128
NEG = -1e30
VMEM_LIMIT = 48 * 1024 * 1024

ADAM_LR, ADAM_B1, ADAM_B2, ADAM_EPS, ADAM_WD, ADAM_STEP = 0.001, 0.9, 0.999, 1e-08, 0.01, 10

OFF_GA, OFF_GB, OFF_QA, OFF_KA, OFF_VA, OFF_QB, OFF_KB, OFF_VB, OFF_ZB, OFF_BA = (
    0, 1024, 2048, 2560, 3072, 3584, 4096, 4608, 5120, 5632)


def _cp(sem=None):
    return pltpu.CompilerParams(dimension_semantics=sem, vmem_limit_bytes=VMEM_LIMIT)


def _tile(n, pref):
    if n <= pref:
        return n
    best = None
    for t in range(LANE, pref + 1, LANE):
        if n % t == 0:
            best = t
    assert best is not None, (n, pref)
    return best


def _sigmoid(x):
    return 1.0 / (1.0 + jnp.exp(-x))


def _silu(x):
    return x * _sigmoid(x)


def _dsilu(x):
    s = _sigmoid(x)
    return s * (1.0 + x * (1.0 - s))


def _dot(a, b, prec=None):
    return jnp.dot(a, b, preferred_element_type=F32, precision=prec)


def _dot_nt(a, b, prec=None):
    return lax.dot_general(a, b, (((1,), (1,)), ((), ())), preferred_element_type=F32, precision=prec)


def _dot_tn(a, b, prec=None):
    return lax.dot_general(a, b, (((0,), (0,)), ((), ())), preferred_element_type=F32, precision=prec)


def _mm(a, b, *, ta=False, tb=False, out_dtype=F32, name, tm=1024, tn=1024, tk=1024):
    halves = a.ndim == 3
    a_rows, a_cols = (a.shape[1], 2 * a.shape[2]) if halves else a.shape
    m, k = (a_cols, a_rows) if ta else (a_rows, a_cols)
    n = b.shape[0] if tb else b.shape[1]
    assert k == (b.shape[1] if tb else b.shape[0]), (a.shape, b.shape, ta, tb)
    tm, tn, tk = _tile(m, tm), _tile(n, tn), _tile(k, tk)
    nk = k // tk
    dn = (((0 if ta else 1,), (1 if tb else 0,)), ((), ()))

    def body(a_ref, b_ref, o_ref, *acc):
        part = lax.dot_general(a_ref[...].astype(BF16), b_ref[...].astype(BF16), dn, preferred_element_type=F32)
        if nk == 1:
            o_ref[...] = part.astype(out_dtype)
            return
        acc_ref, kk = acc[0], pl.program_id(2)

        @pl.when(kk == 0)
        def _():
            acc_ref[...] = part

        @pl.when(kk > 0)
        def _():
            acc_ref[...] += part

        @pl.when(kk == nk - 1)
        def _():
            o_ref[...] = acc_ref[...].astype(out_dtype)

    if halves:
        per = a.shape[2] // (tm if ta else tk)
        a_spec = (pl.BlockSpec((None, tk, tm), lambda i, j, q: (i // per, q, i % per)) if ta else
                  pl.BlockSpec((None, tm, tk), lambda i, j, q: (q // per, i, q % per)))
    else:
        a_spec = pl.BlockSpec((tk, tm), lambda i, j, q: (q, i)) if ta else pl.BlockSpec((tm, tk), lambda i, j, q: (i, q))
    b_spec = pl.BlockSpec((tn, tk), lambda i, j, q: (j, q)) if tb else pl.BlockSpec((tk, tn), lambda i, j, q: (q, j))
    return pl.pallas_call(
        body, grid=(m // tm, n // tn, nk), in_specs=[a_spec, b_spec],
        out_specs=pl.BlockSpec((tm, tn), lambda i, j, q: (i, j)),
        out_shape=jax.ShapeDtypeStruct((m, n), out_dtype),
        scratch_shapes=[pltpu.VMEM((tm, tn), F32)] if nk > 1 else [],
        compiler_params=_cp(("parallel", "parallel", "arbitrary")), name=name)(a, b)


def _rb(tr, width, cb=0):
    return pl.BlockSpec((tr, width), lambda i: (i, cb))


def _whole(shape):
    nd = len(shape)
    return pl.BlockSpec(shape, lambda i: (0,) * nd)


def _colsum(v):
    return jnp.sum(v, axis=0, keepdims=True)


def _adaln_fwd(x, g, sc, sh, t=None, gt=None, *, name, tr=256):
    tt = x.shape[0]
    res = t is not None

    def body(*refs):
        if res:
            x_ref, t_ref, gt_ref, g_ref, sc_ref, sh_ref, xo_ref, h_ref = refs
            xv = x_ref[...] + gt_ref[...] * t_ref[...]
            xo_ref[...] = xv
        else:
            x_ref, g_ref, sc_ref, sh_ref, h_ref = refs
            xv = x_ref[...]
        r = lax.rsqrt(jnp.mean(xv * xv, axis=-1, keepdims=True) + EPS)
        h_ref[...] = ((xv * r * g_ref[...]) * (1.0 + sc_ref[...]) + sh_ref[...]).astype(BF16)

    row, vec = _rb(tr, D), _whole((1, D))
    if res:
        ins, in_specs = (x, t, gt, g, sc, sh), [row, row, vec, vec, vec, vec]
        out_shape = (jax.ShapeDtypeStruct((tt, D), F32), jax.ShapeDtypeStruct((tt, D), BF16))
        out_specs = (row, row)
    else:
        ins, in_specs = (x, g, sc, sh), [row, vec, vec, vec]
        out_shape, out_specs = jax.ShapeDtypeStruct((tt, D), BF16), row
    out = pl.pallas_call(body, grid=(tt // tr,), in_specs=in_specs, out_specs=out_specs, out_shape=out_shape,
                         compiler_params=_cp(("parallel",)), name=name)(*ins)
    return out if res else (x, out)


def _mm_adaln_bwd(a, b, x, g, sc, sh, dx_in, t=None, gt=None, *, name, tk, tm=512):
    tt = x.shape[0]
    res = t is not None
    halves = a.ndim == 3
    k = 2 * a.shape[2] if halves else a.shape[1]
    tm, nk = _tile(tt, tm), k // tk

    def body(*refs):
        if res:
            a_ref, b_ref, x_ref, g_ref, sc_ref, sh_ref, dxi_ref, t_ref, gt_ref, dx_ref, dt_ref, st_ref, acc_ref = refs
        else:
            a_ref, b_ref, x_ref, g_ref, sc_ref, sh_ref, dxi_ref, dx_ref, st_ref, acc_ref = refs
        i, q = pl.program_id(0), pl.program_id(1)
        part = _dot(a_ref[...], b_ref[...])

        @pl.when((i == 0) & (q == 0))
        def _():
            st_ref[...] = jnp.zeros_like(st_ref)

        @pl.when(q == 0)
        def _():
            acc_ref[...] = part

        @pl.when(q > 0)
        def _():
            acc_ref[...] += part

        @pl.when(q == nk - 1)
        def _():
            xv, dh = x_ref[...], acc_ref[...]
            r = lax.rsqrt(jnp.mean(xv * xv, axis=-1, keepdims=True) + EPS)
            nrm = xv * r
            y = nrm * g_ref[...]
            dy = dh * (1.0 + sc_ref[...])
            dn = dy * g_ref[...]
            dx = dxi_ref[...] + r * (dn - nrm * jnp.mean(dn * nrm, axis=-1, keepdims=True))
            dx_ref[...] = dx
            st_ref[0:1, :] += _colsum(dy * nrm)
            st_ref[1:2, :] += _colsum(dh * y)
            st_ref[2:3, :] += _colsum(dh)
            if res:
                dt_ref[...] = (gt_ref[...] * dx).astype(BF16)
                st_ref[3:4, :] += _colsum(dx * t_ref[...])

    if halves:
        per = a.shape[2] // tk
        a_spec = pl.BlockSpec((None, tm, tk), lambda i, q: (q // per, i, q % per))
    else:
        a_spec = pl.BlockSpec((tm, tk), lambda i, q: (i, q))
    row = pl.BlockSpec((tm, D), lambda i, q: (i, 0))
    vec = pl.BlockSpec((1, D), lambda i, q: (0, 0))
    ins = [a, b, x, g, sc, sh, dx_in]
    in_specs = [a_spec, pl.BlockSpec((tk, D), lambda i, q: (q, 0)), row, vec, vec, vec, row]
    out_shape, out_specs = [jax.ShapeDtypeStruct((tt, D), F32)], [row]
    if res:
        ins += [t, gt]
        in_specs += [row, vec]
        out_shape.append(jax.ShapeDtypeStruct((tt, D), BF16))
        out_specs.append(row)
    out_shape.append(jax.ShapeDtypeStruct((8, D), F32))
    out_specs.append(pl.BlockSpec((8, D), lambda i, q: (0, 0)))
    return pl.pallas_call(body, grid=(tt // tm, nk), in_specs=in_specs, out_specs=tuple(out_specs),
                          out_shape=tuple(out_shape), scratch_shapes=[pltpu.VMEM((tm, D), F32)],
                          compiler_params=_cp(("arbitrary", "arbitrary")), name=name)(*ins)


def _loss_head(x, t, gt, fg, tgt, *, name, tr=256):
    tt = x.shape[0]

    def body(x_ref, t_ref, gt_ref, fg_ref, tgt_ref, dx_ref, dt_ref, st_ref):
        @pl.when(pl.program_id(0) == 0)
        def _():
            st_ref[...] = jnp.zeros_like(st_ref)

        tv = t_ref[...]
        xv = x_ref[...] + gt_ref[...] * tv
        r = lax.rsqrt(jnp.mean(xv * xv, axis=-1, keepdims=True) + EPS)
        nrm = xv * r
        err = nrm * fg_ref[...] - tgt_ref[...]
        st_ref[4:5, :] += 0.5 * jnp.sum(jnp.mean(err * err, axis=-1, keepdims=True), axis=0, keepdims=True)
        dy = err * (1.0 / D)
        dn = dy * fg_ref[...]
        dx = r * (dn - nrm * jnp.mean(dn * nrm, axis=-1, keepdims=True))
        dx_ref[...] = dx
        dt_ref[...] = (gt_ref[...] * dx).astype(BF16)
        st_ref[0:1, :] += _colsum(dy * nrm)
        st_ref[3:4, :] += _colsum(dx * tv)

    row, vec = _rb(tr, D), _whole((1, D))
    return pl.pallas_call(
        body, grid=(tt // tr,), in_specs=[row, row, vec, vec, row], out_specs=(row, row, _whole((8, D))),
        out_shape=(jax.ShapeDtypeStruct((tt, D), F32), jax.ShapeDtypeStruct((tt, D), BF16),
                   jax.ShapeDtypeStruct((8, D), F32)),
        compiler_params=_cp(("arbitrary",)), name=name)(x, t, gt, fg, tgt)


def _branch_merge(ya, yb, wa, wb, proj, *, name, tm=512):
    tt = ya.shape[0]
    tm = _tile(tt, tm)

    def body(ya_ref, yb_ref, wa_ref, wb_ref, ga_ref, gb_ref, pa_ref, pb_ref, o_ref):
        pa = _dot(ya_ref[...], wa_ref[...])
        pb = _dot(yb_ref[...], wb_ref[...])
        pa_ref[...] = pa.astype(BF16)
        pb_ref[...] = pb.astype(BF16)
        o_ref[...] = (_sigmoid(ga_ref[...]) * pa + _sigmoid(gb_ref[...]) * pb).astype(BF16)

    row, half, wsp = _rb(tm, D), _rb(tm, A_W), _whole((A_W, D))
    out = jax.ShapeDtypeStruct((tt, D), BF16)
    return pl.pallas_call(body, grid=(tt // tm,), in_specs=[half, half, wsp, wsp, _rb(tm, D, 0), _rb(tm, D, 1)],
                          out_specs=(row, row, row), out_shape=(out, out, out), compiler_params=_cp(("parallel",)),
                          name=name)(ya, yb, wa, wb, proj, proj)


def _out_adaln(a, w, x, gt, g, sc, sh, *, name, tk=None, tm=512):
    tt, k = a.shape
    tm, tk = _tile(tt, tm), tk or k
    nk = k // tk

    def body(a_ref, w_ref, x_ref, gt_ref, g_ref, sc_ref, sh_ref, t_ref, xo_ref, h_ref):
        q = pl.program_id(1)
        part = _dot(a_ref[...], w_ref[...])

        @pl.when(q == 0)
        def _():
            t_ref[...] = part

        @pl.when(q > 0)
        def _():
            t_ref[...] += part

        @pl.when(q == nk - 1)
        def _():
            xv = x_ref[...] + gt_ref[...] * t_ref[...]
            xo_ref[...] = xv
            r = lax.rsqrt(jnp.mean(xv * xv, axis=-1, keepdims=True) + EPS)
            h_ref[...] = ((xv * r * g_ref[...]) * (1.0 + sc_ref[...]) + sh_ref[...]).astype(BF16)

    row = pl.BlockSpec((tm, D), lambda i, q: (i, 0))
    vec = pl.BlockSpec((1, D), lambda i, q: (0, 0))
    f32 = jax.ShapeDtypeStruct((tt, D), F32)
    return pl.pallas_call(
        body, grid=(tt // tm, nk),
        in_specs=[pl.BlockSpec((tm, tk), lambda i, q: (i, q)), pl.BlockSpec((tk, D), lambda i, q: (q, 0)),
                  row, vec, vec, vec, vec],
        out_specs=(row, row, row), out_shape=(f32, f32, jax.ShapeDtypeStruct((tt, D), BF16)),
        compiler_params=_cp(("parallel", "arbitrary")), name=name)(a, w, x, gt, g, sc, sh)


def _out_bwd_merge(dt, wout, proj, pa, pb, *, name, tm=512):
    tt = dt.shape[0]
    tm = _tile(tt, tm)

    def body(dt_ref, w_ref, ga_ref, gb_ref, pa_ref, pb_ref, dg_ref, dpa_ref, dpb_ref):
        dm_v = _dot_nt(dt_ref[...], w_ref[...])
        sa, sb = _sigmoid(ga_ref[...]), _sigmoid(gb_ref[...])
        dpa_ref[...] = (dm_v * sa).astype(BF16)
        dpb_ref[...] = (dm_v * sb).astype(BF16)
        dg_ref[:, 0:D] = (dm_v * pa_ref[...].astype(F32) * sa * (1.0 - sa)).astype(BF16)
        dg_ref[:, D:2 * D] = (dm_v * pb_ref[...].astype(F32) * sb * (1.0 - sb)).astype(BF16)

    row = _rb(tm, D)
    return pl.pallas_call(
        body, grid=(tt // tm,), in_specs=[row, _whole((D, D)), _rb(tm, D, 0), _rb(tm, D, 1), row, row],
        out_specs=(_rb(tm, 2 * D), row, row),
        out_shape=(jax.ShapeDtypeStruct((tt, 2 * D), BF16), jax.ShapeDtypeStruct((tt, D), BF16),
                   jax.ShapeDtypeStruct((tt, D), BF16)),
        compiler_params=_cp(("parallel",)), name=name)(dt, wout, proj, proj, pa, pb)


FTN = FF // 2


def _ffn_in_swiglu(h, wt, *, name, tm=1024):
    tt = h.shape[0]
    tm = _tile(tt, tm)

    def body(h_ref, wg_ref, wu_ref, gu_ref, act_ref):
        hv = h_ref[...]
        g = _dot_nt(hv, wg_ref[...])
        u = _dot_nt(hv, wu_ref[...])
        gu_ref[0] = g.astype(BF16)
        gu_ref[1] = u.astype(BF16)
        act_ref[...] = (_silu(g) * u).astype(BF16)

    nj = FF // FTN
    return pl.pallas_call(
        body, grid=(tt // tm, nj),
        in_specs=[pl.BlockSpec((tm, D), lambda i, j: (i, 0)), pl.BlockSpec((FTN, D), lambda i, j: (j, 0)),
                  pl.BlockSpec((FTN, D), lambda i, j: (j + nj, 0))],
        out_specs=(pl.BlockSpec((2, tm, FTN), lambda i, j: (0, i, j)), pl.BlockSpec((tm, FTN), lambda i, j: (i, j))),
        out_shape=(jax.ShapeDtypeStruct((2, tt, FF), BF16), jax.ShapeDtypeStruct((tt, FF), BF16)),
        compiler_params=_cp(("parallel", "parallel")), name=name)(h, wt, wt)


def _ffn_out_bwd_swiglu(dt, wo, gu, *, name, tm=1024):
    tt = dt.shape[0]
    tm = _tile(tt, tm)

    def body(dt_ref, wo_ref, gu_ref, dgu_ref):
        da = _dot_nt(dt_ref[...], wo_ref[...])
        g, u = gu_ref[0].astype(F32), gu_ref[1].astype(F32)
        dgu_ref[0] = (da * u * _dsilu(g)).astype(BF16)
        dgu_ref[1] = (da * _silu(g)).astype(BF16)

    blk = pl.BlockSpec((2, tm, FTN), lambda i, j: (0, i, j))
    return pl.pallas_call(
        body, grid=(tt // tm, FF // FTN),
        in_specs=[pl.BlockSpec((tm, D), lambda i, j: (i, 0)), pl.BlockSpec((FTN, D), lambda i, j: (j, 0)), blk],
        out_specs=blk, out_shape=jax.ShapeDtypeStruct((2, tt, FF), BF16),
        compiler_params=_cp(("parallel", "parallel")), name=name)(dt, wo, gu)


BIAS_LW = 1152


def _bias_diagonals(table):
    n_far = A_PAST * CH - A_MAX_REL + 1
    far = jnp.broadcast_to(table[:, 2 * A_MAX_REL:], (A_HEADS, n_far))
    mid = jnp.flip(table[:, 1:2 * A_MAX_REL], axis=1)
    near = jnp.broadcast_to(table[:, 0:1], (A_HEADS, KB - n_far - (2 * A_MAX_REL - 1)))
    pos = jnp.concatenate([far, mid, near], axis=1)
    neg = jnp.broadcast_to(table[:, 2 * A_MAX_REL:], (A_HEADS, QB - 1))
    gap = jnp.zeros((A_HEADS, BIAS_LW - KB - (QB - 1)), F32)
    return jnp.concatenate([pos, gap, neg], axis=1)


def _bias_fwd(diag, *, name):
    def body(w_ref, o_ref):
        qc = lax.broadcasted_iota(jnp.int32, (QB, KB), 0) // CH + A_PAST
        col = lax.broadcasted_iota(jnp.int32, (QB, KB), 1)
        inband = (col // CH <= qc) & (col // CH >= qc - A_PAST)
        for h in range(A_HEADS):
            rows = pltpu.roll(jnp.broadcast_to(w_ref[h:h + 1, :], (QB, BIAS_LW)), 0, 1, stride=1, stride_axis=0)
            for var in range(3):
                o_ref[var, h] = jnp.where(inband & (col >= A_PAST * CH - QB * var), rows[:, :KB], NEG)

    return pl.pallas_call(body, out_shape=jax.ShapeDtypeStruct((3, A_HEADS, QB, KB), F32), compiler_params=_cp(),
                          name=name)(diag)


def _bias_bwd(dbias, *, name):
    def body(d_ref, o_ref):
        r = lax.broadcasted_iota(jnp.int32, (QB, QB), 0)
        c = lax.broadcasted_iota(jnp.int32, (QB, QB), 1)
        flip = jnp.where(r + c == QB - 1, 1.0, 0.0).astype(F32)
        for h in range(A_HEADS):
            x = jnp.concatenate([_dot(flip, d_ref[h], HI), jnp.zeros((QB, BIAS_LW - KB), F32)], axis=1)
            o_ref[h:h + 1, :] = jnp.sum(pltpu.roll(x, 0, 1, stride=1, stride_axis=0), axis=0, keepdims=True)

    return pl.pallas_call(body, out_shape=jax.ShapeDtypeStruct((A_HEADS, BIAS_LW), F32), compiler_params=_cp(),
                          name=name)(dbias)


def _kv_pad(proj, *, name, tr=256):
    tt = proj.shape[0]
    npad = A_PAST * CH // tr

    def body(k_ref, v_ref, ko_ref, vo_ref):
        i = pl.program_id(0)

        @pl.when(i < npad)
        def _():
            ko_ref[...] = jnp.zeros_like(ko_ref)
            vo_ref[...] = jnp.zeros_like(vo_ref)

        @pl.when(i >= npad)
        def _():
            ko_ref[...] = k_ref[...].astype(BF16)
            vo_ref[...] = v_ref[...].astype(BF16)

    src = lambda off: pl.BlockSpec((tr, A_W), lambda i: (jnp.maximum(i - npad, 0), off // A_W))
    out = jax.ShapeDtypeStruct((tt + A_PAST * CH, A_W), BF16)
    return pl.pallas_call(body, grid=(tt // tr + npad,), in_specs=[src(OFF_KA), src(OFF_VA)],
                          out_specs=(_rb(tr, A_W), _rb(tr, A_W)), out_shape=(out, out),
                          compiler_params=_cp(("parallel",)), name=name)(proj, proj)


def _attn_fwd(proj, kpad, vpad, bias, *, name):
    tt = proj.shape[0]

    def body(q_ref, k_ref, v_ref, b_ref, o_ref, l_ref):
        q0 = pl.multiple_of(pl.program_id(1) * QB, QB)
        q = q_ref[...] * (A_DH ** -0.5)
        k = k_ref[pl.ds(q0, KB), :]
        v = v_ref[pl.ds(q0, KB), :]
        lane = lax.broadcasted_iota(jnp.int32, (QB, LANE), 1)
        o = jnp.zeros((QB, LANE), F32)
        lse = jnp.zeros((QB, LANE), F32)
        for a in range(2):
            hm = (lane >= A_DH * a) & (lane < A_DH * (a + 1))
            s = _dot_nt(jnp.where(hm, q, 0.0).astype(BF16), k) + b_ref[a]
            m = jnp.max(s, axis=-1, keepdims=True)
            p = jnp.exp(s - m)
            l = jnp.sum(p, axis=-1, keepdims=True)
            o = jnp.where(hm, _dot(p.astype(BF16), v) / l, o)
            lse = jnp.where(hm, m + jnp.log(l), lse)
        o_ref[...] = o.astype(BF16)
        l_ref[...] = lse

    kv = pl.BlockSpec((tt + A_PAST * CH, LANE), lambda h, i: (0, h))
    blk = pl.BlockSpec((QB, LANE), lambda h, i: (i, h))
    return pl.pallas_call(
        body, grid=(A_W // LANE, tt // QB),
        in_specs=[pl.BlockSpec((QB, LANE), lambda h, i: (i, OFF_QA // LANE + h)), kv, kv,
                  pl.BlockSpec((None, 2, QB, KB), lambda h, i: (jnp.minimum(i, 2), h, 0, 0))],
        out_specs=(blk, blk),
        out_shape=(jax.ShapeDtypeStruct((tt, A_W), BF16), jax.ShapeDtypeStruct((tt, A_W), F32)),
        compiler_params=_cp(("parallel", "parallel")), name=name)(proj, kpad, vpad, bias)


def _attn_bwd(proj, kpad, vpad, bias, o, lse, do, *, name):
    tt = proj.shape[0]
    nq = tt // QB

    def body(q_ref, k_ref, v_ref, b_ref, o_ref, l_ref, do_ref, dq_ref, dko_ref, dvo_ref, db_ref, dk_ref, dv_ref):
        @pl.when(pl.program_id(1) == 0)
        def _():
            dk_ref[...] = jnp.zeros_like(dk_ref)
            dv_ref[...] = jnp.zeros_like(dv_ref)
            db_ref[...] = jnp.zeros_like(db_ref)

        q0 = pl.multiple_of(pl.program_id(1) * QB, QB)
        q, do_v, lse = q_ref[...] * (A_DH ** -0.5), do_ref[...], l_ref[...]
        k = k_ref[pl.ds(q0, KB), :]
        v = v_ref[pl.ds(q0, KB), :]
        dsum = do_v * o_ref[...].astype(F32)
        lane = lax.broadcasted_iota(jnp.int32, (QB, LANE), 1)
        dq = jnp.zeros((QB, LANE), F32)
        dk = jnp.zeros((KB, LANE), F32)
        dv = jnp.zeros((KB, LANE), F32)
        for a in range(2):
            hm = (lane >= A_DH * a) & (lane < A_DH * (a + 1))
            qa = jnp.where(hm, q, 0.0).astype(BF16)
            doa = jnp.where(hm, do_v, 0.0).astype(BF16)
            s = _dot_nt(qa, k) + b_ref[a]
            lse_a = jnp.max(jnp.where(hm, lse, NEG), axis=-1, keepdims=True)
            p = jnp.exp(s - lse_a)
            dp = _dot_nt(doa, v)
            dsum_a = jnp.sum(jnp.where(hm, dsum, 0.0), axis=-1, keepdims=True)
            ds = p * (dp - dsum_a)
            db_ref[a] += ds
            dsb = ds.astype(BF16)
            dq = jnp.where(hm, _dot(dsb, k) * (A_DH ** -0.5), dq)
            dk += _dot_tn(dsb, qa)
            dv += _dot_tn(p.astype(BF16), doa)
        dq_ref[...] = dq.astype(BF16)
        dk_ref[pl.ds(q0, KB), :] += dk
        dv_ref[pl.ds(q0, KB), :] += dv

        @pl.when(pl.program_id(1) == nq - 1)
        def _():
            dko_ref[...] = dk_ref[A_PAST * CH:, :].astype(BF16)
            dvo_ref[...] = dv_ref[A_PAST * CH:, :].astype(BF16)

    kv = pl.BlockSpec((tt + A_PAST * CH, LANE), lambda h, i: (0, h))
    blk = pl.BlockSpec((QB, LANE), lambda h, i: (i, h))
    col = pl.BlockSpec((tt, LANE), lambda h, i: (0, h))
    bsp = pl.BlockSpec((2, QB, KB), lambda h, i: (h, 0, 0))
    bias_in = pl.BlockSpec((None, 2, QB, KB), lambda h, i: (jnp.minimum(i, 2), h, 0, 0))
    out = jax.ShapeDtypeStruct((tt, A_W), BF16)
    return pl.pallas_call(
        body, grid=(A_W // LANE, nq),
        in_specs=[pl.BlockSpec((QB, LANE), lambda h, i: (i, OFF_QA // LANE + h)), kv, kv, bias_in, blk, blk, blk],
        out_specs=(blk, col, col, bsp),
        out_shape=(out, out, out, jax.ShapeDtypeStruct((A_HEADS, QB, KB), F32)),
        scratch_shapes=[pltpu.VMEM((tt + A_PAST * CH, LANE), F32), pltpu.VMEM((tt + A_PAST * CH, LANE), F32)],
        compiler_params=_cp(("parallel", "arbitrary")), name=name)(proj, kpad, vpad, bias, o, lse, do)


GTR = 256


def _taps(w_ref, grp):
    return [w_ref[j:j + 1, grp * B_W:(grp + 1) * B_W] for j in range(CONV_K)]


def _shifts(xe, rows):
    return [xe[8:8 + rows]] + [pltpu.roll(xe, s, 0)[8:8 + rows] for s in range(1, CONV_K)]


def _conv(shifts, taps):
    acc = taps[CONV_K - 1] * shifts[0]
    for s in range(1, CONV_K):
        acc = acc + taps[CONV_K - 1 - s] * shifts[s]
    return acc


def _qk_scale(grp):
    return B_DH ** -0.5 if grp == 0 else 1.0


def _act_fwd(c, grp):
    y = _silu(c)
    if grp == 2:
        return y
    parts = []
    for hd in range(B_HEADS):
        yh = y[:, hd * B_DH:(hd + 1) * B_DH]
        parts.append(yh * (lax.rsqrt(jnp.sum(yh * yh, axis=-1, keepdims=True) + EPS) * _qk_scale(grp)))
    return jnp.concatenate(parts, axis=1)


def _act_bwd(c, dy, grp):
    if grp == 2:
        return dy * _dsilu(c)
    y = _silu(c)
    parts = []
    for hd in range(B_HEADS):
        yh = y[:, hd * B_DH:(hd + 1) * B_DH]
        r = lax.rsqrt(jnp.sum(yh * yh, axis=-1, keepdims=True) + EPS)
        dyh = dy[:, hd * B_DH:(hd + 1) * B_DH] * _qk_scale(grp)
        parts.append(r * dyh - yh * (r * r * r) * jnp.sum(dyh * yh, axis=-1, keepdims=True))
    return jnp.concatenate(parts, axis=1) * _dsilu(c)


def _chunk_tri(n, upper=False):
    r = lax.broadcasted_iota(jnp.int32, (n, n), 0)
    c = lax.broadcasted_iota(jnp.int32, (n, n), 1)
    same = (r // CH) == (c // CH)
    return jnp.where(same & ((r <= c) if upper else (r >= c)), 1.0, 0.0).astype(F32)


def _gate_rows(ba, par_ref):
    lane = lax.broadcasted_iota(jnp.int32, ba.shape, 1)
    z = ba + par_ref[1:2, :]
    sp = jnp.maximum(z, 0.0) + jnp.log(1.0 + jnp.exp(-jnp.abs(z)))
    g = -jnp.exp(par_ref[0:1, :]) * sp
    return jnp.where(lane < B_HEADS, _sigmoid(ba), jnp.where(lane < 2 * B_HEADS, g, 0.0)), z


def _prev8(cb):
    return pl.BlockSpec((8, B_W), lambda i: (jnp.maximum(i * (GTR // 8) - 1, 0), cb))


def _next8(cb, nb):
    return pl.BlockSpec((8, B_W), lambda i: (jnp.minimum((i + 1) * (GTR // 8), nb * (GTR // 8) - 1), cb))


def _gdn_pre_fwd(proj, wconv, par, *, name):
    tt = proj.shape[0]

    def body(q_ref, k_ref, v_ref, qh_ref, kh_ref, vh_ref, ba_ref, w_ref, par_ref, qo_ref, ko_ref, vo_ref, aux_ref):
        first = pl.program_id(0) == 0
        for grp, (x_ref, h_ref, o_ref) in enumerate(((q_ref, qh_ref, qo_ref), (k_ref, kh_ref, ko_ref),
                                                     (v_ref, vh_ref, vo_ref))):
            xe = jnp.concatenate([jnp.where(first, 0.0, h_ref[...]), x_ref[...]], axis=0)
            o_ref[...] = _act_fwd(_conv(_shifts(xe, GTR), _taps(w_ref, grp)), grp)
        bg, _ = _gate_rows(ba_ref[...], par_ref)
        lane = lax.broadcasted_iota(jnp.int32, bg.shape, 1)
        aux_ref[...] = jnp.where(lane < B_HEADS, bg, _dot(_chunk_tri(GTR), bg, HI))

    col = lambda off: _rb(GTR, B_W, off // B_W)
    outs = jax.ShapeDtypeStruct((tt, B_W), F32)
    return pl.pallas_call(
        body, grid=(tt // GTR,),
        in_specs=[col(OFF_QB), col(OFF_KB), col(OFF_VB), _prev8(OFF_QB // B_W), _prev8(OFF_KB // B_W),
                  _prev8(OFF_VB // B_W), _rb(GTR, LANE, OFF_BA // LANE), _whole((CONV_K, 3 * B_W)),
                  _whole((8, LANE))],
        out_specs=(_rb(GTR, B_W), _rb(GTR, B_W), _rb(GTR, B_W), _rb(GTR, LANE)),
        out_shape=(outs, outs, outs, jax.ShapeDtypeStruct((tt, LANE), F32)),
        compiler_params=_cp(("parallel",)), name=name)(proj, proj, proj, proj, proj, proj, proj, wconv, par)


def _gdn_pre_bwd(proj, wconv, par, dq, dk, dv, daux, *, name):
    tt = proj.shape[0]
    nb = tt // GTR

    def body(q_ref, k_ref, v_ref, qh_ref, kh_ref, vh_ref, qn_ref, kn_ref, vn_ref, ba_ref, w_ref, par_ref,
             dq_ref, dk_ref, dv_ref, dqn_ref, dkn_ref, dvn_ref, daux_ref, dx_ref, dba_ref, dw_ref, dpar_ref):
        i = pl.program_id(0)
        first, last = i == 0, i == nb - 1

        @pl.when(first)
        def _():
            dw_ref[...] = jnp.zeros_like(dw_ref)
            dpar_ref[...] = jnp.zeros_like(dpar_ref)

        groups = ((q_ref, qh_ref, qn_ref, dq_ref, dqn_ref), (k_ref, kh_ref, kn_ref, dk_ref, dkn_ref),
                  (v_ref, vh_ref, vn_ref, dv_ref, dvn_ref))
        for grp, (x_ref, h_ref, xn_ref, d_ref, dn_ref) in enumerate(groups):
            taps = _taps(w_ref, grp)
            xe = jnp.concatenate([jnp.where(first, 0.0, h_ref[...]), x_ref[...]], axis=0)
            sh = _shifts(xe, GTR)
            dc = _act_bwd(_conv(sh, taps), d_ref[...], grp)
            xe_n = jnp.concatenate([x_ref[GTR - 8:GTR, :], xn_ref[...]], axis=0)
            dcn = _act_bwd(_conv(_shifts(xe_n, 8), taps), dn_ref[...], grp)
            dce = jnp.concatenate([dc, jnp.where(last, 0.0, dcn)], axis=0)
            dx = taps[CONV_K - 1] * dc
            dw_ref[CONV_K - 1:CONV_K, grp * B_W:(grp + 1) * B_W] += _colsum(dc * sh[0])
            for s in range(1, CONV_K):
                dx = dx + taps[CONV_K - 1 - s] * pltpu.roll(dce, GTR + 8 - s, 0)[0:GTR]
                dw_ref[CONV_K - 1 - s:CONV_K - s, grp * B_W:(grp + 1) * B_W] += _colsum(dc * sh[s])
            dx_ref[:, grp * B_W:(grp + 1) * B_W] = dx.astype(BF16)
        ba = ba_ref[...]
        lane = lax.broadcasted_iota(jnp.int32, ba.shape, 1)
        bg, z = _gate_rows(ba, par_ref)
        daux_v = daux_ref[...]
        dg = _dot(_chunk_tri(GTR, upper=True), daux_v, HI)
        dgl = jnp.where((lane >= B_HEADS) & (lane < 2 * B_HEADS), dg, 0.0)
        da = dgl * (-jnp.exp(par_ref[0:1, :])) * _sigmoid(z)
        dbr = jnp.where(lane < B_HEADS, daux_v * bg * (1.0 - bg), 0.0)
        dba_ref[...] = (dbr + da).astype(BF16)
        dpar_ref[0:1, :] += _colsum(dgl * bg)
        dpar_ref[1:2, :] += _colsum(da)

    col = lambda off: _rb(GTR, B_W, off // B_W)
    row, rowl = _rb(GTR, B_W), _rb(GTR, LANE)
    return pl.pallas_call(
        body, grid=(nb,),
        in_specs=[col(OFF_QB), col(OFF_KB), col(OFF_VB),
                  _prev8(OFF_QB // B_W), _prev8(OFF_KB // B_W), _prev8(OFF_VB // B_W),
                  _next8(OFF_QB // B_W, nb), _next8(OFF_KB // B_W, nb), _next8(OFF_VB // B_W, nb),
                  _rb(GTR, LANE, OFF_BA // LANE), _whole((CONV_K, 3 * B_W)), _whole((8, LANE)),
                  row, row, row, _next8(0, nb), _next8(0, nb), _next8(0, nb), rowl],
        out_specs=(_rb(GTR, 3 * B_W), rowl, _whole((8, 3 * B_W)), _whole((8, LANE))),
        out_shape=(jax.ShapeDtypeStruct((tt, 3 * B_W), BF16), jax.ShapeDtypeStruct((tt, LANE), BF16),
                   jax.ShapeDtypeStruct((8, 3 * B_W), F32), jax.ShapeDtypeStruct((8, LANE), F32)),
        compiler_params=_cp(("arbitrary",)), name=name)(
            proj, proj, proj, proj, proj, proj, proj, proj, proj, proj, wconv, par, dq, dk, dv, dq, dk, dv, daux)


def _col(x, j):
    lane = lax.broadcasted_iota(jnp.int32, x.shape, 1)
    return jnp.sum(jnp.where(lane == j, x, 0.0), axis=-1, keepdims=True)


def _split(x):
    hi = x.astype(BF16)
    return hi, (x - hi.astype(F32)).astype(BF16)


def _dot3(a, b, tn=False):
    dot = _dot_tn if tn else _dot
    (ah, al), (bh, bl) = _split(a), _split(b)
    return dot(ah, bh) + (dot(ah, bl) + dot(al, bh))


def _chunk_masks():
    r = lax.broadcasted_iota(jnp.int32, (CH, CH), 0)
    c = lax.broadcasted_iota(jnp.int32, (CH, CH), 1)
    return r > c, r >= c


def _gc_rows(aux, nc):
    t = jnp.transpose(aux[:, B_HEADS:2 * B_HEADS].reshape(nc, CH, B_HEADS), (0, 2, 1))
    return jnp.concatenate([t, jnp.zeros_like(t)], axis=1).reshape(nc * 8, CH)


_CHUNK8 = lambda width, n=1: pl.BlockSpec((8 * n, width), lambda i: (i, 0))
_CHUNK4 = lambda a, b, n=1: pl.BlockSpec((B_HEADS * n, a, b), lambda i: (i, 0, 0))
NCH = 2


def _per_chunk(body, rows):
    def wrapped(*refs):
        for ci in range(NCH):
            body(*[r.at[pl.ds(ci * n, n)] for r, n in zip(refs, rows)])
    return wrapped


def _gdn_lower(k, aux, auxt, *, name):
    tt = k.shape[0]

    def body(k_ref, aux_ref, auxt_ref, l_ref):
        aux_v = aux_ref[...]
        strict, _ = _chunk_masks()
        khs = [k_ref[:, hd * B_DH:(hd + 1) * B_DH].astype(BF16) for hd in range(B_HEADS)]
        kks = [_dot_nt(kh, kh) for kh in khs]
        for hd in range(B_HEADS):
            diff = _col(aux_v, B_HEADS + hd) - auxt_ref[hd:hd + 1, :]
            dec = jnp.exp(jnp.where(strict, diff, NEG))
            l_ref[hd] = _col(aux_v, hd) * kks[hd] * dec

    return pl.pallas_call(
        _per_chunk(body, (CH, CH, 8, B_HEADS)), grid=(tt // CH // NCH,),
        in_specs=[_rb(NCH * CH, B_W), _rb(NCH * CH, LANE), _CHUNK8(CH, NCH)],
        out_specs=_CHUNK4(CH, CH, NCH),
        out_shape=jax.ShapeDtypeStruct((tt // CH * B_HEADS, CH, CH), F32),
        compiler_params=_cp(("parallel",)), name=name)(k, aux, auxt)


def _tri_inverse(lt, *, name):
    nb = lt.shape[2]

    def body(l_ref, t_ref):
        rowid = lax.broadcasted_iota(jnp.int32, (CH, nb), 0)

        def outer(i, carry):
            def inner(j, acc):
                return acc + l_ref[i, pl.ds(j, 1), :] * t_ref[j]

            acc = lax.fori_loop(0, i, inner, jnp.zeros((CH, nb), F32))
            t_ref[i] = jnp.where(rowid == i, 1.0, 0.0) - acc
            return carry

        lax.fori_loop(0, CH, outer, 0)

    return pl.pallas_call(body, out_shape=jax.ShapeDtypeStruct(lt.shape, F32),
                          in_specs=[pl.BlockSpec(memory_space=pltpu.VMEM)],
                          out_specs=pl.BlockSpec(memory_space=pltpu.VMEM),
                          compiler_params=_cp(), name=name)(lt)


def _gdn_gates(aux_v, aux_last, auxt_ref, hd):
    _, incl = _chunk_masks()
    beta = _col(aux_v, hd)
    gc = _col(aux_v, B_HEADS + hd)
    gl = _col(aux_last, B_HEADS + hd)
    dec = jnp.exp(jnp.where(incl, gc - auxt_ref[hd:hd + 1, :], NEG))
    return beta, gc, gl, jnp.exp(gc), dec


def _gdn_intra(q, k, v, aux, auxt, tinv, *, name):
    tt = q.shape[0]
    nc = tt // CH

    def body(q_ref, k_ref, v_ref, aux_ref, auxt_ref, t_ref, u0_ref, w_ref, qd_ref, kd_ref, qk_ref, gle_ref):
        aux_v = aux_ref[...]
        aux_last = aux_ref[CH - 1:CH, :]
        lane8 = lax.broadcasted_iota(jnp.int32, (8, LANE), 1)
        gle = jnp.zeros((8, LANE), F32)
        heads = range(B_HEADS)
        sls = [slice(hd * B_DH, (hd + 1) * B_DH) for hd in heads]
        gates = [_gdn_gates(aux_v, aux_last, auxt_ref, hd) for hd in heads]
        qk0 = [_dot_nt(q_ref[:, sls[hd]].astype(BF16), k_ref[:, sls[hd]].astype(BF16)) for hd in heads]
        u0 = [_dot3(t_ref[hd], v_ref[:, sls[hd]] * gates[hd][0]) for hd in heads]
        wk = [_dot3(t_ref[hd], k_ref[:, sls[hd]] * (gates[hd][0] * gates[hd][3])) for hd in heads]
        for hd in heads:
            sl = sls[hd]
            beta, gc, gl, egc, dec = gates[hd]
            qk_ref[hd] = (qk0[hd] * dec).astype(BF16)
            u0_ref[:, sl] = u0[hd]
            w_ref[:, sl] = wk[hd].astype(BF16)
            qd_ref[:, sl] = (q_ref[:, sl] * egc).astype(BF16)
            kd_ref[:, sl] = (k_ref[:, sl] * jnp.exp(gl - gc)).astype(BF16)
            gle = gle + jnp.where(lane8 == hd, jnp.exp(gl), 0.0)
        gle_ref[...] = gle

    row = _rb(NCH * CH, B_W)
    half = jax.ShapeDtypeStruct((tt, B_W), BF16)
    return pl.pallas_call(
        _per_chunk(body, (CH, CH, CH, CH, 8, B_HEADS, CH, CH, CH, CH, B_HEADS, 8)), grid=(nc // NCH,),
        in_specs=[row, row, row, _rb(NCH * CH, LANE), _CHUNK8(CH, NCH), _CHUNK4(CH, CH, NCH)],
        out_specs=(row, row, row, row, _CHUNK4(CH, CH, NCH), _CHUNK8(LANE, NCH)),
        out_shape=(jax.ShapeDtypeStruct((tt, B_W), F32), half, half, half,
                   jax.ShapeDtypeStruct((nc * B_HEADS, CH, CH), BF16), jax.ShapeDtypeStruct((nc * 8, LANE), F32)),
        compiler_params=_cp(("parallel",)), name=name)(q, k, v, aux, auxt, tinv)


def _gdn_scan_fwd(u0, w, qd, kd, qk, gle, *, name):
    tt = u0.shape[0]
    nc = tt // CH

    def body(u0_ref, w_ref, qd_ref, kd_ref, qk_ref, gle_ref, o_ref, ss_ref, u_ref, s_ref):
        @pl.when(pl.program_id(0) == 0)
        def _():
            s_ref[...] = jnp.zeros_like(s_ref)

        gle = gle_ref[0:1, :]
        heads = range(B_HEADS)
        sls = [slice(hd * B_DH, (hd + 1) * B_DH) for hd in heads]
        st = [s_ref[hd] for hd in heads]
        sb = [t.astype(BF16) for t in st]
        ws = [_dot(w_ref[:, sls[hd]], sb[hd]) for hd in heads]
        qs = [_dot(qd_ref[:, sls[hd]], sb[hd]) for hd in heads]
        ub = [(u0_ref[:, sls[hd]] - ws[hd]).astype(BF16) for hd in heads]
        ku = [_dot_tn(kd_ref[:, sls[hd]], ub[hd]) for hd in heads]
        qu = [_dot(qk_ref[hd], ub[hd]) for hd in heads]
        for hd in heads:
            ss_ref[hd] = st[hd]
            u_ref[:, sls[hd]] = ub[hd]
            o_ref[:, sls[hd]] = qs[hd] + qu[hd]
            s_ref[hd] = st[hd] * _col(gle, hd) + ku[hd]

    row = _rb(CH, B_W)
    return pl.pallas_call(
        body, grid=(nc,), in_specs=[row, row, row, row, _CHUNK4(CH, CH), _CHUNK8(LANE)],
        out_specs=(row, _CHUNK4(B_DH, B_DH), row),
        out_shape=(jax.ShapeDtypeStruct((tt, B_W), F32), jax.ShapeDtypeStruct((nc * B_HEADS, B_DH, B_DH), F32),
                   jax.ShapeDtypeStruct((tt, B_W), BF16)),
        scratch_shapes=[pltpu.VMEM((B_HEADS, B_DH, B_DH), F32)],
        compiler_params=_cp(("arbitrary",)), name=name)(u0, w, qd, kd, qk, gle)


def _gdn_scan_bwd(w, qd, kd, qk, gle, do, *, name):
    tt = w.shape[0]
    nc = tt // CH

    def body(w_ref, qd_ref, kd_ref, qk_ref, gle_ref, do_ref, du_ref, dss_ref, ds_ref):
        @pl.when(pl.program_id(0) == 0)
        def _():
            ds_ref[...] = jnp.zeros_like(ds_ref)

        gle = gle_ref[0:1, :]
        heads = range(B_HEADS)
        sls = [slice(hd * B_DH, (hd + 1) * B_DH) for hd in heads]
        dst = [ds_ref[hd] for hd in heads]
        dob = [do_ref[:, sls[hd]].astype(BF16) for hd in heads]
        kds = [_dot(kd_ref[:, sls[hd]], dst[hd].astype(BF16)) for hd in heads]
        qkd = [_dot_tn(qk_ref[hd], dob[hd]) for hd in heads]
        qdd = [_dot_tn(qd_ref[:, sls[hd]], dob[hd]) for hd in heads]
        du = [qkd[hd] + kds[hd] for hd in heads]
        wdu = [_dot_tn(w_ref[:, sls[hd]], du[hd].astype(BF16)) for hd in heads]
        for hd in heads:
            dss_ref[hd] = dst[hd]
            du_ref[:, sls[hd]] = du[hd]
            ds_ref[hd] = qdd[hd] + _col(gle, hd) * dst[hd] - wdu[hd]

    rev = lambda width: pl.BlockSpec((CH, width), lambda i: (nc - 1 - i, 0))
    rev4 = lambda a, b: pl.BlockSpec((B_HEADS, a, b), lambda i: (nc - 1 - i, 0, 0))
    return pl.pallas_call(
        body, grid=(nc,),
        in_specs=[rev(B_W), rev(B_W), rev(B_W), rev4(CH, CH), pl.BlockSpec((8, LANE), lambda i: (nc - 1 - i, 0)), rev(B_W)],
        out_specs=(rev(B_W), rev4(B_DH, B_DH)),
        out_shape=(jax.ShapeDtypeStruct((tt, B_W), F32), jax.ShapeDtypeStruct((nc * B_HEADS, B_DH, B_DH), F32)),
        scratch_shapes=[pltpu.VMEM((B_HEADS, B_DH, B_DH), F32)],
        compiler_params=_cp(("arbitrary",)), name=name)(w, qd, kd, qk, gle, do)


def _gdn_bwd(q, k, v, aux, auxt, tinv, u0, w, u, ss, dss, du, do, *, name):
    tt = q.shape[0]
    nc = tt // CH

    def body(q_ref, k_ref, v_ref, aux_ref, auxt_ref, t_ref, u0_ref, w_ref, u_ref, ss_ref, dss_ref, du_ref, do_ref,
             dq_ref, dk_ref, dv_ref, daux_ref):
        aux_v = aux_ref[...]
        aux_last = aux_ref[CH - 1:CH, :]
        lane = lax.broadcasted_iota(jnp.int32, (CH, LANE), 1)
        rowi = lax.broadcasted_iota(jnp.int32, (CH, 1), 0)
        strict, incl = _chunk_masks()
        daux = jnp.zeros((CH, LANE), F32)
        heads = range(B_HEADS)
        sls = [slice(hd * B_DH, (hd + 1) * B_DH) for hd in heads]
        gates = [_gdn_gates(aux_v, aux_last, auxt_ref, hd) for hd in heads]
        kbs = [k_ref[:, sl].astype(BF16) for sl in sls]
        qbs = [q_ref[:, sl].astype(BF16) for sl in sls]
        sbs = [ss_ref[hd].astype(BF16) for hd in heads]
        dsbs = [dss_ref[hd].astype(BF16) for hd in heads]
        dobs = [do_ref[:, sl].astype(BF16) for sl in sls]
        kks = [_dot_nt(kbs[hd], kbs[hd]) for hd in heads]
        qk0s = [_dot_nt(qbs[hd], kbs[hd]) for hd in heads]
        dq_decs = [_dot_nt(dobs[hd], sbs[hd]) for hd in heads]
        dqks = [_dot_nt(dobs[hd], u_ref[:, sls[hd]]) for hd in heads]
        dk_decs = [_dot_nt(u_ref[:, sls[hd]], dsbs[hd]) for hd in heads]
        dws = [-_dot_nt(du_ref[:, sls[hd]].astype(BF16), sbs[hd]) for hd in heads]
        drvs = [_dot3(t_ref[hd], du_ref[:, sls[hd]], tn=True) for hd in heads]
        drks = [_dot3(t_ref[hd], dws[hd], tn=True) for hd in heads]
        dls = [-(_dot_nt(drvs[hd].astype(BF16), u0_ref[:, sls[hd]].astype(BF16))
                 + _dot_nt(drks[hd].astype(BF16), w_ref[:, sls[hd]])) for hd in heads]
        ldecs = [jnp.where(strict, dls[hd], 0.0) * gates[hd][4] for hd in heads]
        dqkm = [jnp.where(incl, dqks[hd], 0.0) for hd in heads]
        dkks = [(ldecs[hd] * gates[hd][0]).astype(BF16) for hd in heads]
        dqk0s = [(dqkm[hd] * gates[hd][4]).astype(BF16) for hd in heads]
        ddecs = [ldecs[hd] * gates[hd][0] * kks[hd] + dqkm[hd] * (qk0s[hd] * gates[hd][4]) for hd in heads]
        dq_mm = [_dot(dqk0s[hd], kbs[hd]) for hd in heads]
        dk_mm = [_dot_tn(dqk0s[hd], qbs[hd]) + _dot(dkks[hd], kbs[hd]) + _dot_tn(dkks[hd], kbs[hd]) for hd in heads]
        dcols = [_col_from_rowsum(ddecs[hd]) for hd in heads]
        for hd in heads:
            sl = sls[hd]
            qh, kh, vh = q_ref[:, sl], k_ref[:, sl], v_ref[:, sl]
            beta, gc, gl, egc, dec = gates[hd]
            ekd, eg_last = jnp.exp(gl - gc), jnp.exp(gl)
            kk = kks[hd]
            st, dst = ss_ref[hd], dss_ref[hd]
            dq_dec, dk_dec = dq_decs[hd], dk_decs[hd]
            dgl = jnp.sum(jnp.sum(st * dst, axis=-1, keepdims=True), axis=0, keepdims=True) * eg_last
            drv, drk = drvs[hd], drks[hd]
            dv_ref[:, sl] = drv * beta
            rk = jnp.sum(drk * kh, axis=-1, keepdims=True)
            dbeta = jnp.sum(drv * vh, axis=-1, keepdims=True) + rk * egc
            dgc = rk * beta * egc
            dk = drk * (beta * egc)
            ldec, ddec = ldecs[hd], ddecs[hd]
            dbeta = dbeta + jnp.sum(ldec * kk, axis=-1, keepdims=True)
            dq = dq_mm[hd] + dq_dec * egc
            dk = dk + dk_mm[hd] + dk_dec * ekd
            dgc = dgc + jnp.sum(ddec, axis=-1, keepdims=True) - dcols[hd]
            dgc = dgc + jnp.sum(dq_dec * qh, axis=-1, keepdims=True) * egc
            kd = jnp.sum(dk_dec * kh, axis=-1, keepdims=True) * ekd
            dgc = dgc - kd
            dgc = dgc + jnp.where(rowi == CH - 1, jnp.sum(kd, axis=0, keepdims=True) + dgl, 0.0)
            dq_ref[:, sl] = dq
            dk_ref[:, sl] = dk
            daux = daux + jnp.where(lane == hd, dbeta, 0.0) + jnp.where(lane == B_HEADS + hd, dgc, 0.0)
        daux_ref[...] = daux

    row = _rb(NCH * CH, B_W)
    outs = jax.ShapeDtypeStruct((tt, B_W), F32)
    return pl.pallas_call(
        _per_chunk(body, (CH, CH, CH, CH, 8, B_HEADS, CH, CH, CH, B_HEADS, B_HEADS, CH, CH, CH, CH, CH, CH)),
        grid=(nc // NCH,),
        in_specs=[row, row, row, _rb(NCH * CH, LANE), _CHUNK8(CH, NCH), _CHUNK4(CH, CH, NCH), row, row, row,
                  _CHUNK4(B_DH, B_DH, NCH), _CHUNK4(B_DH, B_DH, NCH), row, row],
        out_specs=(row, row, row, _rb(NCH * CH, LANE)),
        out_shape=(outs, outs, outs, jax.ShapeDtypeStruct((tt, LANE), F32)),
        compiler_params=_cp(("parallel",)), name=name)(q, k, v, aux, auxt, tinv, u0, w, u, ss, dss, du, do)


def _col_from_rowsum(m):
    hi, lo = _split(m)
    ones = jnp.ones((CH, LANE), BF16)
    return (_dot_tn(hi, ones) + _dot_tn(lo, ones))[:, 0:1]


def _gdn_post_fwd(o, proj, gn, *, name, tr=256):
    tt = o.shape[0]

    def body(o_ref, z_ref, g_ref, y_ref):
        for hd in range(B_HEADS):
            sl = slice(hd * B_DH, (hd + 1) * B_DH)
            oh = o_ref[:, sl]
            r = lax.rsqrt(jnp.mean(oh * oh, axis=-1, keepdims=True) + EPS)
            y_ref[:, sl] = (oh * r * g_ref[...] * _silu(z_ref[:, sl])).astype(BF16)

    return pl.pallas_call(body, grid=(tt // tr,), in_specs=[_rb(tr, B_W), _rb(tr, B_W, OFF_ZB // B_W), _whole((1, B_DH))],
                          out_specs=_rb(tr, B_W), out_shape=jax.ShapeDtypeStruct((tt, B_W), BF16),
                          compiler_params=_cp(("parallel",)), name=name)(o, proj, gn)


def _gdn_post_bwd(o, proj, gn, dy, *, name, tr=256):
    tt = o.shape[0]

    def body(o_ref, z_ref, g_ref, dy_ref, do_ref, dz_ref, dg_ref):
        @pl.when(pl.program_id(0) == 0)
        def _():
            dg_ref[...] = jnp.zeros_like(dg_ref)

        g = g_ref[...]
        for hd in range(B_HEADS):
            sl = slice(hd * B_DH, (hd + 1) * B_DH)
            oh, zh, dyh = o_ref[:, sl], z_ref[:, sl], dy_ref[:, sl]
            r = lax.rsqrt(jnp.mean(oh * oh, axis=-1, keepdims=True) + EPS)
            a = oh * r
            s = _silu(zh)
            da = dyh * g * s
            dg_ref[0:1, :] += _colsum(dyh * a * s)
            dz_ref[:, sl] = (dyh * a * g * _dsilu(zh)).astype(BF16)
            do_ref[:, sl] = r * (da - a * jnp.mean(da * a, axis=-1, keepdims=True))

    return pl.pallas_call(
        body, grid=(tt // tr,), in_specs=[_rb(tr, B_W), _rb(tr, B_W, OFF_ZB // B_W), _whole((1, B_DH)), _rb(tr, B_W)],
        out_specs=(_rb(tr, B_W), _rb(tr, B_W), _whole((8, B_DH))),
        out_shape=(jax.ShapeDtypeStruct((tt, B_W), F32), jax.ShapeDtypeStruct((tt, B_W), BF16),
                   jax.ShapeDtypeStruct((8, B_DH), F32)),
        compiler_params=_cp(("arbitrary",)), name=name)(o, proj, gn, dy)


def _adamw(parts, w, m, v, own=None, sel=None, *, name, tr=256):
    npart, nl, r, c = parts.shape
    tr = max([t for t in range(8, min(r, tr) + 1, 8) if r % t == 0], default=r)
    tc = c if tr < r or r <= 256 or c % 256 else 256
    c1, c2 = 1.0 - ADAM_B1 ** ADAM_STEP, 1.0 - ADAM_B2 ** ADAM_STEP

    def body(*refs):
        if own is None:
            p_ref, w_ref, m_ref, v_ref, g_ref, d_ref, mo_ref, vo_ref = refs
            part = lambda i: p_ref[i].astype(F32)
        else:
            p_ref, w_ref, m_ref, v_ref, own_ref, sel_ref, g_ref, d_ref, mo_ref, vo_ref = refs
            part = lambda i: jnp.where(sel_ref[i:i + 1, 0:1] > 0.5, own_ref[...].astype(F32), p_ref[i].astype(F32))
        g = part(0)
        for i in range(1, npart):
            g = g + part(i)
        mn = ADAM_B1 * m_ref[...] + (1.0 - ADAM_B1) * g
        vn = ADAM_B2 * v_ref[...] + (1.0 - ADAM_B2) * (g * g)
        g_ref[...] = g
        mo_ref[...] = mn
        vo_ref[...] = vn
        d_ref[...] = -ADAM_LR * ((mn / c1) / (jnp.sqrt(vn / c2) + ADAM_EPS) + ADAM_WD * w_ref[...])

    row = pl.BlockSpec((None, tr, tc), lambda l, i, j: (l, i, j))
    out = jax.ShapeDtypeStruct((nl, r, c), F32)
    ins, in_specs = [parts, w, m, v], [pl.BlockSpec((npart, None, tr, tc), lambda l, i, j: (0, l, i, j)), row, row, row]
    if own is not None:
        ins += [own, sel]
        in_specs += [row, pl.BlockSpec((N_DEV, LANE), lambda l, i, j: (0, 0))]
    return pl.pallas_call(body, grid=(nl, r // tr, c // tc), in_specs=in_specs, out_specs=(row, row, row, row),
                          out_shape=(out, out, out, out), compiler_params=_cp(("parallel", "parallel", "parallel")),
                          name=name)(*ins)


def _peer(k):
    x, y, c = lax.axis_index("x"), lax.axis_index("y"), lax.axis_index("c")
    return ((1 - x) if k & 4 else x, (1 - y) if k & 2 else y, (1 - c) if k & 1 else c)


def _my_index():
    return 4 * lax.axis_index("x") + 2 * lax.axis_index("y") + lax.axis_index("c")


def _index_of(p):
    return 4 * p[0] + 2 * p[1] + p[2]


def _all_gather(xs, *, name):
    n = len(xs)

    def body(*refs):
        x_refs, o_refs = refs[:n], refs[n:2 * n]
        send, recv, loc = refs[2 * n:]
        me = _my_index()
        copies = []
        for a in range(n):
            cp = pltpu.make_async_copy(x_refs[a], o_refs[a].at[me], loc.at[a])
            cp.start()
            copies.append(cp)
        rdmas = []
        for a in range(n):
            for k in range(1, N_DEV):
                r = pltpu.make_async_remote_copy(
                    src_ref=x_refs[a], dst_ref=o_refs[a].at[me], send_sem=send.at[a, k - 1], recv_sem=recv.at[a, k - 1],
                    device_id=_peer(k), device_id_type=pl.DeviceIdType.MESH)
                r.start()
                rdmas.append(r)
        for a in range(n):
            for k in range(1, N_DEV):
                pltpu.make_async_remote_copy(
                    src_ref=x_refs[a], dst_ref=o_refs[a].at[_index_of(_peer(k))], send_sem=send.at[a, k - 1],
                    recv_sem=recv.at[a, k - 1], device_id=_peer(k), device_id_type=pl.DeviceIdType.MESH).wait_recv()
        for r in rdmas:
            r.wait_send()
        for cp in copies:
            cp.wait()

    any_spec = pl.BlockSpec(memory_space=pl.ANY)
    return pl.pallas_call(
        body, in_specs=[any_spec] * n, out_specs=tuple([any_spec] * n),
        out_shape=tuple(jax.ShapeDtypeStruct((N_DEV,) + x.shape, x.dtype) for x in xs),
        scratch_shapes=[pltpu.SemaphoreType.DMA((n, N_DEV - 1)), pltpu.SemaphoreType.DMA((n, N_DEV - 1)),
                        pltpu.SemaphoreType.DMA((n,))],
        name=name)(*xs)


def _all_gather_two_level(xs, *, name):
    n = len(xs)

    def body(*refs):
        x_refs, o_refs = refs[:n], refs[n:2 * n]
        send, recv, loc = refs[2 * n:]
        x, y, c = lax.axis_index("x"), lax.axis_index("y"), lax.axis_index("c")
        me, sibling = (x, y, c), (x, y, 1 - c)
        chips = [(1 - x, y), (x, 1 - y), (1 - x, 1 - y)]

        def copy(a, k, block, to, src=None):
            dst = o_refs[a].at[_index_of(block)]
            return pltpu.make_async_remote_copy(src_ref=dst if src is None else src, dst_ref=dst, send_sem=send.at[a, k],
                                                recv_sem=recv.at[a, k], device_id=to, device_id_type=pl.DeviceIdType.MESH)

        mine = [pltpu.make_async_copy(x_refs[a], o_refs[a].at[_index_of(me)], loc.at[a]) for a in range(n)]
        first = [copy(a, 0, me, sibling, src=x_refs[a]) for a in range(n)]
        first += [copy(a, 1 + j, me, (*chip, c), src=x_refs[a]) for a in range(n) for j, chip in enumerate(chips)]
        for cp in mine + first:
            cp.start()
        passed = []
        for a in range(n):
            for j, chip in enumerate(chips):
                copy(a, 1 + j, (*chip, c), me).wait_recv()
                passed.append(copy(a, 4 + j, (*chip, c), sibling))
                passed[-1].start()
        for a in range(n):
            copy(a, 0, sibling, me).wait_recv()
            for j, chip in enumerate(chips):
                copy(a, 4 + j, (*chip, 1 - c), me).wait_recv()
        for cp in first + passed:
            cp.wait_send()
        for cp in mine:
            cp.wait()

    any_spec = pl.BlockSpec(memory_space=pl.ANY)
    return pl.pallas_call(
        body, in_specs=[any_spec] * n, out_specs=tuple([any_spec] * n),
        out_shape=tuple(jax.ShapeDtypeStruct((N_DEV,) + t.shape, t.dtype) for t in xs),
        scratch_shapes=[pltpu.SemaphoreType.DMA((n, N_DEV - 1)), pltpu.SemaphoreType.DMA((n, N_DEV - 1)),
                        pltpu.SemaphoreType.DMA((n,))],
        name=name)(*xs)


_HBM = pl.BlockSpec(memory_space=pltpu.HBM)
_SEM = pl.BlockSpec(memory_space=pltpu.SEMAPHORE)
_EFFECT = pltpu.SideEffectType.DATAFLOW_SIDE_EFFECTING


def _split_copy(src_ref, land_ref, send, recv, a, k, scatter, slot, sending):
    me, peer = _my_index(), _index_of(_peer(k))
    src = src_ref.at[peer if sending else me] if scatter else src_ref
    land = land_ref.at[me if sending else peer]
    if slot is not None:
        land = land.at[slot]
    sem = a * (N_DEV - 1) + k - 1
    return pltpu.make_async_remote_copy(src_ref=src, dst_ref=land, send_sem=send.at[sem], recv_sem=recv.at[sem],
                                        device_id=_peer(k), device_id_type=pl.DeviceIdType.MESH)


def _exchange_start(srcs, lands, after, *, scatter, slot=None, name):
    n = len(srcs)

    def body(*refs):
        src_refs, land_refs = refs[:n], refs[n:2 * n]
        send, recv, token = refs[2 * n + 1], refs[2 * n + 2], refs[-1]
        for a in range(n):
            for k in range(1, N_DEV):
                _split_copy(src_refs[a], land_refs[a], send, recv, a, k, scatter, slot, True).start()
        token[...] = jnp.zeros_like(token)

    hbm = lambda t: pltpu.HBM(t.shape, t.dtype)
    sems = pltpu.SemaphoreType.DMA((n * (N_DEV - 1),))
    out = pl.pallas_call(
        body, name=name,
        out_shape=(sems, sems, *[hbm(t) for t in srcs], *[hbm(t) for t in lands], jax.ShapeDtypeStruct((8, LANE), F32)),
        in_specs=[_HBM] * (2 * n) + [pl.BlockSpec(memory_space=pl.ANY)],
        out_specs=(_SEM, _SEM, *[_HBM] * (2 * n), pl.BlockSpec(memory_space=pltpu.VMEM)),
        input_output_aliases={i: 2 + i for i in range(2 * n)},
        compiler_params=pltpu.CompilerParams(has_side_effects=_EFFECT),
    )(*[pltpu.with_memory_space_constraint(t, pltpu.HBM) for t in (*srcs, *lands)], after)
    return out[0], out[1], out[2:2 + n], out[2 + n:2 + 2 * n], out[-1]


def _exchange_wait(send, recv, srcs, lands, after, *, scatter, slot=None, name):
    n = len(srcs)

    def body(*refs):
        src_refs, land_refs = refs[:n], refs[n:2 * n]
        send_ref, recv_ref = refs[2 * n], refs[2 * n + 1]
        for a in range(n):
            for k in range(1, N_DEV):
                _split_copy(src_refs[a], land_refs[a], send_ref, recv_ref, a, k, scatter, slot, True).wait_send()
                _split_copy(src_refs[a], land_refs[a], send_ref, recv_ref, a, k, scatter, slot, False).wait_recv()

    hbm = lambda t: pltpu.HBM(t.shape, t.dtype)
    out = pl.pallas_call(
        body, name=name, out_shape=(*[hbm(t) for t in srcs], *[hbm(t) for t in lands]),
        in_specs=[_HBM] * (2 * n) + [_SEM, _SEM, pl.BlockSpec(memory_space=pl.ANY)],
        out_specs=tuple([_HBM] * (2 * n)), input_output_aliases={i: i for i in range(2 * n)},
        compiler_params=pltpu.CompilerParams(has_side_effects=_EFFECT),
    )(*srcs, *lands, send, recv, after)
    return out[:n], out[n:]


def _win_to_mine(wt):
    pad = jnp.zeros((IN_PAD - IN_DIM,) + wt.shape[1:], wt.dtype)
    return jnp.concatenate([wt[3592:5640], wt[0:3584], wt[3584:3592], pad], axis=0)


def _win_from_mine(gt):
    return jnp.concatenate([gt[2048:5632], gt[5632:5640], gt[0:2048]], axis=0)


def _pad_rows(a, mult=8):
    r = (-a.shape[0]) % mult
    return a if r == 0 else jnp.concatenate([a, jnp.zeros((r,) + a.shape[1:], a.dtype)], axis=0)


def _lanes(vec, start):
    return jnp.zeros((1, LANE), F32).at[0, start:start + vec.shape[0]].set(vec)


def _small_spec(depth):
    return (("b_ada", (depth, 6 * D)), ("norm1_g", (depth, D)), ("norm2_g", (depth, D)),
            ("rel_table", (depth, A_HEADS, 2 * A_MAX_REL + 1)), ("a_log", (depth, B_HEADS)),
            ("dt_bias", (depth, B_HEADS)), ("gdn_norm_g", (depth, B_DH)), ("final_g", (D,)))


def _pack_small(d, extra, depth):
    spec = _small_spec(depth)
    rows = -(-(sum(math.prod(s) for _, s in spec) + 1) // (8 * LANE)) * 8
    flat = jnp.concatenate([d[n].reshape(-1).astype(F32) for n, _ in spec] + [extra.reshape(-1)])
    flat = jnp.concatenate([flat, jnp.zeros((rows * LANE - flat.shape[0],), F32)])
    return flat.reshape(rows, LANE)


def _unpack_small(p, depth):
    flat = p.reshape(-1)
    out, off = {}, 0
    for n, s in _small_spec(depth):
        sz = math.prod(s)
        out[n] = flat[off:off + sz].reshape(s)
        off += sz
    return out, flat[off]


def kernel(x, c, w_ada, b_ada, norm1_g, norm2_g, w_in, rel_table, w_conv, a_log, dt_bias, gdn_norm_g, w_branch_a, w_branch_b, w_out, w_ffn_in, w_ffn_out, final_g, loss_target, m_w_ada, m_b_ada, m_norm1_g, m_norm2_g, m_w_in, m_rel_table, m_w_conv, m_a_log, m_dt_bias, m_gdn_norm_g, m_w_branch_a, m_w_branch_b, m_w_out, m_w_ffn_in, m_w_ffn_out, m_final_g, v_w_ada, v_b_ada, v_norm1_g, v_norm2_g, v_w_in, v_rel_table, v_w_conv, v_a_log, v_dt_bias, v_gdn_norm_g, v_w_branch_a, v_w_branch_b, v_w_out, v_w_ffn_in, v_w_ffn_out, v_final_g):
    tt = x.shape[1]
    x0 = x[0]
    tgt = loss_target[0]
    me = _my_index()
    depth = w_in.shape[0]

    tr_ = lambda t: jnp.transpose(t, (0, 2, 1))
    shards = [tr_(w_in).astype(BF16), w_branch_a.astype(BF16), w_branch_b.astype(BF16), w_out.astype(BF16),
              tr_(w_ffn_in).astype(BF16), w_ffn_out.astype(BF16), w_conv]
    names = ("win", "wa", "wb", "wout", "wfi", "wfo", "wconv")
    early, late, every = (0, 6), (1, 2, 3, 4, 5), tuple(range(7))
    first = _all_gather_two_level([shards[i][0] for i in early] + [_pad_rows(c)], name="gather_first")
    c_all = first[-1][:, 0, :]
    is_me = lax.broadcasted_iota(jnp.int32, (N_DEV, 1, 1), 0) == me

    def unpack(idx, g):
        cols = lambda t: jnp.transpose(t, (1, 0, 2)).reshape(t.shape[1], N_DEV * t.shape[2])
        rows = lambda t: t.reshape(N_DEV * t.shape[1], t.shape[2])
        how = (lambda t: _win_to_mine(rows(t)), cols, cols, rows, rows, rows, cols)
        return {names[i]: how[i](t) for i, t in zip(idx, g)}

    def gather_start(l, idx, after, tag=""):
        srcs = [shards[i][l] for i in idx]
        lands = [lax.empty((N_DEV,) + t.shape, t.dtype) for t in srcs]
        return _exchange_start(srcs, lands, after, scatter=False, name=f"gather_start_{l}{tag}")

    def gather_wait(l, idx, pending, after, tag=""):
        send, recv, srcs, lands, _ = pending
        srcs, lands = _exchange_wait(send, recv, srcs, lands, after, scatter=False, name=f"gather_wait_{l}{tag}")
        return unpack(idx, [jnp.where(is_me, t[None], g) for g, t in zip(lands, srcs)])

    weights = [unpack(early, first[:-1])] + [None] * (depth - 1)
    pending0 = gather_start(0, late, first[-1], "_rest")
    pending = gather_start(1, every, pending0[-1]) if depth > 1 else None
    cond = c_all * (1.0 / (1.0 + jnp.exp(-c_all)))
    cond = _pad_rows(cond, 16)

    mod_cols = jnp.stack([_mm(cond, w_ada[l], name="mod_mm")[:N_DEV] for l in range(depth)])
    (g_mod,) = _all_gather([mod_cols], name="gather_mod")
    mod_all = jnp.transpose(g_mod, (1, 2, 0, 3)).reshape(depth, N_DEV, 6 * D)
    mod = lax.dynamic_index_in_dim(mod_all, me, axis=1, keepdims=False) + b_ada
    mods = mod.reshape(depth, 6, 1, D)

    n1g, n2g = norm1_g.reshape(depth, 1, D), norm2_g.reshape(depth, 1, D)
    gng = gdn_norm_g.reshape(depth, 1, B_DH)
    fg = final_g.reshape(1, D)

    saved = []
    tok = (pending if pending is not None else pending0)[-1][0, 0]
    xin, h1 = _adaln_fwd(x0, n1g[0], mods[0, 1] + tok, mods[0, 0], name="adaln1_first")
    for l in range(depth):
        sh1, sc1, gt1, sh2, sc2, gt2 = (mods[l, i] for i in range(6))
        wl = weights[l]
        proj = _mm(h1, wl["win"], tb=True, name="proj_mm", tn=1152)
        kpad, vpad = _kv_pad(proj, name="kv_pad")
        diag, bias_vjp = jax.vjp(_bias_diagonals, rel_table[l])
        bias = _bias_fwd(diag, name="bias_fwd")
        ya, lse = _attn_fwd(proj, kpad, vpad, bias, name="attn_fwd")
        par = jnp.concatenate([_lanes(a_log[l], B_HEADS), _lanes(dt_bias[l], B_HEADS), jnp.zeros((6, LANE), F32)], axis=0)
        qn, kn, vn, aux = _gdn_pre_fwd(proj, wl["wconv"], par, name="gdn_pre_fwd")
        auxt = _gc_rows(aux, tt // CH)
        lower = _gdn_lower(kn, aux, auxt, name="gdn_lower")
        tinv = jnp.transpose(_tri_inverse(jnp.transpose(lower, (1, 2, 0)), name="gdn_tri_inverse"), (2, 0, 1))
        u0, wg, qd, kd, qk, gle = _gdn_intra(qn, kn, vn, aux, auxt, tinv, name="gdn_intra")
        og, ss, ug = _gdn_scan_fwd(u0, wg, qd, kd, qk, gle, name="gdn_scan_fwd")
        yb = _gdn_post_fwd(og, proj, gng[l], name="gdn_post_fwd")
        if l == 0:
            wl.update(gather_wait(0, late, pending0, yb, "_rest"))
        pa, pb, merged = _branch_merge(ya, yb, wl["wa"], wl["wb"], proj, name="branch_merge")
        t1, x2, h2 = _out_adaln(merged, wl["wout"], xin, gt1, n2g[l], sc2, sh2, name="out_adaln2")
        gu, act = _ffn_in_swiglu(h2, wl["wfi"], name="ffn_in_swiglu")
        saved.append(dict(xin=xin, h1=h1, proj=proj, kpad=kpad, vpad=vpad, bias=bias, bias_vjp=bias_vjp, ya=ya, lse=lse,
                          par=par, qn=qn, kn=kn, vn=vn, aux=aux, auxt=auxt, tinv=tinv, ss=ss, og=og, yb=yb, pa=pa, pb=pb,
                          u0=u0, wg=wg, qd=qd, kd=kd, qk=qk, gle=gle, ug=ug,
                          merged=merged, t1=t1, x2=x2, h2=h2, gu=gu, act=act))
        if l + 1 < depth:
            weights[l + 1] = gather_wait(l + 1, every, pending, act)
            pending = gather_start(l + 2, every, weights[l + 1]["wconv"]) if l + 2 < depth else None
            tok = pending[-1][0, 0] if pending is not None else 0.0
            t2, xin, h1 = _out_adaln(act, wl["wfo"], x2, gt2, n1g[l + 1], mods[l + 1, 1] + tok, mods[l + 1, 0],
                                     tk=FTN, name="ffn_out_adaln1")
        else:
            t2 = _mm(act, wl["wfo"], name="ffn_out_mm", tk=FTN)
        saved[-1]["t2"] = t2

    s = saved[-1]
    dx, dt2, st = _loss_head(s["x2"], s["t2"], mods[depth - 1, 5], fg, tgt, name="loss_head")
    loss_part = st[4, 0]
    small_g = {"final_g": st[0]}
    dmod_rows = [None] * depth
    for n in ("norm1_g", "norm2_g", "rel_table", "a_log", "dt_bias", "gdn_norm_g"):
        small_g[n] = [None] * depth
    dgt2 = st[3]
    cols_slabs = lambda g: jnp.transpose(g.reshape(g.shape[0], N_DEV, g.shape[1] // N_DEV), (1, 0, 2))
    rows_slabs = lambda g: g.reshape(N_DEV, g.shape[0] // N_DEV, g.shape[1])
    mix, ffn = (0, 1, 2, 3, 6), (4, 5)
    lands = {kind: [lax.empty((N_DEV,) + shards[i].shape, shards[i].dtype) for i in idx]
             for kind, idx in (("mix", mix), ("ffn", ffn))}
    own = {kind: [None] * depth for kind in lands}
    pending_s = {kind: None for kind in lands}

    def scatter(kind, l, srcs, after):
        if pending_s[kind] is not None:
            done, lands[kind] = _exchange_wait(*pending_s[kind][:4], after, scatter=True, slot=l + 1,
                                               name=f"scatter_wait_{kind}_{l + 1}")
            own[kind][l + 1] = [lax.dynamic_index_in_dim(t, me, 0, keepdims=False) for t in done]
        pending_s[kind] = _exchange_start(srcs, lands[kind], after, scatter=True, slot=l, name=f"scatter_start_{kind}_{l}")
        return pending_s[kind][-1][0, 0]

    for l in reversed(range(depth)):
        s, wl = saved[l], weights[l]
        sh1, sc1, gt1, sh2, sc2, gt2 = (mods[l, i] for i in range(6))
        gw_fo = _mm(s["act"], dt2, ta=True, out_dtype=BF16, name="ffn_out_dw", tm=1408)
        dgu = _ffn_out_bwd_swiglu(dt2, wl["wfo"], s["gu"], name="ffn_out_bwd_swiglu")
        gw_fi = _mm(dgu, s["h2"], ta=True, out_dtype=BF16, name="ffn_in_dw", tm=1408)
        sc2 = sc2 + scatter("ffn", l, [rows_slabs(gw_fi), rows_slabs(gw_fo)], gw_fi)
        dx, dt1, st2 = _mm_adaln_bwd(dgu, wl["wfi"], s["x2"], n2g[l], sc2, sh2, dx, s["t1"], gt1, tk=FTN,
                                     name="ffn_in_dx_adaln2")
        gw_out = _mm(s["merged"], dt1, ta=True, out_dtype=BF16, name="out_dw")
        dgates, dpa, dpb = _out_bwd_merge(dt1, wl["wout"], s["proj"], s["pa"], s["pb"], name="out_bwd_merge")
        gw_a = _mm(s["ya"], dpa, ta=True, out_dtype=BF16, name="branch_a_dw")
        gw_b = _mm(s["yb"], dpb, ta=True, out_dtype=BF16, name="branch_b_dw")
        dya = _mm(dpa, wl["wa"], tb=True, name="branch_a_dx")
        dyb = _mm(dpb, wl["wb"], tb=True, name="branch_b_dx")
        dqa, dka, dva, dbias = _attn_bwd(s["proj"], s["kpad"], s["vpad"], s["bias"], s["ya"], s["lse"], dya,
                                             name="attn_bwd")
        ddiag = jnp.roll(_bias_bwd(dbias, name="bias_bwd"), -(QB - 1), axis=1)
        small_g["rel_table"][l] = s["bias_vjp"](ddiag)[0]
        dog, dz, dgn = _gdn_post_bwd(s["og"], s["proj"], gng[l], dyb, name="gdn_post_bwd")
        small_g["gdn_norm_g"][l] = dgn[0]
        dug, dss = _gdn_scan_bwd(s["wg"], s["qd"], s["kd"], s["qk"], s["gle"], dog, name="gdn_scan_bwd")
        dqn, dkn, dvn, daux = _gdn_bwd(s["qn"], s["kn"], s["vn"], s["aux"], s["auxt"], s["tinv"], s["u0"], s["wg"],
                                       s["ug"], s["ss"], dss, dug, dog, name="gdn_bwd")
        dqkv, dba, dwc, dpar = _gdn_pre_bwd(s["proj"], wl["wconv"], s["par"], dqn, dkn, dvn, daux, name="gdn_pre_bwd")
        small_g["a_log"][l] = dpar[0, B_HEADS:2 * B_HEADS]
        small_g["dt_bias"][l] = dpar[1, B_HEADS:2 * B_HEADS]
        dproj = jnp.concatenate([dgates, dqa, dka, dva, dqkv, dz, dba], axis=1)
        gw_in = _mm(dproj, s["h1"], ta=True, out_dtype=BF16, name="proj_dw", tm=1152)
        mix_srcs = [rows_slabs(_win_from_mine(gw_in)), cols_slabs(gw_a), cols_slabs(gw_b), rows_slabs(gw_out),
                    cols_slabs(dwc[0:CONV_K])]
        if l > 0:
            sc1 = sc1 + scatter("mix", l, mix_srcs, gw_in)
        if l > 0:
            p = saved[l - 1]
            dx, dt2, st1 = _mm_adaln_bwd(dproj, wl["win"], s["xin"], n1g[l], sc1, sh1, dx, p["t2"], mods[l - 1, 5],
                                         tk=1152, name="proj_dx_adaln1")
        else:
            dx, st1 = _mm_adaln_bwd(dproj, wl["win"], s["xin"], n1g[l], sc1, sh1, dx, tk=1152,
                                    name="proj_dx_adaln1_first")
        small_g["norm1_g"][l], small_g["norm2_g"][l] = st1[0], st2[0]
        dmod_rows[l] = jnp.concatenate([st1[2], st1[1], st2[3], st2[2], st2[1], dgt2])
        if l > 0:
            dgt2 = st1[3]
    grad_x = dx[None]

    small_local = {n: (jnp.stack(vs) if isinstance(vs, list) else vs) for n, vs in small_g.items()}
    small_local["b_ada"] = jnp.stack(dmod_rows)
    (g_small,) = _all_gather([_pack_small(small_local, loss_part, depth)], name="gather_small")
    tok = scatter("mix", 0, mix_srcs, g_small)
    wsm = _pack_small(dict(b_ada=b_ada, norm1_g=norm1_g, norm2_g=norm2_g, rel_table=rel_table, a_log=a_log,
                           dt_bias=dt_bias, gdn_norm_g=gdn_norm_g, final_g=final_g), jnp.zeros((1,), F32) + tok, depth)
    msm = _pack_small(dict(b_ada=m_b_ada, norm1_g=m_norm1_g, norm2_g=m_norm2_g, rel_table=m_rel_table, a_log=m_a_log,
                           dt_bias=m_dt_bias, gdn_norm_g=m_gdn_norm_g, final_g=m_final_g), jnp.zeros((1,), F32), depth)
    vsm = _pack_small(dict(b_ada=v_b_ada, norm1_g=v_norm1_g, norm2_g=v_norm2_g, rel_table=v_rel_table, a_log=v_a_log,
                           dt_bias=v_dt_bias, gdn_norm_g=v_gdn_norm_g, final_g=v_final_g), jnp.ones((1,), F32), depth)
    sm = [_unpack_small(t, depth) for t in _adamw(g_small[:, None], wsm[None], msm[None], vsm[None], name="adamw_small")]
    loss = sm[0][1]

    dmod_all = g_small.reshape(N_DEV, -1)[:, :depth * 6 * D].reshape(N_DEV, depth, 6 * D)
    dmod_mine = lax.dynamic_slice_in_dim(dmod_all, me * (6 * D // N_DEV), 6 * D // N_DEV, axis=2)
    g_ada = jnp.stack([_mm(cond + tok, _pad_rows(dmod_mine[:, l], 16), ta=True, name="ada_dw") for l in range(depth)])

    got, mine = {}, {}
    sel = jnp.broadcast_to(jnp.where(is_me[:, :, 0], 1.0, 0.0), (N_DEV, LANE)).astype(F32) + tok

    def finish(kind, idx, after):
        done, lands[kind] = _exchange_wait(*pending_s[kind][:4], after, scatter=True, slot=0, name=f"scatter_wait_{kind}_0")
        own[kind][0] = [lax.dynamic_index_in_dim(t, me, 0, keepdims=False) for t in done]
        for a, i in enumerate(idx):
            got[i] = lands[kind][a]
            mine[i] = jnp.stack([own[kind][l][a] for l in range(depth)])

    def upd(i, w, m, v, name):
        if i in (0, 4):
            return [tr_(t) for t in _adamw(got[i], tr_(w), tr_(m), tr_(v), mine[i], sel, name=name)]
        return _adamw(got[i], w, m, v, mine[i], sel, name=name)

    finish("ffn", ffn, g_ada)
    res = {
        "w_ada": _adamw(g_ada[None], w_ada, m_w_ada, v_w_ada, name="adamw_w_ada"),
        "w_ffn_in": upd(4, w_ffn_in, m_w_ffn_in, v_w_ffn_in, "adamw_w_ffn_in"),
        "w_ffn_out": upd(5, w_ffn_out, m_w_ffn_out, v_w_ffn_out, "adamw_w_ffn_out"),
    }
    done_first = (res["w_ada"][1][0, 0, 0] + res["w_ffn_in"][1][0, 0, 0] + res["w_ffn_out"][1][0, 0, 0] + sm[1][1])
    finish("mix", mix, jnp.zeros((8, LANE), F32) + done_first)
    res.update({
        "w_in": upd(0, w_in, m_w_in, v_w_in, "adamw_w_in"),
        "w_conv": upd(6, w_conv, m_w_conv, v_w_conv, "adamw_w_conv"),
        "w_branch_a": upd(1, w_branch_a, m_w_branch_a, v_w_branch_a, "adamw_w_branch_a"),
        "w_branch_b": upd(2, w_branch_b, m_w_branch_b, v_w_branch_b, "adamw_w_branch_b"),
        "w_out": upd(3, w_out, m_w_out, v_w_out, "adamw_w_out"),
    })
    for n, _ in _small_spec(depth):
        res[n] = [sm[i][0][n] for i in range(4)]
    order = ("w_ada", "b_ada", "norm1_g", "norm2_g", "w_in", "rel_table", "w_conv", "a_log", "dt_bias", "gdn_norm_g",
             "w_branch_a", "w_branch_b", "w_out", "w_ffn_in", "w_ffn_out", "final_g")
    return (loss, grad_x, *[res[n][0] for n in order], *[res[n][1] for n in order],
            *[res[n][2] for n in order], *[res[n][3] for n in order])
```

```python
import functools
import math

import jax
import jax.numpy as jnp
from jax import lax
from jax.experimental import pallas as pl
from jax.experimental.pallas import tpu as pltpu

F32 = jnp.float32
BF16 = jnp.bfloat16
HI = lax.Precision.HIGHEST

N_DEV = 8
D = 1024
DEPTH = 4
CH = 64
EPS = 1e-6
A_HEADS, A_DH = 8, 64
A_W = A_HEADS * A_DH
A_PAST = 8
A_MAX_REL = 128
QB = 256
KB = QB + A_PAST * CH
B_HEADS, B_DH = 4, 128
B_W = B_HEADS * B_DH
CONV_K = 4
FF = 2816
IN_DIM = 5640
IN_PAD = 5760
LANE = 128
NEG = -1e30
VMEM_LIMIT = 48 * 1024 * 1024

ADAM_LR, ADAM_B1, ADAM_B2, ADAM_EPS, ADAM_WD, ADAM_STEP = 0.001, 0.9, 0.999, 1e-08, 0.01, 10

OFF_GA, OFF_GB, OFF_QA, OFF_KA, OFF_VA, OFF_QB, OFF_KB, OFF_VB, OFF_ZB, OFF_BA = (
    0, 1024, 2048, 2560, 3072, 3584, 4096, 4608, 5120, 5632)


def _cp(sem=None):
    return pltpu.CompilerParams(dimension_semantics=sem, vmem_limit_bytes=VMEM_LIMIT)


def _tile(n, pref):
    if n <= pref:
        return n
    best = None
    for t in range(LANE, pref + 1, LANE):
        if n % t == 0:
            best = t
    assert best is not None, (n, pref)
    return best


def _sigmoid(x):
    return 1.0 / (1.0 + jnp.exp(-x))


def _silu(x):
    return x * _sigmoid(x)


def _dsilu(x):
    s = _sigmoid(x)
    return s * (1.0 + x * (1.0 - s))


def _dot(a, b, prec=None):
    return jnp.dot(a, b, preferred_element_type=F32, precision=prec)


def _dot_nt(a, b, prec=None):
    return lax.dot_general(a, b, (((1,), (1,)), ((), ())), preferred_element_type=F32, precision=prec)


def _dot_tn(a, b, prec=None):
    return lax.dot_general(a, b, (((0,), (0,)), ((), ())), preferred_element_type=F32, precision=prec)


def _mm(a, b, *, ta=False, tb=False, out_dtype=F32, name, tm=1024, tn=1024, tk=1024):
    halves = a.ndim == 3
    a_rows, a_cols = (a.shape[1], 2 * a.shape[2]) if halves else a.shape
    m, k = (a_cols, a_rows) if ta else (a_rows, a_cols)
    n = b.shape[0] if tb else b.shape[1]
    assert k == (b.shape[1] if tb else b.shape[0]), (a.shape, b.shape, ta, tb)
    tm, tn, tk = _tile(m, tm), _tile(n, tn), _tile(k, tk)
    nk = k // tk
    dn = (((0 if ta else 1,), (1 if tb else 0,)), ((), ()))

    def body(a_ref, b_ref, o_ref, *acc):
        part = lax.dot_general(a_ref[...].astype(BF16), b_ref[...].astype(BF16), dn, preferred_element_type=F32)
        if nk == 1:
            o_ref[...] = part.astype(out_dtype)
            return
        acc_ref, kk = acc[0], pl.program_id(2)

        @pl.when(kk == 0)
        def _():
            acc_ref[...] = part

        @pl.when(kk > 0)
        def _():
            acc_ref[...] += part

        @pl.when(kk == nk - 1)
        def _():
            o_ref[...] = acc_ref[...].astype(out_dtype)

    if halves:
        per = a.shape[2] // (tm if ta else tk)
        a_spec = (pl.BlockSpec((None, tk, tm), lambda i, j, q: (i // per, q, i % per)) if ta else
                  pl.BlockSpec((None, tm, tk), lambda i, j, q: (q // per, i, q % per)))
    else:
        a_spec = pl.BlockSpec((tk, tm), lambda i, j, q: (q, i)) if ta else pl.BlockSpec((tm, tk), lambda i, j, q: (i, q))
    b_spec = pl.BlockSpec((tn, tk), lambda i, j, q: (j, q)) if tb else pl.BlockSpec((tk, tn), lambda i, j, q: (q, j))
    return pl.pallas_call(
        body, grid=(m // tm, n // tn, nk), in_specs=[a_spec, b_spec],
        out_specs=pl.BlockSpec((tm, tn), lambda i, j, q: (i, j)),
        out_shape=jax.ShapeDtypeStruct((m, n), out_dtype),
        scratch_shapes=[pltpu.VMEM((tm, tn), F32)] if nk > 1 else [],
        compiler_params=_cp(("parallel", "parallel", "arbitrary")), name=name)(a, b)


def _rb(tr, width, cb=0):
    return pl.BlockSpec((tr, width), lambda i: (i, cb))


def _whole(shape):
    nd = len(shape)
    return pl.BlockSpec(shape, lambda i: (0,) * nd)


def _colsum(v):
    return jnp.sum(v, axis=0, keepdims=True)


def _adaln_fwd(x, g, sc, sh, t=None, gt=None, *, name, tr=256):
    tt = x.shape[0]
    res = t is not None

    def body(*refs):
        if res:
            x_ref, t_ref, gt_ref, g_ref, sc_ref, sh_ref, xo_ref, h_ref = refs
            xv = x_ref[...] + gt_ref[...] * t_ref[...]
            xo_ref[...] = xv
        else:
            x_ref, g_ref, sc_ref, sh_ref, h_ref = refs
            xv = x_ref[...]
        r = lax.rsqrt(jnp.mean(xv * xv, axis=-1, keepdims=True) + EPS)
        h_ref[...] = ((xv * r * g_ref[...]) * (1.0 + sc_ref[...]) + sh_ref[...]).astype(BF16)

    row, vec = _rb(tr, D), _whole((1, D))
    if res:
        ins, in_specs = (x, t, gt, g, sc, sh), [row, row, vec, vec, vec, vec]
        out_shape = (jax.ShapeDtypeStruct((tt, D), F32), jax.ShapeDtypeStruct((tt, D), BF16))
        out_specs = (row, row)
    else:
        ins, in_specs = (x, g, sc, sh), [row, vec, vec, vec]
        out_shape, out_specs = jax.ShapeDtypeStruct((tt, D), BF16), row
    out = pl.pallas_call(body, grid=(tt // tr,), in_specs=in_specs, out_specs=out_specs, out_shape=out_shape,
                         compiler_params=_cp(("parallel",)), name=name)(*ins)
    return out if res else (x, out)


def _mm_adaln_bwd(a, b, x, g, sc, sh, dx_in, t=None, gt=None, *, name, tk, tm=512):
    tt = x.shape[0]
    res = t is not None
    halves = a.ndim == 3
    k = 2 * a.shape[2] if halves else a.shape[1]
    tm, nk = _tile(tt, tm), k // tk

    def body(*refs):
        if res:
            a_ref, b_ref, x_ref, g_ref, sc_ref, sh_ref, dxi_ref, t_ref, gt_ref, dx_ref, dt_ref, st_ref, acc_ref = refs
        else:
            a_ref, b_ref, x_ref, g_ref, sc_ref, sh_ref, dxi_ref, dx_ref, st_ref, acc_ref = refs
        i, q = pl.program_id(0), pl.program_id(1)
        part = _dot(a_ref[...], b_ref[...])

        @pl.when((i == 0) & (q == 0))
        def _():
            st_ref[...] = jnp.zeros_like(st_ref)

        @pl.when(q == 0)
        def _():
            acc_ref[...] = part

        @pl.when(q > 0)
        def _():
            acc_ref[...] += part

        @pl.when(q == nk - 1)
        def _():
            xv, dh = x_ref[...], acc_ref[...]
            r = lax.rsqrt(jnp.mean(xv * xv, axis=-1, keepdims=True) + EPS)
            nrm = xv * r
            y = nrm * g_ref[...]
            dy = dh * (1.0 + sc_ref[...])
            dn = dy * g_ref[...]
            dx = dxi_ref[...] + r * (dn - nrm * jnp.mean(dn * nrm, axis=-1, keepdims=True))
            dx_ref[...] = dx
            st_ref[0:1, :] += _colsum(dy * nrm)
            st_ref[1:2, :] += _colsum(dh * y)
            st_ref[2:3, :] += _colsum(dh)
            if res:
                dt_ref[...] = (gt_ref[...] * dx).astype(BF16)
                st_ref[3:4, :] += _colsum(dx * t_ref[...])

    if halves:
        per = a.shape[2] // tk
        a_spec = pl.BlockSpec((None, tm, tk), lambda i, q: (q // per, i, q % per))
    else:
        a_spec = pl.BlockSpec((tm, tk), lambda i, q: (i, q))
    row = pl.BlockSpec((tm, D), lambda i, q: (i, 0))
    vec = pl.BlockSpec((1, D), lambda i, q: (0, 0))
    ins = [a, b, x, g, sc, sh, dx_in]
    in_specs = [a_spec, pl.BlockSpec((tk, D), lambda i, q: (q, 0)), row, vec, vec, vec, row]
    out_shape, out_specs = [jax.ShapeDtypeStruct((tt, D), F32)], [row]
    if res:
        ins += [t, gt]
        in_specs += [row, vec]
        out_shape.append(jax.ShapeDtypeStruct((tt, D), BF16))
        out_specs.append(row)
    out_shape.append(jax.ShapeDtypeStruct((8, D), F32))
    out_specs.append(pl.BlockSpec((8, D), lambda i, q: (0, 0)))
    return pl.pallas_call(body, grid=(tt // tm, nk), in_specs=in_specs, out_specs=tuple(out_specs),
                          out_shape=tuple(out_shape), scratch_shapes=[pltpu.VMEM((tm, D), F32)],
                          compiler_params=_cp(("arbitrary", "arbitrary")), name=name)(*ins)


def _loss_head(x, t, gt, fg, tgt, *, name, tr=256):
    tt = x.shape[0]

    def body(x_ref, t_ref, gt_ref, fg_ref, tgt_ref, dx_ref, dt_ref, st_ref):
        @pl.when(pl.program_id(0) == 0)
        def _():
            st_ref[...] = jnp.zeros_like(st_ref)

        tv = t_ref[...]
        xv = x_ref[...] + gt_ref[...] * tv
        r = lax.rsqrt(jnp.mean(xv * xv, axis=-1, keepdims=True) + EPS)
        nrm = xv * r
        err = nrm * fg_ref[...] - tgt_ref[...]
        st_ref[4:5, :] += 0.5 * jnp.sum(jnp.mean(err * err, axis=-1, keepdims=True), axis=0, keepdims=True)
        dy = err * (1.0 / D)
        dn = dy * fg_ref[...]
        dx = r * (dn - nrm * jnp.mean(dn * nrm, axis=-1, keepdims=True))
        dx_ref[...] = dx
        dt_ref[...] = (gt_ref[...] * dx).astype(BF16)
        st_ref[0:1, :] += _colsum(dy * nrm)
        st_ref[3:4, :] += _colsum(dx * tv)

    row, vec = _rb(tr, D), _whole((1, D))
    return pl.pallas_call(
        body, grid=(tt // tr,), in_specs=[row, row, vec, vec, row], out_specs=(row, row, _whole((8, D))),
        out_shape=(jax.ShapeDtypeStruct((tt, D), F32), jax.ShapeDtypeStruct((tt, D), BF16),
                   jax.ShapeDtypeStruct((8, D), F32)),
        compiler_params=_cp(("arbitrary",)), name=name)(x, t, gt, fg, tgt)


def _branch_merge(ya, yb, wa, wb, proj, *, name, tm=512):
    tt = ya.shape[0]
    tm = _tile(tt, tm)

    def body(ya_ref, yb_ref, wa_ref, wb_ref, ga_ref, gb_ref, pa_ref, pb_ref, o_ref):
        pa = _dot(ya_ref[...], wa_ref[...])
        pb = _dot(yb_ref[...], wb_ref[...])
        pa_ref[...] = pa.astype(BF16)
        pb_ref[...] = pb.astype(BF16)
        o_ref[...] = (_sigmoid(ga_ref[...]) * pa + _sigmoid(gb_ref[...]) * pb).astype(BF16)

    row, half, wsp = _rb(tm, D), _rb(tm, A_W), _whole((A_W, D))
    out = jax.ShapeDtypeStruct((tt, D), BF16)
    return pl.pallas_call(body, grid=(tt // tm,), in_specs=[half, half, wsp, wsp, _rb(tm, D, 0), _rb(tm, D, 1)],
                          out_specs=(row, row, row), out_shape=(out, out, out), compiler_params=_cp(("parallel",)),
                          name=name)(ya, yb, wa, wb, proj, proj)


def _out_adaln(a, w, x, gt, g, sc, sh, *, name, tk=None, tm=512):
    tt, k = a.shape
    tm, tk = _tile(tt, tm), tk or k
    nk = k // tk

    def body(a_ref, w_ref, x_ref, gt_ref, g_ref, sc_ref, sh_ref, t_ref, xo_ref, h_ref):
        q = pl.program_id(1)
        part = _dot(a_ref[...], w_ref[...])

        @pl.when(q == 0)
        def _():
            t_ref[...] = part

        @pl.when(q > 0)
        def _():
            t_ref[...] += part

        @pl.when(q == nk - 1)
        def _():
            xv = x_ref[...] + gt_ref[...] * t_ref[...]
            xo_ref[...] = xv
            r = lax.rsqrt(jnp.mean(xv * xv, axis=-1, keepdims=True) + EPS)
            h_ref[...] = ((xv * r * g_ref[...]) * (1.0 + sc_ref[...]) + sh_ref[...]).astype(BF16)

    row = pl.BlockSpec((tm, D), lambda i, q: (i, 0))
    vec = pl.BlockSpec((1, D), lambda i, q: (0, 0))
    f32 = jax.ShapeDtypeStruct((tt, D), F32)
    return pl.pallas_call(
        body, grid=(tt // tm, nk),
        in_specs=[pl.BlockSpec((tm, tk), lambda i, q: (i, q)), pl.BlockSpec((tk, D), lambda i, q: (q, 0)),
                  row, vec, vec, vec, vec],
        out_specs=(row, row, row), out_shape=(f32, f32, jax.ShapeDtypeStruct((tt, D), BF16)),
        compiler_params=_cp(("parallel", "arbitrary")), name=name)(a, w, x, gt, g, sc, sh)


def _out_bwd_merge(dt, wout, proj, pa, pb, *, name, tm=512):
    tt = dt.shape[0]
    tm = _tile(tt, tm)

    def body(dt_ref, w_ref, ga_ref, gb_ref, pa_ref, pb_ref, dg_ref, dpa_ref, dpb_ref):
        dm_v = _dot_nt(dt_ref[...], w_ref[...])
        sa, sb = _sigmoid(ga_ref[...]), _sigmoid(gb_ref[...])
        dpa_ref[...] = (dm_v * sa).astype(BF16)
        dpb_ref[...] = (dm_v * sb).astype(BF16)
        dg_ref[:, 0:D] = (dm_v * pa_ref[...].astype(F32) * sa * (1.0 - sa)).astype(BF16)
        dg_ref[:, D:2 * D] = (dm_v * pb_ref[...].astype(F32) * sb * (1.0 - sb)).astype(BF16)

    row = _rb(tm, D)
    return pl.pallas_call(
        body, grid=(tt // tm,), in_specs=[row, _whole((D, D)), _rb(tm, D, 0), _rb(tm, D, 1), row, row],
        out_specs=(_rb(tm, 2 * D), row, row),
        out_shape=(jax.ShapeDtypeStruct((tt, 2 * D), BF16), jax.ShapeDtypeStruct((tt, D), BF16),
                   jax.ShapeDtypeStruct((tt, D), BF16)),
        compiler_params=_cp(("parallel",)), name=name)(dt, wout, proj, proj, pa, pb)


FTN = FF // 2


def _ffn_in_swiglu(h, wt, *, name, tm=1024):
    tt = h.shape[0]
    tm = _tile(tt, tm)

    def body(h_ref, wg_ref, wu_ref, gu_ref, act_ref):
        hv = h_ref[...]
        g = _dot_nt(hv, wg_ref[...])
        u = _dot_nt(hv, wu_ref[...])
        gu_ref[0] = g.astype(BF16)
        gu_ref[1] = u.astype(BF16)
        act_ref[...] = (_silu(g) * u).astype(BF16)

    nj = FF // FTN
    return pl.pallas_call(
        body, grid=(tt // tm, nj),
        in_specs=[pl.BlockSpec((tm, D), lambda i, j: (i, 0)), pl.BlockSpec((FTN, D), lambda i, j: (j, 0)),
                  pl.BlockSpec((FTN, D), lambda i, j: (j + nj, 0))],
        out_specs=(pl.BlockSpec((2, tm, FTN), lambda i, j: (0, i, j)), pl.BlockSpec((tm, FTN), lambda i, j: (i, j))),
        out_shape=(jax.ShapeDtypeStruct((2, tt, FF), BF16), jax.ShapeDtypeStruct((tt, FF), BF16)),
        compiler_params=_cp(("parallel", "parallel")), name=name)(h, wt, wt)


def _ffn_out_bwd_swiglu(dt, wo, gu, *, name, tm=1024):
    tt = dt.shape[0]
    tm = _tile(tt, tm)

    def body(dt_ref, wo_ref, gu_ref, dgu_ref):
        da = _dot_nt(dt_ref[...], wo_ref[...])
        g, u = gu_ref[0].astype(F32), gu_ref[1].astype(F32)
        dgu_ref[0] = (da * u * _dsilu(g)).astype(BF16)
        dgu_ref[1] = (da * _silu(g)).astype(BF16)

    blk = pl.BlockSpec((2, tm, FTN), lambda i, j: (0, i, j))
    return pl.pallas_call(
        body, grid=(tt // tm, FF // FTN),
        in_specs=[pl.BlockSpec((tm, D), lambda i, j: (i, 0)), pl.BlockSpec((FTN, D), lambda i, j: (j, 0)), blk],
        out_specs=blk, out_shape=jax.ShapeDtypeStruct((2, tt, FF), BF16),
        compiler_params=_cp(("parallel", "parallel")), name=name)(dt, wo, gu)


BIAS_LW = 1152


def _bias_diagonals(table):
    n_far = A_PAST * CH - A_MAX_REL + 1
    far = jnp.broadcast_to(table[:, 2 * A_MAX_REL:], (A_HEADS, n_far))
    mid = jnp.flip(table[:, 1:2 * A_MAX_REL], axis=1)
    near = jnp.broadcast_to(table[:, 0:1], (A_HEADS, KB - n_far - (2 * A_MAX_REL - 1)))
    pos = jnp.concatenate([far, mid, near], axis=1)
    neg = jnp.broadcast_to(table[:, 2 * A_MAX_REL:], (A_HEADS, QB - 1))
    gap = jnp.zeros((A_HEADS, BIAS_LW - KB - (QB - 1)), F32)
    return jnp.concatenate([pos, gap, neg], axis=1)


def _bias_fwd(diag, *, name):
    def body(w_ref, o_ref):
        qc = lax.broadcasted_iota(jnp.int32, (QB, KB), 0) // CH + A_PAST
        col = lax.broadcasted_iota(jnp.int32, (QB, KB), 1)
        inband = (col // CH <= qc) & (col // CH >= qc - A_PAST)
        for h in range(A_HEADS):
            rows = pltpu.roll(jnp.broadcast_to(w_ref[h:h + 1, :], (QB, BIAS_LW)), 0, 1, stride=1, stride_axis=0)
            for var in range(3):
                o_ref[var, h] = jnp.where(inband & (col >= A_PAST * CH - QB * var), rows[:, :KB], NEG)

    return pl.pallas_call(body, out_shape=jax.ShapeDtypeStruct((3, A_HEADS, QB, KB), F32), compiler_params=_cp(),
                          name=name)(diag)


def _bias_bwd(dbias, *, name):
    def body(d_ref, o_ref):
        r = lax.broadcasted_iota(jnp.int32, (QB, QB), 0)
        c = lax.broadcasted_iota(jnp.int32, (QB, QB), 1)
        flip = jnp.where(r + c == QB - 1, 1.0, 0.0).astype(F32)
        for h in range(A_HEADS):
            x = jnp.concatenate([_dot(flip, d_ref[h], HI), jnp.zeros((QB, BIAS_LW - KB), F32)], axis=1)
            o_ref[h:h + 1, :] = jnp.sum(pltpu.roll(x, 0, 1, stride=1, stride_axis=0), axis=0, keepdims=True)

    return pl.pallas_call(body, out_shape=jax.ShapeDtypeStruct((A_HEADS, BIAS_LW), F32), compiler_params=_cp(),
                          name=name)(dbias)


def _kv_pad(proj, *, name, tr=256):
    tt = proj.shape[0]
    tr = 2 * tr if tt % (2 * tr) == 0 else tr
    npad = A_PAST * CH // tr

    def body(k_ref, v_ref, ko_ref, vo_ref):
        i = pl.program_id(0)

        @pl.when(i < npad)
        def _():
            ko_ref[...] = jnp.zeros_like(ko_ref)
            vo_ref[...] = jnp.zeros_like(vo_ref)

        @pl.when(i >= npad)
        def _():
            ko_ref[...] = k_ref[...].astype(BF16)
            vo_ref[...] = v_ref[...].astype(BF16)

    src = lambda off: pl.BlockSpec((tr, A_W), lambda i: (jnp.maximum(i - npad, 0), off // A_W))
    out = jax.ShapeDtypeStruct((tt + A_PAST * CH, A_W), BF16)
    return pl.pallas_call(body, grid=(tt // tr + npad,), in_specs=[src(OFF_KA), src(OFF_VA)],
                          out_specs=(_rb(tr, A_W), _rb(tr, A_W)), out_shape=(out, out),
                          compiler_params=_cp(("parallel",)), name=name)(proj, proj)


def _attn_fwd(proj, kpad, vpad, bias, *, name):
    tt = proj.shape[0]

    def body(q_ref, k_ref, v_ref, b_ref, o_ref, l_ref):
        q0 = pl.multiple_of(pl.program_id(1) * QB, QB)
        q = q_ref[...] * (A_DH ** -0.5)
        k = k_ref[pl.ds(q0, KB), :]
        v = v_ref[pl.ds(q0, KB), :]
        lane = lax.broadcasted_iota(jnp.int32, (QB, LANE), 1)
        o = jnp.zeros((QB, LANE), F32)
        lse = jnp.zeros((QB, LANE), F32)
        for a in range(2):
            hm = (lane >= A_DH * a) & (lane < A_DH * (a + 1))
            s = _dot_nt(jnp.where(hm, q, 0.0).astype(BF16), k) + b_ref[a]
            m = jnp.max(s, axis=-1, keepdims=True)
            p = jnp.exp(s - m)
            l = jnp.sum(p, axis=-1, keepdims=True)
            o = jnp.where(hm, _dot(p.astype(BF16), v) / l, o)
            lse = jnp.where(hm, m + jnp.log(l), lse)
        o_ref[...] = o.astype(BF16)
        l_ref[...] = lse

    kv = pl.BlockSpec((tt + A_PAST * CH, LANE), lambda h, i: (0, h))
    blk = pl.BlockSpec((QB, LANE), lambda h, i: (i, h))
    return pl.pallas_call(
        body, grid=(A_W // LANE, tt // QB),
        in_specs=[pl.BlockSpec((QB, LANE), lambda h, i: (i, OFF_QA // LANE + h)), kv, kv,
                  pl.BlockSpec((None, 2, QB, KB), lambda h, i: (jnp.minimum(i, 2), h, 0, 0))],
        out_specs=(blk, blk),
        out_shape=(jax.ShapeDtypeStruct((tt, A_W), BF16), jax.ShapeDtypeStruct((tt, A_W), F32)),
        compiler_params=_cp(("parallel", "parallel")), name=name)(proj, kpad, vpad, bias)


def _attn_bwd(proj, kpad, vpad, bias, o, lse, do, *, name):
    tt = proj.shape[0]
    nq = tt // QB

    def body(q_ref, k_ref, v_ref, b_ref, o_ref, l_ref, do_ref, dq_ref, dko_ref, dvo_ref, db_ref, dk_ref, dv_ref):
        @pl.when(pl.program_id(1) == 0)
        def _():
            dk_ref[...] = jnp.zeros_like(dk_ref)
            dv_ref[...] = jnp.zeros_like(dv_ref)
            db_ref[...] = jnp.zeros_like(db_ref)

        q0 = pl.multiple_of(pl.program_id(1) * QB, QB)
        q, do_v, lse = q_ref[...] * (A_DH ** -0.5), do_ref[...], l_ref[...]
        k = k_ref[pl.ds(q0, KB), :]
        v = v_ref[pl.ds(q0, KB), :]
        dsum = do_v * o_ref[...].astype(F32)
        lane = lax.broadcasted_iota(jnp.int32, (QB, LANE), 1)
        dq = jnp.zeros((QB, LANE), F32)
        dk = jnp.zeros((KB, LANE), F32)
        dv = jnp.zeros((KB, LANE), F32)
        for a in range(2):
            hm = (lane >= A_DH * a) & (lane < A_DH * (a + 1))
            qa = jnp.where(hm, q, 0.0).astype(BF16)
            doa = jnp.where(hm, do_v, 0.0).astype(BF16)
            s = _dot_nt(qa, k) + b_ref[a]
            lse_a = jnp.max(jnp.where(hm, lse, NEG), axis=-1, keepdims=True)
            p = jnp.exp(s - lse_a)
            dp = _dot_nt(doa, v)
            dsum_a = jnp.sum(jnp.where(hm, dsum, 0.0), axis=-1, keepdims=True)
            ds = p * (dp - dsum_a)
            db_ref[a] += ds
            dsb = ds.astype(BF16)
            dq = jnp.where(hm, _dot(dsb, k) * (A_DH ** -0.5), dq)
            dk += _dot_tn(dsb, qa)
            dv += _dot_tn(p.astype(BF16), doa)
        dq_ref[...] = dq.astype(BF16)
        dk_ref[pl.ds(q0, KB), :] += dk
        dv_ref[pl.ds(q0, KB), :] += dv

        @pl.when(pl.program_id(1) == nq - 1)
        def _():
            dko_ref[...] = dk_ref[A_PAST * CH:, :].astype(BF16)
            dvo_ref[...] = dv_ref[A_PAST * CH:, :].astype(BF16)

    kv = pl.BlockSpec((tt + A_PAST * CH, LANE), lambda h, i: (0, h))
    blk = pl.BlockSpec((QB, LANE), lambda h, i: (i, h))
    col = pl.BlockSpec((tt, LANE), lambda h, i: (0, h))
    bsp = pl.BlockSpec((2, QB, KB), lambda h, i: (h, 0, 0))
    bias_in = pl.BlockSpec((None, 2, QB, KB), lambda h, i: (jnp.minimum(i, 2), h, 0, 0))
    out = jax.ShapeDtypeStruct((tt, A_W), BF16)
    return pl.pallas_call(
        body, grid=(A_W // LANE, nq),
        in_specs=[pl.BlockSpec((QB, LANE), lambda h, i: (i, OFF_QA // LANE + h)), kv, kv, bias_in, blk, blk, blk],
        out_specs=(blk, col, col, bsp),
        out_shape=(out, out, out, jax.ShapeDtypeStruct((A_HEADS, QB, KB), F32)),
        scratch_shapes=[pltpu.VMEM((tt + A_PAST * CH, LANE), F32), pltpu.VMEM((tt + A_PAST * CH, LANE), F32)],
        compiler_params=_cp(("parallel", "arbitrary")), name=name)(proj, kpad, vpad, bias, o, lse, do)


GTR = 256


def _taps(w_ref, grp):
    return [w_ref[j:j + 1, grp * B_W:(grp + 1) * B_W] for j in range(CONV_K)]


def _shifts(xe, rows):
    return [xe[8:8 + rows]] + [pltpu.roll(xe, s, 0)[8:8 + rows] for s in range(1, CONV_K)]


def _conv(shifts, taps):
    acc = taps[CONV_K - 1] * shifts[0]
    for s in range(1, CONV_K):
        acc = acc + taps[CONV_K - 1 - s] * shifts[s]
    return acc


def _qk_scale(grp):
    return B_DH ** -0.5 if grp == 0 else 1.0


def _act_fwd(c, grp):
    y = _silu(c)
    if grp == 2:
        return y
    parts = []
    for hd in range(B_HEADS):
        yh = y[:, hd * B_DH:(hd + 1) * B_DH]
        parts.append(yh * (lax.rsqrt(jnp.sum(yh * yh, axis=-1, keepdims=True) + EPS) * _qk_scale(grp)))
    return jnp.concatenate(parts, axis=1)


def _act_bwd(c, dy, grp):
    if grp == 2:
        return dy * _dsilu(c)
    y = _silu(c)
    parts = []
    for hd in range(B_HEADS):
        yh = y[:, hd * B_DH:(hd + 1) * B_DH]
        r = lax.rsqrt(jnp.sum(yh * yh, axis=-1, keepdims=True) + EPS)
        dyh = dy[:, hd * B_DH:(hd + 1) * B_DH] * _qk_scale(grp)
        parts.append(r * dyh - yh * (r * r * r) * jnp.sum(dyh * yh, axis=-1, keepdims=True))
    return jnp.concatenate(parts, axis=1) * _dsilu(c)


def _chunk_tri(n, upper=False):
    r = lax.broadcasted_iota(jnp.int32, (n, n), 0)
    c = lax.broadcasted_iota(jnp.int32, (n, n), 1)
    same = (r // CH) == (c // CH)
    return jnp.where(same & ((r <= c) if upper else (r >= c)), 1.0, 0.0).astype(F32)


def _gate_rows(ba, par_ref):
    lane = lax.broadcasted_iota(jnp.int32, ba.shape, 1)
    z = ba + par_ref[1:2, :]
    sp = jnp.maximum(z, 0.0) + jnp.log(1.0 + jnp.exp(-jnp.abs(z)))
    g = -jnp.exp(par_ref[0:1, :]) * sp
    return jnp.where(lane < B_HEADS, _sigmoid(ba), jnp.where(lane < 2 * B_HEADS, g, 0.0)), z


def _prev8(cb):
    return pl.BlockSpec((8, B_W), lambda i: (jnp.maximum(i * (GTR // 8) - 1, 0), cb))


def _next8(cb, nb):
    return pl.BlockSpec((8, B_W), lambda i: (jnp.minimum((i + 1) * (GTR // 8), nb * (GTR // 8) - 1), cb))


def _gdn_pre_fwd(proj, wconv, par, *, name):
    tt = proj.shape[0]

    def body(q_ref, k_ref, v_ref, qh_ref, kh_ref, vh_ref, ba_ref, w_ref, par_ref, qo_ref, ko_ref, vo_ref, aux_ref):
        first = pl.program_id(0) == 0
        for grp, (x_ref, h_ref, o_ref) in enumerate(((q_ref, qh_ref, qo_ref), (k_ref, kh_ref, ko_ref),
                                                     (v_ref, vh_ref, vo_ref))):
            xe = jnp.concatenate([jnp.where(first, 0.0, h_ref[...]), x_ref[...]], axis=0)
            o_ref[...] = _act_fwd(_conv(_shifts(xe, GTR), _taps(w_ref, grp)), grp)
        bg, _ = _gate_rows(ba_ref[...], par_ref)
        lane = lax.broadcasted_iota(jnp.int32, bg.shape, 1)
        aux_ref[...] = jnp.where(lane < B_HEADS, bg, _dot(_chunk_tri(GTR), bg, HI))

    col = lambda off: _rb(GTR, B_W, off // B_W)
    outs = jax.ShapeDtypeStruct((tt, B_W), F32)
    return pl.pallas_call(
        body, grid=(tt // GTR,),
        in_specs=[col(OFF_QB), col(OFF_KB), col(OFF_VB), _prev8(OFF_QB // B_W), _prev8(OFF_KB // B_W),
                  _prev8(OFF_VB // B_W), _rb(GTR, LANE, OFF_BA // LANE), _whole((CONV_K, 3 * B_W)),
                  _whole((8, LANE))],
        out_specs=(_rb(GTR, B_W), _rb(GTR, B_W), _rb(GTR, B_W), _rb(GTR, LANE)),
        out_shape=(outs, outs, outs, jax.ShapeDtypeStruct((tt, LANE), F32)),
        compiler_params=_cp(("parallel",)), name=name)(proj, proj, proj, proj, proj, proj, proj, wconv, par)


def _gdn_pre_bwd(proj, wconv, par, dq, dk, dv, daux, *, name):
    tt = proj.shape[0]
    nb = tt // GTR

    def body(q_ref, k_ref, v_ref, qh_ref, kh_ref, vh_ref, qn_ref, kn_ref, vn_ref, ba_ref, w_ref, par_ref,
             dq_ref, dk_ref, dv_ref, dqn_ref, dkn_ref, dvn_ref, daux_ref, dx_ref, dba_ref, dw_ref, dpar_ref):
        i = pl.program_id(0)
        first, last = i == 0, i == nb - 1

        @pl.when(first)
        def _():
            dw_ref[...] = jnp.zeros_like(dw_ref)
            dpar_ref[...] = jnp.zeros_like(dpar_ref)

        groups = ((q_ref, qh_ref, qn_ref, dq_ref, dqn_ref), (k_ref, kh_ref, kn_ref, dk_ref, dkn_ref),
                  (v_ref, vh_ref, vn_ref, dv_ref, dvn_ref))
        for grp, (x_ref, h_ref, xn_ref, d_ref, dn_ref) in enumerate(groups):
            taps = _taps(w_ref, grp)
            xe = jnp.concatenate([jnp.where(first, 0.0, h_ref[...]), x_ref[...]], axis=0)
            sh = _shifts(xe, GTR)
            dc = _act_bwd(_conv(sh, taps), d_ref[...], grp)
            xe_n = jnp.concatenate([x_ref[GTR - 8:GTR, :], xn_ref[...]], axis=0)
            dcn = _act_bwd(_conv(_shifts(xe_n, 8), taps), dn_ref[...], grp)
            dce = jnp.concatenate([dc, jnp.where(last, 0.0, dcn)], axis=0)
            dx = taps[CONV_K - 1] * dc
            dw_ref[CONV_K - 1:CONV_K, grp * B_W:(grp + 1) * B_W] += _colsum(dc * sh[0])
            for s in range(1, CONV_K):
                dx = dx + taps[CONV_K - 1 - s] * pltpu.roll(dce, GTR + 8 - s, 0)[0:GTR]
                dw_ref[CONV_K - 1 - s:CONV_K - s, grp * B_W:(grp + 1) * B_W] += _colsum(dc * sh[s])
            dx_ref[:, grp * B_W:(grp + 1) * B_W] = dx.astype(BF16)
        ba = ba_ref[...]
        lane = lax.broadcasted_iota(jnp.int32, ba.shape, 1)
        bg, z = _gate_rows(ba, par_ref)
        daux_v = daux_ref[...]
        dg = _dot(_chunk_tri(GTR, upper=True), daux_v, HI)
        dgl = jnp.where((lane >= B_HEADS) & (lane < 2 * B_HEADS), dg, 0.0)
        da = dgl * (-jnp.exp(par_ref[0:1, :])) * _sigmoid(z)
        dbr = jnp.where(lane < B_HEADS, daux_v * bg * (1.0 - bg), 0.0)
        dba_ref[...] = (dbr + da).astype(BF16)
        dpar_ref[0:1, :] += _colsum(dgl * bg)
        dpar_ref[1:2, :] += _colsum(da)

    col = lambda off: _rb(GTR, B_W, off // B_W)
    row, rowl = _rb(GTR, B_W), _rb(GTR, LANE)
    return pl.pallas_call(
        body, grid=(nb,),
        in_specs=[col(OFF_QB), col(OFF_KB), col(OFF_VB),
                  _prev8(OFF_QB // B_W), _prev8(OFF_KB // B_W), _prev8(OFF_VB // B_W),
                  _next8(OFF_QB // B_W, nb), _next8(OFF_KB // B_W, nb), _next8(OFF_VB // B_W, nb),
                  _rb(GTR, LANE, OFF_BA // LANE), _whole((CONV_K, 3 * B_W)), _whole((8, LANE)),
                  row, row, row, _next8(0, nb), _next8(0, nb), _next8(0, nb), rowl],
        out_specs=(_rb(GTR, 3 * B_W), rowl, _whole((8, 3 * B_W)), _whole((8, LANE))),
        out_shape=(jax.ShapeDtypeStruct((tt, 3 * B_W), BF16), jax.ShapeDtypeStruct((tt, LANE), BF16),
                   jax.ShapeDtypeStruct((8, 3 * B_W), F32), jax.ShapeDtypeStruct((8, LANE), F32)),
        compiler_params=_cp(("arbitrary",)), name=name)(
            proj, proj, proj, proj, proj, proj, proj, proj, proj, proj, wconv, par, dq, dk, dv, dq, dk, dv, daux)


def _col(x, j):
    lane = lax.broadcasted_iota(jnp.int32, x.shape, 1)
    return jnp.sum(jnp.where(lane == j, x, 0.0), axis=-1, keepdims=True)


def _split(x):
    hi = x.astype(BF16)
    return hi, (x - hi.astype(F32)).astype(BF16)


def _dot3(a, b, tn=False):
    dot = _dot_tn if tn else _dot
    (ah, al), (bh, bl) = _split(a), _split(b)
    return dot(ah, bh) + (dot(ah, bl) + dot(al, bh))


def _chunk_masks():
    r = lax.broadcasted_iota(jnp.int32, (CH, CH), 0)
    c = lax.broadcasted_iota(jnp.int32, (CH, CH), 1)
    return r > c, r >= c


def _gc_rows(aux, nc):
    t = jnp.transpose(aux[:, B_HEADS:2 * B_HEADS].reshape(nc, CH, B_HEADS), (0, 2, 1))
    return jnp.concatenate([t, jnp.zeros_like(t)], axis=1).reshape(nc * 8, CH)


_CHUNK8 = lambda width, n=1: pl.BlockSpec((8 * n, width), lambda i: (i, 0))
_CHUNK4 = lambda a, b, n=1: pl.BlockSpec((B_HEADS * n, a, b), lambda i: (i, 0, 0))
NCH = 4


def _per_chunk(body, rows):
    def wrapped(*refs):
        for ci in range(NCH):
            body(*[r.at[pl.ds(ci * n, n)] for r, n in zip(refs, rows)])
    return wrapped


def _gdn_lower(k, aux, auxt, *, name):
    tt = k.shape[0]

    def body(k_ref, aux_ref, auxt_ref, l_ref):
        aux_v = aux_ref[...]
        strict, _ = _chunk_masks()
        khs = [k_ref[:, hd * B_DH:(hd + 1) * B_DH].astype(BF16) for hd in range(B_HEADS)]
        kks = [_dot_nt(kh, kh) for kh in khs]
        for hd in range(B_HEADS):
            diff = _col(aux_v, B_HEADS + hd) - auxt_ref[hd:hd + 1, :]
            dec = jnp.exp(jnp.where(strict, diff, NEG))
            l_ref[hd] = _col(aux_v, hd) * kks[hd] * dec

    return pl.pallas_call(
        _per_chunk(body, (CH, CH, 8, B_HEADS)), grid=(tt // CH // NCH,),
        in_specs=[_rb(NCH * CH, B_W), _rb(NCH * CH, LANE), _CHUNK8(CH, NCH)],
        out_specs=_CHUNK4(CH, CH, NCH),
        out_shape=jax.ShapeDtypeStruct((tt // CH * B_HEADS, CH, CH), F32),
        compiler_params=_cp(("parallel",)), name=name)(k, aux, auxt)


def _tri_inverse(lt, *, name):
    nb = lt.shape[2]

    def body(l_ref, t_ref):
        rowid = lax.broadcasted_iota(jnp.int32, (CH, nb), 0)

        def outer(i, carry):
            def inner(j, acc):
                return acc + l_ref[i, pl.ds(j, 1), :] * t_ref[j]

            acc = lax.fori_loop(0, i, inner, jnp.zeros((CH, nb), F32))
            t_ref[i] = jnp.where(rowid == i, 1.0, 0.0) - acc
            return carry

        lax.fori_loop(0, CH, outer, 0)

    return pl.pallas_call(body, out_shape=jax.ShapeDtypeStruct(lt.shape, F32),
                          in_specs=[pl.BlockSpec(memory_space=pltpu.VMEM)],
                          out_specs=pl.BlockSpec(memory_space=pltpu.VMEM),
                          compiler_params=_cp(), name=name)(lt)


def _gdn_gates(aux_v, aux_last, auxt_ref, hd):
    _, incl = _chunk_masks()
    beta = _col(aux_v, hd)
    gc = _col(aux_v, B_HEADS + hd)
    gl = _col(aux_last, B_HEADS + hd)
    dec = jnp.exp(jnp.where(incl, gc - auxt_ref[hd:hd + 1, :], NEG))
    return beta, gc, gl, jnp.exp(gc), dec


def _gdn_intra(q, k, v, aux, auxt, tinv, *, name):
    tt = q.shape[0]
    nc = tt // CH

    def body(q_ref, k_ref, v_ref, aux_ref, auxt_ref, t_ref, u0_ref, w_ref, qd_ref, kd_ref, qk_ref, gle_ref):
        aux_v = aux_ref[...]
        aux_last = aux_ref[CH - 1:CH, :]
        lane8 = lax.broadcasted_iota(jnp.int32, (8, LANE), 1)
        gle = jnp.zeros((8, LANE), F32)
        heads = range(B_HEADS)
        sls = [slice(hd * B_DH, (hd + 1) * B_DH) for hd in heads]
        gates = [_gdn_gates(aux_v, aux_last, auxt_ref, hd) for hd in heads]
        qk0 = [_dot_nt(q_ref[:, sls[hd]].astype(BF16), k_ref[:, sls[hd]].astype(BF16)) for hd in heads]
        u0 = [_dot3(t_ref[hd], v_ref[:, sls[hd]] * gates[hd][0]) for hd in heads]
        wk = [_dot3(t_ref[hd], k_ref[:, sls[hd]] * (gates[hd][0] * gates[hd][3])) for hd in heads]
        for hd in heads:
            sl = sls[hd]
            beta, gc, gl, egc, dec = gates[hd]
            qk_ref[hd] = (qk0[hd] * dec).astype(BF16)
            u0_ref[:, sl] = u0[hd]
            w_ref[:, sl] = wk[hd].astype(BF16)
            qd_ref[:, sl] = (q_ref[:, sl] * egc).astype(BF16)
            kd_ref[:, sl] = (k_ref[:, sl] * jnp.exp(gl - gc)).astype(BF16)
            gle = gle + jnp.where(lane8 == hd, jnp.exp(gl), 0.0)
        gle_ref[...] = gle

    row = _rb(NCH * CH, B_W)
    half = jax.ShapeDtypeStruct((tt, B_W), BF16)
    return pl.pallas_call(
        _per_chunk(body, (CH, CH, CH, CH, 8, B_HEADS, CH, CH, CH, CH, B_HEADS, 8)), grid=(nc // NCH,),
        in_specs=[row, row, row, _rb(NCH * CH, LANE), _CHUNK8(CH, NCH), _CHUNK4(CH, CH, NCH)],
        out_specs=(row, row, row, row, _CHUNK4(CH, CH, NCH), _CHUNK8(LANE, NCH)),
        out_shape=(jax.ShapeDtypeStruct((tt, B_W), F32), half, half, half,
                   jax.ShapeDtypeStruct((nc * B_HEADS, CH, CH), BF16), jax.ShapeDtypeStruct((nc * 8, LANE), F32)),
        compiler_params=_cp(("parallel",)), name=name)(q, k, v, aux, auxt, tinv)


def _gdn_scan_fwd(u0, w, qd, kd, qk, gle, *, name):
    tt = u0.shape[0]
    nc = tt // CH

    def body(u0_ref, w_ref, qd_ref, kd_ref, qk_ref, gle_ref, o_ref, ss_ref, u_ref, s_ref):
        @pl.when(pl.program_id(0) == 0)
        def _():
            s_ref[...] = jnp.zeros_like(s_ref)

        gle = gle_ref[0:1, :]
        heads = range(B_HEADS)
        sls = [slice(hd * B_DH, (hd + 1) * B_DH) for hd in heads]
        st = [s_ref[hd] for hd in heads]
        sb = [t.astype(BF16) for t in st]
        ws = [_dot(w_ref[:, sls[hd]], sb[hd]) for hd in heads]
        qs = [_dot(qd_ref[:, sls[hd]], sb[hd]) for hd in heads]
        ub = [(u0_ref[:, sls[hd]] - ws[hd]).astype(BF16) for hd in heads]
        ku = [_dot_tn(kd_ref[:, sls[hd]], ub[hd]) for hd in heads]
        qu = [_dot(qk_ref[hd], ub[hd]) for hd in heads]
        for hd in heads:
            ss_ref[hd] = st[hd]
            u_ref[:, sls[hd]] = ub[hd]
            o_ref[:, sls[hd]] = qs[hd] + qu[hd]
            s_ref[hd] = st[hd] * _col(gle, hd) + ku[hd]

    row = _rb(CH, B_W)
    return pl.pallas_call(
        body, grid=(nc,), in_specs=[row, row, row, row, _CHUNK4(CH, CH), _CHUNK8(LANE)],
        out_specs=(row, _CHUNK4(B_DH, B_DH), row),
        out_shape=(jax.ShapeDtypeStruct((tt, B_W), F32), jax.ShapeDtypeStruct((nc * B_HEADS, B_DH, B_DH), F32),
                   jax.ShapeDtypeStruct((tt, B_W), BF16)),
        scratch_shapes=[pltpu.VMEM((B_HEADS, B_DH, B_DH), F32)],
        compiler_params=_cp(("arbitrary",)), name=name)(u0, w, qd, kd, qk, gle)


def _gdn_scan_bwd(w, qd, kd, qk, gle, do, *, name):
    tt = w.shape[0]
    nc = tt // CH

    def body(w_ref, qd_ref, kd_ref, qk_ref, gle_ref, do_ref, du_ref, dss_ref, ds_ref):
        @pl.when(pl.program_id(0) == 0)
        def _():
            ds_ref[...] = jnp.zeros_like(ds_ref)

        gle = gle_ref[0:1, :]
        heads = range(B_HEADS)
        sls = [slice(hd * B_DH, (hd + 1) * B_DH) for hd in heads]
        dst = [ds_ref[hd] for hd in heads]
        dob = [do_ref[:, sls[hd]].astype(BF16) for hd in heads]
        kds = [_dot(kd_ref[:, sls[hd]], dst[hd].astype(BF16)) for hd in heads]
        qkd = [_dot_tn(qk_ref[hd], dob[hd]) for hd in heads]
        qdd = [_dot_tn(qd_ref[:, sls[hd]], dob[hd]) for hd in heads]
        du = [qkd[hd] + kds[hd] for hd in heads]
        wdu = [_dot_tn(w_ref[:, sls[hd]], du[hd].astype(BF16)) for hd in heads]
        for hd in heads:
            dss_ref[hd] = dst[hd]
            du_ref[:, sls[hd]] = du[hd]
            ds_ref[hd] = qdd[hd] + _col(gle, hd) * dst[hd] - wdu[hd]

    rev = lambda width: pl.BlockSpec((CH, width), lambda i: (nc - 1 - i, 0))
    rev4 = lambda a, b: pl.BlockSpec((B_HEADS, a, b), lambda i: (nc - 1 - i, 0, 0))
    return pl.pallas_call(
        body, grid=(nc,),
        in_specs=[rev(B_W), rev(B_W), rev(B_W), rev4(CH, CH), pl.BlockSpec((8, LANE), lambda i: (nc - 1 - i, 0)), rev(B_W)],
        out_specs=(rev(B_W), rev4(B_DH, B_DH)),
        out_shape=(jax.ShapeDtypeStruct((tt, B_W), F32), jax.ShapeDtypeStruct((nc * B_HEADS, B_DH, B_DH), F32)),
        scratch_shapes=[pltpu.VMEM((B_HEADS, B_DH, B_DH), F32)],
        compiler_params=_cp(("arbitrary",)), name=name)(w, qd, kd, qk, gle, do)


def _gdn_bwd(q, k, v, aux, auxt, tinv, u0, w, u, ss, dss, du, do, *, name):
    tt = q.shape[0]
    nc = tt // CH

    def body(q_ref, k_ref, v_ref, aux_ref, auxt_ref, t_ref, u0_ref, w_ref, u_ref, ss_ref, dss_ref, du_ref, do_ref,
             dq_ref, dk_ref, dv_ref, daux_ref):
        aux_v = aux_ref[...]
        aux_last = aux_ref[CH - 1:CH, :]
        lane = lax.broadcasted_iota(jnp.int32, (CH, LANE), 1)
        rowi = lax.broadcasted_iota(jnp.int32, (CH, 1), 0)
        strict, incl = _chunk_masks()
        daux = jnp.zeros((CH, LANE), F32)
        heads = range(B_HEADS)
        sls = [slice(hd * B_DH, (hd + 1) * B_DH) for hd in heads]
        gates = [_gdn_gates(aux_v, aux_last, auxt_ref, hd) for hd in heads]
        kbs = [k_ref[:, sl].astype(BF16) for sl in sls]
        qbs = [q_ref[:, sl].astype(BF16) for sl in sls]
        sbs = [ss_ref[hd].astype(BF16) for hd in heads]
        dsbs = [dss_ref[hd].astype(BF16) for hd in heads]
        dobs = [do_ref[:, sl].astype(BF16) for sl in sls]
        kks = [_dot_nt(kbs[hd], kbs[hd]) for hd in heads]
        qk0s = [_dot_nt(qbs[hd], kbs[hd]) for hd in heads]
        dq_decs = [_dot_nt(dobs[hd], sbs[hd]) for hd in heads]
        dqks = [_dot_nt(dobs[hd], u_ref[:, sls[hd]]) for hd in heads]
        dk_decs = [_dot_nt(u_ref[:, sls[hd]], dsbs[hd]) for hd in heads]
        dws = [-_dot_nt(du_ref[:, sls[hd]].astype(BF16), sbs[hd]) for hd in heads]
        drvs = [_dot3(t_ref[hd], du_ref[:, sls[hd]], tn=True) for hd in heads]
        drks = [_dot3(t_ref[hd], dws[hd], tn=True) for hd in heads]
        dls = [-(_dot_nt(drvs[hd].astype(BF16), u0_ref[:, sls[hd]].astype(BF16))
                 + _dot_nt(drks[hd].astype(BF16), w_ref[:, sls[hd]])) for hd in heads]
        ldecs = [jnp.where(strict, dls[hd], 0.0) * gates[hd][4] for hd in heads]
        dqkm = [jnp.where(incl, dqks[hd], 0.0) for hd in heads]
        dkks = [(ldecs[hd] * gates[hd][0]).astype(BF16) for hd in heads]
        dqk0s = [(dqkm[hd] * gates[hd][4]).astype(BF16) for hd in heads]
        ddecs = [ldecs[hd] * gates[hd][0] * kks[hd] + dqkm[hd] * (qk0s[hd] * gates[hd][4]) for hd in heads]
        dq_mm = [_dot(dqk0s[hd], kbs[hd]) for hd in heads]
        dk_mm = [_dot_tn(dqk0s[hd], qbs[hd]) + _dot(dkks[hd], kbs[hd]) + _dot_tn(dkks[hd], kbs[hd]) for hd in heads]
        dcols = [_col_from_rowsum(ddecs[hd]) for hd in heads]
        for hd in heads:
            sl = sls[hd]
            qh, kh, vh = q_ref[:, sl], k_ref[:, sl], v_ref[:, sl]
            beta, gc, gl, egc, dec = gates[hd]
            ekd, eg_last = jnp.exp(gl - gc), jnp.exp(gl)
            kk = kks[hd]
            st, dst = ss_ref[hd], dss_ref[hd]
            dq_dec, dk_dec = dq_decs[hd], dk_decs[hd]
            dgl = jnp.sum(jnp.sum(st * dst, axis=-1, keepdims=True), axis=0, keepdims=True) * eg_last
            drv, drk = drvs[hd], drks[hd]
            dv_ref[:, sl] = drv * beta
            rk = jnp.sum(drk * kh, axis=-1, keepdims=True)
            dbeta = jnp.sum(drv * vh, axis=-1, keepdims=True) + rk * egc
            dgc = rk * beta * egc
            dk = drk * (beta * egc)
            ldec, ddec = ldecs[hd], ddecs[hd]
            dbeta = dbeta + jnp.sum(ldec * kk, axis=-1, keepdims=True)
            dq = dq_mm[hd] + dq_dec * egc
            dk = dk + dk_mm[hd] + dk_dec * ekd
            dgc = dgc + jnp.sum(ddec, axis=-1, keepdims=True) - dcols[hd]
            dgc = dgc + jnp.sum(dq_dec * qh, axis=-1, keepdims=True) * egc
            kd = jnp.sum(dk_dec * kh, axis=-1, keepdims=True) * ekd
            dgc = dgc - kd
            dgc = dgc + jnp.where(rowi == CH - 1, jnp.sum(kd, axis=0, keepdims=True) + dgl, 0.0)
            dq_ref[:, sl] = dq
            dk_ref[:, sl] = dk
            daux = daux + jnp.where(lane == hd, dbeta, 0.0) + jnp.where(lane == B_HEADS + hd, dgc, 0.0)
        daux_ref[...] = daux

    row = _rb(NCH * CH, B_W)
    outs = jax.ShapeDtypeStruct((tt, B_W), F32)
    return pl.pallas_call(
        _per_chunk(body, (CH, CH, CH, CH, 8, B_HEADS, CH, CH, CH, B_HEADS, B_HEADS, CH, CH, CH, CH, CH, CH)),
        grid=(nc // NCH,),
        in_specs=[row, row, row, _rb(NCH * CH, LANE), _CHUNK8(CH, NCH), _CHUNK4(CH, CH, NCH), row, row, row,
                  _CHUNK4(B_DH, B_DH, NCH), _CHUNK4(B_DH, B_DH, NCH), row, row],
        out_specs=(row, row, row, _rb(NCH * CH, LANE)),
        out_shape=(outs, outs, outs, jax.ShapeDtypeStruct((tt, LANE), F32)),
        compiler_params=_cp(("parallel",)), name=name)(q, k, v, aux, auxt, tinv, u0, w, u, ss, dss, du, do)


def _col_from_rowsum(m):
    hi, lo = _split(m)
    ones = jnp.ones((CH, LANE), BF16)
    return (_dot_tn(hi, ones) + _dot_tn(lo, ones))[:, 0:1]


def _gdn_post_fwd(o, proj, gn, *, name, tr=256):
    tt = o.shape[0]

    def body(o_ref, z_ref, g_ref, y_ref):
        for hd in range(B_HEADS):
            sl = slice(hd * B_DH, (hd + 1) * B_DH)
            oh = o_ref[:, sl]
            r = lax.rsqrt(jnp.mean(oh * oh, axis=-1, keepdims=True) + EPS)
            y_ref[:, sl] = (oh * r * g_ref[...] * _silu(z_ref[:, sl])).astype(BF16)

    return pl.pallas_call(body, grid=(tt // tr,), in_specs=[_rb(tr, B_W), _rb(tr, B_W, OFF_ZB // B_W), _whole((1, B_DH))],
                          out_specs=_rb(tr, B_W), out_shape=jax.ShapeDtypeStruct((tt, B_W), BF16),
                          compiler_params=_cp(("parallel",)), name=name)(o, proj, gn)


def _gdn_post_bwd(o, proj, gn, dy, *, name, tr=256):
    tt = o.shape[0]

    def body(o_ref, z_ref, g_ref, dy_ref, do_ref, dz_ref, dg_ref):
        @pl.when(pl.program_id(0) == 0)
        def _():
            dg_ref[...] = jnp.zeros_like(dg_ref)

        g = g_ref[...]
        for hd in range(B_HEADS):
            sl = slice(hd * B_DH, (hd + 1) * B_DH)
            oh, zh, dyh = o_ref[:, sl], z_ref[:, sl], dy_ref[:, sl]
            r = lax.rsqrt(jnp.mean(oh * oh, axis=-1, keepdims=True) + EPS)
            a = oh * r
            s = _silu(zh)
            da = dyh * g * s
            dg_ref[0:1, :] += _colsum(dyh * a * s)
            dz_ref[:, sl] = (dyh * a * g * _dsilu(zh)).astype(BF16)
            do_ref[:, sl] = r * (da - a * jnp.mean(da * a, axis=-1, keepdims=True))

    return pl.pallas_call(
        body, grid=(tt // tr,), in_specs=[_rb(tr, B_W), _rb(tr, B_W, OFF_ZB // B_W), _whole((1, B_DH)), _rb(tr, B_W)],
        out_specs=(_rb(tr, B_W), _rb(tr, B_W), _whole((8, B_DH))),
        out_shape=(jax.ShapeDtypeStruct((tt, B_W), F32), jax.ShapeDtypeStruct((tt, B_W), BF16),
                   jax.ShapeDtypeStruct((8, B_DH), F32)),
        compiler_params=_cp(("arbitrary",)), name=name)(o, proj, gn, dy)


def _adamw(parts, w, m, v, own=None, sel=None, *, name, tr=256):
    npart, nl, r, c = parts.shape
    tr = max([t for t in range(8, min(r, tr) + 1, 8) if r % t == 0], default=r)
    tc = c if tr < r or r <= 256 or c % 256 else 256
    c1, c2 = 1.0 - ADAM_B1 ** ADAM_STEP, 1.0 - ADAM_B2 ** ADAM_STEP

    def body(*refs):
        if own is None:
            p_ref, w_ref, m_ref, v_ref, g_ref, d_ref, mo_ref, vo_ref = refs
            part = lambda i: p_ref[i].astype(F32)
        else:
            p_ref, w_ref, m_ref, v_ref, own_ref, sel_ref, g_ref, d_ref, mo_ref, vo_ref = refs
            part = lambda i: jnp.where(sel_ref[i:i + 1, 0:1] > 0.5, own_ref[...].astype(F32), p_ref[i].astype(F32))
        g = part(0)
        for i in range(1, npart):
            g = g + part(i)
        mn = ADAM_B1 * m_ref[...] + (1.0 - ADAM_B1) * g
        vn = ADAM_B2 * v_ref[...] + (1.0 - ADAM_B2) * (g * g)
        g_ref[...] = g
        mo_ref[...] = mn
        vo_ref[...] = vn
        d_ref[...] = -ADAM_LR * ((mn / c1) / (jnp.sqrt(vn / c2) + ADAM_EPS) + ADAM_WD * w_ref[...])

    row = pl.BlockSpec((None, tr, tc), lambda l, i, j: (l, i, j))
    out = jax.ShapeDtypeStruct((nl, r, c), F32)
    ins, in_specs = [parts, w, m, v], [pl.BlockSpec((npart, None, tr, tc), lambda l, i, j: (0, l, i, j)), row, row, row]
    if own is not None:
        ins += [own, sel]
        in_specs += [row, pl.BlockSpec((N_DEV, LANE), lambda l, i, j: (0, 0))]
    return pl.pallas_call(body, grid=(nl, r // tr, c // tc), in_specs=in_specs, out_specs=(row, row, row, row),
                          out_shape=(out, out, out, out), compiler_params=_cp(("parallel", "parallel", "parallel")),
                          name=name)(*ins)


def _peer(k):
    x, y, c = lax.axis_index("x"), lax.axis_index("y"), lax.axis_index("c")
    return ((1 - x) if k & 4 else x, (1 - y) if k & 2 else y, (1 - c) if k & 1 else c)


def _my_index():
    return 4 * lax.axis_index("x") + 2 * lax.axis_index("y") + lax.axis_index("c")


def _index_of(p):
    return 4 * p[0] + 2 * p[1] + p[2]


def _all_gather(xs, *, name):
    n = len(xs)

    def body(*refs):
        x_refs, o_refs = refs[:n], refs[n:2 * n]
        send, recv, loc = refs[2 * n:]
        me = _my_index()
        copies = []
        for a in range(n):
            cp = pltpu.make_async_copy(x_refs[a], o_refs[a].at[me], loc.at[a])
            cp.start()
            copies.append(cp)
        rdmas = []
        for a in range(n):
            for k in range(1, N_DEV):
                r = pltpu.make_async_remote_copy(
                    src_ref=x_refs[a], dst_ref=o_refs[a].at[me], send_sem=send.at[a, k - 1], recv_sem=recv.at[a, k - 1],
                    device_id=_peer(k), device_id_type=pl.DeviceIdType.MESH)
                r.start()
                rdmas.append(r)
        for a in range(n):
            for k in range(1, N_DEV):
                pltpu.make_async_remote_copy(
                    src_ref=x_refs[a], dst_ref=o_refs[a].at[_index_of(_peer(k))], send_sem=send.at[a, k - 1],
                    recv_sem=recv.at[a, k - 1], device_id=_peer(k), device_id_type=pl.DeviceIdType.MESH).wait_recv()
        for r in rdmas:
            r.wait_send()
        for cp in copies:
            cp.wait()

    any_spec = pl.BlockSpec(memory_space=pl.ANY)
    return pl.pallas_call(
        body, in_specs=[any_spec] * n, out_specs=tuple([any_spec] * n),
        out_shape=tuple(jax.ShapeDtypeStruct((N_DEV,) + x.shape, x.dtype) for x in xs),
        scratch_shapes=[pltpu.SemaphoreType.DMA((n, N_DEV - 1)), pltpu.SemaphoreType.DMA((n, N_DEV - 1)),
                        pltpu.SemaphoreType.DMA((n,))],
        name=name)(*xs)


def _all_gather_two_level(xs, *, name):
    n = len(xs)

    def body(*refs):
        x_refs, o_refs = refs[:n], refs[n:2 * n]
        send, recv, loc = refs[2 * n:]
        x, y, c = lax.axis_index("x"), lax.axis_index("y"), lax.axis_index("c")
        me, sibling = (x, y, c), (x, y, 1 - c)
        chips = [(1 - x, y), (x, 1 - y), (1 - x, 1 - y)]

        def copy(a, k, block, to, src=None):
            dst = o_refs[a].at[_index_of(block)]
            return pltpu.make_async_remote_copy(src_ref=dst if src is None else src, dst_ref=dst, send_sem=send.at[a, k],
                                                recv_sem=recv.at[a, k], device_id=to, device_id_type=pl.DeviceIdType.MESH)

        mine = [pltpu.make_async_copy(x_refs[a], o_refs[a].at[_index_of(me)], loc.at[a]) for a in range(n)]
        first = [copy(a, 0, me, sibling, src=x_refs[a]) for a in range(n)]
        first += [copy(a, 1 + j, me, (*chip, c), src=x_refs[a]) for a in range(n) for j, chip in enumerate(chips)]
        for cp in mine + first:
            cp.start()
        passed = []
        for a in range(n):
            for j, chip in enumerate(chips):
                copy(a, 1 + j, (*chip, c), me).wait_recv()
                passed.append(copy(a, 4 + j, (*chip, c), sibling))
                passed[-1].start()
        for a in range(n):
            copy(a, 0, sibling, me).wait_recv()
            for j, chip in enumerate(chips):
                copy(a, 4 + j, (*chip, 1 - c), me).wait_recv()
        for cp in first + passed:
            cp.wait_send()
        for cp in mine:
            cp.wait()

    any_spec = pl.BlockSpec(memory_space=pl.ANY)
    return pl.pallas_call(
        body, in_specs=[any_spec] * n, out_specs=tuple([any_spec] * n),
        out_shape=tuple(jax.ShapeDtypeStruct((N_DEV,) + t.shape, t.dtype) for t in xs),
        scratch_shapes=[pltpu.SemaphoreType.DMA((n, N_DEV - 1)), pltpu.SemaphoreType.DMA((n, N_DEV - 1)),
                        pltpu.SemaphoreType.DMA((n,))],
        name=name)(*xs)


_HBM = pl.BlockSpec(memory_space=pltpu.HBM)
_SEM = pl.BlockSpec(memory_space=pltpu.SEMAPHORE)
_EFFECT = pltpu.SideEffectType.DATAFLOW_SIDE_EFFECTING


def _split_copy(src_ref, land_ref, send, recv, a, k, scatter, slot, sending):
    me, peer = _my_index(), _index_of(_peer(k))
    src = src_ref.at[peer if sending else me] if scatter else src_ref
    land = land_ref.at[me if sending else peer]
    if slot is not None:
        land = land.at[slot]
    sem = a * (N_DEV - 1) + k - 1
    return pltpu.make_async_remote_copy(src_ref=src, dst_ref=land, send_sem=send.at[sem], recv_sem=recv.at[sem],
                                        device_id=_peer(k), device_id_type=pl.DeviceIdType.MESH)


def _exchange_start(srcs, lands, after, *, scatter, slot=None, name):
    n = len(srcs)

    def body(*refs):
        src_refs, land_refs = refs[:n], refs[n:2 * n]
        send, recv, token = refs[2 * n + 1], refs[2 * n + 2], refs[-1]
        for a in range(n):
            for k in range(1, N_DEV):
                _split_copy(src_refs[a], land_refs[a], send, recv, a, k, scatter, slot, True).start()
        token[...] = jnp.zeros_like(token)

    hbm = lambda t: pltpu.HBM(t.shape, t.dtype)
    sems = pltpu.SemaphoreType.DMA((n * (N_DEV - 1),))
    out = pl.pallas_call(
        body, name=name,
        out_shape=(sems, sems, *[hbm(t) for t in srcs], *[hbm(t) for t in lands], jax.ShapeDtypeStruct((8, LANE), F32)),
        in_specs=[_HBM] * (2 * n) + [pl.BlockSpec(memory_space=pl.ANY)],
        out_specs=(_SEM, _SEM, *[_HBM] * (2 * n), pl.BlockSpec(memory_space=pltpu.VMEM)),
        input_output_aliases={i: 2 + i for i in range(2 * n)},
        compiler_params=pltpu.CompilerParams(has_side_effects=_EFFECT),
    )(*[pltpu.with_memory_space_constraint(t, pltpu.HBM) for t in (*srcs, *lands)], after)
    return out[0], out[1], out[2:2 + n], out[2 + n:2 + 2 * n], out[-1]


def _exchange_wait(send, recv, srcs, lands, after, *, scatter, slot=None, name):
    n = len(srcs)

    def body(*refs):
        src_refs, land_refs = refs[:n], refs[n:2 * n]
        send_ref, recv_ref = refs[2 * n], refs[2 * n + 1]
        for a in range(n):
            for k in range(1, N_DEV):
                _split_copy(src_refs[a], land_refs[a], send_ref, recv_ref, a, k, scatter, slot, True).wait_send()
                _split_copy(src_refs[a], land_refs[a], send_ref, recv_ref, a, k, scatter, slot, False).wait_recv()

    hbm = lambda t: pltpu.HBM(t.shape, t.dtype)
    out = pl.pallas_call(
        body, name=name, out_shape=(*[hbm(t) for t in srcs], *[hbm(t) for t in lands]),
        in_specs=[_HBM] * (2 * n) + [_SEM, _SEM, pl.BlockSpec(memory_space=pl.ANY)],
        out_specs=tuple([_HBM] * (2 * n)), input_output_aliases={i: i for i in range(2 * n)},
        compiler_params=pltpu.CompilerParams(has_side_effects=_EFFECT),
    )(*srcs, *lands, send, recv, after)
    return out[:n], out[n:]


def _win_to_mine(wt):
    pad = jnp.zeros((IN_PAD - IN_DIM,) + wt.shape[1:], wt.dtype)
    return jnp.concatenate([wt[3592:5640], wt[0:3584], wt[3584:3592], pad], axis=0)


def _win_from_mine(gt):
    return jnp.concatenate([gt[2048:5632], gt[5632:5640], gt[0:2048]], axis=0)


def _pad_rows(a, mult=8):
    r = (-a.shape[0]) % mult
    return a if r == 0 else jnp.concatenate([a, jnp.zeros((r,) + a.shape[1:], a.dtype)], axis=0)


def _lanes(vec, start):
    return jnp.zeros((1, LANE), F32).at[0, start:start + vec.shape[0]].set(vec)


def _small_spec(depth):
    return (("b_ada", (depth, 6 * D)), ("norm1_g", (depth, D)), ("norm2_g", (depth, D)),
            ("rel_table", (depth, A_HEADS, 2 * A_MAX_REL + 1)), ("a_log", (depth, B_HEADS)),
            ("dt_bias", (depth, B_HEADS)), ("gdn_norm_g", (depth, B_DH)), ("final_g", (D,)))


def _pack_small(d, extra, depth):
    spec = _small_spec(depth)
    rows = -(-(sum(math.prod(s) for _, s in spec) + 1) // (8 * LANE)) * 8
    flat = jnp.concatenate([d[n].reshape(-1).astype(F32) for n, _ in spec] + [extra.reshape(-1)])
    flat = jnp.concatenate([flat, jnp.zeros((rows * LANE - flat.shape[0],), F32)])
    return flat.reshape(rows, LANE)


def _unpack_small(p, depth):
    flat = p.reshape(-1)
    out, off = {}, 0
    for n, s in _small_spec(depth):
        sz = math.prod(s)
        out[n] = flat[off:off + sz].reshape(s)
        off += sz
    return out, flat[off]


def kernel(x, c, w_ada, b_ada, norm1_g, norm2_g, w_in, rel_table, w_conv, a_log, dt_bias, gdn_norm_g, w_branch_a, w_branch_b, w_out, w_ffn_in, w_ffn_out, final_g, loss_target, m_w_ada, m_b_ada, m_norm1_g, m_norm2_g, m_w_in, m_rel_table, m_w_conv, m_a_log, m_dt_bias, m_gdn_norm_g, m_w_branch_a, m_w_branch_b, m_w_out, m_w_ffn_in, m_w_ffn_out, m_final_g, v_w_ada, v_b_ada, v_norm1_g, v_norm2_g, v_w_in, v_rel_table, v_w_conv, v_a_log, v_dt_bias, v_gdn_norm_g, v_w_branch_a, v_w_branch_b, v_w_out, v_w_ffn_in, v_w_ffn_out, v_final_g):
    tt = x.shape[1]
    x0 = x[0]
    tgt = loss_target[0]
    me = _my_index()
    depth = w_in.shape[0]

    tr_ = lambda t: jnp.transpose(t, (0, 2, 1))
    shards = [tr_(w_in).astype(BF16), w_branch_a.astype(BF16), w_branch_b.astype(BF16), w_out.astype(BF16),
              tr_(w_ffn_in).astype(BF16), w_ffn_out.astype(BF16), w_conv]
    names = ("win", "wa", "wb", "wout", "wfi", "wfo", "wconv")
    early, late, every = (0, 6), (1, 2, 3, 4, 5), tuple(range(7))
    first = _all_gather_two_level([shards[i][0] for i in early] + [_pad_rows(c)], name="gather_first")
    c_all = first[-1][:, 0, :]
    is_me = lax.broadcasted_iota(jnp.int32, (N_DEV, 1, 1), 0) == me

    def unpack(idx, g):
        cols = lambda t: jnp.transpose(t, (1, 0, 2)).reshape(t.shape[1], N_DEV * t.shape[2])
        rows = lambda t: t.reshape(N_DEV * t.shape[1], t.shape[2])
        how = (lambda t: _win_to_mine(rows(t)), cols, cols, rows, rows, rows, cols)
        return {names[i]: how[i](t) for i, t in zip(idx, g)}

    def gather_start(l, idx, after, tag=""):
        srcs = [shards[i][l] for i in idx]
        lands = [lax.empty((N_DEV,) + t.shape, t.dtype) for t in srcs]
        return _exchange_start(srcs, lands, after, scatter=False, name=f"gather_start_{l}{tag}")

    def gather_wait(l, idx, pending, after, tag=""):
        send, recv, srcs, lands, _ = pending
        srcs, lands = _exchange_wait(send, recv, srcs, lands, after, scatter=False, name=f"gather_wait_{l}{tag}")
        return unpack(idx, [jnp.where(is_me, t[None], g) for g, t in zip(lands, srcs)])

    weights = [unpack(early, first[:-1])] + [None] * (depth - 1)
    pending0 = gather_start(0, late, first[-1], "_rest")
    pending = gather_start(1, every, pending0[-1]) if depth > 1 else None
    cond = c_all * (1.0 / (1.0 + jnp.exp(-c_all)))
    cond = _pad_rows(cond, 16)

    mod_cols = jnp.stack([_mm(cond, w_ada[l], name="mod_mm")[:N_DEV] for l in range(depth)])
    (g_mod,) = _all_gather([mod_cols], name="gather_mod")
    mod_all = jnp.transpose(g_mod, (1, 2, 0, 3)).reshape(depth, N_DEV, 6 * D)
    mod = lax.dynamic_index_in_dim(mod_all, me, axis=1, keepdims=False) + b_ada
    mods = mod.reshape(depth, 6, 1, D)

    n1g, n2g = norm1_g.reshape(depth, 1, D), norm2_g.reshape(depth, 1, D)
    gng = gdn_norm_g.reshape(depth, 1, B_DH)
    fg = final_g.reshape(1, D)

    saved = []
    tok = (pending if pending is not None else pending0)[-1][0, 0]
    xin, h1 = _adaln_fwd(x0, n1g[0], mods[0, 1] + tok, mods[0, 0], name="adaln1_first")
    for l in range(depth):
        sh1, sc1, gt1, sh2, sc2, gt2 = (mods[l, i] for i in range(6))
        wl = weights[l]
        proj = _mm(h1, wl["win"], tb=True, name="proj_mm", tn=1152)
        kpad, vpad = _kv_pad(proj, name="kv_pad")
        diag, bias_vjp = jax.vjp(_bias_diagonals, rel_table[l])
        bias = _bias_fwd(diag, name="bias_fwd")
        ya, lse = _attn_fwd(proj, kpad, vpad, bias, name="attn_fwd")
        par = jnp.concatenate([_lanes(a_log[l], B_HEADS), _lanes(dt_bias[l], B_HEADS), jnp.zeros((6, LANE), F32)], axis=0)
        qn, kn, vn, aux = _gdn_pre_fwd(proj, wl["wconv"], par, name="gdn_pre_fwd")
        auxt = _gc_rows(aux, tt // CH)
        lower = _gdn_lower(kn, aux, auxt, name="gdn_lower")
        tinv = jnp.transpose(_tri_inverse(jnp.transpose(lower, (1, 2, 0)), name="gdn_tri_inverse"), (2, 0, 1))
        u0, wg, qd, kd, qk, gle = _gdn_intra(qn, kn, vn, aux, auxt, tinv, name="gdn_intra")
        og, ss, ug = _gdn_scan_fwd(u0, wg, qd, kd, qk, gle, name="gdn_scan_fwd")
        yb = _gdn_post_fwd(og, proj, gng[l], name="gdn_post_fwd")
        if l == 0:
            wl.update(gather_wait(0, late, pending0, yb, "_rest"))
        pa, pb, merged = _branch_merge(ya, yb, wl["wa"], wl["wb"], proj, name="branch_merge")
        t1, x2, h2 = _out_adaln(merged, wl["wout"], xin, gt1, n2g[l], sc2, sh2, name="out_adaln2")
        gu, act = _ffn_in_swiglu(h2, wl["wfi"], name="ffn_in_swiglu")
        saved.append(dict(xin=xin, h1=h1, proj=proj, kpad=kpad, vpad=vpad, bias=bias, bias_vjp=bias_vjp, ya=ya, lse=lse,
                          par=par, qn=qn, kn=kn, vn=vn, aux=aux, auxt=auxt, tinv=tinv, ss=ss, og=og, yb=yb, pa=pa, pb=pb,
                          u0=u0, wg=wg, qd=qd, kd=kd, qk=qk, gle=gle, ug=ug,
                          merged=merged, t1=t1, x2=x2, h2=h2, gu=gu, act=act))
        if l + 1 < depth:
            weights[l + 1] = gather_wait(l + 1, every, pending, act)
            pending = gather_start(l + 2, every, weights[l + 1]["wconv"]) if l + 2 < depth else None
            tok = pending[-1][0, 0] if pending is not None else 0.0
            t2, xin, h1 = _out_adaln(act, wl["wfo"], x2, gt2, n1g[l + 1], mods[l + 1, 1] + tok, mods[l + 1, 0],
                                     tk=FTN, name="ffn_out_adaln1")
        else:
            t2 = _mm(act, wl["wfo"], name="ffn_out_mm", tk=FTN)
        saved[-1]["t2"] = t2

    s = saved[-1]
    dx, dt2, st = _loss_head(s["x2"], s["t2"], mods[depth - 1, 5], fg, tgt, name="loss_head")
    loss_part = st[4, 0]
    small_g = {"final_g": st[0]}
    dmod_rows = [None] * depth
    for n in ("norm1_g", "norm2_g", "rel_table", "a_log", "dt_bias", "gdn_norm_g"):
        small_g[n] = [None] * depth
    dgt2 = st[3]
    cols_slabs = lambda g: jnp.transpose(g.reshape(g.shape[0], N_DEV, g.shape[1] // N_DEV), (1, 0, 2))
    rows_slabs = lambda g: g.reshape(N_DEV, g.shape[0] // N_DEV, g.shape[1])
    mix, ffn = (0, 1, 2, 3, 6), (4, 5)
    lands = {kind: [lax.empty((N_DEV,) + shards[i].shape, shards[i].dtype) for i in idx]
             for kind, idx in (("mix", mix), ("ffn", ffn))}
    own = {kind: [None] * depth for kind in lands}
    pending_s = {kind: None for kind in lands}

    def scatter(kind, l, srcs, after):
        if pending_s[kind] is not None:
            done, lands[kind] = _exchange_wait(*pending_s[kind][:4], after, scatter=True, slot=l + 1,
                                               name=f"scatter_wait_{kind}_{l + 1}")
            own[kind][l + 1] = [lax.dynamic_index_in_dim(t, me, 0, keepdims=False) for t in done]
        pending_s[kind] = _exchange_start(srcs, lands[kind], after, scatter=True, slot=l, name=f"scatter_start_{kind}_{l}")
        return pending_s[kind][-1][0, 0]

    for l in reversed(range(depth)):
        s, wl = saved[l], weights[l]
        sh1, sc1, gt1, sh2, sc2, gt2 = (mods[l, i] for i in range(6))
        gw_fo = _mm(s["act"], dt2, ta=True, out_dtype=BF16, name="ffn_out_dw", tm=1408)
        dgu = _ffn_out_bwd_swiglu(dt2, wl["wfo"], s["gu"], name="ffn_out_bwd_swiglu")
        gw_fi = _mm(dgu, s["h2"], ta=True, out_dtype=BF16, name="ffn_in_dw", tm=1408)
        sc2 = sc2 + scatter("ffn", l, [rows_slabs(gw_fi), rows_slabs(gw_fo)], gw_fi)
        dx, dt1, st2 = _mm_adaln_bwd(dgu, wl["wfi"], s["x2"], n2g[l], sc2, sh2, dx, s["t1"], gt1, tk=FTN,
                                     name="ffn_in_dx_adaln2")
        gw_out = _mm(s["merged"], dt1, ta=True, out_dtype=BF16, name="out_dw")
        dgates, dpa, dpb = _out_bwd_merge(dt1, wl["wout"], s["proj"], s["pa"], s["pb"], name="out_bwd_merge")
        gw_a = _mm(s["ya"], dpa, ta=True, out_dtype=BF16, name="branch_a_dw")
        gw_b = _mm(s["yb"], dpb, ta=True, out_dtype=BF16, name="branch_b_dw")
        dya = _mm(dpa, wl["wa"], tb=True, name="branch_a_dx")
        dyb = _mm(dpb, wl["wb"], tb=True, name="branch_b_dx")
        dqa, dka, dva, dbias = _attn_bwd(s["proj"], s["kpad"], s["vpad"], s["bias"], s["ya"], s["lse"], dya,
                                             name="attn_bwd")
        ddiag = jnp.roll(_bias_bwd(dbias, name="bias_bwd"), -(QB - 1), axis=1)
        small_g["rel_table"][l] = s["bias_vjp"](ddiag)[0]
        dog, dz, dgn = _gdn_post_bwd(s["og"], s["proj"], gng[l], dyb, name="gdn_post_bwd")
        small_g["gdn_norm_g"][l] = dgn[0]
        dug, dss = _gdn_scan_bwd(s["wg"], s["qd"], s["kd"], s["qk"], s["gle"], dog, name="gdn_scan_bwd")
        dqn, dkn, dvn, daux = _gdn_bwd(s["qn"], s["kn"], s["vn"], s["aux"], s["auxt"], s["tinv"], s["u0"], s["wg"],
                                       s["ug"], s["ss"], dss, dug, dog, name="gdn_bwd")
        dqkv, dba, dwc, dpar = _gdn_pre_bwd(s["proj"], wl["wconv"], s["par"], dqn, dkn, dvn, daux, name="gdn_pre_bwd")
        small_g["a_log"][l] = dpar[0, B_HEADS:2 * B_HEADS]
        small_g["dt_bias"][l] = dpar[1, B_HEADS:2 * B_HEADS]
        dproj = jnp.concatenate([dgates, dqa, dka, dva, dqkv, dz, dba], axis=1)
        gw_in = _mm(dproj, s["h1"], ta=True, out_dtype=BF16, name="proj_dw", tm=1152)
        mix_srcs = [rows_slabs(_win_from_mine(gw_in)), cols_slabs(gw_a), cols_slabs(gw_b), rows_slabs(gw_out),
                    cols_slabs(dwc[0:CONV_K])]
        if l > 0:
            sc1 = sc1 + scatter("mix", l, mix_srcs, gw_in)
        if l > 0:
            p = saved[l - 1]
            dx, dt2, st1 = _mm_adaln_bwd(dproj, wl["win"], s["xin"], n1g[l], sc1, sh1, dx, p["t2"], mods[l - 1, 5],
                                         tk=1152, name="proj_dx_adaln1")
        else:
            dx, st1 = _mm_adaln_bwd(dproj, wl["win"], s["xin"], n1g[l], sc1, sh1, dx, tk=1152,
                                    name="proj_dx_adaln1_first")
        small_g["norm1_g"][l], small_g["norm2_g"][l] = st1[0], st2[0]
        dmod_rows[l] = jnp.concatenate([st1[2], st1[1], st2[3], st2[2], st2[1], dgt2])
        if l > 0:
            dgt2 = st1[3]
    grad_x = dx[None]

    small_local = {n: (jnp.stack(vs) if isinstance(vs, list) else vs) for n, vs in small_g.items()}
    small_local["b_ada"] = jnp.stack(dmod_rows)
    (g_small,) = _all_gather([_pack_small(small_local, loss_part, depth)], name="gather_small")
    tok = scatter("mix", 0, mix_srcs, g_small)
    wsm = _pack_small(dict(b_ada=b_ada, norm1_g=norm1_g, norm2_g=norm2_g, rel_table=rel_table, a_log=a_log,
                           dt_bias=dt_bias, gdn_norm_g=gdn_norm_g, final_g=final_g), jnp.zeros((1,), F32) + tok, depth)
    msm = _pack_small(dict(b_ada=m_b_ada, norm1_g=m_norm1_g, norm2_g=m_norm2_g, rel_table=m_rel_table, a_log=m_a_log,
                           dt_bias=m_dt_bias, gdn_norm_g=m_gdn_norm_g, final_g=m_final_g), jnp.zeros((1,), F32), depth)
    vsm = _pack_small(dict(b_ada=v_b_ada, norm1_g=v_norm1_g, norm2_g=v_norm2_g, rel_table=v_rel_table, a_log=v_a_log,
                           dt_bias=v_dt_bias, gdn_norm_g=v_gdn_norm_g, final_g=v_final_g), jnp.ones((1,), F32), depth)
    sm = [_unpack_small(t, depth) for t in _adamw(g_small[:, None], wsm[None], msm[None], vsm[None], name="adamw_small")]
    loss = sm[0][1]

    dmod_all = g_small.reshape(N_DEV, -1)[:, :depth * 6 * D].reshape(N_DEV, depth, 6 * D)
    dmod_mine = lax.dynamic_slice_in_dim(dmod_all, me * (6 * D // N_DEV), 6 * D // N_DEV, axis=2)
    g_ada = jnp.stack([_mm(cond + tok, _pad_rows(dmod_mine[:, l], 16), ta=True, name="ada_dw") for l in range(depth)])

    got, mine = {}, {}
    sel = jnp.broadcast_to(jnp.where(is_me[:, :, 0], 1.0, 0.0), (N_DEV, LANE)).astype(F32) + tok

    def finish(kind, idx, after):
        done, lands[kind] = _exchange_wait(*pending_s[kind][:4], after, scatter=True, slot=0, name=f"scatter_wait_{kind}_0")
        own[kind][0] = [lax.dynamic_index_in_dim(t, me, 0, keepdims=False) for t in done]
        for a, i in enumerate(idx):
            got[i] = lands[kind][a]
            mine[i] = jnp.stack([own[kind][l][a] for l in range(depth)])

    def upd(i, w, m, v, name):
        if i in (0, 4):
            return [tr_(t) for t in _adamw(got[i], tr_(w), tr_(m), tr_(v), mine[i], sel, name=name)]
        return _adamw(got[i], w, m, v, mine[i], sel, name=name)

    finish("ffn", ffn, g_ada)
    res = {
        "w_ada": _adamw(g_ada[None], w_ada, m_w_ada, v_w_ada, name="adamw_w_ada"),
        "w_ffn_in": upd(4, w_ffn_in, m_w_ffn_in, v_w_ffn_in, "adamw_w_ffn_in"),
        "w_ffn_out": upd(5, w_ffn_out, m_w_ffn_out, v_w_ffn_out, "adamw_w_ffn_out"),
    }
    done_first = (res["w_ada"][1][0, 0, 0] + res["w_ffn_in"][1][0, 0, 0] + res["w_ffn_out"][1][0, 0, 0] + sm[1][1])
    finish("mix", mix, jnp.zeros((8, LANE), F32) + done_first)
    res.update({
        "w_in": upd(0, w_in, m_w_in, v_w_in, "adamw_w_in"),
        "w_conv": upd(6, w_conv, m_w_conv, v_w_conv, "adamw_w_conv"),
        "w_branch_a": upd(1, w_branch_a, m_w_branch_a, v_w_branch_a, "adamw_w_branch_a"),
        "w_branch_b": upd(2, w_branch_b, m_w_branch_b, v_w_branch_b, "adamw_w_branch_b"),
        "w_out": upd(3, w_out, m_w_out, v_w_out, "adamw_w_out"),
    })
    for n, _ in _small_spec(depth):
        res[n] = [sm[i][0][n] for i in range(4)]
    order = ("w_ada", "b_ada", "norm1_g", "norm2_g", "w_in", "rel_table", "w_conv", "a_log", "dt_bias", "gdn_norm_g",
             "w_branch_a", "w_branch_b", "w_out", "w_ffn_in", "w_ffn_out", "final_g")
    return (loss, grad_x, *[res[n][0] for n in order], *[res[n][1] for n in order],
            *[res[n][2] for n in order], *[res[n][3] for n in order])
```

```python
import functools
import math

import jax
import jax.numpy as jnp
from jax import lax
from jax.experimental import pallas as pl
from jax.experimental.pallas import tpu as pltpu

F32 = jnp.float32
BF16 = jnp.bfloat16
HI = lax.Precision.HIGHEST

N_DEV = 8
D = 1024
DEPTH = 4
CH = 64
EPS = 1e-6
A_HEADS, A_DH = 8, 64
A_W = A_HEADS * A_DH
A_PAST = 8
A_MAX_REL = 128
QB = 256
KB = QB + A_PAST * CH
B_HEADS, B_DH = 4, 128
B_W = B_HEADS * B_DH
CONV_K = 4
FF = 2816
IN_DIM = 5640
IN_PAD = 5760
LANE = 128
NEG = -1e30
VMEM_LIMIT = 48 * 1024 * 1024

ADAM_LR, ADAM_B1, ADAM_B2, ADAM_EPS, ADAM_WD, ADAM_STEP = 0.001, 0.9, 0.999, 1e-08, 0.01, 10

OFF_GA, OFF_GB, OFF_QA, OFF_KA, OFF_VA, OFF_QB, OFF_KB, OFF_VB, OFF_ZB, OFF_BA = (
    0, 1024, 2048, 2560, 3072, 3584, 4096, 4608, 5120, 5632)


def _cp(sem=None):
    return pltpu.CompilerParams(dimension_semantics=sem, vmem_limit_bytes=VMEM_LIMIT)


def _tile(n, pref):
    if n <= pref:
        return n
    best = None
    for t in range(LANE, pref + 1, LANE):
        if n % t == 0:
            best = t
    assert best is not None, (n, pref)
    return best


def _sigmoid(x):
    return 1.0 / (1.0 + jnp.exp(-x))


def _silu(x):
    return x * _sigmoid(x)


def _dsilu(x):
    s = _sigmoid(x)
    return s * (1.0 + x * (1.0 - s))


def _dot(a, b, prec=None):
    return jnp.dot(a, b, preferred_element_type=F32, precision=prec)


def _dot_nt(a, b, prec=None):
    return lax.dot_general(a, b, (((1,), (1,)), ((), ())), preferred_element_type=F32, precision=prec)


def _dot_tn(a, b, prec=None):
    return lax.dot_general(a, b, (((0,), (0,)), ((), ())), preferred_element_type=F32, precision=prec)


def _mm(a, b, *, ta=False, tb=False, out_dtype=F32, name, tm=1024, tn=1024, tk=1024):
    halves = a.ndim == 3
    a_rows, a_cols = (a.shape[1], 2 * a.shape[2]) if halves else a.shape
    m, k = (a_cols, a_rows) if ta else (a_rows, a_cols)
    n = b.shape[0] if tb else b.shape[1]
    assert k == (b.shape[1] if tb else b.shape[0]), (a.shape, b.shape, ta, tb)
    tm, tn, tk = _tile(m, tm), _tile(n, tn), _tile(k, tk)
    nk = k // tk
    dn = (((0 if ta else 1,), (1 if tb else 0,)), ((), ()))

    def body(a_ref, b_ref, o_ref, *acc):
        part = lax.dot_general(a_ref[...].astype(BF16), b_ref[...].astype(BF16), dn, preferred_element_type=F32)
        if nk == 1:
            o_ref[...] = part.astype(out_dtype)
            return
        acc_ref, kk = acc[0], pl.program_id(2)

        @pl.when(kk == 0)
        def _():
            acc_ref[...] = part

        @pl.when(kk > 0)
        def _():
            acc_ref[...] += part

        @pl.when(kk == nk - 1)
        def _():
            o_ref[...] = acc_ref[...].astype(out_dtype)

    if halves:
        per = a.shape[2] // (tm if ta else tk)
        a_spec = (pl.BlockSpec((None, tk, tm), lambda i, j, q: (i // per, q, i % per)) if ta else
                  pl.BlockSpec((None, tm, tk), lambda i, j, q: (q // per, i, q % per)))
    else:
        a_spec = pl.BlockSpec((tk, tm), lambda i, j, q: (q, i)) if ta else pl.BlockSpec((tm, tk), lambda i, j, q: (i, q))
    b_spec = pl.BlockSpec((tn, tk), lambda i, j, q: (j, q)) if tb else pl.BlockSpec((tk, tn), lambda i, j, q: (q, j))
    return pl.pallas_call(
        body, grid=(m // tm, n // tn, nk), in_specs=[a_spec, b_spec],
        out_specs=pl.BlockSpec((tm, tn), lambda i, j, q: (i, j)),
        out_shape=jax.ShapeDtypeStruct((m, n), out_dtype),
        scratch_shapes=[pltpu.VMEM((tm, tn), F32)] if nk > 1 else [],
        compiler_params=_cp(("parallel", "parallel", "arbitrary")), name=name)(a, b)


def _rb(tr, width, cb=0):
    return pl.BlockSpec((tr, width), lambda i: (i, cb))


def _whole(shape):
    nd = len(shape)
    return pl.BlockSpec(shape, lambda i: (0,) * nd)


def _colsum(v):
    return jnp.sum(v, axis=0, keepdims=True)


def _adaln_fwd(x, g, sc, sh, t=None, gt=None, *, name, tr=256):
    tt = x.shape[0]
    res = t is not None

    def body(*refs):
        if res:
            x_ref, t_ref, gt_ref, g_ref, sc_ref, sh_ref, xo_ref, h_ref = refs
            xv = x_ref[...] + gt_ref[...] * t_ref[...]
            xo_ref[...] = xv
        else:
            x_ref, g_ref, sc_ref, sh_ref, h_ref = refs
            xv = x_ref[...]
        r = lax.rsqrt(jnp.mean(xv * xv, axis=-1, keepdims=True) + EPS)
        h_ref[...] = ((xv * r * g_ref[...]) * (1.0 + sc_ref[...]) + sh_ref[...]).astype(BF16)

    row, vec = _rb(tr, D), _whole((1, D))
    if res:
        ins, in_specs = (x, t, gt, g, sc, sh), [row, row, vec, vec, vec, vec]
        out_shape = (jax.ShapeDtypeStruct((tt, D), F32), jax.ShapeDtypeStruct((tt, D), BF16))
        out_specs = (row, row)
    else:
        ins, in_specs = (x, g, sc, sh), [row, vec, vec, vec]
        out_shape, out_specs = jax.ShapeDtypeStruct((tt, D), BF16), row
    out = pl.pallas_call(body, grid=(tt // tr,), in_specs=in_specs, out_specs=out_specs, out_shape=out_shape,
                         compiler_params=_cp(("parallel",)), name=name)(*ins)
    return out if res else (x, out)


def _mm_adaln_bwd(a, b, x, g, sc, sh, dx_in, t=None, gt=None, *, name, tk, tm=512):
    tt = x.shape[0]
    res = t is not None
    halves = a.ndim == 3
    k = 2 * a.shape[2] if halves else a.shape[1]
    tm, nk = _tile(tt, tm), k // tk

    def body(*refs):
        if res:
            a_ref, b_ref, x_ref, g_ref, sc_ref, sh_ref, dxi_ref, t_ref, gt_ref, dx_ref, dt_ref, st_ref, acc_ref = refs
        else:
            a_ref, b_ref, x_ref, g_ref, sc_ref, sh_ref, dxi_ref, dx_ref, st_ref, acc_ref = refs
        i, q = pl.program_id(0), pl.program_id(1)
        part = _dot(a_ref[...], b_ref[...])

        @pl.when((i == 0) & (q == 0))
        def _():
            st_ref[...] = jnp.zeros_like(st_ref)

        @pl.when(q == 0)
        def _():
            acc_ref[...] = part

        @pl.when(q > 0)
        def _():
            acc_ref[...] += part

        @pl.when(q == nk - 1)
        def _():
            xv, dh = x_ref[...], acc_ref[...]
            r = lax.rsqrt(jnp.mean(xv * xv, axis=-1, keepdims=True) + EPS)
            nrm = xv * r
            y = nrm * g_ref[...]
            dy = dh * (1.0 + sc_ref[...])
            dn = dy * g_ref[...]
            dx = dxi_ref[...] + r * (dn - nrm * jnp.mean(dn * nrm, axis=-1, keepdims=True))
            dx_ref[...] = dx
            st_ref[0:1, :] += _colsum(dy * nrm)
            st_ref[1:2, :] += _colsum(dh * y)
            st_ref[2:3, :] += _colsum(dh)
            if res:
                dt_ref[...] = (gt_ref[...] * dx).astype(BF16)
                st_ref[3:4, :] += _colsum(dx * t_ref[...])

    if halves:
        per = a.shape[2] // tk
        a_spec = pl.BlockSpec((None, tm, tk), lambda i, q: (q // per, i, q % per))
    else:
        a_spec = pl.BlockSpec((tm, tk), lambda i, q: (i, q))
    row = pl.BlockSpec((tm, D), lambda i, q: (i, 0))
    vec = pl.BlockSpec((1, D), lambda i, q: (0, 0))
    ins = [a, b, x, g, sc, sh, dx_in]
    in_specs = [a_spec, pl.BlockSpec((tk, D), lambda i, q: (q, 0)), row, vec, vec, vec, row]
    out_shape, out_specs = [jax.ShapeDtypeStruct((tt, D), F32)], [row]
    if res:
        ins += [t, gt]
        in_specs += [row, vec]
        out_shape.append(jax.ShapeDtypeStruct((tt, D), BF16))
        out_specs.append(row)
    out_shape.append(jax.ShapeDtypeStruct((8, D), F32))
    out_specs.append(pl.BlockSpec((8, D), lambda i, q: (0, 0)))
    return pl.pallas_call(body, grid=(tt // tm, nk), in_specs=in_specs, out_specs=tuple(out_specs),
                          out_shape=tuple(out_shape), scratch_shapes=[pltpu.VMEM((tm, D), F32)],
                          compiler_params=_cp(("arbitrary", "arbitrary")), name=name)(*ins)


def _loss_head(x, t, gt, fg, tgt, *, name, tr=256):
    tt = x.shape[0]

    def body(x_ref, t_ref, gt_ref, fg_ref, tgt_ref, dx_ref, dt_ref, st_ref):
        @pl.when(pl.program_id(0) == 0)
        def _():
            st_ref[...] = jnp.zeros_like(st_ref)

        tv = t_ref[...]
        xv = x_ref[...] + gt_ref[...] * tv
        r = lax.rsqrt(jnp.mean(xv * xv, axis=-1, keepdims=True) + EPS)
        nrm = xv * r
        err = nrm * fg_ref[...] - tgt_ref[...]
        st_ref[4:5, :] += 0.5 * jnp.sum(jnp.mean(err * err, axis=-1, keepdims=True), axis=0, keepdims=True)
        dy = err * (1.0 / D)
        dn = dy * fg_ref[...]
        dx = r * (dn - nrm * jnp.mean(dn * nrm, axis=-1, keepdims=True))
        dx_ref[...] = dx
        dt_ref[...] = (gt_ref[...] * dx).astype(BF16)
        st_ref[0:1, :] += _colsum(dy * nrm)
        st_ref[3:4, :] += _colsum(dx * tv)

    row, vec = _rb(tr, D), _whole((1, D))
    return pl.pallas_call(
        body, grid=(tt // tr,), in_specs=[row, row, vec, vec, row], out_specs=(row, row, _whole((8, D))),
        out_shape=(jax.ShapeDtypeStruct((tt, D), F32), jax.ShapeDtypeStruct((tt, D), BF16),
                   jax.ShapeDtypeStruct((8, D), F32)),
        compiler_params=_cp(("arbitrary",)), name=name)(x, t, gt, fg, tgt)


def _branch_merge(ya, yb, wa, wb, proj, *, name, tm=512):
    tt = ya.shape[0]
    tm = _tile(tt, tm)

    def body(ya_ref, yb_ref, wa_ref, wb_ref, ga_ref, gb_ref, pa_ref, pb_ref, o_ref):
        pa = _dot(ya_ref[...], wa_ref[...])
        pb = _dot(yb_ref[...], wb_ref[...])
        pa_ref[...] = pa.astype(BF16)
        pb_ref[...] = pb.astype(BF16)
        o_ref[...] = (_sigmoid(ga_ref[...]) * pa + _sigmoid(gb_ref[...]) * pb).astype(BF16)

    row, half, wsp = _rb(tm, D), _rb(tm, A_W), _whole((A_W, D))
    out = jax.ShapeDtypeStruct((tt, D), BF16)
    return pl.pallas_call(body, grid=(tt // tm,), in_specs=[half, half, wsp, wsp, _rb(tm, D, 0), _rb(tm, D, 1)],
                          out_specs=(row, row, row), out_shape=(out, out, out), compiler_params=_cp(("parallel",)),
                          name=name)(ya, yb, wa, wb, proj, proj)


def _out_adaln(a, w, x, gt, g, sc, sh, *, name, tk=None, tm=512):
    tt, k = a.shape
    tm, tk = _tile(tt, tm), tk or k
    nk = k // tk

    def body(a_ref, w_ref, x_ref, gt_ref, g_ref, sc_ref, sh_ref, t_ref, xo_ref, h_ref):
        q = pl.program_id(1)
        part = _dot(a_ref[...], w_ref[...])

        @pl.when(q == 0)
        def _():
            t_ref[...] = part

        @pl.when(q > 0)
        def _():
            t_ref[...] += part

        @pl.when(q == nk - 1)
        def _():
            xv = x_ref[...] + gt_ref[...] * t_ref[...]
            xo_ref[...] = xv
            r = lax.rsqrt(jnp.mean(xv * xv, axis=-1, keepdims=True) + EPS)
            h_ref[...] = ((xv * r * g_ref[...]) * (1.0 + sc_ref[...]) + sh_ref[...]).astype(BF16)

    row = pl.BlockSpec((tm, D), lambda i, q: (i, 0))
    vec = pl.BlockSpec((1, D), lambda i, q: (0, 0))
    f32 = jax.ShapeDtypeStruct((tt, D), F32)
    return pl.pallas_call(
        body, grid=(tt // tm, nk),
        in_specs=[pl.BlockSpec((tm, tk), lambda i, q: (i, q)), pl.BlockSpec((tk, D), lambda i, q: (q, 0)),
                  row, vec, vec, vec, vec],
        out_specs=(row, row, row), out_shape=(f32, f32, jax.ShapeDtypeStruct((tt, D), BF16)),
        compiler_params=_cp(("parallel", "arbitrary")), name=name)(a, w, x, gt, g, sc, sh)


def _out_bwd_merge(dt, wout, proj, pa, pb, *, name, tm=512):
    tt = dt.shape[0]
    tm = _tile(tt, tm)

    def body(dt_ref, w_ref, ga_ref, gb_ref, pa_ref, pb_ref, dg_ref, dpa_ref, dpb_ref):
        dm_v = _dot_nt(dt_ref[...], w_ref[...])
        sa, sb = _sigmoid(ga_ref[...]), _sigmoid(gb_ref[...])
        dpa_ref[...] = (dm_v * sa).astype(BF16)
        dpb_ref[...] = (dm_v * sb).astype(BF16)
        dg_ref[:, 0:D] = (dm_v * pa_ref[...].astype(F32) * sa * (1.0 - sa)).astype(BF16)
        dg_ref[:, D:2 * D] = (dm_v * pb_ref[...].astype(F32) * sb * (1.0 - sb)).astype(BF16)

    row = _rb(tm, D)
    return pl.pallas_call(
        body, grid=(tt // tm,), in_specs=[row, _whole((D, D)), _rb(tm, D, 0), _rb(tm, D, 1), row, row],
        out_specs=(_rb(tm, 2 * D), row, row),
        out_shape=(jax.ShapeDtypeStruct((tt, 2 * D), BF16), jax.ShapeDtypeStruct((tt, D), BF16),
                   jax.ShapeDtypeStruct((tt, D), BF16)),
        compiler_params=_cp(("parallel",)), name=name)(dt, wout, proj, proj, pa, pb)


FTN = FF // 2


def _ffn_in_swiglu(h, wt, *, name, tm=1024):
    tt = h.shape[0]
    tm = _tile(tt, tm)

    def body(h_ref, wg_ref, wu_ref, gu_ref, act_ref):
        hv = h_ref[...]
        g = _dot_nt(hv, wg_ref[...])
        u = _dot_nt(hv, wu_ref[...])
        gu_ref[0] = g.astype(BF16)
        gu_ref[1] = u.astype(BF16)
        act_ref[...] = (_silu(g) * u).astype(BF16)

    nj = FF // FTN
    return pl.pallas_call(
        body, grid=(tt // tm, nj),
        in_specs=[pl.BlockSpec((tm, D), lambda i, j: (i, 0)), pl.BlockSpec((FTN, D), lambda i, j: (j, 0)),
                  pl.BlockSpec((FTN, D), lambda i, j: (j + nj, 0))],
        out_specs=(pl.BlockSpec((2, tm, FTN), lambda i, j: (0, i, j)), pl.BlockSpec((tm, FTN), lambda i, j: (i, j))),
        out_shape=(jax.ShapeDtypeStruct((2, tt, FF), BF16), jax.ShapeDtypeStruct((tt, FF), BF16)),
        compiler_params=_cp(("parallel", "parallel")), name=name)(h, wt, wt)


def _ffn_out_bwd_swiglu(dt, wo, gu, *, name, tm=1024):
    tt = dt.shape[0]
    tm = _tile(tt, tm)

    def body(dt_ref, wo_ref, gu_ref, dgu_ref):
        da = _dot_nt(dt_ref[...], wo_ref[...])
        g, u = gu_ref[0].astype(F32), gu_ref[1].astype(F32)
        dgu_ref[0] = (da * u * _dsilu(g)).astype(BF16)
        dgu_ref[1] = (da * _silu(g)).astype(BF16)

    blk = pl.BlockSpec((2, tm, FTN), lambda i, j: (0, i, j))
    return pl.pallas_call(
        body, grid=(tt // tm, FF // FTN),
        in_specs=[pl.BlockSpec((tm, D), lambda i, j: (i, 0)), pl.BlockSpec((FTN, D), lambda i, j: (j, 0)), blk],
        out_specs=blk, out_shape=jax.ShapeDtypeStruct((2, tt, FF), BF16),
        compiler_params=_cp(("parallel", "parallel")), name=name)(dt, wo, gu)


BIAS_LW = 1152


def _bias_diagonals(table):
    n_far = A_PAST * CH - A_MAX_REL + 1
    far = jnp.broadcast_to(table[:, 2 * A_MAX_REL:], (A_HEADS, n_far))
    mid = jnp.flip(table[:, 1:2 * A_MAX_REL], axis=1)
    near = jnp.broadcast_to(table[:, 0:1], (A_HEADS, KB - n_far - (2 * A_MAX_REL - 1)))
    pos = jnp.concatenate([far, mid, near], axis=1)
    neg = jnp.broadcast_to(table[:, 2 * A_MAX_REL:], (A_HEADS, QB - 1))
    gap = jnp.zeros((A_HEADS, BIAS_LW - KB - (QB - 1)), F32)
    return jnp.concatenate([pos, gap, neg], axis=1)


def _bias_fwd(diag, *, name):
    def body(w_ref, o_ref):
        qc = lax.broadcasted_iota(jnp.int32, (QB, KB), 0) // CH + A_PAST
        col = lax.broadcasted_iota(jnp.int32, (QB, KB), 1)
        inband = (col // CH <= qc) & (col // CH >= qc - A_PAST)
        for h in range(A_HEADS):
            rows = pltpu.roll(jnp.broadcast_to(w_ref[h:h + 1, :], (QB, BIAS_LW)), 0, 1, stride=1, stride_axis=0)
            for var in range(3):
                o_ref[var, h] = jnp.where(inband & (col >= A_PAST * CH - QB * var), rows[:, :KB], NEG)

    return pl.pallas_call(body, out_shape=jax.ShapeDtypeStruct((3, A_HEADS, QB, KB), F32), compiler_params=_cp(),
                          name=name)(diag)


def _bias_bwd(dbias, *, name):
    def body(d_ref, o_ref):
        r = lax.broadcasted_iota(jnp.int32, (QB, QB), 0)
        c = lax.broadcasted_iota(jnp.int32, (QB, QB), 1)
        flip = jnp.where(r + c == QB - 1, 1.0, 0.0).astype(F32)
        for h in range(A_HEADS):
            x = jnp.concatenate([_dot(flip, d_ref[h], HI), jnp.zeros((QB, BIAS_LW - KB), F32)], axis=1)
            o_ref[h:h + 1, :] = jnp.sum(pltpu.roll(x, 0, 1, stride=1, stride_axis=0), axis=0, keepdims=True)

    return pl.pallas_call(body, out_shape=jax.ShapeDtypeStruct((A_HEADS, BIAS_LW), F32), compiler_params=_cp(),
                          name=name)(dbias)


def _kv_pad(proj, *, name, tr=256):
    tt = proj.shape[0]
    tr = 2 * tr if tt % (2 * tr) == 0 else tr
    npad = A_PAST * CH // tr

    def body(k_ref, v_ref, ko_ref, vo_ref):
        i = pl.program_id(0)

        @pl.when(i < npad)
        def _():
            ko_ref[...] = jnp.zeros_like(ko_ref)
            vo_ref[...] = jnp.zeros_like(vo_ref)

        @pl.when(i >= npad)
        def _():
            ko_ref[...] = k_ref[...].astype(BF16)
            vo_ref[...] = v_ref[...].astype(BF16)

    src = lambda off: pl.BlockSpec((tr, A_W), lambda i: (jnp.maximum(i - npad, 0), off // A_W))
    out = jax.ShapeDtypeStruct((tt + A_PAST * CH, A_W), BF16)
    return pl.pallas_call(body, grid=(tt // tr + npad,), in_specs=[src(OFF_KA), src(OFF_VA)],
                          out_specs=(_rb(tr, A_W), _rb(tr, A_W)), out_shape=(out, out),
                          compiler_params=_cp(("parallel",)), name=name)(proj, proj)


def _attn_fwd(proj, kpad, vpad, bias, *, name):
    tt = proj.shape[0]

    def body(q_ref, k_ref, v_ref, b_ref, o_ref, l_ref):
        q0 = pl.multiple_of(pl.program_id(1) * QB, QB)
        q = q_ref[...] * (A_DH ** -0.5)
        k = k_ref[pl.ds(q0, KB), :]
        v = v_ref[pl.ds(q0, KB), :]
        lane = lax.broadcasted_iota(jnp.int32, (QB, LANE), 1)
        o = jnp.zeros((QB, LANE), F32)
        lse = jnp.zeros((QB, LANE), F32)
        for a in range(2):
            hm = (lane >= A_DH * a) & (lane < A_DH * (a + 1))
            s = _dot_nt(jnp.where(hm, q, 0.0).astype(BF16), k) + b_ref[a]
            m = jnp.max(s, axis=-1, keepdims=True)
            p = jnp.exp(s - m)
            l = jnp.sum(p, axis=-1, keepdims=True)
            o = jnp.where(hm, _dot(p.astype(BF16), v) / l, o)
            lse = jnp.where(hm, m + jnp.log(l), lse)
        o_ref[...] = o.astype(BF16)
        l_ref[...] = lse

    kv = pl.BlockSpec((tt + A_PAST * CH, LANE), lambda h, i: (0, h))
    blk = pl.BlockSpec((QB, LANE), lambda h, i: (i, h))
    return pl.pallas_call(
        body, grid=(A_W // LANE, tt // QB),
        in_specs=[pl.BlockSpec((QB, LANE), lambda h, i: (i, OFF_QA // LANE + h)), kv, kv,
                  pl.BlockSpec((None, 2, QB, KB), lambda h, i: (jnp.minimum(i, 2), h, 0, 0))],
        out_specs=(blk, blk),
        out_shape=(jax.ShapeDtypeStruct((tt, A_W), BF16), jax.ShapeDtypeStruct((tt, A_W), F32)),
        compiler_params=_cp(("parallel", "parallel")), name=name)(proj, kpad, vpad, bias)


def _attn_bwd(proj, kpad, vpad, bias, o, lse, do, *, name):
    tt = proj.shape[0]
    nq = tt // QB

    def body(q_ref, k_ref, v_ref, b_ref, o_ref, l_ref, do_ref, dq_ref, dko_ref, dvo_ref, db_ref, dk_ref, dv_ref):
        @pl.when(pl.program_id(1) == 0)
        def _():
            dk_ref[...] = jnp.zeros_like(dk_ref)
            dv_ref[...] = jnp.zeros_like(dv_ref)
            db_ref[...] = jnp.zeros_like(db_ref)

        q0 = pl.multiple_of(pl.program_id(1) * QB, QB)
        q, do_v, lse = q_ref[...] * (A_DH ** -0.5), do_ref[...], l_ref[...]
        k = k_ref[pl.ds(q0, KB), :]
        v = v_ref[pl.ds(q0, KB), :]
        dsum = do_v * o_ref[...].astype(F32)
        lane = lax.broadcasted_iota(jnp.int32, (QB, LANE), 1)
        dq = jnp.zeros((QB, LANE), F32)
        dk = jnp.zeros((KB, LANE), F32)
        dv = jnp.zeros((KB, LANE), F32)
        for a in range(2):
            hm = (lane >= A_DH * a) & (lane < A_DH * (a + 1))
            qa = jnp.where(hm, q, 0.0).astype(BF16)
            doa = jnp.where(hm, do_v, 0.0).astype(BF16)
            s = _dot_nt(qa, k) + b_ref[a]
            lse_a = jnp.max(jnp.where(hm, lse, NEG), axis=-1, keepdims=True)
            p = jnp.exp(s - lse_a)
            dp = _dot_nt(doa, v)
            dsum_a = jnp.sum(jnp.where(hm, dsum, 0.0), axis=-1, keepdims=True)
            ds = p * (dp - dsum_a)
            db_ref[a] += ds
            dsb = ds.astype(BF16)
            dq = jnp.where(hm, _dot(dsb, k) * (A_DH ** -0.5), dq)
            dk += _dot_tn(dsb, qa)
            dv += _dot_tn(p.astype(BF16), doa)
        dq_ref[...] = dq.astype(BF16)
        dk_ref[pl.ds(q0, KB), :] += dk
        dv_ref[pl.ds(q0, KB), :] += dv

        @pl.when(pl.program_id(1) == nq - 1)
        def _():
            dko_ref[...] = dk_ref[A_PAST * CH:, :].astype(BF16)
            dvo_ref[...] = dv_ref[A_PAST * CH:, :].astype(BF16)

    kv = pl.BlockSpec((tt + A_PAST * CH, LANE), lambda h, i: (0, h))
    blk = pl.BlockSpec((QB, LANE), lambda h, i: (i, h))
    col = pl.BlockSpec((tt, LANE), lambda h, i: (0, h))
    bsp = pl.BlockSpec((2, QB, KB), lambda h, i: (h, 0, 0))
    bias_in = pl.BlockSpec((None, 2, QB, KB), lambda h, i: (jnp.minimum(i, 2), h, 0, 0))
    out = jax.ShapeDtypeStruct((tt, A_W), BF16)
    return pl.pallas_call(
        body, grid=(A_W // LANE, nq),
        in_specs=[pl.BlockSpec((QB, LANE), lambda h, i: (i, OFF_QA // LANE + h)), kv, kv, bias_in, blk, blk, blk],
        out_specs=(blk, col, col, bsp),
        out_shape=(out, out, out, jax.ShapeDtypeStruct((A_HEADS, QB, KB), F32)),
        scratch_shapes=[pltpu.VMEM((tt + A_PAST * CH, LANE), F32), pltpu.VMEM((tt + A_PAST * CH, LANE), F32)],
        compiler_params=_cp(("parallel", "arbitrary")), name=name)(proj, kpad, vpad, bias, o, lse, do)


GTR = 256


def _taps(w_ref, grp):
    return [w_ref[j:j + 1, grp * B_W:(grp + 1) * B_W] for j in range(CONV_K)]


def _shifts(xe, rows):
    return [xe[8:8 + rows]] + [pltpu.roll(xe, s, 0)[8:8 + rows] for s in range(1, CONV_K)]


def _conv(shifts, taps):
    acc = taps[CONV_K - 1] * shifts[0]
    for s in range(1, CONV_K):
        acc = acc + taps[CONV_K - 1 - s] * shifts[s]
    return acc


def _qk_scale(grp):
    return B_DH ** -0.5 if grp == 0 else 1.0


def _act_fwd(c, grp):
    y = _silu(c)
    if grp == 2:
        return y
    parts = []
    for hd in range(B_HEADS):
        yh = y[:, hd * B_DH:(hd + 1) * B_DH]
        parts.append(yh * (lax.rsqrt(jnp.sum(yh * yh, axis=-1, keepdims=True) + EPS) * _qk_scale(grp)))
    return jnp.concatenate(parts, axis=1)


def _act_bwd(c, dy, grp):
    if grp == 2:
        return dy * _dsilu(c)
    y = _silu(c)
    parts = []
    for hd in range(B_HEADS):
        yh = y[:, hd * B_DH:(hd + 1) * B_DH]
        r = lax.rsqrt(jnp.sum(yh * yh, axis=-1, keepdims=True) + EPS)
        dyh = dy[:, hd * B_DH:(hd + 1) * B_DH] * _qk_scale(grp)
        parts.append(r * dyh - yh * (r * r * r) * jnp.sum(dyh * yh, axis=-1, keepdims=True))
    return jnp.concatenate(parts, axis=1) * _dsilu(c)


def _chunk_tri(n, upper=False):
    r = lax.broadcasted_iota(jnp.int32, (n, n), 0)
    c = lax.broadcasted_iota(jnp.int32, (n, n), 1)
    same = (r // CH) == (c // CH)
    return jnp.where(same & ((r <= c) if upper else (r >= c)), 1.0, 0.0).astype(F32)


def _gate_rows(ba, par_ref):
    lane = lax.broadcasted_iota(jnp.int32, ba.shape, 1)
    z = ba + par_ref[1:2, :]
    sp = jnp.maximum(z, 0.0) + jnp.log(1.0 + jnp.exp(-jnp.abs(z)))
    g = -jnp.exp(par_ref[0:1, :]) * sp
    return jnp.where(lane < B_HEADS, _sigmoid(ba), jnp.where(lane < 2 * B_HEADS, g, 0.0)), z


def _prev8(cb):
    return pl.BlockSpec((8, B_W), lambda i: (jnp.maximum(i * (GTR // 8) - 1, 0), cb))


def _next8(cb, nb):
    return pl.BlockSpec((8, B_W), lambda i: (jnp.minimum((i + 1) * (GTR // 8), nb * (GTR // 8) - 1), cb))


def _gdn_pre_fwd(proj, wconv, par, *, name):
    tt = proj.shape[0]

    def body(q_ref, k_ref, v_ref, qh_ref, kh_ref, vh_ref, ba_ref, w_ref, par_ref, qo_ref, ko_ref, vo_ref, aux_ref):
        first = pl.program_id(0) == 0
        for grp, (x_ref, h_ref, o_ref) in enumerate(((q_ref, qh_ref, qo_ref), (k_ref, kh_ref, ko_ref),
                                                     (v_ref, vh_ref, vo_ref))):
            xe = jnp.concatenate([jnp.where(first, 0.0, h_ref[...]), x_ref[...]], axis=0)
            o_ref[...] = _act_fwd(_conv(_shifts(xe, GTR), _taps(w_ref, grp)), grp)
        bg, _ = _gate_rows(ba_ref[...], par_ref)
        lane = lax.broadcasted_iota(jnp.int32, bg.shape, 1)
        aux_ref[...] = jnp.where(lane < B_HEADS, bg, _dot(_chunk_tri(GTR), bg, HI))

    col = lambda off: _rb(GTR, B_W, off // B_W)
    outs = jax.ShapeDtypeStruct((tt, B_W), F32)
    return pl.pallas_call(
        body, grid=(tt // GTR,),
        in_specs=[col(OFF_QB), col(OFF_KB), col(OFF_VB), _prev8(OFF_QB // B_W), _prev8(OFF_KB // B_W),
                  _prev8(OFF_VB // B_W), _rb(GTR, LANE, OFF_BA // LANE), _whole((CONV_K, 3 * B_W)),
                  _whole((8, LANE))],
        out_specs=(_rb(GTR, B_W), _rb(GTR, B_W), _rb(GTR, B_W), _rb(GTR, LANE)),
        out_shape=(outs, outs, outs, jax.ShapeDtypeStruct((tt, LANE), F32)),
        compiler_params=_cp(("parallel",)), name=name)(proj, proj, proj, proj, proj, proj, proj, wconv, par)


def _gdn_pre_bwd(proj, wconv, par, dq, dk, dv, daux, *, name):
    tt = proj.shape[0]
    nb = tt // GTR

    def body(q_ref, k_ref, v_ref, qh_ref, kh_ref, vh_ref, qn_ref, kn_ref, vn_ref, ba_ref, w_ref, par_ref,
             dq_ref, dk_ref, dv_ref, dqn_ref, dkn_ref, dvn_ref, daux_ref, dx_ref, dba_ref, dw_ref, dpar_ref):
        i = pl.program_id(0)
        first, last = i == 0, i == nb - 1

        @pl.when(first)
        def _():
            dw_ref[...] = jnp.zeros_like(dw_ref)
            dpar_ref[...] = jnp.zeros_like(dpar_ref)

        groups = ((q_ref, qh_ref, qn_ref, dq_ref, dqn_ref), (k_ref, kh_ref, kn_ref, dk_ref, dkn_ref),
                  (v_ref, vh_ref, vn_ref, dv_ref, dvn_ref))
        for grp, (x_ref, h_ref, xn_ref, d_ref, dn_ref) in enumerate(groups):
            taps = _taps(w_ref, grp)
            xe = jnp.concatenate([jnp.where(first, 0.0, h_ref[...]), x_ref[...]], axis=0)
            sh = _shifts(xe, GTR)
            dc = _act_bwd(_conv(sh, taps), d_ref[...], grp)
            xe_n = jnp.concatenate([x_ref[GTR - 8:GTR, :], xn_ref[...]], axis=0)
            dcn = _act_bwd(_conv(_shifts(xe_n, 8), taps), dn_ref[...], grp)
            dce = jnp.concatenate([dc, jnp.where(last, 0.0, dcn)], axis=0)
            dx = taps[CONV_K - 1] * dc
            dw_ref[CONV_K - 1:CONV_K, grp * B_W:(grp + 1) * B_W] += _colsum(dc * sh[0])
            for s in range(1, CONV_K):
                dx = dx + taps[CONV_K - 1 - s] * pltpu.roll(dce, GTR + 8 - s, 0)[0:GTR]
                dw_ref[CONV_K - 1 - s:CONV_K - s, grp * B_W:(grp + 1) * B_W] += _colsum(dc * sh[s])
            dx_ref[:, grp * B_W:(grp + 1) * B_W] = dx.astype(BF16)
        ba = ba_ref[...]
        lane = lax.broadcasted_iota(jnp.int32, ba.shape, 1)
        bg, z = _gate_rows(ba, par_ref)
        daux_v = daux_ref[...]
        dg = _dot(_chunk_tri(GTR, upper=True), daux_v, HI)
        dgl = jnp.where((lane >= B_HEADS) & (lane < 2 * B_HEADS), dg, 0.0)
        da = dgl * (-jnp.exp(par_ref[0:1, :])) * _sigmoid(z)
        dbr = jnp.where(lane < B_HEADS, daux_v * bg * (1.0 - bg), 0.0)
        dba_ref[...] = (dbr + da).astype(BF16)
        dpar_ref[0:1, :] += _colsum(dgl * bg)
        dpar_ref[1:2, :] += _colsum(da)

    col = lambda off: _rb(GTR, B_W, off // B_W)
    row, rowl = _rb(GTR, B_W), _rb(GTR, LANE)
    return pl.pallas_call(
        body, grid=(nb,),
        in_specs=[col(OFF_QB), col(OFF_KB), col(OFF_VB),
                  _prev8(OFF_QB // B_W), _prev8(OFF_KB // B_W), _prev8(OFF_VB // B_W),
                  _next8(OFF_QB // B_W, nb), _next8(OFF_KB // B_W, nb), _next8(OFF_VB // B_W, nb),
                  _rb(GTR, LANE, OFF_BA // LANE), _whole((CONV_K, 3 * B_W)), _whole((8, LANE)),
                  row, row, row, _next8(0, nb), _next8(0, nb), _next8(0, nb), rowl],
        out_specs=(_rb(GTR, 3 * B_W), rowl, _whole((8, 3 * B_W)), _whole((8, LANE))),
        out_shape=(jax.ShapeDtypeStruct((tt, 3 * B_W), BF16), jax.ShapeDtypeStruct((tt, LANE), BF16),
                   jax.ShapeDtypeStruct((8, 3 * B_W), F32), jax.ShapeDtypeStruct((8, LANE), F32)),
        compiler_params=_cp(("arbitrary",)), name=name)(
            proj, proj, proj, proj, proj, proj, proj, proj, proj, proj, wconv, par, dq, dk, dv, dq, dk, dv, daux)


def _col(x, j):
    lane = lax.broadcasted_iota(jnp.int32, x.shape, 1)
    return jnp.sum(jnp.where(lane == j, x, 0.0), axis=-1, keepdims=True)


def _split(x):
    hi = x.astype(BF16)
    return hi, (x - hi.astype(F32)).astype(BF16)


def _dot3(a, b, tn=False):
    dot = _dot_tn if tn else _dot
    (ah, al), (bh, bl) = _split(a), _split(b)
    return dot(ah, bh) + (dot(ah, bl) + dot(al, bh))


def _chunk_masks():
    r = lax.broadcasted_iota(jnp.int32, (CH, CH), 0)
    c = lax.broadcasted_iota(jnp.int32, (CH, CH), 1)
    return r > c, r >= c


def _gc_rows(aux, nc):
    t = jnp.transpose(aux[:, B_HEADS:2 * B_HEADS].reshape(nc, CH, B_HEADS), (0, 2, 1))
    return jnp.concatenate([t, jnp.zeros_like(t)], axis=1).reshape(nc * 8, CH)


_CHUNK8 = lambda width, n=1: pl.BlockSpec((8 * n, width), lambda i: (i, 0))
_CHUNK4 = lambda a, b, n=1: pl.BlockSpec((B_HEADS * n, a, b), lambda i: (i, 0, 0))
NCH = 4


def _per_chunk(body, rows):
    def wrapped(*refs):
        for ci in range(NCH):
            body(*[r.at[pl.ds(ci * n, n)] for r, n in zip(refs, rows)])
    return wrapped


def _gdn_lower(k, aux, auxt, *, name):
    tt = k.shape[0]

    def body(k_ref, aux_ref, auxt_ref, l_ref):
        aux_v = aux_ref[...]
        strict, _ = _chunk_masks()
        khs = [k_ref[:, hd * B_DH:(hd + 1) * B_DH].astype(BF16) for hd in range(B_HEADS)]
        kks = [_dot_nt(kh, kh) for kh in khs]
        for hd in range(B_HEADS):
            diff = _col(aux_v, B_HEADS + hd) - auxt_ref[hd:hd + 1, :]
            dec = jnp.exp(jnp.where(strict, diff, NEG))
            l_ref[hd] = _col(aux_v, hd) * kks[hd] * dec

    return pl.pallas_call(
        _per_chunk(body, (CH, CH, 8, B_HEADS)), grid=(tt // CH // NCH,),
        in_specs=[_rb(NCH * CH, B_W), _rb(NCH * CH, LANE), _CHUNK8(CH, NCH)],
        out_specs=_CHUNK4(CH, CH, NCH),
        out_shape=jax.ShapeDtypeStruct((tt // CH * B_HEADS, CH, CH), F32),
        compiler_params=_cp(("parallel",)), name=name)(k, aux, auxt)


def _tri_inverse(lt, *, name):
    nb = lt.shape[2]

    def body(l_ref, t_ref):
        rowid = lax.broadcasted_iota(jnp.int32, (CH, nb), 0)

        def outer(i, carry):
            def inner(j, acc):
                return acc + l_ref[i, pl.ds(j, 1), :] * t_ref[j]

            acc = lax.fori_loop(0, i, inner, jnp.zeros((CH, nb), F32))
            t_ref[i] = jnp.where(rowid == i, 1.0, 0.0) - acc
            return carry

        lax.fori_loop(0, CH, outer, 0)

    return pl.pallas_call(body, out_shape=jax.ShapeDtypeStruct(lt.shape, F32),
                          in_specs=[pl.BlockSpec(memory_space=pltpu.VMEM)],
                          out_specs=pl.BlockSpec(memory_space=pltpu.VMEM),
                          compiler_params=_cp(), name=name)(lt)


def _gdn_gates(aux_v, aux_last, auxt_ref, hd):
    _, incl = _chunk_masks()
    beta = _col(aux_v, hd)
    gc = _col(aux_v, B_HEADS + hd)
    gl = _col(aux_last, B_HEADS + hd)
    dec = jnp.exp(jnp.where(incl, gc - auxt_ref[hd:hd + 1, :], NEG))
    return beta, gc, gl, jnp.exp(gc), dec


def _gdn_intra(q, k, v, aux, auxt, tinv, *, name):
    tt = q.shape[0]
    nc = tt // CH

    def body(q_ref, k_ref, v_ref, aux_ref, auxt_ref, t_ref, u0_ref, w_ref, qd_ref, kd_ref, qk_ref, gle_ref):
        aux_v = aux_ref[...]
        aux_last = aux_ref[CH - 1:CH, :]
        lane8 = lax.broadcasted_iota(jnp.int32, (8, LANE), 1)
        gle = jnp.zeros((8, LANE), F32)
        heads = range(B_HEADS)
        sls = [slice(hd * B_DH, (hd + 1) * B_DH) for hd in heads]
        gates = [_gdn_gates(aux_v, aux_last, auxt_ref, hd) for hd in heads]
        qk0 = [_dot_nt(q_ref[:, sls[hd]].astype(BF16), k_ref[:, sls[hd]].astype(BF16)) for hd in heads]
        u0 = [_dot3(t_ref[hd], v_ref[:, sls[hd]] * gates[hd][0]) for hd in heads]
        wk = [_dot3(t_ref[hd], k_ref[:, sls[hd]] * (gates[hd][0] * gates[hd][3])) for hd in heads]
        for hd in heads:
            sl = sls[hd]
            beta, gc, gl, egc, dec = gates[hd]
            qk_ref[hd] = (qk0[hd] * dec).astype(BF16)
            u0_ref[:, sl] = u0[hd]
            w_ref[:, sl] = wk[hd].astype(BF16)
            qd_ref[:, sl] = (q_ref[:, sl] * egc).astype(BF16)
            kd_ref[:, sl] = (k_ref[:, sl] * jnp.exp(gl - gc)).astype(BF16)
            gle = gle + jnp.where(lane8 == hd, jnp.exp(gl), 0.0)
        gle_ref[...] = gle

    row = _rb(NCH * CH, B_W)
    half = jax.ShapeDtypeStruct((tt, B_W), BF16)
    return pl.pallas_call(
        _per_chunk(body, (CH, CH, CH, CH, 8, B_HEADS, CH, CH, CH, CH, B_HEADS, 8)), grid=(nc // NCH,),
        in_specs=[row, row, row, _rb(NCH * CH, LANE), _CHUNK8(CH, NCH), _CHUNK4(CH, CH, NCH)],
        out_specs=(row, row, row, row, _CHUNK4(CH, CH, NCH), _CHUNK8(LANE, NCH)),
        out_shape=(jax.ShapeDtypeStruct((tt, B_W), F32), half, half, half,
                   jax.ShapeDtypeStruct((nc * B_HEADS, CH, CH), BF16), jax.ShapeDtypeStruct((nc * 8, LANE), F32)),
        compiler_params=_cp(("parallel",)), name=name)(q, k, v, aux, auxt, tinv)


def _gdn_scan_fwd(u0, w, qd, kd, qk, gle, *, name):
    tt = u0.shape[0]
    nc = tt // CH

    def body(u0_ref, w_ref, qd_ref, kd_ref, qk_ref, gle_ref, o_ref, ss_ref, u_ref, s_ref):
        @pl.when(pl.program_id(0) == 0)
        def _():
            s_ref[...] = jnp.zeros_like(s_ref)

        gle = gle_ref[0:1, :]
        heads = range(B_HEADS)
        sls = [slice(hd * B_DH, (hd + 1) * B_DH) for hd in heads]
        st = [s_ref[hd] for hd in heads]
        sb = [t.astype(BF16) for t in st]
        ws = [_dot(w_ref[:, sls[hd]], sb[hd]) for hd in heads]
        qs = [_dot(qd_ref[:, sls[hd]], sb[hd]) for hd in heads]
        ub = [(u0_ref[:, sls[hd]] - ws[hd]).astype(BF16) for hd in heads]
        ku = [_dot_tn(kd_ref[:, sls[hd]], ub[hd]) for hd in heads]
        qu = [_dot(qk_ref[hd], ub[hd]) for hd in heads]
        for hd in heads:
            ss_ref[hd] = st[hd]
            u_ref[:, sls[hd]] = ub[hd]
            o_ref[:, sls[hd]] = qs[hd] + qu[hd]
            s_ref[hd] = st[hd] * _col(gle, hd) + ku[hd]

    row = _rb(CH, B_W)
    return pl.pallas_call(
        body, grid=(nc,), in_specs=[row, row, row, row, _CHUNK4(CH, CH), _CHUNK8(LANE)],
        out_specs=(row, _CHUNK4(B_DH, B_DH), row),
        out_shape=(jax.ShapeDtypeStruct((tt, B_W), F32), jax.ShapeDtypeStruct((nc * B_HEADS, B_DH, B_DH), F32),
                   jax.ShapeDtypeStruct((tt, B_W), BF16)),
        scratch_shapes=[pltpu.VMEM((B_HEADS, B_DH, B_DH), F32)],
        compiler_params=_cp(("arbitrary",)), name=name)(u0, w, qd, kd, qk, gle)


def _gdn_scan_bwd(w, qd, kd, qk, gle, do, *, name):
    tt = w.shape[0]
    nc = tt // CH

    def body(w_ref, qd_ref, kd_ref, qk_ref, gle_ref, do_ref, du_ref, dss_ref, ds_ref):
        @pl.when(pl.program_id(0) == 0)
        def _():
            ds_ref[...] = jnp.zeros_like(ds_ref)

        gle = gle_ref[0:1, :]
        heads = range(B_HEADS)
        sls = [slice(hd * B_DH, (hd + 1) * B_DH) for hd in heads]
        dst = [ds_ref[hd] for hd in heads]
        dob = [do_ref[:, sls[hd]].astype(BF16) for hd in heads]
        kds = [_dot(kd_ref[:, sls[hd]], dst[hd].astype(BF16)) for hd in heads]
        qkd = [_dot_tn(qk_ref[hd], dob[hd]) for hd in heads]
        qdd = [_dot_tn(qd_ref[:, sls[hd]], dob[hd]) for hd in heads]
        du = [qkd[hd] + kds[hd] for hd in heads]
        wdu = [_dot_tn(w_ref[:, sls[hd]], du[hd].astype(BF16)) for hd in heads]
        for hd in heads:
            dss_ref[hd] = dst[hd]
            du_ref[:, sls[hd]] = du[hd]
            ds_ref[hd] = qdd[hd] + _col(gle, hd) * dst[hd] - wdu[hd]

    rev = lambda width: pl.BlockSpec((CH, width), lambda i: (nc - 1 - i, 0))
    rev4 = lambda a, b: pl.BlockSpec((B_HEADS, a, b), lambda i: (nc - 1 - i, 0, 0))
    return pl.pallas_call(
        body, grid=(nc,),
        in_specs=[rev(B_W), rev(B_W), rev(B_W), rev4(CH, CH), pl.BlockSpec((8, LANE), lambda i: (nc - 1 - i, 0)), rev(B_W)],
        out_specs=(rev(B_W), rev4(B_DH, B_DH)),
        out_shape=(jax.ShapeDtypeStruct((tt, B_W), F32), jax.ShapeDtypeStruct((nc * B_HEADS, B_DH, B_DH), F32)),
        scratch_shapes=[pltpu.VMEM((B_HEADS, B_DH, B_DH), F32)],
        compiler_params=_cp(("arbitrary",)), name=name)(w, qd, kd, qk, gle, do)


def _gdn_bwd(q, k, v, aux, auxt, tinv, u0, w, u, ss, dss, du, do, *, name):
    tt = q.shape[0]
    nc = tt // CH

    def body(q_ref, k_ref, v_ref, aux_ref, auxt_ref, t_ref, u0_ref, w_ref, u_ref, ss_ref, dss_ref, du_ref, do_ref,
             dq_ref, dk_ref, dv_ref, daux_ref):
        aux_v = aux_ref[...]
        aux_last = aux_ref[CH - 1:CH, :]
        lane = lax.broadcasted_iota(jnp.int32, (CH, LANE), 1)
        rowi = lax.broadcasted_iota(jnp.int32, (CH, 1), 0)
        strict, incl = _chunk_masks()
        daux = jnp.zeros((CH, LANE), F32)
        heads = range(B_HEADS)
        sls = [slice(hd * B_DH, (hd + 1) * B_DH) for hd in heads]
        gates = [_gdn_gates(aux_v, aux_last, auxt_ref, hd) for hd in heads]
        kbs = [k_ref[:, sl].astype(BF16) for sl in sls]
        qbs = [q_ref[:, sl].astype(BF16) for sl in sls]
        sbs = [ss_ref[hd].astype(BF16) for hd in heads]
        dsbs = [dss_ref[hd].astype(BF16) for hd in heads]
        dobs = [do_ref[:, sl].astype(BF16) for sl in sls]
        kks = [_dot_nt(kbs[hd], kbs[hd]) for hd in heads]
        qk0s = [_dot_nt(qbs[hd], kbs[hd]) for hd in heads]
        dq_decs = [_dot_nt(dobs[hd], sbs[hd]) for hd in heads]
        dqks = [_dot_nt(dobs[hd], u_ref[:, sls[hd]]) for hd in heads]
        dk_decs = [_dot_nt(u_ref[:, sls[hd]], dsbs[hd]) for hd in heads]
        dws = [-_dot_nt(du_ref[:, sls[hd]].astype(BF16), sbs[hd]) for hd in heads]
        drvs = [_dot3(t_ref[hd], du_ref[:, sls[hd]], tn=True) for hd in heads]
        drks = [_dot3(t_ref[hd], dws[hd], tn=True) for hd in heads]
        dls = [-(_dot_nt(drvs[hd].astype(BF16), u0_ref[:, sls[hd]].astype(BF16))
                 + _dot_nt(drks[hd].astype(BF16), w_ref[:, sls[hd]])) for hd in heads]
        ldecs = [jnp.where(strict, dls[hd], 0.0) * gates[hd][4] for hd in heads]
        dqkm = [jnp.where(incl, dqks[hd], 0.0) for hd in heads]
        dkks = [(ldecs[hd] * gates[hd][0]).astype(BF16) for hd in heads]
        dqk0s = [(dqkm[hd] * gates[hd][4]).astype(BF16) for hd in heads]
        ddecs = [ldecs[hd] * gates[hd][0] * kks[hd] + dqkm[hd] * (qk0s[hd] * gates[hd][4]) for hd in heads]
        dq_mm = [_dot(dqk0s[hd], kbs[hd]) for hd in heads]
        dk_mm = [_dot_tn(dqk0s[hd], qbs[hd]) + _dot(dkks[hd], kbs[hd]) + _dot_tn(dkks[hd], kbs[hd]) for hd in heads]
        dcols = [_col_from_rowsum(ddecs[hd]) for hd in heads]
        for hd in heads:
            sl = sls[hd]
            qh, kh, vh = q_ref[:, sl], k_ref[:, sl], v_ref[:, sl]
            beta, gc, gl, egc, dec = gates[hd]
            ekd, eg_last = jnp.exp(gl - gc), jnp.exp(gl)
            kk = kks[hd]
            st, dst = ss_ref[hd], dss_ref[hd]
            dq_dec, dk_dec = dq_decs[hd], dk_decs[hd]
            dgl = jnp.sum(jnp.sum(st * dst, axis=-1, keepdims=True), axis=0, keepdims=True) * eg_last
            drv, drk = drvs[hd], drks[hd]
            dv_ref[:, sl] = drv * beta
            rk = jnp.sum(drk * kh, axis=-1, keepdims=True)
            dbeta = jnp.sum(drv * vh, axis=-1, keepdims=True) + rk * egc
            dgc = rk * beta * egc
            dk = drk * (beta * egc)
            ldec, ddec = ldecs[hd], ddecs[hd]
            dbeta = dbeta + jnp.sum(ldec * kk, axis=-1, keepdims=True)
            dq = dq_mm[hd] + dq_dec * egc
            dk = dk + dk_mm[hd] + dk_dec * ekd
            dgc = dgc + jnp.sum(ddec, axis=-1, keepdims=True) - dcols[hd]
            dgc = dgc + jnp.sum(dq_dec * qh, axis=-1, keepdims=True) * egc
            kd = jnp.sum(dk_dec * kh, axis=-1, keepdims=True) * ekd
            dgc = dgc - kd
            dgc = dgc + jnp.where(rowi == CH - 1, jnp.sum(kd, axis=0, keepdims=True) + dgl, 0.0)
            dq_ref[:, sl] = dq
            dk_ref[:, sl] = dk
            daux = daux + jnp.where(lane == hd, dbeta, 0.0) + jnp.where(lane == B_HEADS + hd, dgc, 0.0)
        daux_ref[...] = daux

    row = _rb(NCH * CH, B_W)
    outs = jax.ShapeDtypeStruct((tt, B_W), F32)
    return pl.pallas_call(
        _per_chunk(body, (CH, CH, CH, CH, 8, B_HEADS, CH, CH, CH, B_HEADS, B_HEADS, CH, CH, CH, CH, CH, CH)),
        grid=(nc // NCH,),
        in_specs=[row, row, row, _rb(NCH * CH, LANE), _CHUNK8(CH, NCH), _CHUNK4(CH, CH, NCH), row, row, row,
                  _CHUNK4(B_DH, B_DH, NCH), _CHUNK4(B_DH, B_DH, NCH), row, row],
        out_specs=(row, row, row, _rb(NCH * CH, LANE)),
        out_shape=(outs, outs, outs, jax.ShapeDtypeStruct((tt, LANE), F32)),
        compiler_params=_cp(("parallel",)), name=name)(q, k, v, aux, auxt, tinv, u0, w, u, ss, dss, du, do)


def _col_from_rowsum(m):
    hi, lo = _split(m)
    ones = jnp.ones((CH, LANE), BF16)
    return (_dot_tn(hi, ones) + _dot_tn(lo, ones))[:, 0:1]


def _gdn_post_fwd(o, proj, gn, *, name, tr=256):
    tt = o.shape[0]

    def body(o_ref, z_ref, g_ref, y_ref):
        for hd in range(B_HEADS):
            sl = slice(hd * B_DH, (hd + 1) * B_DH)
            oh = o_ref[:, sl]
            r = lax.rsqrt(jnp.mean(oh * oh, axis=-1, keepdims=True) + EPS)
            y_ref[:, sl] = (oh * r * g_ref[...] * _silu(z_ref[:, sl])).astype(BF16)

    return pl.pallas_call(body, grid=(tt // tr,), in_specs=[_rb(tr, B_W), _rb(tr, B_W, OFF_ZB // B_W), _whole((1, B_DH))],
                          out_specs=_rb(tr, B_W), out_shape=jax.ShapeDtypeStruct((tt, B_W), BF16),
                          compiler_params=_cp(("parallel",)), name=name)(o, proj, gn)


def _gdn_post_bwd(o, proj, gn, dy, *, name, tr=256):
    tt = o.shape[0]

    def body(o_ref, z_ref, g_ref, dy_ref, do_ref, dz_ref, dg_ref):
        @pl.when(pl.program_id(0) == 0)
        def _():
            dg_ref[...] = jnp.zeros_like(dg_ref)

        g = g_ref[...]
        for hd in range(B_HEADS):
            sl = slice(hd * B_DH, (hd + 1) * B_DH)
            oh, zh, dyh = o_ref[:, sl], z_ref[:, sl], dy_ref[:, sl]
            r = lax.rsqrt(jnp.mean(oh * oh, axis=-1, keepdims=True) + EPS)
            a = oh * r
            s = _silu(zh)
            da = dyh * g * s
            dg_ref[0:1, :] += _colsum(dyh * a * s)
            dz_ref[:, sl] = (dyh * a * g * _dsilu(zh)).astype(BF16)
            do_ref[:, sl] = r * (da - a * jnp.mean(da * a, axis=-1, keepdims=True))

    return pl.pallas_call(
        body, grid=(tt // tr,), in_specs=[_rb(tr, B_W), _rb(tr, B_W, OFF_ZB // B_W), _whole((1, B_DH)), _rb(tr, B_W)],
        out_specs=(_rb(tr, B_W), _rb(tr, B_W), _whole((8, B_DH))),
        out_shape=(jax.ShapeDtypeStruct((tt, B_W), F32), jax.ShapeDtypeStruct((tt, B_W), BF16),
                   jax.ShapeDtypeStruct((8, B_DH), F32)),
        compiler_params=_cp(("arbitrary",)), name=name)(o, proj, gn, dy)


def _adamw(parts, w, m, v, own=None, sel=None, *, name, tr=256):
    npart, nl, r, c = parts.shape
    tr = max([t for t in range(8, min(r, tr) + 1, 8) if r % t == 0], default=r)
    tc = c if tr < r or r <= 256 or c % 256 else 256
    c1, c2 = 1.0 - ADAM_B1 ** ADAM_STEP, 1.0 - ADAM_B2 ** ADAM_STEP

    def body(*refs):
        if own is None:
            p_ref, w_ref, m_ref, v_ref, g_ref, d_ref, mo_ref, vo_ref = refs
            part = lambda i: p_ref[i].astype(F32)
        else:
            p_ref, w_ref, m_ref, v_ref, own_ref, sel_ref, g_ref, d_ref, mo_ref, vo_ref = refs
            part = lambda i: jnp.where(sel_ref[i:i + 1, 0:1] > 0.5, own_ref[...].astype(F32), p_ref[i].astype(F32))
        g = part(0)
        for i in range(1, npart):
            g = g + part(i)
        mn = ADAM_B1 * m_ref[...] + (1.0 - ADAM_B1) * g
        vn = ADAM_B2 * v_ref[...] + (1.0 - ADAM_B2) * (g * g)
        g_ref[...] = g
        mo_ref[...] = mn
        vo_ref[...] = vn
        d_ref[...] = -ADAM_LR * ((mn / c1) / (jnp.sqrt(vn / c2) + ADAM_EPS) + ADAM_WD * w_ref[...])

    row = pl.BlockSpec((None, tr, tc), lambda l, i, j: (l, i, j))
    out = jax.ShapeDtypeStruct((nl, r, c), F32)
    ins, in_specs = [parts, w, m, v], [pl.BlockSpec((npart, None, tr, tc), lambda l, i, j: (0, l, i, j)), row, row, row]
    if own is not None:
        ins += [own, sel]
        in_specs += [row, pl.BlockSpec((N_DEV, LANE), lambda l, i, j: (0, 0))]
    return pl.pallas_call(body, grid=(nl, r // tr, c // tc), in_specs=in_specs, out_specs=(row, row, row, row),
                          out_shape=(out, out, out, out), compiler_params=_cp(("parallel", "parallel", "parallel")),
                          name=name)(*ins)


def _peer(k):
    x, y, c = lax.axis_index("x"), lax.axis_index("y"), lax.axis_index("c")
    return ((1 - x) if k & 4 else x, (1 - y) if k & 2 else y, (1 - c) if k & 1 else c)


def _my_index():
    return 4 * lax.axis_index("x") + 2 * lax.axis_index("y") + lax.axis_index("c")


def _index_of(p):
    return 4 * p[0] + 2 * p[1] + p[2]


def _all_gather(xs, *, name):
    n = len(xs)

    def body(*refs):
        x_refs, o_refs = refs[:n], refs[n:2 * n]
        send, recv, loc = refs[2 * n:]
        me = _my_index()
        copies = []
        for a in range(n):
            cp = pltpu.make_async_copy(x_refs[a], o_refs[a].at[me], loc.at[a])
            cp.start()
            copies.append(cp)
        rdmas = []
        for a in range(n):
            for k in range(1, N_DEV):
                r = pltpu.make_async_remote_copy(
                    src_ref=x_refs[a], dst_ref=o_refs[a].at[me], send_sem=send.at[a, k - 1], recv_sem=recv.at[a, k - 1],
                    device_id=_peer(k), device_id_type=pl.DeviceIdType.MESH)
                r.start()
                rdmas.append(r)
        for a in range(n):
            for k in range(1, N_DEV):
                pltpu.make_async_remote_copy(
                    src_ref=x_refs[a], dst_ref=o_refs[a].at[_index_of(_peer(k))], send_sem=send.at[a, k - 1],
                    recv_sem=recv.at[a, k - 1], device_id=_peer(k), device_id_type=pl.DeviceIdType.MESH).wait_recv()
        for r in rdmas:
            r.wait_send()
        for cp in copies:
            cp.wait()

    any_spec = pl.BlockSpec(memory_space=pl.ANY)
    return pl.pallas_call(
        body, in_specs=[any_spec] * n, out_specs=tuple([any_spec] * n),
        out_shape=tuple(jax.ShapeDtypeStruct((N_DEV,) + x.shape, x.dtype) for x in xs),
        scratch_shapes=[pltpu.SemaphoreType.DMA((n, N_DEV - 1)), pltpu.SemaphoreType.DMA((n, N_DEV - 1)),
                        pltpu.SemaphoreType.DMA((n,))],
        name=name)(*xs)


def _all_gather_two_level(xs, *, name):
    n = len(xs)

    def body(*refs):
        x_refs, o_refs = refs[:n], refs[n:2 * n]
        send, recv, loc = refs[2 * n:]
        x, y, c = lax.axis_index("x"), lax.axis_index("y"), lax.axis_index("c")
        me, sibling = (x, y, c), (x, y, 1 - c)
        chips = [(1 - x, y), (x, 1 - y), (1 - x, 1 - y)]

        def copy(a, k, block, to, src=None):
            dst = o_refs[a].at[_index_of(block)]
            return pltpu.make_async_remote_copy(src_ref=dst if src is None else src, dst_ref=dst, send_sem=send.at[a, k],
                                                recv_sem=recv.at[a, k], device_id=to, device_id_type=pl.DeviceIdType.MESH)

        mine = [pltpu.make_async_copy(x_refs[a], o_refs[a].at[_index_of(me)], loc.at[a]) for a in range(n)]
        first = [copy(a, 0, me, sibling, src=x_refs[a]) for a in range(n)]
        first += [copy(a, 1 + j, me, (*chip, c), src=x_refs[a]) for a in range(n) for j, chip in enumerate(chips)]
        for cp in mine + first:
            cp.start()
        passed = []
        for a in range(n):
            for j, chip in enumerate(chips):
                copy(a, 1 + j, (*chip, c), me).wait_recv()
                passed.append(copy(a, 4 + j, (*chip, c), sibling))
                passed[-1].start()
        for a in range(n):
            copy(a, 0, sibling, me).wait_recv()
            for j, chip in enumerate(chips):
                copy(a, 4 + j, (*chip, 1 - c), me).wait_recv()
        for cp in first + passed:
            cp.wait_send()
        for cp in mine:
            cp.wait()

    any_spec = pl.BlockSpec(memory_space=pl.ANY)
    return pl.pallas_call(
        body, in_specs=[any_spec] * n, out_specs=tuple([any_spec] * n),
        out_shape=tuple(jax.ShapeDtypeStruct((N_DEV,) + t.shape, t.dtype) for t in xs),
        scratch_shapes=[pltpu.SemaphoreType.DMA((n, N_DEV - 1)), pltpu.SemaphoreType.DMA((n, N_DEV - 1)),
                        pltpu.SemaphoreType.DMA((n,))],
        name=name)(*xs)


_HBM = pl.BlockSpec(memory_space=pltpu.HBM)
_SEM = pl.BlockSpec(memory_space=pltpu.SEMAPHORE)
_EFFECT = pltpu.SideEffectType.DATAFLOW_SIDE_EFFECTING


def _split_copy(src_ref, land_ref, send, recv, a, k, scatter, slot, sending):
    me, peer = _my_index(), _index_of(_peer(k))
    src = src_ref.at[peer if sending else me] if scatter else src_ref
    land = land_ref.at[me if sending else peer]
    if slot is not None:
        land = land.at[slot]
    sem = a * (N_DEV - 1) + k - 1
    return pltpu.make_async_remote_copy(src_ref=src, dst_ref=land, send_sem=send.at[sem], recv_sem=recv.at[sem],
                                        device_id=_peer(k), device_id_type=pl.DeviceIdType.MESH)


def _exchange_start(srcs, lands, after, *, scatter, slot=None, name):
    n = len(srcs)

    def body(*refs):
        src_refs, land_refs = refs[:n], refs[n:2 * n]
        send, recv, token = refs[2 * n + 1], refs[2 * n + 2], refs[-1]
        for a in range(n):
            for k in range(1, N_DEV):
                _split_copy(src_refs[a], land_refs[a], send, recv, a, k, scatter, slot, True).start()
        token[...] = jnp.zeros_like(token)

    hbm = lambda t: pltpu.HBM(t.shape, t.dtype)
    sems = pltpu.SemaphoreType.DMA((n * (N_DEV - 1),))
    out = pl.pallas_call(
        body, name=name,
        out_shape=(sems, sems, *[hbm(t) for t in srcs], *[hbm(t) for t in lands], jax.ShapeDtypeStruct((8, LANE), F32)),
        in_specs=[_HBM] * (2 * n) + [pl.BlockSpec(memory_space=pl.ANY)],
        out_specs=(_SEM, _SEM, *[_HBM] * (2 * n), pl.BlockSpec(memory_space=pltpu.VMEM)),
        input_output_aliases={i: 2 + i for i in range(2 * n)},
        compiler_params=pltpu.CompilerParams(has_side_effects=_EFFECT),
    )(*[pltpu.with_memory_space_constraint(t, pltpu.HBM) for t in (*srcs, *lands)], after)
    return out[0], out[1], out[2:2 + n], out[2 + n:2 + 2 * n], out[-1]


def _exchange_wait(send, recv, srcs, lands, after, *, scatter, slot=None, name):
    n = len(srcs)

    def body(*refs):
        src_refs, land_refs = refs[:n], refs[n:2 * n]
        send_ref, recv_ref = refs[2 * n], refs[2 * n + 1]
        for a in range(n):
            for k in range(1, N_DEV):
                _split_copy(src_refs[a], land_refs[a], send_ref, recv_ref, a, k, scatter, slot, True).wait_send()
                _split_copy(src_refs[a], land_refs[a], send_ref, recv_ref, a, k, scatter, slot, False).wait_recv()

    hbm = lambda t: pltpu.HBM(t.shape, t.dtype)
    out = pl.pallas_call(
        body, name=name, out_shape=(*[hbm(t) for t in srcs], *[hbm(t) for t in lands]),
        in_specs=[_HBM] * (2 * n) + [_SEM, _SEM, pl.BlockSpec(memory_space=pl.ANY)],
        out_specs=tuple([_HBM] * (2 * n)), input_output_aliases={i: i for i in range(2 * n)},
        compiler_params=pltpu.CompilerParams(has_side_effects=_EFFECT),
    )(*srcs, *lands, send, recv, after)
    return out[:n], out[n:]


def _win_to_mine(wt):
    pad = jnp.zeros((IN_PAD - IN_DIM,) + wt.shape[1:], wt.dtype)
    return jnp.concatenate([wt[3592:5640], wt[0:3584], wt[3584:3592], pad], axis=0)


def _win_from_mine(gt):
    return jnp.concatenate([gt[2048:5632], gt[5632:5640], gt[0:2048]], axis=0)


def _pad_rows(a, mult=8):
    r = (-a.shape[0]) % mult
    return a if r == 0 else jnp.concatenate([a, jnp.zeros((r,) + a.shape[1:], a.dtype)], axis=0)


def _lanes(vec, start):
    return jnp.zeros((1, LANE), F32).at[0, start:start + vec.shape[0]].set(vec)


def _small_spec(depth):
    return (("b_ada", (depth, 6 * D)), ("norm1_g", (depth, D)), ("norm2_g", (depth, D)),
            ("rel_table", (depth, A_HEADS, 2 * A_MAX_REL + 1)), ("a_log", (depth, B_HEADS)),
            ("dt_bias", (depth, B_HEADS)), ("gdn_norm_g", (depth, B_DH)), ("final_g", (D,)))


def _pack_small(d, extra, depth):
    spec = _small_spec(depth)
    rows = -(-(sum(math.prod(s) for _, s in spec) + 1) // (8 * LANE)) * 8
    flat = jnp.concatenate([d[n].reshape(-1).astype(F32) for n, _ in spec] + [extra.reshape(-1)])
    flat = jnp.concatenate([flat, jnp.zeros((rows * LANE - flat.shape[0],), F32)])
    return flat.reshape(rows, LANE)


def _unpack_small(p, depth):
    flat = p.reshape(-1)
    out, off = {}, 0
    for n, s in _small_spec(depth):
        sz = math.prod(s)
        out[n] = flat[off:off + sz].reshape(s)
        off += sz
    return out, flat[off]


def kernel(x, c, w_ada, b_ada, norm1_g, norm2_g, w_in, rel_table, w_conv, a_log, dt_bias, gdn_norm_g, w_branch_a, w_branch_b, w_out, w_ffn_in, w_ffn_out, final_g, loss_target, m_w_ada, m_b_ada, m_norm1_g, m_norm2_g, m_w_in, m_rel_table, m_w_conv, m_a_log, m_dt_bias, m_gdn_norm_g, m_w_branch_a, m_w_branch_b, m_w_out, m_w_ffn_in, m_w_ffn_out, m_final_g, v_w_ada, v_b_ada, v_norm1_g, v_norm2_g, v_w_in, v_rel_table, v_w_conv, v_a_log, v_dt_bias, v_gdn_norm_g, v_w_branch_a, v_w_branch_b, v_w_out, v_w_ffn_in, v_w_ffn_out, v_final_g):
    tt = x.shape[1]
    x0 = x[0]
    tgt = loss_target[0]
    me = _my_index()
    depth = w_in.shape[0]

    tr_ = lambda t: jnp.transpose(t, (0, 2, 1))
    shards = [tr_(w_in).astype(BF16), w_branch_a.astype(BF16), w_branch_b.astype(BF16), w_out.astype(BF16),
              tr_(w_ffn_in).astype(BF16), w_ffn_out.astype(BF16), w_conv]
    names = ("win", "wa", "wb", "wout", "wfi", "wfo", "wconv")
    early, late, every = (0, 6), (1, 2, 3, 4, 5), tuple(range(7))
    first = _all_gather_two_level([shards[i][0] for i in early] + [_pad_rows(c)], name="gather_first")
    c_all = first[-1][:, 0, :]
    is_me = lax.broadcasted_iota(jnp.int32, (N_DEV, 1, 1), 0) == me

    def unpack(idx, g):
        cols = lambda t: jnp.transpose(t, (1, 0, 2)).reshape(t.shape[1], N_DEV * t.shape[2])
        rows = lambda t: t.reshape(N_DEV * t.shape[1], t.shape[2])
        how = (lambda t: _win_to_mine(rows(t)), cols, cols, rows, rows, rows, cols)
        return {names[i]: how[i](t) for i, t in zip(idx, g)}

    def gather_start(l, idx, after, tag=""):
        srcs = [shards[i][l] for i in idx]
        lands = [lax.empty((N_DEV,) + t.shape, t.dtype) for t in srcs]
        return _exchange_start(srcs, lands, after, scatter=False, name=f"gather_start_{l}{tag}")

    def gather_wait(l, idx, pending, after, tag=""):
        send, recv, srcs, lands, _ = pending
        srcs, lands = _exchange_wait(send, recv, srcs, lands, after, scatter=False, name=f"gather_wait_{l}{tag}")
        return unpack(idx, [jnp.where(is_me, t[None], g) for g, t in zip(lands, srcs)])

    weights = [unpack(early, first[:-1])] + [None] * (depth - 1)
    pending0 = gather_start(0, late, first[-1], "_rest")
    pending = gather_start(1, every, pending0[-1]) if depth > 1 else None
    cond = c_all * (1.0 / (1.0 + jnp.exp(-c_all)))
    cond = _pad_rows(cond, 16)

    mod_cols = jnp.stack([_mm(cond, w_ada[l], name="mod_mm")[:N_DEV] for l in range(depth)])
    (g_mod,) = _all_gather([mod_cols], name="gather_mod")
    mod_all = jnp.transpose(g_mod, (1, 2, 0, 3)).reshape(depth, N_DEV, 6 * D)
    mod = lax.dynamic_index_in_dim(mod_all, me, axis=1, keepdims=False) + b_ada
    mods = mod.reshape(depth, 6, 1, D)

    n1g, n2g = norm1_g.reshape(depth, 1, D), norm2_g.reshape(depth, 1, D)
    gng = gdn_norm_g.reshape(depth, 1, B_DH)
    fg = final_g.reshape(1, D)

    saved = []
    tok = (pending if pending is not None else pending0)[-1][0, 0]
    xin, h1 = _adaln_fwd(x0, n1g[0], mods[0, 1] + tok, mods[0, 0], name="adaln1_first")
    for l in range(depth):
        sh1, sc1, gt1, sh2, sc2, gt2 = (mods[l, i] for i in range(6))
        wl = weights[l]
        proj = _mm(h1, wl["win"], tb=True, name="proj_mm", tm=2048, tn=1152)
        kpad, vpad = _kv_pad(proj, name="kv_pad")
        diag, bias_vjp = jax.vjp(_bias_diagonals, rel_table[l])
        bias = _bias_fwd(diag, name="bias_fwd")
        ya, lse = _attn_fwd(proj, kpad, vpad, bias, name="attn_fwd")
        par = jnp.concatenate([_lanes(a_log[l], B_HEADS), _lanes(dt_bias[l], B_HEADS), jnp.zeros((6, LANE), F32)], axis=0)
        qn, kn, vn, aux = _gdn_pre_fwd(proj, wl["wconv"], par, name="gdn_pre_fwd")
        auxt = _gc_rows(aux, tt // CH)
        lower = _gdn_lower(kn, aux, auxt, name="gdn_lower")
        tinv = jnp.transpose(_tri_inverse(jnp.transpose(lower, (1, 2, 0)), name="gdn_tri_inverse"), (2, 0, 1))
        u0, wg, qd, kd, qk, gle = _gdn_intra(qn, kn, vn, aux, auxt, tinv, name="gdn_intra")
        og, ss, ug = _gdn_scan_fwd(u0, wg, qd, kd, qk, gle, name="gdn_scan_fwd")
        yb = _gdn_post_fwd(og, proj, gng[l], name="gdn_post_fwd")
        if l == 0:
            wl.update(gather_wait(0, late, pending0, yb, "_rest"))
        pa, pb, merged = _branch_merge(ya, yb, wl["wa"], wl["wb"], proj, name="branch_merge")
        t1, x2, h2 = _out_adaln(merged, wl["wout"], xin, gt1, n2g[l], sc2, sh2, name="out_adaln2")
        gu, act = _ffn_in_swiglu(h2, wl["wfi"], name="ffn_in_swiglu")
        saved.append(dict(xin=xin, h1=h1, proj=proj, kpad=kpad, vpad=vpad, bias=bias, bias_vjp=bias_vjp, ya=ya, lse=lse,
                          par=par, qn=qn, kn=kn, vn=vn, aux=aux, auxt=auxt, tinv=tinv, ss=ss, og=og, yb=yb, pa=pa, pb=pb,
                          u0=u0, wg=wg, qd=qd, kd=kd, qk=qk, gle=gle, ug=ug,
                          merged=merged, t1=t1, x2=x2, h2=h2, gu=gu, act=act))
        if l + 1 < depth:
            weights[l + 1] = gather_wait(l + 1, every, pending, act)
            pending = gather_start(l + 2, every, weights[l + 1]["wconv"]) if l + 2 < depth else None
            tok = pending[-1][0, 0] if pending is not None else 0.0
            t2, xin, h1 = _out_adaln(act, wl["wfo"], x2, gt2, n1g[l + 1], mods[l + 1, 1] + tok, mods[l + 1, 0],
                                     tk=FTN, name="ffn_out_adaln1")
        else:
            t2 = _mm(act, wl["wfo"], name="ffn_out_mm", tk=FTN)
        saved[-1]["t2"] = t2

    s = saved[-1]
    dx, dt2, st = _loss_head(s["x2"], s["t2"], mods[depth - 1, 5], fg, tgt, name="loss_head")
    loss_part = st[4, 0]
    small_g = {"final_g": st[0]}
    dmod_rows = [None] * depth
    for n in ("norm1_g", "norm2_g", "rel_table", "a_log", "dt_bias", "gdn_norm_g"):
        small_g[n] = [None] * depth
    dgt2 = st[3]
    cols_slabs = lambda g: jnp.transpose(g.reshape(g.shape[0], N_DEV, g.shape[1] // N_DEV), (1, 0, 2))
    rows_slabs = lambda g: g.reshape(N_DEV, g.shape[0] // N_DEV, g.shape[1])
    mix, ffn = (0, 1, 2, 3, 6), (4, 5)
    lands = {kind: [lax.empty((N_DEV,) + shards[i].shape, shards[i].dtype) for i in idx]
             for kind, idx in (("mix", mix), ("ffn", ffn))}
    own = {kind: [None] * depth for kind in lands}
    pending_s = {kind: None for kind in lands}

    def scatter(kind, l, srcs, after):
        if pending_s[kind] is not None:
            done, lands[kind] = _exchange_wait(*pending_s[kind][:4], after, scatter=True, slot=l + 1,
                                               name=f"scatter_wait_{kind}_{l + 1}")
            own[kind][l + 1] = [lax.dynamic_index_in_dim(t, me, 0, keepdims=False) for t in done]
        pending_s[kind] = _exchange_start(srcs, lands[kind], after, scatter=True, slot=l, name=f"scatter_start_{kind}_{l}")
        return pending_s[kind][-1][0, 0]

    for l in reversed(range(depth)):
        s, wl = saved[l], weights[l]
        sh1, sc1, gt1, sh2, sc2, gt2 = (mods[l, i] for i in range(6))
        gw_fo = _mm(s["act"], dt2, ta=True, out_dtype=BF16, name="ffn_out_dw", tm=1408)
        dgu = _ffn_out_bwd_swiglu(dt2, wl["wfo"], s["gu"], name="ffn_out_bwd_swiglu")
        gw_fi = _mm(dgu, s["h2"], ta=True, out_dtype=BF16, name="ffn_in_dw", tm=1408)
        sc2 = sc2 + scatter("ffn", l, [rows_slabs(gw_fi), rows_slabs(gw_fo)], gw_fi)
        dx, dt1, st2 = _mm_adaln_bwd(dgu, wl["wfi"], s["x2"], n2g[l], sc2, sh2, dx, s["t1"], gt1, tk=FTN,
                                     name="ffn_in_dx_adaln2")
        gw_out = _mm(s["merged"], dt1, ta=True, out_dtype=BF16, name="out_dw")
        dgates, dpa, dpb = _out_bwd_merge(dt1, wl["wout"], s["proj"], s["pa"], s["pb"], name="out_bwd_merge")
        gw_a = _mm(s["ya"], dpa, ta=True, out_dtype=BF16, name="branch_a_dw")
        gw_b = _mm(s["yb"], dpb, ta=True, out_dtype=BF16, name="branch_b_dw")
        dya = _mm(dpa, wl["wa"], tb=True, name="branch_a_dx")
        dyb = _mm(dpb, wl["wb"], tb=True, name="branch_b_dx")
        dqa, dka, dva, dbias = _attn_bwd(s["proj"], s["kpad"], s["vpad"], s["bias"], s["ya"], s["lse"], dya,
                                             name="attn_bwd")
        ddiag = jnp.roll(_bias_bwd(dbias, name="bias_bwd"), -(QB - 1), axis=1)
        small_g["rel_table"][l] = s["bias_vjp"](ddiag)[0]
        dog, dz, dgn = _gdn_post_bwd(s["og"], s["proj"], gng[l], dyb, name="gdn_post_bwd")
        small_g["gdn_norm_g"][l] = dgn[0]
        dug, dss = _gdn_scan_bwd(s["wg"], s["qd"], s["kd"], s["qk"], s["gle"], dog, name="gdn_scan_bwd")
        dqn, dkn, dvn, daux = _gdn_bwd(s["qn"], s["kn"], s["vn"], s["aux"], s["auxt"], s["tinv"], s["u0"], s["wg"],
                                       s["ug"], s["ss"], dss, dug, dog, name="gdn_bwd")
        dqkv, dba, dwc, dpar = _gdn_pre_bwd(s["proj"], wl["wconv"], s["par"], dqn, dkn, dvn, daux, name="gdn_pre_bwd")
        small_g["a_log"][l] = dpar[0, B_HEADS:2 * B_HEADS]
        small_g["dt_bias"][l] = dpar[1, B_HEADS:2 * B_HEADS]
        dproj = jnp.concatenate([dgates, dqa, dka, dva, dqkv, dz, dba], axis=1)
        gw_in = _mm(dproj, s["h1"], ta=True, out_dtype=BF16, name="proj_dw", tm=1152)
        mix_srcs = [rows_slabs(_win_from_mine(gw_in)), cols_slabs(gw_a), cols_slabs(gw_b), rows_slabs(gw_out),
                    cols_slabs(dwc[0:CONV_K])]
        if l > 0:
            sc1 = sc1 + scatter("mix", l, mix_srcs, gw_in)
        if l > 0:
            p = saved[l - 1]
            dx, dt2, st1 = _mm_adaln_bwd(dproj, wl["win"], s["xin"], n1g[l], sc1, sh1, dx, p["t2"], mods[l - 1, 5],
                                         tk=1152, name="proj_dx_adaln1")
        else:
            dx, st1 = _mm_adaln_bwd(dproj, wl["win"], s["xin"], n1g[l], sc1, sh1, dx, tk=1152,
                                    name="proj_dx_adaln1_first")
        small_g["norm1_g"][l], small_g["norm2_g"][l] = st1[0], st2[0]
        dmod_rows[l] = jnp.concatenate([st1[2], st1[1], st2[3], st2[2], st2[1], dgt2])
        if l > 0:
            dgt2 = st1[3]
    grad_x = dx[None]

    small_local = {n: (jnp.stack(vs) if isinstance(vs, list) else vs) for n, vs in small_g.items()}
    small_local["b_ada"] = jnp.stack(dmod_rows)
    (g_small,) = _all_gather([_pack_small(small_local, loss_part, depth)], name="gather_small")
    tok = scatter("mix", 0, mix_srcs, g_small)
    wsm = _pack_small(dict(b_ada=b_ada, norm1_g=norm1_g, norm2_g=norm2_g, rel_table=rel_table, a_log=a_log,
                           dt_bias=dt_bias, gdn_norm_g=gdn_norm_g, final_g=final_g), jnp.zeros((1,), F32) + tok, depth)
    msm = _pack_small(dict(b_ada=m_b_ada, norm1_g=m_norm1_g, norm2_g=m_norm2_g, rel_table=m_rel_table, a_log=m_a_log,
                           dt_bias=m_dt_bias, gdn_norm_g=m_gdn_norm_g, final_g=m_final_g), jnp.zeros((1,), F32), depth)
    vsm = _pack_small(dict(b_ada=v_b_ada, norm1_g=v_norm1_g, norm2_g=v_norm2_g, rel_table=v_rel_table, a_log=v_a_log,
                           dt_bias=v_dt_bias, gdn_norm_g=v_gdn_norm_g, final_g=v_final_g), jnp.ones((1,), F32), depth)
    sm = [_unpack_small(t, depth) for t in _adamw(g_small[:, None], wsm[None], msm[None], vsm[None], name="adamw_small")]
    loss = sm[0][1]

    dmod_all = g_small.reshape(N_DEV, -1)[:, :depth * 6 * D].reshape(N_DEV, depth, 6 * D)
    dmod_mine = lax.dynamic_slice_in_dim(dmod_all, me * (6 * D // N_DEV), 6 * D // N_DEV, axis=2)
    g_ada = jnp.stack([_mm(cond + tok, _pad_rows(dmod_mine[:, l], 16), ta=True, name="ada_dw") for l in range(depth)])

    got, mine = {}, {}
    sel = jnp.broadcast_to(jnp.where(is_me[:, :, 0], 1.0, 0.0), (N_DEV, LANE)).astype(F32) + tok

    def finish(kind, idx, after):
        done, lands[kind] = _exchange_wait(*pending_s[kind][:4], after, scatter=True, slot=0, name=f"scatter_wait_{kind}_0")
        own[kind][0] = [lax.dynamic_index_in_dim(t, me, 0, keepdims=False) for t in done]
        for a, i in enumerate(idx):
            got[i] = lands[kind][a]
            mine[i] = jnp.stack([own[kind][l][a] for l in range(depth)])

    def upd(i, w, m, v, name):
        if i in (0, 4):
            return [tr_(t) for t in _adamw(got[i], tr_(w), tr_(m), tr_(v), mine[i], sel, name=name)]
        return _adamw(got[i], w, m, v, mine[i], sel, name=name)

    finish("ffn", ffn, g_ada)
    res = {
        "w_ada": _adamw(g_ada[None], w_ada, m_w_ada, v_w_ada, name="adamw_w_ada"),
        "w_ffn_in": upd(4, w_ffn_in, m_w_ffn_in, v_w_ffn_in, "adamw_w_ffn_in"),
        "w_ffn_out": upd(5, w_ffn_out, m_w_ffn_out, v_w_ffn_out, "adamw_w_ffn_out"),
    }
    done_first = (res["w_ada"][1][0, 0, 0] + res["w_ffn_in"][1][0, 0, 0] + res["w_ffn_out"][1][0, 0, 0] + sm[1][1])
    finish("mix", mix, jnp.zeros((8, LANE), F32) + done_first)
    res.update({
        "w_in": upd(0, w_in, m_w_in, v_w_in, "adamw_w_in"),
        "w_conv": upd(6, w_conv, m_w_conv, v_w_conv, "adamw_w_conv"),
        "w_branch_a": upd(1, w_branch_a, m_w_branch_a, v_w_branch_a, "adamw_w_branch_a"),
        "w_branch_b": upd(2, w_branch_b, m_w_branch_b, v_w_branch_b, "adamw_w_branch_b"),
        "w_out": upd(3, w_out, m_w_out, v_w_out, "adamw_w_out"),
    })
    for n, _ in _small_spec(depth):
        res[n] = [sm[i][0][n] for i in range(4)]
    order = ("w_ada", "b_ada", "norm1_g", "norm2_g", "w_in", "rel_table", "w_conv", "a_log", "dt_bias", "gdn_norm_g",
             "w_branch_a", "w_branch_b", "w_out", "w_ffn_in", "w_ffn_out", "final_g")
    return (loss, grad_x, *[res[n][0] for n in order], *[res[n][1] for n in order],
            *[res[n][2] for n in order], *[res[n][3] for n in order])
```

```python
import functools
import math

import jax
import jax.numpy as jnp
from jax import lax
from jax.experimental import pallas as pl
from jax.experimental.pallas import tpu as pltpu

F32 = jnp.float32
BF16 = jnp.bfloat16
HI = lax.Precision.HIGHEST

N_DEV = 8
D = 1024
DEPTH = 4
CH = 64
EPS = 1e-6
A_HEADS, A_DH = 8, 64
A_W = A_HEADS * A_DH
A_PAST = 8
A_MAX_REL = 128
QB = 256
KB = QB + A_PAST * CH
B_HEADS, B_DH = 4, 128
B_W = B_HEADS * B_DH
CONV_K = 4
FF = 2816
IN_DIM = 5640
IN_PAD = 5760
LANE = 128
NEG = -1e30
VMEM_LIMIT = 48 * 1024 * 1024

ADAM_LR, ADAM_B1, ADAM_B2, ADAM_EPS, ADAM_WD, ADAM_STEP = 0.001, 0.9, 0.999, 1e-08, 0.01, 10

OFF_GA, OFF_GB, OFF_QA, OFF_KA, OFF_VA, OFF_QB, OFF_KB, OFF_VB, OFF_ZB, OFF_BA = (
    0, 1024, 2048, 2560, 3072, 3584, 4096, 4608, 5120, 5632)


def _cp(sem=None):
    return pltpu.CompilerParams(dimension_semantics=sem, vmem_limit_bytes=VMEM_LIMIT)


def _tile(n, pref):
    if n <= pref:
        return n
    best = None
    for t in range(LANE, pref + 1, LANE):
        if n % t == 0:
            best = t
    assert best is not None, (n, pref)
    return best


def _sigmoid(x):
    return 1.0 / (1.0 + jnp.exp(-x))


def _silu(x):
    return x * _sigmoid(x)


def _dsilu(x):
    s = _sigmoid(x)
    return s * (1.0 + x * (1.0 - s))


def _dot(a, b, prec=None):
    return jnp.dot(a, b, preferred_element_type=F32, precision=prec)


def _dot_nt(a, b, prec=None):
    return lax.dot_general(a, b, (((1,), (1,)), ((), ())), preferred_element_type=F32, precision=prec)


def _dot_tn(a, b, prec=None):
    return lax.dot_general(a, b, (((0,), (0,)), ((), ())), preferred_element_type=F32, precision=prec)


def _mm(a, b, *, ta=False, tb=False, out_dtype=F32, name, tm=1024, tn=1024, tk=1024):
    halves = a.ndim == 3
    a_rows, a_cols = (a.shape[1], 2 * a.shape[2]) if halves else a.shape
    m, k = (a_cols, a_rows) if ta else (a_rows, a_cols)
    n = b.shape[0] if tb else b.shape[1]
    assert k == (b.shape[1] if tb else b.shape[0]), (a.shape, b.shape, ta, tb)
    tm, tn, tk = _tile(m, tm), _tile(n, tn), _tile(k, tk)
    nk = k // tk
    dn = (((0 if ta else 1,), (1 if tb else 0,)), ((), ()))

    def body(a_ref, b_ref, o_ref, *acc):
        part = lax.dot_general(a_ref[...].astype(BF16), b_ref[...].astype(BF16), dn, preferred_element_type=F32)
        if nk == 1:
            o_ref[...] = part.astype(out_dtype)
            return
        acc_ref, kk = acc[0], pl.program_id(2)

        @pl.when(kk == 0)
        def _():
            acc_ref[...] = part

        @pl.when(kk > 0)
        def _():
            acc_ref[...] += part

        @pl.when(kk == nk - 1)
        def _():
            o_ref[...] = acc_ref[...].astype(out_dtype)

    if halves:
        per = a.shape[2] // (tm if ta else tk)
        a_spec = (pl.BlockSpec((None, tk, tm), lambda i, j, q: (i // per, q, i % per)) if ta else
                  pl.BlockSpec((None, tm, tk), lambda i, j, q: (q // per, i, q % per)))
    else:
        a_spec = pl.BlockSpec((tk, tm), lambda i, j, q: (q, i)) if ta else pl.BlockSpec((tm, tk), lambda i, j, q: (i, q))
    b_spec = pl.BlockSpec((tn, tk), lambda i, j, q: (j, q)) if tb else pl.BlockSpec((tk, tn), lambda i, j, q: (q, j))
    return pl.pallas_call(
        body, grid=(m // tm, n // tn, nk), in_specs=[a_spec, b_spec],
        out_specs=pl.BlockSpec((tm, tn), lambda i, j, q: (i, j)),
        out_shape=jax.ShapeDtypeStruct((m, n), out_dtype),
        scratch_shapes=[pltpu.VMEM((tm, tn), F32)] if nk > 1 else [],
        compiler_params=_cp(("parallel", "parallel", "arbitrary")), name=name)(a, b)


def _rb(tr, width, cb=0):
    return pl.BlockSpec((tr, width), lambda i: (i, cb))


def _whole(shape):
    nd = len(shape)
    return pl.BlockSpec(shape, lambda i: (0,) * nd)


def _colsum(v):
    return jnp.sum(v, axis=0, keepdims=True)


def _adaln_fwd(x, g, sc, sh, t=None, gt=None, *, name, tr=256):
    tt = x.shape[0]
    res = t is not None

    def body(*refs):
        if res:
            x_ref, t_ref, gt_ref, g_ref, sc_ref, sh_ref, xo_ref, h_ref = refs
            xv = x_ref[...] + gt_ref[...] * t_ref[...]
            xo_ref[...] = xv
        else:
            x_ref, g_ref, sc_ref, sh_ref, h_ref = refs
            xv = x_ref[...]
        r = lax.rsqrt(jnp.mean(xv * xv, axis=-1, keepdims=True) + EPS)
        h_ref[...] = ((xv * r * g_ref[...]) * (1.0 + sc_ref[...]) + sh_ref[...]).astype(BF16)

    row, vec = _rb(tr, D), _whole((1, D))
    if res:
        ins, in_specs = (x, t, gt, g, sc, sh), [row, row, vec, vec, vec, vec]
        out_shape = (jax.ShapeDtypeStruct((tt, D), F32), jax.ShapeDtypeStruct((tt, D), BF16))
        out_specs = (row, row)
    else:
        ins, in_specs = (x, g, sc, sh), [row, vec, vec, vec]
        out_shape, out_specs = jax.ShapeDtypeStruct((tt, D), BF16), row
    out = pl.pallas_call(body, grid=(tt // tr,), in_specs=in_specs, out_specs=out_specs, out_shape=out_shape,
                         compiler_params=_cp(("parallel",)), name=name)(*ins)
    return out if res else (x, out)


def _mm_adaln_bwd(a, b, x, g, sc, sh, dx_in, t=None, gt=None, *, name, tk, tm=512):
    tt = x.shape[0]
    res = t is not None
    halves = a.ndim == 3
    k = 2 * a.shape[2] if halves else a.shape[1]
    tm, nk = _tile(tt, tm), k // tk

    def body(*refs):
        if res:
            a_ref, b_ref, x_ref, g_ref, sc_ref, sh_ref, dxi_ref, t_ref, gt_ref, dx_ref, dt_ref, st_ref, acc_ref = refs
        else:
            a_ref, b_ref, x_ref, g_ref, sc_ref, sh_ref, dxi_ref, dx_ref, st_ref, acc_ref = refs
        i, q = pl.program_id(0), pl.program_id(1)
        part = _dot(a_ref[...], b_ref[...])

        @pl.when((i == 0) & (q == 0))
        def _():
            st_ref[...] = jnp.zeros_like(st_ref)

        @pl.when(q == 0)
        def _():
            acc_ref[...] = part

        @pl.when(q > 0)
        def _():
            acc_ref[...] += part

        @pl.when(q == nk - 1)
        def _():
            xv, dh = x_ref[...], acc_ref[...]
            r = lax.rsqrt(jnp.mean(xv * xv, axis=-1, keepdims=True) + EPS)
            nrm = xv * r
            y = nrm * g_ref[...]
            dy = dh * (1.0 + sc_ref[...])
            dn = dy * g_ref[...]
            dx = dxi_ref[...] + r * (dn - nrm * jnp.mean(dn * nrm, axis=-1, keepdims=True))
            dx_ref[...] = dx
            st_ref[0:1, :] += _colsum(dy * nrm)
            st_ref[1:2, :] += _colsum(dh * y)
            st_ref[2:3, :] += _colsum(dh)
            if res:
                dt_ref[...] = (gt_ref[...] * dx).astype(BF16)
                st_ref[3:4, :] += _colsum(dx * t_ref[...])

    if halves:
        per = a.shape[2] // tk
        a_spec = pl.BlockSpec((None, tm, tk), lambda i, q: (q // per, i, q % per))
    else:
        a_spec = pl.BlockSpec((tm, tk), lambda i, q: (i, q))
    row = pl.BlockSpec((tm, D), lambda i, q: (i, 0))
    vec = pl.BlockSpec((1, D), lambda i, q: (0, 0))
    ins = [a, b, x, g, sc, sh, dx_in]
    in_specs = [a_spec, pl.BlockSpec((tk, D), lambda i, q: (q, 0)), row, vec, vec, vec, row]
    out_shape, out_specs = [jax.ShapeDtypeStruct((tt, D), F32)], [row]
    if res:
        ins += [t, gt]
        in_specs += [row, vec]
        out_shape.append(jax.ShapeDtypeStruct((tt, D), BF16))
        out_specs.append(row)
    out_shape.append(jax.ShapeDtypeStruct((8, D), F32))
    out_specs.append(pl.BlockSpec((8, D), lambda i, q: (0, 0)))
    return pl.pallas_call(body, grid=(tt // tm, nk), in_specs=in_specs, out_specs=tuple(out_specs),
                          out_shape=tuple(out_shape), scratch_shapes=[pltpu.VMEM((tm, D), F32)],
                          compiler_params=_cp(("arbitrary", "arbitrary")), name=name)(*ins)


def _loss_head(x, t, gt, fg, tgt, *, name, tr=256):
    tt = x.shape[0]

    def body(x_ref, t_ref, gt_ref, fg_ref, tgt_ref, dx_ref, dt_ref, st_ref):
        @pl.when(pl.program_id(0) == 0)
        def _():
            st_ref[...] = jnp.zeros_like(st_ref)

        tv = t_ref[...]
        xv = x_ref[...] + gt_ref[...] * tv
        r = lax.rsqrt(jnp.mean(xv * xv, axis=-1, keepdims=True) + EPS)
        nrm = xv * r
        err = nrm * fg_ref[...] - tgt_ref[...]
        st_ref[4:5, :] += 0.5 * jnp.sum(jnp.mean(err * err, axis=-1, keepdims=True), axis=0, keepdims=True)
        dy = err * (1.0 / D)
        dn = dy * fg_ref[...]
        dx = r * (dn - nrm * jnp.mean(dn * nrm, axis=-1, keepdims=True))
        dx_ref[...] = dx
        dt_ref[...] = (gt_ref[...] * dx).astype(BF16)
        st_ref[0:1, :] += _colsum(dy * nrm)
        st_ref[3:4, :] += _colsum(dx * tv)

    row, vec = _rb(tr, D), _whole((1, D))
    return pl.pallas_call(
        body, grid=(tt // tr,), in_specs=[row, row, vec, vec, row], out_specs=(row, row, _whole((8, D))),
        out_shape=(jax.ShapeDtypeStruct((tt, D), F32), jax.ShapeDtypeStruct((tt, D), BF16),
                   jax.ShapeDtypeStruct((8, D), F32)),
        compiler_params=_cp(("arbitrary",)), name=name)(x, t, gt, fg, tgt)


def _branch_merge(ya, yb, wa, wb, proj, *, name, tm=512):
    tt = ya.shape[0]
    tm = _tile(tt, tm)

    def body(ya_ref, yb_ref, wa_ref, wb_ref, ga_ref, gb_ref, pa_ref, pb_ref, o_ref):
        pa = _dot(ya_ref[...], wa_ref[...])
        pb = _dot(yb_ref[...], wb_ref[...])
        pa_ref[...] = pa.astype(BF16)
        pb_ref[...] = pb.astype(BF16)
        o_ref[...] = (_sigmoid(ga_ref[...]) * pa + _sigmoid(gb_ref[...]) * pb).astype(BF16)

    row, half, wsp = _rb(tm, D), _rb(tm, A_W), _whole((A_W, D))
    out = jax.ShapeDtypeStruct((tt, D), BF16)
    return pl.pallas_call(body, grid=(tt // tm,), in_specs=[half, half, wsp, wsp, _rb(tm, D, 0), _rb(tm, D, 1)],
                          out_specs=(row, row, row), out_shape=(out, out, out), compiler_params=_cp(("parallel",)),
                          name=name)(ya, yb, wa, wb, proj, proj)


def _out_adaln(a, w, x, gt, g, sc, sh, *, name, tk=None, tm=512):
    tt, k = a.shape
    tm, tk = _tile(tt, tm), tk or k
    nk = k // tk

    def body(a_ref, w_ref, x_ref, gt_ref, g_ref, sc_ref, sh_ref, t_ref, xo_ref, h_ref):
        q = pl.program_id(1)
        part = _dot(a_ref[...], w_ref[...])

        @pl.when(q == 0)
        def _():
            t_ref[...] = part

        @pl.when(q > 0)
        def _():
            t_ref[...] += part

        @pl.when(q == nk - 1)
        def _():
            xv = x_ref[...] + gt_ref[...] * t_ref[...]
            xo_ref[...] = xv
            r = lax.rsqrt(jnp.mean(xv * xv, axis=-1, keepdims=True) + EPS)
            h_ref[...] = ((xv * r * g_ref[...]) * (1.0 + sc_ref[...]) + sh_ref[...]).astype(BF16)

    row = pl.BlockSpec((tm, D), lambda i, q: (i, 0))
    vec = pl.BlockSpec((1, D), lambda i, q: (0, 0))
    f32 = jax.ShapeDtypeStruct((tt, D), F32)
    return pl.pallas_call(
        body, grid=(tt // tm, nk),
        in_specs=[pl.BlockSpec((tm, tk), lambda i, q: (i, q)), pl.BlockSpec((tk, D), lambda i, q: (q, 0)),
                  row, vec, vec, vec, vec],
        out_specs=(row, row, row), out_shape=(f32, f32, jax.ShapeDtypeStruct((tt, D), BF16)),
        compiler_params=_cp(("parallel", "arbitrary")), name=name)(a, w, x, gt, g, sc, sh)


def _out_bwd_merge(dt, wout, proj, pa, pb, *, name, tm=512):
    tt = dt.shape[0]
    tm = _tile(tt, tm)

    def body(dt_ref, w_ref, ga_ref, gb_ref, pa_ref, pb_ref, dg_ref, dpa_ref, dpb_ref):
        dm_v = _dot_nt(dt_ref[...], w_ref[...])
        sa, sb = _sigmoid(ga_ref[...]), _sigmoid(gb_ref[...])
        dpa_ref[...] = (dm_v * sa).astype(BF16)
        dpb_ref[...] = (dm_v * sb).astype(BF16)
        dg_ref[:, 0:D] = (dm_v * pa_ref[...].astype(F32) * sa * (1.0 - sa)).astype(BF16)
        dg_ref[:, D:2 * D] = (dm_v * pb_ref[...].astype(F32) * sb * (1.0 - sb)).astype(BF16)

    row = _rb(tm, D)
    return pl.pallas_call(
        body, grid=(tt // tm,), in_specs=[row, _whole((D, D)), _rb(tm, D, 0), _rb(tm, D, 1), row, row],
        out_specs=(_rb(tm, 2 * D), row, row),
        out_shape=(jax.ShapeDtypeStruct((tt, 2 * D), BF16), jax.ShapeDtypeStruct((tt, D), BF16),
                   jax.ShapeDtypeStruct((tt, D), BF16)),
        compiler_params=_cp(("parallel",)), name=name)(dt, wout, proj, proj, pa, pb)


FTN = FF // 2


def _ffn_in_swiglu(h, wt, *, name, tm=1024):
    tt = h.shape[0]
    tm = _tile(tt, tm)

    def body(h_ref, wg_ref, wu_ref, gu_ref, act_ref):
        hv = h_ref[...]
        g = _dot_nt(hv, wg_ref[...])
        u = _dot_nt(hv, wu_ref[...])
        gu_ref[0] = g.astype(BF16)
        gu_ref[1] = u.astype(BF16)
        act_ref[...] = (_silu(g) * u).astype(BF16)

    nj = FF // FTN
    return pl.pallas_call(
        body, grid=(tt // tm, nj),
        in_specs=[pl.BlockSpec((tm, D), lambda i, j: (i, 0)), pl.BlockSpec((FTN, D), lambda i, j: (j, 0)),
                  pl.BlockSpec((FTN, D), lambda i, j: (j + nj, 0))],
        out_specs=(pl.BlockSpec((2, tm, FTN), lambda i, j: (0, i, j)), pl.BlockSpec((tm, FTN), lambda i, j: (i, j))),
        out_shape=(jax.ShapeDtypeStruct((2, tt, FF), BF16), jax.ShapeDtypeStruct((tt, FF), BF16)),
        compiler_params=_cp(("parallel", "parallel")), name=name)(h, wt, wt)


def _ffn_out_bwd_swiglu(dt, wo, gu, *, name, tm=1024):
    tt = dt.shape[0]
    tm = _tile(tt, tm)

    def body(dt_ref, wo_ref, gu_ref, dgu_ref):
        da = _dot_nt(dt_ref[...], wo_ref[...])
        g, u = gu_ref[0].astype(F32), gu_ref[1].astype(F32)
        dgu_ref[0] = (da * u * _dsilu(g)).astype(BF16)
        dgu_ref[1] = (da * _silu(g)).astype(BF16)

    blk = pl.BlockSpec((2, tm, FTN), lambda i, j: (0, i, j))
    return pl.pallas_call(
        body, grid=(tt // tm, FF // FTN),
        in_specs=[pl.BlockSpec((tm, D), lambda i, j: (i, 0)), pl.BlockSpec((FTN, D), lambda i, j: (j, 0)), blk],
        out_specs=blk, out_shape=jax.ShapeDtypeStruct((2, tt, FF), BF16),
        compiler_params=_cp(("parallel", "parallel")), name=name)(dt, wo, gu)


BIAS_LW = 1152


def _bias_diagonals(table):
    n_far = A_PAST * CH - A_MAX_REL + 1
    far = jnp.broadcast_to(table[:, 2 * A_MAX_REL:], (A_HEADS, n_far))
    mid = jnp.flip(table[:, 1:2 * A_MAX_REL], axis=1)
    near = jnp.broadcast_to(table[:, 0:1], (A_HEADS, KB - n_far - (2 * A_MAX_REL - 1)))
    pos = jnp.concatenate([far, mid, near], axis=1)
    neg = jnp.broadcast_to(table[:, 2 * A_MAX_REL:], (A_HEADS, QB - 1))
    gap = jnp.zeros((A_HEADS, BIAS_LW - KB - (QB - 1)), F32)
    return jnp.concatenate([pos, gap, neg], axis=1)


def _bias_fwd(diag, *, name):
    def body(w_ref, o_ref):
        qc = lax.broadcasted_iota(jnp.int32, (QB, KB), 0) // CH + A_PAST
        col = lax.broadcasted_iota(jnp.int32, (QB, KB), 1)
        inband = (col // CH <= qc) & (col // CH >= qc - A_PAST)
        for h in range(A_HEADS):
            rows = pltpu.roll(jnp.broadcast_to(w_ref[h:h + 1, :], (QB, BIAS_LW)), 0, 1, stride=1, stride_axis=0)
            for var in range(3):
                o_ref[var, h] = jnp.where(inband & (col >= A_PAST * CH - QB * var), rows[:, :KB], NEG)

    return pl.pallas_call(body, out_shape=jax.ShapeDtypeStruct((3, A_HEADS, QB, KB), F32), compiler_params=_cp(),
                          name=name)(diag)


def _bias_bwd(dbias, *, name):
    def body(d_ref, o_ref):
        r = lax.broadcasted_iota(jnp.int32, (QB, QB), 0)
        c = lax.broadcasted_iota(jnp.int32, (QB, QB), 1)
        flip = jnp.where(r + c == QB - 1, 1.0, 0.0).astype(F32)
        for h in range(A_HEADS):
            x = jnp.concatenate([_dot(flip, d_ref[h], HI), jnp.zeros((QB, BIAS_LW - KB), F32)], axis=1)
            o_ref[h:h + 1, :] = jnp.sum(pltpu.roll(x, 0, 1, stride=1, stride_axis=0), axis=0, keepdims=True)

    return pl.pallas_call(body, out_shape=jax.ShapeDtypeStruct((A_HEADS, BIAS_LW), F32), compiler_params=_cp(),
                          name=name)(dbias)


def _kv_pad(proj, *, name, tr=256):
    tt = proj.shape[0]
    tr = 2 * tr if tt % (2 * tr) == 0 else tr
    npad = A_PAST * CH // tr

    def body(k_ref, v_ref, ko_ref, vo_ref):
        i = pl.program_id(0)

        @pl.when(i < npad)
        def _():
            ko_ref[...] = jnp.zeros_like(ko_ref)
            vo_ref[...] = jnp.zeros_like(vo_ref)

        @pl.when(i >= npad)
        def _():
            ko_ref[...] = k_ref[...].astype(BF16)
            vo_ref[...] = v_ref[...].astype(BF16)

    src = lambda off: pl.BlockSpec((tr, A_W), lambda i: (jnp.maximum(i - npad, 0), off // A_W))
    out = jax.ShapeDtypeStruct((tt + A_PAST * CH, A_W), BF16)
    return pl.pallas_call(body, grid=(tt // tr + npad,), in_specs=[src(OFF_KA), src(OFF_VA)],
                          out_specs=(_rb(tr, A_W), _rb(tr, A_W)), out_shape=(out, out),
                          compiler_params=_cp(("parallel",)), name=name)(proj, proj)


def _attn_fwd(proj, kpad, vpad, bias, *, name):
    tt = proj.shape[0]

    def body(q_ref, k_ref, v_ref, b_ref, o_ref, l_ref):
        q0 = pl.multiple_of(pl.program_id(1) * QB, QB)
        q = q_ref[...] * (A_DH ** -0.5)
        k = k_ref[pl.ds(q0, KB), :]
        v = v_ref[pl.ds(q0, KB), :]
        lane = lax.broadcasted_iota(jnp.int32, (QB, LANE), 1)
        o = jnp.zeros((QB, LANE), F32)
        lse = jnp.zeros((QB, LANE), F32)
        for a in range(2):
            hm = (lane >= A_DH * a) & (lane < A_DH * (a + 1))
            s = _dot_nt(jnp.where(hm, q, 0.0).astype(BF16), k) + b_ref[a]
            m = jnp.max(s, axis=-1, keepdims=True)
            p = jnp.exp(s - m)
            l = jnp.sum(p, axis=-1, keepdims=True)
            o = jnp.where(hm, _dot(p.astype(BF16), v) / l, o)
            lse = jnp.where(hm, m + jnp.log(l), lse)
        o_ref[...] = o.astype(BF16)
        l_ref[...] = lse

    kv = pl.BlockSpec((tt + A_PAST * CH, LANE), lambda h, i: (0, h))
    blk = pl.BlockSpec((QB, LANE), lambda h, i: (i, h))
    return pl.pallas_call(
        body, grid=(A_W // LANE, tt // QB),
        in_specs=[pl.BlockSpec((QB, LANE), lambda h, i: (i, OFF_QA // LANE + h)), kv, kv,
                  pl.BlockSpec((None, 2, QB, KB), lambda h, i: (jnp.minimum(i, 2), h, 0, 0))],
        out_specs=(blk, blk),
        out_shape=(jax.ShapeDtypeStruct((tt, A_W), BF16), jax.ShapeDtypeStruct((tt, A_W), F32)),
        compiler_params=_cp(("parallel", "parallel")), name=name)(proj, kpad, vpad, bias)


def _attn_bwd(proj, kpad, vpad, bias, o, lse, do, *, name):
    tt = proj.shape[0]
    nq = tt // QB

    def body(q_ref, k_ref, v_ref, b_ref, o_ref, l_ref, do_ref, dq_ref, dko_ref, dvo_ref, db_ref, dk_ref, dv_ref):
        @pl.when(pl.program_id(1) == 0)
        def _():
            dk_ref[...] = jnp.zeros_like(dk_ref)
            dv_ref[...] = jnp.zeros_like(dv_ref)
            db_ref[...] = jnp.zeros_like(db_ref)

        q0 = pl.multiple_of(pl.program_id(1) * QB, QB)
        q, do_v, lse = q_ref[...] * (A_DH ** -0.5), do_ref[...], l_ref[...]
        k = k_ref[pl.ds(q0, KB), :]
        v = v_ref[pl.ds(q0, KB), :]
        dsum = do_v * o_ref[...].astype(F32)
        lane = lax.broadcasted_iota(jnp.int32, (QB, LANE), 1)
        dq = jnp.zeros((QB, LANE), F32)
        dk = jnp.zeros((KB, LANE), F32)
        dv = jnp.zeros((KB, LANE), F32)
        for a in range(2):
            hm = (lane >= A_DH * a) & (lane < A_DH * (a + 1))
            qa = jnp.where(hm, q, 0.0).astype(BF16)
            doa = jnp.where(hm, do_v, 0.0).astype(BF16)
            s = _dot_nt(qa, k) + b_ref[a]
            lse_a = jnp.max(jnp.where(hm, lse, NEG), axis=-1, keepdims=True)
            p = jnp.exp(s - lse_a)
            dp = _dot_nt(doa, v)
            dsum_a = jnp.sum(jnp.where(hm, dsum, 0.0), axis=-1, keepdims=True)
            ds = p * (dp - dsum_a)
            db_ref[a] += ds
            dsb = ds.astype(BF16)
            dq = jnp.where(hm, _dot(dsb, k) * (A_DH ** -0.5), dq)
            dk += _dot_tn(dsb, qa)
            dv += _dot_tn(p.astype(BF16), doa)
        dq_ref[...] = dq.astype(BF16)
        dk_ref[pl.ds(q0, KB), :] += dk
        dv_ref[pl.ds(q0, KB), :] += dv

        @pl.when(pl.program_id(1) == nq - 1)
        def _():
            dko_ref[...] = dk_ref[A_PAST * CH:, :].astype(BF16)
            dvo_ref[...] = dv_ref[A_PAST * CH:, :].astype(BF16)

    kv = pl.BlockSpec((tt + A_PAST * CH, LANE), lambda h, i: (0, h))
    blk = pl.BlockSpec((QB, LANE), lambda h, i: (i, h))
    col = pl.BlockSpec((tt, LANE), lambda h, i: (0, h))
    bsp = pl.BlockSpec((2, QB, KB), lambda h, i: (h, 0, 0))
    bias_in = pl.BlockSpec((None, 2, QB, KB), lambda h, i: (jnp.minimum(i, 2), h, 0, 0))
    out = jax.ShapeDtypeStruct((tt, A_W), BF16)
    return pl.pallas_call(
        body, grid=(A_W // LANE, nq),
        in_specs=[pl.BlockSpec((QB, LANE), lambda h, i: (i, OFF_QA // LANE + h)), kv, kv, bias_in, blk, blk, blk],
        out_specs=(blk, col, col, bsp),
        out_shape=(out, out, out, jax.ShapeDtypeStruct((A_HEADS, QB, KB), F32)),
        scratch_shapes=[pltpu.VMEM((tt + A_PAST * CH, LANE), F32), pltpu.VMEM((tt + A_PAST * CH, LANE), F32)],
        compiler_params=_cp(("parallel", "arbitrary")), name=name)(proj, kpad, vpad, bias, o, lse, do)


GTR = 256


def _taps(w_ref, grp):
    return [w_ref[j:j + 1, grp * B_W:(grp + 1) * B_W] for j in range(CONV_K)]


def _shifts(xe, rows):
    return [xe[8:8 + rows]] + [pltpu.roll(xe, s, 0)[8:8 + rows] for s in range(1, CONV_K)]


def _conv(shifts, taps):
    acc = taps[CONV_K - 1] * shifts[0]
    for s in range(1, CONV_K):
        acc = acc + taps[CONV_K - 1 - s] * shifts[s]
    return acc


def _qk_scale(grp):
    return B_DH ** -0.5 if grp == 0 else 1.0


def _act_fwd(c, grp):
    y = _silu(c)
    if grp == 2:
        return y
    parts = []
    for hd in range(B_HEADS):
        yh = y[:, hd * B_DH:(hd + 1) * B_DH]
        parts.append(yh * (lax.rsqrt(jnp.sum(yh * yh, axis=-1, keepdims=True) + EPS) * _qk_scale(grp)))
    return jnp.concatenate(parts, axis=1)


def _act_bwd(c, dy, grp):
    if grp == 2:
        return dy * _dsilu(c)
    y = _silu(c)
    parts = []
    for hd in range(B_HEADS):
        yh = y[:, hd * B_DH:(hd + 1) * B_DH]
        r = lax.rsqrt(jnp.sum(yh * yh, axis=-1, keepdims=True) + EPS)
        dyh = dy[:, hd * B_DH:(hd + 1) * B_DH] * _qk_scale(grp)
        parts.append(r * dyh - yh * (r * r * r) * jnp.sum(dyh * yh, axis=-1, keepdims=True))
    return jnp.concatenate(parts, axis=1) * _dsilu(c)


def _chunk_tri(n, upper=False):
    r = lax.broadcasted_iota(jnp.int32, (n, n), 0)
    c = lax.broadcasted_iota(jnp.int32, (n, n), 1)
    same = (r // CH) == (c // CH)
    return jnp.where(same & ((r <= c) if upper else (r >= c)), 1.0, 0.0).astype(F32)


def _gate_rows(ba, par_ref):
    lane = lax.broadcasted_iota(jnp.int32, ba.shape, 1)
    z = ba + par_ref[1:2, :]
    sp = jnp.maximum(z, 0.0) + jnp.log(1.0 + jnp.exp(-jnp.abs(z)))
    g = -jnp.exp(par_ref[0:1, :]) * sp
    return jnp.where(lane < B_HEADS, _sigmoid(ba), jnp.where(lane < 2 * B_HEADS, g, 0.0)), z


def _prev8(cb):
    return pl.BlockSpec((8, B_W), lambda i: (jnp.maximum(i * (GTR // 8) - 1, 0), cb))


def _next8(cb, nb):
    return pl.BlockSpec((8, B_W), lambda i: (jnp.minimum((i + 1) * (GTR // 8), nb * (GTR // 8) - 1), cb))


def _gdn_pre_fwd(proj, wconv, par, *, name):
    tt = proj.shape[0]

    def body(q_ref, k_ref, v_ref, qh_ref, kh_ref, vh_ref, ba_ref, w_ref, par_ref, qo_ref, ko_ref, vo_ref, aux_ref):
        first = pl.program_id(0) == 0
        for grp, (x_ref, h_ref, o_ref) in enumerate(((q_ref, qh_ref, qo_ref), (k_ref, kh_ref, ko_ref),
                                                     (v_ref, vh_ref, vo_ref))):
            xe = jnp.concatenate([jnp.where(first, 0.0, h_ref[...]), x_ref[...]], axis=0)
            o_ref[...] = _act_fwd(_conv(_shifts(xe, GTR), _taps(w_ref, grp)), grp)
        bg, _ = _gate_rows(ba_ref[...], par_ref)
        lane = lax.broadcasted_iota(jnp.int32, bg.shape, 1)
        aux_ref[...] = jnp.where(lane < B_HEADS, bg, _dot(_chunk_tri(GTR), bg, HI))

    col = lambda off: _rb(GTR, B_W, off // B_W)
    outs = jax.ShapeDtypeStruct((tt, B_W), F32)
    return pl.pallas_call(
        body, grid=(tt // GTR,),
        in_specs=[col(OFF_QB), col(OFF_KB), col(OFF_VB), _prev8(OFF_QB // B_W), _prev8(OFF_KB // B_W),
                  _prev8(OFF_VB // B_W), _rb(GTR, LANE, OFF_BA // LANE), _whole((CONV_K, 3 * B_W)),
                  _whole((8, LANE))],
        out_specs=(_rb(GTR, B_W), _rb(GTR, B_W), _rb(GTR, B_W), _rb(GTR, LANE)),
        out_shape=(outs, outs, outs, jax.ShapeDtypeStruct((tt, LANE), F32)),
        compiler_params=_cp(("parallel",)), name=name)(proj, proj, proj, proj, proj, proj, proj, wconv, par)


def _gdn_pre_bwd(proj, wconv, par, dq, dk, dv, daux, *, name):
    tt = proj.shape[0]
    nb = tt // GTR

    def body(q_ref, k_ref, v_ref, qh_ref, kh_ref, vh_ref, qn_ref, kn_ref, vn_ref, ba_ref, w_ref, par_ref,
             dq_ref, dk_ref, dv_ref, dqn_ref, dkn_ref, dvn_ref, daux_ref, dx_ref, dba_ref, dw_ref, dpar_ref):
        i = pl.program_id(0)
        first, last = i == 0, i == nb - 1

        @pl.when(first)
        def _():
            dw_ref[...] = jnp.zeros_like(dw_ref)
            dpar_ref[...] = jnp.zeros_like(dpar_ref)

        groups = ((q_ref, qh_ref, qn_ref, dq_ref, dqn_ref), (k_ref, kh_ref, kn_ref, dk_ref, dkn_ref),
                  (v_ref, vh_ref, vn_ref, dv_ref, dvn_ref))
        for grp, (x_ref, h_ref, xn_ref, d_ref, dn_ref) in enumerate(groups):
            taps = _taps(w_ref, grp)
            xe = jnp.concatenate([jnp.where(first, 0.0, h_ref[...]), x_ref[...]], axis=0)
            sh = _shifts(xe, GTR)
            dc = _act_bwd(_conv(sh, taps), d_ref[...], grp)
            xe_n = jnp.concatenate([x_ref[GTR - 8:GTR, :], xn_ref[...]], axis=0)
            dcn = _act_bwd(_conv(_shifts(xe_n, 8), taps), dn_ref[...], grp)
            dce = jnp.concatenate([dc, jnp.where(last, 0.0, dcn)], axis=0)
            dx = taps[CONV_K - 1] * dc
            dw_ref[CONV_K - 1:CONV_K, grp * B_W:(grp + 1) * B_W] += _colsum(dc * sh[0])
            for s in range(1, CONV_K):
                dx = dx + taps[CONV_K - 1 - s] * pltpu.roll(dce, GTR + 8 - s, 0)[0:GTR]
                dw_ref[CONV_K - 1 - s:CONV_K - s, grp * B_W:(grp + 1) * B_W] += _colsum(dc * sh[s])
            dx_ref[:, grp * B_W:(grp + 1) * B_W] = dx.astype(BF16)
        ba = ba_ref[...]
        lane = lax.broadcasted_iota(jnp.int32, ba.shape, 1)
        bg, z = _gate_rows(ba, par_ref)
        daux_v = daux_ref[...]
        dg = _dot(_chunk_tri(GTR, upper=True), daux_v, HI)
        dgl = jnp.where((lane >= B_HEADS) & (lane < 2 * B_HEADS), dg, 0.0)
        da = dgl * (-jnp.exp(par_ref[0:1, :])) * _sigmoid(z)
        dbr = jnp.where(lane < B_HEADS, daux_v * bg * (1.0 - bg), 0.0)
        dba_ref[...] = (dbr + da).astype(BF16)
        dpar_ref[0:1, :] += _colsum(dgl * bg)
        dpar_ref[1:2, :] += _colsum(da)

    col = lambda off: _rb(GTR, B_W, off // B_W)
    row, rowl = _rb(GTR, B_W), _rb(GTR, LANE)
    return pl.pallas_call(
        body, grid=(nb,),
        in_specs=[col(OFF_QB), col(OFF_KB), col(OFF_VB),
                  _prev8(OFF_QB // B_W), _prev8(OFF_KB // B_W), _prev8(OFF_VB // B_W),
                  _next8(OFF_QB // B_W, nb), _next8(OFF_KB // B_W, nb), _next8(OFF_VB // B_W, nb),
                  _rb(GTR, LANE, OFF_BA // LANE), _whole((CONV_K, 3 * B_W)), _whole((8, LANE)),
                  row, row, row, _next8(0, nb), _next8(0, nb), _next8(0, nb), rowl],
        out_specs=(_rb(GTR, 3 * B_W), rowl, _whole((8, 3 * B_W)), _whole((8, LANE))),
        out_shape=(jax.ShapeDtypeStruct((tt, 3 * B_W), BF16), jax.ShapeDtypeStruct((tt, LANE), BF16),
                   jax.ShapeDtypeStruct((8, 3 * B_W), F32), jax.ShapeDtypeStruct((8, LANE), F32)),
        compiler_params=_cp(("arbitrary",)), name=name)(
            proj, proj, proj, proj, proj, proj, proj, proj, proj, proj, wconv, par, dq, dk, dv, dq, dk, dv, daux)


def _col(x, j):
    lane = lax.broadcasted_iota(jnp.int32, x.shape, 1)
    return jnp.sum(jnp.where(lane == j, x, 0.0), axis=-1, keepdims=True)


def _split(x):
    hi = x.astype(BF16)
    return hi, (x - hi.astype(F32)).astype(BF16)


def _dot3(a, b, tn=False):
    dot = _dot_tn if tn else _dot
    (ah, al), (bh, bl) = _split(a), _split(b)
    return dot(ah, bh) + (dot(ah, bl) + dot(al, bh))


def _chunk_masks():
    r = lax.broadcasted_iota(jnp.int32, (CH, CH), 0)
    c = lax.broadcasted_iota(jnp.int32, (CH, CH), 1)
    return r > c, r >= c


def _gc_rows(aux, nc):
    t = jnp.transpose(aux[:, B_HEADS:2 * B_HEADS].reshape(nc, CH, B_HEADS), (0, 2, 1))
    return jnp.concatenate([t, jnp.zeros_like(t)], axis=1).reshape(nc * 8, CH)


_CHUNK8 = lambda width, n=1: pl.BlockSpec((8 * n, width), lambda i: (i, 0))
_CHUNK4 = lambda a, b, n=1: pl.BlockSpec((B_HEADS * n, a, b), lambda i: (i, 0, 0))
NCH = 4


def _per_chunk(body, rows):
    def wrapped(*refs):
        for ci in range(NCH):
            body(*[r.at[pl.ds(ci * n, n)] for r, n in zip(refs, rows)])
    return wrapped


def _gdn_lower(k, aux, auxt, *, name):
    tt = k.shape[0]

    def body(k_ref, aux_ref, auxt_ref, l_ref):
        aux_v = aux_ref[...]
        strict, _ = _chunk_masks()
        khs = [k_ref[:, hd * B_DH:(hd + 1) * B_DH].astype(BF16) for hd in range(B_HEADS)]
        kks = [_dot_nt(kh, kh) for kh in khs]
        for hd in range(B_HEADS):
            diff = _col(aux_v, B_HEADS + hd) - auxt_ref[hd:hd + 1, :]
            dec = jnp.exp(jnp.where(strict, diff, NEG))
            l_ref[hd] = _col(aux_v, hd) * kks[hd] * dec

    return pl.pallas_call(
        _per_chunk(body, (CH, CH, 8, B_HEADS)), grid=(tt // CH // NCH,),
        in_specs=[_rb(NCH * CH, B_W), _rb(NCH * CH, LANE), _CHUNK8(CH, NCH)],
        out_specs=_CHUNK4(CH, CH, NCH),
        out_shape=jax.ShapeDtypeStruct((tt // CH * B_HEADS, CH, CH), F32),
        compiler_params=_cp(("parallel",)), name=name)(k, aux, auxt)


def _tri_inverse(lt, *, name):
    nb = lt.shape[2]

    def body(l_ref, t_ref):
        rowid = lax.broadcasted_iota(jnp.int32, (CH, nb), 0)

        def outer(i, carry):
            def inner(j, acc):
                return acc + l_ref[i, pl.ds(j, 1), :] * t_ref[j]

            acc = lax.fori_loop(0, i, inner, jnp.zeros((CH, nb), F32))
            t_ref[i] = jnp.where(rowid == i, 1.0, 0.0) - acc
            return carry

        lax.fori_loop(0, CH, outer, 0)

    return pl.pallas_call(body, out_shape=jax.ShapeDtypeStruct(lt.shape, F32),
                          in_specs=[pl.BlockSpec(memory_space=pltpu.VMEM)],
                          out_specs=pl.BlockSpec(memory_space=pltpu.VMEM),
                          compiler_params=_cp(), name=name)(lt)


def _gdn_gates(aux_v, aux_last, auxt_ref, hd):
    _, incl = _chunk_masks()
    beta = _col(aux_v, hd)
    gc = _col(aux_v, B_HEADS + hd)
    gl = _col(aux_last, B_HEADS + hd)
    dec = jnp.exp(jnp.where(incl, gc - auxt_ref[hd:hd + 1, :], NEG))
    return beta, gc, gl, jnp.exp(gc), dec


def _gdn_intra(q, k, v, aux, auxt, tinv, *, name):
    tt = q.shape[0]
    nc = tt // CH

    def body(q_ref, k_ref, v_ref, aux_ref, auxt_ref, t_ref, u0_ref, w_ref, qd_ref, kd_ref, qk_ref, gle_ref):
        aux_v = aux_ref[...]
        aux_last = aux_ref[CH - 1:CH, :]
        lane8 = lax.broadcasted_iota(jnp.int32, (8, LANE), 1)
        gle = jnp.zeros((8, LANE), F32)
        heads = range(B_HEADS)
        sls = [slice(hd * B_DH, (hd + 1) * B_DH) for hd in heads]
        gates = [_gdn_gates(aux_v, aux_last, auxt_ref, hd) for hd in heads]
        qk0 = [_dot_nt(q_ref[:, sls[hd]].astype(BF16), k_ref[:, sls[hd]].astype(BF16)) for hd in heads]
        u0 = [_dot3(t_ref[hd], v_ref[:, sls[hd]] * gates[hd][0]) for hd in heads]
        wk = [_dot3(t_ref[hd], k_ref[:, sls[hd]] * (gates[hd][0] * gates[hd][3])) for hd in heads]
        for hd in heads:
            sl = sls[hd]
            beta, gc, gl, egc, dec = gates[hd]
            qk_ref[hd] = (qk0[hd] * dec).astype(BF16)
            u0_ref[:, sl] = u0[hd]
            w_ref[:, sl] = wk[hd].astype(BF16)
            qd_ref[:, sl] = (q_ref[:, sl] * egc).astype(BF16)
            kd_ref[:, sl] = (k_ref[:, sl] * jnp.exp(gl - gc)).astype(BF16)
            gle = gle + jnp.where(lane8 == hd, jnp.exp(gl), 0.0)
        gle_ref[...] = gle

    row = _rb(NCH * CH, B_W)
    half = jax.ShapeDtypeStruct((tt, B_W), BF16)
    return pl.pallas_call(
        _per_chunk(body, (CH, CH, CH, CH, 8, B_HEADS, CH, CH, CH, CH, B_HEADS, 8)), grid=(nc // NCH,),
        in_specs=[row, row, row, _rb(NCH * CH, LANE), _CHUNK8(CH, NCH), _CHUNK4(CH, CH, NCH)],
        out_specs=(row, row, row, row, _CHUNK4(CH, CH, NCH), _CHUNK8(LANE, NCH)),
        out_shape=(jax.ShapeDtypeStruct((tt, B_W), F32), half, half, half,
                   jax.ShapeDtypeStruct((nc * B_HEADS, CH, CH), BF16), jax.ShapeDtypeStruct((nc * 8, LANE), F32)),
        compiler_params=_cp(("parallel",)), name=name)(q, k, v, aux, auxt, tinv)


def _gdn_scan_fwd(u0, w, qd, kd, qk, gle, *, name):
    tt = u0.shape[0]
    nc = tt // CH

    def body(u0_ref, w_ref, qd_ref, kd_ref, qk_ref, gle_ref, o_ref, ss_ref, u_ref, s_ref):
        @pl.when(pl.program_id(0) == 0)
        def _():
            s_ref[...] = jnp.zeros_like(s_ref)

        gle = gle_ref[0:1, :]
        heads = range(B_HEADS)
        sls = [slice(hd * B_DH, (hd + 1) * B_DH) for hd in heads]
        st = [s_ref[hd] for hd in heads]
        sb = [t.astype(BF16) for t in st]
        ws = [_dot(w_ref[:, sls[hd]], sb[hd]) for hd in heads]
        qs = [_dot(qd_ref[:, sls[hd]], sb[hd]) for hd in heads]
        ub = [(u0_ref[:, sls[hd]] - ws[hd]).astype(BF16) for hd in heads]
        ku = [_dot_tn(kd_ref[:, sls[hd]], ub[hd]) for hd in heads]
        qu = [_dot(qk_ref[hd], ub[hd]) for hd in heads]
        for hd in heads:
            ss_ref[hd] = st[hd]
            u_ref[:, sls[hd]] = ub[hd]
            o_ref[:, sls[hd]] = qs[hd] + qu[hd]
            s_ref[hd] = st[hd] * _col(gle, hd) + ku[hd]

    row = _rb(CH, B_W)
    return pl.pallas_call(
        body, grid=(nc,), in_specs=[row, row, row, row, _CHUNK4(CH, CH), _CHUNK8(LANE)],
        out_specs=(row, _CHUNK4(B_DH, B_DH), row),
        out_shape=(jax.ShapeDtypeStruct((tt, B_W), F32), jax.ShapeDtypeStruct((nc * B_HEADS, B_DH, B_DH), F32),
                   jax.ShapeDtypeStruct((tt, B_W), BF16)),
        scratch_shapes=[pltpu.VMEM((B_HEADS, B_DH, B_DH), F32)],
        compiler_params=_cp(("arbitrary",)), name=name)(u0, w, qd, kd, qk, gle)


def _gdn_scan_bwd(w, qd, kd, qk, gle, do, *, name):
    tt = w.shape[0]
    nc = tt // CH

    def body(w_ref, qd_ref, kd_ref, qk_ref, gle_ref, do_ref, du_ref, dss_ref, ds_ref):
        @pl.when(pl.program_id(0) == 0)
        def _():
            ds_ref[...] = jnp.zeros_like(ds_ref)

        gle = gle_ref[0:1, :]
        heads = range(B_HEADS)
        sls = [slice(hd * B_DH, (hd + 1) * B_DH) for hd in heads]
        dst = [ds_ref[hd] for hd in heads]
        dob = [do_ref[:, sls[hd]].astype(BF16) for hd in heads]
        kds = [_dot(kd_ref[:, sls[hd]], dst[hd].astype(BF16)) for hd in heads]
        qkd = [_dot_tn(qk_ref[hd], dob[hd]) for hd in heads]
        qdd = [_dot_tn(qd_ref[:, sls[hd]], dob[hd]) for hd in heads]
        du = [qkd[hd] + kds[hd] for hd in heads]
        wdu = [_dot_tn(w_ref[:, sls[hd]], du[hd].astype(BF16)) for hd in heads]
        for hd in heads:
            dss_ref[hd] = dst[hd]
            du_ref[:, sls[hd]] = du[hd]
            ds_ref[hd] = qdd[hd] + _col(gle, hd) * dst[hd] - wdu[hd]

    rev = lambda width: pl.BlockSpec((CH, width), lambda i: (nc - 1 - i, 0))
    rev4 = lambda a, b: pl.BlockSpec((B_HEADS, a, b), lambda i: (nc - 1 - i, 0, 0))
    return pl.pallas_call(
        body, grid=(nc,),
        in_specs=[rev(B_W), rev(B_W), rev(B_W), rev4(CH, CH), pl.BlockSpec((8, LANE), lambda i: (nc - 1 - i, 0)), rev(B_W)],
        out_specs=(rev(B_W), rev4(B_DH, B_DH)),
        out_shape=(jax.ShapeDtypeStruct((tt, B_W), F32), jax.ShapeDtypeStruct((nc * B_HEADS, B_DH, B_DH), F32)),
        scratch_shapes=[pltpu.VMEM((B_HEADS, B_DH, B_DH), F32)],
        compiler_params=_cp(("arbitrary",)), name=name)(w, qd, kd, qk, gle, do)


def _gdn_bwd(q, k, v, aux, auxt, tinv, u0, w, u, ss, dss, du, do, *, name):
    tt = q.shape[0]
    nc = tt // CH

    def body(q_ref, k_ref, v_ref, aux_ref, auxt_ref, t_ref, u0_ref, w_ref, u_ref, ss_ref, dss_ref, du_ref, do_ref,
             dq_ref, dk_ref, dv_ref, daux_ref):
        aux_v = aux_ref[...]
        aux_last = aux_ref[CH - 1:CH, :]
        lane = lax.broadcasted_iota(jnp.int32, (CH, LANE), 1)
        rowi = lax.broadcasted_iota(jnp.int32, (CH, 1), 0)
        strict, incl = _chunk_masks()
        daux = jnp.zeros((CH, LANE), F32)
        heads = range(B_HEADS)
        sls = [slice(hd * B_DH, (hd + 1) * B_DH) for hd in heads]
        gates = [_gdn_gates(aux_v, aux_last, auxt_ref, hd) for hd in heads]
        kbs = [k_ref[:, sl].astype(BF16) for sl in sls]
        qbs = [q_ref[:, sl].astype(BF16) for sl in sls]
        sbs = [ss_ref[hd].astype(BF16) for hd in heads]
        dsbs = [dss_ref[hd].astype(BF16) for hd in heads]
        dobs = [do_ref[:, sl].astype(BF16) for sl in sls]
        kks = [_dot_nt(kbs[hd], kbs[hd]) for hd in heads]
        qk0s = [_dot_nt(qbs[hd], kbs[hd]) for hd in heads]
        dq_decs = [_dot_nt(dobs[hd], sbs[hd]) for hd in heads]
        dqks = [_dot_nt(dobs[hd], u_ref[:, sls[hd]]) for hd in heads]
        dk_decs = [_dot_nt(u_ref[:, sls[hd]], dsbs[hd]) for hd in heads]
        dws = [-_dot_nt(du_ref[:, sls[hd]].astype(BF16), sbs[hd]) for hd in heads]
        drvs = [_dot3(t_ref[hd], du_ref[:, sls[hd]], tn=True) for hd in heads]
        drks = [_dot3(t_ref[hd], dws[hd], tn=True) for hd in heads]
        dls = [-(_dot_nt(drvs[hd].astype(BF16), u0_ref[:, sls[hd]].astype(BF16))
                 + _dot_nt(drks[hd].astype(BF16), w_ref[:, sls[hd]])) for hd in heads]
        ldecs = [jnp.where(strict, dls[hd], 0.0) * gates[hd][4] for hd in heads]
        dqkm = [jnp.where(incl, dqks[hd], 0.0) for hd in heads]
        dkks = [(ldecs[hd] * gates[hd][0]).astype(BF16) for hd in heads]
        dqk0s = [(dqkm[hd] * gates[hd][4]).astype(BF16) for hd in heads]
        ddecs = [ldecs[hd] * gates[hd][0] * kks[hd] + dqkm[hd] * (qk0s[hd] * gates[hd][4]) for hd in heads]
        dq_mm = [_dot(dqk0s[hd], kbs[hd]) for hd in heads]
        dk_mm = [_dot_tn(dqk0s[hd], qbs[hd]) + _dot(dkks[hd], kbs[hd]) + _dot_tn(dkks[hd], kbs[hd]) for hd in heads]
        dcols = [_col_from_rowsum(ddecs[hd]) for hd in heads]
        for hd in heads:
            sl = sls[hd]
            qh, kh, vh = q_ref[:, sl], k_ref[:, sl], v_ref[:, sl]
            beta, gc, gl, egc, dec = gates[hd]
            ekd, eg_last = jnp.exp(gl - gc), jnp.exp(gl)
            kk = kks[hd]
            st, dst = ss_ref[hd], dss_ref[hd]
            dq_dec, dk_dec = dq_decs[hd], dk_decs[hd]
            dgl = jnp.sum(jnp.sum(st * dst, axis=-1, keepdims=True), axis=0, keepdims=True) * eg_last
            drv, drk = drvs[hd], drks[hd]
            dv_ref[:, sl] = drv * beta
            rk = jnp.sum(drk * kh, axis=-1, keepdims=True)
            dbeta = jnp.sum(drv * vh, axis=-1, keepdims=True) + rk * egc
            dgc = rk * beta * egc
            dk = drk * (beta * egc)
            ldec, ddec = ldecs[hd], ddecs[hd]
            dbeta = dbeta + jnp.sum(ldec * kk, axis=-1, keepdims=True)
            dq = dq_mm[hd] + dq_dec * egc
            dk = dk + dk_mm[hd] + dk_dec * ekd
            dgc = dgc + jnp.sum(ddec, axis=-1, keepdims=True) - dcols[hd]
            dgc = dgc + jnp.sum(dq_dec * qh, axis=-1, keepdims=True) * egc
            kd = jnp.sum(dk_dec * kh, axis=-1, keepdims=True) * ekd
            dgc = dgc - kd
            dgc = dgc + jnp.where(rowi == CH - 1, jnp.sum(kd, axis=0, keepdims=True) + dgl, 0.0)
            dq_ref[:, sl] = dq
            dk_ref[:, sl] = dk
            daux = daux + jnp.where(lane == hd, dbeta, 0.0) + jnp.where(lane == B_HEADS + hd, dgc, 0.0)
        daux_ref[...] = daux

    row = _rb(NCH * CH, B_W)
    outs = jax.ShapeDtypeStruct((tt, B_W), F32)
    return pl.pallas_call(
        _per_chunk(body, (CH, CH, CH, CH, 8, B_HEADS, CH, CH, CH, B_HEADS, B_HEADS, CH, CH, CH, CH, CH, CH)),
        grid=(nc // NCH,),
        in_specs=[row, row, row, _rb(NCH * CH, LANE), _CHUNK8(CH, NCH), _CHUNK4(CH, CH, NCH), row, row, row,
                  _CHUNK4(B_DH, B_DH, NCH), _CHUNK4(B_DH, B_DH, NCH), row, row],
        out_specs=(row, row, row, _rb(NCH * CH, LANE)),
        out_shape=(outs, outs, outs, jax.ShapeDtypeStruct((tt, LANE), F32)),
        compiler_params=_cp(("parallel",)), name=name)(q, k, v, aux, auxt, tinv, u0, w, u, ss, dss, du, do)


def _col_from_rowsum(m):
    hi, lo = _split(m)
    ones = jnp.ones((CH, LANE), BF16)
    return (_dot_tn(hi, ones) + _dot_tn(lo, ones))[:, 0:1]


def _gdn_post_fwd(o, proj, gn, *, name, tr=256):
    tt = o.shape[0]

    def body(o_ref, z_ref, g_ref, y_ref):
        for hd in range(B_HEADS):
            sl = slice(hd * B_DH, (hd + 1) * B_DH)
            oh = o_ref[:, sl]
            r = lax.rsqrt(jnp.mean(oh * oh, axis=-1, keepdims=True) + EPS)
            y_ref[:, sl] = (oh * r * g_ref[...] * _silu(z_ref[:, sl])).astype(BF16)

    return pl.pallas_call(body, grid=(tt // tr,), in_specs=[_rb(tr, B_W), _rb(tr, B_W, OFF_ZB // B_W), _whole((1, B_DH))],
                          out_specs=_rb(tr, B_W), out_shape=jax.ShapeDtypeStruct((tt, B_W), BF16),
                          compiler_params=_cp(("parallel",)), name=name)(o, proj, gn)


def _gdn_post_bwd(o, proj, gn, dy, *, name, tr=256):
    tt = o.shape[0]

    def body(o_ref, z_ref, g_ref, dy_ref, do_ref, dz_ref, dg_ref):
        @pl.when(pl.program_id(0) == 0)
        def _():
            dg_ref[...] = jnp.zeros_like(dg_ref)

        g = g_ref[...]
        for hd in range(B_HEADS):
            sl = slice(hd * B_DH, (hd + 1) * B_DH)
            oh, zh, dyh = o_ref[:, sl], z_ref[:, sl], dy_ref[:, sl]
            r = lax.rsqrt(jnp.mean(oh * oh, axis=-1, keepdims=True) + EPS)
            a = oh * r
            s = _silu(zh)
            da = dyh * g * s
            dg_ref[0:1, :] += _colsum(dyh * a * s)
            dz_ref[:, sl] = (dyh * a * g * _dsilu(zh)).astype(BF16)
            do_ref[:, sl] = r * (da - a * jnp.mean(da * a, axis=-1, keepdims=True))

    return pl.pallas_call(
        body, grid=(tt // tr,), in_specs=[_rb(tr, B_W), _rb(tr, B_W, OFF_ZB // B_W), _whole((1, B_DH)), _rb(tr, B_W)],
        out_specs=(_rb(tr, B_W), _rb(tr, B_W), _whole((8, B_DH))),
        out_shape=(jax.ShapeDtypeStruct((tt, B_W), F32), jax.ShapeDtypeStruct((tt, B_W), BF16),
                   jax.ShapeDtypeStruct((8, B_DH), F32)),
        compiler_params=_cp(("arbitrary",)), name=name)(o, proj, gn, dy)


def _adamw(parts, w, m, v, own=None, sel=None, *, name, tr=256):
    npart, nl, r, c = parts.shape
    tr = max([t for t in range(8, min(r, tr) + 1, 8) if r % t == 0], default=r)
    tc = c if tr < r or r <= 256 or c % 256 else 256
    c1, c2 = 1.0 - ADAM_B1 ** ADAM_STEP, 1.0 - ADAM_B2 ** ADAM_STEP

    def body(*refs):
        if own is None:
            p_ref, w_ref, m_ref, v_ref, g_ref, d_ref, mo_ref, vo_ref = refs
            part = lambda i: p_ref[i].astype(F32)
        else:
            p_ref, w_ref, m_ref, v_ref, own_ref, sel_ref, g_ref, d_ref, mo_ref, vo_ref = refs
            part = lambda i: jnp.where(sel_ref[i:i + 1, 0:1] > 0.5, own_ref[...].astype(F32), p_ref[i].astype(F32))
        g = part(0)
        for i in range(1, npart):
            g = g + part(i)
        mn = ADAM_B1 * m_ref[...] + (1.0 - ADAM_B1) * g
        vn = ADAM_B2 * v_ref[...] + (1.0 - ADAM_B2) * (g * g)
        g_ref[...] = g
        mo_ref[...] = mn
        vo_ref[...] = vn
        d_ref[...] = -ADAM_LR * ((mn / c1) / (jnp.sqrt(vn / c2) + ADAM_EPS) + ADAM_WD * w_ref[...])

    row = pl.BlockSpec((None, tr, tc), lambda l, i, j: (l, i, j))
    out = jax.ShapeDtypeStruct((nl, r, c), F32)
    ins, in_specs = [parts, w, m, v], [pl.BlockSpec((npart, None, tr, tc), lambda l, i, j: (0, l, i, j)), row, row, row]
    if own is not None:
        ins += [own, sel]
        in_specs += [row, pl.BlockSpec((N_DEV, LANE), lambda l, i, j: (0, 0))]
    return pl.pallas_call(body, grid=(nl, r // tr, c // tc), in_specs=in_specs, out_specs=(row, row, row, row),
                          out_shape=(out, out, out, out), compiler_params=_cp(("parallel", "parallel", "parallel")),
                          name=name)(*ins)


def _peer(k):
    x, y, c = lax.axis_index("x"), lax.axis_index("y"), lax.axis_index("c")
    return ((1 - x) if k & 4 else x, (1 - y) if k & 2 else y, (1 - c) if k & 1 else c)


def _my_index():
    return 4 * lax.axis_index("x") + 2 * lax.axis_index("y") + lax.axis_index("c")


def _index_of(p):
    return 4 * p[0] + 2 * p[1] + p[2]


def _all_gather(xs, *, name):
    n = len(xs)

    def body(*refs):
        x_refs, o_refs = refs[:n], refs[n:2 * n]
        send, recv, loc = refs[2 * n:]
        me = _my_index()
        copies = []
        for a in range(n):
            cp = pltpu.make_async_copy(x_refs[a], o_refs[a].at[me], loc.at[a])
            cp.start()
            copies.append(cp)
        rdmas = []
        for a in range(n):
            for k in range(1, N_DEV):
                r = pltpu.make_async_remote_copy(
                    src_ref=x_refs[a], dst_ref=o_refs[a].at[me], send_sem=send.at[a, k - 1], recv_sem=recv.at[a, k - 1],
                    device_id=_peer(k), device_id_type=pl.DeviceIdType.MESH)
                r.start()
                rdmas.append(r)
        for a in range(n):
            for k in range(1, N_DEV):
                pltpu.make_async_remote_copy(
                    src_ref=x_refs[a], dst_ref=o_refs[a].at[_index_of(_peer(k))], send_sem=send.at[a, k - 1],
                    recv_sem=recv.at[a, k - 1], device_id=_peer(k), device_id_type=pl.DeviceIdType.MESH).wait_recv()
        for r in rdmas:
            r.wait_send()
        for cp in copies:
            cp.wait()

    any_spec = pl.BlockSpec(memory_space=pl.ANY)
    return pl.pallas_call(
        body, in_specs=[any_spec] * n, out_specs=tuple([any_spec] * n),
        out_shape=tuple(jax.ShapeDtypeStruct((N_DEV,) + x.shape, x.dtype) for x in xs),
        scratch_shapes=[pltpu.SemaphoreType.DMA((n, N_DEV - 1)), pltpu.SemaphoreType.DMA((n, N_DEV - 1)),
                        pltpu.SemaphoreType.DMA((n,))],
        name=name)(*xs)


def _all_gather_two_level(xs, *, name):
    n = len(xs)

    def body(*refs):
        x_refs, o_refs = refs[:n], refs[n:2 * n]
        send, recv, loc = refs[2 * n:]
        x, y, c = lax.axis_index("x"), lax.axis_index("y"), lax.axis_index("c")
        me, sibling = (x, y, c), (x, y, 1 - c)
        chips = [(1 - x, y), (x, 1 - y), (1 - x, 1 - y)]

        def copy(a, k, block, to, src=None):
            dst = o_refs[a].at[_index_of(block)]
            return pltpu.make_async_remote_copy(src_ref=dst if src is None else src, dst_ref=dst, send_sem=send.at[a, k],
                                                recv_sem=recv.at[a, k], device_id=to, device_id_type=pl.DeviceIdType.MESH)

        mine = [pltpu.make_async_copy(x_refs[a], o_refs[a].at[_index_of(me)], loc.at[a]) for a in range(n)]
        first = [copy(a, 0, me, sibling, src=x_refs[a]) for a in range(n)]
        first += [copy(a, 1 + j, me, (*chip, c), src=x_refs[a]) for a in range(n) for j, chip in enumerate(chips)]
        for cp in mine + first:
            cp.start()
        passed = []
        for a in range(n):
            for j, chip in enumerate(chips):
                copy(a, 1 + j, (*chip, c), me).wait_recv()
                passed.append(copy(a, 4 + j, (*chip, c), sibling))
                passed[-1].start()
        for a in range(n):
            copy(a, 0, sibling, me).wait_recv()
            for j, chip in enumerate(chips):
                copy(a, 4 + j, (*chip, 1 - c), me).wait_recv()
        for cp in first + passed:
            cp.wait_send()
        for cp in mine:
            cp.wait()

    any_spec = pl.BlockSpec(memory_space=pl.ANY)
    return pl.pallas_call(
        body, in_specs=[any_spec] * n, out_specs=tuple([any_spec] * n),
        out_shape=tuple(jax.ShapeDtypeStruct((N_DEV,) + t.shape, t.dtype) for t in xs),
        scratch_shapes=[pltpu.SemaphoreType.DMA((n, N_DEV - 1)), pltpu.SemaphoreType.DMA((n, N_DEV - 1)),
                        pltpu.SemaphoreType.DMA((n,))],
        name=name)(*xs)


_HBM = pl.BlockSpec(memory_space=pltpu.HBM)
_SEM = pl.BlockSpec(memory_space=pltpu.SEMAPHORE)
_EFFECT = pltpu.SideEffectType.DATAFLOW_SIDE_EFFECTING


def _split_copy(src_ref, land_ref, send, recv, a, k, scatter, slot, sending):
    me, peer = _my_index(), _index_of(_peer(k))
    src = src_ref.at[peer if sending else me] if scatter else src_ref
    land = land_ref.at[me if sending else peer]
    if slot is not None:
        land = land.at[slot]
    sem = a * (N_DEV - 1) + k - 1
    return pltpu.make_async_remote_copy(src_ref=src, dst_ref=land, send_sem=send.at[sem], recv_sem=recv.at[sem],
                                        device_id=_peer(k), device_id_type=pl.DeviceIdType.MESH)


def _exchange_start(srcs, lands, after, *, scatter, slot=None, name):
    n = len(srcs)

    def body(*refs):
        src_refs, land_refs = refs[:n], refs[n:2 * n]
        send, recv, token = refs[2 * n + 1], refs[2 * n + 2], refs[-1]
        for a in range(n):
            for k in range(1, N_DEV):
                _split_copy(src_refs[a], land_refs[a], send, recv, a, k, scatter, slot, True).start()
        token[...] = jnp.zeros_like(token)

    hbm = lambda t: pltpu.HBM(t.shape, t.dtype)
    sems = pltpu.SemaphoreType.DMA((n * (N_DEV - 1),))
    out = pl.pallas_call(
        body, name=name,
        out_shape=(sems, sems, *[hbm(t) for t in srcs], *[hbm(t) for t in lands], jax.ShapeDtypeStruct((8, LANE), F32)),
        in_specs=[_HBM] * (2 * n) + [pl.BlockSpec(memory_space=pl.ANY)],
        out_specs=(_SEM, _SEM, *[_HBM] * (2 * n), pl.BlockSpec(memory_space=pltpu.VMEM)),
        input_output_aliases={i: 2 + i for i in range(2 * n)},
        compiler_params=pltpu.CompilerParams(has_side_effects=_EFFECT),
    )(*[pltpu.with_memory_space_constraint(t, pltpu.HBM) for t in (*srcs, *lands)], after)
    return out[0], out[1], out[2:2 + n], out[2 + n:2 + 2 * n], out[-1]


def _exchange_wait(send, recv, srcs, lands, after, *, scatter, slot=None, name):
    n = len(srcs)

    def body(*refs):
        src_refs, land_refs = refs[:n], refs[n:2 * n]
        send_ref, recv_ref = refs[2 * n], refs[2 * n + 1]
        for a in range(n):
            for k in range(1, N_DEV):
                _split_copy(src_refs[a], land_refs[a], send_ref, recv_ref, a, k, scatter, slot, True).wait_send()
                _split_copy(src_refs[a], land_refs[a], send_ref, recv_ref, a, k, scatter, slot, False).wait_recv()

    hbm = lambda t: pltpu.HBM(t.shape, t.dtype)
    out = pl.pallas_call(
        body, name=name, out_shape=(*[hbm(t) for t in srcs], *[hbm(t) for t in lands]),
        in_specs=[_HBM] * (2 * n) + [_SEM, _SEM, pl.BlockSpec(memory_space=pl.ANY)],
        out_specs=tuple([_HBM] * (2 * n)), input_output_aliases={i: i for i in range(2 * n)},
        compiler_params=pltpu.CompilerParams(has_side_effects=_EFFECT),
    )(*srcs, *lands, send, recv, after)
    return out[:n], out[n:]


def _win_to_mine(wt):
    pad = jnp.zeros((IN_PAD - IN_DIM,) + wt.shape[1:], wt.dtype)
    return jnp.concatenate([wt[3592:5640], wt[0:3584], wt[3584:3592], pad], axis=0)


def _win_from_mine(gt):
    return jnp.concatenate([gt[2048:5632], gt[5632:5640], gt[0:2048]], axis=0)


def _pad_rows(a, mult=8):
    r = (-a.shape[0]) % mult
    return a if r == 0 else jnp.concatenate([a, jnp.zeros((r,) + a.shape[1:], a.dtype)], axis=0)


def _lanes(vec, start):
    return jnp.zeros((1, LANE), F32).at[0, start:start + vec.shape[0]].set(vec)


def _small_spec(depth):
    return (("b_ada", (depth, 6 * D)), ("norm1_g", (depth, D)), ("norm2_g", (depth, D)),
            ("rel_table", (depth, A_HEADS, 2 * A_MAX_REL + 1)), ("a_log", (depth, B_HEADS)),
            ("dt_bias", (depth, B_HEADS)), ("gdn_norm_g", (depth, B_DH)), ("final_g", (D,)))


def _pack_small(d, extra, depth):
    spec = _small_spec(depth)
    rows = -(-(sum(math.prod(s) for _, s in spec) + 1) // (8 * LANE)) * 8
    flat = jnp.concatenate([d[n].reshape(-1).astype(F32) for n, _ in spec] + [extra.reshape(-1)])
    flat = jnp.concatenate([flat, jnp.zeros((rows * LANE - flat.shape[0],), F32)])
    return flat.reshape(rows, LANE)


def _unpack_small(p, depth):
    flat = p.reshape(-1)
    out, off = {}, 0
    for n, s in _small_spec(depth):
        sz = math.prod(s)
        out[n] = flat[off:off + sz].reshape(s)
        off += sz
    return out, flat[off]


def kernel(x, c, w_ada, b_ada, norm1_g, norm2_g, w_in, rel_table, w_conv, a_log, dt_bias, gdn_norm_g, w_branch_a, w_branch_b, w_out, w_ffn_in, w_ffn_out, final_g, loss_target, m_w_ada, m_b_ada, m_norm1_g, m_norm2_g, m_w_in, m_rel_table, m_w_conv, m_a_log, m_dt_bias, m_gdn_norm_g, m_w_branch_a, m_w_branch_b, m_w_out, m_w_ffn_in, m_w_ffn_out, m_final_g, v_w_ada, v_b_ada, v_norm1_g, v_norm2_g, v_w_in, v_rel_table, v_w_conv, v_a_log, v_dt_bias, v_gdn_norm_g, v_w_branch_a, v_w_branch_b, v_w_out, v_w_ffn_in, v_w_ffn_out, v_final_g):
    tt = x.shape[1]
    x0 = x[0]
    tgt = loss_target[0]
    me = _my_index()
    depth = w_in.shape[0]

    tr_ = lambda t: jnp.transpose(t, (0, 2, 1))
    shards = [tr_(w_in).astype(BF16), w_branch_a.astype(BF16), w_branch_b.astype(BF16), w_out.astype(BF16),
              tr_(w_ffn_in).astype(BF16), w_ffn_out.astype(BF16), w_conv]
    names = ("win", "wa", "wb", "wout", "wfi", "wfo", "wconv")
    early, late, every = (0, 6), (1, 2, 3, 4, 5), tuple(range(7))
    first = _all_gather_two_level([shards[i][0] for i in early] + [_pad_rows(c)], name="gather_first")
    c_all = first[-1][:, 0, :]
    is_me = lax.broadcasted_iota(jnp.int32, (N_DEV, 1, 1), 0) == me

    def unpack(idx, g):
        cols = lambda t: jnp.transpose(t, (1, 0, 2)).reshape(t.shape[1], N_DEV * t.shape[2])
        rows = lambda t: t.reshape(N_DEV * t.shape[1], t.shape[2])
        how = (lambda t: _win_to_mine(rows(t)), cols, cols, rows, rows, rows, cols)
        return {names[i]: how[i](t) for i, t in zip(idx, g)}

    def gather_start(l, idx, after, tag=""):
        srcs = [shards[i][l] for i in idx]
        lands = [lax.empty((N_DEV,) + t.shape, t.dtype) for t in srcs]
        return _exchange_start(srcs, lands, after, scatter=False, name=f"gather_start_{l}{tag}")

    def gather_wait(l, idx, pending, after, tag=""):
        send, recv, srcs, lands, _ = pending
        srcs, lands = _exchange_wait(send, recv, srcs, lands, after, scatter=False, name=f"gather_wait_{l}{tag}")
        return unpack(idx, [jnp.where(is_me, t[None], g) for g, t in zip(lands, srcs)])

    weights = [unpack(early, first[:-1])] + [None] * (depth - 1)
    pending0 = gather_start(0, late, first[-1], "_rest")
    pending = gather_start(1, every, pending0[-1]) if depth > 1 else None
    cond = c_all * (1.0 / (1.0 + jnp.exp(-c_all)))
    cond = _pad_rows(cond, 16)

    mod_cols = jnp.stack([_mm(cond, w_ada[l], name="mod_mm")[:N_DEV] for l in range(depth)])
    (g_mod,) = _all_gather([mod_cols], name="gather_mod")
    mod_all = jnp.transpose(g_mod, (1, 2, 0, 3)).reshape(depth, N_DEV, 6 * D)
    mod = lax.dynamic_index_in_dim(mod_all, me, axis=1, keepdims=False) + b_ada
    mods = mod.reshape(depth, 6, 1, D)

    n1g, n2g = norm1_g.reshape(depth, 1, D), norm2_g.reshape(depth, 1, D)
    gng = gdn_norm_g.reshape(depth, 1, B_DH)
    fg = final_g.reshape(1, D)

    saved = []
    tok = (pending if pending is not None else pending0)[-1][0, 0]
    xin, h1 = _adaln_fwd(x0, n1g[0], mods[0, 1] + tok, mods[0, 0], name="adaln1_first")
    for l in range(depth):
        sh1, sc1, gt1, sh2, sc2, gt2 = (mods[l, i] for i in range(6))
        wl = weights[l]
        proj = _mm(h1, wl["win"], tb=True, name="proj_mm", tm=2048, tn=1152)
        kpad, vpad = _kv_pad(proj, name="kv_pad")
        diag, bias_vjp = jax.vjp(_bias_diagonals, rel_table[l])
        bias = _bias_fwd(diag, name="bias_fwd")
        ya, lse = _attn_fwd(proj, kpad, vpad, bias, name="attn_fwd")
        par = jnp.concatenate([_lanes(a_log[l], B_HEADS), _lanes(dt_bias[l], B_HEADS), jnp.zeros((6, LANE), F32)], axis=0)
        qn, kn, vn, aux = _gdn_pre_fwd(proj, wl["wconv"], par, name="gdn_pre_fwd")
        auxt = _gc_rows(aux, tt // CH)
        lower = _gdn_lower(kn, aux, auxt, name="gdn_lower")
        tinv = jnp.transpose(_tri_inverse(jnp.transpose(lower, (1, 2, 0)), name="gdn_tri_inverse"), (2, 0, 1))
        u0, wg, qd, kd, qk, gle = _gdn_intra(qn, kn, vn, aux, auxt, tinv, name="gdn_intra")
        og, ss, ug = _gdn_scan_fwd(u0, wg, qd, kd, qk, gle, name="gdn_scan_fwd")
        yb = _gdn_post_fwd(og, proj, gng[l], name="gdn_post_fwd")
        if l == 0:
            wl.update(gather_wait(0, late, pending0, yb, "_rest"))
        pa, pb, merged = _branch_merge(ya, yb, wl["wa"], wl["wb"], proj, name="branch_merge")
        t1, x2, h2 = _out_adaln(merged, wl["wout"], xin, gt1, n2g[l], sc2, sh2, name="out_adaln2")
        gu, act = _ffn_in_swiglu(h2, wl["wfi"], name="ffn_in_swiglu")
        saved.append(dict(xin=xin, h1=h1, proj=proj, kpad=kpad, vpad=vpad, bias=bias, bias_vjp=bias_vjp, ya=ya, lse=lse,
                          par=par, qn=qn, kn=kn, vn=vn, aux=aux, auxt=auxt, tinv=tinv, ss=ss, og=og, yb=yb, pa=pa, pb=pb,
                          u0=u0, wg=wg, qd=qd, kd=kd, qk=qk, gle=gle, ug=ug,
                          merged=merged, t1=t1, x2=x2, h2=h2, gu=gu, act=act))
        if l + 1 < depth:
            weights[l + 1] = gather_wait(l + 1, every, pending, act)
            pending = gather_start(l + 2, every, weights[l + 1]["wconv"]) if l + 2 < depth else None
            tok = pending[-1][0, 0] if pending is not None else 0.0
            t2, xin, h1 = _out_adaln(act, wl["wfo"], x2, gt2, n1g[l + 1], mods[l + 1, 1] + tok, mods[l + 1, 0],
                                     tk=FTN, name="ffn_out_adaln1")
        else:
            t2 = _mm(act, wl["wfo"], name="ffn_out_mm", tk=FTN)
        saved[-1]["t2"] = t2

    s = saved[-1]
    dx, dt2, st = _loss_head(s["x2"], s["t2"], mods[depth - 1, 5], fg, tgt, name="loss_head")
    loss_part = st[4, 0]
    small_g = {"final_g": st[0]}
    dmod_rows = [None] * depth
    for n in ("norm1_g", "norm2_g", "rel_table", "a_log", "dt_bias", "gdn_norm_g"):
        small_g[n] = [None] * depth
    dgt2 = st[3]
    cols_slabs = lambda g: jnp.transpose(g.reshape(g.shape[0], N_DEV, g.shape[1] // N_DEV), (1, 0, 2))
    rows_slabs = lambda g: g.reshape(N_DEV, g.shape[0] // N_DEV, g.shape[1])
    mix, ffn = (0, 1, 2, 3, 6), (4, 5)
    lands = {kind: [lax.empty((N_DEV,) + shards[i].shape, shards[i].dtype) for i in idx]
             for kind, idx in (("mix", mix), ("ffn", ffn))}
    own = {kind: [None] * depth for kind in lands}
    pending_s = {kind: None for kind in lands}

    def scatter(kind, l, srcs, after):
        if pending_s[kind] is not None:
            done, lands[kind] = _exchange_wait(*pending_s[kind][:4], after, scatter=True, slot=l + 1,
                                               name=f"scatter_wait_{kind}_{l + 1}")
            own[kind][l + 1] = [lax.dynamic_index_in_dim(t, me, 0, keepdims=False) for t in done]
        pending_s[kind] = _exchange_start(srcs, lands[kind], after, scatter=True, slot=l, name=f"scatter_start_{kind}_{l}")
        return pending_s[kind][-1][0, 0]

    for l in reversed(range(depth)):
        s, wl = saved[l], weights[l]
        sh1, sc1, gt1, sh2, sc2, gt2 = (mods[l, i] for i in range(6))
        gw_fo = _mm(s["act"], dt2, ta=True, out_dtype=BF16, name="ffn_out_dw", tm=1408, tk=2048)
        dgu = _ffn_out_bwd_swiglu(dt2, wl["wfo"], s["gu"], name="ffn_out_bwd_swiglu")
        gw_fi = _mm(dgu, s["h2"], ta=True, out_dtype=BF16, name="ffn_in_dw", tm=1408, tk=2048)
        sc2 = sc2 + scatter("ffn", l, [rows_slabs(gw_fi), rows_slabs(gw_fo)], gw_fi)
        dx, dt1, st2 = _mm_adaln_bwd(dgu, wl["wfi"], s["x2"], n2g[l], sc2, sh2, dx, s["t1"], gt1, tk=FTN,
                                     name="ffn_in_dx_adaln2")
        gw_out = _mm(s["merged"], dt1, ta=True, out_dtype=BF16, name="out_dw")
        dgates, dpa, dpb = _out_bwd_merge(dt1, wl["wout"], s["proj"], s["pa"], s["pb"], name="out_bwd_merge")
        gw_a = _mm(s["ya"], dpa, ta=True, out_dtype=BF16, name="branch_a_dw")
        gw_b = _mm(s["yb"], dpb, ta=True, out_dtype=BF16, name="branch_b_dw")
        dya = _mm(dpa, wl["wa"], tb=True, name="branch_a_dx")
        dyb = _mm(dpb, wl["wb"], tb=True, name="branch_b_dx")
        dqa, dka, dva, dbias = _attn_bwd(s["proj"], s["kpad"], s["vpad"], s["bias"], s["ya"], s["lse"], dya,
                                             name="attn_bwd")
        ddiag = jnp.roll(_bias_bwd(dbias, name="bias_bwd"), -(QB - 1), axis=1)
        small_g["rel_table"][l] = s["bias_vjp"](ddiag)[0]
        dog, dz, dgn = _gdn_post_bwd(s["og"], s["proj"], gng[l], dyb, name="gdn_post_bwd")
        small_g["gdn_norm_g"][l] = dgn[0]
        dug, dss = _gdn_scan_bwd(s["wg"], s["qd"], s["kd"], s["qk"], s["gle"], dog, name="gdn_scan_bwd")
        dqn, dkn, dvn, daux = _gdn_bwd(s["qn"], s["kn"], s["vn"], s["aux"], s["auxt"], s["tinv"], s["u0"], s["wg"],
                                       s["ug"], s["ss"], dss, dug, dog, name="gdn_bwd")
        dqkv, dba, dwc, dpar = _gdn_pre_bwd(s["proj"], wl["wconv"], s["par"], dqn, dkn, dvn, daux, name="gdn_pre_bwd")
        small_g["a_log"][l] = dpar[0, B_HEADS:2 * B_HEADS]
        small_g["dt_bias"][l] = dpar[1, B_HEADS:2 * B_HEADS]
        dproj = jnp.concatenate([dgates, dqa, dka, dva, dqkv, dz, dba], axis=1)
        gw_in = _mm(dproj, s["h1"], ta=True, out_dtype=BF16, name="proj_dw", tm=1152, tk=2048)
        mix_srcs = [rows_slabs(_win_from_mine(gw_in)), cols_slabs(gw_a), cols_slabs(gw_b), rows_slabs(gw_out),
                    cols_slabs(dwc[0:CONV_K])]
        if l > 0:
            sc1 = sc1 + scatter("mix", l, mix_srcs, gw_in)
        if l > 0:
            p = saved[l - 1]
            dx, dt2, st1 = _mm_adaln_bwd(dproj, wl["win"], s["xin"], n1g[l], sc1, sh1, dx, p["t2"], mods[l - 1, 5],
                                         tk=1152, name="proj_dx_adaln1")
        else:
            dx, st1 = _mm_adaln_bwd(dproj, wl["win"], s["xin"], n1g[l], sc1, sh1, dx, tk=1152,
                                    name="proj_dx_adaln1_first")
        small_g["norm1_g"][l], small_g["norm2_g"][l] = st1[0], st2[0]
        dmod_rows[l] = jnp.concatenate([st1[2], st1[1], st2[3], st2[2], st2[1], dgt2])
        if l > 0:
            dgt2 = st1[3]
    grad_x = dx[None]

    small_local = {n: (jnp.stack(vs) if isinstance(vs, list) else vs) for n, vs in small_g.items()}
    small_local["b_ada"] = jnp.stack(dmod_rows)
    (g_small,) = _all_gather([_pack_small(small_local, loss_part, depth)], name="gather_small")
    tok = scatter("mix", 0, mix_srcs, g_small)
    wsm = _pack_small(dict(b_ada=b_ada, norm1_g=norm1_g, norm2_g=norm2_g, rel_table=rel_table, a_log=a_log,
                           dt_bias=dt_bias, gdn_norm_g=gdn_norm_g, final_g=final_g), jnp.zeros((1,), F32) + tok, depth)
    msm = _pack_small(dict(b_ada=m_b_ada, norm1_g=m_norm1_g, norm2_g=m_norm2_g, rel_table=m_rel_table, a_log=m_a_log,
                           dt_bias=m_dt_bias, gdn_norm_g=m_gdn_norm_g, final_g=m_final_g), jnp.zeros((1,), F32), depth)
    vsm = _pack_small(dict(b_ada=v_b_ada, norm1_g=v_norm1_g, norm2_g=v_norm2_g, rel_table=v_rel_table, a_log=v_a_log,
                           dt_bias=v_dt_bias, gdn_norm_g=v_gdn_norm_g, final_g=v_final_g), jnp.ones((1,), F32), depth)
    sm = [_unpack_small(t, depth) for t in _adamw(g_small[:, None], wsm[None], msm[None], vsm[None], name="adamw_small")]
    loss = sm[0][1]

    dmod_all = g_small.reshape(N_DEV, -1)[:, :depth * 6 * D].reshape(N_DEV, depth, 6 * D)
    dmod_mine = lax.dynamic_slice_in_dim(dmod_all, me * (6 * D // N_DEV), 6 * D // N_DEV, axis=2)
    g_ada = jnp.stack([_mm(cond + tok, _pad_rows(dmod_mine[:, l], 16), ta=True, name="ada_dw") for l in range(depth)])

    got, mine = {}, {}
    sel = jnp.broadcast_to(jnp.where(is_me[:, :, 0], 1.0, 0.0), (N_DEV, LANE)).astype(F32) + tok

    def finish(kind, idx, after):
        done, lands[kind] = _exchange_wait(*pending_s[kind][:4], after, scatter=True, slot=0, name=f"scatter_wait_{kind}_0")
        own[kind][0] = [lax.dynamic_index_in_dim(t, me, 0, keepdims=False) for t in done]
        for a, i in enumerate(idx):
            got[i] = lands[kind][a]
            mine[i] = jnp.stack([own[kind][l][a] for l in range(depth)])

    def upd(i, w, m, v, name):
        if i in (0, 4):
            return [tr_(t) for t in _adamw(got[i], tr_(w), tr_(m), tr_(v), mine[i], sel, name=name)]
        return _adamw(got[i], w, m, v, mine[i], sel, name=name)

    finish("ffn", ffn, g_ada)
    res = {
        "w_ada": _adamw(g_ada[None], w_ada, m_w_ada, v_w_ada, name="adamw_w_ada"),
        "w_ffn_in": upd(4, w_ffn_in, m_w_ffn_in, v_w_ffn_in, "adamw_w_ffn_in"),
        "w_ffn_out": upd(5, w_ffn_out, m_w_ffn_out, v_w_ffn_out, "adamw_w_ffn_out"),
    }
    done_first = (res["w_ada"][1][0, 0, 0] + res["w_ffn_in"][1][0, 0, 0] + res["w_ffn_out"][1][0, 0, 0] + sm[1][1])
    finish("mix", mix, jnp.zeros((8, LANE), F32) + done_first)
    res.update({
        "w_in": upd(0, w_in, m_w_in, v_w_in, "adamw_w_in"),
        "w_conv": upd(6, w_conv, m_w_conv, v_w_conv, "adamw_w_conv"),
        "w_branch_a": upd(1, w_branch_a, m_w_branch_a, v_w_branch_a, "adamw_w_branch_a"),
        "w_branch_b": upd(2, w_branch_b, m_w_branch_b, v_w_branch_b, "adamw_w_branch_b"),
        "w_out": upd(3, w_out, m_w_out, v_w_out, "adamw_w_out"),
    })
    for n, _ in _small_spec(depth):
        res[n] = [sm[i][0][n] for i in range(4)]
    order = ("w_ada", "b_ada", "norm1_g", "norm2_g", "w_in", "rel_table", "w_conv", "a_log", "dt_bias", "gdn_norm_g",
             "w_branch_a", "w_branch_b", "w_out", "w_ffn_in", "w_ffn_out", "final_g")
    return (loss, grad_x, *[res[n][0] for n in order], *[res[n][1] for n in order],
            *[res[n][2] for n in order], *[res[n][3] for n in order])
```

```python
import functools
import math

import jax
import jax.numpy as jnp
from jax import lax
from jax.experimental import pallas as pl
from jax.experimental.pallas import tpu as pltpu

F32 = jnp.float32
BF16 = jnp.bfloat16
HI = lax.Precision.HIGHEST

N_DEV = 8
D = 1024
DEPTH = 4
CH = 64
EPS = 1e-6
A_HEADS, A_DH = 8, 64
A_W = A_HEADS * A_DH
A_PAST = 8
A_MAX_REL = 128
QB = 256
QH = 128
KB = QB + A_PAST * CH
B_HEADS, B_DH = 4, 128
B_W = B_HEADS * B_DH
CONV_K = 4
FF = 2816
IN_DIM = 5640
IN_PAD = 5760
LANE = 128
NEG = -1e30
VMEM_LIMIT = 48 * 1024 * 1024

ADAM_LR, ADAM_B1, ADAM_B2, ADAM_EPS, ADAM_WD, ADAM_STEP = 0.001, 0.9, 0.999, 1e-08, 0.01, 10

OFF_GA, OFF_GB, OFF_QA, OFF_KA, OFF_VA, OFF_QB, OFF_KB, OFF_VB, OFF_ZB, OFF_BA = (
    0, 1024, 2048, 2560, 3072, 3584, 4096, 4608, 5120, 5632)


def _cp(sem=None):
    return pltpu.CompilerParams(dimension_semantics=sem, vmem_limit_bytes=VMEM_LIMIT)


def _tile(n, pref):
    if n <= pref:
        return n
    best = None
    for t in range(LANE, pref + 1, LANE):
        if n % t == 0:
            best = t
    assert best is not None, (n, pref)
    return best


def _sigmoid(x):
    return 1.0 / (1.0 + jnp.exp(-x))


def _silu(x):
    return x * _sigmoid(x)


def _dsilu(x):
    s = _sigmoid(x)
    return s * (1.0 + x * (1.0 - s))


def _dot(a, b, prec=None):
    return jnp.dot(a, b, preferred_element_type=F32, precision=prec)


def _dot_nt(a, b, prec=None):
    return lax.dot_general(a, b, (((1,), (1,)), ((), ())), preferred_element_type=F32, precision=prec)


def _dot_tn(a, b, prec=None):
    return lax.dot_general(a, b, (((0,), (0,)), ((), ())), preferred_element_type=F32, precision=prec)


def _mm(a, b, *, ta=False, tb=False, out_dtype=F32, name, tm=1024, tn=1024, tk=1024):
    halves = a.ndim == 3
    a_rows, a_cols = (a.shape[1], 2 * a.shape[2]) if halves else a.shape
    m, k = (a_cols, a_rows) if ta else (a_rows, a_cols)
    n = b.shape[0] if tb else b.shape[1]
    assert k == (b.shape[1] if tb else b.shape[0]), (a.shape, b.shape, ta, tb)
    tm, tn, tk = _tile(m, tm), _tile(n, tn), _tile(k, tk)
    nk = k // tk
    dn = (((0 if ta else 1,), (1 if tb else 0,)), ((), ()))

    def body(a_ref, b_ref, o_ref, *acc):
        part = lax.dot_general(a_ref[...].astype(BF16), b_ref[...].astype(BF16), dn, preferred_element_type=F32)
        if nk == 1:
            o_ref[...] = part.astype(out_dtype)
            return
        acc_ref, kk = acc[0], pl.program_id(2)

        @pl.when(kk == 0)
        def _():
            acc_ref[...] = part

        @pl.when(kk > 0)
        def _():
            acc_ref[...] += part

        @pl.when(kk == nk - 1)
        def _():
            o_ref[...] = acc_ref[...].astype(out_dtype)

    if halves:
        per = a.shape[2] // (tm if ta else tk)
        a_spec = (pl.BlockSpec((None, tk, tm), lambda i, j, q: (i // per, q, i % per)) if ta else
                  pl.BlockSpec((None, tm, tk), lambda i, j, q: (q // per, i, q % per)))
    else:
        a_spec = pl.BlockSpec((tk, tm), lambda i, j, q: (q, i)) if ta else pl.BlockSpec((tm, tk), lambda i, j, q: (i, q))
    b_spec = pl.BlockSpec((tn, tk), lambda i, j, q: (j, q)) if tb else pl.BlockSpec((tk, tn), lambda i, j, q: (q, j))
    return pl.pallas_call(
        body, grid=(m // tm, n // tn, nk), in_specs=[a_spec, b_spec],
        out_specs=pl.BlockSpec((tm, tn), lambda i, j, q: (i, j)),
        out_shape=jax.ShapeDtypeStruct((m, n), out_dtype),
        scratch_shapes=[pltpu.VMEM((tm, tn), F32)] if nk > 1 else [],
        compiler_params=_cp(("parallel", "parallel", "arbitrary")), name=name)(a, b)


def _rb(tr, width, cb=0):
    return pl.BlockSpec((tr, width), lambda i: (i, cb))


def _whole(shape):
    nd = len(shape)
    return pl.BlockSpec(shape, lambda i: (0,) * nd)


def _colsum(v):
    return jnp.sum(v, axis=0, keepdims=True)


def _adaln_fwd(x, g, sc, sh, t=None, gt=None, *, name, tr=256):
    tt = x.shape[0]
    res = t is not None

    def body(*refs):
        if res:
            x_ref, t_ref, gt_ref, g_ref, sc_ref, sh_ref, xo_ref, h_ref = refs
            xv = x_ref[...] + gt_ref[...] * t_ref[...]
            xo_ref[...] = xv
        else:
            x_ref, g_ref, sc_ref, sh_ref, h_ref = refs
            xv = x_ref[...]
        r = lax.rsqrt(jnp.mean(xv * xv, axis=-1, keepdims=True) + EPS)
        h_ref[...] = ((xv * r * g_ref[...]) * (1.0 + sc_ref[...]) + sh_ref[...]).astype(BF16)

    row, vec = _rb(tr, D), _whole((1, D))
    if res:
        ins, in_specs = (x, t, gt, g, sc, sh), [row, row, vec, vec, vec, vec]
        out_shape = (jax.ShapeDtypeStruct((tt, D), F32), jax.ShapeDtypeStruct((tt, D), BF16))
        out_specs = (row, row)
    else:
        ins, in_specs = (x, g, sc, sh), [row, vec, vec, vec]
        out_shape, out_specs = jax.ShapeDtypeStruct((tt, D), BF16), row
    out = pl.pallas_call(body, grid=(tt // tr,), in_specs=in_specs, out_specs=out_specs, out_shape=out_shape,
                         compiler_params=_cp(("parallel",)), name=name)(*ins)
    return out if res else (x, out)


def _mm_adaln_bwd(a, b, x, g, sc, sh, dx_in, t=None, gt=None, *, name, tk, tm=512):
    tt = x.shape[0]
    res = t is not None
    halves = a.ndim == 3
    k = 2 * a.shape[2] if halves else a.shape[1]
    tm, nk = _tile(tt, tm), k // tk

    def body(*refs):
        if res:
            a_ref, b_ref, x_ref, g_ref, sc_ref, sh_ref, dxi_ref, t_ref, gt_ref, dx_ref, dt_ref, st_ref, acc_ref = refs
        else:
            a_ref, b_ref, x_ref, g_ref, sc_ref, sh_ref, dxi_ref, dx_ref, st_ref, acc_ref = refs
        i, q = pl.program_id(0), pl.program_id(1)
        part = _dot(a_ref[...], b_ref[...])

        @pl.when((i == 0) & (q == 0))
        def _():
            st_ref[...] = jnp.zeros_like(st_ref)

        @pl.when(q == 0)
        def _():
            acc_ref[...] = part

        @pl.when(q > 0)
        def _():
            acc_ref[...] += part

        @pl.when(q == nk - 1)
        def _():
            xv, dh = x_ref[...], acc_ref[...]
            r = lax.rsqrt(jnp.mean(xv * xv, axis=-1, keepdims=True) + EPS)
            nrm = xv * r
            y = nrm * g_ref[...]
            dy = dh * (1.0 + sc_ref[...])
            dn = dy * g_ref[...]
            dx = dxi_ref[...] + r * (dn - nrm * jnp.mean(dn * nrm, axis=-1, keepdims=True))
            dx_ref[...] = dx
            st_ref[0:1, :] += _colsum(dy * nrm)
            st_ref[1:2, :] += _colsum(dh * y)
            st_ref[2:3, :] += _colsum(dh)
            if res:
                dt_ref[...] = (gt_ref[...] * dx).astype(BF16)
                st_ref[3:4, :] += _colsum(dx * t_ref[...])

    if halves:
        per = a.shape[2] // tk
        a_spec = pl.BlockSpec((None, tm, tk), lambda i, q: (q // per, i, q % per))
    else:
        a_spec = pl.BlockSpec((tm, tk), lambda i, q: (i, q))
    row = pl.BlockSpec((tm, D), lambda i, q: (i, 0))
    vec = pl.BlockSpec((1, D), lambda i, q: (0, 0))
    ins = [a, b, x, g, sc, sh, dx_in]
    in_specs = [a_spec, pl.BlockSpec((tk, D), lambda i, q: (q, 0)), row, vec, vec, vec, row]
    out_shape, out_specs = [jax.ShapeDtypeStruct((tt, D), F32)], [row]
    if res:
        ins += [t, gt]
        in_specs += [row, vec]
        out_shape.append(jax.ShapeDtypeStruct((tt, D), BF16))
        out_specs.append(row)
    out_shape.append(jax.ShapeDtypeStruct((8, D), F32))
    out_specs.append(pl.BlockSpec((8, D), lambda i, q: (0, 0)))
    return pl.pallas_call(body, grid=(tt // tm, nk), in_specs=in_specs, out_specs=tuple(out_specs),
                          out_shape=tuple(out_shape), scratch_shapes=[pltpu.VMEM((tm, D), F32)],
                          compiler_params=_cp(("arbitrary", "arbitrary")), name=name)(*ins)


def _loss_head(x, t, gt, fg, tgt, *, name, tr=256):
    tt = x.shape[0]

    def body(x_ref, t_ref, gt_ref, fg_ref, tgt_ref, dx_ref, dt_ref, st_ref):
        @pl.when(pl.program_id(0) == 0)
        def _():
            st_ref[...] = jnp.zeros_like(st_ref)

        tv = t_ref[...]
        xv = x_ref[...] + gt_ref[...] * tv
        r = lax.rsqrt(jnp.mean(xv * xv, axis=-1, keepdims=True) + EPS)
        nrm = xv * r
        err = nrm * fg_ref[...] - tgt_ref[...]
        st_ref[4:5, :] += 0.5 * jnp.sum(jnp.mean(err * err, axis=-1, keepdims=True), axis=0, keepdims=True)
        dy = err * (1.0 / D)
        dn = dy * fg_ref[...]
        dx = r * (dn - nrm * jnp.mean(dn * nrm, axis=-1, keepdims=True))
        dx_ref[...] = dx
        dt_ref[...] = (gt_ref[...] * dx).astype(BF16)
        st_ref[0:1, :] += _colsum(dy * nrm)
        st_ref[3:4, :] += _colsum(dx * tv)

    row, vec = _rb(tr, D), _whole((1, D))
    return pl.pallas_call(
        body, grid=(tt // tr,), in_specs=[row, row, vec, vec, row], out_specs=(row, row, _whole((8, D))),
        out_shape=(jax.ShapeDtypeStruct((tt, D), F32), jax.ShapeDtypeStruct((tt, D), BF16),
                   jax.ShapeDtypeStruct((8, D), F32)),
        compiler_params=_cp(("arbitrary",)), name=name)(x, t, gt, fg, tgt)


def _branch_merge(ya, yb, wa, wb, proj, *, name, tm=512):
    tt = ya.shape[0]
    tm = _tile(tt, tm)

    def body(ya_ref, yb_ref, wa_ref, wb_ref, ga_ref, gb_ref, pa_ref, pb_ref, o_ref):
        pa = _dot(ya_ref[...], wa_ref[...])
        pb = _dot(yb_ref[...], wb_ref[...])
        pa_ref[...] = pa.astype(BF16)
        pb_ref[...] = pb.astype(BF16)
        o_ref[...] = (_sigmoid(ga_ref[...]) * pa + _sigmoid(gb_ref[...]) * pb).astype(BF16)

    row, half, wsp = _rb(tm, D), _rb(tm, A_W), _whole((A_W, D))
    out = jax.ShapeDtypeStruct((tt, D), BF16)
    return pl.pallas_call(body, grid=(tt // tm,), in_specs=[half, half, wsp, wsp, _rb(tm, D, 0), _rb(tm, D, 1)],
                          out_specs=(row, row, row), out_shape=(out, out, out), compiler_params=_cp(("parallel",)),
                          name=name)(ya, yb, wa, wb, proj, proj)


def _out_adaln(a, w, x, gt, g, sc, sh, *, name, tk=None, tm=512):
    tt, k = a.shape
    tm, tk = _tile(tt, tm), tk or k
    nk = k // tk

    def body(a_ref, w_ref, x_ref, gt_ref, g_ref, sc_ref, sh_ref, t_ref, xo_ref, h_ref):
        q = pl.program_id(1)
        part = _dot(a_ref[...], w_ref[...])

        @pl.when(q == 0)
        def _():
            t_ref[...] = part

        @pl.when(q > 0)
        def _():
            t_ref[...] += part

        @pl.when(q == nk - 1)
        def _():
            xv = x_ref[...] + gt_ref[...] * t_ref[...]
            xo_ref[...] = xv
            r = lax.rsqrt(jnp.mean(xv * xv, axis=-1, keepdims=True) + EPS)
            h_ref[...] = ((xv * r * g_ref[...]) * (1.0 + sc_ref[...]) + sh_ref[...]).astype(BF16)

    row = pl.BlockSpec((tm, D), lambda i, q: (i, 0))
    vec = pl.BlockSpec((1, D), lambda i, q: (0, 0))
    f32 = jax.ShapeDtypeStruct((tt, D), F32)
    return pl.pallas_call(
        body, grid=(tt // tm, nk),
        in_specs=[pl.BlockSpec((tm, tk), lambda i, q: (i, q)), pl.BlockSpec((tk, D), lambda i, q: (q, 0)),
                  row, vec, vec, vec, vec],
        out_specs=(row, row, row), out_shape=(f32, f32, jax.ShapeDtypeStruct((tt, D), BF16)),
        compiler_params=_cp(("parallel", "arbitrary")), name=name)(a, w, x, gt, g, sc, sh)


def _out_bwd_merge(dt, wout, proj, pa, pb, *, name, tm=512):
    tt = dt.shape[0]
    tm = _tile(tt, tm)

    def body(dt_ref, w_ref, ga_ref, gb_ref, pa_ref, pb_ref, dg_ref, dpa_ref, dpb_ref):
        dm_v = _dot_nt(dt_ref[...], w_ref[...])
        sa, sb = _sigmoid(ga_ref[...]), _sigmoid(gb_ref[...])
        dpa_ref[...] = (dm_v * sa).astype(BF16)
        dpb_ref[...] = (dm_v * sb).astype(BF16)
        dg_ref[:, 0:D] = (dm_v * pa_ref[...].astype(F32) * sa * (1.0 - sa)).astype(BF16)
        dg_ref[:, D:2 * D] = (dm_v * pb_ref[...].astype(F32) * sb * (1.0 - sb)).astype(BF16)

    row = _rb(tm, D)
    return pl.pallas_call(
        body, grid=(tt // tm,), in_specs=[row, _whole((D, D)), _rb(tm, D, 0), _rb(tm, D, 1), row, row],
        out_specs=(_rb(tm, 2 * D), row, row),
        out_shape=(jax.ShapeDtypeStruct((tt, 2 * D), BF16), jax.ShapeDtypeStruct((tt, D), BF16),
                   jax.ShapeDtypeStruct((tt, D), BF16)),
        compiler_params=_cp(("parallel",)), name=name)(dt, wout, proj, proj, pa, pb)


FTN = FF // 2


def _ffn_in_swiglu(h, wt, *, name, tm=1024):
    tt = h.shape[0]
    tm = _tile(tt, tm)

    def body(h_ref, wg_ref, wu_ref, gu_ref, act_ref):
        hv = h_ref[...]
        g = _dot_nt(hv, wg_ref[...])
        u = _dot_nt(hv, wu_ref[...])
        gu_ref[0] = g.astype(BF16)
        gu_ref[1] = u.astype(BF16)
        act_ref[...] = (_silu(g) * u).astype(BF16)

    nj = FF // FTN
    return pl.pallas_call(
        body, grid=(tt // tm, nj),
        in_specs=[pl.BlockSpec((tm, D), lambda i, j: (i, 0)), pl.BlockSpec((FTN, D), lambda i, j: (j, 0)),
                  pl.BlockSpec((FTN, D), lambda i, j: (j + nj, 0))],
        out_specs=(pl.BlockSpec((2, tm, FTN), lambda i, j: (0, i, j)), pl.BlockSpec((tm, FTN), lambda i, j: (i, j))),
        out_shape=(jax.ShapeDtypeStruct((2, tt, FF), BF16), jax.ShapeDtypeStruct((tt, FF), BF16)),
        compiler_params=_cp(("parallel", "parallel")), name=name)(h, wt, wt)


def _ffn_out_bwd_swiglu(dt, wo, gu, *, name, tm=1024):
    tt = dt.shape[0]
    tm = _tile(tt, tm)

    def body(dt_ref, wo_ref, gu_ref, dgu_ref):
        da = _dot_nt(dt_ref[...], wo_ref[...])
        g, u = gu_ref[0].astype(F32), gu_ref[1].astype(F32)
        dgu_ref[0] = (da * u * _dsilu(g)).astype(BF16)
        dgu_ref[1] = (da * _silu(g)).astype(BF16)

    blk = pl.BlockSpec((2, tm, FTN), lambda i, j: (0, i, j))
    return pl.pallas_call(
        body, grid=(tt // tm, FF // FTN),
        in_specs=[pl.BlockSpec((tm, D), lambda i, j: (i, 0)), pl.BlockSpec((FTN, D), lambda i, j: (j, 0)), blk],
        out_specs=blk, out_shape=jax.ShapeDtypeStruct((2, tt, FF), BF16),
        compiler_params=_cp(("parallel", "parallel")), name=name)(dt, wo, gu)


BIAS_LW = 1152


def _bias_diagonals(table):
    n_far = A_PAST * CH - A_MAX_REL + 1
    far = jnp.broadcast_to(table[:, 2 * A_MAX_REL:], (A_HEADS, n_far))
    mid = jnp.flip(table[:, 1:2 * A_MAX_REL], axis=1)
    near = jnp.broadcast_to(table[:, 0:1], (A_HEADS, KB - n_far - (2 * A_MAX_REL - 1)))
    pos = jnp.concatenate([far, mid, near], axis=1)
    neg = jnp.broadcast_to(table[:, 2 * A_MAX_REL:], (A_HEADS, QB - 1))
    gap = jnp.zeros((A_HEADS, BIAS_LW - KB - (QB - 1)), F32)
    return jnp.concatenate([pos, gap, neg], axis=1)


def _bias_fwd(diag, *, name):
    def body(w_ref, o_ref):
        qc = lax.broadcasted_iota(jnp.int32, (QB, KB), 0) // CH + A_PAST
        col = lax.broadcasted_iota(jnp.int32, (QB, KB), 1)
        inband = (col // CH <= qc) & (col // CH >= qc - A_PAST)
        for h in range(A_HEADS):
            rows = pltpu.roll(jnp.broadcast_to(w_ref[h:h + 1, :], (QB, BIAS_LW)), 0, 1, stride=1, stride_axis=0)
            for var in range(3):
                o_ref[var, h] = jnp.where(inband & (col >= A_PAST * CH - QB * var), rows[:, :KB], NEG)

    return pl.pallas_call(body, out_shape=jax.ShapeDtypeStruct((3, A_HEADS, QB, KB), F32), compiler_params=_cp(),
                          name=name)(diag)


def _bias_bwd(dbias, *, name):
    def body(d_ref, o_ref):
        r = lax.broadcasted_iota(jnp.int32, (QB, QB), 0)
        c = lax.broadcasted_iota(jnp.int32, (QB, QB), 1)
        flip = jnp.where(r + c == QB - 1, 1.0, 0.0).astype(F32)
        for h in range(A_HEADS):
            x = jnp.concatenate([_dot(flip, d_ref[h], HI), jnp.zeros((QB, BIAS_LW - KB), F32)], axis=1)
            o_ref[h:h + 1, :] = jnp.sum(pltpu.roll(x, 0, 1, stride=1, stride_axis=0), axis=0, keepdims=True)

    return pl.pallas_call(body, out_shape=jax.ShapeDtypeStruct((A_HEADS, BIAS_LW), F32), compiler_params=_cp(),
                          name=name)(dbias)


def _kv_pad(proj, *, name, tr=256):
    tt = proj.shape[0]
    tr = 2 * tr if tt % (2 * tr) == 0 else tr
    npad = A_PAST * CH // tr

    def body(k_ref, v_ref, ko_ref, vo_ref):
        i = pl.program_id(0)

        @pl.when(i < npad)
        def _():
            ko_ref[...] = jnp.zeros_like(ko_ref)
            vo_ref[...] = jnp.zeros_like(vo_ref)

        @pl.when(i >= npad)
        def _():
            ko_ref[...] = k_ref[...].astype(BF16)
            vo_ref[...] = v_ref[...].astype(BF16)

    src = lambda off: pl.BlockSpec((tr, A_W), lambda i: (jnp.maximum(i - npad, 0), off // A_W))
    out = jax.ShapeDtypeStruct((tt + A_PAST * CH, A_W), BF16)
    return pl.pallas_call(body, grid=(tt // tr + npad,), in_specs=[src(OFF_KA), src(OFF_VA)],
                          out_specs=(_rb(tr, A_W), _rb(tr, A_W)), out_shape=(out, out),
                          compiler_params=_cp(("parallel",)), name=name)(proj, proj)


def _attn_fwd(proj, kpad, vpad, bias, *, name):
    tt = proj.shape[0]

    def body(q_ref, k_ref, v_ref, b_ref, o_ref, l_ref):
        q0 = pl.multiple_of(pl.program_id(1) * QB, QB)
        q = q_ref[...] * (A_DH ** -0.5)
        k = k_ref[pl.ds(q0, KB), :]
        v = v_ref[pl.ds(q0, KB), :]
        lane = lax.broadcasted_iota(jnp.int32, (QB, LANE), 1)
        o = jnp.zeros((QB, LANE), F32)
        lse = jnp.zeros((QB, LANE), F32)
        for a in range(2):
            hm = (lane >= A_DH * a) & (lane < A_DH * (a + 1))
            qa = jnp.where(hm, q, 0.0)
            oh, lh = [], []
            for r in range(QB // QH):
                rs = slice(r * QH, (r + 1) * QH)
                s = _dot_nt(qa[rs].astype(BF16), k) + b_ref[a, rs, :]
                m = jnp.max(s, axis=-1, keepdims=True)
                p = jnp.exp(s - m)
                l = jnp.sum(p, axis=-1, keepdims=True)
                oh.append(_dot(p.astype(BF16), v) / l)
                lh.append(m + jnp.log(l))
            o = jnp.where(hm, jnp.concatenate(oh, axis=0), o)
            lse = jnp.where(hm, jnp.concatenate(lh, axis=0), lse)
        o_ref[...] = o.astype(BF16)
        l_ref[...] = lse

    kv = pl.BlockSpec((tt + A_PAST * CH, LANE), lambda h, i: (0, h))
    blk = pl.BlockSpec((QB, LANE), lambda h, i: (i, h))
    return pl.pallas_call(
        body, grid=(A_W // LANE, tt // QB),
        in_specs=[pl.BlockSpec((QB, LANE), lambda h, i: (i, OFF_QA // LANE + h)), kv, kv,
                  pl.BlockSpec((None, 2, QB, KB), lambda h, i: (jnp.minimum(i, 2), h, 0, 0))],
        out_specs=(blk, blk),
        out_shape=(jax.ShapeDtypeStruct((tt, A_W), BF16), jax.ShapeDtypeStruct((tt, A_W), F32)),
        compiler_params=_cp(("parallel", "parallel")), name=name)(proj, kpad, vpad, bias)


def _attn_bwd(proj, kpad, vpad, bias, o, lse, do, *, name):
    tt = proj.shape[0]
    nq = tt // QB

    def body(q_ref, k_ref, v_ref, b_ref, o_ref, l_ref, do_ref, dq_ref, dko_ref, dvo_ref, db_ref, dk_ref, dv_ref):
        @pl.when(pl.program_id(1) == 0)
        def _():
            dk_ref[...] = jnp.zeros_like(dk_ref)
            dv_ref[...] = jnp.zeros_like(dv_ref)
            db_ref[...] = jnp.zeros_like(db_ref)

        q0 = pl.multiple_of(pl.program_id(1) * QB, QB)
        q, do_v, lse = q_ref[...] * (A_DH ** -0.5), do_ref[...], l_ref[...]
        k = k_ref[pl.ds(q0, KB), :]
        v = v_ref[pl.ds(q0, KB), :]
        dsum = do_v * o_ref[...].astype(F32)
        lane = lax.broadcasted_iota(jnp.int32, (QB, LANE), 1)
        dq = jnp.zeros((QB, LANE), F32)
        dk = jnp.zeros((KB, LANE), F32)
        dv = jnp.zeros((KB, LANE), F32)
        for a in range(2):
            hm = (lane >= A_DH * a) & (lane < A_DH * (a + 1))
            qa_f = jnp.where(hm, q, 0.0)
            doa_f = jnp.where(hm, do_v, 0.0)
            lse_a = jnp.max(jnp.where(hm, lse, NEG), axis=-1, keepdims=True)
            dsum_a = jnp.sum(jnp.where(hm, dsum, 0.0), axis=-1, keepdims=True)
            dqh = []
            for r in range(QB // QH):
                rs = slice(r * QH, (r + 1) * QH)
                qa, doa = qa_f[rs].astype(BF16), doa_f[rs].astype(BF16)
                s = _dot_nt(qa, k) + b_ref[a, rs, :]
                p = jnp.exp(s - lse_a[rs])
                dp = _dot_nt(doa, v)
                ds = p * (dp - dsum_a[rs])
                db_ref[a, rs, :] += ds
                dsb = ds.astype(BF16)
                dqh.append(_dot(dsb, k) * (A_DH ** -0.5))
                dk += _dot_tn(dsb, qa)
                dv += _dot_tn(p.astype(BF16), doa)
            dq = jnp.where(hm, jnp.concatenate(dqh, axis=0), dq)
        dq_ref[...] = dq.astype(BF16)
        dk_ref[pl.ds(q0, KB), :] += dk
        dv_ref[pl.ds(q0, KB), :] += dv

        @pl.when(pl.program_id(1) == nq - 1)
        def _():
            dko_ref[...] = dk_ref[A_PAST * CH:, :].astype(BF16)
            dvo_ref[...] = dv_ref[A_PAST * CH:, :].astype(BF16)

    kv = pl.BlockSpec((tt + A_PAST * CH, LANE), lambda h, i: (0, h))
    blk = pl.BlockSpec((QB, LANE), lambda h, i: (i, h))
    col = pl.BlockSpec((tt, LANE), lambda h, i: (0, h))
    bsp = pl.BlockSpec((2, QB, KB), lambda h, i: (h, 0, 0))
    bias_in = pl.BlockSpec((None, 2, QB, KB), lambda h, i: (jnp.minimum(i, 2), h, 0, 0))
    out = jax.ShapeDtypeStruct((tt, A_W), BF16)
    return pl.pallas_call(
        body, grid=(A_W // LANE, nq),
        in_specs=[pl.BlockSpec((QB, LANE), lambda h, i: (i, OFF_QA // LANE + h)), kv, kv, bias_in, blk, blk, blk],
        out_specs=(blk, col, col, bsp),
        out_shape=(out, out, out, jax.ShapeDtypeStruct((A_HEADS, QB, KB), F32)),
        scratch_shapes=[pltpu.VMEM((tt + A_PAST * CH, LANE), F32), pltpu.VMEM((tt + A_PAST * CH, LANE), F32)],
        compiler_params=_cp(("parallel", "arbitrary")), name=name)(proj, kpad, vpad, bias, o, lse, do)


GTR = 256


def _taps(w_ref, grp):
    return [w_ref[j:j + 1, grp * B_W:(grp + 1) * B_W] for j in range(CONV_K)]


def _shifts(xe, rows):
    return [xe[8:8 + rows]] + [pltpu.roll(xe, s, 0)[8:8 + rows] for s in range(1, CONV_K)]


def _conv(shifts, taps):
    acc = taps[CONV_K - 1] * shifts[0]
    for s in range(1, CONV_K):
        acc = acc + taps[CONV_K - 1 - s] * shifts[s]
    return acc


def _qk_scale(grp):
    return B_DH ** -0.5 if grp == 0 else 1.0


def _act_fwd(c, grp):
    y = _silu(c)
    if grp == 2:
        return y
    parts = []
    for hd in range(B_HEADS):
        yh = y[:, hd * B_DH:(hd + 1) * B_DH]
        parts.append(yh * (lax.rsqrt(jnp.sum(yh * yh, axis=-1, keepdims=True) + EPS) * _qk_scale(grp)))
    return jnp.concatenate(parts, axis=1)


def _act_bwd(c, dy, grp):
    if grp == 2:
        return dy * _dsilu(c)
    y = _silu(c)
    parts = []
    for hd in range(B_HEADS):
        yh = y[:, hd * B_DH:(hd + 1) * B_DH]
        r = lax.rsqrt(jnp.sum(yh * yh, axis=-1, keepdims=True) + EPS)
        dyh = dy[:, hd * B_DH:(hd + 1) * B_DH] * _qk_scale(grp)
        parts.append(r * dyh - yh * (r * r * r) * jnp.sum(dyh * yh, axis=-1, keepdims=True))
    return jnp.concatenate(parts, axis=1) * _dsilu(c)


def _chunk_tri(n, upper=False):
    r = lax.broadcasted_iota(jnp.int32, (n, n), 0)
    c = lax.broadcasted_iota(jnp.int32, (n, n), 1)
    same = (r // CH) == (c // CH)
    return jnp.where(same & ((r <= c) if upper else (r >= c)), 1.0, 0.0).astype(F32)


def _gate_rows(ba, par_ref):
    lane = lax.broadcasted_iota(jnp.int32, ba.shape, 1)
    z = ba + par_ref[1:2, :]
    sp = jnp.maximum(z, 0.0) + jnp.log(1.0 + jnp.exp(-jnp.abs(z)))
    g = -jnp.exp(par_ref[0:1, :]) * sp
    return jnp.where(lane < B_HEADS, _sigmoid(ba), jnp.where(lane < 2 * B_HEADS, g, 0.0)), z


def _prev8(cb):
    return pl.BlockSpec((8, B_W), lambda i: (jnp.maximum(i * (GTR // 8) - 1, 0), cb))


def _next8(cb, nb):
    return pl.BlockSpec((8, B_W), lambda i: (jnp.minimum((i + 1) * (GTR // 8), nb * (GTR // 8) - 1), cb))


def _gdn_pre_fwd(proj, wconv, par, *, name):
    tt = proj.shape[0]

    def body(q_ref, k_ref, v_ref, qh_ref, kh_ref, vh_ref, ba_ref, w_ref, par_ref, qo_ref, ko_ref, vo_ref, aux_ref):
        first = pl.program_id(0) == 0
        for grp, (x_ref, h_ref, o_ref) in enumerate(((q_ref, qh_ref, qo_ref), (k_ref, kh_ref, ko_ref),
                                                     (v_ref, vh_ref, vo_ref))):
            xe = jnp.concatenate([jnp.where(first, 0.0, h_ref[...]), x_ref[...]], axis=0)
            o_ref[...] = _act_fwd(_conv(_shifts(xe, GTR), _taps(w_ref, grp)), grp)
        bg, _ = _gate_rows(ba_ref[...], par_ref)
        lane = lax.broadcasted_iota(jnp.int32, bg.shape, 1)
        aux_ref[...] = jnp.where(lane < B_HEADS, bg, _dot(_chunk_tri(GTR), bg, HI))

    col = lambda off: _rb(GTR, B_W, off // B_W)
    outs = jax.ShapeDtypeStruct((tt, B_W), F32)
    return pl.pallas_call(
        body, grid=(tt // GTR,),
        in_specs=[col(OFF_QB), col(OFF_KB), col(OFF_VB), _prev8(OFF_QB // B_W), _prev8(OFF_KB // B_W),
                  _prev8(OFF_VB // B_W), _rb(GTR, LANE, OFF_BA // LANE), _whole((CONV_K, 3 * B_W)),
                  _whole((8, LANE))],
        out_specs=(_rb(GTR, B_W), _rb(GTR, B_W), _rb(GTR, B_W), _rb(GTR, LANE)),
        out_shape=(outs, outs, outs, jax.ShapeDtypeStruct((tt, LANE), F32)),
        compiler_params=_cp(("parallel",)), name=name)(proj, proj, proj, proj, proj, proj, proj, wconv, par)


def _gdn_pre_bwd(proj, wconv, par, dq, dk, dv, daux, *, name):
    tt = proj.shape[0]
    nb = tt // GTR

    def body(q_ref, k_ref, v_ref, qh_ref, kh_ref, vh_ref, qn_ref, kn_ref, vn_ref, ba_ref, w_ref, par_ref,
             dq_ref, dk_ref, dv_ref, dqn_ref, dkn_ref, dvn_ref, daux_ref, dx_ref, dba_ref, dw_ref, dpar_ref):
        i = pl.program_id(0)
        first, last = i == 0, i == nb - 1

        @pl.when(first)
        def _():
            dw_ref[...] = jnp.zeros_like(dw_ref)
            dpar_ref[...] = jnp.zeros_like(dpar_ref)

        groups = ((q_ref, qh_ref, qn_ref, dq_ref, dqn_ref), (k_ref, kh_ref, kn_ref, dk_ref, dkn_ref),
                  (v_ref, vh_ref, vn_ref, dv_ref, dvn_ref))
        for grp, (x_ref, h_ref, xn_ref, d_ref, dn_ref) in enumerate(groups):
            taps = _taps(w_ref, grp)
            xe = jnp.concatenate([jnp.where(first, 0.0, h_ref[...]), x_ref[...]], axis=0)
            sh = _shifts(xe, GTR)
            dc = _act_bwd(_conv(sh, taps), d_ref[...], grp)
            xe_n = jnp.concatenate([x_ref[GTR - 8:GTR, :], xn_ref[...]], axis=0)
            dcn = _act_bwd(_conv(_shifts(xe_n, 8), taps), dn_ref[...], grp)
            dce = jnp.concatenate([dc, jnp.where(last, 0.0, dcn)], axis=0)
            dx = taps[CONV_K - 1] * dc
            dw_ref[CONV_K - 1:CONV_K, grp * B_W:(grp + 1) * B_W] += _colsum(dc * sh[0])
            for s in range(1, CONV_K):
                dx = dx + taps[CONV_K - 1 - s] * pltpu.roll(dce, GTR + 8 - s, 0)[0:GTR]
                dw_ref[CONV_K - 1 - s:CONV_K - s, grp * B_W:(grp + 1) * B_W] += _colsum(dc * sh[s])
            dx_ref[:, grp * B_W:(grp + 1) * B_W] = dx.astype(BF16)
        ba = ba_ref[...]
        lane = lax.broadcasted_iota(jnp.int32, ba.shape, 1)
        bg, z = _gate_rows(ba, par_ref)
        daux_v = daux_ref[...]
        dg = _dot(_chunk_tri(GTR, upper=True), daux_v, HI)
        dgl = jnp.where((lane >= B_HEADS) & (lane < 2 * B_HEADS), dg, 0.0)
        da = dgl * (-jnp.exp(par_ref[0:1, :])) * _sigmoid(z)
        dbr = jnp.where(lane < B_HEADS, daux_v * bg * (1.0 - bg), 0.0)
        dba_ref[...] = (dbr + da).astype(BF16)
        dpar_ref[0:1, :] += _colsum(dgl * bg)
        dpar_ref[1:2, :] += _colsum(da)

    col = lambda off: _rb(GTR, B_W, off // B_W)
    row, rowl = _rb(GTR, B_W), _rb(GTR, LANE)
    return pl.pallas_call(
        body, grid=(nb,),
        in_specs=[col(OFF_QB), col(OFF_KB), col(OFF_VB),
                  _prev8(OFF_QB // B_W), _prev8(OFF_KB // B_W), _prev8(OFF_VB // B_W),
                  _next8(OFF_QB // B_W, nb), _next8(OFF_KB // B_W, nb), _next8(OFF_VB // B_W, nb),
                  _rb(GTR, LANE, OFF_BA // LANE), _whole((CONV_K, 3 * B_W)), _whole((8, LANE)),
                  row, row, row, _next8(0, nb), _next8(0, nb), _next8(0, nb), rowl],
        out_specs=(_rb(GTR, 3 * B_W), rowl, _whole((8, 3 * B_W)), _whole((8, LANE))),
        out_shape=(jax.ShapeDtypeStruct((tt, 3 * B_W), BF16), jax.ShapeDtypeStruct((tt, LANE), BF16),
                   jax.ShapeDtypeStruct((8, 3 * B_W), F32), jax.ShapeDtypeStruct((8, LANE), F32)),
        compiler_params=_cp(("arbitrary",)), name=name)(
            proj, proj, proj, proj, proj, proj, proj, proj, proj, proj, wconv, par, dq, dk, dv, dq, dk, dv, daux)


def _col(x, j):
    lane = lax.broadcasted_iota(jnp.int32, x.shape, 1)
    return jnp.sum(jnp.where(lane == j, x, 0.0), axis=-1, keepdims=True)


def _split(x):
    hi = x.astype(BF16)
    return hi, (x - hi.astype(F32)).astype(BF16)


def _dot3(a, b, tn=False):
    dot = _dot_tn if tn else _dot
    (ah, al), (bh, bl) = _split(a), _split(b)
    return dot(ah, bh) + (dot(ah, bl) + dot(al, bh))


def _chunk_masks():
    r = lax.broadcasted_iota(jnp.int32, (CH, CH), 0)
    c = lax.broadcasted_iota(jnp.int32, (CH, CH), 1)
    return r > c, r >= c


def _gc_rows(aux, nc):
    t = jnp.transpose(aux[:, B_HEADS:2 * B_HEADS].reshape(nc, CH, B_HEADS), (0, 2, 1))
    return jnp.concatenate([t, jnp.zeros_like(t)], axis=1).reshape(nc * 8, CH)


_CHUNK8 = lambda width, n=1: pl.BlockSpec((8 * n, width), lambda i: (i, 0))
_CHUNK4 = lambda a, b, n=1: pl.BlockSpec((B_HEADS * n, a, b), lambda i: (i, 0, 0))
NCH = 4


def _per_chunk(body, rows):
    def wrapped(*refs):
        for ci in range(NCH):
            body(*[r.at[pl.ds(ci * n, n)] for r, n in zip(refs, rows)])
    return wrapped


def _gdn_lower(k, aux, auxt, *, name):
    tt = k.shape[0]

    def body(k_ref, aux_ref, auxt_ref, l_ref):
        aux_v = aux_ref[...]
        strict, _ = _chunk_masks()
        khs = [k_ref[:, hd * B_DH:(hd + 1) * B_DH].astype(BF16) for hd in range(B_HEADS)]
        kks = [_dot_nt(kh, kh) for kh in khs]
        for hd in range(B_HEADS):
            diff = _col(aux_v, B_HEADS + hd) - auxt_ref[hd:hd + 1, :]
            dec = jnp.exp(jnp.where(strict, diff, NEG))
            l_ref[hd] = _col(aux_v, hd) * kks[hd] * dec

    return pl.pallas_call(
        _per_chunk(body, (CH, CH, 8, B_HEADS)), grid=(tt // CH // NCH,),
        in_specs=[_rb(NCH * CH, B_W), _rb(NCH * CH, LANE), _CHUNK8(CH, NCH)],
        out_specs=_CHUNK4(CH, CH, NCH),
        out_shape=jax.ShapeDtypeStruct((tt // CH * B_HEADS, CH, CH), F32),
        compiler_params=_cp(("parallel",)), name=name)(k, aux, auxt)


def _tri_inverse(lt, *, name):
    nb = lt.shape[2]

    def body(l_ref, t_ref):
        rowid = lax.broadcasted_iota(jnp.int32, (CH, nb), 0)

        def outer(i, carry):
            def inner(j, acc):
                return acc + l_ref[i, pl.ds(j, 1), :] * t_ref[j]

            acc = lax.fori_loop(0, i, inner, jnp.zeros((CH, nb), F32))
            t_ref[i] = jnp.where(rowid == i, 1.0, 0.0) - acc
            return carry

        lax.fori_loop(0, CH, outer, 0)

    return pl.pallas_call(body, out_shape=jax.ShapeDtypeStruct(lt.shape, F32),
                          in_specs=[pl.BlockSpec(memory_space=pltpu.VMEM)],
                          out_specs=pl.BlockSpec(memory_space=pltpu.VMEM),
                          compiler_params=_cp(), name=name)(lt)


def _gdn_gates(aux_v, aux_last, auxt_ref, hd):
    _, incl = _chunk_masks()
    beta = _col(aux_v, hd)
    gc = _col(aux_v, B_HEADS + hd)
    gl = _col(aux_last, B_HEADS + hd)
    dec = jnp.exp(jnp.where(incl, gc - auxt_ref[hd:hd + 1, :], NEG))
    return beta, gc, gl, jnp.exp(gc), dec


def _gdn_intra(q, k, v, aux, auxt, tinv, *, name):
    tt = q.shape[0]
    nc = tt // CH

    def body(q_ref, k_ref, v_ref, aux_ref, auxt_ref, t_ref, u0_ref, w_ref, qd_ref, kd_ref, qk_ref, gle_ref):
        aux_v = aux_ref[...]
        aux_last = aux_ref[CH - 1:CH, :]
        lane8 = lax.broadcasted_iota(jnp.int32, (8, LANE), 1)
        gle = jnp.zeros((8, LANE), F32)
        heads = range(B_HEADS)
        sls = [slice(hd * B_DH, (hd + 1) * B_DH) for hd in heads]
        gates = [_gdn_gates(aux_v, aux_last, auxt_ref, hd) for hd in heads]
        qk0 = [_dot_nt(q_ref[:, sls[hd]].astype(BF16), k_ref[:, sls[hd]].astype(BF16)) for hd in heads]
        u0 = [_dot3(t_ref[hd], v_ref[:, sls[hd]] * gates[hd][0]) for hd in heads]
        wk = [_dot3(t_ref[hd], k_ref[:, sls[hd]] * (gates[hd][0] * gates[hd][3])) for hd in heads]
        for hd in heads:
            sl = sls[hd]
            beta, gc, gl, egc, dec = gates[hd]
            qk_ref[hd] = (qk0[hd] * dec).astype(BF16)
            u0_ref[:, sl] = u0[hd]
            w_ref[:, sl] = wk[hd].astype(BF16)
            qd_ref[:, sl] = (q_ref[:, sl] * egc).astype(BF16)
            kd_ref[:, sl] = (k_ref[:, sl] * jnp.exp(gl - gc)).astype(BF16)
            gle = gle + jnp.where(lane8 == hd, jnp.exp(gl), 0.0)
        gle_ref[...] = gle

    row = _rb(NCH * CH, B_W)
    half = jax.ShapeDtypeStruct((tt, B_W), BF16)
    return pl.pallas_call(
        _per_chunk(body, (CH, CH, CH, CH, 8, B_HEADS, CH, CH, CH, CH, B_HEADS, 8)), grid=(nc // NCH,),
        in_specs=[row, row, row, _rb(NCH * CH, LANE), _CHUNK8(CH, NCH), _CHUNK4(CH, CH, NCH)],
        out_specs=(row, row, row, row, _CHUNK4(CH, CH, NCH), _CHUNK8(LANE, NCH)),
        out_shape=(jax.ShapeDtypeStruct((tt, B_W), F32), half, half, half,
                   jax.ShapeDtypeStruct((nc * B_HEADS, CH, CH), BF16), jax.ShapeDtypeStruct((nc * 8, LANE), F32)),
        compiler_params=_cp(("parallel",)), name=name)(q, k, v, aux, auxt, tinv)


def _gdn_scan_fwd(u0, w, qd, kd, qk, gle, *, name):
    tt = u0.shape[0]
    nc = tt // CH

    def body(u0_ref, w_ref, qd_ref, kd_ref, qk_ref, gle_ref, o_ref, ss_ref, u_ref, s_ref):
        @pl.when(pl.program_id(0) == 0)
        def _():
            s_ref[...] = jnp.zeros_like(s_ref)

        gle = gle_ref[0:1, :]
        heads = range(B_HEADS)
        sls = [slice(hd * B_DH, (hd + 1) * B_DH) for hd in heads]
        st = [s_ref[hd] for hd in heads]
        sb = [t.astype(BF16) for t in st]
        ws = [_dot(w_ref[:, sls[hd]], sb[hd]) for hd in heads]
        qs = [_dot(qd_ref[:, sls[hd]], sb[hd]) for hd in heads]
        ub = [(u0_ref[:, sls[hd]] - ws[hd]).astype(BF16) for hd in heads]
        ku = [_dot_tn(kd_ref[:, sls[hd]], ub[hd]) for hd in heads]
        qu = [_dot(qk_ref[hd], ub[hd]) for hd in heads]
        for hd in heads:
            ss_ref[hd] = st[hd]
            u_ref[:, sls[hd]] = ub[hd]
            o_ref[:, sls[hd]] = qs[hd] + qu[hd]
            s_ref[hd] = st[hd] * _col(gle, hd) + ku[hd]

    row = _rb(CH, B_W)
    return pl.pallas_call(
        body, grid=(nc,), in_specs=[row, row, row, row, _CHUNK4(CH, CH), _CHUNK8(LANE)],
        out_specs=(row, _CHUNK4(B_DH, B_DH), row),
        out_shape=(jax.ShapeDtypeStruct((tt, B_W), F32), jax.ShapeDtypeStruct((nc * B_HEADS, B_DH, B_DH), F32),
                   jax.ShapeDtypeStruct((tt, B_W), BF16)),
        scratch_shapes=[pltpu.VMEM((B_HEADS, B_DH, B_DH), F32)],
        compiler_params=_cp(("arbitrary",)), name=name)(u0, w, qd, kd, qk, gle)


def _gdn_scan_bwd(w, qd, kd, qk, gle, do, *, name):
    tt = w.shape[0]
    nc = tt // CH

    def body(w_ref, qd_ref, kd_ref, qk_ref, gle_ref, do_ref, du_ref, dss_ref, ds_ref):
        @pl.when(pl.program_id(0) == 0)
        def _():
            ds_ref[...] = jnp.zeros_like(ds_ref)

        gle = gle_ref[0:1, :]
        heads = range(B_HEADS)
        sls = [slice(hd * B_DH, (hd + 1) * B_DH) for hd in heads]
        dst = [ds_ref[hd] for hd in heads]
        dob = [do_ref[:, sls[hd]].astype(BF16) for hd in heads]
        kds = [_dot(kd_ref[:, sls[hd]], dst[hd].astype(BF16)) for hd in heads]
        qkd = [_dot_tn(qk_ref[hd], dob[hd]) for hd in heads]
        qdd = [_dot_tn(qd_ref[:, sls[hd]], dob[hd]) for hd in heads]
        du = [qkd[hd] + kds[hd] for hd in heads]
        wdu = [_dot_tn(w_ref[:, sls[hd]], du[hd].astype(BF16)) for hd in heads]
        for hd in heads:
            dss_ref[hd] = dst[hd]
            du_ref[:, sls[hd]] = du[hd]
            ds_ref[hd] = qdd[hd] + _col(gle, hd) * dst[hd] - wdu[hd]

    rev = lambda width: pl.BlockSpec((CH, width), lambda i: (nc - 1 - i, 0))
    rev4 = lambda a, b: pl.BlockSpec((B_HEADS, a, b), lambda i: (nc - 1 - i, 0, 0))
    return pl.pallas_call(
        body, grid=(nc,),
        in_specs=[rev(B_W), rev(B_W), rev(B_W), rev4(CH, CH), pl.BlockSpec((8, LANE), lambda i: (nc - 1 - i, 0)), rev(B_W)],
        out_specs=(rev(B_W), rev4(B_DH, B_DH)),
        out_shape=(jax.ShapeDtypeStruct((tt, B_W), F32), jax.ShapeDtypeStruct((nc * B_HEADS, B_DH, B_DH), F32)),
        scratch_shapes=[pltpu.VMEM((B_HEADS, B_DH, B_DH), F32)],
        compiler_params=_cp(("arbitrary",)), name=name)(w, qd, kd, qk, gle, do)


def _gdn_bwd(q, k, v, aux, auxt, tinv, u0, w, u, ss, dss, du, do, *, name):
    tt = q.shape[0]
    nc = tt // CH

    def body(q_ref, k_ref, v_ref, aux_ref, auxt_ref, t_ref, u0_ref, w_ref, u_ref, ss_ref, dss_ref, du_ref, do_ref,
             dq_ref, dk_ref, dv_ref, daux_ref):
        aux_v = aux_ref[...]
        aux_last = aux_ref[CH - 1:CH, :]
        lane = lax.broadcasted_iota(jnp.int32, (CH, LANE), 1)
        rowi = lax.broadcasted_iota(jnp.int32, (CH, 1), 0)
        strict, incl = _chunk_masks()
        daux = jnp.zeros((CH, LANE), F32)
        heads = range(B_HEADS)
        sls = [slice(hd * B_DH, (hd + 1) * B_DH) for hd in heads]
        gates = [_gdn_gates(aux_v, aux_last, auxt_ref, hd) for hd in heads]
        kbs = [k_ref[:, sl].astype(BF16) for sl in sls]
        qbs = [q_ref[:, sl].astype(BF16) for sl in sls]
        sbs = [ss_ref[hd].astype(BF16) for hd in heads]
        dsbs = [dss_ref[hd].astype(BF16) for hd in heads]
        dobs = [do_ref[:, sl].astype(BF16) for sl in sls]
        kks = [_dot_nt(kbs[hd], kbs[hd]) for hd in heads]
        qk0s = [_dot_nt(qbs[hd], kbs[hd]) for hd in heads]
        dq_decs = [_dot_nt(dobs[hd], sbs[hd]) for hd in heads]
        dqks = [_dot_nt(dobs[hd], u_ref[:, sls[hd]]) for hd in heads]
        dk_decs = [_dot_nt(u_ref[:, sls[hd]], dsbs[hd]) for hd in heads]
        dws = [-_dot_nt(du_ref[:, sls[hd]].astype(BF16), sbs[hd]) for hd in heads]
        drvs = [_dot3(t_ref[hd], du_ref[:, sls[hd]], tn=True) for hd in heads]
        drks = [_dot3(t_ref[hd], dws[hd], tn=True) for hd in heads]
        dls = [-(_dot_nt(drvs[hd].astype(BF16), u0_ref[:, sls[hd]].astype(BF16))
                 + _dot_nt(drks[hd].astype(BF16), w_ref[:, sls[hd]])) for hd in heads]
        ldecs = [jnp.where(strict, dls[hd], 0.0) * gates[hd][4] for hd in heads]
        dqkm = [jnp.where(incl, dqks[hd], 0.0) for hd in heads]
        dkks = [(ldecs[hd] * gates[hd][0]).astype(BF16) for hd in heads]
        dqk0s = [(dqkm[hd] * gates[hd][4]).astype(BF16) for hd in heads]
        ddecs = [ldecs[hd] * gates[hd][0] * kks[hd] + dqkm[hd] * (qk0s[hd] * gates[hd][4]) for hd in heads]
        dq_mm = [_dot(dqk0s[hd], kbs[hd]) for hd in heads]
        dk_mm = [_dot_tn(dqk0s[hd], qbs[hd]) + _dot(dkks[hd], kbs[hd]) + _dot_tn(dkks[hd], kbs[hd]) for hd in heads]
        dcols = [_col_from_rowsum(ddecs[hd]) for hd in heads]
        for hd in heads:
            sl = sls[hd]
            qh, kh, vh = q_ref[:, sl], k_ref[:, sl], v_ref[:, sl]
            beta, gc, gl, egc, dec = gates[hd]
            ekd, eg_last = jnp.exp(gl - gc), jnp.exp(gl)
            kk = kks[hd]
            st, dst = ss_ref[hd], dss_ref[hd]
            dq_dec, dk_dec = dq_decs[hd], dk_decs[hd]
            dgl = jnp.sum(jnp.sum(st * dst, axis=-1, keepdims=True), axis=0, keepdims=True) * eg_last
            drv, drk = drvs[hd], drks[hd]
            dv_ref[:, sl] = drv * beta
            rk = jnp.sum(drk * kh, axis=-1, keepdims=True)
            dbeta = jnp.sum(drv * vh, axis=-1, keepdims=True) + rk * egc
            dgc = rk * beta * egc
            dk = drk * (beta * egc)
            ldec, ddec = ldecs[hd], ddecs[hd]
            dbeta = dbeta + jnp.sum(ldec * kk, axis=-1, keepdims=True)
            dq = dq_mm[hd] + dq_dec * egc
            dk = dk + dk_mm[hd] + dk_dec * ekd
            dgc = dgc + jnp.sum(ddec, axis=-1, keepdims=True) - dcols[hd]
            dgc = dgc + jnp.sum(dq_dec * qh, axis=-1, keepdims=True) * egc
            kd = jnp.sum(dk_dec * kh, axis=-1, keepdims=True) * ekd
            dgc = dgc - kd
            dgc = dgc + jnp.where(rowi == CH - 1, jnp.sum(kd, axis=0, keepdims=True) + dgl, 0.0)
            dq_ref[:, sl] = dq
            dk_ref[:, sl] = dk
            daux = daux + jnp.where(lane == hd, dbeta, 0.0) + jnp.where(lane == B_HEADS + hd, dgc, 0.0)
        daux_ref[...] = daux

    row = _rb(NCH * CH, B_W)
    outs = jax.ShapeDtypeStruct((tt, B_W), F32)
    return pl.pallas_call(
        _per_chunk(body, (CH, CH, CH, CH, 8, B_HEADS, CH, CH, CH, B_HEADS, B_HEADS, CH, CH, CH, CH, CH, CH)),
        grid=(nc // NCH,),
        in_specs=[row, row, row, _rb(NCH * CH, LANE), _CHUNK8(CH, NCH), _CHUNK4(CH, CH, NCH), row, row, row,
                  _CHUNK4(B_DH, B_DH, NCH), _CHUNK4(B_DH, B_DH, NCH), row, row],
        out_specs=(row, row, row, _rb(NCH * CH, LANE)),
        out_shape=(outs, outs, outs, jax.ShapeDtypeStruct((tt, LANE), F32)),
        compiler_params=_cp(("parallel",)), name=name)(q, k, v, aux, auxt, tinv, u0, w, u, ss, dss, du, do)


def _col_from_rowsum(m):
    hi, lo = _split(m)
    ones = jnp.ones((CH, LANE), BF16)
    return (_dot_tn(hi, ones) + _dot_tn(lo, ones))[:, 0:1]


def _gdn_post_fwd(o, proj, gn, *, name, tr=256):
    tt = o.shape[0]

    def body(o_ref, z_ref, g_ref, y_ref):
        for hd in range(B_HEADS):
            sl = slice(hd * B_DH, (hd + 1) * B_DH)
            oh = o_ref[:, sl]
            r = lax.rsqrt(jnp.mean(oh * oh, axis=-1, keepdims=True) + EPS)
            y_ref[:, sl] = (oh * r * g_ref[...] * _silu(z_ref[:, sl])).astype(BF16)

    return pl.pallas_call(body, grid=(tt // tr,), in_specs=[_rb(tr, B_W), _rb(tr, B_W, OFF_ZB // B_W), _whole((1, B_DH))],
                          out_specs=_rb(tr, B_W), out_shape=jax.ShapeDtypeStruct((tt, B_W), BF16),
                          compiler_params=_cp(("parallel",)), name=name)(o, proj, gn)


def _gdn_post_bwd(o, proj, gn, dy, *, name, tr=256):
    tt = o.shape[0]

    def body(o_ref, z_ref, g_ref, dy_ref, do_ref, dz_ref, dg_ref):
        @pl.when(pl.program_id(0) == 0)
        def _():
            dg_ref[...] = jnp.zeros_like(dg_ref)

        g = g_ref[...]
        for hd in range(B_HEADS):
            sl = slice(hd * B_DH, (hd + 1) * B_DH)
            oh, zh, dyh = o_ref[:, sl], z_ref[:, sl], dy_ref[:, sl]
            r = lax.rsqrt(jnp.mean(oh * oh, axis=-1, keepdims=True) + EPS)
            a = oh * r
            s = _silu(zh)
            da = dyh * g * s
            dg_ref[0:1, :] += _colsum(dyh * a * s)
            dz_ref[:, sl] = (dyh * a * g * _dsilu(zh)).astype(BF16)
            do_ref[:, sl] = r * (da - a * jnp.mean(da * a, axis=-1, keepdims=True))

    return pl.pallas_call(
        body, grid=(tt // tr,), in_specs=[_rb(tr, B_W), _rb(tr, B_W, OFF_ZB // B_W), _whole((1, B_DH)), _rb(tr, B_W)],
        out_specs=(_rb(tr, B_W), _rb(tr, B_W), _whole((8, B_DH))),
        out_shape=(jax.ShapeDtypeStruct((tt, B_W), F32), jax.ShapeDtypeStruct((tt, B_W), BF16),
                   jax.ShapeDtypeStruct((8, B_DH), F32)),
        compiler_params=_cp(("arbitrary",)), name=name)(o, proj, gn, dy)


def _adamw(parts, w, m, v, own=None, sel=None, *, name, tr=256):
    npart, nl, r, c = parts.shape
    tr = max([t for t in range(8, min(r, tr) + 1, 8) if r % t == 0], default=r)
    tc = c if tr < r or r <= 256 or c % 256 else 256
    c1, c2 = 1.0 - ADAM_B1 ** ADAM_STEP, 1.0 - ADAM_B2 ** ADAM_STEP

    def body(*refs):
        if own is None:
            p_ref, w_ref, m_ref, v_ref, g_ref, d_ref, mo_ref, vo_ref = refs
            part = lambda i: p_ref[i].astype(F32)
        else:
            p_ref, w_ref, m_ref, v_ref, own_ref, sel_ref, g_ref, d_ref, mo_ref, vo_ref = refs
            part = lambda i: jnp.where(sel_ref[i:i + 1, 0:1] > 0.5, own_ref[...].astype(F32), p_ref[i].astype(F32))
        g = part(0)
        for i in range(1, npart):
            g = g + part(i)
        mn = ADAM_B1 * m_ref[...] + (1.0 - ADAM_B1) * g
        vn = ADAM_B2 * v_ref[...] + (1.0 - ADAM_B2) * (g * g)
        g_ref[...] = g
        mo_ref[...] = mn
        vo_ref[...] = vn
        d_ref[...] = -ADAM_LR * ((mn / c1) / (jnp.sqrt(vn / c2) + ADAM_EPS) + ADAM_WD * w_ref[...])

    row = pl.BlockSpec((None, tr, tc), lambda l, i, j: (l, i, j))
    out = jax.ShapeDtypeStruct((nl, r, c), F32)
    ins, in_specs = [parts, w, m, v], [pl.BlockSpec((npart, None, tr, tc), lambda l, i, j: (0, l, i, j)), row, row, row]
    if own is not None:
        ins += [own, sel]
        in_specs += [row, pl.BlockSpec((N_DEV, LANE), lambda l, i, j: (0, 0))]
    return pl.pallas_call(body, grid=(nl, r // tr, c // tc), in_specs=in_specs, out_specs=(row, row, row, row),
                          out_shape=(out, out, out, out), compiler_params=_cp(("parallel", "parallel", "parallel")),
                          name=name)(*ins)


def _peer(k):
    x, y, c = lax.axis_index("x"), lax.axis_index("y"), lax.axis_index("c")
    return ((1 - x) if k & 4 else x, (1 - y) if k & 2 else y, (1 - c) if k & 1 else c)


def _my_index():
    return 4 * lax.axis_index("x") + 2 * lax.axis_index("y") + lax.axis_index("c")


def _index_of(p):
    return 4 * p[0] + 2 * p[1] + p[2]


def _all_gather(xs, *, name):
    n = len(xs)

    def body(*refs):
        x_refs, o_refs = refs[:n], refs[n:2 * n]
        send, recv, loc = refs[2 * n:]
        me = _my_index()
        copies = []
        for a in range(n):
            cp = pltpu.make_async_copy(x_refs[a], o_refs[a].at[me], loc.at[a])
            cp.start()
            copies.append(cp)
        rdmas = []
        for a in range(n):
            for k in range(1, N_DEV):
                r = pltpu.make_async_remote_copy(
                    src_ref=x_refs[a], dst_ref=o_refs[a].at[me], send_sem=send.at[a, k - 1], recv_sem=recv.at[a, k - 1],
                    device_id=_peer(k), device_id_type=pl.DeviceIdType.MESH)
                r.start()
                rdmas.append(r)
        for a in range(n):
            for k in range(1, N_DEV):
                pltpu.make_async_remote_copy(
                    src_ref=x_refs[a], dst_ref=o_refs[a].at[_index_of(_peer(k))], send_sem=send.at[a, k - 1],
                    recv_sem=recv.at[a, k - 1], device_id=_peer(k), device_id_type=pl.DeviceIdType.MESH).wait_recv()
        for r in rdmas:
            r.wait_send()
        for cp in copies:
            cp.wait()

    any_spec = pl.BlockSpec(memory_space=pl.ANY)
    return pl.pallas_call(
        body, in_specs=[any_spec] * n, out_specs=tuple([any_spec] * n),
        out_shape=tuple(jax.ShapeDtypeStruct((N_DEV,) + x.shape, x.dtype) for x in xs),
        scratch_shapes=[pltpu.SemaphoreType.DMA((n, N_DEV - 1)), pltpu.SemaphoreType.DMA((n, N_DEV - 1)),
                        pltpu.SemaphoreType.DMA((n,))],
        name=name)(*xs)


def _all_gather_two_level(xs, *, name):
    n = len(xs)

    def body(*refs):
        x_refs, o_refs = refs[:n], refs[n:2 * n]
        send, recv, loc = refs[2 * n:]
        x, y, c = lax.axis_index("x"), lax.axis_index("y"), lax.axis_index("c")
        me, sibling = (x, y, c), (x, y, 1 - c)
        chips = [(1 - x, y), (x, 1 - y), (1 - x, 1 - y)]

        def copy(a, k, block, to, src=None):
            dst = o_refs[a].at[_index_of(block)]
            return pltpu.make_async_remote_copy(src_ref=dst if src is None else src, dst_ref=dst, send_sem=send.at[a, k],
                                                recv_sem=recv.at[a, k], device_id=to, device_id_type=pl.DeviceIdType.MESH)

        mine = [pltpu.make_async_copy(x_refs[a], o_refs[a].at[_index_of(me)], loc.at[a]) for a in range(n)]
        first = [copy(a, 0, me, sibling, src=x_refs[a]) for a in range(n)]
        first += [copy(a, 1 + j, me, (*chip, c), src=x_refs[a]) for a in range(n) for j, chip in enumerate(chips)]
        for cp in mine + first:
            cp.start()
        passed = []
        for a in range(n):
            for j, chip in enumerate(chips):
                copy(a, 1 + j, (*chip, c), me).wait_recv()
                passed.append(copy(a, 4 + j, (*chip, c), sibling))
                passed[-1].start()
        for a in range(n):
            copy(a, 0, sibling, me).wait_recv()
            for j, chip in enumerate(chips):
                copy(a, 4 + j, (*chip, 1 - c), me).wait_recv()
        for cp in first + passed:
            cp.wait_send()
        for cp in mine:
            cp.wait()

    any_spec = pl.BlockSpec(memory_space=pl.ANY)
    return pl.pallas_call(
        body, in_specs=[any_spec] * n, out_specs=tuple([any_spec] * n),
        out_shape=tuple(jax.ShapeDtypeStruct((N_DEV,) + t.shape, t.dtype) for t in xs),
        scratch_shapes=[pltpu.SemaphoreType.DMA((n, N_DEV - 1)), pltpu.SemaphoreType.DMA((n, N_DEV - 1)),
                        pltpu.SemaphoreType.DMA((n,))],
        name=name)(*xs)


_HBM = pl.BlockSpec(memory_space=pltpu.HBM)
_SEM = pl.BlockSpec(memory_space=pltpu.SEMAPHORE)
_EFFECT = pltpu.SideEffectType.DATAFLOW_SIDE_EFFECTING


def _split_copy(src_ref, land_ref, send, recv, a, k, scatter, slot, sending):
    me, peer = _my_index(), _index_of(_peer(k))
    src = src_ref.at[peer if sending else me] if scatter else src_ref
    land = land_ref.at[me if sending else peer]
    if slot is not None:
        land = land.at[slot]
    sem = a * (N_DEV - 1) + k - 1
    return pltpu.make_async_remote_copy(src_ref=src, dst_ref=land, send_sem=send.at[sem], recv_sem=recv.at[sem],
                                        device_id=_peer(k), device_id_type=pl.DeviceIdType.MESH)


def _exchange_start(srcs, lands, after, *, scatter, slot=None, name):
    n = len(srcs)

    def body(*refs):
        src_refs, land_refs = refs[:n], refs[n:2 * n]
        send, recv, token = refs[2 * n + 1], refs[2 * n + 2], refs[-1]
        for a in range(n):
            for k in range(1, N_DEV):
                _split_copy(src_refs[a], land_refs[a], send, recv, a, k, scatter, slot, True).start()
        token[...] = jnp.zeros_like(token)

    hbm = lambda t: pltpu.HBM(t.shape, t.dtype)
    sems = pltpu.SemaphoreType.DMA((n * (N_DEV - 1),))
    out = pl.pallas_call(
        body, name=name,
        out_shape=(sems, sems, *[hbm(t) for t in srcs], *[hbm(t) for t in lands], jax.ShapeDtypeStruct((8, LANE), F32)),
        in_specs=[_HBM] * (2 * n) + [pl.BlockSpec(memory_space=pl.ANY)],
        out_specs=(_SEM, _SEM, *[_HBM] * (2 * n), pl.BlockSpec(memory_space=pltpu.VMEM)),
        input_output_aliases={i: 2 + i for i in range(2 * n)},
        compiler_params=pltpu.CompilerParams(has_side_effects=_EFFECT),
    )(*[pltpu.with_memory_space_constraint(t, pltpu.HBM) for t in (*srcs, *lands)], after)
    return out[0], out[1], out[2:2 + n], out[2 + n:2 + 2 * n], out[-1]


def _exchange_wait(send, recv, srcs, lands, after, *, scatter, slot=None, name):
    n = len(srcs)

    def body(*refs):
        src_refs, land_refs = refs[:n], refs[n:2 * n]
        send_ref, recv_ref = refs[2 * n], refs[2 * n + 1]
        for a in range(n):
            for k in range(1, N_DEV):
                _split_copy(src_refs[a], land_refs[a], send_ref, recv_ref, a, k, scatter, slot, True).wait_send()
                _split_copy(src_refs[a], land_refs[a], send_ref, recv_ref, a, k, scatter, slot, False).wait_recv()

    hbm = lambda t: pltpu.HBM(t.shape, t.dtype)
    out = pl.pallas_call(
        body, name=name, out_shape=(*[hbm(t) for t in srcs], *[hbm(t) for t in lands]),
        in_specs=[_HBM] * (2 * n) + [_SEM, _SEM, pl.BlockSpec(memory_space=pl.ANY)],
        out_specs=tuple([_HBM] * (2 * n)), input_output_aliases={i: i for i in range(2 * n)},
        compiler_params=pltpu.CompilerParams(has_side_effects=_EFFECT),
    )(*srcs, *lands, send, recv, after)
    return out[:n], out[n:]


def _win_to_mine(wt):
    pad = jnp.zeros((IN_PAD - IN_DIM,) + wt.shape[1:], wt.dtype)
    return jnp.concatenate([wt[3592:5640], wt[0:3584], wt[3584:3592], pad], axis=0)


def _win_from_mine(gt):
    return jnp.concatenate([gt[2048:5632], gt[5632:5640], gt[0:2048]], axis=0)


def _pad_rows(a, mult=8):
    r = (-a.shape[0]) % mult
    return a if r == 0 else jnp.concatenate([a, jnp.zeros((r,) + a.shape[1:], a.dtype)], axis=0)


def _lanes(vec, start):
    return jnp.zeros((1, LANE), F32).at[0, start:start + vec.shape[0]].set(vec)


def _small_spec(depth):
    return (("b_ada", (depth, 6 * D)), ("norm1_g", (depth, D)), ("norm2_g", (depth, D)),
            ("rel_table", (depth, A_HEADS, 2 * A_MAX_REL + 1)), ("a_log", (depth, B_HEADS)),
            ("dt_bias", (depth, B_HEADS)), ("gdn_norm_g", (depth, B_DH)), ("final_g", (D,)))


def _pack_small(d, extra, depth):
    spec = _small_spec(depth)
    rows = -(-(sum(math.prod(s) for _, s in spec) + 1) // (8 * LANE)) * 8
    flat = jnp.concatenate([d[n].reshape(-1).astype(F32) for n, _ in spec] + [extra.reshape(-1)])
    flat = jnp.concatenate([flat, jnp.zeros((rows * LANE - flat.shape[0],), F32)])
    return flat.reshape(rows, LANE)


def _unpack_small(p, depth):
    flat = p.reshape(-1)
    out, off = {}, 0
    for n, s in _small_spec(depth):
        sz = math.prod(s)
        out[n] = flat[off:off + sz].reshape(s)
        off += sz
    return out, flat[off]


def kernel(x, c, w_ada, b_ada, norm1_g, norm2_g, w_in, rel_table, w_conv, a_log, dt_bias, gdn_norm_g, w_branch_a, w_branch_b, w_out, w_ffn_in, w_ffn_out, final_g, loss_target, m_w_ada, m_b_ada, m_norm1_g, m_norm2_g, m_w_in, m_rel_table, m_w_conv, m_a_log, m_dt_bias, m_gdn_norm_g, m_w_branch_a, m_w_branch_b, m_w_out, m_w_ffn_in, m_w_ffn_out, m_final_g, v_w_ada, v_b_ada, v_norm1_g, v_norm2_g, v_w_in, v_rel_table, v_w_conv, v_a_log, v_dt_bias, v_gdn_norm_g, v_w_branch_a, v_w_branch_b, v_w_out, v_w_ffn_in, v_w_ffn_out, v_final_g):
    tt = x.shape[1]
    x0 = x[0]
    tgt = loss_target[0]
    me = _my_index()
    depth = w_in.shape[0]

    tr_ = lambda t: jnp.transpose(t, (0, 2, 1))
    shards = [tr_(w_in).astype(BF16), w_branch_a.astype(BF16), w_branch_b.astype(BF16), w_out.astype(BF16),
              tr_(w_ffn_in).astype(BF16), w_ffn_out.astype(BF16), w_conv]
    names = ("win", "wa", "wb", "wout", "wfi", "wfo", "wconv")
    early, late, every = (0, 6), (1, 2, 3, 4, 5), tuple(range(7))
    first = _all_gather_two_level([shards[i][0] for i in early] + [_pad_rows(c)], name="gather_first")
    c_all = first[-1][:, 0, :]
    is_me = lax.broadcasted_iota(jnp.int32, (N_DEV, 1, 1), 0) == me

    def unpack(idx, g):
        cols = lambda t: jnp.transpose(t, (1, 0, 2)).reshape(t.shape[1], N_DEV * t.shape[2])
        rows = lambda t: t.reshape(N_DEV * t.shape[1], t.shape[2])
        how = (lambda t: _win_to_mine(rows(t)), cols, cols, rows, rows, rows, cols)
        return {names[i]: how[i](t) for i, t in zip(idx, g)}

    def gather_start(l, idx, after, tag=""):
        srcs = [shards[i][l] for i in idx]
        lands = [lax.empty((N_DEV,) + t.shape, t.dtype) for t in srcs]
        return _exchange_start(srcs, lands, after, scatter=False, name=f"gather_start_{l}{tag}")

    def gather_wait(l, idx, pending, after, tag=""):
        send, recv, srcs, lands, _ = pending
        srcs, lands = _exchange_wait(send, recv, srcs, lands, after, scatter=False, name=f"gather_wait_{l}{tag}")
        return unpack(idx, [jnp.where(is_me, t[None], g) for g, t in zip(lands, srcs)])

    weights = [unpack(early, first[:-1])] + [None] * (depth - 1)
    pending0 = gather_start(0, late, first[-1], "_rest")
    pending = gather_start(1, every, pending0[-1]) if depth > 1 else None
    cond = c_all * (1.0 / (1.0 + jnp.exp(-c_all)))
    cond = _pad_rows(cond, 16)

    mod_cols = jnp.stack([_mm(cond, w_ada[l], name="mod_mm")[:N_DEV] for l in range(depth)])
    (g_mod,) = _all_gather([mod_cols], name="gather_mod")
    mod_all = jnp.transpose(g_mod, (1, 2, 0, 3)).reshape(depth, N_DEV, 6 * D)
    mod = lax.dynamic_index_in_dim(mod_all, me, axis=1, keepdims=False) + b_ada
    mods = mod.reshape(depth, 6, 1, D)

    n1g, n2g = norm1_g.reshape(depth, 1, D), norm2_g.reshape(depth, 1, D)
    gng = gdn_norm_g.reshape(depth, 1, B_DH)
    fg = final_g.reshape(1, D)

    saved = []
    tok = (pending if pending is not None else pending0)[-1][0, 0]
    xin, h1 = _adaln_fwd(x0, n1g[0], mods[0, 1] + tok, mods[0, 0], name="adaln1_first")
    for l in range(depth):
        sh1, sc1, gt1, sh2, sc2, gt2 = (mods[l, i] for i in range(6))
        wl = weights[l]
        proj = _mm(h1, wl["win"], tb=True, name="proj_mm", tm=2048, tn=1152)
        kpad, vpad = _kv_pad(proj, name="kv_pad")
        diag, bias_vjp = jax.vjp(_bias_diagonals, rel_table[l])
        bias = _bias_fwd(diag, name="bias_fwd")
        ya, lse = _attn_fwd(proj, kpad, vpad, bias, name="attn_fwd")
        par = jnp.concatenate([_lanes(a_log[l], B_HEADS), _lanes(dt_bias[l], B_HEADS), jnp.zeros((6, LANE), F32)], axis=0)
        qn, kn, vn, aux = _gdn_pre_fwd(proj, wl["wconv"], par, name="gdn_pre_fwd")
        auxt = _gc_rows(aux, tt // CH)
        lower = _gdn_lower(kn, aux, auxt, name="gdn_lower")
        tinv = jnp.transpose(_tri_inverse(jnp.transpose(lower, (1, 2, 0)), name="gdn_tri_inverse"), (2, 0, 1))
        u0, wg, qd, kd, qk, gle = _gdn_intra(qn, kn, vn, aux, auxt, tinv, name="gdn_intra")
        og, ss, ug = _gdn_scan_fwd(u0, wg, qd, kd, qk, gle, name="gdn_scan_fwd")
        yb = _gdn_post_fwd(og, proj, gng[l], name="gdn_post_fwd")
        if l == 0:
            wl.update(gather_wait(0, late, pending0, yb, "_rest"))
        pa, pb, merged = _branch_merge(ya, yb, wl["wa"], wl["wb"], proj, name="branch_merge")
        t1, x2, h2 = _out_adaln(merged, wl["wout"], xin, gt1, n2g[l], sc2, sh2, name="out_adaln2")
        gu, act = _ffn_in_swiglu(h2, wl["wfi"], name="ffn_in_swiglu")
        saved.append(dict(xin=xin, h1=h1, proj=proj, kpad=kpad, vpad=vpad, bias=bias, bias_vjp=bias_vjp, ya=ya, lse=lse,
                          par=par, qn=qn, kn=kn, vn=vn, aux=aux, auxt=auxt, tinv=tinv, ss=ss, og=og, yb=yb, pa=pa, pb=pb,
                          u0=u0, wg=wg, qd=qd, kd=kd, qk=qk, gle=gle, ug=ug,
                          merged=merged, t1=t1, x2=x2, h2=h2, gu=gu, act=act))
        if l + 1 < depth:
            weights[l + 1] = gather_wait(l + 1, every, pending, act)
            pending = gather_start(l + 2, every, weights[l + 1]["wconv"]) if l + 2 < depth else None
            tok = pending[-1][0, 0] if pending is not None else 0.0
            t2, xin, h1 = _out_adaln(act, wl["wfo"], x2, gt2, n1g[l + 1], mods[l + 1, 1] + tok, mods[l + 1, 0],
                                     tk=FTN, name="ffn_out_adaln1")
        else:
            t2 = _mm(act, wl["wfo"], name="ffn_out_mm", tk=FTN)
        saved[-1]["t2"] = t2

    s = saved[-1]
    dx, dt2, st = _loss_head(s["x2"], s["t2"], mods[depth - 1, 5], fg, tgt, name="loss_head")
    loss_part = st[4, 0]
    small_g = {"final_g": st[0]}
    dmod_rows = [None] * depth
    for n in ("norm1_g", "norm2_g", "rel_table", "a_log", "dt_bias", "gdn_norm_g"):
        small_g[n] = [None] * depth
    dgt2 = st[3]
    cols_slabs = lambda g: jnp.transpose(g.reshape(g.shape[0], N_DEV, g.shape[1] // N_DEV), (1, 0, 2))
    rows_slabs = lambda g: g.reshape(N_DEV, g.shape[0] // N_DEV, g.shape[1])
    mix, ffn = (0, 1, 2, 3, 6), (4, 5)
    lands = {kind: [lax.empty((N_DEV,) + shards[i].shape, shards[i].dtype) for i in idx]
             for kind, idx in (("mix", mix), ("ffn", ffn))}
    own = {kind: [None] * depth for kind in lands}
    pending_s = {kind: None for kind in lands}

    def scatter(kind, l, srcs, after):
        if pending_s[kind] is not None:
            done, lands[kind] = _exchange_wait(*pending_s[kind][:4], after, scatter=True, slot=l + 1,
                                               name=f"scatter_wait_{kind}_{l + 1}")
            own[kind][l + 1] = [lax.dynamic_index_in_dim(t, me, 0, keepdims=False) for t in done]
        pending_s[kind] = _exchange_start(srcs, lands[kind], after, scatter=True, slot=l, name=f"scatter_start_{kind}_{l}")
        return pending_s[kind][-1][0, 0]

    for l in reversed(range(depth)):
        s, wl = saved[l], weights[l]
        sh1, sc1, gt1, sh2, sc2, gt2 = (mods[l, i] for i in range(6))
        gw_fo = _mm(s["act"], dt2, ta=True, out_dtype=BF16, name="ffn_out_dw", tm=1408, tk=2048)
        dgu = _ffn_out_bwd_swiglu(dt2, wl["wfo"], s["gu"], name="ffn_out_bwd_swiglu")
        gw_fi = _mm(dgu, s["h2"], ta=True, out_dtype=BF16, name="ffn_in_dw", tm=1408, tk=2048)
        sc2 = sc2 + scatter("ffn", l, [rows_slabs(gw_fi), rows_slabs(gw_fo)], gw_fi)
        dx, dt1, st2 = _mm_adaln_bwd(dgu, wl["wfi"], s["x2"], n2g[l], sc2, sh2, dx, s["t1"], gt1, tk=FTN,
                                     name="ffn_in_dx_adaln2")
        gw_out = _mm(s["merged"], dt1, ta=True, out_dtype=BF16, name="out_dw")
        dgates, dpa, dpb = _out_bwd_merge(dt1, wl["wout"], s["proj"], s["pa"], s["pb"], name="out_bwd_merge")
        gw_a = _mm(s["ya"], dpa, ta=True, out_dtype=BF16, name="branch_a_dw")
        gw_b = _mm(s["yb"], dpb, ta=True, out_dtype=BF16, name="branch_b_dw")
        dya = _mm(dpa, wl["wa"], tb=True, name="branch_a_dx")
        dyb = _mm(dpb, wl["wb"], tb=True, name="branch_b_dx")
        dqa, dka, dva, dbias = _attn_bwd(s["proj"], s["kpad"], s["vpad"], s["bias"], s["ya"], s["lse"], dya,
                                             name="attn_bwd")
        ddiag = jnp.roll(_bias_bwd(dbias, name="bias_bwd"), -(QB - 1), axis=1)
        small_g["rel_table"][l] = s["bias_vjp"](ddiag)[0]
        dog, dz, dgn = _gdn_post_bwd(s["og"], s["proj"], gng[l], dyb, name="gdn_post_bwd")
        small_g["gdn_norm_g"][l] = dgn[0]
        dug, dss = _gdn_scan_bwd(s["wg"], s["qd"], s["kd"], s["qk"], s["gle"], dog, name="gdn_scan_bwd")
        dqn, dkn, dvn, daux = _gdn_bwd(s["qn"], s["kn"], s["vn"], s["aux"], s["auxt"], s["tinv"], s["u0"], s["wg"],
                                       s["ug"], s["ss"], dss, dug, dog, name="gdn_bwd")
        dqkv, dba, dwc, dpar = _gdn_pre_bwd(s["proj"], wl["wconv"], s["par"], dqn, dkn, dvn, daux, name="gdn_pre_bwd")
        small_g["a_log"][l] = dpar[0, B_HEADS:2 * B_HEADS]
        small_g["dt_bias"][l] = dpar[1, B_HEADS:2 * B_HEADS]
        dproj = jnp.concatenate([dgates, dqa, dka, dva, dqkv, dz, dba], axis=1)
        gw_in = _mm(dproj, s["h1"], ta=True, out_dtype=BF16, name="proj_dw", tm=1152, tk=2048)
        mix_srcs = [rows_slabs(_win_from_mine(gw_in)), cols_slabs(gw_a), cols_slabs(gw_b), rows_slabs(gw_out),
                    cols_slabs(dwc[0:CONV_K])]
        if l > 0:
            sc1 = sc1 + scatter("mix", l, mix_srcs, gw_in)
        if l > 0:
            p = saved[l - 1]
            dx, dt2, st1 = _mm_adaln_bwd(dproj, wl["win"], s["xin"], n1g[l], sc1, sh1, dx, p["t2"], mods[l - 1, 5],
                                         tk=1152, name="proj_dx_adaln1")
        else:
            dx, st1 = _mm_adaln_bwd(dproj, wl["win"], s["xin"], n1g[l], sc1, sh1, dx, tk=1152,
                                    name="proj_dx_adaln1_first")
        small_g["norm1_g"][l], small_g["norm2_g"][l] = st1[0], st2[0]
        dmod_rows[l] = jnp.concatenate([st1[2], st1[1], st2[3], st2[2], st2[1], dgt2])
        if l > 0:
            dgt2 = st1[3]
    grad_x = dx[None]

    small_local = {n: (jnp.stack(vs) if isinstance(vs, list) else vs) for n, vs in small_g.items()}
    small_local["b_ada"] = jnp.stack(dmod_rows)
    (g_small,) = _all_gather([_pack_small(small_local, loss_part, depth)], name="gather_small")
    tok = scatter("mix", 0, mix_srcs, g_small)
    wsm = _pack_small(dict(b_ada=b_ada, norm1_g=norm1_g, norm2_g=norm2_g, rel_table=rel_table, a_log=a_log,
                           dt_bias=dt_bias, gdn_norm_g=gdn_norm_g, final_g=final_g), jnp.zeros((1,), F32) + tok, depth)
    msm = _pack_small(dict(b_ada=m_b_ada, norm1_g=m_norm1_g, norm2_g=m_norm2_g, rel_table=m_rel_table, a_log=m_a_log,
                           dt_bias=m_dt_bias, gdn_norm_g=m_gdn_norm_g, final_g=m_final_g), jnp.zeros((1,), F32), depth)
    vsm = _pack_small(dict(b_ada=v_b_ada, norm1_g=v_norm1_g, norm2_g=v_norm2_g, rel_table=v_rel_table, a_log=v_a_log,
                           dt_bias=v_dt_bias, gdn_norm_g=v_gdn_norm_g, final_g=v_final_g), jnp.ones((1,), F32), depth)
    sm = [_unpack_small(t, depth) for t in _adamw(g_small[:, None], wsm[None], msm[None], vsm[None], name="adamw_small")]
    loss = sm[0][1]

    dmod_all = g_small.reshape(N_DEV, -1)[:, :depth * 6 * D].reshape(N_DEV, depth, 6 * D)
    dmod_mine = lax.dynamic_slice_in_dim(dmod_all, me * (6 * D // N_DEV), 6 * D // N_DEV, axis=2)
    g_ada = jnp.stack([_mm(cond + tok, _pad_rows(dmod_mine[:, l], 16), ta=True, name="ada_dw") for l in range(depth)])

    got, mine = {}, {}
    sel = jnp.broadcast_to(jnp.where(is_me[:, :, 0], 1.0, 0.0), (N_DEV, LANE)).astype(F32) + tok

    def finish(kind, idx, after):
        done, lands[kind] = _exchange_wait(*pending_s[kind][:4], after, scatter=True, slot=0, name=f"scatter_wait_{kind}_0")
        own[kind][0] = [lax.dynamic_index_in_dim(t, me, 0, keepdims=False) for t in done]
        for a, i in enumerate(idx):
            got[i] = lands[kind][a]
            mine[i] = jnp.stack([own[kind][l][a] for l in range(depth)])

    def upd(i, w, m, v, name):
        if i in (0, 4):
            return [tr_(t) for t in _adamw(got[i], tr_(w), tr_(m), tr_(v), mine[i], sel, name=name)]
        return _adamw(got[i], w, m, v, mine[i], sel, name=name)

    finish("ffn", ffn, g_ada)
    res = {
        "w_ada": _adamw(g_ada[None], w_ada, m_w_ada, v_w_ada, name="adamw_w_ada"),
        "w_ffn_in": upd(4, w_ffn_in, m_w_ffn_in, v_w_ffn_in, "adamw_w_ffn_in"),
        "w_ffn_out": upd(5, w_ffn_out, m_w_ffn_out, v_w_ffn_out, "adamw_w_ffn_out"),
    }
    done_first = (res["w_ada"][1][0, 0, 0] + res["w_ffn_in"][1][0, 0, 0] + res["w_ffn_out"][1][0, 0, 0] + sm[1][1])
    finish("mix", mix, jnp.zeros((8, LANE), F32) + done_first)
    res.update({
        "w_in": upd(0, w_in, m_w_in, v_w_in, "adamw_w_in"),
        "w_conv": upd(6, w_conv, m_w_conv, v_w_conv, "adamw_w_conv"),
        "w_branch_a": upd(1, w_branch_a, m_w_branch_a, v_w_branch_a, "adamw_w_branch_a"),
        "w_branch_b": upd(2, w_branch_b, m_w_branch_b, v_w_branch_b, "adamw_w_branch_b"),
        "w_out": upd(3, w_out, m_w_out, v_w_out, "adamw_w_out"),
    })
    for n, _ in _small_spec(depth):
        res[n] = [sm[i][0][n] for i in range(4)]
    order = ("w_ada", "b_ada", "norm1_g", "norm2_g", "w_in", "rel_table", "w_conv", "a_log", "dt_bias", "gdn_norm_g",
             "w_branch_a", "w_branch_b", "w_out", "w_ffn_in", "w_ffn_out", "final_g")
    return (loss, grad_x, *[res[n][0] for n in order], *[res[n][1] for n in order],
            *[res[n][2] for n in order], *[res[n][3] for n in order])
```
